```python
import math
import jax
import jax.numpy as jnp
from jax import lax
import numpy as np

D_MODEL = 1024
BATCH = 32
SEQ = 256
DEPTH = 2
DEC_BATCH = 4
DEC_SEQ = 1024
PAST_LEN = 256

GRID_W = 64
N_EVEN = (DEPTH + 1) // 2
N_ODD = DEPTH // 2
GLA_HEADS = 4
GLA_DK = 64
GLA_DV = 128
GLA_GATE_RANK = 16
GLA_GATE_NORM = 16.0
HGRN_HEADS = 4
HGRN_DK = 64
HGRN_DV = 128
CHUNK = 16
GLA_QK = GLA_HEADS * GLA_DK
GLA_VW = GLA_HEADS * GLA_DV
HGRN_QK = HGRN_HEADS * HGRN_DK
HGRN_VW = HGRN_HEADS * HGRN_DV
AB_SIZES = (GLA_QK, GLA_QK, GLA_VW, GLA_VW, GLA_GATE_RANK, GLA_GATE_RANK,
            HGRN_QK, HGRN_QK, HGRN_QK, HGRN_VW, HGRN_VW)
AB_IN = 2 * GLA_QK + 2 * GLA_VW + 2 * GLA_GATE_RANK + 3 * HGRN_QK + 2 * HGRN_VW
AB_MIX = GLA_VW + HGRN_VW
DIFF_HEADS = 8
DIFF_HD = 64
DIFF_QK = DIFF_HEADS * 2 * DIFF_HD
DIFF_VW = DIFF_HEADS * 2 * DIFF_HD
Q_BLOCK = 128
ROPE_THETA = 10000.0
ROPE_AXIS_DIM = DIFF_HD // 2
N_GROUPS = 4
EXPERTS_PER_GROUP = 8
N_EXPERTS = N_GROUPS * EXPERTS_PER_GROUP
TOP_K = 2
MOE_HIDDEN = 512
EPS = 1e-6
F32 = jnp.float32

kernel_name = 'hybrid_diffusion_gla_hgrn2_diffattn_hmoe_step'


def rmsnorm(x, w):
    xf = x.astype(F32)
    y = xf * lax.rsqrt(jnp.mean(xf * xf, axis=-1, keepdims=True) + EPS)
    return (y * w.astype(F32)).astype(x.dtype)


def adaln(cond, w, b):
    m = jax.nn.silu(cond) @ w + b
    return [t[:, None, :] for t in jnp.split(m, 6, axis=-1)]


def modulate(x, g, shift, scale):
    return rmsnorm(x, g) * (1 + scale) + shift


def chunk_gla(q, k, v, log_a, s0):
    B, H, L, dk = q.shape
    dv = v.shape[-1]
    n = L // CHUNK
    qc = q.astype(F32).reshape(B, H, n, CHUNK, dk)
    kc = k.astype(F32).reshape(B, H, n, CHUNK, dk)
    vc = v.astype(F32).reshape(B, H, n, CHUNK, dv)
    b = jnp.cumsum(log_a.astype(F32).reshape(B, H, n, CHUNK, dk), axis=3)
    causal = jnp.tril(jnp.ones((CHUNK, CHUNK), dtype=bool))[:, :, None]
    rel = jnp.where(causal, b[..., :, None, :] - b[..., None, :, :], -jnp.inf)
    scores = jnp.einsum('bhnik,bhnjk,bhnijk->bhnij', qc, kc, jnp.exp(rel))
    o_intra = jnp.einsum('bhnij,bhnjv->bhniv', scores, vc)
    b_last = b[..., -1:, :]
    q_dec = qc * jnp.exp(b)
    k_dec = kc * jnp.exp(b_last - b)
    a_chunk = jnp.exp(b_last[..., 0, :])

    def step(S, xs):
        qd, kd, vv, ad = xs
        o = jnp.einsum('bhik,bhkv->bhiv', qd, S)
        S = ad[..., None] * S + jnp.einsum('bhjk,bhjv->bhkv', kd, vv)
        return S, o

    xs = (jnp.moveaxis(q_dec, 2, 0), jnp.moveaxis(k_dec, 2, 0), jnp.moveaxis(vc, 2, 0), jnp.moveaxis(a_chunk, 2, 0))
    s_fin, o_inter = lax.scan(step, s0.astype(F32), xs)
    o = o_intra + jnp.moveaxis(o_inter, 0, 2)
    return o.reshape(B, H, L, dv), s_fin


def bidirectional_scan(q, k_f, k_b, v, la_f, la_b, s0_f, s0_b):
    o_f, s_f = chunk_gla(q, k_f, v, la_f, s0_f)
    flip = lambda t: jnp.flip(t, axis=2)
    o_b, s_b = chunk_gla(flip(q), flip(k_b), flip(v), flip(la_b), s0_b)
    return o_f + flip(o_b), s_f, s_b


def mixer_ab(h, w_in, a2, a_bias, lb, onorm_gla, onorm_hg, w_out, s_gla, s_hg):
    B, L, _ = h.shape
    split_idx = [int(i) for i in np.cumsum(AB_SIZES)[:-1]]
    gq, gk, gv, gg, ga_f, ga_b, hq, hf_f, hf_b, hi, hgate = jnp.split(h @ w_in, split_idx, axis=-1)

    def heads(t, n):
        return t.reshape(B, L, n, -1).transpose(0, 2, 1, 3)

    la_f = jax.nn.log_sigmoid((ga_f @ a2[0] + a_bias[0]).astype(F32)) / GLA_GATE_NORM
    la_b = jax.nn.log_sigmoid((ga_b @ a2[1] + a_bias[1]).astype(F32)) / GLA_GATE_NORM
    gk_h = heads(gk, GLA_HEADS)
    o_gla, g_f, g_b = bidirectional_scan(
        heads(gq, GLA_HEADS) * GLA_DK ** -0.5, gk_h, gk_h, heads(gv, GLA_HEADS),
        heads(la_f, GLA_HEADS), heads(la_b, GLA_HEADS), s_gla[:, 0], s_gla[:, 1])
    f_f = lb[0] + (1 - lb[0]) * jax.nn.sigmoid(hf_f.astype(F32))
    f_b = lb[1] + (1 - lb[1]) * jax.nn.sigmoid(hf_b.astype(F32))
    o_hg, h_f, h_b = bidirectional_scan(
        heads(jax.nn.silu(hq), HGRN_HEADS) * HGRN_DK ** -0.5,
        heads(1 - f_f, HGRN_HEADS), heads(1 - f_b, HGRN_HEADS), heads(hi, HGRN_HEADS),
        heads(jnp.log(f_f), HGRN_HEADS), heads(jnp.log(f_b), HGRN_HEADS), s_hg[:, 0], s_hg[:, 1])

    def gated_norm(o, gate, w):
        o = o.transpose(0, 2, 1, 3)
        y = rmsnorm(o, w) * jax.nn.silu(gate.astype(F32)).reshape(o.shape)
        return y.reshape(B, L, -1)

    mixed = jnp.concatenate([gated_norm(o_gla, gg, onorm_gla), gated_norm(o_hg, hgate, onorm_hg)], axis=-1)
    out = mixed.astype(h.dtype) @ w_out
    return out, jnp.stack([g_f, g_b], axis=1), jnp.stack([h_f, h_b], axis=1)


def _rotate(x, pos):
    half = x.shape[-1] // 2
    inv = ROPE_THETA ** (-jnp.arange(half, dtype=F32) / half)
    ang = pos[:, None] * inv[None, :]
    cos = jnp.cos(ang)[None, :, None, None, :]
    sin = jnp.sin(ang)[None, :, None, None, :]
    x1, x2 = x[..., :half], x[..., half:]
    return jnp.concatenate([x1 * cos - x2 * sin, x2 * cos + x1 * sin], axis=-1)


def axial_rope(x):
    n_tok = x.shape[1]
    rows = n_tok // GRID_W
    row = jnp.repeat(jnp.arange(rows, dtype=F32), GRID_W)
    col = jnp.tile(jnp.arange(GRID_W, dtype=F32), rows)
    xf = x.astype(F32)
    y = jnp.concatenate([_rotate(xf[..., :ROPE_AXIS_DIM], row), _rotate(xf[..., ROPE_AXIS_DIM:], col)], axis=-1)
    return y.astype(x.dtype)


def diff_qkv(h, w_in):
    B, L, _ = h.shape
    q, k, v = jnp.split(h @ w_in, [DIFF_QK, 2 * DIFF_QK], axis=-1)
    return (q.reshape(B, L, DIFF_HEADS, 2, DIFF_HD), k.reshape(B, L, DIFF_HEADS, 2, DIFF_HD),
            v.reshape(B, L, DIFF_HEADS, 2 * DIFF_HD))


def diff_attend(q, k, v, lam, lam_init, subln_w, w_out):
    B, Lq = q.shape[0], q.shape[1]
    nb = Lq // Q_BLOCK
    qb = q.reshape(B, nb, Q_BLOCK, DIFF_HEADS, 2, DIFF_HD).transpose(1, 0, 2, 3, 4, 5)
    kf = k.astype(F32)
    vf = v.astype(F32)
    scale = DIFF_HD ** -0.5

    def block(qblk):
        s = jnp.einsum('bqhcd,bkhcd->bhcqk', qblk.astype(F32), kf) * scale
        p = jax.nn.softmax(s, axis=-1)
        a = p[:, :, 0] - lam * p[:, :, 1]
        return jnp.einsum('bhqk,bkhe->bqhe', a, vf)

    o = lax.map(block, qb)
    o = o.transpose(1, 0, 2, 3, 4).reshape(B, Lq, DIFF_HEADS, 2 * DIFF_HD)
    o = rmsnorm(o, subln_w) * (1 - lam_init)
    return o.reshape(B, Lq, DIFF_VW).astype(q.dtype) @ w_out


def hier_moe(h, w_rg, w_re, w1, w3, w2):
    B, L, D = h.shape
    t = h.reshape(B * L, D)
    p_group = jax.nn.softmax((t @ w_rg).astype(F32), axis=-1)
    g_val, g_idx = lax.top_k(p_group, 1)
    logit_e = (t @ w_re).astype(F32).reshape(B * L, N_GROUPS, EXPERTS_PER_GROUP)
    logit_in = jnp.einsum('tge,tg->te', logit_e, jax.nn.one_hot(g_idx[:, 0], N_GROUPS, dtype=F32))
    e_val, e_idx = lax.top_k(jax.nn.softmax(logit_in, axis=-1), TOP_K)
    e_val = e_val / jnp.sum(e_val, axis=-1, keepdims=True)
    weights = g_val * e_val
    ids = g_idx * EXPERTS_PER_GROUP + e_idx
    combine = jnp.sum(jax.nn.one_hot(ids, N_EXPERTS, dtype=F32) * weights[..., None], axis=1)

    def expert(acc, xs):
        a, b_, c_, wt = xs
        y = (jax.nn.silu(t @ a) * (t @ b_)) @ c_
        return acc + wt[:, None].astype(y.dtype) * y, None

    out, _ = lax.scan(expert, jnp.zeros_like(t), (w1, w3, w2, combine.T))
    return out.reshape(B, L, D)


def setup_inputs(seed: int = 0) -> dict:
    key = jax.random.key(seed)
    ks = iter(jax.random.split(key, 40))
    nrm = lambda shape, s=1.0: jax.random.normal(next(ks), shape, F32) * s
    D = D_MODEL
    return {
        'x_prompt': nrm((BATCH, SEQ, D)),
        'x_sample': nrm((DEC_BATCH, DEC_SEQ, D)),
        'state_gla': nrm((DEC_BATCH, N_EVEN, 2, GLA_HEADS, GLA_DK, GLA_DV), 0.3),
        'state_hgrn': nrm((DEC_BATCH, N_EVEN, 2, HGRN_HEADS, HGRN_DK, HGRN_DV), 0.3),
        'cache_diff_k': nrm((DEC_BATCH, N_ODD, PAST_LEN, DIFF_HEADS, 2, DIFF_HD)),
        'cache_diff_v': nrm((DEC_BATCH, N_ODD, PAST_LEN, DIFF_HEADS, 2 * DIFF_HD)),
        'c': nrm((DEC_BATCH, D)),
        'c_ctx': nrm((D,)),
        'w_ada': nrm((DEPTH, D, 6 * D), 0.5 * D ** -0.5),
        'b_ada': nrm((DEPTH, 6 * D), 0.02),
        'norm1_w': 1.0 + nrm((DEPTH, D), 0.02),
        'norm2_w': 1.0 + nrm((DEPTH, D), 0.02),
        'w_in_ab': nrm((N_EVEN, D, AB_IN), D ** -0.5),
        'gla_a2': nrm((N_EVEN, 2, GLA_GATE_RANK, GLA_QK), GLA_GATE_RANK ** -0.5),
        'gla_a_bias': nrm((N_EVEN, 2, GLA_QK), 0.02),
        'hgrn_lb': 1.0 + nrm((DEPTH, 2, HGRN_QK), 0.1),
        'gla_onorm_w': 1.0 + nrm((N_EVEN, GLA_DV), 0.02),
        'hgrn_onorm_w': 1.0 + nrm((N_EVEN, HGRN_DV), 0.02),
        'w_out_ab': nrm((N_EVEN, AB_MIX, D), AB_MIX ** -0.5),
        'w_in_c': nrm((N_ODD, D, 2 * DIFF_QK + DIFF_VW), D ** -0.5),
        'lam_q1': nrm((N_ODD, DIFF_HD), 0.1),
        'lam_k1': nrm((N_ODD, DIFF_HD), 0.1),
        'lam_q2': nrm((N_ODD, DIFF_HD), 0.1),
        'lam_k2': nrm((N_ODD, DIFF_HD), 0.1),
        'diff_subln_w': 1.0 + nrm((N_ODD, 2 * DIFF_HD), 0.02),
        'w_out_c': nrm((N_ODD, DIFF_VW, D), DIFF_VW ** -0.5),
        'router_group': nrm((DEPTH, D, N_GROUPS), D ** -0.5),
        'router_expert': nrm((DEPTH, D, N_EXPERTS), D ** -0.5),
        'moe_w1': nrm((DEPTH, N_EXPERTS, D, MOE_HIDDEN), D ** -0.5),
        'moe_w3': nrm((DEPTH, N_EXPERTS, D, MOE_HIDDEN), D ** -0.5),
        'moe_w2': nrm((DEPTH, N_EXPERTS, MOE_HIDDEN, D), MOE_HIDDEN ** -0.5),
        'final_norm_w': 1.0 + nrm((D,), 0.02),
    }


def reference(x_prompt, x_sample, state_gla, state_hgrn, cache_diff_k, cache_diff_v, c, c_ctx,
              w_ada, b_ada, norm1_w, norm2_w, w_in_ab, gla_a2, gla_a_bias, hgrn_lb, gla_onorm_w,
              hgrn_onorm_w, w_out_ab, w_in_c, lam_q1, lam_k1, lam_q2, lam_k2, diff_subln_w, w_out_c,
              router_group, router_expert, moe_w1, moe_w3, moe_w2, final_norm_w):
    lb_all = jnp.cumsum(jax.nn.softmax(hgrn_lb.astype(F32), axis=0), axis=0)
    xp, xs = x_prompt, x_sample
    new_gla, new_hg, new_k, new_v = [], [], [], []
    for l in range(DEPTH):
        sh1_p, sc1_p, ga1_p, sh2_p, sc2_p, ga2_p = adaln(c_ctx[None, :], w_ada[l], b_ada[l])
        sh1_s, sc1_s, ga1_s, sh2_s, sc2_s, ga2_s = adaln(c, w_ada[l], b_ada[l])
        hp = modulate(xp, norm1_w[l], sh1_p, sc1_p)
        hs = modulate(xs, norm1_w[l], sh1_s, sc1_s)
        if l % 2 == 0:
            e = l // 2
            zg = jnp.zeros((xp.shape[0], 2, GLA_HEADS, GLA_DK, GLA_DV), F32)
            zh = jnp.zeros((xp.shape[0], 2, HGRN_HEADS, HGRN_DK, HGRN_DV), F32)
            out_p, sg, shg = mixer_ab(hp, w_in_ab[e], gla_a2[e], gla_a_bias[e], lb_all[l],
                                      gla_onorm_w[e], hgrn_onorm_w[e], w_out_ab[e], zg, zh)
            new_gla.append(sg)
            new_hg.append(shg)
            out_s, _, _ = mixer_ab(hs, w_in_ab[e], gla_a2[e], gla_a_bias[e], lb_all[l],
                                   gla_onorm_w[e], hgrn_onorm_w[e], w_out_ab[e],
                                   state_gla[:, e], state_hgrn[:, e])
        else:
            o = l // 2
            lam_init = 0.8 - 0.6 * math.exp(-0.3 * l)
            lam = (jnp.exp(jnp.sum((lam_q1[o] * lam_k1[o]).astype(F32)))
                   - jnp.exp(jnp.sum((lam_q2[o] * lam_k2[o]).astype(F32))) + lam_init)
            qp, kp, vp = diff_qkv(hp, w_in_c[o])
            out_p = diff_attend(qp, kp, vp, lam, lam_init, diff_subln_w[o], w_out_c[o])
            new_k.append(kp)
            new_v.append(vp)
            qs, ks_, vs = diff_qkv(hs, w_in_c[o])
            k_all = jnp.concatenate([cache_diff_k[:, o].astype(ks_.dtype), axial_rope(ks_)], axis=1)
            v_all = jnp.concatenate([cache_diff_v[:, o].astype(vs.dtype), vs], axis=1)
            out_s = diff_attend(axial_rope(qs), k_all, v_all, lam, lam_init, diff_subln_w[o], w_out_c[o])
        xp = xp + ga1_p * out_p
        xs = xs + ga1_s * out_s
        xp = xp + ga2_p * hier_moe(modulate(xp, norm2_w[l], sh2_p, sc2_p), router_group[l],
                                   router_expert[l], moe_w1[l], moe_w3[l], moe_w2[l])
        xs = xs + ga2_s * hier_moe(modulate(xs, norm2_w[l], sh2_s, sc2_s), router_group[l],
                                   router_expert[l], moe_w1[l], moe_w3[l], moe_w2[l])
    y_prompt = rmsnorm(xp, final_norm_w)
    y_sample = rmsnorm(xs, final_norm_w)
    new_state_gla = jnp.stack(new_gla, axis=1)
    new_state_hgrn = jnp.stack(new_hg, axis=1)
    new_cache_diff_k = jnp.stack(new_k, axis=1)
    new_cache_diff_v = jnp.stack(new_v, axis=1)
    return (y_prompt, y_sample, new_state_gla, new_state_hgrn, new_cache_diff_k, new_cache_diff_v)
```

```python
import functools
import math

import jax
import jax.numpy as jnp
import numpy as np
from jax import lax
from jax.experimental import pallas as pl
from jax.experimental.pallas import tpu as pltpu

F32 = jnp.float32
BF16 = jnp.bfloat16
I32 = jnp.int32

D_MODEL = 1024
GLA_HEADS = 4
HGRN_HEADS = 4
SCAN_HEADS = GLA_HEADS + HGRN_HEADS
HEAD_DK = 64
HEAD_DV = 128
GATE_RANK = 16
GLA_GATE_NORM = 16.0
DIFF_HEADS = 8
DIFF_HD = 64
GRID_W = 64
ROPE_THETA = 10000.0
N_GROUPS = 4
EXPERTS_PER_GROUP = 8
N_EXPERTS = N_GROUPS * EXPERTS_PER_GROUP
MOE_HIDDEN = 512
EPS = 1e-6
LANES = 128
NEG_BIG = -1e30

ROW_TILE = 256
SCAN_CHUNK = 64
EXPERT_TILE = 256
DISPATCH_TILE = 512
VMEM_LIMIT = 56 * 1024 * 1024

_C_GQ, _C_GK, _C_GV, _C_GG = 0, 256, 512, 1024
_C_HQ, _C_HFF, _C_HFB, _C_HI, _C_HG = 1536, 1792, 2048, 2304, 2816
_C_GAF, _C_GAB = 3328, 3344
AB_COLS = 3456


def _params(n_axes, vmem=VMEM_LIMIT):
    return pltpu.CompilerParams(dimension_semantics=("arbitrary",) * n_axes,
                                vmem_limit_bytes=vmem)


def _dot(a, b):
    return jnp.dot(a, b, preferred_element_type=F32)


def _dot_nt(a, b):
    return lax.dot_general(a, b, (((1,), (1,)), ((), ())), preferred_element_type=F32)


def _dot_tn(a, b):
    return lax.dot_general(a, b, (((0,), (0,)), ((), ())), preferred_element_type=F32)


def _split_bf16(x):
    hi = x.astype(BF16)
    lo = (x - hi.astype(F32)).astype(BF16)
    return hi, lo


def _silu(x):
    return x * jax.nn.sigmoid(x)


def _log_sigmoid(x):
    return jnp.minimum(x, 0.0) - jnp.log(1.0 + jnp.exp(-jnp.abs(x)))


def _rms(x):
    return x * lax.rsqrt(jnp.mean(x * x, axis=-1, keepdims=True) + EPS)


def _modulate(x, norm_w, shift, scale):
    return (_rms(x) * norm_w) * (1.0 + scale) + shift


def _ada_kernel(c_ref, w_ref, b_ref, o_ref):
    s = _silu(c_ref[...])
    o_ref[0] = _dot(s.astype(BF16), w_ref[0].astype(BF16)) + b_ref[0]


def _adaln(cond8, w_ada, b_ada):
    depth, d, n = w_ada.shape
    tn = 1536
    return pl.pallas_call(
        _ada_kernel,
        grid=(depth, n // tn),
        in_specs=[pl.BlockSpec((8, d), lambda l, j: (0, 0)),
                  pl.BlockSpec((1, d, tn), lambda l, j: (l, 0, j)),
                  pl.BlockSpec((1, 1, tn), lambda l, j: (l, 0, j))],
        out_specs=pl.BlockSpec((1, 8, tn), lambda l, j: (l, 0, j)),
        out_shape=jax.ShapeDtypeStruct((depth, 8, n), F32),
        compiler_params=_params(2),
        name="adaln",
    )(cond8, w_ada, b_ada.reshape(depth, 1, n))


def _mod_row(i, layer, n_prompt_tiles, tiles_per_sample):
    r = jnp.where(i < n_prompt_tiles, 0, 1 + (i - n_prompt_tiles) // tiles_per_sample)
    return layer * 8 + r


def _inproj0_kernel(xp_ref, xs_ref, mod_ref, nw_ref, w_ref, o_ref, *, n_prompt_tiles):
    i = pl.program_id(0)
    x = jnp.where(i < n_prompt_tiles, xp_ref[...], xs_ref[...])
    h = _modulate(x, nw_ref[...], mod_ref[0, 0:1, :], mod_ref[0, 1:2, :])
    o_ref[...] = _dot(h.astype(BF16), w_ref[...])


def _inproj0(xp, xs, mods, norm_w, w_bf16, tiles_per_sample):
    tp, d = xp.shape
    ts = xs.shape[0]
    n = w_bf16.shape[1]
    npt, nst = tp // ROW_TILE, ts // ROW_TILE
    mod_map = lambda i: (_mod_row(i, 0, npt, tiles_per_sample), 0, 0)
    return pl.pallas_call(
        functools.partial(_inproj0_kernel, n_prompt_tiles=npt),
        grid=(npt + nst,),
        in_specs=[pl.BlockSpec((ROW_TILE, d), lambda i: (jnp.minimum(i, npt - 1), 0)),
                  pl.BlockSpec((ROW_TILE, d), lambda i: (jnp.maximum(i - npt, 0), 0)),
                  pl.BlockSpec((1, 6, d), mod_map),
                  pl.BlockSpec((1, d), lambda i: (0, 0)),
                  pl.BlockSpec((d, n), lambda i: (0, 0))],
        out_specs=pl.BlockSpec((ROW_TILE, n), lambda i: (i, 0)),
        out_shape=jax.ShapeDtypeStruct((tp + ts, n), F32),
        compiler_params=_params(1),
        name="inproj0",
    )(xp, xs, mods, norm_w, w_bf16)


def _scan_kernel(*refs, seq_len, has_state):
    if has_state:
        (p_ref, a2_ref, ab_ref, lb_ref, ong_ref, onh_ref, s0_ref,
         mixed_ref, qf, kf, qb, kb, vv, dec_f, dec_b, o_scr, st_f, st_b) = refs
        sfin_ref = None
    else:
        (p_ref, a2_ref, ab_ref, lb_ref, ong_ref, onh_ref,
         mixed_ref, sfin_ref, qf, kf, qb, kb, vv, dec_f, dec_b, o_scr, st_f, st_b) = refs
        s0_ref = None
    C = SCAN_CHUNK
    n_chunks = seq_len // C
    gqk = GLA_HEADS * HEAD_DK

    row = lax.broadcasted_iota(I32, (C, C), 0)
    col = lax.broadcasted_iota(I32, (C, C), 1)
    lower = col <= row
    upper = col >= row
    tri_lo = jnp.where(lower, 1.0, 0.0).astype(BF16)
    tri_up = jnp.where(upper, 1.0, 0.0).astype(BF16)

    lbp = lb_ref[...]
    lb_max = jnp.maximum(lbp[0], lbp[1])
    lb_e0 = jnp.exp(lbp[0] - lb_max)
    lb_e1 = jnp.exp(lbp[1] - lb_max)
    lb = lb_e0 / (lb_e0 + lb_e1)

    def cumsum_chunk(tri, la):
        hi, lo = _split_bf16(la)
        return _dot(tri, hi) + _dot(tri, lo)

    def prep(n, carry):
        r0 = pl.multiple_of(n * C, C)
        rows = pl.ds(r0, C)
        gq = p_ref[rows, _C_GQ:_C_GQ + gqk] * (HEAD_DK ** -0.5)
        gk = p_ref[rows, _C_GK:_C_GK + gqk]
        hq = _silu(p_ref[rows, _C_HQ:_C_HQ + gqk]) * (HEAD_DK ** -0.5)
        for d_i, (q_s, k_s, dec_s, tri, last) in enumerate(
                ((qf, kf, dec_f, tri_lo, C - 1), (qb, kb, dec_b, tri_up, 0))):
            c_ga = _C_GAF if d_i == 0 else _C_GAB
            c_hf = _C_HFF if d_i == 0 else _C_HFB
            ga = p_ref[rows, c_ga:c_ga + GATE_RANK]
            xg = _dot(ga.astype(BF16), a2_ref[d_i].astype(BF16)) + ab_ref[d_i]
            la_g = _log_sigmoid(xg) / GLA_GATE_NORM
            f = lb[d_i:d_i + 1, :] + (1.0 - lb[d_i:d_i + 1, :]) * jax.nn.sigmoid(
                p_ref[rows, c_hf:c_hf + gqk])
            la_h = jnp.log(f)
            for q, k, la, c0 in ((gq, gk, la_g, 0), (hq, 1.0 - f, la_h, gqk)):
                b = cumsum_chunk(tri, la)
                q_s[rows, c0:c0 + gqk] = (q * jnp.exp(b)).astype(BF16)
                k_s[rows, c0:c0 + gqk] = (k * jnp.exp(-b)).astype(BF16)
                dec_s[n, :, c0:c0 + gqk] = jnp.exp(b[last:last + 1, :])
        vv[rows, 0:512] = p_ref[rows, _C_GV:_C_GV + 512].astype(BF16)
        vv[rows, 512:1024] = p_ref[rows, _C_HI:_C_HI + 512].astype(BF16)
        return carry

    lax.fori_loop(0, n_chunks, prep, 0)

    o_scr[...] = jnp.zeros_like(o_scr)
    for h in range(SCAN_HEADS):
        if has_state:
            st_f[h] = s0_ref[0, 0, h]
            st_b[h] = s0_ref[0, 1, h]
        else:
            st_f[h] = jnp.zeros((HEAD_DV, HEAD_DK), F32)
            st_b[h] = jnp.zeros((HEAD_DV, HEAD_DK), F32)

    def sweep(n, carry):
        m = n_chunks - 1 - n
        rows = pl.ds(pl.multiple_of(n * C, C), C)
        rows_m = pl.ds(pl.multiple_of(m * C, C), C)
        decay_f, decay_b = dec_f[n], dec_b[m]
        for h in range(SCAN_HEADS):
            ks = slice(h * HEAD_DK, (h + 1) * HEAD_DK)
            vs = slice(h * HEAD_DV, (h + 1) * HEAD_DV)
            qd, kd, vh = qf[rows, ks], kf[rows, ks], vv[rows, vs]
            s_f = st_f[h]
            sc = (jnp.where(lower, _dot_nt(qd, kd), 0.0)
                  + jnp.where(upper, _dot_nt(qb[rows, ks], kb[rows, ks]), 0.0))
            o_scr[rows, vs] += _dot_nt(qd, s_f.astype(BF16)) + _dot(sc.astype(BF16), vh)
            st_f[h] = decay_f[:, ks] * (s_f + _dot_tn(vh, kd))
            s_b = st_b[h]
            vm, kbm = vv[rows_m, vs], kb[rows_m, ks]
            o_scr[rows_m, vs] += _dot_nt(qb[rows_m, ks], s_b.astype(BF16))
            st_b[h] = decay_b[:, ks] * (s_b + _dot_tn(vm, kbm))
        return carry

    lax.fori_loop(0, n_chunks, sweep, 0)

    def finish(n, carry):
        rows = pl.ds(pl.multiple_of(n * C, C), C)
        for h in range(SCAN_HEADS):
            vs = slice(h * HEAD_DV, (h + 1) * HEAD_DV)
            if h < GLA_HEADS:
                gate = p_ref[rows, _C_GG + h * HEAD_DV:_C_GG + (h + 1) * HEAD_DV]
                onw = ong_ref[...]
            else:
                hh = h - GLA_HEADS
                gate = p_ref[rows, _C_HG + hh * HEAD_DV:_C_HG + (hh + 1) * HEAD_DV]
                onw = onh_ref[...]
            mixed_ref[rows, vs] = ((_rms(o_scr[rows, vs]) * onw) * _silu(gate)).astype(BF16)
        return carry

    lax.fori_loop(0, n_chunks, finish, 0)

    if sfin_ref is not None:
        for h in range(SCAN_HEADS):
            sfin_ref[0, 0, h] = st_f[h]
            sfin_ref[0, 1, h] = st_b[h]


def _scan(p, row0, batch, seq_len, a2, a_bias, lb, onorm_g, onorm_h, s0=None):
    n = p.shape[1]
    assert row0 % seq_len == 0
    blk0 = row0 // seq_len
    has_state = s0 is not None
    n_chunks = seq_len // SCAN_CHUNK
    st_shape = (1, 2, SCAN_HEADS, HEAD_DV, HEAD_DK)
    in_specs = [pl.BlockSpec((seq_len, n), lambda b: (blk0 + b, 0), pipeline_mode=pl.Buffered(1)),
                pl.BlockSpec(a2.shape, lambda b: (0, 0, 0)),
                pl.BlockSpec(a_bias.shape, lambda b: (0, 0, 0)),
                pl.BlockSpec(lb.shape, lambda b: (0, 0, 0)),
                pl.BlockSpec((1, HEAD_DV), lambda b: (0, 0)),
                pl.BlockSpec((1, HEAD_DV), lambda b: (0, 0))]
    args = [p, a2, a_bias, lb, onorm_g, onorm_h]
    mixed_shape = jax.ShapeDtypeStruct((batch * seq_len, D_MODEL), BF16)
    mixed_spec = pl.BlockSpec((seq_len, D_MODEL), lambda b: (b, 0))
    if has_state:
        in_specs.append(pl.BlockSpec(st_shape, lambda b: (b, 0, 0, 0, 0)))
        args.append(s0)
        out_shape, out_specs = mixed_shape, mixed_spec
    else:
        out_shape = (mixed_shape, jax.ShapeDtypeStruct((batch,) + st_shape[1:], F32))
        out_specs = (mixed_spec, pl.BlockSpec(st_shape, lambda b: (b, 0, 0, 0, 0)))
    scratch = [pltpu.VMEM((seq_len, 512), BF16) for _ in range(4)]
    scratch += [pltpu.VMEM((seq_len, D_MODEL), BF16),
                pltpu.VMEM((n_chunks, 1, 512), F32), pltpu.VMEM((n_chunks, 1, 512), F32),
                pltpu.VMEM((seq_len, D_MODEL), F32),
                pltpu.VMEM((SCAN_HEADS, HEAD_DV, HEAD_DK), F32),
                pltpu.VMEM((SCAN_HEADS, HEAD_DV, HEAD_DK), F32)]
    return pl.pallas_call(
        functools.partial(_scan_kernel, seq_len=seq_len, has_state=has_state),
        grid=(batch,),
        in_specs=in_specs, out_specs=out_specs, out_shape=out_shape,
        scratch_shapes=scratch,
        compiler_params=_params(1),
        name="scan_state" if has_state else "scan_fresh",
    )(*args)


def _post_kernel(*refs, split_x, n_prompt_tiles):
    if split_x:
        xp_ref, xs_ref = refs[0], refs[1]
        refs = refs[2:]
    else:
        x_ref = refs[0]
        refs = refs[1:]
    (mp_ref, ms_ref, mod_ref, nw_ref, wo_ref, wrh_ref, wrl_ref,
     x1_ref, h2_ref, slab_ref, cnt_ref, carry) = refs
    i = pl.program_id(0)
    is_prompt = i < n_prompt_tiles
    if split_x:
        x = jnp.where(is_prompt, xp_ref[...], xs_ref[...])
    else:
        x = x_ref[...]
    mixed = jnp.where(is_prompt, mp_ref[...], ms_ref[...])
    x1 = x + mod_ref[0, 2:3, :] * _dot(mixed, wo_ref[...])
    x1_ref[...] = x1
    h2 = _modulate(x1, nw_ref[...], mod_ref[0, 3:4, :], mod_ref[0, 4:5, :])
    h2_ref[...] = h2

    hh, hl = _split_bf16(h2)
    logits = _dot(hh, wrh_ref[...]) + _dot(hl, wrh_ref[...]) + _dot(hh, wrl_ref[...])
    tm = logits.shape[0]
    lane = lax.broadcasted_iota(I32, (tm, LANES), 1).astype(F32)

    def first_max(v):
        mx = jnp.max(v, axis=1, keepdims=True)
        idx = jnp.min(jnp.where(v == mx, lane, float(LANES)), axis=1, keepdims=True)
        return mx, idx

    gl = jnp.where(lane < N_GROUPS, logits, NEG_BIG)
    gmax, gidx = first_max(gl)
    g_val = 1.0 / jnp.sum(jnp.exp(gl - gmax), axis=1, keepdims=True)
    lo = N_GROUPS + EXPERTS_PER_GROUP * gidx
    el = jnp.where((lane >= lo) & (lane < lo + EXPERTS_PER_GROUP), logits, NEG_BIG)
    emax, l1 = first_max(el)
    esum = jnp.sum(jnp.exp(el - emax), axis=1, keepdims=True)
    e2max, l2 = first_max(jnp.where(lane == l1, NEG_BIG, el))
    p1 = 1.0 / esum
    p2 = jnp.exp(e2max - emax) / esum
    w1 = g_val * (p1 / (p1 + p2))
    w2 = g_val * (p2 / (p1 + p2))
    id1 = l1 - N_GROUPS
    id2 = l2 - N_GROUPS

    @pl.when(i == 0)
    def _():
        carry[...] = jnp.zeros_like(carry)

    sel1 = lane == id1
    sel2 = lane == id2
    onehot = jnp.where(sel1 | sel2, 1.0, 0.0)
    row = lax.broadcasted_iota(I32, (tm, tm), 0)
    col = lax.broadcasted_iota(I32, (tm, tm), 1)
    earlier = jnp.where(col < row, 1.0, 0.0).astype(BF16)
    before = _dot(earlier, onehot.astype(BF16)) + carry[...]
    rank1 = jnp.sum(jnp.where(sel1, before, 0.0), axis=1, keepdims=True)
    rank2 = jnp.sum(jnp.where(sel2, before, 0.0), axis=1, keepdims=True)
    total = carry[...] + jnp.sum(onehot, axis=0, keepdims=True)
    carry[...] = total
    cnt_ref[...] = total

    slab = jnp.zeros((tm, LANES), F32)
    for k, v in enumerate((id1, id2, w1, w2, rank1, rank2)):
        slab = jnp.where(lane == k, v, slab)
    slab_ref[...] = slab


def _post(x_args, mixed_p, mixed_s, mods, layer, norm_w, w_out_bf16, wr_hi, wr_lo,
          tiles_per_sample):
    split_x = len(x_args) == 2
    tp, ts = mixed_p.shape[0], mixed_s.shape[0]
    t, d = tp + ts, D_MODEL
    npt, nst = tp // ROW_TILE, ts // ROW_TILE
    tile = lambda i: (i, 0)
    if split_x:
        x_specs = [pl.BlockSpec((ROW_TILE, d), lambda i: (jnp.minimum(i, npt - 1), 0)),
                   pl.BlockSpec((ROW_TILE, d), lambda i: (jnp.maximum(i - npt, 0), 0))]
    else:
        x_specs = [pl.BlockSpec((ROW_TILE, d), tile)]
    in_specs = x_specs + [
        pl.BlockSpec((ROW_TILE, d), lambda i: (jnp.minimum(i, npt - 1), 0)),
        pl.BlockSpec((ROW_TILE, d), lambda i: (jnp.maximum(i - npt, 0), 0)),
        pl.BlockSpec((1, 6, d), lambda i: (_mod_row(i, layer, npt, tiles_per_sample), 0, 0)),
        pl.BlockSpec((1, d), lambda i: (0, 0)),
        pl.BlockSpec((d, d), lambda i: (0, 0)),
        pl.BlockSpec((d, LANES), lambda i: (0, 0)),
        pl.BlockSpec((d, LANES), lambda i: (0, 0))]
    return pl.pallas_call(
        functools.partial(_post_kernel, split_x=split_x, n_prompt_tiles=npt),
        grid=(npt + nst,),
        in_specs=in_specs,
        out_specs=(pl.BlockSpec((ROW_TILE, d), tile), pl.BlockSpec((ROW_TILE, d), tile),
                   pl.BlockSpec((ROW_TILE, LANES), tile), pl.BlockSpec((1, LANES), lambda i: (0, 0))),
        out_shape=(jax.ShapeDtypeStruct((t, d), F32), jax.ShapeDtypeStruct((t, d), F32),
                   jax.ShapeDtypeStruct((t, LANES), F32), jax.ShapeDtypeStruct((1, LANES), F32)),
        scratch_shapes=[pltpu.VMEM((1, LANES), F32)],
        compiler_params=_params(1),
        name=f"post{layer}",
    )(*x_args, mixed_p, mixed_s, mods, norm_w, w_out_bf16, wr_hi, wr_lo)


def _dispatch_kernel(off_ref, tail_ref, route_ref, h2_ref, hs_ref, pos_ref, zero_buf, sem):
    j = pl.program_id(0)
    td = DISPATCH_TILE

    @pl.when(j == 0)
    def _():
        zero_buf[...] = jnp.zeros_like(zero_buf)

        def tail_copy(e):
            start = pl.multiple_of(tail_ref[e], EXPERT_TILE)
            return pltpu.make_async_copy(zero_buf, hs_ref.at[pl.ds(start, EXPERT_TILE)], sem)

        def start_tail(e, c):
            @pl.when(tail_ref[e] >= 0)
            def _():
                tail_copy(e).start()
            return c

        def wait_tail(e, c):
            @pl.when(tail_ref[e] >= 0)
            def _():
                tail_copy(e).wait()
            return c

        lax.fori_loop(0, N_EXPERTS, start_tail, 0)
        lax.fori_loop(0, N_EXPERTS, wait_tail, 0)

    def row_copy(src_row, dst_row):
        return pltpu.make_async_copy(h2_ref.at[pl.ds(src_row, 1)], hs_ref.at[pl.ds(dst_row, 1)], sem)

    def issue(r, c):
        for s in range(2):
            p = off_ref[route_ref[0, s, r]] + route_ref[0, 2 + s, r]
            pos_ref[0, s, r] = p
            row_copy(j * td + r, p).start()
        return c

    def drain(r, c):
        for s in range(2):
            row_copy(j * td + r, pos_ref[0, s, r]).wait()
        return c

    lax.fori_loop(0, td, issue, 0)
    lax.fori_loop(0, td, drain, 0)


def _dispatch(offsets, tails, route, h2, n_rows):
    t, d = h2.shape
    nt = t // DISPATCH_TILE
    grid_spec = pltpu.PrefetchScalarGridSpec(
        num_scalar_prefetch=2,
        grid=(nt,),
        in_specs=[pl.BlockSpec((1, 4, DISPATCH_TILE), lambda j, *_: (j, 0, 0),
                               memory_space=pltpu.SMEM),
                  pl.BlockSpec(memory_space=pl.ANY)],
        out_specs=(pl.BlockSpec(memory_space=pl.ANY),
                   pl.BlockSpec((1, 2, DISPATCH_TILE), lambda j, *_: (j, 0, 0),
                                memory_space=pltpu.SMEM)),
        scratch_shapes=[pltpu.VMEM((EXPERT_TILE, d), F32), pltpu.SemaphoreType.DMA(())])
    return pl.pallas_call(
        _dispatch_kernel,
        grid_spec=grid_spec,
        out_shape=(jax.ShapeDtypeStruct((n_rows, d), F32),
                   jax.ShapeDtypeStruct((nt, 2, DISPATCH_TILE), I32)),
        compiler_params=_params(1),
        name="dispatch",
    )(offsets, tails, route, h2)


def _expert_kernel(te_ref, src_ref, nv_ref, hs_ref, w1_ref, w3_ref, w2_ref, ys_ref,
                   w1b, w3b, w2b):
    i = pl.program_id(0)
    prev = te_ref[jnp.maximum(i - 1, 0)]

    @pl.when((i == 0) | (te_ref[i] != prev))
    def _():
        w1b[...] = w1_ref[0, 0].astype(BF16)
        w3b[...] = w3_ref[0, 0].astype(BF16)
        w2b[...] = w2_ref[0, 0].astype(BF16)

    @pl.when(nv_ref[i] > 0)
    def _():
        h = hs_ref[...].astype(BF16)
        g = _silu(_dot(h, w1b[...])) * _dot(h, w3b[...])
        ys_ref[...] = _dot(g.astype(BF16), w2b[...])


def _experts(tile_expert, tile_src, tile_rows, hs, w1, w3, w2, layer):
    n_rows, d = hs.shape
    nt = n_rows // EXPERT_TILE
    hid = w1.shape[-1]
    row_map = lambda i, te, src, nv: (src[i], 0)
    grid_spec = pltpu.PrefetchScalarGridSpec(
        num_scalar_prefetch=3,
        grid=(nt,),
        in_specs=[pl.BlockSpec((EXPERT_TILE, d), row_map),
                  pl.BlockSpec((1, 1, d, hid), lambda i, te, src, nv: (layer, te[i], 0, 0)),
                  pl.BlockSpec((1, 1, d, hid), lambda i, te, src, nv: (layer, te[i], 0, 0)),
                  pl.BlockSpec((1, 1, hid, d), lambda i, te, src, nv: (layer, te[i], 0, 0))],
        out_specs=pl.BlockSpec((EXPERT_TILE, d), row_map),
        scratch_shapes=[pltpu.VMEM((d, hid), BF16), pltpu.VMEM((d, hid), BF16),
                        pltpu.VMEM((hid, d), BF16)])
    return pl.pallas_call(
        _expert_kernel,
        grid_spec=grid_spec,
        out_shape=jax.ShapeDtypeStruct((n_rows, d), F32),
        compiler_params=_params(1),
        name=f"experts{layer}",
    )(tile_expert, tile_src, tile_rows, hs, w1, w3, w2)


def _combine_kernel(pos_ref, x1_ref, slab_ref, mod_ref, fw_ref, ys_ref, out_ref, ybuf, sem,
                    *, final_norm):
    tm = ROW_TILE

    def row_copy(s, r):
        return pltpu.make_async_copy(ys_ref.at[pl.ds(pos_ref[0, s, r], 1)],
                                     ybuf.at[s, pl.ds(r, 1)], sem)

    def issue(r, c):
        row_copy(0, r).start()
        row_copy(1, r).start()
        return c

    def drain(r, c):
        row_copy(0, r).wait()
        row_copy(1, r).wait()
        return c

    lax.fori_loop(0, tm, issue, 0)
    lax.fori_loop(0, tm, drain, 0)
    y = slab_ref[:, 2:3] * ybuf[0] + slab_ref[:, 3:4] * ybuf[1]
    x2 = x1_ref[...] + mod_ref[0, 5:6, :] * y
    if final_norm:
        x2 = _rms(x2) * fw_ref[...]
    out_ref[...] = x2


def _combine(pos, x1, slab, mods, layer, final_w, ys, tile0, n_tiles, n_prompt_tiles,
             tiles_per_sample, final_norm):
    d = D_MODEL
    tile = lambda i: (tile0 + i, 0)
    mod_map = lambda i: (_mod_row(tile0 + i, layer, n_prompt_tiles, tiles_per_sample), 0, 0)
    return pl.pallas_call(
        functools.partial(_combine_kernel, final_norm=final_norm),
        grid=(n_tiles,),
        in_specs=[pl.BlockSpec((1, 2, ROW_TILE), lambda i: (tile0 + i, 0, 0),
                               memory_space=pltpu.SMEM),
                  pl.BlockSpec((ROW_TILE, d), tile),
                  pl.BlockSpec((ROW_TILE, LANES), tile),
                  pl.BlockSpec((1, 6, d), mod_map),
                  pl.BlockSpec((1, d), lambda i: (0, 0)),
                  pl.BlockSpec(memory_space=pl.ANY)],
        out_specs=pl.BlockSpec((ROW_TILE, d), lambda i: (i, 0)),
        out_shape=jax.ShapeDtypeStruct((n_tiles * ROW_TILE, d), F32),
        scratch_shapes=[pltpu.VMEM((2, ROW_TILE, d), F32), pltpu.SemaphoreType.DMA(())],
        compiler_params=_params(1),
        name=f"combine{layer}_{tile0}",
    )(pos, x1, slab, mods, final_w, ys)


def _moe(h2, slab, counts, w1, w3, w2, layer):
    t = h2.shape[0]
    n_rows = 2 * t + N_EXPERTS * EXPERT_TILE
    nt = n_rows // EXPERT_TILE
    cnt = counts[0, :N_EXPERTS].astype(I32)
    padded = ((cnt + EXPERT_TILE - 1) // EXPERT_TILE) * EXPERT_TILE
    ends = jnp.cumsum(padded)
    offsets = ends - padded
    tails = jnp.where(cnt > 0, ends - EXPERT_TILE, -1).astype(I32)
    used = ends[-1] // EXPERT_TILE
    tile_start = jnp.arange(nt, dtype=I32) * EXPERT_TILE
    tile_src = jnp.minimum(jnp.arange(nt, dtype=I32), used - 1)
    tile_expert = jnp.sum((tile_src * EXPERT_TILE)[:, None] >= ends[None, :], axis=1).astype(I32)
    tile_rows = jnp.where(tile_start < ends[-1],
                          jnp.clip(cnt[tile_expert] - (tile_start - offsets[tile_expert]),
                                   0, EXPERT_TILE), 0).astype(I32)
    route = slab[:, jnp.array([0, 1, 4, 5])].astype(I32)
    route = route.reshape(t // DISPATCH_TILE, DISPATCH_TILE, 4).transpose(0, 2, 1)
    hs, pos = _dispatch(offsets.astype(I32), tails, route, h2, n_rows)
    ys = _experts(tile_expert, tile_src, tile_rows, hs, w1, w3, w2, layer)
    pos = pos.reshape(t // DISPATCH_TILE, 2, DISPATCH_TILE // ROW_TILE, ROW_TILE)
    pos = pos.transpose(0, 2, 1, 3).reshape(t // ROW_TILE, 2, ROW_TILE)
    return ys, pos


def _rope(x, cos, sin_signed):
    lane = lax.broadcasted_iota(I32, (x.shape[0], LANES), 1)
    low = (lane % 32) < 16
    outs = []
    for j in range(x.shape[1] // LANES):
        xb = x[:, j * LANES:(j + 1) * LANES]
        partner = jnp.where(low, pltpu.roll(xb, LANES - 16, 1), pltpu.roll(xb, 16, 1))
        outs.append(xb * cos + partner * sin_signed)
    return jnp.concatenate(outs, axis=1)


def _inproj1_prompt_kernel(x_ref, mod_ref, nw_ref, w_ref, q_ref, k_ref, v_ref):
    d = D_MODEL
    h = _modulate(x_ref[...], nw_ref[...], mod_ref[0, 0:1, :], mod_ref[0, 1:2, :]).astype(BF16)
    q_ref[...] = (_dot(h, w_ref[:, 0:d]) * (DIFF_HD ** -0.5)).astype(BF16)
    k_ref[...] = _dot(h, w_ref[:, d:2 * d])
    v_ref[...] = _dot(h, w_ref[:, 2 * d:3 * d])


def _inproj1_sample_kernel(x_ref, mod_ref, nw_ref, w_ref, cos_ref, sin_ref, q_ref, k_ref, v_ref):
    d = D_MODEL
    h = _modulate(x_ref[...], nw_ref[...], mod_ref[0, 0:1, :], mod_ref[0, 1:2, :]).astype(BF16)
    cos, sin = cos_ref[...], sin_ref[...]
    q_ref[...] = (_rope(_dot(h, w_ref[:, 0:d]), cos, sin) * (DIFF_HD ** -0.5)).astype(BF16)
    k_ref[...] = _rope(_dot(h, w_ref[:, d:2 * d]), cos, sin).astype(BF16)
    v_ref[...] = _dot(h, w_ref[:, 2 * d:3 * d]).astype(BF16)


def _inproj1(x, mods, norm_w, w_bf16, n_prompt_tiles, n_sample_tiles, tiles_per_sample,
             cos_t, sin_t):
    d = D_MODEL
    npt, nst = n_prompt_tiles, n_sample_tiles
    common = [pl.BlockSpec((1, d), lambda i: (0, 0)), pl.BlockSpec((d, 3 * d), lambda i: (0, 0))]
    tile = lambda i: (i, 0)
    out_specs = tuple(pl.BlockSpec((ROW_TILE, d), tile) for _ in range(3))
    qp, kp, vp = pl.pallas_call(
        _inproj1_prompt_kernel,
        grid=(npt,),
        in_specs=[pl.BlockSpec((ROW_TILE, d), tile),
                  pl.BlockSpec((1, 6, d), lambda i: (8, 0, 0))] + common,
        out_specs=out_specs,
        out_shape=(jax.ShapeDtypeStruct((npt * ROW_TILE, d), BF16),
                   jax.ShapeDtypeStruct((npt * ROW_TILE, d), F32),
                   jax.ShapeDtypeStruct((npt * ROW_TILE, d), F32)),
        compiler_params=_params(1),
        name="inproj1_prompt",
    )(x, mods, norm_w, w_bf16)
    rope_tile = lambda i: (i % tiles_per_sample, 0)
    qs, ks, vs = pl.pallas_call(
        _inproj1_sample_kernel,
        grid=(nst,),
        in_specs=[pl.BlockSpec((ROW_TILE, d), lambda i: (npt + i, 0)),
                  pl.BlockSpec((1, 6, d), lambda i: (8 + 1 + i // tiles_per_sample, 0, 0))]
        + common + [pl.BlockSpec((ROW_TILE, LANES), rope_tile),
                    pl.BlockSpec((ROW_TILE, LANES), rope_tile)],
        out_specs=out_specs,
        out_shape=tuple(jax.ShapeDtypeStruct((nst * ROW_TILE, d), BF16) for _ in range(3)),
        compiler_params=_params(1),
        name="inproj1_sample",
    )(x, mods, norm_w, w_bf16, cos_t, sin_t)
    return (qp, kp, vp), (qs, ks, vs)


def _rope_tables(n_tok):
    half = DIFF_HD // 4
    pos = np.arange(n_tok)
    lane = np.arange(LANES)
    sub = lane % DIFF_HD
    p = np.where(sub[None, :] < DIFF_HD // 2, (pos // GRID_W)[:, None], (pos % GRID_W)[:, None])
    inv = jnp.asarray(ROPE_THETA, F32) ** (-jnp.asarray(sub % half, F32) / half)
    ang = jnp.asarray(p, F32) * inv[None, :]
    sign = np.where((lane % (2 * half)) < half, -1.0, 1.0).astype(np.float32)
    return jnp.cos(ang), jnp.sin(ang) * sign[None, :]


def _diffattn_kernel(*refs, has_cache, lam_init):
    if has_cache:
        q_ref, k_ref, v_ref, ck_ref, cv_ref, lam_ref, sw_ref, o_ref = refs
    else:
        q_ref, k_ref, v_ref, lam_ref, sw_ref, o_ref = refs
    lv = lam_ref[...]
    lam = (jnp.exp(jnp.sum(lv[0:1] * lv[1:2], axis=1, keepdims=True))
           - jnp.exp(jnp.sum(lv[2:3] * lv[3:4], axis=1, keepdims=True)) + lam_init)
    q = q_ref[...]
    lane = lax.broadcasted_iota(I32, q.shape, 1)
    zero = jnp.zeros_like(q)
    keys = [(k_ref[...].astype(BF16), v_ref[...].astype(BF16))]
    if has_cache:
        keys.append((ck_ref[...].astype(BF16), cv_ref[...].astype(BF16)))
    o = None
    for c in range(2):
        qc = jnp.where((lane < DIFF_HD) == (c == 0), q, zero)
        s = [_dot_nt(qc, k) for k, _ in keys]
        mx = functools.reduce(jnp.maximum, [jnp.max(si, axis=1, keepdims=True) for si in s])
        e = [jnp.exp(si - mx) for si in s]
        z = functools.reduce(jnp.add, [jnp.sum(ei, axis=1, keepdims=True) for ei in e])
        pv = functools.reduce(jnp.add, [_dot(ei.astype(BF16), v) for ei, (_, v) in zip(e, keys)])
        pv = pv * (1.0 / z)
        o = pv if c == 0 else o - lam * pv
    o_ref[...] = ((_rms(o) * sw_ref[...]) * (1.0 - lam_init)).astype(BF16)


def _diffattn(q, k, v, lam_vecs, subln_w, batch, seq_len, q_block, lam_init, cache=None):
    hd2 = 2 * DIFF_HD
    nq = seq_len // q_block
    has_cache = cache is not None
    kv_spec = pl.BlockSpec((seq_len, hd2), lambda b, h, qi: (b, h))
    in_specs = [pl.BlockSpec((q_block, hd2), lambda b, h, qi: (b * nq + qi, h)), kv_spec, kv_spec]
    args = [q, k, v]
    if has_cache:
        past = cache[0].shape[0] // batch
        c_spec = pl.BlockSpec((past, hd2), lambda b, h, qi: (b, h))
        in_specs += [c_spec, c_spec]
        args += list(cache)
    in_specs += [pl.BlockSpec((4, DIFF_HD), lambda b, h, qi: (0, 0)),
                 pl.BlockSpec((1, hd2), lambda b, h, qi: (0, 0))]
    args += [lam_vecs, subln_w]
    return pl.pallas_call(
        functools.partial(_diffattn_kernel, has_cache=has_cache, lam_init=lam_init),
        grid=(batch, DIFF_HEADS, nq),
        in_specs=in_specs,
        out_specs=pl.BlockSpec((q_block, hd2), lambda b, h, qi: (b * nq + qi, h)),
        out_shape=jax.ShapeDtypeStruct((batch * seq_len, D_MODEL), BF16),
        compiler_params=_params(3),
        name="diffattn_cache" if has_cache else "diffattn",
    )(*args)


def _router_weights(router_group, router_expert):
    w = jnp.concatenate([router_group, router_expert], axis=1)
    w = jnp.pad(w, ((0, 0), (0, LANES - w.shape[1])))
    hi = w.astype(BF16)
    return hi, (w - hi.astype(F32)).astype(BF16)


def _inproj0_weights(w_in):
    gq, gk, gv, gg, gaf, gab, hq, hff, hfb, hi, hg = jnp.split(
        w_in, [256, 512, 1024, 1536, 1552, 1568, 1824, 2080, 2336, 2848], axis=1)
    w = jnp.concatenate([gq, gk, gv, gg, hq, hff, hfb, hi, hg, gaf, gab], axis=1)
    return jnp.pad(w, ((0, 0), (0, AB_COLS - w.shape[1]))).astype(BF16)


def kernel(x_prompt, x_sample, state_gla, state_hgrn, cache_diff_k, cache_diff_v, c, c_ctx,
           w_ada, b_ada, norm1_w, norm2_w, w_in_ab, gla_a2, gla_a_bias, hgrn_lb, gla_onorm_w,
           hgrn_onorm_w, w_out_ab, w_in_c, lam_q1, lam_k1, lam_q2, lam_k2, diff_subln_w, w_out_c,
           router_group, router_expert, moe_w1, moe_w3, moe_w2, final_norm_w):
    bp, lp, d = x_prompt.shape
    bs, ls, _ = x_sample.shape
    depth = w_ada.shape[0]
    assert depth == 2 and d == D_MODEL and bs <= 7
    tp, ts = bp * lp, bs * ls
    npt, nst = tp // ROW_TILE, ts // ROW_TILE
    tps = ls // ROW_TILE
    xp = x_prompt.reshape(tp, d)
    xs = x_sample.reshape(ts, d)

    cond8 = jnp.concatenate([c_ctx[None, :], c, jnp.zeros((7 - bs, d), F32)], axis=0)
    mods = _adaln(cond8, w_ada, b_ada).reshape(depth * 8, 6, d)

    proj = _inproj0(xp, xs, mods, norm1_w[0:1], _inproj0_weights(w_in_ab[0]), tps)
    a_bias = gla_a_bias[0][:, None, :]
    scan_args = (gla_a2[0], a_bias, hgrn_lb, gla_onorm_w[0:1], hgrn_onorm_w[0:1])
    mixed_p, s_fin = _scan(proj, 0, bp, lp, *scan_args)
    s0 = jnp.concatenate([state_gla[:, 0], state_hgrn[:, 0]], axis=2).swapaxes(-1, -2)
    mixed_s = _scan(proj, tp, bs, ls, *scan_args, s0=s0)
    s_fin = s_fin.swapaxes(-1, -2)
    new_state_gla = s_fin[:, None, :, :GLA_HEADS]
    new_state_hgrn = s_fin[:, None, :, GLA_HEADS:]

    wr = _router_weights(router_group[0], router_expert[0])
    x1, h2, slab, counts = _post((xp, xs), mixed_p, mixed_s, mods, 0, norm2_w[0:1],
                                 w_out_ab[0].astype(BF16), *wr, tps)
    ys, pos = _moe(h2, slab, counts, moe_w1, moe_w3, moe_w2, 0)
    x2 = _combine(pos, x1, slab, mods, 0, final_norm_w[None, :], ys, 0, npt + nst, npt, tps, False)

    lam_init = 0.8 - 0.6 * math.exp(-0.3 * 1)
    cos_t, sin_t = _rope_tables(ls)
    (qp, kp, vp), (qs, ks, vs) = _inproj1(x2, mods, norm1_w[1:2], w_in_c[0].astype(BF16),
                                          npt, nst, tps, cos_t, sin_t)
    lam_vecs = jnp.stack([lam_q1[0], lam_k1[0], lam_q2[0], lam_k2[0]])
    att_p = _diffattn(qp, kp, vp, lam_vecs, diff_subln_w[0:1], bp, lp, lp, lam_init)
    past = cache_diff_k.shape[2]
    cache = (cache_diff_k[:, 0].reshape(bs * past, d), cache_diff_v[:, 0].reshape(bs * past, d))
    att_s = _diffattn(qs, ks, vs, lam_vecs, diff_subln_w[0:1], bs, ls, ROW_TILE, lam_init, cache)

    wr = _router_weights(router_group[1], router_expert[1])
    x3, h2, slab, counts = _post((x2,), att_p, att_s, mods, 1, norm2_w[1:2],
                                 w_out_c[0].astype(BF16), *wr, tps)
    ys, pos = _moe(h2, slab, counts, moe_w1, moe_w3, moe_w2, 1)
    fw = final_norm_w[None, :]
    y_p = _combine(pos, x3, slab, mods, 1, fw, ys, 0, npt, npt, tps, True)
    y_s = _combine(pos, x3, slab, mods, 1, fw, ys, npt, nst, npt, tps, True)

    return (y_p.reshape(bp, lp, d), y_s.reshape(bs, ls, d), new_state_gla, new_state_hgrn,
            kp.reshape(bp, 1, lp, DIFF_HEADS, 2, DIFF_HD),
            vp.reshape(bp, 1, lp, DIFF_HEADS, 2 * DIFF_HD))
```

```python
import functools
import math

import jax
import jax.numpy as jnp
import numpy as np
from jax import lax
from jax.experimental import pallas as pl
from jax.experimental.pallas import tpu as pltpu

F32 = jnp.float32
BF16 = jnp.bfloat16
I32 = jnp.int32

D_MODEL = 1024
GLA_HEADS = 4
HGRN_HEADS = 4
SCAN_HEADS = GLA_HEADS + HGRN_HEADS
HEAD_DK = 64
HEAD_DV = 128
GATE_RANK = 16
GLA_GATE_NORM = 16.0
DIFF_HEADS = 8
DIFF_HD = 64
GRID_W = 64
ROPE_THETA = 10000.0
N_GROUPS = 4
EXPERTS_PER_GROUP = 8
N_EXPERTS = N_GROUPS * EXPERTS_PER_GROUP
MOE_HIDDEN = 512
EPS = 1e-6
LANES = 128
NEG_BIG = -1e30

ROW_TILE = 256
SCAN_CHUNK = 64
EXPERT_TILE = 256
DISPATCH_TILE = 512
VMEM_LIMIT = 56 * 1024 * 1024

_C_GQ, _C_GK, _C_GV, _C_GG = 0, 256, 512, 1024
_C_HQ, _C_HFF, _C_HFB, _C_HI, _C_HG = 1536, 1792, 2048, 2304, 2816
_C_GAF, _C_GAB = 3328, 3344
AB_COLS = 3456


def _params(n_axes, vmem=VMEM_LIMIT):
    return pltpu.CompilerParams(dimension_semantics=("arbitrary",) * n_axes,
                                vmem_limit_bytes=vmem)


def _dot(a, b):
    return jnp.dot(a, b, preferred_element_type=F32)


def _dot_nt(a, b):
    return lax.dot_general(a, b, (((1,), (1,)), ((), ())), preferred_element_type=F32)


def _dot_tn(a, b):
    return lax.dot_general(a, b, (((0,), (0,)), ((), ())), preferred_element_type=F32)


def _split_bf16(x):
    hi = x.astype(BF16)
    lo = (x - hi.astype(F32)).astype(BF16)
    return hi, lo


def _silu(x):
    return x * jax.nn.sigmoid(x)


def _log_sigmoid(x):
    return jnp.minimum(x, 0.0) - jnp.log(1.0 + jnp.exp(-jnp.abs(x)))


def _rms(x):
    return x * lax.rsqrt(jnp.mean(x * x, axis=-1, keepdims=True) + EPS)


def _modulate(x, norm_w, shift, scale):
    return (_rms(x) * norm_w) * (1.0 + scale) + shift


def _ada_kernel(c_ref, w_ref, b_ref, o_ref):
    s = _silu(c_ref[...])
    o_ref[0] = _dot(s.astype(BF16), w_ref[0].astype(BF16)) + b_ref[0]


def _adaln(cond8, w_ada, b_ada):
    depth, d, n = w_ada.shape
    tn = 1536
    return pl.pallas_call(
        _ada_kernel,
        grid=(depth, n // tn),
        in_specs=[pl.BlockSpec((8, d), lambda l, j: (0, 0)),
                  pl.BlockSpec((1, d, tn), lambda l, j: (l, 0, j)),
                  pl.BlockSpec((1, 1, tn), lambda l, j: (l, 0, j))],
        out_specs=pl.BlockSpec((1, 8, tn), lambda l, j: (l, 0, j)),
        out_shape=jax.ShapeDtypeStruct((depth, 8, n), F32),
        compiler_params=_params(2),
        name="adaln",
    )(cond8, w_ada, b_ada.reshape(depth, 1, n))


def _mod_row(i, layer, n_prompt_tiles, tiles_per_sample):
    r = jnp.where(i < n_prompt_tiles, 0, 1 + (i - n_prompt_tiles) // tiles_per_sample)
    return layer * 8 + r


def _inproj0_kernel(xp_ref, xs_ref, mod_ref, nw_ref, w_ref, o_ref, *, n_prompt_tiles):
    i = pl.program_id(0)
    x = jnp.where(i < n_prompt_tiles, xp_ref[...], xs_ref[...])
    h = _modulate(x, nw_ref[...], mod_ref[0, 0:1, :], mod_ref[0, 1:2, :])
    o_ref[...] = _dot(h.astype(BF16), w_ref[...])


def _inproj0(xp, xs, mods, norm_w, w_bf16, tiles_per_sample):
    tp, d = xp.shape
    ts = xs.shape[0]
    n = w_bf16.shape[1]
    npt, nst = tp // ROW_TILE, ts // ROW_TILE
    mod_map = lambda i: (_mod_row(i, 0, npt, tiles_per_sample), 0, 0)
    return pl.pallas_call(
        functools.partial(_inproj0_kernel, n_prompt_tiles=npt),
        grid=(npt + nst,),
        in_specs=[pl.BlockSpec((ROW_TILE, d), lambda i: (jnp.minimum(i, npt - 1), 0)),
                  pl.BlockSpec((ROW_TILE, d), lambda i: (jnp.maximum(i - npt, 0), 0)),
                  pl.BlockSpec((1, 6, d), mod_map),
                  pl.BlockSpec((1, d), lambda i: (0, 0)),
                  pl.BlockSpec((d, n), lambda i: (0, 0))],
        out_specs=pl.BlockSpec((ROW_TILE, n), lambda i: (i, 0)),
        out_shape=jax.ShapeDtypeStruct((tp + ts, n), F32),
        compiler_params=_params(1),
        name="inproj0",
    )(xp, xs, mods, norm_w, w_bf16)


def _scan_kernel(*refs, seq_len, has_state):
    if has_state:
        (p_ref, a2_ref, ab_ref, lb_ref, ong_ref, onh_ref, s0_ref,
         mixed_ref, qf, kf, qb, kb, vv, dec_f, dec_b, o_scr, st_f, st_b) = refs
        sfin_ref = None
    else:
        (p_ref, a2_ref, ab_ref, lb_ref, ong_ref, onh_ref,
         mixed_ref, sfin_ref, qf, kf, qb, kb, vv, dec_f, dec_b, o_scr, st_f, st_b) = refs
        s0_ref = None
    C = SCAN_CHUNK
    n_chunks = seq_len // C
    gqk = GLA_HEADS * HEAD_DK

    row = lax.broadcasted_iota(I32, (C, C), 0)
    col = lax.broadcasted_iota(I32, (C, C), 1)
    lower = col <= row
    upper = col >= row
    tri_lo = jnp.where(lower, 1.0, 0.0).astype(BF16)
    tri_up = jnp.where(upper, 1.0, 0.0).astype(BF16)

    lbp = lb_ref[...]
    lb_max = jnp.maximum(lbp[0], lbp[1])
    lb_e0 = jnp.exp(lbp[0] - lb_max)
    lb_e1 = jnp.exp(lbp[1] - lb_max)
    lb = lb_e0 / (lb_e0 + lb_e1)

    def cumsum_chunk(tri, la):
        hi, lo = _split_bf16(la)
        return _dot(tri, hi) + _dot(tri, lo)

    def prep(n, carry):
        r0 = pl.multiple_of(n * C, C)
        rows = pl.ds(r0, C)
        gq = p_ref[rows, _C_GQ:_C_GQ + gqk] * (HEAD_DK ** -0.5)
        gk = p_ref[rows, _C_GK:_C_GK + gqk]
        hq = _silu(p_ref[rows, _C_HQ:_C_HQ + gqk]) * (HEAD_DK ** -0.5)
        for d_i, (q_s, k_s, dec_s, tri, last) in enumerate(
                ((qf, kf, dec_f, tri_lo, C - 1), (qb, kb, dec_b, tri_up, 0))):
            c_ga = _C_GAF if d_i == 0 else _C_GAB
            c_hf = _C_HFF if d_i == 0 else _C_HFB
            ga = p_ref[rows, c_ga:c_ga + GATE_RANK]
            xg = _dot(ga.astype(BF16), a2_ref[d_i].astype(BF16)) + ab_ref[d_i]
            la_g = _log_sigmoid(xg) / GLA_GATE_NORM
            f = lb[d_i:d_i + 1, :] + (1.0 - lb[d_i:d_i + 1, :]) * jax.nn.sigmoid(
                p_ref[rows, c_hf:c_hf + gqk])
            la_h = jnp.log(f)
            for q, k, la, c0 in ((gq, gk, la_g, 0), (hq, 1.0 - f, la_h, gqk)):
                b = cumsum_chunk(tri, la)
                q_s[rows, c0:c0 + gqk] = (q * jnp.exp(b)).astype(BF16)
                k_s[rows, c0:c0 + gqk] = (k * jnp.exp(-b)).astype(BF16)
                dec_s[n, :, c0:c0 + gqk] = jnp.exp(b[last:last + 1, :])
        vv[rows, 0:512] = p_ref[rows, _C_GV:_C_GV + 512].astype(BF16)
        vv[rows, 512:1024] = p_ref[rows, _C_HI:_C_HI + 512].astype(BF16)
        return carry

    lax.fori_loop(0, n_chunks, prep, 0)

    o_scr[...] = jnp.zeros_like(o_scr)
    for h in range(SCAN_HEADS):
        if has_state:
            st_f[h] = s0_ref[0, 0, h]
            st_b[h] = s0_ref[0, 1, h]
        else:
            st_f[h] = jnp.zeros((HEAD_DV, HEAD_DK), F32)
            st_b[h] = jnp.zeros((HEAD_DV, HEAD_DK), F32)

    def sweep(n, carry):
        m = n_chunks - 1 - n
        rows = pl.ds(pl.multiple_of(n * C, C), C)
        rows_m = pl.ds(pl.multiple_of(m * C, C), C)
        decay_f, decay_b = dec_f[n], dec_b[m]
        for h in range(SCAN_HEADS):
            ks = slice(h * HEAD_DK, (h + 1) * HEAD_DK)
            vs = slice(h * HEAD_DV, (h + 1) * HEAD_DV)
            qd, kd, vh = qf[rows, ks], kf[rows, ks], vv[rows, vs]
            s_f = st_f[h]
            sc = (jnp.where(lower, _dot_nt(qd, kd), 0.0)
                  + jnp.where(upper, _dot_nt(qb[rows, ks], kb[rows, ks]), 0.0))
            o_scr[rows, vs] += _dot_nt(qd, s_f.astype(BF16)) + _dot(sc.astype(BF16), vh)
            st_f[h] = decay_f[:, ks] * (s_f + _dot_tn(vh, kd))
            s_b = st_b[h]
            vm, kbm = vv[rows_m, vs], kb[rows_m, ks]
            o_scr[rows_m, vs] += _dot_nt(qb[rows_m, ks], s_b.astype(BF16))
            st_b[h] = decay_b[:, ks] * (s_b + _dot_tn(vm, kbm))
        return carry

    lax.fori_loop(0, n_chunks, sweep, 0)

    def finish(n, carry):
        rows = pl.ds(pl.multiple_of(n * C, C), C)
        for h in range(SCAN_HEADS):
            vs = slice(h * HEAD_DV, (h + 1) * HEAD_DV)
            if h < GLA_HEADS:
                gate = p_ref[rows, _C_GG + h * HEAD_DV:_C_GG + (h + 1) * HEAD_DV]
                onw = ong_ref[...]
            else:
                hh = h - GLA_HEADS
                gate = p_ref[rows, _C_HG + hh * HEAD_DV:_C_HG + (hh + 1) * HEAD_DV]
                onw = onh_ref[...]
            mixed_ref[rows, vs] = ((_rms(o_scr[rows, vs]) * onw) * _silu(gate)).astype(BF16)
        return carry

    lax.fori_loop(0, n_chunks, finish, 0)

    if sfin_ref is not None:
        for h in range(SCAN_HEADS):
            sfin_ref[0, 0, h] = st_f[h]
            sfin_ref[0, 1, h] = st_b[h]


def _scan(p, row0, batch, seq_len, a2, a_bias, lb, onorm_g, onorm_h, s0=None):
    n = p.shape[1]
    assert row0 % seq_len == 0
    blk0 = row0 // seq_len
    has_state = s0 is not None
    n_chunks = seq_len // SCAN_CHUNK
    st_shape = (1, 2, SCAN_HEADS, HEAD_DV, HEAD_DK)
    in_specs = [pl.BlockSpec((seq_len, n), lambda b: (blk0 + b, 0), pipeline_mode=pl.Buffered(1)),
                pl.BlockSpec(a2.shape, lambda b: (0, 0, 0)),
                pl.BlockSpec(a_bias.shape, lambda b: (0, 0, 0)),
                pl.BlockSpec(lb.shape, lambda b: (0, 0, 0)),
                pl.BlockSpec((1, HEAD_DV), lambda b: (0, 0)),
                pl.BlockSpec((1, HEAD_DV), lambda b: (0, 0))]
    args = [p, a2, a_bias, lb, onorm_g, onorm_h]
    mixed_shape = jax.ShapeDtypeStruct((batch * seq_len, D_MODEL), BF16)
    mixed_spec = pl.BlockSpec((seq_len, D_MODEL), lambda b: (b, 0))
    if has_state:
        in_specs.append(pl.BlockSpec(st_shape, lambda b: (b, 0, 0, 0, 0)))
        args.append(s0)
        out_shape, out_specs = mixed_shape, mixed_spec
    else:
        out_shape = (mixed_shape, jax.ShapeDtypeStruct((batch,) + st_shape[1:], F32))
        out_specs = (mixed_spec, pl.BlockSpec(st_shape, lambda b: (b, 0, 0, 0, 0)))
    scratch = [pltpu.VMEM((seq_len, 512), BF16) for _ in range(4)]
    scratch += [pltpu.VMEM((seq_len, D_MODEL), BF16),
                pltpu.VMEM((n_chunks, 1, 512), F32), pltpu.VMEM((n_chunks, 1, 512), F32),
                pltpu.VMEM((seq_len, D_MODEL), F32),
                pltpu.VMEM((SCAN_HEADS, HEAD_DV, HEAD_DK), F32),
                pltpu.VMEM((SCAN_HEADS, HEAD_DV, HEAD_DK), F32)]
    return pl.pallas_call(
        functools.partial(_scan_kernel, seq_len=seq_len, has_state=has_state),
        grid=(batch,),
        in_specs=in_specs, out_specs=out_specs, out_shape=out_shape,
        scratch_shapes=scratch,
        compiler_params=_params(1),
        name="scan_state" if has_state else "scan_fresh",
    )(*args)


def _post_kernel(*refs, split_x, n_prompt_tiles):
    if split_x:
        xp_ref, xs_ref = refs[0], refs[1]
        refs = refs[2:]
    else:
        x_ref = refs[0]
        refs = refs[1:]
    (mp_ref, ms_ref, mod_ref, nw_ref, wo_ref, wrh_ref, wrl_ref,
     x1_ref, h2_ref, slab_ref, cnt_ref, carry) = refs
    i = pl.program_id(0)
    is_prompt = i < n_prompt_tiles
    if split_x:
        x = jnp.where(is_prompt, xp_ref[...], xs_ref[...])
    else:
        x = x_ref[...]
    mixed = jnp.where(is_prompt, mp_ref[...], ms_ref[...])
    x1 = x + mod_ref[0, 2:3, :] * _dot(mixed, wo_ref[...])
    x1_ref[...] = x1
    h2 = _modulate(x1, nw_ref[...], mod_ref[0, 3:4, :], mod_ref[0, 4:5, :])
    h2_ref[...] = h2

    hh, hl = _split_bf16(h2)
    logits = _dot(hh, wrh_ref[...]) + _dot(hl, wrh_ref[...]) + _dot(hh, wrl_ref[...])
    tm = logits.shape[0]
    lane = lax.broadcasted_iota(I32, (tm, LANES), 1).astype(F32)

    def first_max(v):
        mx = jnp.max(v, axis=1, keepdims=True)
        idx = jnp.min(jnp.where(v == mx, lane, float(LANES)), axis=1, keepdims=True)
        return mx, idx

    gl = jnp.where(lane < N_GROUPS, logits, NEG_BIG)
    gmax, gidx = first_max(gl)
    g_val = 1.0 / jnp.sum(jnp.exp(gl - gmax), axis=1, keepdims=True)
    lo = N_GROUPS + EXPERTS_PER_GROUP * gidx
    el = jnp.where((lane >= lo) & (lane < lo + EXPERTS_PER_GROUP), logits, NEG_BIG)
    emax, l1 = first_max(el)
    esum = jnp.sum(jnp.exp(el - emax), axis=1, keepdims=True)
    e2max, l2 = first_max(jnp.where(lane == l1, NEG_BIG, el))
    p1 = 1.0 / esum
    p2 = jnp.exp(e2max - emax) / esum
    w1 = g_val * (p1 / (p1 + p2))
    w2 = g_val * (p2 / (p1 + p2))
    id1 = l1 - N_GROUPS
    id2 = l2 - N_GROUPS

    @pl.when(i == 0)
    def _():
        carry[...] = jnp.zeros_like(carry)

    sel1 = lane == id1
    sel2 = lane == id2
    onehot = jnp.where(sel1 | sel2, 1.0, 0.0)
    row = lax.broadcasted_iota(I32, (tm, tm), 0)
    col = lax.broadcasted_iota(I32, (tm, tm), 1)
    earlier = jnp.where(col < row, 1.0, 0.0).astype(BF16)
    before = _dot(earlier, onehot.astype(BF16)) + carry[...]
    rank1 = jnp.sum(jnp.where(sel1, before, 0.0), axis=1, keepdims=True)
    rank2 = jnp.sum(jnp.where(sel2, before, 0.0), axis=1, keepdims=True)
    total = carry[...] + jnp.sum(onehot, axis=0, keepdims=True)
    carry[...] = total
    cnt_ref[...] = total

    slab = jnp.zeros((tm, LANES), F32)
    for k, v in enumerate((id1, id2, w1, w2, rank1, rank2)):
        slab = jnp.where(lane == k, v, slab)
    slab_ref[...] = slab


def _post(x_args, mixed_p, mixed_s, mods, layer, norm_w, w_out_bf16, wr_hi, wr_lo,
          tiles_per_sample):
    split_x = len(x_args) == 2
    tp, ts = mixed_p.shape[0], mixed_s.shape[0]
    t, d = tp + ts, D_MODEL
    npt, nst = tp // ROW_TILE, ts // ROW_TILE
    tile = lambda i: (i, 0)
    if split_x:
        x_specs = [pl.BlockSpec((ROW_TILE, d), lambda i: (jnp.minimum(i, npt - 1), 0)),
                   pl.BlockSpec((ROW_TILE, d), lambda i: (jnp.maximum(i - npt, 0), 0))]
    else:
        x_specs = [pl.BlockSpec((ROW_TILE, d), tile)]
    in_specs = x_specs + [
        pl.BlockSpec((ROW_TILE, d), lambda i: (jnp.minimum(i, npt - 1), 0)),
        pl.BlockSpec((ROW_TILE, d), lambda i: (jnp.maximum(i - npt, 0), 0)),
        pl.BlockSpec((1, 6, d), lambda i: (_mod_row(i, layer, npt, tiles_per_sample), 0, 0)),
        pl.BlockSpec((1, d), lambda i: (0, 0)),
        pl.BlockSpec((d, d), lambda i: (0, 0)),
        pl.BlockSpec((d, LANES), lambda i: (0, 0)),
        pl.BlockSpec((d, LANES), lambda i: (0, 0))]
    return pl.pallas_call(
        functools.partial(_post_kernel, split_x=split_x, n_prompt_tiles=npt),
        grid=(npt + nst,),
        in_specs=in_specs,
        out_specs=(pl.BlockSpec((ROW_TILE, d), tile), pl.BlockSpec((ROW_TILE, d), tile),
                   pl.BlockSpec((ROW_TILE, LANES), tile), pl.BlockSpec((1, LANES), lambda i: (0, 0))),
        out_shape=(jax.ShapeDtypeStruct((t, d), F32), jax.ShapeDtypeStruct((t, d), F32),
                   jax.ShapeDtypeStruct((t, LANES), F32), jax.ShapeDtypeStruct((1, LANES), F32)),
        scratch_shapes=[pltpu.VMEM((1, LANES), F32)],
        compiler_params=_params(1),
        name=f"post{layer}",
    )(*x_args, mixed_p, mixed_s, mods, norm_w, w_out_bf16, wr_hi, wr_lo)


def _dispatch_kernel(off_ref, zero_ref, route_ref, h2_ref, hs_ref, pos_ref, zero_buf, sem):
    j = pl.program_id(0)
    td = DISPATCH_TILE

    @pl.when(j == 0)
    def _():
        zero_buf[...] = jnp.zeros_like(zero_buf)

        def zero_copy(k):
            start = pl.multiple_of(zero_ref[k], EXPERT_TILE)
            return pltpu.make_async_copy(zero_buf, hs_ref.at[pl.ds(start, EXPERT_TILE)], sem)

        def start_zero(k, c):
            @pl.when(zero_ref[k] >= 0)
            def _():
                zero_copy(k).start()
            return c

        def wait_zero(k, c):
            @pl.when(zero_ref[k] >= 0)
            def _():
                zero_copy(k).wait()
            return c

        lax.fori_loop(0, 2 * N_EXPERTS, start_zero, 0)
        lax.fori_loop(0, 2 * N_EXPERTS, wait_zero, 0)

    def issue(r, c):
        for s in range(2):
            p = off_ref[route_ref[0, s, r]] + route_ref[0, 2 + s, r]
            pos_ref[0, s, r] = p
            pltpu.make_async_copy(h2_ref.at[pl.ds(r, 1)], hs_ref.at[pl.ds(p, 1)], sem).start()
        return c

    lax.fori_loop(0, td, issue, 0, unroll=8)
    for s in range(2):
        pltpu.make_async_copy(h2_ref, hs_ref.at[pl.ds(0, td)], sem).wait()


def _dispatch(offsets, zero_tiles, route, h2, n_rows):
    t, d = h2.shape
    nt = t // DISPATCH_TILE
    grid_spec = pltpu.PrefetchScalarGridSpec(
        num_scalar_prefetch=2,
        grid=(nt,),
        in_specs=[pl.BlockSpec((1, 4, DISPATCH_TILE), lambda j, *_: (j, 0, 0),
                               memory_space=pltpu.SMEM),
                  pl.BlockSpec((DISPATCH_TILE, d), lambda j, *_: (j, 0))],
        out_specs=(pl.BlockSpec(memory_space=pl.ANY),
                   pl.BlockSpec((1, 2, DISPATCH_TILE), lambda j, *_: (j, 0, 0),
                                memory_space=pltpu.SMEM)),
        scratch_shapes=[pltpu.VMEM((EXPERT_TILE, d), F32), pltpu.SemaphoreType.DMA(())])
    return pl.pallas_call(
        _dispatch_kernel,
        grid_spec=grid_spec,
        out_shape=(jax.ShapeDtypeStruct((n_rows, d), F32),
                   jax.ShapeDtypeStruct((nt, 2, DISPATCH_TILE), I32)),
        compiler_params=_params(1),
        name="dispatch",
    )(offsets, zero_tiles, route, h2)


def _expert_kernel(te_ref, src_ref, nv_ref, hs_ref, w1_ref, w3_ref, w2_ref, ys_ref,
                   w1b, w3b, w2b):
    i = pl.program_id(0)
    prev = te_ref[jnp.maximum(i - 1, 0)]

    @pl.when((i == 0) | (te_ref[i] != prev))
    def _():
        w1b[...] = w1_ref[0, 0].astype(BF16)
        w3b[...] = w3_ref[0, 0].astype(BF16)
        w2b[...] = w2_ref[0, 0].astype(BF16)

    @pl.when(nv_ref[i] > 0)
    def _():
        h = hs_ref[...].astype(BF16)
        g = _silu(_dot(h, w1b[...])) * _dot(h, w3b[...])
        ys_ref[...] = _dot(g.astype(BF16), w2b[...])

    @pl.when(nv_ref[i] == 0)
    def _():
        ys_ref[...] = jnp.zeros_like(ys_ref)


def _experts(tile_expert, tile_src, tile_rows, hs, w1, w3, w2, layer):
    n_rows, d = hs.shape
    nt = n_rows // EXPERT_TILE
    hid = w1.shape[-1]
    row_map = lambda i, te, src, nv: (src[i], 0)
    grid_spec = pltpu.PrefetchScalarGridSpec(
        num_scalar_prefetch=3,
        grid=(nt,),
        in_specs=[pl.BlockSpec((EXPERT_TILE, d), row_map),
                  pl.BlockSpec((1, 1, d, hid), lambda i, te, src, nv: (layer, te[i], 0, 0)),
                  pl.BlockSpec((1, 1, d, hid), lambda i, te, src, nv: (layer, te[i], 0, 0)),
                  pl.BlockSpec((1, 1, hid, d), lambda i, te, src, nv: (layer, te[i], 0, 0))],
        out_specs=pl.BlockSpec((EXPERT_TILE, d), lambda i, te, src, nv: (i, 0)),
        scratch_shapes=[pltpu.VMEM((d, hid), BF16), pltpu.VMEM((d, hid), BF16),
                        pltpu.VMEM((hid, d), BF16)])
    return pl.pallas_call(
        _expert_kernel,
        grid_spec=grid_spec,
        out_shape=jax.ShapeDtypeStruct((n_rows, d), F32),
        compiler_params=_params(1),
        name=f"experts{layer}",
    )(tile_expert, tile_src, tile_rows, hs, w1, w3, w2)


def _combine_kernel(pos_ref, x1_ref, slab_ref, mod_ref, fw_ref, ys_ref, out_ref, ybuf, sem,
                    *, final_norm):
    tm = ROW_TILE

    def issue(r, c):
        for s in range(2):
            pltpu.make_async_copy(ys_ref.at[pl.ds(pos_ref[0, s, r], 1)],
                                  ybuf.at[pl.ds(s * tm + r, 1)], sem).start()
        return c

    lax.fori_loop(0, tm, issue, 0, unroll=8)
    pltpu.make_async_copy(ys_ref.at[pl.ds(0, 2 * tm)], ybuf, sem).wait()
    y = slab_ref[:, 2:3] * ybuf[0:tm, :] + slab_ref[:, 3:4] * ybuf[tm:2 * tm, :]
    x2 = x1_ref[...] + mod_ref[0, 5:6, :] * y
    if final_norm:
        x2 = _rms(x2) * fw_ref[...]
    out_ref[...] = x2


def _combine(pos, x1, slab, mods, layer, final_w, ys, tile0, n_tiles, n_prompt_tiles,
             tiles_per_sample, final_norm):
    d = D_MODEL
    tile = lambda i: (tile0 + i, 0)
    mod_map = lambda i: (_mod_row(tile0 + i, layer, n_prompt_tiles, tiles_per_sample), 0, 0)
    return pl.pallas_call(
        functools.partial(_combine_kernel, final_norm=final_norm),
        grid=(n_tiles,),
        in_specs=[pl.BlockSpec((1, 2, ROW_TILE), lambda i: (tile0 + i, 0, 0),
                               memory_space=pltpu.SMEM),
                  pl.BlockSpec((ROW_TILE, d), tile),
                  pl.BlockSpec((ROW_TILE, LANES), tile),
                  pl.BlockSpec((1, 6, d), mod_map),
                  pl.BlockSpec((1, d), lambda i: (0, 0)),
                  pl.BlockSpec(memory_space=pl.ANY)],
        out_specs=pl.BlockSpec((ROW_TILE, d), lambda i: (i, 0)),
        out_shape=jax.ShapeDtypeStruct((n_tiles * ROW_TILE, d), F32),
        scratch_shapes=[pltpu.VMEM((2 * ROW_TILE, d), F32), pltpu.SemaphoreType.DMA(())],
        compiler_params=_params(1),
        name=f"combine{layer}_{tile0}",
    )(pos, x1, slab, mods, final_w, ys)


def _moe(h2, slab, counts, w1, w3, w2, layer):
    t = h2.shape[0]
    n_rows = 2 * t + N_EXPERTS * EXPERT_TILE
    nt = n_rows // EXPERT_TILE
    cnt = counts[0, :N_EXPERTS].astype(I32)
    padded = ((cnt + EXPERT_TILE - 1) // EXPERT_TILE) * EXPERT_TILE
    ends = jnp.cumsum(padded)
    offsets = ends - padded
    tails = jnp.where(cnt > 0, ends - EXPERT_TILE, -1).astype(I32)
    used = ends[-1] // EXPERT_TILE
    tile_start = jnp.arange(nt, dtype=I32) * EXPERT_TILE
    unused = (used + jnp.arange(N_EXPERTS, dtype=I32)) * EXPERT_TILE
    zero_tiles = jnp.concatenate([tails, jnp.where(unused < n_rows, unused, -1)]).astype(I32)
    tile_src = jnp.minimum(jnp.arange(nt, dtype=I32), used - 1)
    tile_expert = jnp.sum((tile_src * EXPERT_TILE)[:, None] >= ends[None, :], axis=1).astype(I32)
    tile_rows = jnp.where(tile_start < ends[-1],
                          jnp.clip(cnt[tile_expert] - (tile_start - offsets[tile_expert]),
                                   0, EXPERT_TILE), 0).astype(I32)
    route = slab[:, jnp.array([0, 1, 4, 5])].astype(I32)
    route = route.reshape(t // DISPATCH_TILE, DISPATCH_TILE, 4).transpose(0, 2, 1)
    hs, pos = _dispatch(offsets.astype(I32), zero_tiles, route, h2, n_rows)
    ys = _experts(tile_expert, tile_src, tile_rows, hs, w1, w3, w2, layer)
    pos = pos.reshape(t // DISPATCH_TILE, 2, DISPATCH_TILE // ROW_TILE, ROW_TILE)
    pos = pos.transpose(0, 2, 1, 3).reshape(t // ROW_TILE, 2, ROW_TILE)
    return ys, pos


def _rope(x, cos, sin_signed):
    lane = lax.broadcasted_iota(I32, (x.shape[0], LANES), 1)
    low = (lane % 32) < 16
    outs = []
    for j in range(x.shape[1] // LANES):
        xb = x[:, j * LANES:(j + 1) * LANES]
        partner = jnp.where(low, pltpu.roll(xb, LANES - 16, 1), pltpu.roll(xb, 16, 1))
        outs.append(xb * cos + partner * sin_signed)
    return jnp.concatenate(outs, axis=1)


def _inproj1_prompt_kernel(x_ref, mod_ref, nw_ref, w_ref, q_ref, k_ref, v_ref):
    d = D_MODEL
    h = _modulate(x_ref[...], nw_ref[...], mod_ref[0, 0:1, :], mod_ref[0, 1:2, :]).astype(BF16)
    q_ref[...] = (_dot(h, w_ref[:, 0:d]) * (DIFF_HD ** -0.5)).astype(BF16)
    k_ref[...] = _dot(h, w_ref[:, d:2 * d])
    v_ref[...] = _dot(h, w_ref[:, 2 * d:3 * d])


def _inproj1_sample_kernel(x_ref, mod_ref, nw_ref, w_ref, cos_ref, sin_ref, q_ref, k_ref, v_ref):
    d = D_MODEL
    h = _modulate(x_ref[...], nw_ref[...], mod_ref[0, 0:1, :], mod_ref[0, 1:2, :]).astype(BF16)
    cos, sin = cos_ref[...], sin_ref[...]
    q_ref[...] = (_rope(_dot(h, w_ref[:, 0:d]), cos, sin) * (DIFF_HD ** -0.5)).astype(BF16)
    k_ref[...] = _rope(_dot(h, w_ref[:, d:2 * d]), cos, sin).astype(BF16)
    v_ref[...] = _dot(h, w_ref[:, 2 * d:3 * d]).astype(BF16)


def _inproj1(x, mods, norm_w, w_bf16, n_prompt_tiles, n_sample_tiles, tiles_per_sample,
             cos_t, sin_t):
    d = D_MODEL
    npt, nst = n_prompt_tiles, n_sample_tiles
    common = [pl.BlockSpec((1, d), lambda i: (0, 0)), pl.BlockSpec((d, 3 * d), lambda i: (0, 0))]
    tile = lambda i: (i, 0)
    out_specs = tuple(pl.BlockSpec((ROW_TILE, d), tile) for _ in range(3))
    qp, kp, vp = pl.pallas_call(
        _inproj1_prompt_kernel,
        grid=(npt,),
        in_specs=[pl.BlockSpec((ROW_TILE, d), tile),
                  pl.BlockSpec((1, 6, d), lambda i: (8, 0, 0))] + common,
        out_specs=out_specs,
        out_shape=(jax.ShapeDtypeStruct((npt * ROW_TILE, d), BF16),
                   jax.ShapeDtypeStruct((npt * ROW_TILE, d), F32),
                   jax.ShapeDtypeStruct((npt * ROW_TILE, d), F32)),
        compiler_params=_params(1),
        name="inproj1_prompt",
    )(x, mods, norm_w, w_bf16)
    rope_tile = lambda i: (i % tiles_per_sample, 0)
    qs, ks, vs = pl.pallas_call(
        _inproj1_sample_kernel,
        grid=(nst,),
        in_specs=[pl.BlockSpec((ROW_TILE, d), lambda i: (npt + i, 0)),
                  pl.BlockSpec((1, 6, d), lambda i: (8 + 1 + i // tiles_per_sample, 0, 0))]
        + common + [pl.BlockSpec((ROW_TILE, LANES), rope_tile),
                    pl.BlockSpec((ROW_TILE, LANES), rope_tile)],
        out_specs=out_specs,
        out_shape=tuple(jax.ShapeDtypeStruct((nst * ROW_TILE, d), BF16) for _ in range(3)),
        compiler_params=_params(1),
        name="inproj1_sample",
    )(x, mods, norm_w, w_bf16, cos_t, sin_t)
    return (qp, kp, vp), (qs, ks, vs)


def _rope_tables(n_tok):
    half = DIFF_HD // 4
    pos = np.arange(n_tok)
    lane = np.arange(LANES)
    sub = lane % DIFF_HD
    p = np.where(sub[None, :] < DIFF_HD // 2, (pos // GRID_W)[:, None], (pos % GRID_W)[:, None])
    inv = jnp.asarray(ROPE_THETA, F32) ** (-jnp.asarray(sub % half, F32) / half)
    ang = jnp.asarray(p, F32) * inv[None, :]
    sign = np.where((lane % (2 * half)) < half, -1.0, 1.0).astype(np.float32)
    return jnp.cos(ang), jnp.sin(ang) * sign[None, :]


def _diffattn_kernel(*refs, has_cache, lam_init):
    if has_cache:
        q_ref, k_ref, v_ref, ck_ref, cv_ref, lam_ref, sw_ref, o_ref = refs
    else:
        q_ref, k_ref, v_ref, lam_ref, sw_ref, o_ref = refs
    lv = lam_ref[...]
    lam = (jnp.exp(jnp.sum(lv[0:1] * lv[1:2], axis=1, keepdims=True))
           - jnp.exp(jnp.sum(lv[2:3] * lv[3:4], axis=1, keepdims=True)) + lam_init)
    q = q_ref[...]
    lane = lax.broadcasted_iota(I32, q.shape, 1)
    zero = jnp.zeros_like(q)
    keys = [(k_ref[...].astype(BF16), v_ref[...].astype(BF16))]
    if has_cache:
        keys.append((ck_ref[...].astype(BF16), cv_ref[...].astype(BF16)))
    o = None
    for c in range(2):
        qc = jnp.where((lane < DIFF_HD) == (c == 0), q, zero)
        s = [_dot_nt(qc, k) for k, _ in keys]
        mx = functools.reduce(jnp.maximum, [jnp.max(si, axis=1, keepdims=True) for si in s])
        e = [jnp.exp(si - mx) for si in s]
        z = functools.reduce(jnp.add, [jnp.sum(ei, axis=1, keepdims=True) for ei in e])
        pv = functools.reduce(jnp.add, [_dot(ei.astype(BF16), v) for ei, (_, v) in zip(e, keys)])
        pv = pv * (1.0 / z)
        o = pv if c == 0 else o - lam * pv
    o_ref[...] = ((_rms(o) * sw_ref[...]) * (1.0 - lam_init)).astype(BF16)


def _diffattn(q, k, v, lam_vecs, subln_w, batch, seq_len, q_block, lam_init, cache=None):
    hd2 = 2 * DIFF_HD
    nq = seq_len // q_block
    has_cache = cache is not None
    kv_spec = pl.BlockSpec((seq_len, hd2), lambda b, h, qi: (b, h))
    in_specs = [pl.BlockSpec((q_block, hd2), lambda b, h, qi: (b * nq + qi, h)), kv_spec, kv_spec]
    args = [q, k, v]
    if has_cache:
        past = cache[0].shape[0] // batch
        c_spec = pl.BlockSpec((past, hd2), lambda b, h, qi: (b, h))
        in_specs += [c_spec, c_spec]
        args += list(cache)
    in_specs += [pl.BlockSpec((4, DIFF_HD), lambda b, h, qi: (0, 0)),
                 pl.BlockSpec((1, hd2), lambda b, h, qi: (0, 0))]
    args += [lam_vecs, subln_w]
    return pl.pallas_call(
        functools.partial(_diffattn_kernel, has_cache=has_cache, lam_init=lam_init),
        grid=(batch, DIFF_HEADS, nq),
        in_specs=in_specs,
        out_specs=pl.BlockSpec((q_block, hd2), lambda b, h, qi: (b * nq + qi, h)),
        out_shape=jax.ShapeDtypeStruct((batch * seq_len, D_MODEL), BF16),
        compiler_params=_params(3),
        name="diffattn_cache" if has_cache else "diffattn",
    )(*args)


def _router_weights(router_group, router_expert):
    w = jnp.concatenate([router_group, router_expert], axis=1)
    w = jnp.pad(w, ((0, 0), (0, LANES - w.shape[1])))
    hi = w.astype(BF16)
    return hi, (w - hi.astype(F32)).astype(BF16)


def _inproj0_weights(w_in):
    gq, gk, gv, gg, gaf, gab, hq, hff, hfb, hi, hg = jnp.split(
        w_in, [256, 512, 1024, 1536, 1552, 1568, 1824, 2080, 2336, 2848], axis=1)
    w = jnp.concatenate([gq, gk, gv, gg, hq, hff, hfb, hi, hg, gaf, gab], axis=1)
    return jnp.pad(w, ((0, 0), (0, AB_COLS - w.shape[1]))).astype(BF16)


def kernel(x_prompt, x_sample, state_gla, state_hgrn, cache_diff_k, cache_diff_v, c, c_ctx,
           w_ada, b_ada, norm1_w, norm2_w, w_in_ab, gla_a2, gla_a_bias, hgrn_lb, gla_onorm_w,
           hgrn_onorm_w, w_out_ab, w_in_c, lam_q1, lam_k1, lam_q2, lam_k2, diff_subln_w, w_out_c,
           router_group, router_expert, moe_w1, moe_w3, moe_w2, final_norm_w):
    bp, lp, d = x_prompt.shape
    bs, ls, _ = x_sample.shape
    depth = w_ada.shape[0]
    assert depth == 2 and d == D_MODEL and bs <= 7
    tp, ts = bp * lp, bs * ls
    npt, nst = tp // ROW_TILE, ts // ROW_TILE
    tps = ls // ROW_TILE
    xp = x_prompt.reshape(tp, d)
    xs = x_sample.reshape(ts, d)

    cond8 = jnp.concatenate([c_ctx[None, :], c, jnp.zeros((7 - bs, d), F32)], axis=0)
    mods = _adaln(cond8, w_ada, b_ada).reshape(depth * 8, 6, d)

    proj = _inproj0(xp, xs, mods, norm1_w[0:1], _inproj0_weights(w_in_ab[0]), tps)
    a_bias = gla_a_bias[0][:, None, :]
    scan_args = (gla_a2[0], a_bias, hgrn_lb, gla_onorm_w[0:1], hgrn_onorm_w[0:1])
    mixed_p, s_fin = _scan(proj, 0, bp, lp, *scan_args)
    s0 = jnp.concatenate([state_gla[:, 0], state_hgrn[:, 0]], axis=2).swapaxes(-1, -2)
    mixed_s = _scan(proj, tp, bs, ls, *scan_args, s0=s0)
    s_fin = s_fin.swapaxes(-1, -2)
    new_state_gla = s_fin[:, None, :, :GLA_HEADS]
    new_state_hgrn = s_fin[:, None, :, GLA_HEADS:]

    wr = _router_weights(router_group[0], router_expert[0])
    x1, h2, slab, counts = _post((xp, xs), mixed_p, mixed_s, mods, 0, norm2_w[0:1],
                                 w_out_ab[0].astype(BF16), *wr, tps)
    ys, pos = _moe(h2, slab, counts, moe_w1, moe_w3, moe_w2, 0)
    x2 = _combine(pos, x1, slab, mods, 0, final_norm_w[None, :], ys, 0, npt + nst, npt, tps, False)

    lam_init = 0.8 - 0.6 * math.exp(-0.3 * 1)
    cos_t, sin_t = _rope_tables(ls)
    (qp, kp, vp), (qs, ks, vs) = _inproj1(x2, mods, norm1_w[1:2], w_in_c[0].astype(BF16),
                                          npt, nst, tps, cos_t, sin_t)
    lam_vecs = jnp.stack([lam_q1[0], lam_k1[0], lam_q2[0], lam_k2[0]])
    att_p = _diffattn(qp, kp, vp, lam_vecs, diff_subln_w[0:1], bp, lp, lp, lam_init)
    past = cache_diff_k.shape[2]
    cache = (cache_diff_k[:, 0].reshape(bs * past, d), cache_diff_v[:, 0].reshape(bs * past, d))
    att_s = _diffattn(qs, ks, vs, lam_vecs, diff_subln_w[0:1], bs, ls, ROW_TILE, lam_init, cache)

    wr = _router_weights(router_group[1], router_expert[1])
    x3, h2, slab, counts = _post((x2,), att_p, att_s, mods, 1, norm2_w[1:2],
                                 w_out_c[0].astype(BF16), *wr, tps)
    ys, pos = _moe(h2, slab, counts, moe_w1, moe_w3, moe_w2, 1)
    fw = final_norm_w[None, :]
    y_p = _combine(pos, x3, slab, mods, 1, fw, ys, 0, npt, npt, tps, True)
    y_s = _combine(pos, x3, slab, mods, 1, fw, ys, npt, nst, npt, tps, True)

    return (y_p.reshape(bp, lp, d), y_s.reshape(bs, ls, d), new_state_gla, new_state_hgrn,
            kp.reshape(bp, 1, lp, DIFF_HEADS, 2, DIFF_HD),
            vp.reshape(bp, 1, lp, DIFF_HEADS, 2 * DIFF_HD))
```

```python
import functools
import math

import jax
import jax.numpy as jnp
import numpy as np
from jax import lax
from jax.experimental import pallas as pl
from jax.experimental.pallas import tpu as pltpu

F32 = jnp.float32
BF16 = jnp.bfloat16
I32 = jnp.int32

D_MODEL = 1024
GLA_HEADS = 4
HGRN_HEADS = 4
SCAN_HEADS = GLA_HEADS + HGRN_HEADS
SCAN_PAIRS = SCAN_HEADS // 2
HEAD_DK = 64
HEAD_DV = 128
GATE_RANK = 16
GLA_GATE_NORM = 16.0
DIFF_HEADS = 8
DIFF_HD = 64
GRID_W = 64
ROPE_THETA = 10000.0
N_GROUPS = 4
EXPERTS_PER_GROUP = 8
N_EXPERTS = N_GROUPS * EXPERTS_PER_GROUP
MOE_HIDDEN = 512
EPS = 1e-6
LANES = 128
NEG_BIG = -1e30

ROW_TILE = 256
SCAN_CHUNK = 64
EXPERT_TILE = 256
DISPATCH_TILE = 512
VMEM_LIMIT = 56 * 1024 * 1024

_C_GQ, _C_GK, _C_GV, _C_GG = 0, 256, 512, 1024
_C_HQ, _C_HFF, _C_HFB, _C_HI, _C_HG = 1536, 1792, 2048, 2304, 2816
_C_GAF, _C_GAB = 3328, 3344
AB_COLS = 3456


def _params(n_axes, vmem=VMEM_LIMIT):
    return pltpu.CompilerParams(dimension_semantics=("arbitrary",) * n_axes,
                                vmem_limit_bytes=vmem)


def _dot(a, b):
    return jnp.dot(a, b, preferred_element_type=F32)


def _dot_nt(a, b):
    return lax.dot_general(a, b, (((1,), (1,)), ((), ())), preferred_element_type=F32)


def _dot_tn(a, b):
    return lax.dot_general(a, b, (((0,), (0,)), ((), ())), preferred_element_type=F32)


def _split_bf16(x):
    hi = x.astype(BF16)
    lo = (x - hi.astype(F32)).astype(BF16)
    return hi, lo


def _silu(x):
    return x * jax.nn.sigmoid(x)


def _log_sigmoid(x):
    return jnp.minimum(x, 0.0) - jnp.log(1.0 + jnp.exp(-jnp.abs(x)))


def _rms(x):
    return x * lax.rsqrt(jnp.mean(x * x, axis=-1, keepdims=True) + EPS)


def _modulate(x, norm_w, shift, scale):
    return (_rms(x) * norm_w) * (1.0 + scale) + shift


def _ada_kernel(c_ref, w_ref, b_ref, o_ref):
    s = _silu(c_ref[...])
    o_ref[0] = _dot(s.astype(BF16), w_ref[0].astype(BF16)) + b_ref[0]


def _adaln(cond8, w_ada, b_ada):
    depth, d, n = w_ada.shape
    tn = 1536
    return pl.pallas_call(
        _ada_kernel,
        grid=(depth, n // tn),
        in_specs=[pl.BlockSpec((8, d), lambda l, j: (0, 0)),
                  pl.BlockSpec((1, d, tn), lambda l, j: (l, 0, j)),
                  pl.BlockSpec((1, 1, tn), lambda l, j: (l, 0, j))],
        out_specs=pl.BlockSpec((1, 8, tn), lambda l, j: (l, 0, j)),
        out_shape=jax.ShapeDtypeStruct((depth, 8, n), F32),
        compiler_params=_params(2),
        name="adaln",
    )(cond8, w_ada, b_ada.reshape(depth, 1, n))


def _mod_row(i, layer, n_prompt_tiles, tiles_per_sample):
    r = jnp.where(i < n_prompt_tiles, 0, 1 + (i - n_prompt_tiles) // tiles_per_sample)
    return layer * 8 + r


def _inproj0_kernel(xp_ref, xs_ref, mod_ref, nw_ref, w_ref, o_ref, *, n_prompt_tiles):
    i = pl.program_id(0)
    x = jnp.where(i < n_prompt_tiles, xp_ref[...], xs_ref[...])
    h = _modulate(x, nw_ref[...], mod_ref[0, 0:1, :], mod_ref[0, 1:2, :])
    o_ref[...] = _dot(h.astype(BF16), w_ref[...])


def _inproj0(xp, xs, mods, norm_w, w_bf16, tiles_per_sample):
    tp, d = xp.shape
    ts = xs.shape[0]
    n = w_bf16.shape[1]
    npt, nst = tp // ROW_TILE, ts // ROW_TILE
    mod_map = lambda i: (_mod_row(i, 0, npt, tiles_per_sample), 0, 0)
    return pl.pallas_call(
        functools.partial(_inproj0_kernel, n_prompt_tiles=npt),
        grid=(npt + nst,),
        in_specs=[pl.BlockSpec((ROW_TILE, d), lambda i: (jnp.minimum(i, npt - 1), 0)),
                  pl.BlockSpec((ROW_TILE, d), lambda i: (jnp.maximum(i - npt, 0), 0)),
                  pl.BlockSpec((1, 6, d), mod_map),
                  pl.BlockSpec((1, d), lambda i: (0, 0)),
                  pl.BlockSpec((d, n), lambda i: (0, 0))],
        out_specs=pl.BlockSpec((ROW_TILE, n), lambda i: (i, 0)),
        out_shape=jax.ShapeDtypeStruct((tp + ts, n), F32),
        compiler_params=_params(1),
        name="inproj0",
    )(xp, xs, mods, norm_w, w_bf16)


def _scan_kernel(*refs, seq_len, has_state):
    if has_state:
        (p_ref, a2_ref, ab_ref, lb_ref, ong_ref, onh_ref, s0_ref,
         mixed_ref, qf, kf, qb, kb, vv, dec_f, dec_b, o_scr, st_f, st_b) = refs
        sfin_ref = None
    else:
        (p_ref, a2_ref, ab_ref, lb_ref, ong_ref, onh_ref,
         mixed_ref, sfin_ref, qf, kf, qb, kb, vv, dec_f, dec_b, o_scr, st_f, st_b) = refs
        s0_ref = None
    C = SCAN_CHUNK
    n_chunks = seq_len // C
    gqk = GLA_HEADS * HEAD_DK

    row = lax.broadcasted_iota(I32, (C, C), 0)
    col = lax.broadcasted_iota(I32, (C, C), 1)
    lower = col <= row
    upper = col >= row
    tri_lo = jnp.where(lower, 1.0, 0.0).astype(BF16)
    tri_up = jnp.where(upper, 1.0, 0.0).astype(BF16)

    lbp = lb_ref[...]
    lb_max = jnp.maximum(lbp[0], lbp[1])
    lb_e0 = jnp.exp(lbp[0] - lb_max)
    lb_e1 = jnp.exp(lbp[1] - lb_max)
    lb = lb_e0 / (lb_e0 + lb_e1)

    def cumsum_chunk(tri, la):
        hi, lo = _split_bf16(la)
        return _dot(tri, hi) + _dot(tri, lo)

    def prep(n, carry):
        r0 = pl.multiple_of(n * C, C)
        rows = pl.ds(r0, C)
        gq = p_ref[rows, _C_GQ:_C_GQ + gqk] * (HEAD_DK ** -0.5)
        gk = p_ref[rows, _C_GK:_C_GK + gqk]
        hq = _silu(p_ref[rows, _C_HQ:_C_HQ + gqk]) * (HEAD_DK ** -0.5)
        for d_i, (q_s, k_s, dec_s, tri, last) in enumerate(
                ((qf, kf, dec_f, tri_lo, C - 1), (qb, kb, dec_b, tri_up, 0))):
            c_ga = _C_GAF if d_i == 0 else _C_GAB
            c_hf = _C_HFF if d_i == 0 else _C_HFB
            ga = p_ref[rows, c_ga:c_ga + GATE_RANK]
            xg = _dot(ga.astype(BF16), a2_ref[d_i].astype(BF16)) + ab_ref[d_i]
            la_g = _log_sigmoid(xg) / GLA_GATE_NORM
            f = lb[d_i:d_i + 1, :] + (1.0 - lb[d_i:d_i + 1, :]) * jax.nn.sigmoid(
                p_ref[rows, c_hf:c_hf + gqk])
            la_h = jnp.log(f)
            for q, k, la, c0 in ((gq, gk, la_g, 0), (hq, 1.0 - f, la_h, gqk)):
                b = cumsum_chunk(tri, la)
                q_s[rows, c0:c0 + gqk] = (q * jnp.exp(b)).astype(BF16)
                k_s[rows, c0:c0 + gqk] = (k * jnp.exp(-b)).astype(BF16)
                dec_s[n, :, c0:c0 + gqk] = jnp.exp(b[last:last + 1, :])
        vv[rows, 0:512] = p_ref[rows, _C_GV:_C_GV + 512].astype(BF16)
        vv[rows, 512:1024] = p_ref[rows, _C_HI:_C_HI + 512].astype(BF16)
        return carry

    lax.fori_loop(0, n_chunks, prep, 0)

    o_scr[...] = jnp.zeros_like(o_scr)
    for p in range(SCAN_PAIRS):
        if has_state:
            st_f[p] = s0_ref[0, 0, p]
            st_b[p] = s0_ref[0, 1, p]
        else:
            st_f[p] = jnp.zeros((2 * HEAD_DV, 2 * HEAD_DK), F32)
            st_b[p] = jnp.zeros((2 * HEAD_DV, 2 * HEAD_DK), F32)

    first_head = lax.broadcasted_iota(I32, (C, 2 * HEAD_DK), 1) < HEAD_DK
    row2 = lax.broadcasted_iota(I32, (2 * C, C), 0) % C
    col2 = lax.broadcasted_iota(I32, (2 * C, C), 1)
    lower2 = col2 <= row2
    upper2 = col2 >= row2

    def per_head_rows(x):
        z = jnp.zeros_like(x)
        return jnp.concatenate([jnp.where(first_head, x, z), jnp.where(first_head, z, x)], axis=0)

    def add_out(rows, p, res):
        c0 = p * 2 * HEAD_DV
        o_scr[rows, c0:c0 + HEAD_DV] += res[0:C, 0:HEAD_DV]
        o_scr[rows, c0 + HEAD_DV:c0 + 2 * HEAD_DV] += res[C:2 * C, HEAD_DV:2 * HEAD_DV]

    def sweep(n, carry):
        m = n_chunks - 1 - n
        rows = pl.ds(pl.multiple_of(n * C, C), C)
        rows_m = pl.ds(pl.multiple_of(m * C, C), C)
        decay_f, decay_b = dec_f[n], dec_b[m]
        for p in range(SCAN_PAIRS):
            ks = slice(p * 2 * HEAD_DK, (p + 1) * 2 * HEAD_DK)
            vs = slice(p * 2 * HEAD_DV, (p + 1) * 2 * HEAD_DV)
            qd, kd, vh = per_head_rows(qf[rows, ks]), kf[rows, ks], vv[rows, vs]
            s_f = st_f[p]
            sc = (jnp.where(lower2, _dot_nt(qd, kd), 0.0)
                  + jnp.where(upper2, _dot_nt(per_head_rows(qb[rows, ks]), kb[rows, ks]), 0.0))
            add_out(rows, p, _dot_nt(qd, s_f.astype(BF16)) + _dot(sc.astype(BF16), vh))
            st_f[p] = decay_f[:, ks] * (s_f + _dot_tn(vh, kd))
            s_b = st_b[p]
            vm, kbm = vv[rows_m, vs], kb[rows_m, ks]
            add_out(rows_m, p, _dot_nt(per_head_rows(qb[rows_m, ks]), s_b.astype(BF16)))
            st_b[p] = decay_b[:, ks] * (s_b + _dot_tn(vm, kbm))
        return carry

    lax.fori_loop(0, n_chunks, sweep, 0)

    def finish(n, carry):
        rows = pl.ds(pl.multiple_of(n * C, C), C)
        for h in range(SCAN_HEADS):
            vs = slice(h * HEAD_DV, (h + 1) * HEAD_DV)
            if h < GLA_HEADS:
                gate = p_ref[rows, _C_GG + h * HEAD_DV:_C_GG + (h + 1) * HEAD_DV]
                onw = ong_ref[...]
            else:
                hh = h - GLA_HEADS
                gate = p_ref[rows, _C_HG + hh * HEAD_DV:_C_HG + (hh + 1) * HEAD_DV]
                onw = onh_ref[...]
            mixed_ref[rows, vs] = ((_rms(o_scr[rows, vs]) * onw) * _silu(gate)).astype(BF16)
        return carry

    lax.fori_loop(0, n_chunks, finish, 0)

    if sfin_ref is not None:
        for d_i, st in enumerate((st_f, st_b)):
            for p in range(SCAN_PAIRS):
                sfin_ref[0, d_i, 2 * p] = st[p, 0:HEAD_DV, 0:HEAD_DK]
                sfin_ref[0, d_i, 2 * p + 1] = st[p, HEAD_DV:2 * HEAD_DV, HEAD_DK:2 * HEAD_DK]


def _scan(p, row0, batch, seq_len, a2, a_bias, lb, onorm_g, onorm_h, s0=None):
    n = p.shape[1]
    assert row0 % seq_len == 0
    blk0 = row0 // seq_len
    has_state = s0 is not None
    n_chunks = seq_len // SCAN_CHUNK
    st_shape = (1, 2, SCAN_HEADS, HEAD_DV, HEAD_DK)
    pair_shape = (SCAN_PAIRS, 2 * HEAD_DV, 2 * HEAD_DK)
    in_specs = [pl.BlockSpec((seq_len, n), lambda b: (blk0 + b, 0), pipeline_mode=pl.Buffered(1)),
                pl.BlockSpec(a2.shape, lambda b: (0, 0, 0)),
                pl.BlockSpec(a_bias.shape, lambda b: (0, 0, 0)),
                pl.BlockSpec(lb.shape, lambda b: (0, 0, 0)),
                pl.BlockSpec((1, HEAD_DV), lambda b: (0, 0)),
                pl.BlockSpec((1, HEAD_DV), lambda b: (0, 0))]
    args = [p, a2, a_bias, lb, onorm_g, onorm_h]
    mixed_shape = jax.ShapeDtypeStruct((batch * seq_len, D_MODEL), BF16)
    mixed_spec = pl.BlockSpec((seq_len, D_MODEL), lambda b: (b, 0))
    if has_state:
        in_specs.append(pl.BlockSpec((1, 2) + pair_shape, lambda b: (b, 0, 0, 0, 0)))
        args.append(s0)
        out_shape, out_specs = mixed_shape, mixed_spec
    else:
        out_shape = (mixed_shape, jax.ShapeDtypeStruct((batch,) + st_shape[1:], F32))
        out_specs = (mixed_spec, pl.BlockSpec(st_shape, lambda b: (b, 0, 0, 0, 0)))
    scratch = [pltpu.VMEM((seq_len, 512), BF16) for _ in range(4)]
    scratch += [pltpu.VMEM((seq_len, D_MODEL), BF16),
                pltpu.VMEM((n_chunks, 1, 512), F32), pltpu.VMEM((n_chunks, 1, 512), F32),
                pltpu.VMEM((seq_len, D_MODEL), F32),
                pltpu.VMEM(pair_shape, F32), pltpu.VMEM(pair_shape, F32)]
    return pl.pallas_call(
        functools.partial(_scan_kernel, seq_len=seq_len, has_state=has_state),
        grid=(batch,),
        in_specs=in_specs, out_specs=out_specs, out_shape=out_shape,
        scratch_shapes=scratch,
        compiler_params=_params(1),
        name="scan_state" if has_state else "scan_fresh",
    )(*args)


def _post_kernel(*refs, split_x, n_prompt_tiles):
    if split_x:
        xp_ref, xs_ref = refs[0], refs[1]
        refs = refs[2:]
    else:
        x_ref = refs[0]
        refs = refs[1:]
    (mp_ref, ms_ref, mod_ref, nw_ref, wo_ref, wrh_ref, wrl_ref,
     x1_ref, h2_ref, slab_ref, cnt_ref, carry) = refs
    i = pl.program_id(0)
    is_prompt = i < n_prompt_tiles
    if split_x:
        x = jnp.where(is_prompt, xp_ref[...], xs_ref[...])
    else:
        x = x_ref[...]
    mixed = jnp.where(is_prompt, mp_ref[...], ms_ref[...])
    x1 = x + mod_ref[0, 2:3, :] * _dot(mixed, wo_ref[...])
    x1_ref[...] = x1
    h2 = _modulate(x1, nw_ref[...], mod_ref[0, 3:4, :], mod_ref[0, 4:5, :])
    h2_ref[...] = h2

    hh, hl = _split_bf16(h2)
    logits = _dot(hh, wrh_ref[...]) + _dot(hl, wrh_ref[...]) + _dot(hh, wrl_ref[...])
    tm = logits.shape[0]
    lane = lax.broadcasted_iota(I32, (tm, LANES), 1).astype(F32)

    def first_max(v):
        mx = jnp.max(v, axis=1, keepdims=True)
        idx = jnp.min(jnp.where(v == mx, lane, float(LANES)), axis=1, keepdims=True)
        return mx, idx

    gl = jnp.where(lane < N_GROUPS, logits, NEG_BIG)
    gmax, gidx = first_max(gl)
    g_val = 1.0 / jnp.sum(jnp.exp(gl - gmax), axis=1, keepdims=True)
    lo = N_GROUPS + EXPERTS_PER_GROUP * gidx
    el = jnp.where((lane >= lo) & (lane < lo + EXPERTS_PER_GROUP), logits, NEG_BIG)
    emax, l1 = first_max(el)
    esum = jnp.sum(jnp.exp(el - emax), axis=1, keepdims=True)
    e2max, l2 = first_max(jnp.where(lane == l1, NEG_BIG, el))
    p1 = 1.0 / esum
    p2 = jnp.exp(e2max - emax) / esum
    w1 = g_val * (p1 / (p1 + p2))
    w2 = g_val * (p2 / (p1 + p2))
    id1 = l1 - N_GROUPS
    id2 = l2 - N_GROUPS

    @pl.when(i == 0)
    def _():
        carry[...] = jnp.zeros_like(carry)

    sel1 = lane == id1
    sel2 = lane == id2
    onehot = jnp.where(sel1 | sel2, 1.0, 0.0)
    row = lax.broadcasted_iota(I32, (tm, tm), 0)
    col = lax.broadcasted_iota(I32, (tm, tm), 1)
    earlier = jnp.where(col < row, 1.0, 0.0).astype(BF16)
    before = _dot(earlier, onehot.astype(BF16)) + carry[...]
    rank1 = jnp.sum(jnp.where(sel1, before, 0.0), axis=1, keepdims=True)
    rank2 = jnp.sum(jnp.where(sel2, before, 0.0), axis=1, keepdims=True)
    total = carry[...] + jnp.sum(onehot, axis=0, keepdims=True)
    carry[...] = total
    cnt_ref[...] = total

    slab = jnp.zeros((tm, LANES), F32)
    for k, v in enumerate((id1, id2, w1, w2, rank1, rank2)):
        slab = jnp.where(lane == k, v, slab)
    slab_ref[...] = slab


def _post(x_args, mixed_p, mixed_s, mods, layer, norm_w, w_out_bf16, wr_hi, wr_lo,
          tiles_per_sample):
    split_x = len(x_args) == 2
    tp, ts = mixed_p.shape[0], mixed_s.shape[0]
    t, d = tp + ts, D_MODEL
    npt, nst = tp // ROW_TILE, ts // ROW_TILE
    tile = lambda i: (i, 0)
    if split_x:
        x_specs = [pl.BlockSpec((ROW_TILE, d), lambda i: (jnp.minimum(i, npt - 1), 0)),
                   pl.BlockSpec((ROW_TILE, d), lambda i: (jnp.maximum(i - npt, 0), 0))]
    else:
        x_specs = [pl.BlockSpec((ROW_TILE, d), tile)]
    in_specs = x_specs + [
        pl.BlockSpec((ROW_TILE, d), lambda i: (jnp.minimum(i, npt - 1), 0)),
        pl.BlockSpec((ROW_TILE, d), lambda i: (jnp.maximum(i - npt, 0), 0)),
        pl.BlockSpec((1, 6, d), lambda i: (_mod_row(i, layer, npt, tiles_per_sample), 0, 0)),
        pl.BlockSpec((1, d), lambda i: (0, 0)),
        pl.BlockSpec((d, d), lambda i: (0, 0)),
        pl.BlockSpec((d, LANES), lambda i: (0, 0)),
        pl.BlockSpec((d, LANES), lambda i: (0, 0))]
    return pl.pallas_call(
        functools.partial(_post_kernel, split_x=split_x, n_prompt_tiles=npt),
        grid=(npt + nst,),
        in_specs=in_specs,
        out_specs=(pl.BlockSpec((ROW_TILE, d), tile), pl.BlockSpec((ROW_TILE, d), tile),
                   pl.BlockSpec((ROW_TILE, LANES), tile), pl.BlockSpec((1, LANES), lambda i: (0, 0))),
        out_shape=(jax.ShapeDtypeStruct((t, d), F32), jax.ShapeDtypeStruct((t, d), F32),
                   jax.ShapeDtypeStruct((t, LANES), F32), jax.ShapeDtypeStruct((1, LANES), F32)),
        scratch_shapes=[pltpu.VMEM((1, LANES), F32)],
        compiler_params=_params(1),
        name=f"post{layer}",
    )(*x_args, mixed_p, mixed_s, mods, norm_w, w_out_bf16, wr_hi, wr_lo)


def _dispatch_kernel(zero_ref, pos_ref, h2_ref, hs_ref, zero_buf, sem):
    j = pl.program_id(0)
    td = DISPATCH_TILE

    @pl.when(j == 0)
    def _():
        zero_buf[...] = jnp.zeros_like(zero_buf)

        def zero_copy(k):
            start = pl.multiple_of(zero_ref[k], EXPERT_TILE)
            return pltpu.make_async_copy(zero_buf, hs_ref.at[pl.ds(start, EXPERT_TILE)], sem)

        def start_zero(k, c):
            @pl.when(zero_ref[k] >= 0)
            def _():
                zero_copy(k).start()
            return c

        def wait_zero(k, c):
            @pl.when(zero_ref[k] >= 0)
            def _():
                zero_copy(k).wait()
            return c

        lax.fori_loop(0, 2 * N_EXPERTS, start_zero, 0)
        lax.fori_loop(0, 2 * N_EXPERTS, wait_zero, 0)

    def issue(r, c):
        for s in range(2):
            pltpu.make_async_copy(h2_ref.at[pl.ds(r, 1)],
                                  hs_ref.at[pl.ds(pos_ref[0, s, r], 1)], sem).start()
        return c

    lax.fori_loop(0, td, issue, 0, unroll=8)
    for s in range(2):
        pltpu.make_async_copy(h2_ref, hs_ref.at[pl.ds(0, td)], sem).wait()


def _dispatch(zero_tiles, pos, h2, n_rows):
    t, d = h2.shape
    nt = t // DISPATCH_TILE
    grid_spec = pltpu.PrefetchScalarGridSpec(
        num_scalar_prefetch=1,
        grid=(nt,),
        in_specs=[pl.BlockSpec((1, 2, DISPATCH_TILE), lambda j, *_: (j, 0, 0),
                               memory_space=pltpu.SMEM),
                  pl.BlockSpec((DISPATCH_TILE, d), lambda j, *_: (j, 0))],
        out_specs=pl.BlockSpec(memory_space=pl.ANY),
        scratch_shapes=[pltpu.VMEM((EXPERT_TILE, d), F32), pltpu.SemaphoreType.DMA(())])
    return pl.pallas_call(
        _dispatch_kernel,
        grid_spec=grid_spec,
        out_shape=jax.ShapeDtypeStruct((n_rows, d), F32),
        compiler_params=_params(1),
        name="dispatch",
    )(zero_tiles, pos, h2)


def _expert_kernel(te_ref, src_ref, nv_ref, hs_ref, w1_ref, w3_ref, w2_ref, ys_ref,
                   w1b, w3b, w2b):
    i = pl.program_id(0)
    prev = te_ref[jnp.maximum(i - 1, 0)]

    @pl.when((i == 0) | (te_ref[i] != prev))
    def _():
        w1b[...] = w1_ref[0, 0].astype(BF16)
        w3b[...] = w3_ref[0, 0].astype(BF16)
        w2b[...] = w2_ref[0, 0].astype(BF16)

    @pl.when(nv_ref[i] > 0)
    def _():
        h = hs_ref[...].astype(BF16)
        g = _silu(_dot(h, w1b[...])) * _dot(h, w3b[...])
        ys_ref[...] = _dot(g.astype(BF16), w2b[...])

    @pl.when(nv_ref[i] == 0)
    def _():
        ys_ref[...] = jnp.zeros_like(ys_ref)


def _experts(tile_expert, tile_src, tile_rows, hs, w1, w3, w2, layer):
    n_rows, d = hs.shape
    nt = n_rows // EXPERT_TILE
    hid = w1.shape[-1]
    row_map = lambda i, te, src, nv: (src[i], 0)
    grid_spec = pltpu.PrefetchScalarGridSpec(
        num_scalar_prefetch=3,
        grid=(nt,),
        in_specs=[pl.BlockSpec((EXPERT_TILE, d), row_map),
                  pl.BlockSpec((1, 1, d, hid), lambda i, te, src, nv: (layer, te[i], 0, 0)),
                  pl.BlockSpec((1, 1, d, hid), lambda i, te, src, nv: (layer, te[i], 0, 0)),
                  pl.BlockSpec((1, 1, hid, d), lambda i, te, src, nv: (layer, te[i], 0, 0))],
        out_specs=pl.BlockSpec((EXPERT_TILE, d), lambda i, te, src, nv: (i, 0)),
        scratch_shapes=[pltpu.VMEM((d, hid), BF16), pltpu.VMEM((d, hid), BF16),
                        pltpu.VMEM((hid, d), BF16)])
    return pl.pallas_call(
        _expert_kernel,
        grid_spec=grid_spec,
        out_shape=jax.ShapeDtypeStruct((n_rows, d), F32),
        compiler_params=_params(1),
        name=f"experts{layer}",
    )(tile_expert, tile_src, tile_rows, hs, w1, w3, w2)


def _combine_kernel(pos_ref, x1_ref, slab_ref, mod_ref, fw_ref, ys_ref, out_ref, ybuf, sem,
                    *, final_norm):
    tm = ROW_TILE

    def issue(r, c):
        for s in range(2):
            pltpu.make_async_copy(ys_ref.at[pl.ds(pos_ref[0, s, r], 1)],
                                  ybuf.at[pl.ds(s * tm + r, 1)], sem).start()
        return c

    lax.fori_loop(0, tm, issue, 0, unroll=8)
    pltpu.make_async_copy(ys_ref.at[pl.ds(0, 2 * tm)], ybuf, sem).wait()
    y = slab_ref[:, 2:3] * ybuf[0:tm, :] + slab_ref[:, 3:4] * ybuf[tm:2 * tm, :]
    x2 = x1_ref[...] + mod_ref[0, 5:6, :] * y
    if final_norm:
        x2 = _rms(x2) * fw_ref[...]
    out_ref[...] = x2


def _combine(pos, x1, slab, mods, layer, final_w, ys, tile0, n_tiles, n_prompt_tiles,
             tiles_per_sample, final_norm):
    d = D_MODEL
    tile = lambda i: (tile0 + i, 0)
    mod_map = lambda i: (_mod_row(tile0 + i, layer, n_prompt_tiles, tiles_per_sample), 0, 0)
    return pl.pallas_call(
        functools.partial(_combine_kernel, final_norm=final_norm),
        grid=(n_tiles,),
        in_specs=[pl.BlockSpec((1, 2, ROW_TILE), lambda i: (tile0 + i, 0, 0),
                               memory_space=pltpu.SMEM),
                  pl.BlockSpec((ROW_TILE, d), tile),
                  pl.BlockSpec((ROW_TILE, LANES), tile),
                  pl.BlockSpec((1, 6, d), mod_map),
                  pl.BlockSpec((1, d), lambda i: (0, 0)),
                  pl.BlockSpec(memory_space=pl.ANY)],
        out_specs=pl.BlockSpec((ROW_TILE, d), lambda i: (i, 0)),
        out_shape=jax.ShapeDtypeStruct((n_tiles * ROW_TILE, d), F32),
        scratch_shapes=[pltpu.VMEM((2 * ROW_TILE, d), F32), pltpu.SemaphoreType.DMA(())],
        compiler_params=_params(1),
        name=f"combine{layer}_{tile0}",
    )(pos, x1, slab, mods, final_w, ys)


def _moe(h2, slab, counts, w1, w3, w2, layer):
    t = h2.shape[0]
    n_rows = 2 * t + N_EXPERTS * EXPERT_TILE
    nt = n_rows // EXPERT_TILE
    cnt = counts[0, :N_EXPERTS].astype(I32)
    padded = ((cnt + EXPERT_TILE - 1) // EXPERT_TILE) * EXPERT_TILE
    ends = jnp.cumsum(padded)
    offsets = ends - padded
    tails = jnp.where(cnt > 0, ends - EXPERT_TILE, -1).astype(I32)
    used = ends[-1] // EXPERT_TILE
    tile_start = jnp.arange(nt, dtype=I32) * EXPERT_TILE
    unused = (used + jnp.arange(N_EXPERTS, dtype=I32)) * EXPERT_TILE
    zero_tiles = jnp.concatenate([tails, jnp.where(unused < n_rows, unused, -1)]).astype(I32)
    tile_src = jnp.minimum(jnp.arange(nt, dtype=I32), used - 1)
    tile_expert = jnp.sum((tile_src * EXPERT_TILE)[:, None] >= ends[None, :], axis=1).astype(I32)
    tile_rows = jnp.where(tile_start < ends[-1],
                          jnp.clip(cnt[tile_expert] - (tile_start - offsets[tile_expert]),
                                   0, EXPERT_TILE), 0).astype(I32)
    ids = slab[:, 0:2].astype(I32)
    pos = offsets.astype(I32)[ids] + slab[:, 4:6].astype(I32)
    tiled = lambda tile: pos.reshape(t // tile, tile, 2).transpose(0, 2, 1)
    hs = _dispatch(zero_tiles, tiled(DISPATCH_TILE), h2, n_rows)
    ys = _experts(tile_expert, tile_src, tile_rows, hs, w1, w3, w2, layer)
    return ys, tiled(ROW_TILE)


def _rope(x, cos, sin_signed):
    lane = lax.broadcasted_iota(I32, (x.shape[0], LANES), 1)
    low = (lane % 32) < 16
    outs = []
    for j in range(x.shape[1] // LANES):
        xb = x[:, j * LANES:(j + 1) * LANES]
        partner = jnp.where(low, pltpu.roll(xb, LANES - 16, 1), pltpu.roll(xb, 16, 1))
        outs.append(xb * cos + partner * sin_signed)
    return jnp.concatenate(outs, axis=1)


def _inproj1_prompt_kernel(x_ref, mod_ref, nw_ref, w_ref, q_ref, k_ref, v_ref):
    d = D_MODEL
    h = _modulate(x_ref[...], nw_ref[...], mod_ref[0, 0:1, :], mod_ref[0, 1:2, :]).astype(BF16)
    q_ref[...] = (_dot(h, w_ref[:, 0:d]) * (DIFF_HD ** -0.5)).astype(BF16)
    k_ref[...] = _dot(h, w_ref[:, d:2 * d])
    v_ref[...] = _dot(h, w_ref[:, 2 * d:3 * d])


def _inproj1_sample_kernel(x_ref, mod_ref, nw_ref, w_ref, cos_ref, sin_ref, q_ref, k_ref, v_ref):
    d = D_MODEL
    h = _modulate(x_ref[...], nw_ref[...], mod_ref[0, 0:1, :], mod_ref[0, 1:2, :]).astype(BF16)
    cos, sin = cos_ref[...], sin_ref[...]
    q_ref[...] = (_rope(_dot(h, w_ref[:, 0:d]), cos, sin) * (DIFF_HD ** -0.5)).astype(BF16)
    k_ref[...] = _rope(_dot(h, w_ref[:, d:2 * d]), cos, sin).astype(BF16)
    v_ref[...] = _dot(h, w_ref[:, 2 * d:3 * d]).astype(BF16)


def _inproj1(x, mods, norm_w, w_bf16, n_prompt_tiles, n_sample_tiles, tiles_per_sample,
             cos_t, sin_t):
    d = D_MODEL
    npt, nst = n_prompt_tiles, n_sample_tiles
    common = [pl.BlockSpec((1, d), lambda i: (0, 0)), pl.BlockSpec((d, 3 * d), lambda i: (0, 0))]
    tile = lambda i: (i, 0)
    out_specs = tuple(pl.BlockSpec((ROW_TILE, d), tile) for _ in range(3))
    qp, kp, vp = pl.pallas_call(
        _inproj1_prompt_kernel,
        grid=(npt,),
        in_specs=[pl.BlockSpec((ROW_TILE, d), tile),
                  pl.BlockSpec((1, 6, d), lambda i: (8, 0, 0))] + common,
        out_specs=out_specs,
        out_shape=(jax.ShapeDtypeStruct((npt * ROW_TILE, d), BF16),
                   jax.ShapeDtypeStruct((npt * ROW_TILE, d), F32),
                   jax.ShapeDtypeStruct((npt * ROW_TILE, d), F32)),
        compiler_params=_params(1),
        name="inproj1_prompt",
    )(x, mods, norm_w, w_bf16)
    rope_tile = lambda i: (i % tiles_per_sample, 0)
    qs, ks, vs = pl.pallas_call(
        _inproj1_sample_kernel,
        grid=(nst,),
        in_specs=[pl.BlockSpec((ROW_TILE, d), lambda i: (npt + i, 0)),
                  pl.BlockSpec((1, 6, d), lambda i: (8 + 1 + i // tiles_per_sample, 0, 0))]
        + common + [pl.BlockSpec((ROW_TILE, LANES), rope_tile),
                    pl.BlockSpec((ROW_TILE, LANES), rope_tile)],
        out_specs=out_specs,
        out_shape=tuple(jax.ShapeDtypeStruct((nst * ROW_TILE, d), BF16) for _ in range(3)),
        compiler_params=_params(1),
        name="inproj1_sample",
    )(x, mods, norm_w, w_bf16, cos_t, sin_t)
    return (qp, kp, vp), (qs, ks, vs)


def _rope_tables(n_tok):
    half = DIFF_HD // 4
    pos = np.arange(n_tok)
    lane = np.arange(LANES)
    sub = lane % DIFF_HD
    p = np.where(sub[None, :] < DIFF_HD // 2, (pos // GRID_W)[:, None], (pos % GRID_W)[:, None])
    inv = jnp.asarray(ROPE_THETA, F32) ** (-jnp.asarray(sub % half, F32) / half)
    ang = jnp.asarray(p, F32) * inv[None, :]
    sign = np.where((lane % (2 * half)) < half, -1.0, 1.0).astype(np.float32)
    return jnp.cos(ang), jnp.sin(ang) * sign[None, :]


def _diffattn_kernel(*refs, has_cache, lam_init):
    if has_cache:
        q_ref, k_ref, v_ref, ck_ref, cv_ref, lam_ref, sw_ref, o_ref = refs
    else:
        q_ref, k_ref, v_ref, lam_ref, sw_ref, o_ref = refs
    hd2 = 2 * DIFF_HD
    lv = lam_ref[...]
    lam = (jnp.exp(jnp.sum(lv[0:1] * lv[1:2], axis=1, keepdims=True))
           - jnp.exp(jnp.sum(lv[2:3] * lv[3:4], axis=1, keepdims=True)) + lam_init)
    lane = lax.broadcasted_iota(I32, (q_ref.shape[0], hd2), 1)
    for h in range(DIFF_HEADS):
        cols = slice(h * hd2, (h + 1) * hd2)
        q = q_ref[:, cols]
        zero = jnp.zeros_like(q)
        keys = [(k_ref[:, cols].astype(BF16), v_ref[:, cols].astype(BF16))]
        if has_cache:
            keys.append((ck_ref[:, cols].astype(BF16), cv_ref[:, cols].astype(BF16)))
        o = None
        for c in range(2):
            qc = jnp.where((lane < DIFF_HD) == (c == 0), q, zero)
            s = [_dot_nt(qc, k) for k, _ in keys]
            mx = functools.reduce(jnp.maximum, [jnp.max(si, axis=1, keepdims=True) for si in s])
            e = [jnp.exp(si - mx) for si in s]
            z = functools.reduce(jnp.add, [jnp.sum(ei, axis=1, keepdims=True) for ei in e])
            pv = functools.reduce(jnp.add,
                                  [_dot(ei.astype(BF16), v) for ei, (_, v) in zip(e, keys)])
            pv = pv * (1.0 / z)
            o = pv if c == 0 else o - lam * pv
        o_ref[:, cols] = ((_rms(o) * sw_ref[...]) * (1.0 - lam_init)).astype(BF16)


def _diffattn(q, k, v, lam_vecs, subln_w, batch, seq_len, q_block, lam_init, cache=None):
    d = D_MODEL
    nq = seq_len // q_block
    has_cache = cache is not None
    kv_spec = pl.BlockSpec((seq_len, d), lambda b, qi: (b, 0))
    in_specs = [pl.BlockSpec((q_block, d), lambda b, qi: (b * nq + qi, 0)), kv_spec, kv_spec]
    args = [q, k, v]
    if has_cache:
        past = cache[0].shape[0] // batch
        c_spec = pl.BlockSpec((past, d), lambda b, qi: (b, 0))
        in_specs += [c_spec, c_spec]
        args += list(cache)
    in_specs += [pl.BlockSpec((4, DIFF_HD), lambda b, qi: (0, 0)),
                 pl.BlockSpec((1, 2 * DIFF_HD), lambda b, qi: (0, 0))]
    args += [lam_vecs, subln_w]
    return pl.pallas_call(
        functools.partial(_diffattn_kernel, has_cache=has_cache, lam_init=lam_init),
        grid=(batch, nq),
        in_specs=in_specs,
        out_specs=pl.BlockSpec((q_block, d), lambda b, qi: (b * nq + qi, 0)),
        out_shape=jax.ShapeDtypeStruct((batch * seq_len, d), BF16),
        compiler_params=_params(2),
        name="diffattn_cache" if has_cache else "diffattn",
    )(*args)


def _router_weights(router_group, router_expert):
    w = jnp.concatenate([router_group, router_expert], axis=1)
    w = jnp.pad(w, ((0, 0), (0, LANES - w.shape[1])))
    hi = w.astype(BF16)
    return hi, (w - hi.astype(F32)).astype(BF16)


def _inproj0_weights(w_in):
    gq, gk, gv, gg, gaf, gab, hq, hff, hfb, hi, hg = jnp.split(
        w_in, [256, 512, 1024, 1536, 1552, 1568, 1824, 2080, 2336, 2848], axis=1)
    w = jnp.concatenate([gq, gk, gv, gg, hq, hff, hfb, hi, hg, gaf, gab], axis=1)
    return jnp.pad(w, ((0, 0), (0, AB_COLS - w.shape[1]))).astype(BF16)


def kernel(x_prompt, x_sample, state_gla, state_hgrn, cache_diff_k, cache_diff_v, c, c_ctx,
           w_ada, b_ada, norm1_w, norm2_w, w_in_ab, gla_a2, gla_a_bias, hgrn_lb, gla_onorm_w,
           hgrn_onorm_w, w_out_ab, w_in_c, lam_q1, lam_k1, lam_q2, lam_k2, diff_subln_w, w_out_c,
           router_group, router_expert, moe_w1, moe_w3, moe_w2, final_norm_w):
    bp, lp, d = x_prompt.shape
    bs, ls, _ = x_sample.shape
    depth = w_ada.shape[0]
    assert depth == 2 and d == D_MODEL and bs <= 7
    tp, ts = bp * lp, bs * ls
    npt, nst = tp // ROW_TILE, ts // ROW_TILE
    tps = ls // ROW_TILE
    xp = x_prompt.reshape(tp, d)
    xs = x_sample.reshape(ts, d)

    cond8 = jnp.concatenate([c_ctx[None, :], c, jnp.zeros((7 - bs, d), F32)], axis=0)
    mods = _adaln(cond8, w_ada, b_ada).reshape(depth * 8, 6, d)

    proj = _inproj0(xp, xs, mods, norm1_w[0:1], _inproj0_weights(w_in_ab[0]), tps)
    a_bias = gla_a_bias[0][:, None, :]
    scan_args = (gla_a2[0], a_bias, hgrn_lb, gla_onorm_w[0:1], hgrn_onorm_w[0:1])
    mixed_p, s_fin = _scan(proj, 0, bp, lp, *scan_args)
    s0 = jnp.concatenate([state_gla[:, 0], state_hgrn[:, 0]], axis=2).swapaxes(-1, -2)
    s0 = s0.reshape(bs, 2, SCAN_PAIRS, 2, HEAD_DV, HEAD_DK)
    zero = jnp.zeros_like(s0[:, :, :, 0])
    s0 = jnp.concatenate([jnp.concatenate([s0[:, :, :, 0], zero], axis=-1),
                          jnp.concatenate([zero, s0[:, :, :, 1]], axis=-1)], axis=-2)
    mixed_s = _scan(proj, tp, bs, ls, *scan_args, s0=s0)
    s_fin = s_fin.swapaxes(-1, -2)
    new_state_gla = s_fin[:, None, :, :GLA_HEADS]
    new_state_hgrn = s_fin[:, None, :, GLA_HEADS:]

    wr = _router_weights(router_group[0], router_expert[0])
    x1, h2, slab, counts = _post((xp, xs), mixed_p, mixed_s, mods, 0, norm2_w[0:1],
                                 w_out_ab[0].astype(BF16), *wr, tps)
    ys, pos = _moe(h2, slab, counts, moe_w1, moe_w3, moe_w2, 0)
    x2 = _combine(pos, x1, slab, mods, 0, final_norm_w[None, :], ys, 0, npt + nst, npt, tps, False)

    lam_init = 0.8 - 0.6 * math.exp(-0.3 * 1)
    cos_t, sin_t = _rope_tables(ls)
    (qp, kp, vp), (qs, ks, vs) = _inproj1(x2, mods, norm1_w[1:2], w_in_c[0].astype(BF16),
                                          npt, nst, tps, cos_t, sin_t)
    lam_vecs = jnp.stack([lam_q1[0], lam_k1[0], lam_q2[0], lam_k2[0]])
    att_p = _diffattn(qp, kp, vp, lam_vecs, diff_subln_w[0:1], bp, lp, lp, lam_init)
    past = cache_diff_k.shape[2]
    cache = (cache_diff_k[:, 0].reshape(bs * past, d), cache_diff_v[:, 0].reshape(bs * past, d))
    att_s = _diffattn(qs, ks, vs, lam_vecs, diff_subln_w[0:1], bs, ls, ROW_TILE, lam_init, cache)

    wr = _router_weights(router_group[1], router_expert[1])
    x3, h2, slab, counts = _post((x2,), att_p, att_s, mods, 1, norm2_w[1:2],
                                 w_out_c[0].astype(BF16), *wr, tps)
    ys, pos = _moe(h2, slab, counts, moe_w1, moe_w3, moe_w2, 1)
    fw = final_norm_w[None, :]
    y_p = _combine(pos, x3, slab, mods, 1, fw, ys, 0, npt, npt, tps, True)
    y_s = _combine(pos, x3, slab, mods, 1, fw, ys, npt, nst, npt, tps, True)

    return (y_p.reshape(bp, lp, d), y_s.reshape(bs, ls, d), new_state_gla, new_state_hgrn,
            kp.reshape(bp, 1, lp, DIFF_HEADS, 2, DIFF_HD),
            vp.reshape(bp, 1, lp, DIFF_HEADS, 2 * DIFF_HD))
```

```python
import functools
import math

import jax
import jax.numpy as jnp
import numpy as np
from jax import lax
from jax.experimental import pallas as pl
from jax.experimental.pallas import tpu as pltpu

F32 = jnp.float32
BF16 = jnp.bfloat16
I32 = jnp.int32

D_MODEL = 1024
GLA_HEADS = 4
HGRN_HEADS = 4
SCAN_HEADS = GLA_HEADS + HGRN_HEADS
SCAN_PAIRS = SCAN_HEADS // 2
HEAD_DK = 64
HEAD_DV = 128
GATE_RANK = 16
GLA_GATE_NORM = 16.0
DIFF_HEADS = 8
DIFF_HD = 64
GRID_W = 64
ROPE_THETA = 10000.0
N_GROUPS = 4
EXPERTS_PER_GROUP = 8
N_EXPERTS = N_GROUPS * EXPERTS_PER_GROUP
MOE_HIDDEN = 512
EPS = 1e-6
LANES = 128
TOKEN_ROWS = D_MODEL // LANES
NEG_BIG = -1e30

ROW_TILE = 256
SCAN_CHUNK = 64
EXPERT_TILE = 256
DISPATCH_TILE = 512
VMEM_LIMIT = 56 * 1024 * 1024

_C_GQ, _C_GK, _C_GV, _C_GG = 0, 256, 512, 1024
_C_HQ, _C_HFF, _C_HFB, _C_HI, _C_HG = 1536, 1792, 2048, 2304, 2816
_C_GAF, _C_GAB = 3328, 3344
AB_COLS = 3456


def _params(n_axes, vmem=VMEM_LIMIT):
    return pltpu.CompilerParams(dimension_semantics=("arbitrary",) * n_axes,
                                vmem_limit_bytes=vmem)


def _dot(a, b):
    return jnp.dot(a, b, preferred_element_type=F32)


def _dot_nt(a, b):
    return lax.dot_general(a, b, (((1,), (1,)), ((), ())), preferred_element_type=F32)


def _dot_tn(a, b):
    return lax.dot_general(a, b, (((0,), (0,)), ((), ())), preferred_element_type=F32)


def _split_bf16(x):
    hi = x.astype(BF16)
    lo = (x - hi.astype(F32)).astype(BF16)
    return hi, lo


def _silu(x):
    return x * jax.nn.sigmoid(x)


def _log_sigmoid(x):
    return jnp.minimum(x, 0.0) - jnp.log(1.0 + jnp.exp(-jnp.abs(x)))


def _rms(x):
    return x * lax.rsqrt(jnp.mean(x * x, axis=-1, keepdims=True) + EPS)


def _modulate(x, norm_w, shift, scale):
    return (_rms(x) * norm_w) * (1.0 + scale) + shift


def _to_token_major(dst_ref, x, row0=0):
    n = x.shape[0]
    for s in range(TOKEN_ROWS):
        dst_ref[pl.ds(row0 + s, n, stride=TOKEN_ROWS), :] = x[:, s * LANES:(s + 1) * LANES]


def _from_token_major(src_ref, n, row0=0):
    return jnp.concatenate([src_ref[pl.ds(row0 + s, n, stride=TOKEN_ROWS), :]
                            for s in range(TOKEN_ROWS)], axis=1)


def _ada_kernel(c_ref, w_ref, b_ref, o_ref):
    s = _silu(c_ref[...])
    o_ref[0] = _dot(s.astype(BF16), w_ref[0].astype(BF16)) + b_ref[0]


def _adaln(cond8, w_ada, b_ada):
    depth, d, n = w_ada.shape
    tn = 1536
    return pl.pallas_call(
        _ada_kernel,
        grid=(depth, n // tn),
        in_specs=[pl.BlockSpec((8, d), lambda l, j: (0, 0)),
                  pl.BlockSpec((1, d, tn), lambda l, j: (l, 0, j)),
                  pl.BlockSpec((1, 1, tn), lambda l, j: (l, 0, j))],
        out_specs=pl.BlockSpec((1, 8, tn), lambda l, j: (l, 0, j)),
        out_shape=jax.ShapeDtypeStruct((depth, 8, n), F32),
        compiler_params=_params(2),
        name="adaln",
    )(cond8, w_ada, b_ada.reshape(depth, 1, n))


def _mod_row(i, layer, n_prompt_tiles, tiles_per_sample):
    r = jnp.where(i < n_prompt_tiles, 0, 1 + (i - n_prompt_tiles) // tiles_per_sample)
    return layer * 8 + r


def _inproj0_kernel(xp_ref, xs_ref, mod_ref, nw_ref, w_ref, o_ref, *, n_prompt_tiles):
    i = pl.program_id(0)
    x = jnp.where(i < n_prompt_tiles, xp_ref[...], xs_ref[...])
    h = _modulate(x, nw_ref[...], mod_ref[0, 0:1, :], mod_ref[0, 1:2, :])
    o_ref[...] = _dot(h.astype(BF16), w_ref[...])


def _inproj0(xp, xs, mods, norm_w, w_bf16, tiles_per_sample):
    tp, d = xp.shape
    ts = xs.shape[0]
    n = w_bf16.shape[1]
    npt, nst = tp // ROW_TILE, ts // ROW_TILE
    mod_map = lambda i: (_mod_row(i, 0, npt, tiles_per_sample), 0, 0)
    return pl.pallas_call(
        functools.partial(_inproj0_kernel, n_prompt_tiles=npt),
        grid=(npt + nst,),
        in_specs=[pl.BlockSpec((ROW_TILE, d), lambda i: (jnp.minimum(i, npt - 1), 0)),
                  pl.BlockSpec((ROW_TILE, d), lambda i: (jnp.maximum(i - npt, 0), 0)),
                  pl.BlockSpec((1, 6, d), mod_map),
                  pl.BlockSpec((1, d), lambda i: (0, 0)),
                  pl.BlockSpec((d, n), lambda i: (0, 0))],
        out_specs=pl.BlockSpec((ROW_TILE, n), lambda i: (i, 0)),
        out_shape=jax.ShapeDtypeStruct((tp + ts, n), F32),
        compiler_params=_params(1),
        name="inproj0",
    )(xp, xs, mods, norm_w, w_bf16)


def _scan_kernel(*refs, seq_len, has_state):
    if has_state:
        (p_ref, a2_ref, ab_ref, lb_ref, ong_ref, onh_ref, s0_ref,
         mixed_ref, qf, kf, qb, kb, vv, dec_f, dec_b, o_scr, st_f, st_b) = refs
        sfin_ref = None
    else:
        (p_ref, a2_ref, ab_ref, lb_ref, ong_ref, onh_ref,
         mixed_ref, sfin_ref, qf, kf, qb, kb, vv, dec_f, dec_b, o_scr, st_f, st_b) = refs
        s0_ref = None
    C = SCAN_CHUNK
    n_chunks = seq_len // C
    gqk = GLA_HEADS * HEAD_DK

    row = lax.broadcasted_iota(I32, (C, C), 0)
    col = lax.broadcasted_iota(I32, (C, C), 1)
    lower = col <= row
    upper = col >= row
    tri_lo = jnp.where(lower, 1.0, 0.0).astype(BF16)
    tri_up = jnp.where(upper, 1.0, 0.0).astype(BF16)

    lbp = lb_ref[...]
    lb_max = jnp.maximum(lbp[0], lbp[1])
    lb_e0 = jnp.exp(lbp[0] - lb_max)
    lb_e1 = jnp.exp(lbp[1] - lb_max)
    lb = lb_e0 / (lb_e0 + lb_e1)

    def cumsum_chunk(tri, la):
        hi, lo = _split_bf16(la)
        return _dot(tri, hi) + _dot(tri, lo)

    def prep(n, carry):
        r0 = pl.multiple_of(n * C, C)
        rows = pl.ds(r0, C)
        gq = p_ref[rows, _C_GQ:_C_GQ + gqk] * (HEAD_DK ** -0.5)
        gk = p_ref[rows, _C_GK:_C_GK + gqk]
        hq = _silu(p_ref[rows, _C_HQ:_C_HQ + gqk]) * (HEAD_DK ** -0.5)
        for d_i, (q_s, k_s, dec_s, tri, last) in enumerate(
                ((qf, kf, dec_f, tri_lo, C - 1), (qb, kb, dec_b, tri_up, 0))):
            c_ga = _C_GAF if d_i == 0 else _C_GAB
            c_hf = _C_HFF if d_i == 0 else _C_HFB
            ga = p_ref[rows, c_ga:c_ga + GATE_RANK]
            xg = _dot(ga.astype(BF16), a2_ref[d_i].astype(BF16)) + ab_ref[d_i]
            la_g = _log_sigmoid(xg) / GLA_GATE_NORM
            f = lb[d_i:d_i + 1, :] + (1.0 - lb[d_i:d_i + 1, :]) * jax.nn.sigmoid(
                p_ref[rows, c_hf:c_hf + gqk])
            la_h = jnp.log(f)
            for q, k, la, c0 in ((gq, gk, la_g, 0), (hq, 1.0 - f, la_h, gqk)):
                b = cumsum_chunk(tri, la)
                q_s[rows, c0:c0 + gqk] = (q * jnp.exp(b)).astype(BF16)
                k_s[rows, c0:c0 + gqk] = (k * jnp.exp(-b)).astype(BF16)
                dec_s[n, :, c0:c0 + gqk] = jnp.exp(b[last:last + 1, :])
        vv[rows, 0:512] = p_ref[rows, _C_GV:_C_GV + 512].astype(BF16)
        vv[rows, 512:1024] = p_ref[rows, _C_HI:_C_HI + 512].astype(BF16)
        return carry

    lax.fori_loop(0, n_chunks, prep, 0)

    o_scr[...] = jnp.zeros_like(o_scr)
    for p in range(SCAN_PAIRS):
        if has_state:
            st_f[p] = s0_ref[0, 0, p]
            st_b[p] = s0_ref[0, 1, p]
        else:
            st_f[p] = jnp.zeros((2 * HEAD_DV, 2 * HEAD_DK), F32)
            st_b[p] = jnp.zeros((2 * HEAD_DV, 2 * HEAD_DK), F32)

    first_head = lax.broadcasted_iota(I32, (C, 2 * HEAD_DK), 1) < HEAD_DK
    row2 = lax.broadcasted_iota(I32, (2 * C, C), 0) % C
    col2 = lax.broadcasted_iota(I32, (2 * C, C), 1)
    lower2 = col2 <= row2
    upper2 = col2 >= row2

    def per_head_rows(x):
        z = jnp.zeros_like(x)
        return jnp.concatenate([jnp.where(first_head, x, z), jnp.where(first_head, z, x)], axis=0)

    def add_out(rows, p, res):
        c0 = p * 2 * HEAD_DV
        o_scr[rows, c0:c0 + HEAD_DV] += res[0:C, 0:HEAD_DV]
        o_scr[rows, c0 + HEAD_DV:c0 + 2 * HEAD_DV] += res[C:2 * C, HEAD_DV:2 * HEAD_DV]

    def sweep(n, carry):
        m = n_chunks - 1 - n
        rows = pl.ds(pl.multiple_of(n * C, C), C)
        rows_m = pl.ds(pl.multiple_of(m * C, C), C)
        decay_f, decay_b = dec_f[n], dec_b[m]
        for p in range(SCAN_PAIRS):
            ks = slice(p * 2 * HEAD_DK, (p + 1) * 2 * HEAD_DK)
            vs = slice(p * 2 * HEAD_DV, (p + 1) * 2 * HEAD_DV)
            qd, kd, vh = per_head_rows(qf[rows, ks]), kf[rows, ks], vv[rows, vs]
            s_f = st_f[p]
            sc = (jnp.where(lower2, _dot_nt(qd, kd), 0.0)
                  + jnp.where(upper2, _dot_nt(per_head_rows(qb[rows, ks]), kb[rows, ks]), 0.0))
            add_out(rows, p, _dot_nt(qd, s_f.astype(BF16)) + _dot(sc.astype(BF16), vh))
            st_f[p] = decay_f[:, ks] * (s_f + _dot_tn(vh, kd))
            s_b = st_b[p]
            vm, kbm = vv[rows_m, vs], kb[rows_m, ks]
            add_out(rows_m, p, _dot_nt(per_head_rows(qb[rows_m, ks]), s_b.astype(BF16)))
            st_b[p] = decay_b[:, ks] * (s_b + _dot_tn(vm, kbm))
        return carry

    lax.fori_loop(0, n_chunks, sweep, 0)

    def finish(n, carry):
        rows = pl.ds(pl.multiple_of(n * C, C), C)
        for h in range(SCAN_HEADS):
            vs = slice(h * HEAD_DV, (h + 1) * HEAD_DV)
            if h < GLA_HEADS:
                gate = p_ref[rows, _C_GG + h * HEAD_DV:_C_GG + (h + 1) * HEAD_DV]
                onw = ong_ref[...]
            else:
                hh = h - GLA_HEADS
                gate = p_ref[rows, _C_HG + hh * HEAD_DV:_C_HG + (hh + 1) * HEAD_DV]
                onw = onh_ref[...]
            mixed_ref[rows, vs] = ((_rms(o_scr[rows, vs]) * onw) * _silu(gate)).astype(BF16)
        return carry

    lax.fori_loop(0, n_chunks, finish, 0)

    if sfin_ref is not None:
        for d_i, st in enumerate((st_f, st_b)):
            for p in range(SCAN_PAIRS):
                sfin_ref[0, d_i, 2 * p] = st[p, 0:HEAD_DV, 0:HEAD_DK]
                sfin_ref[0, d_i, 2 * p + 1] = st[p, HEAD_DV:2 * HEAD_DV, HEAD_DK:2 * HEAD_DK]


def _scan(p, row0, batch, seq_len, a2, a_bias, lb, onorm_g, onorm_h, s0=None):
    n = p.shape[1]
    assert row0 % seq_len == 0
    blk0 = row0 // seq_len
    has_state = s0 is not None
    n_chunks = seq_len // SCAN_CHUNK
    st_shape = (1, 2, SCAN_HEADS, HEAD_DV, HEAD_DK)
    pair_shape = (SCAN_PAIRS, 2 * HEAD_DV, 2 * HEAD_DK)
    in_specs = [pl.BlockSpec((seq_len, n), lambda b: (blk0 + b, 0), pipeline_mode=pl.Buffered(1)),
                pl.BlockSpec(a2.shape, lambda b: (0, 0, 0)),
                pl.BlockSpec(a_bias.shape, lambda b: (0, 0, 0)),
                pl.BlockSpec(lb.shape, lambda b: (0, 0, 0)),
                pl.BlockSpec((1, HEAD_DV), lambda b: (0, 0)),
                pl.BlockSpec((1, HEAD_DV), lambda b: (0, 0))]
    args = [p, a2, a_bias, lb, onorm_g, onorm_h]
    mixed_shape = jax.ShapeDtypeStruct((batch * seq_len, D_MODEL), BF16)
    mixed_spec = pl.BlockSpec((seq_len, D_MODEL), lambda b: (b, 0))
    if has_state:
        in_specs.append(pl.BlockSpec((1, 2) + pair_shape, lambda b: (b, 0, 0, 0, 0)))
        args.append(s0)
        out_shape, out_specs = mixed_shape, mixed_spec
    else:
        out_shape = (mixed_shape, jax.ShapeDtypeStruct((batch,) + st_shape[1:], F32))
        out_specs = (mixed_spec, pl.BlockSpec(st_shape, lambda b: (b, 0, 0, 0, 0)))
    scratch = [pltpu.VMEM((seq_len, 512), BF16) for _ in range(4)]
    scratch += [pltpu.VMEM((seq_len, D_MODEL), BF16),
                pltpu.VMEM((n_chunks, 1, 512), F32), pltpu.VMEM((n_chunks, 1, 512), F32),
                pltpu.VMEM((seq_len, D_MODEL), F32),
                pltpu.VMEM(pair_shape, F32), pltpu.VMEM(pair_shape, F32)]
    return pl.pallas_call(
        functools.partial(_scan_kernel, seq_len=seq_len, has_state=has_state),
        grid=(batch,),
        in_specs=in_specs, out_specs=out_specs, out_shape=out_shape,
        scratch_shapes=scratch,
        compiler_params=_params(1),
        name="scan_state" if has_state else "scan_fresh",
    )(*args)


def _post_kernel(*refs, split_x, n_prompt_tiles):
    if split_x:
        xp_ref, xs_ref = refs[0], refs[1]
        refs = refs[2:]
    else:
        x_ref = refs[0]
        refs = refs[1:]
    (mp_ref, ms_ref, mod_ref, nw_ref, wo_ref, wrh_ref, wrl_ref,
     x1_ref, h2_ref, slab_ref, cnt_ref, carry) = refs
    i = pl.program_id(0)
    is_prompt = i < n_prompt_tiles
    if split_x:
        x = jnp.where(is_prompt, xp_ref[...], xs_ref[...])
    else:
        x = x_ref[...]
    mixed = jnp.where(is_prompt, mp_ref[...], ms_ref[...])
    x1 = x + mod_ref[0, 2:3, :] * _dot(mixed, wo_ref[...])
    x1_ref[...] = x1
    h2 = _modulate(x1, nw_ref[...], mod_ref[0, 3:4, :], mod_ref[0, 4:5, :])
    _to_token_major(h2_ref, h2)

    hh, hl = _split_bf16(h2)
    logits = _dot(hh, wrh_ref[...]) + _dot(hl, wrh_ref[...]) + _dot(hh, wrl_ref[...])
    tm = logits.shape[0]
    lane = lax.broadcasted_iota(I32, (tm, LANES), 1).astype(F32)

    def first_max(v):
        mx = jnp.max(v, axis=1, keepdims=True)
        idx = jnp.min(jnp.where(v == mx, lane, float(LANES)), axis=1, keepdims=True)
        return mx, idx

    gl = jnp.where(lane < N_GROUPS, logits, NEG_BIG)
    gmax, gidx = first_max(gl)
    g_val = 1.0 / jnp.sum(jnp.exp(gl - gmax), axis=1, keepdims=True)
    lo = N_GROUPS + EXPERTS_PER_GROUP * gidx
    el = jnp.where((lane >= lo) & (lane < lo + EXPERTS_PER_GROUP), logits, NEG_BIG)
    emax, l1 = first_max(el)
    esum = jnp.sum(jnp.exp(el - emax), axis=1, keepdims=True)
    e2max, l2 = first_max(jnp.where(lane == l1, NEG_BIG, el))
    p1 = 1.0 / esum
    p2 = jnp.exp(e2max - emax) / esum
    w1 = g_val * (p1 / (p1 + p2))
    w2 = g_val * (p2 / (p1 + p2))
    id1 = l1 - N_GROUPS
    id2 = l2 - N_GROUPS

    @pl.when(i == 0)
    def _():
        carry[...] = jnp.zeros_like(carry)

    sel1 = lane == id1
    sel2 = lane == id2
    onehot = jnp.where(sel1 | sel2, 1.0, 0.0)
    row = lax.broadcasted_iota(I32, (tm, tm), 0)
    col = lax.broadcasted_iota(I32, (tm, tm), 1)
    earlier = jnp.where(col < row, 1.0, 0.0).astype(BF16)
    before = _dot(earlier, onehot.astype(BF16)) + carry[...]
    rank1 = jnp.sum(jnp.where(sel1, before, 0.0), axis=1, keepdims=True)
    rank2 = jnp.sum(jnp.where(sel2, before, 0.0), axis=1, keepdims=True)
    total = carry[...] + jnp.sum(onehot, axis=0, keepdims=True)
    carry[...] = total
    cnt_ref[...] = total

    slab = jnp.zeros((tm, LANES), F32)
    for k, v in enumerate((id1, id2, w1, w2, rank1, rank2)):
        slab = jnp.where(lane == k, v, slab)
    slab_ref[...] = slab


def _post(x_args, mixed_p, mixed_s, mods, layer, norm_w, w_out_bf16, wr_hi, wr_lo,
          tiles_per_sample):
    split_x = len(x_args) == 2
    tp, ts = mixed_p.shape[0], mixed_s.shape[0]
    t, d = tp + ts, D_MODEL
    npt, nst = tp // ROW_TILE, ts // ROW_TILE
    tile = lambda i: (i, 0)
    if split_x:
        x_specs = [pl.BlockSpec((ROW_TILE, d), lambda i: (jnp.minimum(i, npt - 1), 0)),
                   pl.BlockSpec((ROW_TILE, d), lambda i: (jnp.maximum(i - npt, 0), 0))]
    else:
        x_specs = [pl.BlockSpec((ROW_TILE, d), tile)]
    in_specs = x_specs + [
        pl.BlockSpec((ROW_TILE, d), lambda i: (jnp.minimum(i, npt - 1), 0)),
        pl.BlockSpec((ROW_TILE, d), lambda i: (jnp.maximum(i - npt, 0), 0)),
        pl.BlockSpec((1, 6, d), lambda i: (_mod_row(i, layer, npt, tiles_per_sample), 0, 0)),
        pl.BlockSpec((1, d), lambda i: (0, 0)),
        pl.BlockSpec((d, d), lambda i: (0, 0)),
        pl.BlockSpec((d, LANES), lambda i: (0, 0)),
        pl.BlockSpec((d, LANES), lambda i: (0, 0))]
    return pl.pallas_call(
        functools.partial(_post_kernel, split_x=split_x, n_prompt_tiles=npt),
        grid=(npt + nst,),
        in_specs=in_specs,
        out_specs=(pl.BlockSpec((ROW_TILE, d), tile),
                   pl.BlockSpec((ROW_TILE * TOKEN_ROWS, LANES), tile),
                   pl.BlockSpec((ROW_TILE, LANES), tile), pl.BlockSpec((1, LANES), lambda i: (0, 0))),
        out_shape=(jax.ShapeDtypeStruct((t, d), F32),
                   jax.ShapeDtypeStruct((t * TOKEN_ROWS, LANES), F32),
                   jax.ShapeDtypeStruct((t, LANES), F32), jax.ShapeDtypeStruct((1, LANES), F32)),
        scratch_shapes=[pltpu.VMEM((1, LANES), F32)],
        compiler_params=_params(1),
        name=f"post{layer}",
    )(*x_args, mixed_p, mixed_s, mods, norm_w, w_out_bf16, wr_hi, wr_lo)


def _dispatch_kernel(zero_ref, pos_ref, h2_ref, hs_ref, zero_buf, sem):
    j = pl.program_id(0)
    td = DISPATCH_TILE

    @pl.when(j == 0)
    def _():
        zero_buf[...] = jnp.zeros_like(zero_buf)

        def zero_copy(k):
            start = pl.multiple_of(zero_ref[k], EXPERT_TILE * TOKEN_ROWS)
            return pltpu.make_async_copy(
                zero_buf, hs_ref.at[pl.ds(start, EXPERT_TILE * TOKEN_ROWS)], sem)

        def start_zero(k, c):
            @pl.when(zero_ref[k] >= 0)
            def _():
                zero_copy(k).start()
            return c

        def wait_zero(k, c):
            @pl.when(zero_ref[k] >= 0)
            def _():
                zero_copy(k).wait()
            return c

        lax.fori_loop(0, 2 * N_EXPERTS, start_zero, 0)
        lax.fori_loop(0, 2 * N_EXPERTS, wait_zero, 0)

    def issue(r, c):
        for s in range(2):
            dst = pl.multiple_of(pos_ref[0, s, r], TOKEN_ROWS)
            pltpu.make_async_copy(h2_ref.at[pl.ds(r * TOKEN_ROWS, TOKEN_ROWS)],
                                  hs_ref.at[pl.ds(dst, TOKEN_ROWS)], sem).start()
        return c

    lax.fori_loop(0, td, issue, 0, unroll=8)
    for s in range(2):
        pltpu.make_async_copy(h2_ref, hs_ref.at[pl.ds(0, td * TOKEN_ROWS)], sem).wait()


def _dispatch(zero_tiles, pos, h2, n_rows):
    t = h2.shape[0] // TOKEN_ROWS
    nt = t // DISPATCH_TILE
    grid_spec = pltpu.PrefetchScalarGridSpec(
        num_scalar_prefetch=1,
        grid=(nt,),
        in_specs=[pl.BlockSpec((1, 2, DISPATCH_TILE), lambda j, *_: (j, 0, 0),
                               memory_space=pltpu.SMEM),
                  pl.BlockSpec((DISPATCH_TILE * TOKEN_ROWS, LANES), lambda j, *_: (j, 0))],
        out_specs=pl.BlockSpec(memory_space=pl.ANY),
        scratch_shapes=[pltpu.VMEM((EXPERT_TILE * TOKEN_ROWS, LANES), F32),
                        pltpu.SemaphoreType.DMA(())])
    return pl.pallas_call(
        _dispatch_kernel,
        grid_spec=grid_spec,
        out_shape=jax.ShapeDtypeStruct((n_rows * TOKEN_ROWS, LANES), F32),
        compiler_params=_params(1),
        name="dispatch",
    )(zero_tiles, pos, h2)


def _expert_kernel(te_ref, src_ref, nv_ref, hs_ref, w1_ref, w3_ref, w2_ref, ys_ref,
                   w1b, w3b, w2b):
    i = pl.program_id(0)
    prev = te_ref[jnp.maximum(i - 1, 0)]

    @pl.when((i == 0) | (te_ref[i] != prev))
    def _():
        w1b[...] = w1_ref[0, 0].astype(BF16)
        w3b[...] = w3_ref[0, 0].astype(BF16)
        w2b[...] = w2_ref[0, 0].astype(BF16)

    @pl.when(nv_ref[i] > 0)
    def _():
        h = _from_token_major(hs_ref, EXPERT_TILE).astype(BF16)
        g = _silu(_dot(h, w1b[...])) * _dot(h, w3b[...])
        _to_token_major(ys_ref, _dot(g.astype(BF16), w2b[...]))

    @pl.when(nv_ref[i] == 0)
    def _():
        ys_ref[...] = jnp.zeros_like(ys_ref)


def _experts(tile_expert, tile_src, tile_rows, hs, w1, w3, w2, layer):
    n_rows, d = hs.shape[0] // TOKEN_ROWS, D_MODEL
    nt = n_rows // EXPERT_TILE
    hid = w1.shape[-1]
    tok_tile = (EXPERT_TILE * TOKEN_ROWS, LANES)
    row_map = lambda i, te, src, nv: (src[i], 0)
    grid_spec = pltpu.PrefetchScalarGridSpec(
        num_scalar_prefetch=3,
        grid=(nt,),
        in_specs=[pl.BlockSpec(tok_tile, row_map),
                  pl.BlockSpec((1, 1, d, hid), lambda i, te, src, nv: (layer, te[i], 0, 0)),
                  pl.BlockSpec((1, 1, d, hid), lambda i, te, src, nv: (layer, te[i], 0, 0)),
                  pl.BlockSpec((1, 1, hid, d), lambda i, te, src, nv: (layer, te[i], 0, 0))],
        out_specs=pl.BlockSpec(tok_tile, lambda i, te, src, nv: (i, 0)),
        scratch_shapes=[pltpu.VMEM((d, hid), BF16), pltpu.VMEM((d, hid), BF16),
                        pltpu.VMEM((hid, d), BF16)])
    return pl.pallas_call(
        _expert_kernel,
        grid_spec=grid_spec,
        out_shape=jax.ShapeDtypeStruct(hs.shape, F32),
        compiler_params=_params(1),
        name=f"experts{layer}",
    )(tile_expert, tile_src, tile_rows, hs, w1, w3, w2)


def _combine_kernel(pos_ref, x1_ref, slab_ref, mod_ref, fw_ref, ys_ref, out_ref, ybuf, sem,
                    *, final_norm):
    tm = ROW_TILE

    def issue(r, c):
        for s in range(2):
            src = pl.multiple_of(pos_ref[0, s, r], TOKEN_ROWS)
            pltpu.make_async_copy(ys_ref.at[pl.ds(src, TOKEN_ROWS)],
                                  ybuf.at[pl.ds((s * tm + r) * TOKEN_ROWS, TOKEN_ROWS)], sem).start()
        return c

    lax.fori_loop(0, tm, issue, 0, unroll=8)
    pltpu.make_async_copy(ys_ref.at[pl.ds(0, 2 * tm * TOKEN_ROWS)], ybuf, sem).wait()
    y = (slab_ref[:, 2:3] * _from_token_major(ybuf, tm)
         + slab_ref[:, 3:4] * _from_token_major(ybuf, tm, tm * TOKEN_ROWS))
    x2 = x1_ref[...] + mod_ref[0, 5:6, :] * y
    if final_norm:
        x2 = _rms(x2) * fw_ref[...]
    out_ref[...] = x2


def _combine(pos, x1, slab, mods, layer, final_w, ys, tile0, n_tiles, n_prompt_tiles,
             tiles_per_sample, final_norm):
    d = D_MODEL
    tile = lambda i: (tile0 + i, 0)
    mod_map = lambda i: (_mod_row(tile0 + i, layer, n_prompt_tiles, tiles_per_sample), 0, 0)
    return pl.pallas_call(
        functools.partial(_combine_kernel, final_norm=final_norm),
        grid=(n_tiles,),
        in_specs=[pl.BlockSpec((1, 2, ROW_TILE), lambda i: (tile0 + i, 0, 0),
                               memory_space=pltpu.SMEM),
                  pl.BlockSpec((ROW_TILE, d), tile),
                  pl.BlockSpec((ROW_TILE, LANES), tile),
                  pl.BlockSpec((1, 6, d), mod_map),
                  pl.BlockSpec((1, d), lambda i: (0, 0)),
                  pl.BlockSpec(memory_space=pl.ANY)],
        out_specs=pl.BlockSpec((ROW_TILE, d), lambda i: (i, 0)),
        out_shape=jax.ShapeDtypeStruct((n_tiles * ROW_TILE, d), F32),
        scratch_shapes=[pltpu.VMEM((2 * ROW_TILE * TOKEN_ROWS, LANES), F32),
                        pltpu.SemaphoreType.DMA(())],
        compiler_params=_params(1),
        name=f"combine{layer}_{tile0}",
    )(pos, x1, slab, mods, final_w, ys)


def _moe(h2, slab, counts, w1, w3, w2, layer):
    t = h2.shape[0] // TOKEN_ROWS
    n_rows = 2 * t + N_EXPERTS * EXPERT_TILE
    nt = n_rows // EXPERT_TILE
    cnt = counts[0, :N_EXPERTS].astype(I32)
    padded = ((cnt + EXPERT_TILE - 1) // EXPERT_TILE) * EXPERT_TILE
    ends = jnp.cumsum(padded)
    offsets = ends - padded
    tails = jnp.where(cnt > 0, ends - EXPERT_TILE, -1).astype(I32)
    used = ends[-1] // EXPERT_TILE
    tile_start = jnp.arange(nt, dtype=I32) * EXPERT_TILE
    unused = (used + jnp.arange(N_EXPERTS, dtype=I32)) * EXPERT_TILE
    zero_tiles = jnp.concatenate([tails, jnp.where(unused < n_rows, unused, -1)])
    zero_tiles = jnp.where(zero_tiles >= 0, zero_tiles * TOKEN_ROWS, -1).astype(I32)
    tile_src = jnp.minimum(jnp.arange(nt, dtype=I32), used - 1)
    tile_expert = jnp.sum((tile_src * EXPERT_TILE)[:, None] >= ends[None, :], axis=1).astype(I32)
    tile_rows = jnp.where(tile_start < ends[-1],
                          jnp.clip(cnt[tile_expert] - (tile_start - offsets[tile_expert]),
                                   0, EXPERT_TILE), 0).astype(I32)
    ids = slab[:, 0:2].T.astype(I32)
    expert = jnp.arange(N_EXPERTS, dtype=I32)[:, None, None]
    start = jnp.sum(jnp.where(ids[None] == expert, offsets.astype(I32)[:, None, None], 0), axis=0)
    pos = (start + slab[:, 4:6].T.astype(I32)) * TOKEN_ROWS
    tiled = lambda tile: pos.reshape(2, t // tile, tile).transpose(1, 0, 2)
    hs = _dispatch(zero_tiles, tiled(DISPATCH_TILE), h2, n_rows)
    ys = _experts(tile_expert, tile_src, tile_rows, hs, w1, w3, w2, layer)
    return ys, tiled(ROW_TILE)


def _rope(x, cos, sin_signed):
    lane = lax.broadcasted_iota(I32, (x.shape[0], LANES), 1)
    low = (lane % 32) < 16
    outs = []
    for j in range(x.shape[1] // LANES):
        xb = x[:, j * LANES:(j + 1) * LANES]
        partner = jnp.where(low, pltpu.roll(xb, LANES - 16, 1), pltpu.roll(xb, 16, 1))
        outs.append(xb * cos + partner * sin_signed)
    return jnp.concatenate(outs, axis=1)


def _inproj1_prompt_kernel(x_ref, mod_ref, nw_ref, w_ref, q_ref, k_ref, v_ref):
    d = D_MODEL
    h = _modulate(x_ref[...], nw_ref[...], mod_ref[0, 0:1, :], mod_ref[0, 1:2, :]).astype(BF16)
    q_ref[...] = (_dot(h, w_ref[:, 0:d]) * (DIFF_HD ** -0.5)).astype(BF16)
    k_ref[...] = _dot(h, w_ref[:, d:2 * d])
    v_ref[...] = _dot(h, w_ref[:, 2 * d:3 * d])


def _inproj1_sample_kernel(x_ref, mod_ref, nw_ref, w_ref, cos_ref, sin_ref, q_ref, k_ref, v_ref):
    d = D_MODEL
    h = _modulate(x_ref[...], nw_ref[...], mod_ref[0, 0:1, :], mod_ref[0, 1:2, :]).astype(BF16)
    cos, sin = cos_ref[...], sin_ref[...]
    q_ref[...] = (_rope(_dot(h, w_ref[:, 0:d]), cos, sin) * (DIFF_HD ** -0.5)).astype(BF16)
    k_ref[...] = _rope(_dot(h, w_ref[:, d:2 * d]), cos, sin).astype(BF16)
    v_ref[...] = _dot(h, w_ref[:, 2 * d:3 * d]).astype(BF16)


def _inproj1(x, mods, norm_w, w_bf16, n_prompt_tiles, n_sample_tiles, tiles_per_sample,
             cos_t, sin_t):
    d = D_MODEL
    npt, nst = n_prompt_tiles, n_sample_tiles
    common = [pl.BlockSpec((1, d), lambda i: (0, 0)), pl.BlockSpec((d, 3 * d), lambda i: (0, 0))]
    tile = lambda i: (i, 0)
    out_specs = tuple(pl.BlockSpec((ROW_TILE, d), tile) for _ in range(3))
    qp, kp, vp = pl.pallas_call(
        _inproj1_prompt_kernel,
        grid=(npt,),
        in_specs=[pl.BlockSpec((ROW_TILE, d), tile),
                  pl.BlockSpec((1, 6, d), lambda i: (8, 0, 0))] + common,
        out_specs=out_specs,
        out_shape=(jax.ShapeDtypeStruct((npt * ROW_TILE, d), BF16),
                   jax.ShapeDtypeStruct((npt * ROW_TILE, d), F32),
                   jax.ShapeDtypeStruct((npt * ROW_TILE, d), F32)),
        compiler_params=_params(1),
        name="inproj1_prompt",
    )(x, mods, norm_w, w_bf16)
    rope_tile = lambda i: (i % tiles_per_sample, 0)
    qs, ks, vs = pl.pallas_call(
        _inproj1_sample_kernel,
        grid=(nst,),
        in_specs=[pl.BlockSpec((ROW_TILE, d), lambda i: (npt + i, 0)),
                  pl.BlockSpec((1, 6, d), lambda i: (8 + 1 + i // tiles_per_sample, 0, 0))]
        + common + [pl.BlockSpec((ROW_TILE, LANES), rope_tile),
                    pl.BlockSpec((ROW_TILE, LANES), rope_tile)],
        out_specs=out_specs,
        out_shape=tuple(jax.ShapeDtypeStruct((nst * ROW_TILE, d), BF16) for _ in range(3)),
        compiler_params=_params(1),
        name="inproj1_sample",
    )(x, mods, norm_w, w_bf16, cos_t, sin_t)
    return (qp, kp, vp), (qs, ks, vs)


def _rope_tables(n_tok):
    half = DIFF_HD // 4
    pos = np.arange(n_tok)
    lane = np.arange(LANES)
    sub = lane % DIFF_HD
    p = np.where(sub[None, :] < DIFF_HD // 2, (pos // GRID_W)[:, None], (pos % GRID_W)[:, None])
    inv = jnp.asarray(ROPE_THETA, F32) ** (-jnp.asarray(sub % half, F32) / half)
    ang = jnp.asarray(p, F32) * inv[None, :]
    sign = np.where((lane % (2 * half)) < half, -1.0, 1.0).astype(np.float32)
    return jnp.cos(ang), jnp.sin(ang) * sign[None, :]


def _diffattn_kernel(*refs, has_cache, lam_init):
    if has_cache:
        q_ref, k_ref, v_ref, ck_ref, cv_ref, lam_ref, sw_ref, o_ref = refs
    else:
        q_ref, k_ref, v_ref, lam_ref, sw_ref, o_ref = refs
    hd2 = 2 * DIFF_HD
    lv = lam_ref[...]
    lam = (jnp.exp(jnp.sum(lv[0:1] * lv[1:2], axis=1, keepdims=True))
           - jnp.exp(jnp.sum(lv[2:3] * lv[3:4], axis=1, keepdims=True)) + lam_init)
    lane = lax.broadcasted_iota(I32, (q_ref.shape[0], hd2), 1)
    for h in range(DIFF_HEADS):
        cols = slice(h * hd2, (h + 1) * hd2)
        q = q_ref[:, cols]
        zero = jnp.zeros_like(q)
        keys = [(k_ref[:, cols].astype(BF16), v_ref[:, cols].astype(BF16))]
        if has_cache:
            keys.append((ck_ref[:, cols].astype(BF16), cv_ref[:, cols].astype(BF16)))
        o = None
        for c in range(2):
            qc = jnp.where((lane < DIFF_HD) == (c == 0), q, zero)
            s = [_dot_nt(qc, k) for k, _ in keys]
            mx = functools.reduce(jnp.maximum, [jnp.max(si, axis=1, keepdims=True) for si in s])
            e = [jnp.exp(si - mx) for si in s]
            z = functools.reduce(jnp.add, [jnp.sum(ei, axis=1, keepdims=True) for ei in e])
            pv = functools.reduce(jnp.add,
                                  [_dot(ei.astype(BF16), v) for ei, (_, v) in zip(e, keys)])
            pv = pv * (1.0 / z)
            o = pv if c == 0 else o - lam * pv
        o_ref[:, cols] = ((_rms(o) * sw_ref[...]) * (1.0 - lam_init)).astype(BF16)


def _diffattn(q, k, v, lam_vecs, subln_w, batch, seq_len, q_block, lam_init, cache=None):
    d = D_MODEL
    nq = seq_len // q_block
    has_cache = cache is not None
    kv_spec = pl.BlockSpec((seq_len, d), lambda b, qi: (b, 0))
    in_specs = [pl.BlockSpec((q_block, d), lambda b, qi: (b * nq + qi, 0)), kv_spec, kv_spec]
    args = [q, k, v]
    if has_cache:
        past = cache[0].shape[0] // batch
        c_spec = pl.BlockSpec((past, d), lambda b, qi: (b, 0))
        in_specs += [c_spec, c_spec]
        args += list(cache)
    in_specs += [pl.BlockSpec((4, DIFF_HD), lambda b, qi: (0, 0)),
                 pl.BlockSpec((1, 2 * DIFF_HD), lambda b, qi: (0, 0))]
    args += [lam_vecs, subln_w]
    return pl.pallas_call(
        functools.partial(_diffattn_kernel, has_cache=has_cache, lam_init=lam_init),
        grid=(batch, nq),
        in_specs=in_specs,
        out_specs=pl.BlockSpec((q_block, d), lambda b, qi: (b * nq + qi, 0)),
        out_shape=jax.ShapeDtypeStruct((batch * seq_len, d), BF16),
        compiler_params=_params(2),
        name="diffattn_cache" if has_cache else "diffattn",
    )(*args)


def _router_weights(router_group, router_expert):
    w = jnp.concatenate([router_group, router_expert], axis=1)
    w = jnp.pad(w, ((0, 0), (0, LANES - w.shape[1])))
    hi = w.astype(BF16)
    return hi, (w - hi.astype(F32)).astype(BF16)


def _inproj0_weights(w_in):
    gq, gk, gv, gg, gaf, gab, hq, hff, hfb, hi, hg = jnp.split(
        w_in, [256, 512, 1024, 1536, 1552, 1568, 1824, 2080, 2336, 2848], axis=1)
    w = jnp.concatenate([gq, gk, gv, gg, hq, hff, hfb, hi, hg, gaf, gab], axis=1)
    return jnp.pad(w, ((0, 0), (0, AB_COLS - w.shape[1]))).astype(BF16)


def kernel(x_prompt, x_sample, state_gla, state_hgrn, cache_diff_k, cache_diff_v, c, c_ctx,
           w_ada, b_ada, norm1_w, norm2_w, w_in_ab, gla_a2, gla_a_bias, hgrn_lb, gla_onorm_w,
           hgrn_onorm_w, w_out_ab, w_in_c, lam_q1, lam_k1, lam_q2, lam_k2, diff_subln_w, w_out_c,
           router_group, router_expert, moe_w1, moe_w3, moe_w2, final_norm_w):
    bp, lp, d = x_prompt.shape
    bs, ls, _ = x_sample.shape
    depth = w_ada.shape[0]
    assert depth == 2 and d == D_MODEL and bs <= 7
    tp, ts = bp * lp, bs * ls
    npt, nst = tp // ROW_TILE, ts // ROW_TILE
    tps = ls // ROW_TILE
    xp = x_prompt.reshape(tp, d)
    xs = x_sample.reshape(ts, d)

    cond8 = jnp.concatenate([c_ctx[None, :], c, jnp.zeros((7 - bs, d), F32)], axis=0)
    mods = _adaln(cond8, w_ada, b_ada).reshape(depth * 8, 6, d)

    proj = _inproj0(xp, xs, mods, norm1_w[0:1], _inproj0_weights(w_in_ab[0]), tps)
    a_bias = gla_a_bias[0][:, None, :]
    scan_args = (gla_a2[0], a_bias, hgrn_lb, gla_onorm_w[0:1], hgrn_onorm_w[0:1])
    mixed_p, s_fin = _scan(proj, 0, bp, lp, *scan_args)
    s0 = jnp.concatenate([state_gla[:, 0], state_hgrn[:, 0]], axis=2).swapaxes(-1, -2)
    s0 = s0.reshape(bs, 2, SCAN_PAIRS, 2, HEAD_DV, HEAD_DK)
    zero = jnp.zeros_like(s0[:, :, :, 0])
    s0 = jnp.concatenate([jnp.concatenate([s0[:, :, :, 0], zero], axis=-1),
                          jnp.concatenate([zero, s0[:, :, :, 1]], axis=-1)], axis=-2)
    mixed_s = _scan(proj, tp, bs, ls, *scan_args, s0=s0)
    s_fin = s_fin.swapaxes(-1, -2)
    new_state_gla = s_fin[:, None, :, :GLA_HEADS]
    new_state_hgrn = s_fin[:, None, :, GLA_HEADS:]

    wr = _router_weights(router_group[0], router_expert[0])
    x1, h2, slab, counts = _post((xp, xs), mixed_p, mixed_s, mods, 0, norm2_w[0:1],
                                 w_out_ab[0].astype(BF16), *wr, tps)
    ys, pos = _moe(h2, slab, counts, moe_w1, moe_w3, moe_w2, 0)
    x2 = _combine(pos, x1, slab, mods, 0, final_norm_w[None, :], ys, 0, npt + nst, npt, tps, False)

    lam_init = 0.8 - 0.6 * math.exp(-0.3 * 1)
    cos_t, sin_t = _rope_tables(ls)
    (qp, kp, vp), (qs, ks, vs) = _inproj1(x2, mods, norm1_w[1:2], w_in_c[0].astype(BF16),
                                          npt, nst, tps, cos_t, sin_t)
    lam_vecs = jnp.stack([lam_q1[0], lam_k1[0], lam_q2[0], lam_k2[0]])
    att_p = _diffattn(qp, kp, vp, lam_vecs, diff_subln_w[0:1], bp, lp, lp, lam_init)
    past = cache_diff_k.shape[2]
    cache = (cache_diff_k[:, 0].reshape(bs * past, d), cache_diff_v[:, 0].reshape(bs * past, d))
    att_s = _diffattn(qs, ks, vs, lam_vecs, diff_subln_w[0:1], bs, ls, ROW_TILE, lam_init, cache)

    wr = _router_weights(router_group[1], router_expert[1])
    x3, h2, slab, counts = _post((x2,), att_p, att_s, mods, 1, norm2_w[1:2],
                                 w_out_c[0].astype(BF16), *wr, tps)
    ys, pos = _moe(h2, slab, counts, moe_w1, moe_w3, moe_w2, 1)
    fw = final_norm_w[None, :]
    y_p = _combine(pos, x3, slab, mods, 1, fw, ys, 0, npt, npt, tps, True)
    y_s = _combine(pos, x3, slab, mods, 1, fw, ys, npt, nst, npt, tps, True)

    return (y_p.reshape(bp, lp, d), y_s.reshape(bs, ls, d), new_state_gla, new_state_hgrn,
            kp.reshape(bp, 1, lp, DIFF_HEADS, 2, DIFF_HD),
            vp.reshape(bp, 1, lp, DIFF_HEADS, 2 * DIFF_HD))
```

```python
import functools
import math

import jax
import jax.numpy as jnp
import numpy as np
from jax import lax
from jax.experimental import pallas as pl
from jax.experimental.pallas import tpu as pltpu

F32 = jnp.float32
BF16 = jnp.bfloat16
I32 = jnp.int32

D_MODEL = 1024
GLA_HEADS = 4
HGRN_HEADS = 4
SCAN_HEADS = GLA_HEADS + HGRN_HEADS
SCAN_PAIRS = SCAN_HEADS // 2
HEAD_DK = 64
HEAD_DV = 128
GATE_RANK = 16
GLA_GATE_NORM = 16.0
DIFF_HEADS = 8
DIFF_HD = 64
GRID_W = 64
ROPE_THETA = 10000.0
N_GROUPS = 4
EXPERTS_PER_GROUP = 8
N_EXPERTS = N_GROUPS * EXPERTS_PER_GROUP
MOE_HIDDEN = 512
EPS = 1e-6
LANES = 128
TOKEN_ROWS = D_MODEL // LANES
NEG_BIG = -1e30

ROW_TILE = 256
SCAN_CHUNK = 64
EXPERT_TILE = 256
MOVE_BLOCK = 16
STAGE_TOKENS = 2 * ROW_TILE + N_EXPERTS * MOVE_BLOCK
MAX_BLOCKS = STAGE_TOKENS // MOVE_BLOCK
VMEM_LIMIT = 56 * 1024 * 1024

_C_GQ, _C_GK, _C_GV, _C_GG = 0, 256, 512, 1024
_C_HQ, _C_HFF, _C_HFB, _C_HI, _C_HG = 1536, 1792, 2048, 2304, 2816
_C_GAF, _C_GAB = 3328, 3344
AB_COLS = 3456


def _params(n_axes, vmem=VMEM_LIMIT):
    return pltpu.CompilerParams(dimension_semantics=("arbitrary",) * n_axes,
                                vmem_limit_bytes=vmem)


def _cdiv(a, b):
    return (a + b - 1) // b


def _dot(a, b):
    return jnp.dot(a, b, preferred_element_type=F32)


def _dot_nt(a, b):
    return lax.dot_general(a, b, (((1,), (1,)), ((), ())), preferred_element_type=F32)


def _dot_tn(a, b):
    return lax.dot_general(a, b, (((0,), (0,)), ((), ())), preferred_element_type=F32)


def _split_bf16(x):
    hi = x.astype(BF16)
    lo = (x - hi.astype(F32)).astype(BF16)
    return hi, lo


def _silu(x):
    return x * jax.nn.sigmoid(x)


def _log_sigmoid(x):
    return jnp.minimum(x, 0.0) - jnp.log(1.0 + jnp.exp(-jnp.abs(x)))


def _rms(x):
    return x * lax.rsqrt(jnp.mean(x * x, axis=-1, keepdims=True) + EPS)


def _modulate(x, norm_w, shift, scale):
    return (_rms(x) * norm_w) * (1.0 + scale) + shift


def _to_token_major(dst_ref, x, row0=0):
    n = x.shape[0]
    for s in range(TOKEN_ROWS):
        dst_ref[pl.ds(row0 + s, n, stride=TOKEN_ROWS), :] = x[:, s * LANES:(s + 1) * LANES]


def _from_token_major(src_ref, n, row0=0):
    return jnp.concatenate([src_ref[pl.ds(row0 + s, n, stride=TOKEN_ROWS), :]
                            for s in range(TOKEN_ROWS)], axis=1)


def _ada_kernel(c_ref, w_ref, b_ref, o_ref):
    s = _silu(c_ref[...])
    o_ref[0] = _dot(s.astype(BF16), w_ref[0].astype(BF16)) + b_ref[0]


def _adaln(cond8, w_ada, b_ada):
    depth, d, n = w_ada.shape
    tn = 1536
    return pl.pallas_call(
        _ada_kernel,
        grid=(depth, n // tn),
        in_specs=[pl.BlockSpec((8, d), lambda l, j: (0, 0)),
                  pl.BlockSpec((1, d, tn), lambda l, j: (l, 0, j)),
                  pl.BlockSpec((1, 1, tn), lambda l, j: (l, 0, j))],
        out_specs=pl.BlockSpec((1, 8, tn), lambda l, j: (l, 0, j)),
        out_shape=jax.ShapeDtypeStruct((depth, 8, n), F32),
        compiler_params=_params(2),
        name="adaln",
    )(cond8, w_ada, b_ada.reshape(depth, 1, n))


def _mod_row(i, layer, n_prompt_tiles, tiles_per_sample):
    r = jnp.where(i < n_prompt_tiles, 0, 1 + (i - n_prompt_tiles) // tiles_per_sample)
    return layer * 8 + r


def _inproj0_kernel(xp_ref, xs_ref, mod_ref, nw_ref, w_ref, o_ref, *, n_prompt_tiles):
    i = pl.program_id(0)
    x = jnp.where(i < n_prompt_tiles, xp_ref[...], xs_ref[...])
    h = _modulate(x, nw_ref[...], mod_ref[0, 0:1, :], mod_ref[0, 1:2, :])
    o_ref[...] = _dot(h.astype(BF16), w_ref[...])


def _inproj0(xp, xs, mods, norm_w, w_bf16, tiles_per_sample):
    tp, d = xp.shape
    ts = xs.shape[0]
    n = w_bf16.shape[1]
    npt, nst = tp // ROW_TILE, ts // ROW_TILE
    mod_map = lambda i: (_mod_row(i, 0, npt, tiles_per_sample), 0, 0)
    return pl.pallas_call(
        functools.partial(_inproj0_kernel, n_prompt_tiles=npt),
        grid=(npt + nst,),
        in_specs=[pl.BlockSpec((ROW_TILE, d), lambda i: (jnp.minimum(i, npt - 1), 0)),
                  pl.BlockSpec((ROW_TILE, d), lambda i: (jnp.maximum(i - npt, 0), 0)),
                  pl.BlockSpec((1, 6, d), mod_map),
                  pl.BlockSpec((1, d), lambda i: (0, 0)),
                  pl.BlockSpec((d, n), lambda i: (0, 0))],
        out_specs=pl.BlockSpec((ROW_TILE, n), lambda i: (i, 0)),
        out_shape=jax.ShapeDtypeStruct((tp + ts, n), F32),
        compiler_params=_params(1),
        name="inproj0",
    )(xp, xs, mods, norm_w, w_bf16)


def _scan_kernel(*refs, seq_len, has_state):
    if has_state:
        (p_ref, a2_ref, ab_ref, lb_ref, ong_ref, onh_ref, s0_ref,
         mixed_ref, qf, kf, qb, kb, vv, dec_f, dec_b, o_scr, st_f, st_b) = refs
        sfin_ref = None
    else:
        (p_ref, a2_ref, ab_ref, lb_ref, ong_ref, onh_ref,
         mixed_ref, sfin_ref, qf, kf, qb, kb, vv, dec_f, dec_b, o_scr, st_f, st_b) = refs
        s0_ref = None
    C = SCAN_CHUNK
    n_chunks = seq_len // C
    gqk = GLA_HEADS * HEAD_DK

    row = lax.broadcasted_iota(I32, (C, C), 0)
    col = lax.broadcasted_iota(I32, (C, C), 1)
    lower = col <= row
    upper = col >= row
    tri_lo = jnp.where(lower, 1.0, 0.0).astype(BF16)
    tri_up = jnp.where(upper, 1.0, 0.0).astype(BF16)

    lbp = lb_ref[...]
    lb_max = jnp.maximum(lbp[0], lbp[1])
    lb_e0 = jnp.exp(lbp[0] - lb_max)
    lb_e1 = jnp.exp(lbp[1] - lb_max)
    lb = lb_e0 / (lb_e0 + lb_e1)

    def cumsum_chunk(tri, la):
        hi, lo = _split_bf16(la)
        return _dot(tri, hi) + _dot(tri, lo)

    def prep(n, carry):
        r0 = pl.multiple_of(n * C, C)
        rows = pl.ds(r0, C)
        gq = p_ref[rows, _C_GQ:_C_GQ + gqk] * (HEAD_DK ** -0.5)
        gk = p_ref[rows, _C_GK:_C_GK + gqk]
        hq = _silu(p_ref[rows, _C_HQ:_C_HQ + gqk]) * (HEAD_DK ** -0.5)
        for d_i, (q_s, k_s, dec_s, tri, last) in enumerate(
                ((qf, kf, dec_f, tri_lo, C - 1), (qb, kb, dec_b, tri_up, 0))):
            c_ga = _C_GAF if d_i == 0 else _C_GAB
            c_hf = _C_HFF if d_i == 0 else _C_HFB
            ga = p_ref[rows, c_ga:c_ga + GATE_RANK]
            xg = _dot(ga.astype(BF16), a2_ref[d_i].astype(BF16)) + ab_ref[d_i]
            la_g = _log_sigmoid(xg) / GLA_GATE_NORM
            f = lb[d_i:d_i + 1, :] + (1.0 - lb[d_i:d_i + 1, :]) * jax.nn.sigmoid(
                p_ref[rows, c_hf:c_hf + gqk])
            la_h = jnp.log(f)
            for q, k, la, c0 in ((gq, gk, la_g, 0), (hq, 1.0 - f, la_h, gqk)):
                b = cumsum_chunk(tri, la)
                q_s[rows, c0:c0 + gqk] = (q * jnp.exp(b)).astype(BF16)
                k_s[rows, c0:c0 + gqk] = (k * jnp.exp(-b)).astype(BF16)
                dec_s[n, :, c0:c0 + gqk] = jnp.exp(b[last:last + 1, :])
        vv[rows, 0:512] = p_ref[rows, _C_GV:_C_GV + 512].astype(BF16)
        vv[rows, 512:1024] = p_ref[rows, _C_HI:_C_HI + 512].astype(BF16)
        return carry

    lax.fori_loop(0, n_chunks, prep, 0)

    o_scr[...] = jnp.zeros_like(o_scr)
    for p in range(SCAN_PAIRS):
        if has_state:
            st_f[p] = s0_ref[0, 0, p]
            st_b[p] = s0_ref[0, 1, p]
        else:
            st_f[p] = jnp.zeros((2 * HEAD_DV, 2 * HEAD_DK), F32)
            st_b[p] = jnp.zeros((2 * HEAD_DV, 2 * HEAD_DK), F32)

    first_head = lax.broadcasted_iota(I32, (C, 2 * HEAD_DK), 1) < HEAD_DK
    row2 = lax.broadcasted_iota(I32, (2 * C, C), 0) % C
    col2 = lax.broadcasted_iota(I32, (2 * C, C), 1)
    lower2 = col2 <= row2
    upper2 = col2 >= row2

    def per_head_rows(x):
        z = jnp.zeros_like(x)
        return jnp.concatenate([jnp.where(first_head, x, z), jnp.where(first_head, z, x)], axis=0)

    def add_out(rows, p, res):
        c0 = p * 2 * HEAD_DV
        o_scr[rows, c0:c0 + HEAD_DV] += res[0:C, 0:HEAD_DV]
        o_scr[rows, c0 + HEAD_DV:c0 + 2 * HEAD_DV] += res[C:2 * C, HEAD_DV:2 * HEAD_DV]

    def sweep(n, carry):
        m = n_chunks - 1 - n
        rows = pl.ds(pl.multiple_of(n * C, C), C)
        rows_m = pl.ds(pl.multiple_of(m * C, C), C)
        decay_f, decay_b = dec_f[n], dec_b[m]
        for p in range(SCAN_PAIRS):
            ks = slice(p * 2 * HEAD_DK, (p + 1) * 2 * HEAD_DK)
            vs = slice(p * 2 * HEAD_DV, (p + 1) * 2 * HEAD_DV)
            qd, kd, vh = per_head_rows(qf[rows, ks]), kf[rows, ks], vv[rows, vs]
            s_f = st_f[p]
            sc = (jnp.where(lower2, _dot_nt(qd, kd), 0.0)
                  + jnp.where(upper2, _dot_nt(per_head_rows(qb[rows, ks]), kb[rows, ks]), 0.0))
            add_out(rows, p, _dot_nt(qd, s_f.astype(BF16)) + _dot(sc.astype(BF16), vh))
            st_f[p] = decay_f[:, ks] * (s_f + _dot_tn(vh, kd))
            s_b = st_b[p]
            vm, kbm = vv[rows_m, vs], kb[rows_m, ks]
            add_out(rows_m, p, _dot_nt(per_head_rows(qb[rows_m, ks]), s_b.astype(BF16)))
            st_b[p] = decay_b[:, ks] * (s_b + _dot_tn(vm, kbm))
        return carry

    lax.fori_loop(0, n_chunks, sweep, 0)

    def finish(n, carry):
        rows = pl.ds(pl.multiple_of(n * C, C), C)
        for h in range(SCAN_HEADS):
            vs = slice(h * HEAD_DV, (h + 1) * HEAD_DV)
            if h < GLA_HEADS:
                gate = p_ref[rows, _C_GG + h * HEAD_DV:_C_GG + (h + 1) * HEAD_DV]
                onw = ong_ref[...]
            else:
                hh = h - GLA_HEADS
                gate = p_ref[rows, _C_HG + hh * HEAD_DV:_C_HG + (hh + 1) * HEAD_DV]
                onw = onh_ref[...]
            mixed_ref[rows, vs] = ((_rms(o_scr[rows, vs]) * onw) * _silu(gate)).astype(BF16)
        return carry

    lax.fori_loop(0, n_chunks, finish, 0)

    if sfin_ref is not None:
        for d_i, st in enumerate((st_f, st_b)):
            for p in range(SCAN_PAIRS):
                sfin_ref[0, d_i, 2 * p] = st[p, 0:HEAD_DV, 0:HEAD_DK]
                sfin_ref[0, d_i, 2 * p + 1] = st[p, HEAD_DV:2 * HEAD_DV, HEAD_DK:2 * HEAD_DK]


def _scan(p, row0, batch, seq_len, a2, a_bias, lb, onorm_g, onorm_h, s0=None):
    n = p.shape[1]
    assert row0 % seq_len == 0
    blk0 = row0 // seq_len
    has_state = s0 is not None
    n_chunks = seq_len // SCAN_CHUNK
    st_shape = (1, 2, SCAN_HEADS, HEAD_DV, HEAD_DK)
    pair_shape = (SCAN_PAIRS, 2 * HEAD_DV, 2 * HEAD_DK)
    in_specs = [pl.BlockSpec((seq_len, n), lambda b: (blk0 + b, 0), pipeline_mode=pl.Buffered(1)),
                pl.BlockSpec(a2.shape, lambda b: (0, 0, 0)),
                pl.BlockSpec(a_bias.shape, lambda b: (0, 0, 0)),
                pl.BlockSpec(lb.shape, lambda b: (0, 0, 0)),
                pl.BlockSpec((1, HEAD_DV), lambda b: (0, 0)),
                pl.BlockSpec((1, HEAD_DV), lambda b: (0, 0))]
    args = [p, a2, a_bias, lb, onorm_g, onorm_h]
    mixed_shape = jax.ShapeDtypeStruct((batch * seq_len, D_MODEL), BF16)
    mixed_spec = pl.BlockSpec((seq_len, D_MODEL), lambda b: (b, 0))
    if has_state:
        in_specs.append(pl.BlockSpec((1, 2) + pair_shape, lambda b: (b, 0, 0, 0, 0)))
        args.append(s0)
        out_shape, out_specs = mixed_shape, mixed_spec
    else:
        out_shape = (mixed_shape, jax.ShapeDtypeStruct((batch,) + st_shape[1:], F32))
        out_specs = (mixed_spec, pl.BlockSpec(st_shape, lambda b: (b, 0, 0, 0, 0)))
    scratch = [pltpu.VMEM((seq_len, 512), BF16) for _ in range(4)]
    scratch += [pltpu.VMEM((seq_len, D_MODEL), BF16),
                pltpu.VMEM((n_chunks, 1, 512), F32), pltpu.VMEM((n_chunks, 1, 512), F32),
                pltpu.VMEM((seq_len, D_MODEL), F32),
                pltpu.VMEM(pair_shape, F32), pltpu.VMEM(pair_shape, F32)]
    return pl.pallas_call(
        functools.partial(_scan_kernel, seq_len=seq_len, has_state=has_state),
        grid=(batch,),
        in_specs=in_specs, out_specs=out_specs, out_shape=out_shape,
        scratch_shapes=scratch,
        compiler_params=_params(1),
        name="scan_state" if has_state else "scan_fresh",
    )(*args)


def _post_kernel(*refs, split_x, n_prompt_tiles):
    if split_x:
        xp_ref, xs_ref = refs[0], refs[1]
        refs = refs[2:]
    else:
        x_ref = refs[0]
        refs = refs[1:]
    (mp_ref, ms_ref, mod_ref, nw_ref, wo_ref, wrh_ref, wrl_ref,
     x1_ref, h2_ref, slab_ref, tab_ref, carry) = refs
    i = pl.program_id(0)
    is_prompt = i < n_prompt_tiles
    if split_x:
        x = jnp.where(is_prompt, xp_ref[...], xs_ref[...])
    else:
        x = x_ref[...]
    mixed = jnp.where(is_prompt, mp_ref[...], ms_ref[...])
    x1 = x + mod_ref[0, 2:3, :] * _dot(mixed, wo_ref[...])
    x1_ref[...] = x1
    h2 = _modulate(x1, nw_ref[...], mod_ref[0, 3:4, :], mod_ref[0, 4:5, :])
    _to_token_major(h2_ref, h2)

    hh, hl = _split_bf16(h2)
    logits = _dot(hh, wrh_ref[...]) + _dot(hl, wrh_ref[...]) + _dot(hh, wrl_ref[...])
    tm = logits.shape[0]
    lane = lax.broadcasted_iota(I32, (tm, LANES), 1).astype(F32)

    def first_max(v):
        mx = jnp.max(v, axis=1, keepdims=True)
        idx = jnp.min(jnp.where(v == mx, lane, float(LANES)), axis=1, keepdims=True)
        return mx, idx

    gl = jnp.where(lane < N_GROUPS, logits, NEG_BIG)
    gmax, gidx = first_max(gl)
    g_val = 1.0 / jnp.sum(jnp.exp(gl - gmax), axis=1, keepdims=True)
    lo = N_GROUPS + EXPERTS_PER_GROUP * gidx
    el = jnp.where((lane >= lo) & (lane < lo + EXPERTS_PER_GROUP), logits, NEG_BIG)
    emax, l1 = first_max(el)
    esum = jnp.sum(jnp.exp(el - emax), axis=1, keepdims=True)
    e2max, l2 = first_max(jnp.where(lane == l1, NEG_BIG, el))
    p1 = 1.0 / esum
    p2 = jnp.exp(e2max - emax) / esum
    w1 = g_val * (p1 / (p1 + p2))
    w2 = g_val * (p2 / (p1 + p2))
    id1 = l1 - N_GROUPS
    id2 = l2 - N_GROUPS

    @pl.when(i == 0)
    def _():
        carry[...] = jnp.zeros_like(carry)

    sel1 = lane == id1
    sel2 = lane == id2
    onehot = jnp.where(sel1 | sel2, 1.0, 0.0)
    row = lax.broadcasted_iota(I32, (tm, tm), 0)
    col = lax.broadcasted_iota(I32, (tm, tm), 1)
    earlier = jnp.where(col < row, 1.0, 0.0).astype(BF16)
    before = _dot(earlier, onehot.astype(BF16))
    count = jnp.sum(onehot, axis=0, keepdims=True)
    blocks = jnp.floor((count + (MOVE_BLOCK - 1.0)) * (1.0 / MOVE_BLOCK)) * MOVE_BLOCK
    e_row = lax.broadcasted_iota(I32, (LANES, LANES), 0)
    e_col = lax.broadcasted_iota(I32, (LANES, LANES), 1)
    lower_experts = jnp.where(e_row < e_col, 1.0, 0.0).astype(BF16)
    run_start = _dot(jnp.broadcast_to(blocks, (8, LANES)).astype(BF16), lower_experts)[0:1]
    slot = before + run_start
    q1 = jnp.sum(jnp.where(sel1, slot, 0.0), axis=1, keepdims=True)
    q2 = jnp.sum(jnp.where(sel2, slot, 0.0), axis=1, keepdims=True)
    tab_row = lax.broadcasted_iota(I32, (8, LANES), 0)
    tab_ref[0] = jnp.where(tab_row == 0, count,
                           jnp.where(tab_row == 1, carry[...],
                                     jnp.where(tab_row == 2, run_start, 0.0)))
    carry[...] = carry[...] + count

    slab = jnp.zeros((tm, LANES), F32)
    for k, v in enumerate((id1, id2, w1, w2, q1, q2)):
        slab = jnp.where(lane == k, v, slab)
    slab_ref[...] = slab


def _post(x_args, mixed_p, mixed_s, mods, layer, norm_w, w_out_bf16, wr_hi, wr_lo,
          tiles_per_sample):
    split_x = len(x_args) == 2
    tp, ts = mixed_p.shape[0], mixed_s.shape[0]
    t, d = tp + ts, D_MODEL
    npt, nst = tp // ROW_TILE, ts // ROW_TILE
    tile = lambda i: (i, 0)
    if split_x:
        x_specs = [pl.BlockSpec((ROW_TILE, d), lambda i: (jnp.minimum(i, npt - 1), 0)),
                   pl.BlockSpec((ROW_TILE, d), lambda i: (jnp.maximum(i - npt, 0), 0))]
    else:
        x_specs = [pl.BlockSpec((ROW_TILE, d), tile)]
    in_specs = x_specs + [
        pl.BlockSpec((ROW_TILE, d), lambda i: (jnp.minimum(i, npt - 1), 0)),
        pl.BlockSpec((ROW_TILE, d), lambda i: (jnp.maximum(i - npt, 0), 0)),
        pl.BlockSpec((1, 6, d), lambda i: (_mod_row(i, layer, npt, tiles_per_sample), 0, 0)),
        pl.BlockSpec((1, d), lambda i: (0, 0)),
        pl.BlockSpec((d, d), lambda i: (0, 0)),
        pl.BlockSpec((d, LANES), lambda i: (0, 0)),
        pl.BlockSpec((d, LANES), lambda i: (0, 0))]
    return pl.pallas_call(
        functools.partial(_post_kernel, split_x=split_x, n_prompt_tiles=npt),
        grid=(npt + nst,),
        in_specs=in_specs,
        out_specs=(pl.BlockSpec((ROW_TILE, d), tile),
                   pl.BlockSpec((ROW_TILE * TOKEN_ROWS, LANES), tile),
                   pl.BlockSpec((ROW_TILE, LANES), tile),
                   pl.BlockSpec((1, 8, LANES), lambda i: (i, 0, 0))),
        out_shape=(jax.ShapeDtypeStruct((t, d), F32),
                   jax.ShapeDtypeStruct((t * TOKEN_ROWS, LANES), F32),
                   jax.ShapeDtypeStruct((t, LANES), F32),
                   jax.ShapeDtypeStruct((npt + nst, 8, LANES), F32)),
        scratch_shapes=[pltpu.VMEM((1, LANES), F32)],
        compiler_params=_params(1),
        name=f"post{layer}",
    )(*x_args, mixed_p, mixed_s, mods, norm_w, w_out_bf16, wr_hi, wr_lo)


def _for_blocks(tab_ref, fn):
    block_rows = MOVE_BLOCK * TOKEN_ROWS

    def body(k, c):
        fn(pl.multiple_of(k * block_rows, block_rows),
           pl.multiple_of(tab_ref[0, 0, k], TOKEN_ROWS))
        return c

    lax.fori_loop(0, tab_ref[0, 0, MAX_BLOCKS], body, 0)


def _dispatch_kernel(zero_ref, tab_ref, q_ref, h2_ref, hs_ref, zero_buf, stage, sem):
    j = pl.program_id(0)
    block_rows = MOVE_BLOCK * TOKEN_ROWS

    @pl.when(j == 0)
    def _():
        zero_buf[...] = jnp.zeros_like(zero_buf)

        def zero_copy(k):
            start = pl.multiple_of(zero_ref[k], EXPERT_TILE * TOKEN_ROWS)
            return pltpu.make_async_copy(
                zero_buf, hs_ref.at[pl.ds(start, EXPERT_TILE * TOKEN_ROWS)], sem)

        def start_zero(k, c):
            @pl.when(zero_ref[k] >= 0)
            def _():
                zero_copy(k).start()
            return c

        def wait_zero(k, c):
            @pl.when(zero_ref[k] >= 0)
            def _():
                zero_copy(k).wait()
            return c

        lax.fori_loop(0, zero_ref.shape[0], start_zero, 0)
        lax.fori_loop(0, zero_ref.shape[0], wait_zero, 0)

        stage[...] = jnp.zeros_like(stage)

    def place(r, c):
        tok = h2_ref[pl.ds(pl.multiple_of(r * TOKEN_ROWS, TOKEN_ROWS), TOKEN_ROWS), :]
        for s in range(2):
            stage[pl.ds(pl.multiple_of(q_ref[0, 0, 2 * r + s], TOKEN_ROWS), TOKEN_ROWS), :] = tok
        return c

    lax.fori_loop(0, ROW_TILE, place, 0, unroll=8)

    def block_copy(stage_row, sorted_row):
        return pltpu.make_async_copy(stage.at[pl.ds(stage_row, block_rows)],
                                     hs_ref.at[pl.ds(sorted_row, block_rows)], sem)

    _for_blocks(tab_ref, lambda a, b: block_copy(a, b).start())
    _for_blocks(tab_ref, lambda a, b: block_copy(0, 0).wait())


def _dispatch(zero_tiles, block_tab, slots, h2, n_rows):
    t = h2.shape[0] // TOKEN_ROWS
    nt = t // ROW_TILE
    smem_tile = lambda shape: pl.BlockSpec((1,) + shape, lambda j, *_: (j, 0, 0),
                                           memory_space=pltpu.SMEM)
    grid_spec = pltpu.PrefetchScalarGridSpec(
        num_scalar_prefetch=1,
        grid=(nt,),
        in_specs=[smem_tile((1, LANES)), smem_tile((1, 2 * ROW_TILE)),
                  pl.BlockSpec((ROW_TILE * TOKEN_ROWS, LANES), lambda j, *_: (j, 0))],
        out_specs=pl.BlockSpec(memory_space=pl.ANY),
        scratch_shapes=[pltpu.VMEM((EXPERT_TILE * TOKEN_ROWS, LANES), F32),
                        pltpu.VMEM((STAGE_TOKENS * TOKEN_ROWS, LANES), F32),
                        pltpu.SemaphoreType.DMA(())])
    return pl.pallas_call(
        _dispatch_kernel,
        grid_spec=grid_spec,
        out_shape=jax.ShapeDtypeStruct((n_rows * TOKEN_ROWS, LANES), F32),
        compiler_params=_params(1),
        name="dispatch",
    )(zero_tiles, block_tab, slots, h2)


def _expert_kernel(te_ref, src_ref, nv_ref, hs_ref, w1_ref, w3_ref, w2_ref, ys_ref,
                   w1b, w3b, w2b):
    i = pl.program_id(0)
    prev = te_ref[jnp.maximum(i - 1, 0)]

    @pl.when((i == 0) | (te_ref[i] != prev))
    def _():
        w1b[...] = w1_ref[0, 0].astype(BF16)
        w3b[...] = w3_ref[0, 0].astype(BF16)
        w2b[...] = w2_ref[0, 0].astype(BF16)

    @pl.when(nv_ref[i] > 0)
    def _():
        h = _from_token_major(hs_ref, EXPERT_TILE).astype(BF16)
        g = _silu(_dot(h, w1b[...])) * _dot(h, w3b[...])
        _to_token_major(ys_ref, _dot(g.astype(BF16), w2b[...]))

    @pl.when(nv_ref[i] == 0)
    def _():
        ys_ref[...] = jnp.zeros_like(ys_ref)


def _experts(tile_expert, tile_src, tile_rows, hs, w1, w3, w2, layer):
    n_rows, d = hs.shape[0] // TOKEN_ROWS, D_MODEL
    nt = n_rows // EXPERT_TILE
    hid = w1.shape[-1]
    tok_tile = (EXPERT_TILE * TOKEN_ROWS, LANES)
    row_map = lambda i, te, src, nv: (src[i], 0)
    grid_spec = pltpu.PrefetchScalarGridSpec(
        num_scalar_prefetch=3,
        grid=(nt,),
        in_specs=[pl.BlockSpec(tok_tile, row_map),
                  pl.BlockSpec((1, 1, d, hid), lambda i, te, src, nv: (layer, te[i], 0, 0)),
                  pl.BlockSpec((1, 1, d, hid), lambda i, te, src, nv: (layer, te[i], 0, 0)),
                  pl.BlockSpec((1, 1, hid, d), lambda i, te, src, nv: (layer, te[i], 0, 0))],
        out_specs=pl.BlockSpec(tok_tile, lambda i, te, src, nv: (i, 0)),
        scratch_shapes=[pltpu.VMEM((d, hid), BF16), pltpu.VMEM((d, hid), BF16),
                        pltpu.VMEM((hid, d), BF16)])
    return pl.pallas_call(
        _expert_kernel,
        grid_spec=grid_spec,
        out_shape=jax.ShapeDtypeStruct(hs.shape, F32),
        compiler_params=_params(1),
        name=f"experts{layer}",
    )(tile_expert, tile_src, tile_rows, hs, w1, w3, w2)


def _combine_kernel(tab_ref, q_ref, w_ref, x1_ref, mod_ref, fw_ref, ys_ref, out_ref,
                    stage, y_tok, sem, *, final_norm):
    block_rows = MOVE_BLOCK * TOKEN_ROWS

    def block_copy(stage_row, sorted_row):
        return pltpu.make_async_copy(ys_ref.at[pl.ds(sorted_row, block_rows)],
                                     stage.at[pl.ds(stage_row, block_rows)], sem)

    _for_blocks(tab_ref, lambda a, b: block_copy(a, b).start())
    _for_blocks(tab_ref, lambda a, b: block_copy(0, 0).wait())

    def pick(r, c):
        rows = [stage[pl.ds(pl.multiple_of(q_ref[0, 0, 2 * r + s], TOKEN_ROWS), TOKEN_ROWS), :]
                for s in range(2)]
        y_tok[pl.ds(pl.multiple_of(r * TOKEN_ROWS, TOKEN_ROWS), TOKEN_ROWS), :] = (
            w_ref[0, 0, 2 * r] * rows[0] + w_ref[0, 0, 2 * r + 1] * rows[1])
        return c

    lax.fori_loop(0, ROW_TILE, pick, 0, unroll=8)
    x2 = x1_ref[...] + mod_ref[0, 5:6, :] * _from_token_major(y_tok, ROW_TILE)
    if final_norm:
        x2 = _rms(x2) * fw_ref[...]
    out_ref[...] = x2


def _combine(block_tab, slots, weights, x1, mods, layer, final_w, ys, tile0, n_tiles,
             n_prompt_tiles, tiles_per_sample, final_norm):
    d = D_MODEL
    tile = lambda i: (tile0 + i, 0)
    mod_map = lambda i: (_mod_row(tile0 + i, layer, n_prompt_tiles, tiles_per_sample), 0, 0)
    smem_tile = lambda shape: pl.BlockSpec((1,) + shape, lambda i: (tile0 + i, 0, 0),
                                           memory_space=pltpu.SMEM)
    return pl.pallas_call(
        functools.partial(_combine_kernel, final_norm=final_norm),
        grid=(n_tiles,),
        in_specs=[smem_tile((1, LANES)), smem_tile((1, 2 * ROW_TILE)),
                  smem_tile((1, 2 * ROW_TILE)),
                  pl.BlockSpec((ROW_TILE, d), tile),
                  pl.BlockSpec((1, 6, d), mod_map),
                  pl.BlockSpec((1, d), lambda i: (0, 0)),
                  pl.BlockSpec(memory_space=pl.ANY)],
        out_specs=pl.BlockSpec((ROW_TILE, d), lambda i: (i, 0)),
        out_shape=jax.ShapeDtypeStruct((n_tiles * ROW_TILE, d), F32),
        scratch_shapes=[pltpu.VMEM((STAGE_TOKENS * TOKEN_ROWS, LANES), F32),
                        pltpu.VMEM((ROW_TILE * TOKEN_ROWS, LANES), F32),
                        pltpu.SemaphoreType.DMA(())],
        compiler_params=_params(1),
        name=f"combine{layer}_{tile0}",
    )(block_tab, slots, weights, x1, mods, final_w, ys)


def _moe(h2, slab, tile_tab, w1, w3, w2, layer):
    t = h2.shape[0] // TOKEN_ROWS
    n_tiles = t // ROW_TILE
    extra_tiles = N_EXPERTS + _cdiv(N_EXPERTS * MOVE_BLOCK, EXPERT_TILE)
    n_rows = 2 * t + extra_tiles * EXPERT_TILE
    nt = n_rows // EXPERT_TILE
    tab = tile_tab[:, :, :N_EXPERTS].astype(I32)
    cnt = tab[-1, 0] + tab[-1, 1]
    tight = _cdiv(cnt, EXPERT_TILE) * EXPERT_TILE
    padded = jnp.where(cnt > 0, _cdiv(cnt + MOVE_BLOCK - 1, EXPERT_TILE) * EXPERT_TILE, 0)
    ends = jnp.cumsum(padded)
    offsets = ends - padded
    tails = jnp.where(cnt > 0, ends - EXPERT_TILE, -1)
    tails2 = jnp.where(padded > tight, ends - 2 * EXPERT_TILE, -1)
    used = ends[-1] // EXPERT_TILE
    tile_start = jnp.arange(nt, dtype=I32) * EXPERT_TILE
    unused = (used + jnp.arange(extra_tiles, dtype=I32)) * EXPERT_TILE
    zero_tiles = jnp.concatenate([tails, tails2, jnp.where(unused < n_rows, unused, -1)])
    zero_tiles = jnp.where(zero_tiles >= 0, zero_tiles * TOKEN_ROWS, -1).astype(I32)
    tile_src = jnp.minimum(jnp.arange(nt, dtype=I32), used - 1)
    tile_expert = jnp.sum((tile_src * EXPERT_TILE)[:, None] >= ends[None, :], axis=1).astype(I32)
    tile_rows = jnp.where(tile_start < ends[-1],
                          jnp.clip(cnt[tile_expert] - (tile_start - offsets[tile_expert]),
                                   0, EXPERT_TILE), 0).astype(I32)
    n_blocks = _cdiv(tab[:, 0], MOVE_BLOCK)
    blocks_through = jnp.cumsum(n_blocks, axis=1)
    k = jnp.arange(MAX_BLOCKS, dtype=I32)
    owner = jnp.sum(blocks_through[:, None, :] <= k[None, :, None], axis=2)
    is_owner = owner[:, :, None] == jnp.arange(N_EXPERTS, dtype=I32)[None, None, :]
    pick = lambda v: jnp.sum(jnp.where(is_owner, v[:, None, :], 0), axis=2)
    run_first = pick(offsets[None, :] + tab[:, 1])
    block_in_run = k[None, :] - pick(blocks_through - n_blocks)
    sorted_row = (run_first + block_in_run * MOVE_BLOCK) * TOKEN_ROWS
    block_tab = jnp.concatenate(
        [sorted_row, blocks_through[:, -1:],
         jnp.zeros((n_tiles, LANES - MAX_BLOCKS - 1), I32)], axis=1).astype(I32)[:, None, :]
    per_tile = lambda cols: cols.reshape(n_tiles, 1, 2 * ROW_TILE)
    slots = per_tile(slab[:, 4:6]).astype(I32) * TOKEN_ROWS
    weights = per_tile(slab[:, 2:4])
    hs = _dispatch(zero_tiles, block_tab, slots, h2, n_rows)
    ys = _experts(tile_expert, tile_src, tile_rows, hs, w1, w3, w2, layer)
    return ys, (block_tab, slots, weights)


def _rope(x, cos, sin_signed):
    lane = lax.broadcasted_iota(I32, (x.shape[0], LANES), 1)
    low = (lane % 32) < 16
    outs = []
    for j in range(x.shape[1] // LANES):
        xb = x[:, j * LANES:(j + 1) * LANES]
        partner = jnp.where(low, pltpu.roll(xb, LANES - 16, 1), pltpu.roll(xb, 16, 1))
        outs.append(xb * cos + partner * sin_signed)
    return jnp.concatenate(outs, axis=1)


def _inproj1_prompt_kernel(x_ref, mod_ref, nw_ref, w_ref, q_ref, k_ref, v_ref):
    d = D_MODEL
    h = _modulate(x_ref[...], nw_ref[...], mod_ref[0, 0:1, :], mod_ref[0, 1:2, :]).astype(BF16)
    q_ref[...] = (_dot(h, w_ref[:, 0:d]) * (DIFF_HD ** -0.5)).astype(BF16)
    k_ref[...] = _dot(h, w_ref[:, d:2 * d])
    v_ref[...] = _dot(h, w_ref[:, 2 * d:3 * d])


def _inproj1_sample_kernel(x_ref, mod_ref, nw_ref, w_ref, cos_ref, sin_ref, q_ref, k_ref, v_ref):
    d = D_MODEL
    h = _modulate(x_ref[...], nw_ref[...], mod_ref[0, 0:1, :], mod_ref[0, 1:2, :]).astype(BF16)
    cos, sin = cos_ref[...], sin_ref[...]
    q_ref[...] = (_rope(_dot(h, w_ref[:, 0:d]), cos, sin) * (DIFF_HD ** -0.5)).astype(BF16)
    k_ref[...] = _rope(_dot(h, w_ref[:, d:2 * d]), cos, sin).astype(BF16)
    v_ref[...] = _dot(h, w_ref[:, 2 * d:3 * d]).astype(BF16)


def _inproj1(x, mods, norm_w, w_bf16, n_prompt_tiles, n_sample_tiles, tiles_per_sample,
             cos_t, sin_t):
    d = D_MODEL
    npt, nst = n_prompt_tiles, n_sample_tiles
    common = [pl.BlockSpec((1, d), lambda i: (0, 0)), pl.BlockSpec((d, 3 * d), lambda i: (0, 0))]
    tile = lambda i: (i, 0)
    out_specs = tuple(pl.BlockSpec((ROW_TILE, d), tile) for _ in range(3))
    qp, kp, vp = pl.pallas_call(
        _inproj1_prompt_kernel,
        grid=(npt,),
        in_specs=[pl.BlockSpec((ROW_TILE, d), tile),
                  pl.BlockSpec((1, 6, d), lambda i: (8, 0, 0))] + common,
        out_specs=out_specs,
        out_shape=(jax.ShapeDtypeStruct((npt * ROW_TILE, d), BF16),
                   jax.ShapeDtypeStruct((npt * ROW_TILE, d), F32),
                   jax.ShapeDtypeStruct((npt * ROW_TILE, d), F32)),
        compiler_params=_params(1),
        name="inproj1_prompt",
    )(x, mods, norm_w, w_bf16)
    rope_tile = lambda i: (i % tiles_per_sample, 0)
    qs, ks, vs = pl.pallas_call(
        _inproj1_sample_kernel,
        grid=(nst,),
        in_specs=[pl.BlockSpec((ROW_TILE, d), lambda i: (npt + i, 0)),
                  pl.BlockSpec((1, 6, d), lambda i: (8 + 1 + i // tiles_per_sample, 0, 0))]
        + common + [pl.BlockSpec((ROW_TILE, LANES), rope_tile),
                    pl.BlockSpec((ROW_TILE, LANES), rope_tile)],
        out_specs=out_specs,
        out_shape=tuple(jax.ShapeDtypeStruct((nst * ROW_TILE, d), BF16) for _ in range(3)),
        compiler_params=_params(1),
        name="inproj1_sample",
    )(x, mods, norm_w, w_bf16, cos_t, sin_t)
    return (qp, kp, vp), (qs, ks, vs)


def _rope_tables(n_tok):
    half = DIFF_HD // 4
    pos = np.arange(n_tok)
    lane = np.arange(LANES)
    sub = lane % DIFF_HD
    p = np.where(sub[None, :] < DIFF_HD // 2, (pos // GRID_W)[:, None], (pos % GRID_W)[:, None])
    inv = jnp.asarray(ROPE_THETA, F32) ** (-jnp.asarray(sub % half, F32) / half)
    ang = jnp.asarray(p, F32) * inv[None, :]
    sign = np.where((lane % (2 * half)) < half, -1.0, 1.0).astype(np.float32)
    return jnp.cos(ang), jnp.sin(ang) * sign[None, :]


def _diffattn_kernel(*refs, has_cache, lam_init):
    if has_cache:
        q_ref, k_ref, v_ref, ck_ref, cv_ref, lam_ref, sw_ref, o_ref = refs
    else:
        q_ref, k_ref, v_ref, lam_ref, sw_ref, o_ref = refs
    hd2 = 2 * DIFF_HD
    lv = lam_ref[...]
    lam = (jnp.exp(jnp.sum(lv[0:1] * lv[1:2], axis=1, keepdims=True))
           - jnp.exp(jnp.sum(lv[2:3] * lv[3:4], axis=1, keepdims=True)) + lam_init)
    lane = lax.broadcasted_iota(I32, (q_ref.shape[0], hd2), 1)
    for h in range(DIFF_HEADS):
        cols = slice(h * hd2, (h + 1) * hd2)
        q = q_ref[:, cols]
        zero = jnp.zeros_like(q)
        keys = [(k_ref[:, cols].astype(BF16), v_ref[:, cols].astype(BF16))]
        if has_cache:
            keys.append((ck_ref[:, cols].astype(BF16), cv_ref[:, cols].astype(BF16)))
        o = None
        for c in range(2):
            qc = jnp.where((lane < DIFF_HD) == (c == 0), q, zero)
            s = [_dot_nt(qc, k) for k, _ in keys]
            mx = functools.reduce(jnp.maximum, [jnp.max(si, axis=1, keepdims=True) for si in s])
            e = [jnp.exp(si - mx) for si in s]
            z = functools.reduce(jnp.add, [jnp.sum(ei, axis=1, keepdims=True) for ei in e])
            pv = functools.reduce(jnp.add,
                                  [_dot(ei.astype(BF16), v) for ei, (_, v) in zip(e, keys)])
            pv = pv * (1.0 / z)
            o = pv if c == 0 else o - lam * pv
        o_ref[:, cols] = ((_rms(o) * sw_ref[...]) * (1.0 - lam_init)).astype(BF16)


def _diffattn(q, k, v, lam_vecs, subln_w, batch, seq_len, q_block, lam_init, cache=None):
    d = D_MODEL
    nq = seq_len // q_block
    has_cache = cache is not None
    kv_spec = pl.BlockSpec((seq_len, d), lambda b, qi: (b, 0))
    in_specs = [pl.BlockSpec((q_block, d), lambda b, qi: (b * nq + qi, 0)), kv_spec, kv_spec]
    args = [q, k, v]
    if has_cache:
        past = cache[0].shape[0] // batch
        c_spec = pl.BlockSpec((past, d), lambda b, qi: (b, 0))
        in_specs += [c_spec, c_spec]
        args += list(cache)
    in_specs += [pl.BlockSpec((4, DIFF_HD), lambda b, qi: (0, 0)),
                 pl.BlockSpec((1, 2 * DIFF_HD), lambda b, qi: (0, 0))]
    args += [lam_vecs, subln_w]
    return pl.pallas_call(
        functools.partial(_diffattn_kernel, has_cache=has_cache, lam_init=lam_init),
        grid=(batch, nq),
        in_specs=in_specs,
        out_specs=pl.BlockSpec((q_block, d), lambda b, qi: (b * nq + qi, 0)),
        out_shape=jax.ShapeDtypeStruct((batch * seq_len, d), BF16),
        compiler_params=_params(2),
        name="diffattn_cache" if has_cache else "diffattn",
    )(*args)


def _router_weights(router_group, router_expert):
    w = jnp.concatenate([router_group, router_expert], axis=1)
    w = jnp.pad(w, ((0, 0), (0, LANES - w.shape[1])))
    hi = w.astype(BF16)
    return hi, (w - hi.astype(F32)).astype(BF16)


def _inproj0_weights(w_in):
    gq, gk, gv, gg, gaf, gab, hq, hff, hfb, hi, hg = jnp.split(
        w_in, [256, 512, 1024, 1536, 1552, 1568, 1824, 2080, 2336, 2848], axis=1)
    w = jnp.concatenate([gq, gk, gv, gg, hq, hff, hfb, hi, hg, gaf, gab], axis=1)
    return jnp.pad(w, ((0, 0), (0, AB_COLS - w.shape[1]))).astype(BF16)


def kernel(x_prompt, x_sample, state_gla, state_hgrn, cache_diff_k, cache_diff_v, c, c_ctx,
           w_ada, b_ada, norm1_w, norm2_w, w_in_ab, gla_a2, gla_a_bias, hgrn_lb, gla_onorm_w,
           hgrn_onorm_w, w_out_ab, w_in_c, lam_q1, lam_k1, lam_q2, lam_k2, diff_subln_w, w_out_c,
           router_group, router_expert, moe_w1, moe_w3, moe_w2, final_norm_w):
    bp, lp, d = x_prompt.shape
    bs, ls, _ = x_sample.shape
    depth = w_ada.shape[0]
    assert depth == 2 and d == D_MODEL and bs <= 7
    tp, ts = bp * lp, bs * ls
    npt, nst = tp // ROW_TILE, ts // ROW_TILE
    tps = ls // ROW_TILE
    xp = x_prompt.reshape(tp, d)
    xs = x_sample.reshape(ts, d)

    cond8 = jnp.concatenate([c_ctx[None, :], c, jnp.zeros((7 - bs, d), F32)], axis=0)
    mods = _adaln(cond8, w_ada, b_ada).reshape(depth * 8, 6, d)

    proj = _inproj0(xp, xs, mods, norm1_w[0:1], _inproj0_weights(w_in_ab[0]), tps)
    a_bias = gla_a_bias[0][:, None, :]
    scan_args = (gla_a2[0], a_bias, hgrn_lb, gla_onorm_w[0:1], hgrn_onorm_w[0:1])
    mixed_p, s_fin = _scan(proj, 0, bp, lp, *scan_args)
    s0 = jnp.concatenate([state_gla[:, 0], state_hgrn[:, 0]], axis=2).swapaxes(-1, -2)
    s0 = s0.reshape(bs, 2, SCAN_PAIRS, 2, HEAD_DV, HEAD_DK)
    zero = jnp.zeros_like(s0[:, :, :, 0])
    s0 = jnp.concatenate([jnp.concatenate([s0[:, :, :, 0], zero], axis=-1),
                          jnp.concatenate([zero, s0[:, :, :, 1]], axis=-1)], axis=-2)
    mixed_s = _scan(proj, tp, bs, ls, *scan_args, s0=s0)
    s_fin = s_fin.swapaxes(-1, -2)
    new_state_gla = s_fin[:, None, :, :GLA_HEADS]
    new_state_hgrn = s_fin[:, None, :, GLA_HEADS:]

    wr = _router_weights(router_group[0], router_expert[0])
    x1, h2, slab, tile_tab = _post((xp, xs), mixed_p, mixed_s, mods, 0, norm2_w[0:1],
                                   w_out_ab[0].astype(BF16), *wr, tps)
    ys, tables = _moe(h2, slab, tile_tab, moe_w1, moe_w3, moe_w2, 0)
    x2 = _combine(*tables, x1, mods, 0, final_norm_w[None, :], ys, 0, npt + nst, npt, tps, False)

    lam_init = 0.8 - 0.6 * math.exp(-0.3 * 1)
    cos_t, sin_t = _rope_tables(ls)
    (qp, kp, vp), (qs, ks, vs) = _inproj1(x2, mods, norm1_w[1:2], w_in_c[0].astype(BF16),
                                          npt, nst, tps, cos_t, sin_t)
    lam_vecs = jnp.stack([lam_q1[0], lam_k1[0], lam_q2[0], lam_k2[0]])
    att_p = _diffattn(qp, kp, vp, lam_vecs, diff_subln_w[0:1], bp, lp, lp, lam_init)
    past = cache_diff_k.shape[2]
    cache = (cache_diff_k[:, 0].reshape(bs * past, d), cache_diff_v[:, 0].reshape(bs * past, d))
    att_s = _diffattn(qs, ks, vs, lam_vecs, diff_subln_w[0:1], bs, ls, ROW_TILE, lam_init, cache)

    wr = _router_weights(router_group[1], router_expert[1])
    x3, h2, slab, tile_tab = _post((x2,), att_p, att_s, mods, 1, norm2_w[1:2],
                                   w_out_c[0].astype(BF16), *wr, tps)
    ys, tables = _moe(h2, slab, tile_tab, moe_w1, moe_w3, moe_w2, 1)
    fw = final_norm_w[None, :]
    y_p = _combine(*tables, x3, mods, 1, fw, ys, 0, npt, npt, tps, True)
    y_s = _combine(*tables, x3, mods, 1, fw, ys, npt, nst, npt, tps, True)

    return (y_p.reshape(bp, lp, d), y_s.reshape(bs, ls, d), new_state_gla, new_state_hgrn,
            kp.reshape(bp, 1, lp, DIFF_HEADS, 2, DIFF_HD),
            vp.reshape(bp, 1, lp, DIFF_HEADS, 2 * DIFF_HD))
```

```python
import functools
import math

import jax
import jax.numpy as jnp
import numpy as np
from jax import lax
from jax.experimental import pallas as pl
from jax.experimental.pallas import tpu as pltpu

F32 = jnp.float32
BF16 = jnp.bfloat16
I32 = jnp.int32

D_MODEL = 1024
GLA_HEADS = 4
HGRN_HEADS = 4
SCAN_HEADS = GLA_HEADS + HGRN_HEADS
SCAN_PAIRS = SCAN_HEADS // 2
HEAD_DK = 64
HEAD_DV = 128
GATE_RANK = 16
GLA_GATE_NORM = 16.0
DIFF_HEADS = 8
DIFF_HD = 64
GRID_W = 64
ROPE_THETA = 10000.0
N_GROUPS = 4
EXPERTS_PER_GROUP = 8
N_EXPERTS = N_GROUPS * EXPERTS_PER_GROUP
MOE_HIDDEN = 512
EPS = 1e-6
LANES = 128
TOKEN_ROWS = D_MODEL // LANES
NEG_BIG = -1e30

ROW_TILE = 256
SCAN_CHUNK = 64
EXPERT_TILE = 256
MOVE_BLOCK = 16
STAGE_TOKENS = 2 * ROW_TILE + N_EXPERTS * MOVE_BLOCK
MAX_BLOCKS = STAGE_TOKENS // MOVE_BLOCK
VMEM_LIMIT = 56 * 1024 * 1024

_C_GQ, _C_GK, _C_GV, _C_GG = 0, 256, 512, 1024
_C_HQ, _C_HFF, _C_HFB, _C_HI, _C_HG = 1536, 1792, 2048, 2304, 2816
_C_GAF, _C_GAB = 3328, 3344
AB_COLS = 3456


def _params(n_axes, vmem=VMEM_LIMIT):
    return pltpu.CompilerParams(dimension_semantics=("arbitrary",) * n_axes,
                                vmem_limit_bytes=vmem)


def _cdiv(a, b):
    return (a + b - 1) // b


def _dot(a, b):
    return jnp.dot(a, b, preferred_element_type=F32)


def _dot_nt(a, b):
    return lax.dot_general(a, b, (((1,), (1,)), ((), ())), preferred_element_type=F32)


def _dot_tn(a, b):
    return lax.dot_general(a, b, (((0,), (0,)), ((), ())), preferred_element_type=F32)


def _split_bf16(x):
    hi = x.astype(BF16)
    lo = (x - hi.astype(F32)).astype(BF16)
    return hi, lo


def _silu(x):
    return x * jax.nn.sigmoid(x)


def _log_sigmoid(x):
    return jnp.minimum(x, 0.0) - jnp.log(1.0 + jnp.exp(-jnp.abs(x)))


def _rms(x):
    return x * lax.rsqrt(jnp.mean(x * x, axis=-1, keepdims=True) + EPS)


def _modulate(x, norm_w, shift, scale):
    return (_rms(x) * norm_w) * (1.0 + scale) + shift


def _to_token_major(dst_ref, x, row0=0):
    n = x.shape[0]
    for s in range(TOKEN_ROWS):
        dst_ref[pl.ds(row0 + s, n, stride=TOKEN_ROWS), :] = x[:, s * LANES:(s + 1) * LANES]


def _from_token_major(src_ref, n, row0=0):
    return jnp.concatenate([src_ref[pl.ds(row0 + s, n, stride=TOKEN_ROWS), :]
                            for s in range(TOKEN_ROWS)], axis=1)


def _ada_kernel(c_ref, w_ref, b_ref, o_ref):
    s = _silu(c_ref[...])
    o_ref[0] = _dot(s.astype(BF16), w_ref[0].astype(BF16)) + b_ref[0]


def _adaln(cond8, w_ada, b_ada):
    depth, d, n = w_ada.shape
    tn = 1536
    return pl.pallas_call(
        _ada_kernel,
        grid=(depth, n // tn),
        in_specs=[pl.BlockSpec((8, d), lambda l, j: (0, 0)),
                  pl.BlockSpec((1, d, tn), lambda l, j: (l, 0, j)),
                  pl.BlockSpec((1, 1, tn), lambda l, j: (l, 0, j))],
        out_specs=pl.BlockSpec((1, 8, tn), lambda l, j: (l, 0, j)),
        out_shape=jax.ShapeDtypeStruct((depth, 8, n), F32),
        compiler_params=_params(2),
        name="adaln",
    )(cond8, w_ada, b_ada.reshape(depth, 1, n))


def _mod_row(i, layer, n_prompt_tiles, tiles_per_sample):
    r = jnp.where(i < n_prompt_tiles, 0, 1 + (i - n_prompt_tiles) // tiles_per_sample)
    return layer * 8 + r


def _inproj0_kernel(xp_ref, xs_ref, mod_ref, nw_ref, w_ref, o_ref, *, n_prompt_tiles):
    i = pl.program_id(0)
    x = jnp.where(i < n_prompt_tiles, xp_ref[...], xs_ref[...])
    h = _modulate(x, nw_ref[...], mod_ref[0, 0:1, :], mod_ref[0, 1:2, :])
    o_ref[...] = _dot(h.astype(BF16), w_ref[...])


def _inproj0(xp, xs, mods, norm_w, w_bf16, tiles_per_sample):
    tp, d = xp.shape
    ts = xs.shape[0]
    n = w_bf16.shape[1]
    npt, nst = tp // ROW_TILE, ts // ROW_TILE
    mod_map = lambda i: (_mod_row(i, 0, npt, tiles_per_sample), 0, 0)
    return pl.pallas_call(
        functools.partial(_inproj0_kernel, n_prompt_tiles=npt),
        grid=(npt + nst,),
        in_specs=[pl.BlockSpec((ROW_TILE, d), lambda i: (jnp.minimum(i, npt - 1), 0)),
                  pl.BlockSpec((ROW_TILE, d), lambda i: (jnp.maximum(i - npt, 0), 0)),
                  pl.BlockSpec((1, 6, d), mod_map),
                  pl.BlockSpec((1, d), lambda i: (0, 0)),
                  pl.BlockSpec((d, n), lambda i: (0, 0))],
        out_specs=pl.BlockSpec((ROW_TILE, n), lambda i: (i, 0)),
        out_shape=jax.ShapeDtypeStruct((tp + ts, n), F32),
        compiler_params=_params(1),
        name="inproj0",
    )(xp, xs, mods, norm_w, w_bf16)


def _scan_kernel(*refs, seq_len, has_state):
    if has_state:
        (p_ref, a2_ref, ab_ref, lb_ref, ong_ref, onh_ref, s0_ref,
         mixed_ref, qf, kf, qb, kb, vv, dec_f, dec_b, o_scr, st_f, st_b) = refs
        sfin_ref = None
    else:
        (p_ref, a2_ref, ab_ref, lb_ref, ong_ref, onh_ref,
         mixed_ref, sfin_ref, qf, kf, qb, kb, vv, dec_f, dec_b, o_scr, st_f, st_b) = refs
        s0_ref = None
    C = SCAN_CHUNK
    n_chunks = seq_len // C
    gqk = GLA_HEADS * HEAD_DK

    row = lax.broadcasted_iota(I32, (C, C), 0)
    col = lax.broadcasted_iota(I32, (C, C), 1)
    lower = col <= row
    upper = col >= row
    tri_lo = jnp.where(lower, 1.0, 0.0).astype(BF16)
    tri_up = jnp.where(upper, 1.0, 0.0).astype(BF16)

    lbp = lb_ref[...]
    lb_max = jnp.maximum(lbp[0], lbp[1])
    lb_e0 = jnp.exp(lbp[0] - lb_max)
    lb_e1 = jnp.exp(lbp[1] - lb_max)
    lb = lb_e0 / (lb_e0 + lb_e1)

    def cumsum_chunk(tri, la):
        hi, lo = _split_bf16(la)
        return _dot(tri, hi) + _dot(tri, lo)

    def prep(n, carry):
        r0 = pl.multiple_of(n * C, C)
        rows = pl.ds(r0, C)
        gq = p_ref[rows, _C_GQ:_C_GQ + gqk] * (HEAD_DK ** -0.5)
        gk = p_ref[rows, _C_GK:_C_GK + gqk]
        hq = _silu(p_ref[rows, _C_HQ:_C_HQ + gqk]) * (HEAD_DK ** -0.5)
        for d_i, (q_s, k_s, dec_s, tri, last) in enumerate(
                ((qf, kf, dec_f, tri_lo, C - 1), (qb, kb, dec_b, tri_up, 0))):
            c_ga = _C_GAF if d_i == 0 else _C_GAB
            c_hf = _C_HFF if d_i == 0 else _C_HFB
            ga = p_ref[rows, c_ga:c_ga + GATE_RANK]
            xg = _dot(ga.astype(BF16), a2_ref[d_i].astype(BF16)) + ab_ref[d_i]
            la_g = _log_sigmoid(xg) / GLA_GATE_NORM
            f = lb[d_i:d_i + 1, :] + (1.0 - lb[d_i:d_i + 1, :]) * jax.nn.sigmoid(
                p_ref[rows, c_hf:c_hf + gqk])
            la_h = jnp.log(f)
            for q, k, la, c0 in ((gq, gk, la_g, 0), (hq, 1.0 - f, la_h, gqk)):
                b = cumsum_chunk(tri, la)
                q_s[rows, c0:c0 + gqk] = (q * jnp.exp(b)).astype(BF16)
                k_s[rows, c0:c0 + gqk] = (k * jnp.exp(-b)).astype(BF16)
                dec_s[n, :, c0:c0 + gqk] = jnp.exp(b[last:last + 1, :])
        vv[rows, 0:512] = p_ref[rows, _C_GV:_C_GV + 512].astype(BF16)
        vv[rows, 512:1024] = p_ref[rows, _C_HI:_C_HI + 512].astype(BF16)
        return carry

    lax.fori_loop(0, n_chunks, prep, 0)

    o_scr[...] = jnp.zeros_like(o_scr)
    for p in range(SCAN_PAIRS):
        if has_state:
            st_f[p] = s0_ref[0, 0, p]
            st_b[p] = s0_ref[0, 1, p]
        else:
            st_f[p] = jnp.zeros((2 * HEAD_DV, 2 * HEAD_DK), F32)
            st_b[p] = jnp.zeros((2 * HEAD_DV, 2 * HEAD_DK), F32)

    first_head = lax.broadcasted_iota(I32, (C, 2 * HEAD_DK), 1) < HEAD_DK
    row2 = lax.broadcasted_iota(I32, (2 * C, C), 0) % C
    col2 = lax.broadcasted_iota(I32, (2 * C, C), 1)
    lower2 = col2 <= row2
    upper2 = col2 >= row2

    def per_head_rows(x):
        z = jnp.zeros_like(x)
        return jnp.concatenate([jnp.where(first_head, x, z), jnp.where(first_head, z, x)], axis=0)

    def add_out(rows, p, res):
        c0 = p * 2 * HEAD_DV
        o_scr[rows, c0:c0 + HEAD_DV] += res[0:C, 0:HEAD_DV]
        o_scr[rows, c0 + HEAD_DV:c0 + 2 * HEAD_DV] += res[C:2 * C, HEAD_DV:2 * HEAD_DV]

    def sweep(n, carry):
        m = n_chunks - 1 - n
        rows = pl.ds(pl.multiple_of(n * C, C), C)
        rows_m = pl.ds(pl.multiple_of(m * C, C), C)
        decay_f, decay_b = dec_f[n], dec_b[m]
        for p in range(SCAN_PAIRS):
            ks = slice(p * 2 * HEAD_DK, (p + 1) * 2 * HEAD_DK)
            vs = slice(p * 2 * HEAD_DV, (p + 1) * 2 * HEAD_DV)
            qd, kd, vh = per_head_rows(qf[rows, ks]), kf[rows, ks], vv[rows, vs]
            s_f = st_f[p]
            sc = (jnp.where(lower2, _dot_nt(qd, kd), 0.0)
                  + jnp.where(upper2, _dot_nt(per_head_rows(qb[rows, ks]), kb[rows, ks]), 0.0))
            add_out(rows, p, _dot_nt(qd, s_f.astype(BF16)) + _dot(sc.astype(BF16), vh))
            st_f[p] = decay_f[:, ks] * (s_f + _dot_tn(vh, kd))
            s_b = st_b[p]
            vm, kbm = vv[rows_m, vs], kb[rows_m, ks]
            add_out(rows_m, p, _dot_nt(per_head_rows(qb[rows_m, ks]), s_b.astype(BF16)))
            st_b[p] = decay_b[:, ks] * (s_b + _dot_tn(vm, kbm))
        return carry

    lax.fori_loop(0, n_chunks, sweep, 0)

    def finish(n, carry):
        rows = pl.ds(pl.multiple_of(n * C, C), C)
        for h in range(SCAN_HEADS):
            vs = slice(h * HEAD_DV, (h + 1) * HEAD_DV)
            if h < GLA_HEADS:
                gate = p_ref[rows, _C_GG + h * HEAD_DV:_C_GG + (h + 1) * HEAD_DV]
                onw = ong_ref[...]
            else:
                hh = h - GLA_HEADS
                gate = p_ref[rows, _C_HG + hh * HEAD_DV:_C_HG + (hh + 1) * HEAD_DV]
                onw = onh_ref[...]
            mixed_ref[rows, vs] = ((_rms(o_scr[rows, vs]) * onw) * _silu(gate)).astype(BF16)
        return carry

    lax.fori_loop(0, n_chunks, finish, 0)

    if sfin_ref is not None:
        for d_i, st in enumerate((st_f, st_b)):
            for p in range(SCAN_PAIRS):
                sfin_ref[0, d_i, 2 * p] = st[p, 0:HEAD_DV, 0:HEAD_DK]
                sfin_ref[0, d_i, 2 * p + 1] = st[p, HEAD_DV:2 * HEAD_DV, HEAD_DK:2 * HEAD_DK]


def _scan(p, row0, batch, seq_len, a2, a_bias, lb, onorm_g, onorm_h, s0=None):
    n = p.shape[1]
    assert row0 % seq_len == 0
    blk0 = row0 // seq_len
    has_state = s0 is not None
    n_chunks = seq_len // SCAN_CHUNK
    st_shape = (1, 2, SCAN_HEADS, HEAD_DV, HEAD_DK)
    pair_shape = (SCAN_PAIRS, 2 * HEAD_DV, 2 * HEAD_DK)
    in_specs = [pl.BlockSpec((seq_len, n), lambda b: (blk0 + b, 0), pipeline_mode=pl.Buffered(1)),
                pl.BlockSpec(a2.shape, lambda b: (0, 0, 0)),
                pl.BlockSpec(a_bias.shape, lambda b: (0, 0, 0)),
                pl.BlockSpec(lb.shape, lambda b: (0, 0, 0)),
                pl.BlockSpec((1, HEAD_DV), lambda b: (0, 0)),
                pl.BlockSpec((1, HEAD_DV), lambda b: (0, 0))]
    args = [p, a2, a_bias, lb, onorm_g, onorm_h]
    mixed_shape = jax.ShapeDtypeStruct((batch * seq_len, D_MODEL), BF16)
    mixed_spec = pl.BlockSpec((seq_len, D_MODEL), lambda b: (b, 0))
    if has_state:
        in_specs.append(pl.BlockSpec((1, 2) + pair_shape, lambda b: (b, 0, 0, 0, 0)))
        args.append(s0)
        out_shape, out_specs = mixed_shape, mixed_spec
    else:
        out_shape = (mixed_shape, jax.ShapeDtypeStruct((batch,) + st_shape[1:], F32))
        out_specs = (mixed_spec, pl.BlockSpec(st_shape, lambda b: (b, 0, 0, 0, 0)))
    scratch = [pltpu.VMEM((seq_len, 512), BF16) for _ in range(4)]
    scratch += [pltpu.VMEM((seq_len, D_MODEL), BF16),
                pltpu.VMEM((n_chunks, 1, 512), F32), pltpu.VMEM((n_chunks, 1, 512), F32),
                pltpu.VMEM((seq_len, D_MODEL), F32),
                pltpu.VMEM(pair_shape, F32), pltpu.VMEM(pair_shape, F32)]
    return pl.pallas_call(
        functools.partial(_scan_kernel, seq_len=seq_len, has_state=has_state),
        grid=(batch,),
        in_specs=in_specs, out_specs=out_specs, out_shape=out_shape,
        scratch_shapes=scratch,
        compiler_params=_params(1),
        name="scan_state" if has_state else "scan_fresh",
    )(*args)


def _post_kernel(*refs, split_x, n_prompt_tiles):
    if split_x:
        xp_ref, xs_ref = refs[0], refs[1]
        refs = refs[2:]
    else:
        x_ref = refs[0]
        refs = refs[1:]
    (mp_ref, ms_ref, mod_ref, nw_ref, wo_ref, wrh_ref, wrl_ref,
     x1_ref, h2_ref, slab_ref, tab_ref, carry) = refs
    i = pl.program_id(0)
    is_prompt = i < n_prompt_tiles
    if split_x:
        x = jnp.where(is_prompt, xp_ref[...], xs_ref[...])
    else:
        x = x_ref[...]
    mixed = jnp.where(is_prompt, mp_ref[...], ms_ref[...])
    x1 = x + mod_ref[0, 2:3, :] * _dot(mixed, wo_ref[...])
    x1_ref[...] = x1
    h2 = _modulate(x1, nw_ref[...], mod_ref[0, 3:4, :], mod_ref[0, 4:5, :])
    _to_token_major(h2_ref, h2)

    hh, hl = _split_bf16(h2)
    logits = _dot(hh, wrh_ref[...]) + _dot(hl, wrh_ref[...]) + _dot(hh, wrl_ref[...])
    tm = logits.shape[0]
    lane = lax.broadcasted_iota(I32, (tm, LANES), 1).astype(F32)

    def first_max(v):
        mx = jnp.max(v, axis=1, keepdims=True)
        idx = jnp.min(jnp.where(v == mx, lane, float(LANES)), axis=1, keepdims=True)
        return mx, idx

    gl = jnp.where(lane < N_GROUPS, logits, NEG_BIG)
    gmax, gidx = first_max(gl)
    g_val = 1.0 / jnp.sum(jnp.exp(gl - gmax), axis=1, keepdims=True)
    lo = N_GROUPS + EXPERTS_PER_GROUP * gidx
    el = jnp.where((lane >= lo) & (lane < lo + EXPERTS_PER_GROUP), logits, NEG_BIG)
    emax, l1 = first_max(el)
    esum = jnp.sum(jnp.exp(el - emax), axis=1, keepdims=True)
    e2max, l2 = first_max(jnp.where(lane == l1, NEG_BIG, el))
    p1 = 1.0 / esum
    p2 = jnp.exp(e2max - emax) / esum
    w1 = g_val * (p1 / (p1 + p2))
    w2 = g_val * (p2 / (p1 + p2))
    id1 = l1 - N_GROUPS
    id2 = l2 - N_GROUPS

    @pl.when(i == 0)
    def _():
        carry[...] = jnp.zeros_like(carry)

    sel1 = lane == id1
    sel2 = lane == id2
    onehot = jnp.where(sel1 | sel2, 1.0, 0.0)
    row = lax.broadcasted_iota(I32, (tm, tm), 0)
    col = lax.broadcasted_iota(I32, (tm, tm), 1)
    earlier = jnp.where(col < row, 1.0, 0.0).astype(BF16)
    before = _dot(earlier, onehot.astype(BF16))
    count = jnp.sum(onehot, axis=0, keepdims=True)
    blocks = jnp.floor((count + (MOVE_BLOCK - 1.0)) * (1.0 / MOVE_BLOCK)) * MOVE_BLOCK
    e_row = lax.broadcasted_iota(I32, (LANES, LANES), 0)
    e_col = lax.broadcasted_iota(I32, (LANES, LANES), 1)
    lower_experts = jnp.where(e_row < e_col, 1.0, 0.0).astype(BF16)
    run_start = _dot(jnp.broadcast_to(blocks, (8, LANES)).astype(BF16), lower_experts)[0:1]
    slot = before + run_start
    q1 = jnp.sum(jnp.where(sel1, slot, 0.0), axis=1, keepdims=True)
    q2 = jnp.sum(jnp.where(sel2, slot, 0.0), axis=1, keepdims=True)
    tab_row = lax.broadcasted_iota(I32, (8, LANES), 0)
    tab_ref[0] = jnp.where(tab_row == 0, count,
                           jnp.where(tab_row == 1, carry[...],
                                     jnp.where(tab_row == 2, run_start, 0.0)))
    carry[...] = carry[...] + count

    slab = jnp.zeros((tm, LANES), F32)
    for k, v in enumerate((id1, id2, w1, w2, q1, q2)):
        slab = jnp.where(lane == k, v, slab)
    slab_ref[...] = slab


def _post(x_args, mixed_p, mixed_s, mods, layer, norm_w, w_out_bf16, wr_hi, wr_lo,
          tiles_per_sample):
    split_x = len(x_args) == 2
    tp, ts = mixed_p.shape[0], mixed_s.shape[0]
    t, d = tp + ts, D_MODEL
    npt, nst = tp // ROW_TILE, ts // ROW_TILE
    tile = lambda i: (i, 0)
    if split_x:
        x_specs = [pl.BlockSpec((ROW_TILE, d), lambda i: (jnp.minimum(i, npt - 1), 0)),
                   pl.BlockSpec((ROW_TILE, d), lambda i: (jnp.maximum(i - npt, 0), 0))]
    else:
        x_specs = [pl.BlockSpec((ROW_TILE, d), tile)]
    in_specs = x_specs + [
        pl.BlockSpec((ROW_TILE, d), lambda i: (jnp.minimum(i, npt - 1), 0)),
        pl.BlockSpec((ROW_TILE, d), lambda i: (jnp.maximum(i - npt, 0), 0)),
        pl.BlockSpec((1, 6, d), lambda i: (_mod_row(i, layer, npt, tiles_per_sample), 0, 0)),
        pl.BlockSpec((1, d), lambda i: (0, 0)),
        pl.BlockSpec((d, d), lambda i: (0, 0)),
        pl.BlockSpec((d, LANES), lambda i: (0, 0)),
        pl.BlockSpec((d, LANES), lambda i: (0, 0))]
    return pl.pallas_call(
        functools.partial(_post_kernel, split_x=split_x, n_prompt_tiles=npt),
        grid=(npt + nst,),
        in_specs=in_specs,
        out_specs=(pl.BlockSpec((ROW_TILE, d), tile),
                   pl.BlockSpec((ROW_TILE * TOKEN_ROWS, LANES), tile),
                   pl.BlockSpec((ROW_TILE, LANES), tile),
                   pl.BlockSpec((1, 8, LANES), lambda i: (i, 0, 0))),
        out_shape=(jax.ShapeDtypeStruct((t, d), F32),
                   jax.ShapeDtypeStruct((t * TOKEN_ROWS, LANES), F32),
                   jax.ShapeDtypeStruct((t, LANES), F32),
                   jax.ShapeDtypeStruct((npt + nst, 8, LANES), F32)),
        scratch_shapes=[pltpu.VMEM((1, LANES), F32)],
        compiler_params=_params(1),
        name=f"post{layer}",
    )(*x_args, mixed_p, mixed_s, mods, norm_w, w_out_bf16, wr_hi, wr_lo)


def _for_blocks(tab_ref, fn):
    block_rows = MOVE_BLOCK * TOKEN_ROWS
    count = tab_ref[0, 0, MAX_BLOCKS]

    def call(k, parity):
        fn(pl.multiple_of(k * block_rows, block_rows),
           pl.multiple_of(tab_ref[0, 0, k], TOKEN_ROWS), parity)

    def body(k2, c):
        call(2 * k2, 0)

        @pl.when(2 * k2 + 1 < count)
        def _():
            call(2 * k2 + 1, 1)
        return c

    lax.fori_loop(0, _cdiv(count, 2), body, 0)


def _wait_blocks(tab_ref, copy):
    def body(k, c):
        copy.wait()
        return c

    lax.fori_loop(0, tab_ref[0, 0, MAX_BLOCKS], body, 0)


def _dispatch_kernel(zero_ref, tab_ref, prev_tab_ref, q_ref, h2_ref, hs_ref, zero_buf, stage, sem):
    j = pl.program_id(0)
    slot = j % 2
    block_rows = MOVE_BLOCK * TOKEN_ROWS

    @pl.when(j == 0)
    def _():
        zero_buf[...] = jnp.zeros_like(zero_buf)

        def zero_copy(k):
            start = pl.multiple_of(zero_ref[k], EXPERT_TILE * TOKEN_ROWS)
            return pltpu.make_async_copy(
                zero_buf, hs_ref.at[pl.ds(start, EXPERT_TILE * TOKEN_ROWS)], sem.at[0])

        def start_zero(k, c):
            @pl.when(zero_ref[k] >= 0)
            def _():
                zero_copy(k).start()
            return c

        def wait_zero(k, c):
            @pl.when(zero_ref[k] >= 0)
            def _():
                zero_copy(k).wait()
            return c

        lax.fori_loop(0, zero_ref.shape[0], start_zero, 0)
        lax.fori_loop(0, zero_ref.shape[0], wait_zero, 0)

        stage[...] = jnp.zeros_like(stage)

    def place(r, c):
        tok = h2_ref[pl.ds(pl.multiple_of(r * TOKEN_ROWS, TOKEN_ROWS), TOKEN_ROWS), :]
        for s in range(2):
            row = pl.multiple_of(q_ref[0, 0, 2 * r + s], TOKEN_ROWS)
            stage[slot, pl.ds(row, TOKEN_ROWS), :] = tok
        return c

    lax.fori_loop(0, ROW_TILE, place, 0, unroll=8)

    def block_copy(buf, stage_row, sorted_row):
        return pltpu.make_async_copy(stage.at[buf, pl.ds(stage_row, block_rows)],
                                     hs_ref.at[pl.ds(sorted_row, block_rows)], sem.at[buf])

    @pl.when(j > 0)
    def _():
        _wait_blocks(prev_tab_ref, block_copy(1 - slot, 0, 0))

    _for_blocks(tab_ref, lambda a, b, parity: block_copy(slot, a, b).start(priority=parity))

    @pl.when(j == pl.num_programs(0) - 1)
    def _():
        _wait_blocks(tab_ref, block_copy(slot, 0, 0))


def _dispatch(zero_tiles, block_tab, slots, h2, n_rows):
    t = h2.shape[0] // TOKEN_ROWS
    nt = t // ROW_TILE
    smem_tile = lambda shape: pl.BlockSpec((1,) + shape, lambda j, *_: (j, 0, 0),
                                           memory_space=pltpu.SMEM)
    grid_spec = pltpu.PrefetchScalarGridSpec(
        num_scalar_prefetch=1,
        grid=(nt,),
        in_specs=[smem_tile((1, LANES)),
                  pl.BlockSpec((1, 1, LANES), lambda j, *_: (jnp.maximum(j - 1, 0), 0, 0),
                               memory_space=pltpu.SMEM),
                  smem_tile((1, 2 * ROW_TILE)),
                  pl.BlockSpec((ROW_TILE * TOKEN_ROWS, LANES), lambda j, *_: (j, 0))],
        out_specs=pl.BlockSpec(memory_space=pl.ANY),
        scratch_shapes=[pltpu.VMEM((EXPERT_TILE * TOKEN_ROWS, LANES), F32),
                        pltpu.VMEM((2, STAGE_TOKENS * TOKEN_ROWS, LANES), F32),
                        pltpu.SemaphoreType.DMA((2,))])
    return pl.pallas_call(
        _dispatch_kernel,
        grid_spec=grid_spec,
        out_shape=jax.ShapeDtypeStruct((n_rows * TOKEN_ROWS, LANES), F32),
        compiler_params=_params(1),
        name="dispatch",
    )(zero_tiles, block_tab, block_tab, slots, h2)


def _expert_kernel(te_ref, src_ref, nv_ref, hs_ref, w1_ref, w3_ref, w2_ref, ys_ref,
                   w1b, w3b, w2b):
    i = pl.program_id(0)
    prev = te_ref[jnp.maximum(i - 1, 0)]

    @pl.when((i == 0) | (te_ref[i] != prev))
    def _():
        w1b[...] = w1_ref[0, 0].astype(BF16)
        w3b[...] = w3_ref[0, 0].astype(BF16)
        w2b[...] = w2_ref[0, 0].astype(BF16)

    @pl.when(nv_ref[i] > 0)
    def _():
        h = _from_token_major(hs_ref, EXPERT_TILE).astype(BF16)
        g = _silu(_dot(h, w1b[...])) * _dot(h, w3b[...])
        _to_token_major(ys_ref, _dot(g.astype(BF16), w2b[...]))

    @pl.when(nv_ref[i] == 0)
    def _():
        ys_ref[...] = jnp.zeros_like(ys_ref)


def _experts(tile_expert, tile_src, tile_rows, hs, w1, w3, w2, layer):
    n_rows, d = hs.shape[0] // TOKEN_ROWS, D_MODEL
    nt = n_rows // EXPERT_TILE
    hid = w1.shape[-1]
    tok_tile = (EXPERT_TILE * TOKEN_ROWS, LANES)
    row_map = lambda i, te, src, nv: (src[i], 0)
    grid_spec = pltpu.PrefetchScalarGridSpec(
        num_scalar_prefetch=3,
        grid=(nt,),
        in_specs=[pl.BlockSpec(tok_tile, row_map),
                  pl.BlockSpec((1, 1, d, hid), lambda i, te, src, nv: (layer, te[i], 0, 0)),
                  pl.BlockSpec((1, 1, d, hid), lambda i, te, src, nv: (layer, te[i], 0, 0)),
                  pl.BlockSpec((1, 1, hid, d), lambda i, te, src, nv: (layer, te[i], 0, 0))],
        out_specs=pl.BlockSpec(tok_tile, lambda i, te, src, nv: (i, 0)),
        scratch_shapes=[pltpu.VMEM((d, hid), BF16), pltpu.VMEM((d, hid), BF16),
                        pltpu.VMEM((hid, d), BF16)])
    return pl.pallas_call(
        _expert_kernel,
        grid_spec=grid_spec,
        out_shape=jax.ShapeDtypeStruct(hs.shape, F32),
        compiler_params=_params(1),
        name=f"experts{layer}",
    )(tile_expert, tile_src, tile_rows, hs, w1, w3, w2)


def _combine_kernel(tab_ref, next_tab_ref, q_ref, w_ref, x1_ref, mod_ref, fw_ref, ys_ref, out_ref,
                    stage, y_tok, sem, *, final_norm):
    i = pl.program_id(0)
    slot = i % 2
    block_rows = MOVE_BLOCK * TOKEN_ROWS

    def block_copy(buf, stage_row, sorted_row):
        return pltpu.make_async_copy(ys_ref.at[pl.ds(sorted_row, block_rows)],
                                     stage.at[buf, pl.ds(stage_row, block_rows)], sem.at[buf])

    def fetch(tab, buf):
        _for_blocks(tab, lambda a, b, parity: block_copy(buf, a, b).start(priority=parity))

    @pl.when(i == 0)
    def _():
        fetch(tab_ref, slot)

    @pl.when(i + 1 < pl.num_programs(0))
    def _():
        fetch(next_tab_ref, 1 - slot)

    _wait_blocks(tab_ref, block_copy(slot, 0, 0))

    def pick(r, c):
        rows = [stage[slot, pl.ds(pl.multiple_of(q_ref[0, 0, 2 * r + s], TOKEN_ROWS), TOKEN_ROWS), :]
                for s in range(2)]
        y_tok[pl.ds(pl.multiple_of(r * TOKEN_ROWS, TOKEN_ROWS), TOKEN_ROWS), :] = (
            w_ref[0, 0, 2 * r] * rows[0] + w_ref[0, 0, 2 * r + 1] * rows[1])
        return c

    lax.fori_loop(0, ROW_TILE, pick, 0, unroll=8)
    x2 = x1_ref[...] + mod_ref[0, 5:6, :] * _from_token_major(y_tok, ROW_TILE)
    if final_norm:
        x2 = _rms(x2) * fw_ref[...]
    out_ref[...] = x2


def _combine(block_tab, slots, weights, x1, mods, layer, final_w, ys, tile0, n_tiles,
             n_prompt_tiles, tiles_per_sample, final_norm):
    d = D_MODEL
    tile = lambda i: (tile0 + i, 0)
    mod_map = lambda i: (_mod_row(tile0 + i, layer, n_prompt_tiles, tiles_per_sample), 0, 0)
    smem_tile = lambda shape: pl.BlockSpec((1,) + shape, lambda i: (tile0 + i, 0, 0),
                                           memory_space=pltpu.SMEM)
    return pl.pallas_call(
        functools.partial(_combine_kernel, final_norm=final_norm),
        grid=(n_tiles,),
        in_specs=[smem_tile((1, LANES)),
                  pl.BlockSpec((1, 1, LANES),
                               lambda i: (tile0 + jnp.minimum(i + 1, n_tiles - 1), 0, 0),
                               memory_space=pltpu.SMEM),
                  smem_tile((1, 2 * ROW_TILE)), smem_tile((1, 2 * ROW_TILE)),
                  pl.BlockSpec((ROW_TILE, d), tile),
                  pl.BlockSpec((1, 6, d), mod_map),
                  pl.BlockSpec((1, d), lambda i: (0, 0)),
                  pl.BlockSpec(memory_space=pl.ANY)],
        out_specs=pl.BlockSpec((ROW_TILE, d), lambda i: (i, 0)),
        out_shape=jax.ShapeDtypeStruct((n_tiles * ROW_TILE, d), F32),
        scratch_shapes=[pltpu.VMEM((2, STAGE_TOKENS * TOKEN_ROWS, LANES), F32),
                        pltpu.VMEM((ROW_TILE * TOKEN_ROWS, LANES), F32),
                        pltpu.SemaphoreType.DMA((2,))],
        compiler_params=_params(1),
        name=f"combine{layer}_{tile0}",
    )(block_tab, block_tab, slots, weights, x1, mods, final_w, ys)


def _moe(h2, slab, tile_tab, w1, w3, w2, layer):
    t = h2.shape[0] // TOKEN_ROWS
    n_tiles = t // ROW_TILE
    extra_tiles = N_EXPERTS + _cdiv(N_EXPERTS * MOVE_BLOCK, EXPERT_TILE)
    n_rows = 2 * t + extra_tiles * EXPERT_TILE
    nt = n_rows // EXPERT_TILE
    tab = tile_tab[:, :, :N_EXPERTS].astype(I32)
    cnt = tab[-1, 0] + tab[-1, 1]
    tight = _cdiv(cnt, EXPERT_TILE) * EXPERT_TILE
    padded = jnp.where(cnt > 0, _cdiv(cnt + MOVE_BLOCK - 1, EXPERT_TILE) * EXPERT_TILE, 0)
    ends = jnp.cumsum(padded)
    offsets = ends - padded
    tails = jnp.where(cnt > 0, ends - EXPERT_TILE, -1)
    tails2 = jnp.where(padded > tight, ends - 2 * EXPERT_TILE, -1)
    used = ends[-1] // EXPERT_TILE
    tile_start = jnp.arange(nt, dtype=I32) * EXPERT_TILE
    unused = (used + jnp.arange(extra_tiles, dtype=I32)) * EXPERT_TILE
    zero_tiles = jnp.concatenate([tails, tails2, jnp.where(unused < n_rows, unused, -1)])
    zero_tiles = jnp.where(zero_tiles >= 0, zero_tiles * TOKEN_ROWS, -1).astype(I32)
    tile_src = jnp.minimum(jnp.arange(nt, dtype=I32), used - 1)
    tile_expert = jnp.sum((tile_src * EXPERT_TILE)[:, None] >= ends[None, :], axis=1).astype(I32)
    tile_rows = jnp.where(tile_start < ends[-1],
                          jnp.clip(cnt[tile_expert] - (tile_start - offsets[tile_expert]),
                                   0, EXPERT_TILE), 0).astype(I32)
    n_blocks = _cdiv(tab[:, 0], MOVE_BLOCK)
    blocks_through = jnp.cumsum(n_blocks, axis=1)
    k = jnp.arange(MAX_BLOCKS, dtype=I32)
    owner = jnp.sum(blocks_through[:, None, :] <= k[None, :, None], axis=2)
    is_owner = owner[:, :, None] == jnp.arange(N_EXPERTS, dtype=I32)[None, None, :]
    pick = lambda v: jnp.sum(jnp.where(is_owner, v[:, None, :], 0), axis=2)
    run_first = pick(offsets[None, :] + tab[:, 1])
    block_in_run = k[None, :] - pick(blocks_through - n_blocks)
    sorted_row = (run_first + block_in_run * MOVE_BLOCK) * TOKEN_ROWS
    block_tab = jnp.concatenate(
        [sorted_row, blocks_through[:, -1:],
         jnp.zeros((n_tiles, LANES - MAX_BLOCKS - 1), I32)], axis=1).astype(I32)[:, None, :]
    per_tile = lambda cols: cols.reshape(n_tiles, 1, 2 * ROW_TILE)
    slots = per_tile(slab[:, 4:6]).astype(I32) * TOKEN_ROWS
    weights = per_tile(slab[:, 2:4])
    hs = _dispatch(zero_tiles, block_tab, slots, h2, n_rows)
    ys = _experts(tile_expert, tile_src, tile_rows, hs, w1, w3, w2, layer)
    return ys, (block_tab, slots, weights)


def _rope(x, cos, sin_signed):
    lane = lax.broadcasted_iota(I32, (x.shape[0], LANES), 1)
    low = (lane % 32) < 16
    outs = []
    for j in range(x.shape[1] // LANES):
        xb = x[:, j * LANES:(j + 1) * LANES]
        partner = jnp.where(low, pltpu.roll(xb, LANES - 16, 1), pltpu.roll(xb, 16, 1))
        outs.append(xb * cos + partner * sin_signed)
    return jnp.concatenate(outs, axis=1)


def _inproj1_prompt_kernel(x_ref, mod_ref, nw_ref, w_ref, q_ref, k_ref, v_ref):
    d = D_MODEL
    h = _modulate(x_ref[...], nw_ref[...], mod_ref[0, 0:1, :], mod_ref[0, 1:2, :]).astype(BF16)
    q_ref[...] = (_dot(h, w_ref[:, 0:d]) * (DIFF_HD ** -0.5)).astype(BF16)
    k_ref[...] = _dot(h, w_ref[:, d:2 * d])
    v_ref[...] = _dot(h, w_ref[:, 2 * d:3 * d])


def _inproj1_sample_kernel(x_ref, mod_ref, nw_ref, w_ref, cos_ref, sin_ref, q_ref, k_ref, v_ref):
    d = D_MODEL
    h = _modulate(x_ref[...], nw_ref[...], mod_ref[0, 0:1, :], mod_ref[0, 1:2, :]).astype(BF16)
    cos, sin = cos_ref[...], sin_ref[...]
    q_ref[...] = (_rope(_dot(h, w_ref[:, 0:d]), cos, sin) * (DIFF_HD ** -0.5)).astype(BF16)
    k_ref[...] = _rope(_dot(h, w_ref[:, d:2 * d]), cos, sin).astype(BF16)
    v_ref[...] = _dot(h, w_ref[:, 2 * d:3 * d]).astype(BF16)


def _inproj1(x, mods, norm_w, w_bf16, n_prompt_tiles, n_sample_tiles, tiles_per_sample,
             cos_t, sin_t):
    d = D_MODEL
    npt, nst = n_prompt_tiles, n_sample_tiles
    common = [pl.BlockSpec((1, d), lambda i: (0, 0)), pl.BlockSpec((d, 3 * d), lambda i: (0, 0))]
    tile = lambda i: (i, 0)
    out_specs = tuple(pl.BlockSpec((ROW_TILE, d), tile) for _ in range(3))
    qp, kp, vp = pl.pallas_call(
        _inproj1_prompt_kernel,
        grid=(npt,),
        in_specs=[pl.BlockSpec((ROW_TILE, d), tile),
                  pl.BlockSpec((1, 6, d), lambda i: (8, 0, 0))] + common,
        out_specs=out_specs,
        out_shape=(jax.ShapeDtypeStruct((npt * ROW_TILE, d), BF16),
                   jax.ShapeDtypeStruct((npt * ROW_TILE, d), F32),
                   jax.ShapeDtypeStruct((npt * ROW_TILE, d), F32)),
        compiler_params=_params(1),
        name="inproj1_prompt",
    )(x, mods, norm_w, w_bf16)
    rope_tile = lambda i: (i % tiles_per_sample, 0)
    qs, ks, vs = pl.pallas_call(
        _inproj1_sample_kernel,
        grid=(nst,),
        in_specs=[pl.BlockSpec((ROW_TILE, d), lambda i: (npt + i, 0)),
                  pl.BlockSpec((1, 6, d), lambda i: (8 + 1 + i // tiles_per_sample, 0, 0))]
        + common + [pl.BlockSpec((ROW_TILE, LANES), rope_tile),
                    pl.BlockSpec((ROW_TILE, LANES), rope_tile)],
        out_specs=out_specs,
        out_shape=tuple(jax.ShapeDtypeStruct((nst * ROW_TILE, d), BF16) for _ in range(3)),
        compiler_params=_params(1),
        name="inproj1_sample",
    )(x, mods, norm_w, w_bf16, cos_t, sin_t)
    return (qp, kp, vp), (qs, ks, vs)


def _rope_tables(n_tok):
    half = DIFF_HD // 4
    pos = np.arange(n_tok)
    lane = np.arange(LANES)
    sub = lane % DIFF_HD
    p = np.where(sub[None, :] < DIFF_HD // 2, (pos // GRID_W)[:, None], (pos % GRID_W)[:, None])
    inv = jnp.asarray(ROPE_THETA, F32) ** (-jnp.asarray(sub % half, F32) / half)
    ang = jnp.asarray(p, F32) * inv[None, :]
    sign = np.where((lane % (2 * half)) < half, -1.0, 1.0).astype(np.float32)
    return jnp.cos(ang), jnp.sin(ang) * sign[None, :]


def _diffattn_kernel(*refs, has_cache, lam_init):
    if has_cache:
        q_ref, k_ref, v_ref, ck_ref, cv_ref, lam_ref, sw_ref, o_ref = refs
    else:
        q_ref, k_ref, v_ref, lam_ref, sw_ref, o_ref = refs
    hd2 = 2 * DIFF_HD
    lv = lam_ref[...]
    lam = (jnp.exp(jnp.sum(lv[0:1] * lv[1:2], axis=1, keepdims=True))
           - jnp.exp(jnp.sum(lv[2:3] * lv[3:4], axis=1, keepdims=True)) + lam_init)
    lane = lax.broadcasted_iota(I32, (q_ref.shape[0], hd2), 1)
    for h in range(DIFF_HEADS):
        cols = slice(h * hd2, (h + 1) * hd2)
        q = q_ref[:, cols]
        zero = jnp.zeros_like(q)
        keys = [(k_ref[:, cols].astype(BF16), v_ref[:, cols].astype(BF16))]
        if has_cache:
            keys.append((ck_ref[:, cols].astype(BF16), cv_ref[:, cols].astype(BF16)))
        o = None
        for c in range(2):
            qc = jnp.where((lane < DIFF_HD) == (c == 0), q, zero)
            s = [_dot_nt(qc, k) for k, _ in keys]
            mx = functools.reduce(jnp.maximum, [jnp.max(si, axis=1, keepdims=True) for si in s])
            e = [jnp.exp(si - mx) for si in s]
            z = functools.reduce(jnp.add, [jnp.sum(ei, axis=1, keepdims=True) for ei in e])
            pv = functools.reduce(jnp.add,
                                  [_dot(ei.astype(BF16), v) for ei, (_, v) in zip(e, keys)])
            pv = pv * (1.0 / z)
            o = pv if c == 0 else o - lam * pv
        o_ref[:, cols] = ((_rms(o) * sw_ref[...]) * (1.0 - lam_init)).astype(BF16)


def _diffattn(q, k, v, lam_vecs, subln_w, batch, seq_len, q_block, lam_init, cache=None):
    d = D_MODEL
    nq = seq_len // q_block
    has_cache = cache is not None
    kv_spec = pl.BlockSpec((seq_len, d), lambda b, qi: (b, 0))
    in_specs = [pl.BlockSpec((q_block, d), lambda b, qi: (b * nq + qi, 0)), kv_spec, kv_spec]
    args = [q, k, v]
    if has_cache:
        past = cache[0].shape[0] // batch
        c_spec = pl.BlockSpec((past, d), lambda b, qi: (b, 0))
        in_specs += [c_spec, c_spec]
        args += list(cache)
    in_specs += [pl.BlockSpec((4, DIFF_HD), lambda b, qi: (0, 0)),
                 pl.BlockSpec((1, 2 * DIFF_HD), lambda b, qi: (0, 0))]
    args += [lam_vecs, subln_w]
    return pl.pallas_call(
        functools.partial(_diffattn_kernel, has_cache=has_cache, lam_init=lam_init),
        grid=(batch, nq),
        in_specs=in_specs,
        out_specs=pl.BlockSpec((q_block, d), lambda b, qi: (b * nq + qi, 0)),
        out_shape=jax.ShapeDtypeStruct((batch * seq_len, d), BF16),
        compiler_params=_params(2),
        name="diffattn_cache" if has_cache else "diffattn",
    )(*args)


def _router_weights(router_group, router_expert):
    w = jnp.concatenate([router_group, router_expert], axis=1)
    w = jnp.pad(w, ((0, 0), (0, LANES - w.shape[1])))
    hi = w.astype(BF16)
    return hi, (w - hi.astype(F32)).astype(BF16)


def _inproj0_weights(w_in):
    gq, gk, gv, gg, gaf, gab, hq, hff, hfb, hi, hg = jnp.split(
        w_in, [256, 512, 1024, 1536, 1552, 1568, 1824, 2080, 2336, 2848], axis=1)
    w = jnp.concatenate([gq, gk, gv, gg, hq, hff, hfb, hi, hg, gaf, gab], axis=1)
    return jnp.pad(w, ((0, 0), (0, AB_COLS - w.shape[1]))).astype(BF16)


def kernel(x_prompt, x_sample, state_gla, state_hgrn, cache_diff_k, cache_diff_v, c, c_ctx,
           w_ada, b_ada, norm1_w, norm2_w, w_in_ab, gla_a2, gla_a_bias, hgrn_lb, gla_onorm_w,
           hgrn_onorm_w, w_out_ab, w_in_c, lam_q1, lam_k1, lam_q2, lam_k2, diff_subln_w, w_out_c,
           router_group, router_expert, moe_w1, moe_w3, moe_w2, final_norm_w):
    bp, lp, d = x_prompt.shape
    bs, ls, _ = x_sample.shape
    depth = w_ada.shape[0]
    assert depth == 2 and d == D_MODEL and bs <= 7
    tp, ts = bp * lp, bs * ls
    npt, nst = tp // ROW_TILE, ts // ROW_TILE
    tps = ls // ROW_TILE
    xp = x_prompt.reshape(tp, d)
    xs = x_sample.reshape(ts, d)

    cond8 = jnp.concatenate([c_ctx[None, :], c, jnp.zeros((7 - bs, d), F32)], axis=0)
    mods = _adaln(cond8, w_ada, b_ada).reshape(depth * 8, 6, d)

    proj = _inproj0(xp, xs, mods, norm1_w[0:1], _inproj0_weights(w_in_ab[0]), tps)
    a_bias = gla_a_bias[0][:, None, :]
    scan_args = (gla_a2[0], a_bias, hgrn_lb, gla_onorm_w[0:1], hgrn_onorm_w[0:1])
    mixed_p, s_fin = _scan(proj, 0, bp, lp, *scan_args)
    s0 = jnp.concatenate([state_gla[:, 0], state_hgrn[:, 0]], axis=2).swapaxes(-1, -2)
    s0 = s0.reshape(bs, 2, SCAN_PAIRS, 2, HEAD_DV, HEAD_DK)
    zero = jnp.zeros_like(s0[:, :, :, 0])
    s0 = jnp.concatenate([jnp.concatenate([s0[:, :, :, 0], zero], axis=-1),
                          jnp.concatenate([zero, s0[:, :, :, 1]], axis=-1)], axis=-2)
    mixed_s = _scan(proj, tp, bs, ls, *scan_args, s0=s0)
    s_fin = s_fin.swapaxes(-1, -2)
    new_state_gla = s_fin[:, None, :, :GLA_HEADS]
    new_state_hgrn = s_fin[:, None, :, GLA_HEADS:]

    wr = _router_weights(router_group[0], router_expert[0])
    x1, h2, slab, tile_tab = _post((xp, xs), mixed_p, mixed_s, mods, 0, norm2_w[0:1],
                                   w_out_ab[0].astype(BF16), *wr, tps)
    ys, tables = _moe(h2, slab, tile_tab, moe_w1, moe_w3, moe_w2, 0)
    x2 = _combine(*tables, x1, mods, 0, final_norm_w[None, :], ys, 0, npt + nst, npt, tps, False)

    lam_init = 0.8 - 0.6 * math.exp(-0.3 * 1)
    cos_t, sin_t = _rope_tables(ls)
    (qp, kp, vp), (qs, ks, vs) = _inproj1(x2, mods, norm1_w[1:2], w_in_c[0].astype(BF16),
                                          npt, nst, tps, cos_t, sin_t)
    lam_vecs = jnp.stack([lam_q1[0], lam_k1[0], lam_q2[0], lam_k2[0]])
    att_p = _diffattn(qp, kp, vp, lam_vecs, diff_subln_w[0:1], bp, lp, lp, lam_init)
    past = cache_diff_k.shape[2]
    cache = (cache_diff_k[:, 0].reshape(bs * past, d), cache_diff_v[:, 0].reshape(bs * past, d))
    att_s = _diffattn(qs, ks, vs, lam_vecs, diff_subln_w[0:1], bs, ls, ROW_TILE, lam_init, cache)

    wr = _router_weights(router_group[1], router_expert[1])
    x3, h2, slab, tile_tab = _post((x2,), att_p, att_s, mods, 1, norm2_w[1:2],
                                   w_out_c[0].astype(BF16), *wr, tps)
    ys, tables = _moe(h2, slab, tile_tab, moe_w1, moe_w3, moe_w2, 1)
    fw = final_norm_w[None, :]
    y_p = _combine(*tables, x3, mods, 1, fw, ys, 0, npt, npt, tps, True)
    y_s = _combine(*tables, x3, mods, 1, fw, ys, npt, nst, npt, tps, True)

    return (y_p.reshape(bp, lp, d), y_s.reshape(bs, ls, d), new_state_gla, new_state_hgrn,
            kp.reshape(bp, 1, lp, DIFF_HEADS, 2, DIFF_HD),
            vp.reshape(bp, 1, lp, DIFF_HEADS, 2 * DIFF_HD))
```

```python
import functools
import math

import jax
import jax.numpy as jnp
import numpy as np
from jax import lax
from jax.experimental import pallas as pl
from jax.experimental.pallas import tpu as pltpu

F32 = jnp.float32
BF16 = jnp.bfloat16
I32 = jnp.int32

D_MODEL = 1024
GLA_HEADS = 4
HGRN_HEADS = 4
SCAN_HEADS = GLA_HEADS + HGRN_HEADS
SCAN_PAIRS = SCAN_HEADS // 2
HEAD_DK = 64
HEAD_DV = 128
GATE_RANK = 16
GLA_GATE_NORM = 16.0
DIFF_HEADS = 8
DIFF_HD = 64
GRID_W = 64
ROPE_THETA = 10000.0
N_GROUPS = 4
EXPERTS_PER_GROUP = 8
N_EXPERTS = N_GROUPS * EXPERTS_PER_GROUP
MOE_HIDDEN = 512
EPS = 1e-6
LANES = 128
TOKEN_ROWS = D_MODEL // LANES
NEG_BIG = -1e30

ROW_TILE = 256
SCAN_CHUNK = 64
EXPERT_TILE = 256
MOVE_BLOCK = 16
STAGE_TOKENS = 2 * ROW_TILE + N_EXPERTS * MOVE_BLOCK
MAX_BLOCKS = STAGE_TOKENS // MOVE_BLOCK
VMEM_LIMIT = 56 * 1024 * 1024

_C_GQ, _C_GK, _C_GV, _C_GG = 0, 256, 512, 1024
_C_HQ, _C_HFF, _C_HFB, _C_HI, _C_HG = 1536, 1792, 2048, 2304, 2816
_C_GAF, _C_GAB = 3328, 3344
AB_COLS = 3456


def _params(n_axes, vmem=VMEM_LIMIT):
    return pltpu.CompilerParams(dimension_semantics=("arbitrary",) * n_axes,
                                vmem_limit_bytes=vmem)


def _cdiv(a, b):
    return (a + b - 1) // b


def _dot(a, b):
    return jnp.dot(a, b, preferred_element_type=F32)


def _dot_nt(a, b):
    return lax.dot_general(a, b, (((1,), (1,)), ((), ())), preferred_element_type=F32)


def _dot_tn(a, b):
    return lax.dot_general(a, b, (((0,), (0,)), ((), ())), preferred_element_type=F32)


def _split_bf16(x):
    hi = x.astype(BF16)
    lo = (x - hi.astype(F32)).astype(BF16)
    return hi, lo


def _silu(x):
    return x * jax.nn.sigmoid(x)


def _log_sigmoid(x):
    return jnp.minimum(x, 0.0) - jnp.log(1.0 + jnp.exp(-jnp.abs(x)))


def _rms(x):
    return x * lax.rsqrt(jnp.mean(x * x, axis=-1, keepdims=True) + EPS)


def _modulate(x, norm_w, shift, scale):
    return (_rms(x) * norm_w) * (1.0 + scale) + shift


def _to_token_major(dst_ref, x, row0=0):
    n = x.shape[0]
    for s in range(TOKEN_ROWS):
        dst_ref[pl.ds(row0 + s, n, stride=TOKEN_ROWS), :] = x[:, s * LANES:(s + 1) * LANES]


def _from_token_major(src_ref, n, row0=0):
    return jnp.concatenate([src_ref[pl.ds(row0 + s, n, stride=TOKEN_ROWS), :]
                            for s in range(TOKEN_ROWS)], axis=1)


def _ada_kernel(c_ref, w_ref, b_ref, o_ref):
    s = _silu(c_ref[...])
    o_ref[0] = _dot(s.astype(BF16), w_ref[0].astype(BF16)) + b_ref[0]


def _adaln(cond8, w_ada, b_ada):
    depth, d, n = w_ada.shape
    tn = 1536
    return pl.pallas_call(
        _ada_kernel,
        grid=(depth, n // tn),
        in_specs=[pl.BlockSpec((8, d), lambda l, j: (0, 0)),
                  pl.BlockSpec((1, d, tn), lambda l, j: (l, 0, j)),
                  pl.BlockSpec((1, 1, tn), lambda l, j: (l, 0, j))],
        out_specs=pl.BlockSpec((1, 8, tn), lambda l, j: (l, 0, j)),
        out_shape=jax.ShapeDtypeStruct((depth, 8, n), F32),
        compiler_params=_params(2),
        name="adaln",
    )(cond8, w_ada, b_ada.reshape(depth, 1, n))


def _mod_row(i, layer, n_prompt_tiles, tiles_per_sample):
    r = jnp.where(i < n_prompt_tiles, 0, 1 + (i - n_prompt_tiles) // tiles_per_sample)
    return layer * 8 + r


def _inproj0_kernel(xp_ref, xs_ref, mod_ref, nw_ref, w_ref, o_ref, *, n_prompt_tiles):
    i = pl.program_id(0)
    x = jnp.where(i < n_prompt_tiles, xp_ref[...], xs_ref[...])
    h = _modulate(x, nw_ref[...], mod_ref[0, 0:1, :], mod_ref[0, 1:2, :])
    o_ref[...] = _dot(h.astype(BF16), w_ref[...])


def _inproj0(xp, xs, mods, norm_w, w_bf16, tiles_per_sample):
    tp, d = xp.shape
    ts = xs.shape[0]
    n = w_bf16.shape[1]
    npt, nst = tp // ROW_TILE, ts // ROW_TILE
    mod_map = lambda i: (_mod_row(i, 0, npt, tiles_per_sample), 0, 0)
    return pl.pallas_call(
        functools.partial(_inproj0_kernel, n_prompt_tiles=npt),
        grid=(npt + nst,),
        in_specs=[pl.BlockSpec((ROW_TILE, d), lambda i: (jnp.minimum(i, npt - 1), 0)),
                  pl.BlockSpec((ROW_TILE, d), lambda i: (jnp.maximum(i - npt, 0), 0)),
                  pl.BlockSpec((1, 6, d), mod_map),
                  pl.BlockSpec((1, d), lambda i: (0, 0)),
                  pl.BlockSpec((d, n), lambda i: (0, 0))],
        out_specs=pl.BlockSpec((ROW_TILE, n), lambda i: (i, 0)),
        out_shape=jax.ShapeDtypeStruct((tp + ts, n), F32),
        compiler_params=_params(1),
        name="inproj0",
    )(xp, xs, mods, norm_w, w_bf16)


def _scan_kernel(*refs, seq_len, has_state):
    if has_state:
        (p_ref, a2_ref, ab_ref, lb_ref, ong_ref, onh_ref, s0_ref,
         mixed_ref, qf, kf, qb, kb, vv, dec_f, dec_b, o_f, o_b, st_f, st_b) = refs
        sfin_ref = None
    else:
        (p_ref, a2_ref, ab_ref, lb_ref, ong_ref, onh_ref,
         mixed_ref, sfin_ref, qf, kf, qb, kb, vv, dec_f, dec_b, o_f, o_b, st_f, st_b) = refs
        s0_ref = None
    C = SCAN_CHUNK
    n_chunks = seq_len // C
    gqk = GLA_HEADS * HEAD_DK

    row = lax.broadcasted_iota(I32, (C, C), 0)
    col = lax.broadcasted_iota(I32, (C, C), 1)
    lower = col <= row
    upper = col >= row
    tri_lo = jnp.where(lower, 1.0, 0.0).astype(BF16)
    tri_up = jnp.where(upper, 1.0, 0.0).astype(BF16)

    lbp = lb_ref[...]
    lb_max = jnp.maximum(lbp[0], lbp[1])
    lb_e0 = jnp.exp(lbp[0] - lb_max)
    lb_e1 = jnp.exp(lbp[1] - lb_max)
    lb = lb_e0 / (lb_e0 + lb_e1)

    def cumsum_chunk(tri, la):
        hi, lo = _split_bf16(la)
        return _dot(tri, hi) + _dot(tri, lo)

    def prep(n, carry):
        r0 = pl.multiple_of(n * C, C)
        rows = pl.ds(r0, C)
        gq = p_ref[rows, _C_GQ:_C_GQ + gqk] * (HEAD_DK ** -0.5)
        gk = p_ref[rows, _C_GK:_C_GK + gqk]
        hq = _silu(p_ref[rows, _C_HQ:_C_HQ + gqk]) * (HEAD_DK ** -0.5)
        for d_i, (q_s, k_s, dec_s, tri, last) in enumerate(
                ((qf, kf, dec_f, tri_lo, C - 1), (qb, kb, dec_b, tri_up, 0))):
            c_ga = _C_GAF if d_i == 0 else _C_GAB
            c_hf = _C_HFF if d_i == 0 else _C_HFB
            ga = p_ref[rows, c_ga:c_ga + GATE_RANK]
            xg = _dot(ga.astype(BF16), a2_ref[d_i].astype(BF16)) + ab_ref[d_i]
            la_g = _log_sigmoid(xg) / GLA_GATE_NORM
            f = lb[d_i:d_i + 1, :] + (1.0 - lb[d_i:d_i + 1, :]) * jax.nn.sigmoid(
                p_ref[rows, c_hf:c_hf + gqk])
            la_h = jnp.log(f)
            for q, k, la, c0 in ((gq, gk, la_g, 0), (hq, 1.0 - f, la_h, gqk)):
                b = cumsum_chunk(tri, la)
                q_s[rows, c0:c0 + gqk] = (q * jnp.exp(b)).astype(BF16)
                k_s[rows, c0:c0 + gqk] = (k * jnp.exp(-b)).astype(BF16)
                dec_s[n, :, c0:c0 + gqk] = jnp.exp(b[last:last + 1, :])
        vv[rows, 0:512] = p_ref[rows, _C_GV:_C_GV + 512].astype(BF16)
        vv[rows, 512:1024] = p_ref[rows, _C_HI:_C_HI + 512].astype(BF16)
        return carry

    lax.fori_loop(0, n_chunks, prep, 0)

    for p in range(SCAN_PAIRS):
        if has_state:
            st_f[p] = s0_ref[0, 0, p]
            st_b[p] = s0_ref[0, 1, p]
        else:
            st_f[p] = jnp.zeros((2 * HEAD_DV, 2 * HEAD_DK), F32)
            st_b[p] = jnp.zeros((2 * HEAD_DV, 2 * HEAD_DK), F32)

    first_head = lax.broadcasted_iota(I32, (C, 2 * HEAD_DK), 1) < HEAD_DK
    row2 = lax.broadcasted_iota(I32, (2 * C, C), 0) % C
    col2 = lax.broadcasted_iota(I32, (2 * C, C), 1)
    lower2 = col2 <= row2
    upper2 = col2 >= row2

    def per_head_rows(x):
        z = jnp.zeros_like(x)
        return jnp.concatenate([jnp.where(first_head, x, z), jnp.where(first_head, z, x)], axis=0)

    def put_out(o_ref, rows, p, res):
        c0 = p * 2 * HEAD_DV
        o_ref[rows, c0:c0 + HEAD_DV] = res[0:C, 0:HEAD_DV]
        o_ref[rows, c0 + HEAD_DV:c0 + 2 * HEAD_DV] = res[C:2 * C, HEAD_DV:2 * HEAD_DV]

    def sweep(n, carry):
        m = n_chunks - 1 - n
        rows = pl.ds(pl.multiple_of(n * C, C), C)
        rows_m = pl.ds(pl.multiple_of(m * C, C), C)
        decay_f, decay_b = dec_f[n], dec_b[m]
        for p in range(SCAN_PAIRS):
            ks = slice(p * 2 * HEAD_DK, (p + 1) * 2 * HEAD_DK)
            vs = slice(p * 2 * HEAD_DV, (p + 1) * 2 * HEAD_DV)
            qd, kd, vh = per_head_rows(qf[rows, ks]), kf[rows, ks], vv[rows, vs]
            s_f = st_f[p]
            sc = (jnp.where(lower2, _dot_nt(qd, kd), 0.0)
                  + jnp.where(upper2, _dot_nt(per_head_rows(qb[rows, ks]), kb[rows, ks]), 0.0))
            put_out(o_f, rows, p, _dot_nt(qd, s_f.astype(BF16)) + _dot(sc.astype(BF16), vh))
            st_f[p] = decay_f[:, ks] * (s_f + _dot_tn(vh, kd))
            s_b = st_b[p]
            vm, kbm = vv[rows_m, vs], kb[rows_m, ks]
            put_out(o_b, rows_m, p, _dot_nt(per_head_rows(qb[rows_m, ks]), s_b.astype(BF16)))
            st_b[p] = decay_b[:, ks] * (s_b + _dot_tn(vm, kbm))
        return carry

    lax.fori_loop(0, n_chunks, sweep, 0)

    def finish(n, carry):
        rows = pl.ds(pl.multiple_of(n * C, C), C)
        for h in range(SCAN_HEADS):
            vs = slice(h * HEAD_DV, (h + 1) * HEAD_DV)
            if h < GLA_HEADS:
                gate = p_ref[rows, _C_GG + h * HEAD_DV:_C_GG + (h + 1) * HEAD_DV]
                onw = ong_ref[...]
            else:
                hh = h - GLA_HEADS
                gate = p_ref[rows, _C_HG + hh * HEAD_DV:_C_HG + (hh + 1) * HEAD_DV]
                onw = onh_ref[...]
            o = o_f[rows, vs] + o_b[rows, vs]
            mixed_ref[rows, vs] = ((_rms(o) * onw) * _silu(gate)).astype(BF16)
        return carry

    lax.fori_loop(0, n_chunks, finish, 0)

    if sfin_ref is not None:
        for d_i, st in enumerate((st_f, st_b)):
            for p in range(SCAN_PAIRS):
                s_pair = st[p].T
                sfin_ref[0, d_i, 2 * p] = s_pair[0:HEAD_DK, 0:HEAD_DV]
                sfin_ref[0, d_i, 2 * p + 1] = s_pair[HEAD_DK:2 * HEAD_DK, HEAD_DV:2 * HEAD_DV]


def _scan(p, row0, batch, seq_len, a2, a_bias, lb, onorm_g, onorm_h, s0=None):
    n = p.shape[1]
    assert row0 % seq_len == 0
    blk0 = row0 // seq_len
    has_state = s0 is not None
    n_chunks = seq_len // SCAN_CHUNK
    st_shape = (1, 2, SCAN_HEADS, HEAD_DK, HEAD_DV)
    pair_shape = (SCAN_PAIRS, 2 * HEAD_DV, 2 * HEAD_DK)
    in_specs = [pl.BlockSpec((seq_len, n), lambda b: (blk0 + b, 0)),
                pl.BlockSpec(a2.shape, lambda b: (0, 0, 0)),
                pl.BlockSpec(a_bias.shape, lambda b: (0, 0, 0)),
                pl.BlockSpec(lb.shape, lambda b: (0, 0, 0)),
                pl.BlockSpec((1, HEAD_DV), lambda b: (0, 0)),
                pl.BlockSpec((1, HEAD_DV), lambda b: (0, 0))]
    args = [p, a2, a_bias, lb, onorm_g, onorm_h]
    mixed_shape = jax.ShapeDtypeStruct((batch * seq_len, D_MODEL), BF16)
    mixed_spec = pl.BlockSpec((seq_len, D_MODEL), lambda b: (b, 0))
    if has_state:
        in_specs.append(pl.BlockSpec((1, 2) + pair_shape, lambda b: (b, 0, 0, 0, 0)))
        args.append(s0)
        out_shape, out_specs = mixed_shape, mixed_spec
    else:
        out_shape = (mixed_shape, jax.ShapeDtypeStruct((batch,) + st_shape[1:], F32))
        out_specs = (mixed_spec, pl.BlockSpec(st_shape, lambda b: (b, 0, 0, 0, 0)))
    scratch = [pltpu.VMEM((seq_len, 512), BF16) for _ in range(4)]
    scratch += [pltpu.VMEM((seq_len, D_MODEL), BF16),
                pltpu.VMEM((n_chunks, 1, 512), F32), pltpu.VMEM((n_chunks, 1, 512), F32),
                pltpu.VMEM((seq_len, D_MODEL), F32), pltpu.VMEM((seq_len, D_MODEL), F32),
                pltpu.VMEM(pair_shape, F32), pltpu.VMEM(pair_shape, F32)]
    return pl.pallas_call(
        functools.partial(_scan_kernel, seq_len=seq_len, has_state=has_state),
        grid=(batch,),
        in_specs=in_specs, out_specs=out_specs, out_shape=out_shape,
        scratch_shapes=scratch,
        compiler_params=_params(1),
        name="scan_state" if has_state else "scan_fresh",
    )(*args)


def _post_kernel(*refs, split_x, n_prompt_tiles):
    if split_x:
        xp_ref, xs_ref = refs[0], refs[1]
        refs = refs[2:]
    else:
        x_ref = refs[0]
        refs = refs[1:]
    (mp_ref, ms_ref, mod_ref, nw_ref, wo_ref, wrh_ref, wrl_ref,
     x1_ref, h2_ref, slot_ref, wgt_ref, tab_ref, carry) = refs
    i = pl.program_id(0)
    is_prompt = i < n_prompt_tiles
    if split_x:
        x = jnp.where(is_prompt, xp_ref[...], xs_ref[...])
    else:
        x = x_ref[...]
    mixed = jnp.where(is_prompt, mp_ref[...], ms_ref[...])
    x1 = x + mod_ref[0, 2:3, :] * _dot(mixed, wo_ref[...])
    x1_ref[...] = x1
    h2 = _modulate(x1, nw_ref[...], mod_ref[0, 3:4, :], mod_ref[0, 4:5, :])
    _to_token_major(h2_ref, h2)

    hh, hl = _split_bf16(h2)
    logits = _dot(hh, wrh_ref[...]) + _dot(hl, wrh_ref[...]) + _dot(hh, wrl_ref[...])
    tm = logits.shape[0]
    lane = lax.broadcasted_iota(I32, (tm, LANES), 1).astype(F32)

    def first_max(v):
        mx = jnp.max(v, axis=1, keepdims=True)
        idx = jnp.min(jnp.where(v == mx, lane, float(LANES)), axis=1, keepdims=True)
        return mx, idx

    gl = jnp.where(lane < N_GROUPS, logits, NEG_BIG)
    gmax, gidx = first_max(gl)
    g_val = 1.0 / jnp.sum(jnp.exp(gl - gmax), axis=1, keepdims=True)
    lo = N_GROUPS + EXPERTS_PER_GROUP * gidx
    el = jnp.where((lane >= lo) & (lane < lo + EXPERTS_PER_GROUP), logits, NEG_BIG)
    emax, l1 = first_max(el)
    esum = jnp.sum(jnp.exp(el - emax), axis=1, keepdims=True)
    e2max, l2 = first_max(jnp.where(lane == l1, NEG_BIG, el))
    p1 = 1.0 / esum
    p2 = jnp.exp(e2max - emax) / esum
    w1 = g_val * (p1 / (p1 + p2))
    w2 = g_val * (p2 / (p1 + p2))
    id1 = l1 - N_GROUPS
    id2 = l2 - N_GROUPS

    @pl.when(i == 0)
    def _():
        carry[...] = jnp.zeros_like(carry)

    sel1 = lane == id1
    sel2 = lane == id2
    onehot = jnp.where(sel1 | sel2, 1.0, 0.0)
    row = lax.broadcasted_iota(I32, (tm, tm), 0)
    col = lax.broadcasted_iota(I32, (tm, tm), 1)
    earlier = jnp.where(col < row, 1.0, 0.0).astype(BF16)
    before = _dot(earlier, onehot.astype(BF16))
    count = jnp.sum(onehot, axis=0, keepdims=True)
    blocks = jnp.floor((count + (MOVE_BLOCK - 1.0)) * (1.0 / MOVE_BLOCK)) * MOVE_BLOCK
    e_row = lax.broadcasted_iota(I32, (LANES, LANES), 0)
    e_col = lax.broadcasted_iota(I32, (LANES, LANES), 1)
    lower_experts = jnp.where(e_row < e_col, 1.0, 0.0).astype(BF16)
    run_start = _dot(jnp.broadcast_to(blocks, (8, LANES)).astype(BF16), lower_experts)[0:1]
    slot = before + run_start
    q1 = jnp.sum(jnp.where(sel1, slot, 0.0), axis=1, keepdims=True)
    q2 = jnp.sum(jnp.where(sel2, slot, 0.0), axis=1, keepdims=True)
    tab_row = lax.broadcasted_iota(I32, (8, LANES), 0)
    tab_ref[0] = jnp.where(tab_row == 0, count,
                           jnp.where(tab_row == 1, carry[...],
                                     jnp.where(tab_row == 2, run_start, 0.0)))
    carry[...] = carry[...] + count

    cols = jnp.zeros((tm, LANES), F32)
    for k, v in enumerate((q1 * TOKEN_ROWS, q2 * TOKEN_ROWS, w1, w2)):
        cols = jnp.where(lane == k, v, cols)
    rows = cols.T
    slot_ref[0] = rows[0:2].astype(I32)
    wgt_ref[0] = rows[2:4]


def _post(x_args, mixed_p, mixed_s, mods, layer, norm_w, w_out_bf16, wr_hi, wr_lo,
          tiles_per_sample):
    split_x = len(x_args) == 2
    tp, ts = mixed_p.shape[0], mixed_s.shape[0]
    t, d = tp + ts, D_MODEL
    npt, nst = tp // ROW_TILE, ts // ROW_TILE
    tile = lambda i: (i, 0)
    if split_x:
        x_specs = [pl.BlockSpec((ROW_TILE, d), lambda i: (jnp.minimum(i, npt - 1), 0)),
                   pl.BlockSpec((ROW_TILE, d), lambda i: (jnp.maximum(i - npt, 0), 0))]
    else:
        x_specs = [pl.BlockSpec((ROW_TILE, d), tile)]
    in_specs = x_specs + [
        pl.BlockSpec((ROW_TILE, d), lambda i: (jnp.minimum(i, npt - 1), 0)),
        pl.BlockSpec((ROW_TILE, d), lambda i: (jnp.maximum(i - npt, 0), 0)),
        pl.BlockSpec((1, 6, d), lambda i: (_mod_row(i, layer, npt, tiles_per_sample), 0, 0)),
        pl.BlockSpec((1, d), lambda i: (0, 0)),
        pl.BlockSpec((d, d), lambda i: (0, 0)),
        pl.BlockSpec((d, LANES), lambda i: (0, 0)),
        pl.BlockSpec((d, LANES), lambda i: (0, 0))]
    return pl.pallas_call(
        functools.partial(_post_kernel, split_x=split_x, n_prompt_tiles=npt),
        grid=(npt + nst,),
        in_specs=in_specs,
        out_specs=(pl.BlockSpec((ROW_TILE, d), tile),
                   pl.BlockSpec((ROW_TILE * TOKEN_ROWS, LANES), tile),
                   pl.BlockSpec((1, 2, ROW_TILE), lambda i: (i, 0, 0)),
                   pl.BlockSpec((1, 2, ROW_TILE), lambda i: (i, 0, 0)),
                   pl.BlockSpec((1, 8, LANES), lambda i: (i, 0, 0))),
        out_shape=(jax.ShapeDtypeStruct((t, d), F32),
                   jax.ShapeDtypeStruct((t * TOKEN_ROWS, LANES), F32),
                   jax.ShapeDtypeStruct((npt + nst, 2, ROW_TILE), I32),
                   jax.ShapeDtypeStruct((npt + nst, 2, ROW_TILE), F32),
                   jax.ShapeDtypeStruct((npt + nst, 8, LANES), F32)),
        scratch_shapes=[pltpu.VMEM((1, LANES), F32)],
        compiler_params=_params(1),
        name=f"post{layer}",
    )(*x_args, mixed_p, mixed_s, mods, norm_w, w_out_bf16, wr_hi, wr_lo)


def _for_blocks(tab_ref, fn):
    block_rows = MOVE_BLOCK * TOKEN_ROWS
    count = tab_ref[0, 0, MAX_BLOCKS]

    def call(k, parity):
        fn(pl.multiple_of(k * block_rows, block_rows),
           pl.multiple_of(tab_ref[0, 0, k], TOKEN_ROWS), parity)

    def body(k2, c):
        call(2 * k2, 0)

        @pl.when(2 * k2 + 1 < count)
        def _():
            call(2 * k2 + 1, 1)
        return c

    lax.fori_loop(0, _cdiv(count, 2), body, 0)


def _wait_blocks(tab_ref, copy):
    def body(k, c):
        copy.wait()
        return c

    lax.fori_loop(0, tab_ref[0, 0, MAX_BLOCKS], body, 0)


def _dispatch_kernel(zero_ref, tab_ref, prev_tab_ref, q_ref, h2_ref, hs_ref, zero_buf, stage, sem):
    j = pl.program_id(0)
    slot = j % 2
    block_rows = MOVE_BLOCK * TOKEN_ROWS

    @pl.when(j == 0)
    def _():
        zero_buf[...] = jnp.zeros_like(zero_buf)

        def zero_copy(k):
            start = pl.multiple_of(zero_ref[k], EXPERT_TILE * TOKEN_ROWS)
            return pltpu.make_async_copy(
                zero_buf, hs_ref.at[pl.ds(start, EXPERT_TILE * TOKEN_ROWS)], sem.at[0])

        def start_zero(k, c):
            @pl.when(zero_ref[k] >= 0)
            def _():
                zero_copy(k).start()
            return c

        def wait_zero(k, c):
            @pl.when(zero_ref[k] >= 0)
            def _():
                zero_copy(k).wait()
            return c

        lax.fori_loop(0, zero_ref.shape[0], start_zero, 0)
        lax.fori_loop(0, zero_ref.shape[0], wait_zero, 0)

        stage[...] = jnp.zeros_like(stage)

    def place(r, c):
        tok = h2_ref[pl.ds(pl.multiple_of(r * TOKEN_ROWS, TOKEN_ROWS), TOKEN_ROWS), :]
        for s in range(2):
            row = pl.multiple_of(q_ref[0, s, r], TOKEN_ROWS)
            stage[slot, pl.ds(row, TOKEN_ROWS), :] = tok
        return c

    lax.fori_loop(0, ROW_TILE, place, 0, unroll=8)

    def block_copy(buf, stage_row, sorted_row):
        return pltpu.make_async_copy(stage.at[buf, pl.ds(stage_row, block_rows)],
                                     hs_ref.at[pl.ds(sorted_row, block_rows)], sem.at[buf])

    @pl.when(j > 0)
    def _():
        _wait_blocks(prev_tab_ref, block_copy(1 - slot, 0, 0))

    _for_blocks(tab_ref, lambda a, b, parity: block_copy(slot, a, b).start(priority=parity))

    @pl.when(j == pl.num_programs(0) - 1)
    def _():
        _wait_blocks(tab_ref, block_copy(slot, 0, 0))


def _dispatch(zero_tiles, block_tab, slots, h2, n_rows):
    t = h2.shape[0] // TOKEN_ROWS
    nt = t // ROW_TILE
    smem_tile = lambda shape: pl.BlockSpec((1,) + shape, lambda j, *_: (j, 0, 0),
                                           memory_space=pltpu.SMEM)
    grid_spec = pltpu.PrefetchScalarGridSpec(
        num_scalar_prefetch=1,
        grid=(nt,),
        in_specs=[smem_tile((1, LANES)),
                  pl.BlockSpec((1, 1, LANES), lambda j, *_: (jnp.maximum(j - 1, 0), 0, 0),
                               memory_space=pltpu.SMEM),
                  smem_tile((2, ROW_TILE)),
                  pl.BlockSpec((ROW_TILE * TOKEN_ROWS, LANES), lambda j, *_: (j, 0))],
        out_specs=pl.BlockSpec(memory_space=pl.ANY),
        scratch_shapes=[pltpu.VMEM((EXPERT_TILE * TOKEN_ROWS, LANES), F32),
                        pltpu.VMEM((2, STAGE_TOKENS * TOKEN_ROWS, LANES), F32),
                        pltpu.SemaphoreType.DMA((2,))])
    return pl.pallas_call(
        _dispatch_kernel,
        grid_spec=grid_spec,
        out_shape=jax.ShapeDtypeStruct((n_rows * TOKEN_ROWS, LANES), F32),
        compiler_params=_params(1),
        name="dispatch",
    )(zero_tiles, block_tab, block_tab, slots, h2)


def _expert_kernel(te_ref, src_ref, nv_ref, hs_ref, w1_ref, w3_ref, w2_ref, ys_ref,
                   w1b, w3b, w2b):
    i = pl.program_id(0)
    prev = te_ref[jnp.maximum(i - 1, 0)]

    @pl.when((i == 0) | (te_ref[i] != prev))
    def _():
        w1b[...] = w1_ref[0, 0].astype(BF16)
        w3b[...] = w3_ref[0, 0].astype(BF16)
        w2b[...] = w2_ref[0, 0].astype(BF16)

    @pl.when(nv_ref[i] > 0)
    def _():
        h = _from_token_major(hs_ref, EXPERT_TILE).astype(BF16)
        g = _silu(_dot(h, w1b[...])) * _dot(h, w3b[...])
        _to_token_major(ys_ref, _dot(g.astype(BF16), w2b[...]))

    @pl.when(nv_ref[i] == 0)
    def _():
        ys_ref[...] = jnp.zeros_like(ys_ref)


def _experts(tile_expert, tile_src, tile_rows, hs, w1, w3, w2, layer):
    n_rows, d = hs.shape[0] // TOKEN_ROWS, D_MODEL
    nt = n_rows // EXPERT_TILE
    hid = w1.shape[-1]
    tok_tile = (EXPERT_TILE * TOKEN_ROWS, LANES)
    row_map = lambda i, te, src, nv: (src[i], 0)
    grid_spec = pltpu.PrefetchScalarGridSpec(
        num_scalar_prefetch=3,
        grid=(nt,),
        in_specs=[pl.BlockSpec(tok_tile, row_map),
                  pl.BlockSpec((1, 1, d, hid), lambda i, te, src, nv: (layer, te[i], 0, 0)),
                  pl.BlockSpec((1, 1, d, hid), lambda i, te, src, nv: (layer, te[i], 0, 0)),
                  pl.BlockSpec((1, 1, hid, d), lambda i, te, src, nv: (layer, te[i], 0, 0))],
        out_specs=pl.BlockSpec(tok_tile, lambda i, te, src, nv: (i, 0)),
        scratch_shapes=[pltpu.VMEM((d, hid), BF16), pltpu.VMEM((d, hid), BF16),
                        pltpu.VMEM((hid, d), BF16)])
    return pl.pallas_call(
        _expert_kernel,
        grid_spec=grid_spec,
        out_shape=jax.ShapeDtypeStruct(hs.shape, F32),
        compiler_params=_params(1),
        name=f"experts{layer}",
    )(tile_expert, tile_src, tile_rows, hs, w1, w3, w2)


def _combine_kernel(tab_ref, next_tab_ref, q_ref, w_ref, x1_ref, mod_ref, fw_ref, ys_ref, out_ref,
                    stage, y_tok, sem, *, final_norm):
    i = pl.program_id(0)
    slot = i % 2
    block_rows = MOVE_BLOCK * TOKEN_ROWS

    def block_copy(buf, stage_row, sorted_row):
        return pltpu.make_async_copy(ys_ref.at[pl.ds(sorted_row, block_rows)],
                                     stage.at[buf, pl.ds(stage_row, block_rows)], sem.at[buf])

    def fetch(tab, buf):
        _for_blocks(tab, lambda a, b, parity: block_copy(buf, a, b).start(priority=parity))

    @pl.when(i == 0)
    def _():
        fetch(tab_ref, slot)

    @pl.when(i + 1 < pl.num_programs(0))
    def _():
        fetch(next_tab_ref, 1 - slot)

    _wait_blocks(tab_ref, block_copy(slot, 0, 0))

    def pick(r, c):
        rows = [stage[slot, pl.ds(pl.multiple_of(q_ref[0, s, r], TOKEN_ROWS), TOKEN_ROWS), :]
                for s in range(2)]
        y_tok[pl.ds(pl.multiple_of(r * TOKEN_ROWS, TOKEN_ROWS), TOKEN_ROWS), :] = (
            w_ref[0, 0, r] * rows[0] + w_ref[0, 1, r] * rows[1])
        return c

    lax.fori_loop(0, ROW_TILE, pick, 0, unroll=8)
    x2 = x1_ref[...] + mod_ref[0, 5:6, :] * _from_token_major(y_tok, ROW_TILE)
    if final_norm:
        x2 = _rms(x2) * fw_ref[...]
    out_ref[...] = x2


def _combine(block_tab, slots, weights, x1, mods, layer, final_w, ys, tile0, n_tiles,
             n_prompt_tiles, tiles_per_sample, final_norm):
    d = D_MODEL
    tile = lambda i: (tile0 + i, 0)
    mod_map = lambda i: (_mod_row(tile0 + i, layer, n_prompt_tiles, tiles_per_sample), 0, 0)
    smem_tile = lambda shape: pl.BlockSpec((1,) + shape, lambda i: (tile0 + i, 0, 0),
                                           memory_space=pltpu.SMEM)
    return pl.pallas_call(
        functools.partial(_combine_kernel, final_norm=final_norm),
        grid=(n_tiles,),
        in_specs=[smem_tile((1, LANES)),
                  pl.BlockSpec((1, 1, LANES),
                               lambda i: (tile0 + jnp.minimum(i + 1, n_tiles - 1), 0, 0),
                               memory_space=pltpu.SMEM),
                  smem_tile((2, ROW_TILE)), smem_tile((2, ROW_TILE)),
                  pl.BlockSpec((ROW_TILE, d), tile),
                  pl.BlockSpec((1, 6, d), mod_map),
                  pl.BlockSpec((1, d), lambda i: (0, 0)),
                  pl.BlockSpec(memory_space=pl.ANY)],
        out_specs=pl.BlockSpec((ROW_TILE, d), lambda i: (i, 0)),
        out_shape=jax.ShapeDtypeStruct((n_tiles * ROW_TILE, d), F32),
        scratch_shapes=[pltpu.VMEM((2, STAGE_TOKENS * TOKEN_ROWS, LANES), F32),
                        pltpu.VMEM((ROW_TILE * TOKEN_ROWS, LANES), F32),
                        pltpu.SemaphoreType.DMA((2,))],
        compiler_params=_params(1),
        name=f"combine{layer}_{tile0}",
    )(block_tab, block_tab, slots, weights, x1, mods, final_w, ys)


def _moe(h2, slots, weights, tile_tab, w1, w3, w2, layer):
    t = h2.shape[0] // TOKEN_ROWS
    n_tiles = t // ROW_TILE
    extra_tiles = N_EXPERTS + _cdiv(N_EXPERTS * MOVE_BLOCK, EXPERT_TILE)
    n_rows = 2 * t + extra_tiles * EXPERT_TILE
    nt = n_rows // EXPERT_TILE
    tab = tile_tab[:, :, :N_EXPERTS].astype(I32)
    cnt = tab[-1, 0] + tab[-1, 1]
    tight = _cdiv(cnt, EXPERT_TILE) * EXPERT_TILE
    padded = jnp.where(cnt > 0, _cdiv(cnt + MOVE_BLOCK - 1, EXPERT_TILE) * EXPERT_TILE, 0)
    ends = jnp.cumsum(padded)
    offsets = ends - padded
    tails = jnp.where(cnt > 0, ends - EXPERT_TILE, -1)
    tails2 = jnp.where(padded > tight, ends - 2 * EXPERT_TILE, -1)
    used = ends[-1] // EXPERT_TILE
    tile_start = jnp.arange(nt, dtype=I32) * EXPERT_TILE
    unused = (used + jnp.arange(extra_tiles, dtype=I32)) * EXPERT_TILE
    zero_tiles = jnp.concatenate([tails, tails2, jnp.where(unused < n_rows, unused, -1)])
    zero_tiles = jnp.where(zero_tiles >= 0, zero_tiles * TOKEN_ROWS, -1).astype(I32)
    tile_src = jnp.minimum(jnp.arange(nt, dtype=I32), used - 1)
    tile_expert = jnp.sum((tile_src * EXPERT_TILE)[:, None] >= ends[None, :], axis=1).astype(I32)
    tile_rows = jnp.where(tile_start < ends[-1],
                          jnp.clip(cnt[tile_expert] - (tile_start - offsets[tile_expert]),
                                   0, EXPERT_TILE), 0).astype(I32)
    n_blocks = _cdiv(tab[:, 0], MOVE_BLOCK)
    blocks_through = jnp.cumsum(n_blocks, axis=1)
    k = jnp.arange(MAX_BLOCKS, dtype=I32)
    owner = jnp.sum(blocks_through[:, None, :] <= k[None, :, None], axis=2)
    is_owner = owner[:, :, None] == jnp.arange(N_EXPERTS, dtype=I32)[None, None, :]
    pick = lambda v: jnp.sum(jnp.where(is_owner, v[:, None, :], 0), axis=2)
    run_first = pick(offsets[None, :] + tab[:, 1])
    block_in_run = k[None, :] - pick(blocks_through - n_blocks)
    sorted_row = (run_first + block_in_run * MOVE_BLOCK) * TOKEN_ROWS
    block_tab = jnp.concatenate(
        [sorted_row, blocks_through[:, -1:],
         jnp.zeros((n_tiles, LANES - MAX_BLOCKS - 1), I32)], axis=1).astype(I32)[:, None, :]
    hs = _dispatch(zero_tiles, block_tab, slots, h2, n_rows)
    ys = _experts(tile_expert, tile_src, tile_rows, hs, w1, w3, w2, layer)
    return ys, (block_tab, slots, weights)


def _rope(x, cos, sin_signed):
    lane = lax.broadcasted_iota(I32, (x.shape[0], LANES), 1)
    low = (lane % 32) < 16
    outs = []
    for j in range(x.shape[1] // LANES):
        xb = x[:, j * LANES:(j + 1) * LANES]
        partner = jnp.where(low, pltpu.roll(xb, LANES - 16, 1), pltpu.roll(xb, 16, 1))
        outs.append(xb * cos + partner * sin_signed)
    return jnp.concatenate(outs, axis=1)


def _inproj1_prompt_kernel(x_ref, mod_ref, nw_ref, w_ref, q_ref, k_ref, v_ref):
    d = D_MODEL
    h = _modulate(x_ref[...], nw_ref[...], mod_ref[0, 0:1, :], mod_ref[0, 1:2, :]).astype(BF16)
    q_ref[...] = (_dot(h, w_ref[:, 0:d]) * (DIFF_HD ** -0.5)).astype(BF16)
    k_ref[...] = _dot(h, w_ref[:, d:2 * d])
    v_ref[...] = _dot(h, w_ref[:, 2 * d:3 * d])


def _inproj1_sample_kernel(x_ref, mod_ref, nw_ref, w_ref, cos_ref, sin_ref, q_ref, k_ref, v_ref):
    d = D_MODEL
    h = _modulate(x_ref[...], nw_ref[...], mod_ref[0, 0:1, :], mod_ref[0, 1:2, :]).astype(BF16)
    cos, sin = cos_ref[...], sin_ref[...]
    q_ref[...] = (_rope(_dot(h, w_ref[:, 0:d]), cos, sin) * (DIFF_HD ** -0.5)).astype(BF16)
    k_ref[...] = _rope(_dot(h, w_ref[:, d:2 * d]), cos, sin).astype(BF16)
    v_ref[...] = _dot(h, w_ref[:, 2 * d:3 * d]).astype(BF16)


def _inproj1(x, mods, norm_w, w_bf16, n_prompt_tiles, n_sample_tiles, tiles_per_sample,
             cos_t, sin_t):
    d = D_MODEL
    npt, nst = n_prompt_tiles, n_sample_tiles
    common = [pl.BlockSpec((1, d), lambda i: (0, 0)), pl.BlockSpec((d, 3 * d), lambda i: (0, 0))]
    tile = lambda i: (i, 0)
    out_specs = tuple(pl.BlockSpec((ROW_TILE, d), tile) for _ in range(3))
    qp, kp, vp = pl.pallas_call(
        _inproj1_prompt_kernel,
        grid=(npt,),
        in_specs=[pl.BlockSpec((ROW_TILE, d), tile),
                  pl.BlockSpec((1, 6, d), lambda i: (8, 0, 0))] + common,
        out_specs=out_specs,
        out_shape=(jax.ShapeDtypeStruct((npt * ROW_TILE, d), BF16),
                   jax.ShapeDtypeStruct((npt * ROW_TILE, d), F32),
                   jax.ShapeDtypeStruct((npt * ROW_TILE, d), F32)),
        compiler_params=_params(1),
        name="inproj1_prompt",
    )(x, mods, norm_w, w_bf16)
    rope_tile = lambda i: (i % tiles_per_sample, 0)
    qs, ks, vs = pl.pallas_call(
        _inproj1_sample_kernel,
        grid=(nst,),
        in_specs=[pl.BlockSpec((ROW_TILE, d), lambda i: (npt + i, 0)),
                  pl.BlockSpec((1, 6, d), lambda i: (8 + 1 + i // tiles_per_sample, 0, 0))]
        + common + [pl.BlockSpec((ROW_TILE, LANES), rope_tile),
                    pl.BlockSpec((ROW_TILE, LANES), rope_tile)],
        out_specs=out_specs,
        out_shape=tuple(jax.ShapeDtypeStruct((nst * ROW_TILE, d), BF16) for _ in range(3)),
        compiler_params=_params(1),
        name="inproj1_sample",
    )(x, mods, norm_w, w_bf16, cos_t, sin_t)
    return (qp, kp, vp), (qs, ks, vs)


def _rope_tables(n_tok):
    half = DIFF_HD // 4
    pos = np.arange(n_tok)
    lane = np.arange(LANES)
    sub = lane % DIFF_HD
    p = np.where(sub[None, :] < DIFF_HD // 2, (pos // GRID_W)[:, None], (pos % GRID_W)[:, None])
    inv = jnp.asarray(ROPE_THETA, F32) ** (-jnp.asarray(sub % half, F32) / half)
    ang = jnp.asarray(p, F32) * inv[None, :]
    sign = np.where((lane % (2 * half)) < half, -1.0, 1.0).astype(np.float32)
    return jnp.cos(ang), jnp.sin(ang) * sign[None, :]


def _diffattn_kernel(*refs, has_cache, lam_init):
    if has_cache:
        q_ref, k_ref, v_ref, ck_ref, cv_ref, lam_ref, sw_ref, o_ref = refs
    else:
        q_ref, k_ref, v_ref, lam_ref, sw_ref, o_ref = refs
    hd2 = 2 * DIFF_HD
    lv = lam_ref[...]
    lam = (jnp.exp(jnp.sum(lv[0:1] * lv[1:2], axis=1, keepdims=True))
           - jnp.exp(jnp.sum(lv[2:3] * lv[3:4], axis=1, keepdims=True)) + lam_init)
    lane = lax.broadcasted_iota(I32, (q_ref.shape[0], hd2), 1)
    for h in range(DIFF_HEADS):
        cols = slice(h * hd2, (h + 1) * hd2)
        q = q_ref[:, cols]
        zero = jnp.zeros_like(q)
        keys = [(k_ref[:, cols].astype(BF16), v_ref[:, cols].astype(BF16))]
        if has_cache:
            keys.append((ck_ref[:, cols].astype(BF16), cv_ref[:, cols].astype(BF16)))
        o = None
        for c in range(2):
            qc = jnp.where((lane < DIFF_HD) == (c == 0), q, zero)
            s = [_dot_nt(qc, k) for k, _ in keys]
            mx = functools.reduce(jnp.maximum, [jnp.max(si, axis=1, keepdims=True) for si in s])
            e = [jnp.exp(si - mx) for si in s]
            z = functools.reduce(jnp.add, [jnp.sum(ei, axis=1, keepdims=True) for ei in e])
            pv = functools.reduce(jnp.add,
                                  [_dot(ei.astype(BF16), v) for ei, (_, v) in zip(e, keys)])
            pv = pv * (1.0 / z)
            o = pv if c == 0 else o - lam * pv
        o_ref[:, cols] = ((_rms(o) * sw_ref[...]) * (1.0 - lam_init)).astype(BF16)


def _diffattn(q, k, v, lam_vecs, subln_w, batch, seq_len, q_block, lam_init, cache=None):
    d = D_MODEL
    nq = seq_len // q_block
    has_cache = cache is not None
    kv_spec = pl.BlockSpec((seq_len, d), lambda b, qi: (b, 0))
    in_specs = [pl.BlockSpec((q_block, d), lambda b, qi: (b * nq + qi, 0)), kv_spec, kv_spec]
    args = [q, k, v]
    if has_cache:
        past = cache[0].shape[0] // batch
        c_spec = pl.BlockSpec((past, d), lambda b, qi: (b, 0))
        in_specs += [c_spec, c_spec]
        args += list(cache)
    in_specs += [pl.BlockSpec((4, DIFF_HD), lambda b, qi: (0, 0)),
                 pl.BlockSpec((1, 2 * DIFF_HD), lambda b, qi: (0, 0))]
    args += [lam_vecs, subln_w]
    return pl.pallas_call(
        functools.partial(_diffattn_kernel, has_cache=has_cache, lam_init=lam_init),
        grid=(batch, nq),
        in_specs=in_specs,
        out_specs=pl.BlockSpec((q_block, d), lambda b, qi: (b * nq + qi, 0)),
        out_shape=jax.ShapeDtypeStruct((batch * seq_len, d), BF16),
        compiler_params=_params(2),
        name="diffattn_cache" if has_cache else "diffattn",
    )(*args)


def _router_weights(router_group, router_expert):
    w = jnp.concatenate([router_group, router_expert], axis=1)
    w = jnp.pad(w, ((0, 0), (0, LANES - w.shape[1])))
    hi = w.astype(BF16)
    return hi, (w - hi.astype(F32)).astype(BF16)


def _inproj0_weights(w_in):
    gq, gk, gv, gg, gaf, gab, hq, hff, hfb, hi, hg = jnp.split(
        w_in, [256, 512, 1024, 1536, 1552, 1568, 1824, 2080, 2336, 2848], axis=1)
    w = jnp.concatenate([gq, gk, gv, gg, hq, hff, hfb, hi, hg, gaf, gab], axis=1)
    return jnp.pad(w, ((0, 0), (0, AB_COLS - w.shape[1]))).astype(BF16)


def kernel(x_prompt, x_sample, state_gla, state_hgrn, cache_diff_k, cache_diff_v, c, c_ctx,
           w_ada, b_ada, norm1_w, norm2_w, w_in_ab, gla_a2, gla_a_bias, hgrn_lb, gla_onorm_w,
           hgrn_onorm_w, w_out_ab, w_in_c, lam_q1, lam_k1, lam_q2, lam_k2, diff_subln_w, w_out_c,
           router_group, router_expert, moe_w1, moe_w3, moe_w2, final_norm_w):
    bp, lp, d = x_prompt.shape
    bs, ls, _ = x_sample.shape
    depth = w_ada.shape[0]
    assert depth == 2 and d == D_MODEL and bs <= 7
    tp, ts = bp * lp, bs * ls
    npt, nst = tp // ROW_TILE, ts // ROW_TILE
    tps = ls // ROW_TILE
    xp = x_prompt.reshape(tp, d)
    xs = x_sample.reshape(ts, d)

    cond8 = jnp.concatenate([c_ctx[None, :], c, jnp.zeros((7 - bs, d), F32)], axis=0)
    mods = _adaln(cond8, w_ada, b_ada).reshape(depth * 8, 6, d)

    proj = _inproj0(xp, xs, mods, norm1_w[0:1], _inproj0_weights(w_in_ab[0]), tps)
    a_bias = gla_a_bias[0][:, None, :]
    scan_args = (gla_a2[0], a_bias, hgrn_lb, gla_onorm_w[0:1], hgrn_onorm_w[0:1])
    mixed_p, s_fin = _scan(proj, 0, bp, lp, *scan_args)
    s0 = jnp.concatenate([state_gla[:, 0], state_hgrn[:, 0]], axis=2).swapaxes(-1, -2)
    s0 = s0.reshape(bs, 2, SCAN_PAIRS, 2, HEAD_DV, HEAD_DK)
    zero = jnp.zeros_like(s0[:, :, :, 0])
    s0 = jnp.concatenate([jnp.concatenate([s0[:, :, :, 0], zero], axis=-1),
                          jnp.concatenate([zero, s0[:, :, :, 1]], axis=-1)], axis=-2)
    mixed_s = _scan(proj, tp, bs, ls, *scan_args, s0=s0)
    new_state_gla = s_fin[:, None, :, :GLA_HEADS]
    new_state_hgrn = s_fin[:, None, :, GLA_HEADS:]

    wr = _router_weights(router_group[0], router_expert[0])
    x1, *routed = _post((xp, xs), mixed_p, mixed_s, mods, 0, norm2_w[0:1],
                        w_out_ab[0].astype(BF16), *wr, tps)
    ys, tables = _moe(*routed, moe_w1, moe_w3, moe_w2, 0)
    x2 = _combine(*tables, x1, mods, 0, final_norm_w[None, :], ys, 0, npt + nst, npt, tps, False)

    lam_init = 0.8 - 0.6 * math.exp(-0.3 * 1)
    cos_t, sin_t = _rope_tables(ls)
    (qp, kp, vp), (qs, ks, vs) = _inproj1(x2, mods, norm1_w[1:2], w_in_c[0].astype(BF16),
                                          npt, nst, tps, cos_t, sin_t)
    lam_vecs = jnp.stack([lam_q1[0], lam_k1[0], lam_q2[0], lam_k2[0]])
    att_p = _diffattn(qp, kp, vp, lam_vecs, diff_subln_w[0:1], bp, lp, lp, lam_init)
    past = cache_diff_k.shape[2]
    cache = (cache_diff_k[:, 0].reshape(bs * past, d), cache_diff_v[:, 0].reshape(bs * past, d))
    att_s = _diffattn(qs, ks, vs, lam_vecs, diff_subln_w[0:1], bs, ls, ROW_TILE, lam_init, cache)

    wr = _router_weights(router_group[1], router_expert[1])
    x3, *routed = _post((x2,), att_p, att_s, mods, 1, norm2_w[1:2],
                        w_out_c[0].astype(BF16), *wr, tps)
    ys, tables = _moe(*routed, moe_w1, moe_w3, moe_w2, 1)
    fw = final_norm_w[None, :]
    y_p = _combine(*tables, x3, mods, 1, fw, ys, 0, npt, npt, tps, True)
    y_s = _combine(*tables, x3, mods, 1, fw, ys, npt, nst, npt, tps, True)

    return (y_p.reshape(bp, lp, d), y_s.reshape(bs, ls, d), new_state_gla, new_state_hgrn,
            kp.reshape(bp, 1, lp, DIFF_HEADS, 2, DIFF_HD),
            vp.reshape(bp, 1, lp, DIFF_HEADS, 2 * DIFF_HD))
```

```python
import functools
import math

import jax
import jax.numpy as jnp
import numpy as np
from jax import lax
from jax.experimental import pallas as pl
from jax.experimental.pallas import tpu as pltpu

F32 = jnp.float32
BF16 = jnp.bfloat16
I32 = jnp.int32

D_MODEL = 1024
GLA_HEADS = 4
HGRN_HEADS = 4
SCAN_HEADS = GLA_HEADS + HGRN_HEADS
SCAN_PAIRS = SCAN_HEADS // 2
HEAD_DK = 64
HEAD_DV = 128
GATE_RANK = 16
GLA_GATE_NORM = 16.0
DIFF_HEADS = 8
DIFF_HD = 64
GRID_W = 64
ROPE_THETA = 10000.0
N_GROUPS = 4
EXPERTS_PER_GROUP = 8
N_EXPERTS = N_GROUPS * EXPERTS_PER_GROUP
MOE_HIDDEN = 512
EPS = 1e-6
LANES = 128
TOKEN_ROWS = D_MODEL // LANES
NEG_BIG = -1e30

ROW_TILE = 256
SCAN_CHUNK = 64
EXPERT_TILE = 256
MOVE_BLOCK = 16
STAGE_TOKENS = 2 * ROW_TILE + N_EXPERTS * MOVE_BLOCK
MAX_BLOCKS = STAGE_TOKENS // MOVE_BLOCK
VMEM_LIMIT = 56 * 1024 * 1024

_C_GQ, _C_GK, _C_GV, _C_GG = 0, 256, 512, 1024
_C_HQ, _C_HFF, _C_HFB, _C_HI, _C_HG = 1536, 1792, 2048, 2304, 2816
_C_GAF, _C_GAB = 3328, 3344
AB_COLS = 3456


def _params(n_axes, vmem=VMEM_LIMIT):
    return pltpu.CompilerParams(dimension_semantics=("arbitrary",) * n_axes,
                                vmem_limit_bytes=vmem)


def _cdiv(a, b):
    return (a + b - 1) // b


def _dot(a, b):
    return jnp.dot(a, b, preferred_element_type=F32)


def _dot_nt(a, b):
    return lax.dot_general(a, b, (((1,), (1,)), ((), ())), preferred_element_type=F32)


def _dot_tn(a, b):
    return lax.dot_general(a, b, (((0,), (0,)), ((), ())), preferred_element_type=F32)


def _split_bf16(x):
    hi = x.astype(BF16)
    lo = (x - hi.astype(F32)).astype(BF16)
    return hi, lo


def _silu(x):
    return x * jax.nn.sigmoid(x)


def _log_sigmoid(x):
    return jnp.minimum(x, 0.0) - jnp.log(1.0 + jnp.exp(-jnp.abs(x)))


def _rms(x):
    return x * lax.rsqrt(jnp.mean(x * x, axis=-1, keepdims=True) + EPS)


def _modulate(x, norm_w, shift, scale):
    return (_rms(x) * norm_w) * (1.0 + scale) + shift


def _to_token_major(dst_ref, x, row0=0):
    n = x.shape[0]
    for s in range(TOKEN_ROWS):
        dst_ref[pl.ds(row0 + s, n, stride=TOKEN_ROWS), :] = x[:, s * LANES:(s + 1) * LANES]


def _from_token_major(src_ref, n, row0=0):
    return jnp.concatenate([src_ref[pl.ds(row0 + s, n, stride=TOKEN_ROWS), :]
                            for s in range(TOKEN_ROWS)], axis=1)


def _ada_kernel(c_ref, w_ref, b_ref, o_ref):
    s = _silu(c_ref[...])
    o_ref[0] = _dot(s.astype(BF16), w_ref[0].astype(BF16)) + b_ref[0]


def _adaln(cond8, w_ada, b_ada):
    depth, d, n = w_ada.shape
    tn = 1536
    return pl.pallas_call(
        _ada_kernel,
        grid=(depth, n // tn),
        in_specs=[pl.BlockSpec((8, d), lambda l, j: (0, 0)),
                  pl.BlockSpec((1, d, tn), lambda l, j: (l, 0, j)),
                  pl.BlockSpec((1, 1, tn), lambda l, j: (l, 0, j))],
        out_specs=pl.BlockSpec((1, 8, tn), lambda l, j: (l, 0, j)),
        out_shape=jax.ShapeDtypeStruct((depth, 8, n), F32),
        compiler_params=_params(2),
        name="adaln",
    )(cond8, w_ada, b_ada.reshape(depth, 1, n))


def _mod_row(i, layer, n_prompt_tiles, tiles_per_sample):
    r = jnp.where(i < n_prompt_tiles, 0, 1 + (i - n_prompt_tiles) // tiles_per_sample)
    return layer * 8 + r


def _inproj0_kernel(xp_ref, xs_ref, mod_ref, nw_ref, w_ref, o_ref, *, n_prompt_tiles):
    i = pl.program_id(0)
    x = jnp.where(i < n_prompt_tiles, xp_ref[...], xs_ref[...])
    h = _modulate(x, nw_ref[...], mod_ref[0, 0:1, :], mod_ref[0, 1:2, :])
    o_ref[...] = _dot(h.astype(BF16), w_ref[...])


def _inproj0(xp, xs, mods, norm_w, w_bf16, tiles_per_sample):
    tp, d = xp.shape
    ts = xs.shape[0]
    n = w_bf16.shape[1]
    npt, nst = tp // ROW_TILE, ts // ROW_TILE
    mod_map = lambda i: (_mod_row(i, 0, npt, tiles_per_sample), 0, 0)
    return pl.pallas_call(
        functools.partial(_inproj0_kernel, n_prompt_tiles=npt),
        grid=(npt + nst,),
        in_specs=[pl.BlockSpec((ROW_TILE, d), lambda i: (jnp.minimum(i, npt - 1), 0)),
                  pl.BlockSpec((ROW_TILE, d), lambda i: (jnp.maximum(i - npt, 0), 0)),
                  pl.BlockSpec((1, 6, d), mod_map),
                  pl.BlockSpec((1, d), lambda i: (0, 0)),
                  pl.BlockSpec((d, n), lambda i: (0, 0))],
        out_specs=pl.BlockSpec((ROW_TILE, n), lambda i: (i, 0)),
        out_shape=jax.ShapeDtypeStruct((tp + ts, n), F32),
        compiler_params=_params(1),
        name="inproj0",
    )(xp, xs, mods, norm_w, w_bf16)


def _scan_kernel(*refs, seq_len, has_state):
    if has_state:
        (p_ref, a2_ref, ab_ref, lb_ref, ong_ref, onh_ref, s0_ref,
         mixed_ref, qf, kf, qb, kb, vv, dec_f, dec_b, o_f, o_b, st_f, st_b) = refs
        sfin_ref = None
    else:
        (p_ref, a2_ref, ab_ref, lb_ref, ong_ref, onh_ref,
         mixed_ref, sfin_ref, qf, kf, qb, kb, vv, dec_f, dec_b, o_f, o_b, st_f, st_b) = refs
        s0_ref = None
    C = SCAN_CHUNK
    n_chunks = seq_len // C
    gqk = GLA_HEADS * HEAD_DK

    row = lax.broadcasted_iota(I32, (C, C), 0)
    col = lax.broadcasted_iota(I32, (C, C), 1)
    lower = col <= row
    upper = col >= row
    tri_lo = jnp.where(lower, 1.0, 0.0).astype(BF16)
    tri_up = jnp.where(upper, 1.0, 0.0).astype(BF16)

    lbp = lb_ref[...]
    lb_max = jnp.maximum(lbp[0], lbp[1])
    lb_e0 = jnp.exp(lbp[0] - lb_max)
    lb_e1 = jnp.exp(lbp[1] - lb_max)
    lb = lb_e0 / (lb_e0 + lb_e1)

    def cumsum_chunk(tri, la):
        hi, lo = _split_bf16(la)
        return _dot(tri, hi) + _dot(tri, lo)

    def prep(n, carry):
        r0 = pl.multiple_of(n * C, C)
        rows = pl.ds(r0, C)
        gq = p_ref[rows, _C_GQ:_C_GQ + gqk] * (HEAD_DK ** -0.5)
        gk = p_ref[rows, _C_GK:_C_GK + gqk]
        hq = _silu(p_ref[rows, _C_HQ:_C_HQ + gqk]) * (HEAD_DK ** -0.5)
        for d_i, (q_s, k_s, dec_s, tri, last) in enumerate(
                ((qf, kf, dec_f, tri_lo, C - 1), (qb, kb, dec_b, tri_up, 0))):
            c_ga = _C_GAF if d_i == 0 else _C_GAB
            c_hf = _C_HFF if d_i == 0 else _C_HFB
            ga = p_ref[rows, c_ga:c_ga + GATE_RANK]
            xg = _dot(ga.astype(BF16), a2_ref[d_i].astype(BF16)) + ab_ref[d_i]
            la_g = _log_sigmoid(xg) / GLA_GATE_NORM
            f = lb[d_i:d_i + 1, :] + (1.0 - lb[d_i:d_i + 1, :]) * jax.nn.sigmoid(
                p_ref[rows, c_hf:c_hf + gqk])
            la_h = jnp.log(f)
            for q, k, la, c0 in ((gq, gk, la_g, 0), (hq, 1.0 - f, la_h, gqk)):
                b = cumsum_chunk(tri, la)
                q_s[rows, c0:c0 + gqk] = (q * jnp.exp(b)).astype(BF16)
                k_s[rows, c0:c0 + gqk] = (k * jnp.exp(-b)).astype(BF16)
                dec_s[n, :, c0:c0 + gqk] = jnp.exp(b[last:last + 1, :])
        vv[rows, 0:512] = p_ref[rows, _C_GV:_C_GV + 512].astype(BF16)
        vv[rows, 512:1024] = p_ref[rows, _C_HI:_C_HI + 512].astype(BF16)
        return carry

    lax.fori_loop(0, n_chunks, prep, 0)

    for p in range(SCAN_PAIRS):
        if has_state:
            st_f[p] = s0_ref[0, 0, p]
            st_b[p] = s0_ref[0, 1, p]
        else:
            st_f[p] = jnp.zeros((2 * HEAD_DV, 2 * HEAD_DK), F32)
            st_b[p] = jnp.zeros((2 * HEAD_DV, 2 * HEAD_DK), F32)

    first_head = lax.broadcasted_iota(I32, (C, 2 * HEAD_DK), 1) < HEAD_DK
    row2 = lax.broadcasted_iota(I32, (2 * C, C), 0) % C
    col2 = lax.broadcasted_iota(I32, (2 * C, C), 1)
    lower2 = col2 <= row2
    upper2 = col2 >= row2

    def per_head_rows(x):
        z = jnp.zeros_like(x)
        return jnp.concatenate([jnp.where(first_head, x, z), jnp.where(first_head, z, x)], axis=0)

    def put_out(o_ref, rows, p, res):
        c0 = p * 2 * HEAD_DV
        o_ref[rows, c0:c0 + HEAD_DV] = res[0:C, 0:HEAD_DV]
        o_ref[rows, c0 + HEAD_DV:c0 + 2 * HEAD_DV] = res[C:2 * C, HEAD_DV:2 * HEAD_DV]

    def sweep(n, carry):
        m = n_chunks - 1 - n
        rows = pl.ds(pl.multiple_of(n * C, C), C)
        rows_m = pl.ds(pl.multiple_of(m * C, C), C)
        decay_f, decay_b = dec_f[n], dec_b[m]
        for p in range(SCAN_PAIRS):
            ks = slice(p * 2 * HEAD_DK, (p + 1) * 2 * HEAD_DK)
            vs = slice(p * 2 * HEAD_DV, (p + 1) * 2 * HEAD_DV)
            qd, kd, vh = per_head_rows(qf[rows, ks]), kf[rows, ks], vv[rows, vs]
            s_f = st_f[p]
            sc = (jnp.where(lower2, _dot_nt(qd, kd), 0.0)
                  + jnp.where(upper2, _dot_nt(per_head_rows(qb[rows, ks]), kb[rows, ks]), 0.0))
            put_out(o_f, rows, p, _dot_nt(qd, s_f.astype(BF16)) + _dot(sc.astype(BF16), vh))
            st_f[p] = decay_f[:, ks] * (s_f + _dot_tn(vh, kd))
            s_b = st_b[p]
            vm, kbm = vv[rows_m, vs], kb[rows_m, ks]
            put_out(o_b, rows_m, p, _dot_nt(per_head_rows(qb[rows_m, ks]), s_b.astype(BF16)))
            st_b[p] = decay_b[:, ks] * (s_b + _dot_tn(vm, kbm))
        return carry

    lax.fori_loop(0, n_chunks, sweep, 0)

    def finish(n, carry):
        rows = pl.ds(pl.multiple_of(n * C, C), C)
        for h in range(SCAN_HEADS):
            vs = slice(h * HEAD_DV, (h + 1) * HEAD_DV)
            if h < GLA_HEADS:
                gate = p_ref[rows, _C_GG + h * HEAD_DV:_C_GG + (h + 1) * HEAD_DV]
                onw = ong_ref[...]
            else:
                hh = h - GLA_HEADS
                gate = p_ref[rows, _C_HG + hh * HEAD_DV:_C_HG + (hh + 1) * HEAD_DV]
                onw = onh_ref[...]
            o = o_f[rows, vs] + o_b[rows, vs]
            mixed_ref[rows, vs] = ((_rms(o) * onw) * _silu(gate)).astype(BF16)
        return carry

    lax.fori_loop(0, n_chunks, finish, 0)

    if sfin_ref is not None:
        for d_i, st in enumerate((st_f, st_b)):
            for p in range(SCAN_PAIRS):
                s_pair = st[p].T
                sfin_ref[0, d_i, 2 * p] = s_pair[0:HEAD_DK, 0:HEAD_DV]
                sfin_ref[0, d_i, 2 * p + 1] = s_pair[HEAD_DK:2 * HEAD_DK, HEAD_DV:2 * HEAD_DV]


def _scan(p, row0, batch, seq_len, a2, a_bias, lb, onorm_g, onorm_h, s0=None):
    n = p.shape[1]
    assert row0 % seq_len == 0
    blk0 = row0 // seq_len
    has_state = s0 is not None
    n_chunks = seq_len // SCAN_CHUNK
    st_shape = (1, 2, SCAN_HEADS, HEAD_DK, HEAD_DV)
    pair_shape = (SCAN_PAIRS, 2 * HEAD_DV, 2 * HEAD_DK)
    in_specs = [pl.BlockSpec((seq_len, n), lambda b: (blk0 + b, 0)),
                pl.BlockSpec(a2.shape, lambda b: (0, 0, 0)),
                pl.BlockSpec(a_bias.shape, lambda b: (0, 0, 0)),
                pl.BlockSpec(lb.shape, lambda b: (0, 0, 0)),
                pl.BlockSpec((1, HEAD_DV), lambda b: (0, 0)),
                pl.BlockSpec((1, HEAD_DV), lambda b: (0, 0))]
    args = [p, a2, a_bias, lb, onorm_g, onorm_h]
    mixed_shape = jax.ShapeDtypeStruct((batch * seq_len, D_MODEL), BF16)
    mixed_spec = pl.BlockSpec((seq_len, D_MODEL), lambda b: (b, 0))
    if has_state:
        in_specs.append(pl.BlockSpec((1, 2) + pair_shape, lambda b: (b, 0, 0, 0, 0)))
        args.append(s0)
        out_shape, out_specs = mixed_shape, mixed_spec
    else:
        out_shape = (mixed_shape, jax.ShapeDtypeStruct((batch,) + st_shape[1:], F32))
        out_specs = (mixed_spec, pl.BlockSpec(st_shape, lambda b: (b, 0, 0, 0, 0)))
    scratch = [pltpu.VMEM((seq_len, 512), BF16) for _ in range(4)]
    scratch += [pltpu.VMEM((seq_len, D_MODEL), BF16),
                pltpu.VMEM((n_chunks, 1, 512), F32), pltpu.VMEM((n_chunks, 1, 512), F32),
                pltpu.VMEM((seq_len, D_MODEL), F32), pltpu.VMEM((seq_len, D_MODEL), F32),
                pltpu.VMEM(pair_shape, F32), pltpu.VMEM(pair_shape, F32)]
    return pl.pallas_call(
        functools.partial(_scan_kernel, seq_len=seq_len, has_state=has_state),
        grid=(batch,),
        in_specs=in_specs, out_specs=out_specs, out_shape=out_shape,
        scratch_shapes=scratch,
        compiler_params=_params(1),
        name="scan_state" if has_state else "scan_fresh",
    )(*args)


def _post_kernel(*refs, split_x, n_prompt_tiles):
    if split_x:
        xp_ref, xs_ref = refs[0], refs[1]
        refs = refs[2:]
    else:
        x_ref = refs[0]
        refs = refs[1:]
    (mp_ref, ms_ref, mod_ref, nw_ref, wo_ref, wrh_ref, wrl_ref,
     x1_ref, h2_ref, slot_ref, wgt_ref, tab_ref, carry) = refs
    i = pl.program_id(0)
    is_prompt = i < n_prompt_tiles
    if split_x:
        x = jnp.where(is_prompt, xp_ref[...], xs_ref[...])
    else:
        x = x_ref[...]
    mixed = jnp.where(is_prompt, mp_ref[...], ms_ref[...])
    x1 = x + mod_ref[0, 2:3, :] * _dot(mixed, wo_ref[...])
    x1_ref[...] = x1
    h2 = _modulate(x1, nw_ref[...], mod_ref[0, 3:4, :], mod_ref[0, 4:5, :])
    _to_token_major(h2_ref, h2)

    hh, hl = _split_bf16(h2)
    logits = _dot(hh, wrh_ref[...]) + _dot(hl, wrh_ref[...]) + _dot(hh, wrl_ref[...])
    tm = logits.shape[0]
    lane = lax.broadcasted_iota(I32, (tm, LANES), 1).astype(F32)

    def first_max(v):
        mx = jnp.max(v, axis=1, keepdims=True)
        idx = jnp.min(jnp.where(v == mx, lane, float(LANES)), axis=1, keepdims=True)
        return mx, idx

    gl = jnp.where(lane < N_GROUPS, logits, NEG_BIG)
    gmax, gidx = first_max(gl)
    g_val = 1.0 / jnp.sum(jnp.exp(gl - gmax), axis=1, keepdims=True)
    lo = N_GROUPS + EXPERTS_PER_GROUP * gidx
    el = jnp.where((lane >= lo) & (lane < lo + EXPERTS_PER_GROUP), logits, NEG_BIG)
    emax, l1 = first_max(el)
    esum = jnp.sum(jnp.exp(el - emax), axis=1, keepdims=True)
    e2max, l2 = first_max(jnp.where(lane == l1, NEG_BIG, el))
    p1 = 1.0 / esum
    p2 = jnp.exp(e2max - emax) / esum
    w1 = g_val * (p1 / (p1 + p2))
    w2 = g_val * (p2 / (p1 + p2))
    id1 = l1 - N_GROUPS
    id2 = l2 - N_GROUPS

    @pl.when(i == 0)
    def _():
        carry[...] = jnp.zeros_like(carry)

    sel1 = lane == id1
    sel2 = lane == id2
    onehot = jnp.where(sel1 | sel2, 1.0, 0.0)
    row = lax.broadcasted_iota(I32, (tm, tm), 0)
    col = lax.broadcasted_iota(I32, (tm, tm), 1)
    earlier = jnp.where(col < row, 1.0, 0.0).astype(BF16)
    before = _dot(earlier, onehot.astype(BF16))
    count = jnp.sum(onehot, axis=0, keepdims=True)
    blocks = jnp.floor((count + (MOVE_BLOCK - 1.0)) * (1.0 / MOVE_BLOCK)) * MOVE_BLOCK
    e_row = lax.broadcasted_iota(I32, (LANES, LANES), 0)
    e_col = lax.broadcasted_iota(I32, (LANES, LANES), 1)
    lower_experts = jnp.where(e_row < e_col, 1.0, 0.0).astype(BF16)
    run_start = _dot(jnp.broadcast_to(blocks, (8, LANES)).astype(BF16), lower_experts)[0:1]
    slot = before + run_start
    q1 = jnp.sum(jnp.where(sel1, slot, 0.0), axis=1, keepdims=True)
    q2 = jnp.sum(jnp.where(sel2, slot, 0.0), axis=1, keepdims=True)
    tab_row = lax.broadcasted_iota(I32, (8, LANES), 0)
    tab_ref[0] = jnp.where(tab_row == 0, count,
                           jnp.where(tab_row == 1, carry[...],
                                     jnp.where(tab_row == 2, run_start, 0.0)))
    carry[...] = carry[...] + count

    cols = jnp.zeros((tm, LANES), F32)
    for k, v in enumerate((q1 * TOKEN_ROWS, q2 * TOKEN_ROWS, w1, w2)):
        cols = jnp.where(lane == k, v, cols)
    rows = cols.T
    slot_ref[0] = rows[0:2].astype(I32)
    wgt_ref[0] = rows[2:4]


def _post(x_args, mixed_p, mixed_s, mods, layer, norm_w, w_out_bf16, wr_hi, wr_lo,
          tiles_per_sample):
    split_x = len(x_args) == 2
    tp, ts = mixed_p.shape[0], mixed_s.shape[0]
    t, d = tp + ts, D_MODEL
    npt, nst = tp // ROW_TILE, ts // ROW_TILE
    tile = lambda i: (i, 0)
    if split_x:
        x_specs = [pl.BlockSpec((ROW_TILE, d), lambda i: (jnp.minimum(i, npt - 1), 0)),
                   pl.BlockSpec((ROW_TILE, d), lambda i: (jnp.maximum(i - npt, 0), 0))]
    else:
        x_specs = [pl.BlockSpec((ROW_TILE, d), tile)]
    in_specs = x_specs + [
        pl.BlockSpec((ROW_TILE, d), lambda i: (jnp.minimum(i, npt - 1), 0)),
        pl.BlockSpec((ROW_TILE, d), lambda i: (jnp.maximum(i - npt, 0), 0)),
        pl.BlockSpec((1, 6, d), lambda i: (_mod_row(i, layer, npt, tiles_per_sample), 0, 0)),
        pl.BlockSpec((1, d), lambda i: (0, 0)),
        pl.BlockSpec((d, d), lambda i: (0, 0)),
        pl.BlockSpec((d, LANES), lambda i: (0, 0)),
        pl.BlockSpec((d, LANES), lambda i: (0, 0))]
    return pl.pallas_call(
        functools.partial(_post_kernel, split_x=split_x, n_prompt_tiles=npt),
        grid=(npt + nst,),
        in_specs=in_specs,
        out_specs=(pl.BlockSpec((ROW_TILE, d), tile),
                   pl.BlockSpec((ROW_TILE * TOKEN_ROWS, LANES), tile),
                   pl.BlockSpec((1, 2, ROW_TILE), lambda i: (i, 0, 0)),
                   pl.BlockSpec((1, 2, ROW_TILE), lambda i: (i, 0, 0)),
                   pl.BlockSpec((1, 8, LANES), lambda i: (i, 0, 0))),
        out_shape=(jax.ShapeDtypeStruct((t, d), F32),
                   jax.ShapeDtypeStruct((t * TOKEN_ROWS, LANES), F32),
                   jax.ShapeDtypeStruct((npt + nst, 2, ROW_TILE), I32),
                   jax.ShapeDtypeStruct((npt + nst, 2, ROW_TILE), F32),
                   jax.ShapeDtypeStruct((npt + nst, 8, LANES), F32)),
        scratch_shapes=[pltpu.VMEM((1, LANES), F32)],
        compiler_params=_params(1),
        name=f"post{layer}",
    )(*x_args, mixed_p, mixed_s, mods, norm_w, w_out_bf16, wr_hi, wr_lo)


def _for_blocks(tab_ref, fn):
    block_rows = MOVE_BLOCK * TOKEN_ROWS
    count = tab_ref[0, 0, MAX_BLOCKS]

    def call(k, parity):
        fn(pl.multiple_of(k * block_rows, block_rows),
           pl.multiple_of(tab_ref[0, 0, k], TOKEN_ROWS), parity)

    def body(k2, c):
        call(2 * k2, 0)

        @pl.when(2 * k2 + 1 < count)
        def _():
            call(2 * k2 + 1, 1)
        return c

    lax.fori_loop(0, _cdiv(count, 2), body, 0)


def _wait_blocks(tab_ref, copy):
    def body(k, c):
        copy.wait()
        return c

    lax.fori_loop(0, tab_ref[0, 0, MAX_BLOCKS], body, 0)


def _dispatch_kernel(zero_ref, tab_ref, prev_tab_ref, q_ref, h2_ref, hs_ref, zero_buf, stage, sem):
    j = pl.program_id(0)
    slot = j % 2
    block_rows = MOVE_BLOCK * TOKEN_ROWS

    @pl.when(j == 0)
    def _():
        zero_buf[...] = jnp.zeros_like(zero_buf)

        def zero_copy(k):
            start = pl.multiple_of(zero_ref[k], EXPERT_TILE * TOKEN_ROWS)
            return pltpu.make_async_copy(
                zero_buf, hs_ref.at[pl.ds(start, EXPERT_TILE * TOKEN_ROWS)], sem.at[0])

        def start_zero(k, c):
            @pl.when(zero_ref[k] >= 0)
            def _():
                zero_copy(k).start()
            return c

        def wait_zero(k, c):
            @pl.when(zero_ref[k] >= 0)
            def _():
                zero_copy(k).wait()
            return c

        lax.fori_loop(0, zero_ref.shape[0], start_zero, 0)
        lax.fori_loop(0, zero_ref.shape[0], wait_zero, 0)

        stage[...] = jnp.zeros_like(stage)

    def place(r, c):
        tok = h2_ref[pl.ds(pl.multiple_of(r * TOKEN_ROWS, TOKEN_ROWS), TOKEN_ROWS), :]
        for s in range(2):
            row = pl.multiple_of(q_ref[0, s, r], TOKEN_ROWS)
            stage[slot, pl.ds(row, TOKEN_ROWS), :] = tok
        return c

    lax.fori_loop(0, ROW_TILE, place, 0, unroll=8)

    def block_copy(buf, stage_row, sorted_row):
        return pltpu.make_async_copy(stage.at[buf, pl.ds(stage_row, block_rows)],
                                     hs_ref.at[pl.ds(sorted_row, block_rows)], sem.at[buf])

    @pl.when(j > 0)
    def _():
        _wait_blocks(prev_tab_ref, block_copy(1 - slot, 0, 0))

    _for_blocks(tab_ref, lambda a, b, parity: block_copy(slot, a, b).start(priority=parity))

    @pl.when(j == pl.num_programs(0) - 1)
    def _():
        _wait_blocks(tab_ref, block_copy(slot, 0, 0))


def _dispatch(zero_tiles, block_tab, slots, h2, n_rows):
    t = h2.shape[0] // TOKEN_ROWS
    nt = t // ROW_TILE
    smem_tile = lambda shape: pl.BlockSpec((1,) + shape, lambda j, *_: (j, 0, 0),
                                           memory_space=pltpu.SMEM)
    grid_spec = pltpu.PrefetchScalarGridSpec(
        num_scalar_prefetch=1,
        grid=(nt,),
        in_specs=[smem_tile((1, LANES)),
                  pl.BlockSpec((1, 1, LANES), lambda j, *_: (jnp.maximum(j - 1, 0), 0, 0),
                               memory_space=pltpu.SMEM),
                  smem_tile((2, ROW_TILE)),
                  pl.BlockSpec((ROW_TILE * TOKEN_ROWS, LANES), lambda j, *_: (j, 0))],
        out_specs=pl.BlockSpec(memory_space=pl.ANY),
        scratch_shapes=[pltpu.VMEM((EXPERT_TILE * TOKEN_ROWS, LANES), F32),
                        pltpu.VMEM((2, STAGE_TOKENS * TOKEN_ROWS, LANES), F32),
                        pltpu.SemaphoreType.DMA((2,))])
    return pl.pallas_call(
        _dispatch_kernel,
        grid_spec=grid_spec,
        out_shape=jax.ShapeDtypeStruct((n_rows * TOKEN_ROWS, LANES), F32),
        compiler_params=_params(1),
        name="dispatch",
    )(zero_tiles, block_tab, block_tab, slots, h2)


def _expert_kernel(te_ref, src_ref, nv_ref, hs_ref, w1_ref, w3_ref, w2_ref, ys_ref,
                   w1b, w3b, w2b):
    i = pl.program_id(0)
    prev = te_ref[jnp.maximum(i - 1, 0)]

    @pl.when((i == 0) | (te_ref[i] != prev))
    def _():
        w1b[...] = w1_ref[0, 0].astype(BF16)
        w3b[...] = w3_ref[0, 0].astype(BF16)
        w2b[...] = w2_ref[0, 0].astype(BF16)

    @pl.when(nv_ref[i] > 0)
    def _():
        h = _from_token_major(hs_ref, EXPERT_TILE).astype(BF16)
        g = _silu(_dot(h, w1b[...])) * _dot(h, w3b[...])
        _to_token_major(ys_ref, _dot(g.astype(BF16), w2b[...]))

    @pl.when(nv_ref[i] == 0)
    def _():
        ys_ref[...] = jnp.zeros_like(ys_ref)


def _experts(tile_expert, tile_src, tile_rows, hs, w1, w3, w2, layer):
    n_rows, d = hs.shape[0] // TOKEN_ROWS, D_MODEL
    nt = n_rows // EXPERT_TILE
    hid = w1.shape[-1]
    tok_tile = (EXPERT_TILE * TOKEN_ROWS, LANES)
    row_map = lambda i, te, src, nv: (src[i], 0)
    grid_spec = pltpu.PrefetchScalarGridSpec(
        num_scalar_prefetch=3,
        grid=(nt,),
        in_specs=[pl.BlockSpec(tok_tile, row_map),
                  pl.BlockSpec((1, 1, d, hid), lambda i, te, src, nv: (layer, te[i], 0, 0)),
                  pl.BlockSpec((1, 1, d, hid), lambda i, te, src, nv: (layer, te[i], 0, 0)),
                  pl.BlockSpec((1, 1, hid, d), lambda i, te, src, nv: (layer, te[i], 0, 0))],
        out_specs=pl.BlockSpec(tok_tile, lambda i, te, src, nv: (i, 0)),
        scratch_shapes=[pltpu.VMEM((d, hid), BF16), pltpu.VMEM((d, hid), BF16),
                        pltpu.VMEM((hid, d), BF16)])
    return pl.pallas_call(
        _expert_kernel,
        grid_spec=grid_spec,
        out_shape=jax.ShapeDtypeStruct(hs.shape, F32),
        compiler_params=_params(1),
        name=f"experts{layer}",
    )(tile_expert, tile_src, tile_rows, hs, w1, w3, w2)


def _combine_kernel(tab_ref, next_tab_ref, q_ref, w_ref, x1_ref, mod_ref, fw_ref, ys_ref, out_ref,
                    stage, y_tok, sem, *, final_norm):
    i = pl.program_id(0)
    slot = i % 2
    block_rows = MOVE_BLOCK * TOKEN_ROWS

    def block_copy(buf, stage_row, sorted_row):
        return pltpu.make_async_copy(ys_ref.at[pl.ds(sorted_row, block_rows)],
                                     stage.at[buf, pl.ds(stage_row, block_rows)], sem.at[buf])

    def fetch(tab, buf):
        _for_blocks(tab, lambda a, b, parity: block_copy(buf, a, b).start(priority=parity))

    @pl.when(i == 0)
    def _():
        fetch(tab_ref, slot)

    @pl.when(i + 1 < pl.num_programs(0))
    def _():
        fetch(next_tab_ref, 1 - slot)

    _wait_blocks(tab_ref, block_copy(slot, 0, 0))

    def pick(r, c):
        rows = [stage[slot, pl.ds(pl.multiple_of(q_ref[0, s, r], TOKEN_ROWS), TOKEN_ROWS), :]
                for s in range(2)]
        y_tok[pl.ds(pl.multiple_of(r * TOKEN_ROWS, TOKEN_ROWS), TOKEN_ROWS), :] = (
            w_ref[0, 0, r] * rows[0] + w_ref[0, 1, r] * rows[1])
        return c

    lax.fori_loop(0, ROW_TILE, pick, 0, unroll=8)
    x2 = x1_ref[...] + mod_ref[0, 5:6, :] * _from_token_major(y_tok, ROW_TILE)
    if final_norm:
        x2 = _rms(x2) * fw_ref[...]
    out_ref[...] = x2


def _combine(block_tab, slots, weights, x1, mods, layer, final_w, ys, tile0, n_tiles,
             n_prompt_tiles, tiles_per_sample, final_norm):
    d = D_MODEL
    tile = lambda i: (tile0 + i, 0)
    mod_map = lambda i: (_mod_row(tile0 + i, layer, n_prompt_tiles, tiles_per_sample), 0, 0)
    smem_tile = lambda shape: pl.BlockSpec((1,) + shape, lambda i: (tile0 + i, 0, 0),
                                           memory_space=pltpu.SMEM)
    return pl.pallas_call(
        functools.partial(_combine_kernel, final_norm=final_norm),
        grid=(n_tiles,),
        in_specs=[smem_tile((1, LANES)),
                  pl.BlockSpec((1, 1, LANES),
                               lambda i: (tile0 + jnp.minimum(i + 1, n_tiles - 1), 0, 0),
                               memory_space=pltpu.SMEM),
                  smem_tile((2, ROW_TILE)), smem_tile((2, ROW_TILE)),
                  pl.BlockSpec((ROW_TILE, d), tile),
                  pl.BlockSpec((1, 6, d), mod_map),
                  pl.BlockSpec((1, d), lambda i: (0, 0)),
                  pl.BlockSpec(memory_space=pl.ANY)],
        out_specs=pl.BlockSpec((ROW_TILE, d), lambda i: (i, 0)),
        out_shape=jax.ShapeDtypeStruct((n_tiles * ROW_TILE, d), F32),
        scratch_shapes=[pltpu.VMEM((2, STAGE_TOKENS * TOKEN_ROWS, LANES), F32),
                        pltpu.VMEM((ROW_TILE * TOKEN_ROWS, LANES), F32),
                        pltpu.SemaphoreType.DMA((2,))],
        compiler_params=_params(1),
        name=f"combine{layer}_{tile0}",
    )(block_tab, block_tab, slots, weights, x1, mods, final_w, ys)


def _moe(h2, slots, weights, tile_tab, w1, w3, w2, layer):
    t = h2.shape[0] // TOKEN_ROWS
    n_tiles = t // ROW_TILE
    extra_tiles = N_EXPERTS + _cdiv(N_EXPERTS * MOVE_BLOCK, EXPERT_TILE)
    n_rows = 2 * t + extra_tiles * EXPERT_TILE
    nt = n_rows // EXPERT_TILE
    tab = tile_tab[:, :, :N_EXPERTS].astype(I32)
    cnt = tab[-1, 0] + tab[-1, 1]
    tight = _cdiv(cnt, EXPERT_TILE) * EXPERT_TILE
    padded = jnp.where(cnt > 0, _cdiv(cnt + MOVE_BLOCK - 1, EXPERT_TILE) * EXPERT_TILE, 0)
    ends = jnp.cumsum(padded)
    offsets = ends - padded
    tails = jnp.where(cnt > 0, ends - EXPERT_TILE, -1)
    tails2 = jnp.where(padded > tight, ends - 2 * EXPERT_TILE, -1)
    used = ends[-1] // EXPERT_TILE
    tile_start = jnp.arange(nt, dtype=I32) * EXPERT_TILE
    unused = (used + jnp.arange(extra_tiles, dtype=I32)) * EXPERT_TILE
    zero_tiles = jnp.concatenate([tails, tails2, jnp.where(unused < n_rows, unused, -1)])
    zero_tiles = jnp.where(zero_tiles >= 0, zero_tiles * TOKEN_ROWS, -1).astype(I32)
    tile_src = jnp.minimum(jnp.arange(nt, dtype=I32), used - 1)
    tile_expert = jnp.sum((tile_src * EXPERT_TILE)[:, None] >= ends[None, :], axis=1).astype(I32)
    tile_rows = jnp.where(tile_start < ends[-1],
                          jnp.clip(cnt[tile_expert] - (tile_start - offsets[tile_expert]),
                                   0, EXPERT_TILE), 0).astype(I32)
    n_blocks = _cdiv(tab[:, 0], MOVE_BLOCK)
    blocks_through = jnp.cumsum(n_blocks, axis=1)
    k = jnp.arange(MAX_BLOCKS, dtype=I32)
    owner = jnp.sum(blocks_through[:, None, :] <= k[None, :, None], axis=2)
    is_owner = owner[:, :, None] == jnp.arange(N_EXPERTS, dtype=I32)[None, None, :]
    pick = lambda v: jnp.sum(jnp.where(is_owner, v[:, None, :], 0), axis=2)
    run_first = pick(offsets[None, :] + tab[:, 1])
    block_in_run = k[None, :] - pick(blocks_through - n_blocks)
    sorted_row = (run_first + block_in_run * MOVE_BLOCK) * TOKEN_ROWS
    block_tab = jnp.concatenate(
        [sorted_row, blocks_through[:, -1:],
         jnp.zeros((n_tiles, LANES - MAX_BLOCKS - 1), I32)], axis=1).astype(I32)[:, None, :]
    hs = _dispatch(zero_tiles, block_tab, slots, h2, n_rows)
    ys = _experts(tile_expert, tile_src, tile_rows, hs, w1, w3, w2, layer)
    return ys, (block_tab, slots, weights)


def _rope(x, cos, sin_signed):
    lane = lax.broadcasted_iota(I32, (x.shape[0], LANES), 1)
    low = (lane % 32) < 16
    outs = []
    for j in range(x.shape[1] // LANES):
        xb = x[:, j * LANES:(j + 1) * LANES]
        partner = jnp.where(low, pltpu.roll(xb, LANES - 16, 1), pltpu.roll(xb, 16, 1))
        outs.append(xb * cos + partner * sin_signed)
    return jnp.concatenate(outs, axis=1)


def _inproj1_prompt_kernel(x_ref, mod_ref, nw_ref, w_ref, q_ref, k_ref, v_ref, kc_ref, vc_ref):
    d = D_MODEL
    h = _modulate(x_ref[...], nw_ref[...], mod_ref[0, 0:1, :], mod_ref[0, 1:2, :]).astype(BF16)
    q_ref[...] = (_dot(h, w_ref[:, 0:d]) * (DIFF_HD ** -0.5)).astype(BF16)
    k = _dot(h, w_ref[:, d:2 * d])
    v = _dot(h, w_ref[:, 2 * d:3 * d])
    k_ref[...] = k.astype(BF16)
    v_ref[...] = v.astype(BF16)
    kc_ref[...] = k.T
    _to_token_major(vc_ref, v)


def _inproj1_sample_kernel(x_ref, mod_ref, nw_ref, w_ref, cos_ref, sin_ref, q_ref, k_ref, v_ref):
    d = D_MODEL
    h = _modulate(x_ref[...], nw_ref[...], mod_ref[0, 0:1, :], mod_ref[0, 1:2, :]).astype(BF16)
    cos, sin = cos_ref[...], sin_ref[...]
    q_ref[...] = (_rope(_dot(h, w_ref[:, 0:d]), cos, sin) * (DIFF_HD ** -0.5)).astype(BF16)
    k_ref[...] = _rope(_dot(h, w_ref[:, d:2 * d]), cos, sin).astype(BF16)
    v_ref[...] = _dot(h, w_ref[:, 2 * d:3 * d]).astype(BF16)


def _inproj1(x, mods, norm_w, w_bf16, n_prompt_tiles, n_sample_tiles, tiles_per_sample,
             cos_t, sin_t):
    d = D_MODEL
    npt, nst = n_prompt_tiles, n_sample_tiles
    common = [pl.BlockSpec((1, d), lambda i: (0, 0)), pl.BlockSpec((d, 3 * d), lambda i: (0, 0))]
    tile = lambda i: (i, 0)
    out_specs = tuple(pl.BlockSpec((ROW_TILE, d), tile) for _ in range(3))
    qp, kp, vp, k_cache, v_cache = pl.pallas_call(
        _inproj1_prompt_kernel,
        grid=(npt,),
        in_specs=[pl.BlockSpec((ROW_TILE, d), tile),
                  pl.BlockSpec((1, 6, d), lambda i: (8, 0, 0))] + common,
        out_specs=out_specs + (pl.BlockSpec((d, ROW_TILE), tile),
                               pl.BlockSpec((ROW_TILE * TOKEN_ROWS, LANES), tile)),
        out_shape=tuple(jax.ShapeDtypeStruct((npt * ROW_TILE, d), BF16) for _ in range(3))
        + (jax.ShapeDtypeStruct((npt * d, ROW_TILE), F32),
           jax.ShapeDtypeStruct((npt * ROW_TILE * TOKEN_ROWS, LANES), F32)),
        compiler_params=_params(1),
        name="inproj1_prompt",
    )(x, mods, norm_w, w_bf16)
    rope_tile = lambda i: (i % tiles_per_sample, 0)
    qs, ks, vs = pl.pallas_call(
        _inproj1_sample_kernel,
        grid=(nst,),
        in_specs=[pl.BlockSpec((ROW_TILE, d), lambda i: (npt + i, 0)),
                  pl.BlockSpec((1, 6, d), lambda i: (8 + 1 + i // tiles_per_sample, 0, 0))]
        + common + [pl.BlockSpec((ROW_TILE, LANES), rope_tile),
                    pl.BlockSpec((ROW_TILE, LANES), rope_tile)],
        out_specs=out_specs,
        out_shape=tuple(jax.ShapeDtypeStruct((nst * ROW_TILE, d), BF16) for _ in range(3)),
        compiler_params=_params(1),
        name="inproj1_sample",
    )(x, mods, norm_w, w_bf16, cos_t, sin_t)
    return (qp, kp, vp), (qs, ks, vs), (k_cache, v_cache)


def _rope_tables(n_tok):
    half = DIFF_HD // 4
    pos = np.arange(n_tok)
    lane = np.arange(LANES)
    sub = lane % DIFF_HD
    p = np.where(sub[None, :] < DIFF_HD // 2, (pos // GRID_W)[:, None], (pos % GRID_W)[:, None])
    inv = jnp.asarray(ROPE_THETA, F32) ** (-jnp.asarray(sub % half, F32) / half)
    ang = jnp.asarray(p, F32) * inv[None, :]
    sign = np.where((lane % (2 * half)) < half, -1.0, 1.0).astype(np.float32)
    return jnp.cos(ang), jnp.sin(ang) * sign[None, :]


def _diffattn_kernel(*refs, has_cache, lam_init):
    if has_cache:
        q_ref, k_ref, v_ref, ck_ref, cv_ref, lam_ref, sw_ref, o_ref = refs
    else:
        q_ref, k_ref, v_ref, lam_ref, sw_ref, o_ref = refs
    hd2 = 2 * DIFF_HD
    lv = lam_ref[...]
    lam = (jnp.exp(jnp.sum(lv[0:1] * lv[1:2], axis=1, keepdims=True))
           - jnp.exp(jnp.sum(lv[2:3] * lv[3:4], axis=1, keepdims=True)) + lam_init)
    lane = lax.broadcasted_iota(I32, (q_ref.shape[0], hd2), 1)
    for h in range(DIFF_HEADS):
        cols = slice(h * hd2, (h + 1) * hd2)
        q = q_ref[:, cols]
        zero = jnp.zeros_like(q)
        k_new = k_ref[:, cols].astype(BF16)
        values = [v_ref[:, cols].astype(BF16)]
        if has_cache:
            past = ck_ref.shape[1]
            k_past_t = ck_ref[cols, :].astype(BF16)
            values.append(cv_ref[pl.ds(h, past, stride=DIFF_HEADS), :].astype(BF16))
        o = None
        for c in range(2):
            qc = jnp.where((lane < DIFF_HD) == (c == 0), q, zero)
            s = [_dot_nt(qc, k_new)]
            if has_cache:
                s.append(_dot(qc, k_past_t))
            mx = functools.reduce(jnp.maximum, [jnp.max(si, axis=1, keepdims=True) for si in s])
            e = [jnp.exp(si - mx) for si in s]
            z = functools.reduce(jnp.add, [jnp.sum(ei, axis=1, keepdims=True) for ei in e])
            pv = functools.reduce(jnp.add, [_dot(ei.astype(BF16), v) for ei, v in zip(e, values)])
            pv = pv * (1.0 / z)
            o = pv if c == 0 else o - lam * pv
        o_ref[:, cols] = ((_rms(o) * sw_ref[...]) * (1.0 - lam_init)).astype(BF16)


def _diffattn(q, k, v, lam_vecs, subln_w, batch, seq_len, q_block, lam_init, cache=None):
    d = D_MODEL
    nq = seq_len // q_block
    has_cache = cache is not None
    kv_spec = pl.BlockSpec((seq_len, d), lambda b, qi: (b, 0))
    in_specs = [pl.BlockSpec((q_block, d), lambda b, qi: (b * nq + qi, 0)), kv_spec, kv_spec]
    args = [q, k, v]
    if has_cache:
        past = cache[0].shape[1]
        in_specs += [pl.BlockSpec((d, past), lambda b, qi: (b, 0)),
                     pl.BlockSpec((past * DIFF_HEADS, 2 * DIFF_HD), lambda b, qi: (b, 0))]
        args += list(cache)
    in_specs += [pl.BlockSpec((4, DIFF_HD), lambda b, qi: (0, 0)),
                 pl.BlockSpec((1, 2 * DIFF_HD), lambda b, qi: (0, 0))]
    args += [lam_vecs, subln_w]
    return pl.pallas_call(
        functools.partial(_diffattn_kernel, has_cache=has_cache, lam_init=lam_init),
        grid=(batch, nq),
        in_specs=in_specs,
        out_specs=pl.BlockSpec((q_block, d), lambda b, qi: (b * nq + qi, 0)),
        out_shape=jax.ShapeDtypeStruct((batch * seq_len, d), BF16),
        compiler_params=_params(2),
        name="diffattn_cache" if has_cache else "diffattn",
    )(*args)


def _router_weights(router_group, router_expert):
    w = jnp.concatenate([router_group, router_expert], axis=1)
    w = jnp.pad(w, ((0, 0), (0, LANES - w.shape[1])))
    hi = w.astype(BF16)
    return hi, (w - hi.astype(F32)).astype(BF16)


def _inproj0_weights(w_in):
    gq, gk, gv, gg, gaf, gab, hq, hff, hfb, hi, hg = jnp.split(
        w_in, [256, 512, 1024, 1536, 1552, 1568, 1824, 2080, 2336, 2848], axis=1)
    w = jnp.concatenate([gq, gk, gv, gg, hq, hff, hfb, hi, hg, gaf, gab], axis=1)
    return jnp.pad(w, ((0, 0), (0, AB_COLS - w.shape[1]))).astype(BF16)


def kernel(x_prompt, x_sample, state_gla, state_hgrn, cache_diff_k, cache_diff_v, c, c_ctx,
           w_ada, b_ada, norm1_w, norm2_w, w_in_ab, gla_a2, gla_a_bias, hgrn_lb, gla_onorm_w,
           hgrn_onorm_w, w_out_ab, w_in_c, lam_q1, lam_k1, lam_q2, lam_k2, diff_subln_w, w_out_c,
           router_group, router_expert, moe_w1, moe_w3, moe_w2, final_norm_w):
    bp, lp, d = x_prompt.shape
    bs, ls, _ = x_sample.shape
    depth = w_ada.shape[0]
    assert depth == 2 and d == D_MODEL and bs <= 7
    tp, ts = bp * lp, bs * ls
    npt, nst = tp // ROW_TILE, ts // ROW_TILE
    tps = ls // ROW_TILE
    xp = x_prompt.reshape(tp, d)
    xs = x_sample.reshape(ts, d)

    cond8 = jnp.concatenate([c_ctx[None, :], c, jnp.zeros((7 - bs, d), F32)], axis=0)
    mods = _adaln(cond8, w_ada, b_ada).reshape(depth * 8, 6, d)

    proj = _inproj0(xp, xs, mods, norm1_w[0:1], _inproj0_weights(w_in_ab[0]), tps)
    a_bias = gla_a_bias[0][:, None, :]
    scan_args = (gla_a2[0], a_bias, hgrn_lb, gla_onorm_w[0:1], hgrn_onorm_w[0:1])
    mixed_p, s_fin = _scan(proj, 0, bp, lp, *scan_args)
    s0 = jnp.concatenate([state_gla[:, 0], state_hgrn[:, 0]], axis=2).swapaxes(-1, -2)
    s0 = s0.reshape(bs, 2, SCAN_PAIRS, 2, HEAD_DV, HEAD_DK)
    zero = jnp.zeros_like(s0[:, :, :, 0])
    s0 = jnp.concatenate([jnp.concatenate([s0[:, :, :, 0], zero], axis=-1),
                          jnp.concatenate([zero, s0[:, :, :, 1]], axis=-1)], axis=-2)
    mixed_s = _scan(proj, tp, bs, ls, *scan_args, s0=s0)
    new_state_gla = s_fin[:, None, :, :GLA_HEADS]
    new_state_hgrn = s_fin[:, None, :, GLA_HEADS:]

    wr = _router_weights(router_group[0], router_expert[0])
    x1, *routed = _post((xp, xs), mixed_p, mixed_s, mods, 0, norm2_w[0:1],
                        w_out_ab[0].astype(BF16), *wr, tps)
    ys, tables = _moe(*routed, moe_w1, moe_w3, moe_w2, 0)
    x2 = _combine(*tables, x1, mods, 0, final_norm_w[None, :], ys, 0, npt + nst, npt, tps, False)

    lam_init = 0.8 - 0.6 * math.exp(-0.3 * 1)
    cos_t, sin_t = _rope_tables(ls)
    (qp, kp, vp), (qs, ks, vs), (k_cache, v_cache) = _inproj1(
        x2, mods, norm1_w[1:2], w_in_c[0].astype(BF16), npt, nst, tps, cos_t, sin_t)
    lam_vecs = jnp.stack([lam_q1[0], lam_k1[0], lam_q2[0], lam_k2[0]])
    att_p = _diffattn(qp, kp, vp, lam_vecs, diff_subln_w[0:1], bp, lp, lp, lam_init)
    past = cache_diff_k.shape[2]
    assert lp == ROW_TILE and DIFF_HEADS == TOKEN_ROWS
    cache = (cache_diff_k[:, 0].transpose(0, 2, 3, 4, 1).reshape(bs * d, past),
             cache_diff_v[:, 0].reshape(bs * past * DIFF_HEADS, 2 * DIFF_HD))
    att_s = _diffattn(qs, ks, vs, lam_vecs, diff_subln_w[0:1], bs, ls, ROW_TILE, lam_init, cache)

    wr = _router_weights(router_group[1], router_expert[1])
    x3, *routed = _post((x2,), att_p, att_s, mods, 1, norm2_w[1:2],
                        w_out_c[0].astype(BF16), *wr, tps)
    ys, tables = _moe(*routed, moe_w1, moe_w3, moe_w2, 1)
    fw = final_norm_w[None, :]
    y_p = _combine(*tables, x3, mods, 1, fw, ys, 0, npt, npt, tps, True)
    y_s = _combine(*tables, x3, mods, 1, fw, ys, npt, nst, npt, tps, True)

    return (y_p.reshape(bp, lp, d), y_s.reshape(bs, ls, d), new_state_gla, new_state_hgrn,
            k_cache.reshape(bp, 1, DIFF_HEADS, 2, DIFF_HD, lp).transpose(0, 1, 5, 2, 3, 4),
            v_cache.reshape(bp, 1, lp, DIFF_HEADS, 2 * DIFF_HD))
```

```python
import functools
import math

import jax
import jax.numpy as jnp
import numpy as np
from jax import lax
from jax.experimental import pallas as pl
from jax.experimental.pallas import tpu as pltpu

F32 = jnp.float32
BF16 = jnp.bfloat16
I32 = jnp.int32

D_MODEL = 1024
GLA_HEADS = 4
HGRN_HEADS = 4
SCAN_HEADS = GLA_HEADS + HGRN_HEADS
SCAN_PAIRS = SCAN_HEADS // 2
HEAD_DK = 64
HEAD_DV = 128
GATE_RANK = 16
GLA_GATE_NORM = 16.0
DIFF_HEADS = 8
DIFF_HD = 64
GRID_W = 64
ROPE_THETA = 10000.0
N_GROUPS = 4
EXPERTS_PER_GROUP = 8
N_EXPERTS = N_GROUPS * EXPERTS_PER_GROUP
MOE_HIDDEN = 512
EPS = 1e-6
LANES = 128
TOKEN_ROWS = D_MODEL // LANES
NEG_BIG = -1e30

ROW_TILE = 256
SCAN_CHUNK = 64
EXPERT_TILE = 256
MOVE_BLOCK = 8
STAGE_TOKENS = 2 * ROW_TILE + N_EXPERTS * MOVE_BLOCK
MAX_BLOCKS = STAGE_TOKENS // MOVE_BLOCK
VMEM_LIMIT = 56 * 1024 * 1024

_C_GQ, _C_GK, _C_GV, _C_GG = 0, 256, 512, 1024
_C_HQ, _C_HFF, _C_HFB, _C_HI, _C_HG = 1536, 1792, 2048, 2304, 2816
_C_GAF, _C_GAB = 3328, 3344
AB_COLS = 3456


def _params(n_axes, vmem=VMEM_LIMIT):
    return pltpu.CompilerParams(dimension_semantics=("arbitrary",) * n_axes,
                                vmem_limit_bytes=vmem)


def _cdiv(a, b):
    return (a + b - 1) // b


def _dot(a, b):
    return jnp.dot(a, b, preferred_element_type=F32)


def _dot_nt(a, b):
    return lax.dot_general(a, b, (((1,), (1,)), ((), ())), preferred_element_type=F32)


def _dot_tn(a, b):
    return lax.dot_general(a, b, (((0,), (0,)), ((), ())), preferred_element_type=F32)


def _split_bf16(x):
    hi = x.astype(BF16)
    lo = (x - hi.astype(F32)).astype(BF16)
    return hi, lo


def _silu(x):
    return x * jax.nn.sigmoid(x)


def _log_sigmoid(x):
    return jnp.minimum(x, 0.0) - jnp.log(1.0 + jnp.exp(-jnp.abs(x)))


def _rms(x):
    return x * lax.rsqrt(jnp.mean(x * x, axis=-1, keepdims=True) + EPS)


def _modulate(x, norm_w, shift, scale):
    return (_rms(x) * norm_w) * (1.0 + scale) + shift


def _to_token_major(dst_ref, x, row0=0):
    n = x.shape[0]
    for s in range(TOKEN_ROWS):
        dst_ref[pl.ds(row0 + s, n, stride=TOKEN_ROWS), :] = x[:, s * LANES:(s + 1) * LANES]


def _from_token_major(src_ref, n, row0=0):
    return jnp.concatenate([src_ref[pl.ds(row0 + s, n, stride=TOKEN_ROWS), :]
                            for s in range(TOKEN_ROWS)], axis=1)


def _ada_kernel(c_ref, w_ref, b_ref, o_ref):
    s = _silu(c_ref[...])
    o_ref[0] = _dot(s.astype(BF16), w_ref[0].astype(BF16)) + b_ref[0]


def _adaln(cond8, w_ada, b_ada):
    depth, d, n = w_ada.shape
    tn = 1536
    return pl.pallas_call(
        _ada_kernel,
        grid=(depth, n // tn),
        in_specs=[pl.BlockSpec((8, d), lambda l, j: (0, 0)),
                  pl.BlockSpec((1, d, tn), lambda l, j: (l, 0, j)),
                  pl.BlockSpec((1, 1, tn), lambda l, j: (l, 0, j))],
        out_specs=pl.BlockSpec((1, 8, tn), lambda l, j: (l, 0, j)),
        out_shape=jax.ShapeDtypeStruct((depth, 8, n), F32),
        compiler_params=_params(2),
        name="adaln",
    )(cond8, w_ada, b_ada.reshape(depth, 1, n))


def _mod_row(i, layer, n_prompt_tiles, tiles_per_sample):
    r = jnp.where(i < n_prompt_tiles, 0, 1 + (i - n_prompt_tiles) // tiles_per_sample)
    return layer * 8 + r


def _inproj0_kernel(xp_ref, xs_ref, mod_ref, nw_ref, w_ref, o_ref, *, n_prompt_tiles):
    i = pl.program_id(0)
    x = jnp.where(i < n_prompt_tiles, xp_ref[...], xs_ref[...])
    h = _modulate(x, nw_ref[...], mod_ref[0, 0:1, :], mod_ref[0, 1:2, :])
    o_ref[...] = _dot(h.astype(BF16), w_ref[...])


def _inproj0(xp, xs, mods, norm_w, w_bf16, tiles_per_sample):
    tp, d = xp.shape
    ts = xs.shape[0]
    n = w_bf16.shape[1]
    npt, nst = tp // ROW_TILE, ts // ROW_TILE
    mod_map = lambda i: (_mod_row(i, 0, npt, tiles_per_sample), 0, 0)
    return pl.pallas_call(
        functools.partial(_inproj0_kernel, n_prompt_tiles=npt),
        grid=(npt + nst,),
        in_specs=[pl.BlockSpec((ROW_TILE, d), lambda i: (jnp.minimum(i, npt - 1), 0)),
                  pl.BlockSpec((ROW_TILE, d), lambda i: (jnp.maximum(i - npt, 0), 0)),
                  pl.BlockSpec((1, 6, d), mod_map),
                  pl.BlockSpec((1, d), lambda i: (0, 0)),
                  pl.BlockSpec((d, n), lambda i: (0, 0))],
        out_specs=pl.BlockSpec((ROW_TILE, n), lambda i: (i, 0)),
        out_shape=jax.ShapeDtypeStruct((tp + ts, n), F32),
        compiler_params=_params(1),
        name="inproj0",
    )(xp, xs, mods, norm_w, w_bf16)


def _scan_kernel(*refs, seq_len, has_state):
    if has_state:
        (p_ref, a2_ref, ab_ref, lb_ref, ong_ref, onh_ref, s0_ref,
         mixed_ref, qf, kf, qb, kb, vv, dec_f, dec_b, o_f, o_b, st_f, st_b) = refs
        sfin_ref = None
    else:
        (p_ref, a2_ref, ab_ref, lb_ref, ong_ref, onh_ref,
         mixed_ref, sfin_ref, qf, kf, qb, kb, vv, dec_f, dec_b, o_f, o_b, st_f, st_b) = refs
        s0_ref = None
    C = SCAN_CHUNK
    n_chunks = seq_len // C
    gqk = GLA_HEADS * HEAD_DK

    row = lax.broadcasted_iota(I32, (C, C), 0)
    col = lax.broadcasted_iota(I32, (C, C), 1)
    lower = col <= row
    upper = col >= row
    tri_lo = jnp.where(lower, 1.0, 0.0).astype(BF16)
    tri_up = jnp.where(upper, 1.0, 0.0).astype(BF16)

    lbp = lb_ref[...]
    lb_max = jnp.maximum(lbp[0], lbp[1])
    lb_e0 = jnp.exp(lbp[0] - lb_max)
    lb_e1 = jnp.exp(lbp[1] - lb_max)
    lb = lb_e0 / (lb_e0 + lb_e1)

    def cumsum_chunk(tri, la):
        hi, lo = _split_bf16(la)
        return _dot(tri, hi) + _dot(tri, lo)

    def prep(n, carry):
        r0 = pl.multiple_of(n * C, C)
        rows = pl.ds(r0, C)
        gq = p_ref[rows, _C_GQ:_C_GQ + gqk] * (HEAD_DK ** -0.5)
        gk = p_ref[rows, _C_GK:_C_GK + gqk]
        hq = _silu(p_ref[rows, _C_HQ:_C_HQ + gqk]) * (HEAD_DK ** -0.5)
        for d_i, (q_s, k_s, dec_s, tri, last) in enumerate(
                ((qf, kf, dec_f, tri_lo, C - 1), (qb, kb, dec_b, tri_up, 0))):
            c_ga = _C_GAF if d_i == 0 else _C_GAB
            c_hf = _C_HFF if d_i == 0 else _C_HFB
            ga = p_ref[rows, c_ga:c_ga + GATE_RANK]
            xg = _dot(ga.astype(BF16), a2_ref[d_i].astype(BF16)) + ab_ref[d_i]
            la_g = _log_sigmoid(xg) / GLA_GATE_NORM
            f = lb[d_i:d_i + 1, :] + (1.0 - lb[d_i:d_i + 1, :]) * jax.nn.sigmoid(
                p_ref[rows, c_hf:c_hf + gqk])
            la_h = jnp.log(f)
            for q, k, la, c0 in ((gq, gk, la_g, 0), (hq, 1.0 - f, la_h, gqk)):
                b = cumsum_chunk(tri, la)
                q_s[rows, c0:c0 + gqk] = (q * jnp.exp(b)).astype(BF16)
                k_s[rows, c0:c0 + gqk] = (k * jnp.exp(-b)).astype(BF16)
                dec_s[n, :, c0:c0 + gqk] = jnp.exp(b[last:last + 1, :])
        vv[rows, 0:512] = p_ref[rows, _C_GV:_C_GV + 512].astype(BF16)
        vv[rows, 512:1024] = p_ref[rows, _C_HI:_C_HI + 512].astype(BF16)
        return carry

    lax.fori_loop(0, n_chunks, prep, 0)

    for p in range(SCAN_PAIRS):
        if has_state:
            st_f[p] = s0_ref[0, 0, p]
            st_b[p] = s0_ref[0, 1, p]
        else:
            st_f[p] = jnp.zeros((2 * HEAD_DV, 2 * HEAD_DK), F32)
            st_b[p] = jnp.zeros((2 * HEAD_DV, 2 * HEAD_DK), F32)

    first_head = lax.broadcasted_iota(I32, (C, 2 * HEAD_DK), 1) < HEAD_DK
    row2 = lax.broadcasted_iota(I32, (2 * C, C), 0) % C
    col2 = lax.broadcasted_iota(I32, (2 * C, C), 1)
    lower2 = col2 <= row2
    upper2 = col2 >= row2

    def per_head_rows(x):
        z = jnp.zeros_like(x)
        return jnp.concatenate([jnp.where(first_head, x, z), jnp.where(first_head, z, x)], axis=0)

    def put_out(o_ref, rows, p, res):
        c0 = p * 2 * HEAD_DV
        o_ref[rows, c0:c0 + HEAD_DV] = res[0:C, 0:HEAD_DV]
        o_ref[rows, c0 + HEAD_DV:c0 + 2 * HEAD_DV] = res[C:2 * C, HEAD_DV:2 * HEAD_DV]

    def sweep(n, carry):
        m = n_chunks - 1 - n
        rows = pl.ds(pl.multiple_of(n * C, C), C)
        rows_m = pl.ds(pl.multiple_of(m * C, C), C)
        decay_f, decay_b = dec_f[n], dec_b[m]
        for p in range(SCAN_PAIRS):
            ks = slice(p * 2 * HEAD_DK, (p + 1) * 2 * HEAD_DK)
            vs = slice(p * 2 * HEAD_DV, (p + 1) * 2 * HEAD_DV)
            qd, kd, vh = per_head_rows(qf[rows, ks]), kf[rows, ks], vv[rows, vs]
            s_f = st_f[p]
            sc = (jnp.where(lower2, _dot_nt(qd, kd), 0.0)
                  + jnp.where(upper2, _dot_nt(per_head_rows(qb[rows, ks]), kb[rows, ks]), 0.0))
            put_out(o_f, rows, p, _dot_nt(qd, s_f.astype(BF16)) + _dot(sc.astype(BF16), vh))
            st_f[p] = decay_f[:, ks] * (s_f + _dot_tn(vh, kd))
            s_b = st_b[p]
            vm, kbm = vv[rows_m, vs], kb[rows_m, ks]
            put_out(o_b, rows_m, p, _dot_nt(per_head_rows(qb[rows_m, ks]), s_b.astype(BF16)))
            st_b[p] = decay_b[:, ks] * (s_b + _dot_tn(vm, kbm))
        return carry

    lax.fori_loop(0, n_chunks, sweep, 0)

    def finish(n, carry):
        rows = pl.ds(pl.multiple_of(n * C, C), C)
        for h in range(SCAN_HEADS):
            vs = slice(h * HEAD_DV, (h + 1) * HEAD_DV)
            if h < GLA_HEADS:
                gate = p_ref[rows, _C_GG + h * HEAD_DV:_C_GG + (h + 1) * HEAD_DV]
                onw = ong_ref[...]
            else:
                hh = h - GLA_HEADS
                gate = p_ref[rows, _C_HG + hh * HEAD_DV:_C_HG + (hh + 1) * HEAD_DV]
                onw = onh_ref[...]
            o = o_f[rows, vs] + o_b[rows, vs]
            mixed_ref[rows, vs] = ((_rms(o) * onw) * _silu(gate)).astype(BF16)
        return carry

    lax.fori_loop(0, n_chunks, finish, 0)

    if sfin_ref is not None:
        for d_i, st in enumerate((st_f, st_b)):
            for p in range(SCAN_PAIRS):
                s_pair = st[p].T
                sfin_ref[0, d_i, 2 * p] = s_pair[0:HEAD_DK, 0:HEAD_DV]
                sfin_ref[0, d_i, 2 * p + 1] = s_pair[HEAD_DK:2 * HEAD_DK, HEAD_DV:2 * HEAD_DV]


def _scan(p, row0, batch, seq_len, a2, a_bias, lb, onorm_g, onorm_h, s0=None):
    n = p.shape[1]
    assert row0 % seq_len == 0
    blk0 = row0 // seq_len
    has_state = s0 is not None
    n_chunks = seq_len // SCAN_CHUNK
    st_shape = (1, 2, SCAN_HEADS, HEAD_DK, HEAD_DV)
    pair_shape = (SCAN_PAIRS, 2 * HEAD_DV, 2 * HEAD_DK)
    in_specs = [pl.BlockSpec((seq_len, n), lambda b: (blk0 + b, 0)),
                pl.BlockSpec(a2.shape, lambda b: (0, 0, 0)),
                pl.BlockSpec(a_bias.shape, lambda b: (0, 0, 0)),
                pl.BlockSpec(lb.shape, lambda b: (0, 0, 0)),
                pl.BlockSpec((1, HEAD_DV), lambda b: (0, 0)),
                pl.BlockSpec((1, HEAD_DV), lambda b: (0, 0))]
    args = [p, a2, a_bias, lb, onorm_g, onorm_h]
    mixed_shape = jax.ShapeDtypeStruct((batch * seq_len, D_MODEL), BF16)
    mixed_spec = pl.BlockSpec((seq_len, D_MODEL), lambda b: (b, 0))
    if has_state:
        in_specs.append(pl.BlockSpec((1, 2) + pair_shape, lambda b: (b, 0, 0, 0, 0)))
        args.append(s0)
        out_shape, out_specs = mixed_shape, mixed_spec
    else:
        out_shape = (mixed_shape, jax.ShapeDtypeStruct((batch,) + st_shape[1:], F32))
        out_specs = (mixed_spec, pl.BlockSpec(st_shape, lambda b: (b, 0, 0, 0, 0)))
    scratch = [pltpu.VMEM((seq_len, 512), BF16) for _ in range(4)]
    scratch += [pltpu.VMEM((seq_len, D_MODEL), BF16),
                pltpu.VMEM((n_chunks, 1, 512), F32), pltpu.VMEM((n_chunks, 1, 512), F32),
                pltpu.VMEM((seq_len, D_MODEL), F32), pltpu.VMEM((seq_len, D_MODEL), F32),
                pltpu.VMEM(pair_shape, F32), pltpu.VMEM(pair_shape, F32)]
    return pl.pallas_call(
        functools.partial(_scan_kernel, seq_len=seq_len, has_state=has_state),
        grid=(batch,),
        in_specs=in_specs, out_specs=out_specs, out_shape=out_shape,
        scratch_shapes=scratch,
        compiler_params=_params(1),
        name="scan_state" if has_state else "scan_fresh",
    )(*args)


def _post_kernel(*refs, split_x, n_prompt_tiles):
    if split_x:
        xp_ref, xs_ref = refs[0], refs[1]
        refs = refs[2:]
    else:
        x_ref = refs[0]
        refs = refs[1:]
    (mp_ref, ms_ref, mod_ref, nw_ref, wo_ref, wrh_ref, wrl_ref,
     x1_ref, h2_ref, slot_ref, wgt_ref, tab_ref, carry) = refs
    i = pl.program_id(0)
    is_prompt = i < n_prompt_tiles
    if split_x:
        x = jnp.where(is_prompt, xp_ref[...], xs_ref[...])
    else:
        x = x_ref[...]
    mixed = jnp.where(is_prompt, mp_ref[...], ms_ref[...])
    x1 = x + mod_ref[0, 2:3, :] * _dot(mixed, wo_ref[...])
    x1_ref[...] = x1
    h2 = _modulate(x1, nw_ref[...], mod_ref[0, 3:4, :], mod_ref[0, 4:5, :])
    _to_token_major(h2_ref, h2)

    hh, hl = _split_bf16(h2)
    logits = _dot(hh, wrh_ref[...]) + _dot(hl, wrh_ref[...]) + _dot(hh, wrl_ref[...])
    tm = logits.shape[0]
    lane = lax.broadcasted_iota(I32, (tm, LANES), 1).astype(F32)

    def first_max(v):
        mx = jnp.max(v, axis=1, keepdims=True)
        idx = jnp.min(jnp.where(v == mx, lane, float(LANES)), axis=1, keepdims=True)
        return mx, idx

    gl = jnp.where(lane < N_GROUPS, logits, NEG_BIG)
    gmax, gidx = first_max(gl)
    g_val = 1.0 / jnp.sum(jnp.exp(gl - gmax), axis=1, keepdims=True)
    lo = N_GROUPS + EXPERTS_PER_GROUP * gidx
    el = jnp.where((lane >= lo) & (lane < lo + EXPERTS_PER_GROUP), logits, NEG_BIG)
    emax, l1 = first_max(el)
    esum = jnp.sum(jnp.exp(el - emax), axis=1, keepdims=True)
    e2max, l2 = first_max(jnp.where(lane == l1, NEG_BIG, el))
    p1 = 1.0 / esum
    p2 = jnp.exp(e2max - emax) / esum
    w1 = g_val * (p1 / (p1 + p2))
    w2 = g_val * (p2 / (p1 + p2))
    id1 = l1 - N_GROUPS
    id2 = l2 - N_GROUPS

    @pl.when(i == 0)
    def _():
        carry[...] = jnp.zeros_like(carry)

    sel1 = lane == id1
    sel2 = lane == id2
    onehot = jnp.where(sel1 | sel2, 1.0, 0.0)
    row = lax.broadcasted_iota(I32, (tm, tm), 0)
    col = lax.broadcasted_iota(I32, (tm, tm), 1)
    earlier = jnp.where(col < row, 1.0, 0.0).astype(BF16)
    before = _dot(earlier, onehot.astype(BF16))
    count = jnp.sum(onehot, axis=0, keepdims=True)
    blocks = jnp.floor((count + (MOVE_BLOCK - 1.0)) * (1.0 / MOVE_BLOCK)) * MOVE_BLOCK
    e_row = lax.broadcasted_iota(I32, (LANES, LANES), 0)
    e_col = lax.broadcasted_iota(I32, (LANES, LANES), 1)
    lower_experts = jnp.where(e_row < e_col, 1.0, 0.0).astype(BF16)
    run_start = _dot(jnp.broadcast_to(blocks, (8, LANES)).astype(BF16), lower_experts)[0:1]
    slot = before + run_start
    q1 = jnp.sum(jnp.where(sel1, slot, 0.0), axis=1, keepdims=True)
    q2 = jnp.sum(jnp.where(sel2, slot, 0.0), axis=1, keepdims=True)
    tab_row = lax.broadcasted_iota(I32, (8, LANES), 0)
    tab_ref[0] = jnp.where(tab_row == 0, count,
                           jnp.where(tab_row == 1, carry[...],
                                     jnp.where(tab_row == 2, run_start, 0.0)))
    carry[...] = carry[...] + count

    cols = jnp.zeros((tm, LANES), F32)
    for k, v in enumerate((q1 * TOKEN_ROWS, q2 * TOKEN_ROWS, w1, w2)):
        cols = jnp.where(lane == k, v, cols)
    rows = cols.T
    slot_ref[0] = rows[0:2].astype(I32)
    wgt_ref[0] = rows[2:4]


def _post(x_args, mixed_p, mixed_s, mods, layer, norm_w, w_out_bf16, wr_hi, wr_lo,
          tiles_per_sample):
    split_x = len(x_args) == 2
    tp, ts = mixed_p.shape[0], mixed_s.shape[0]
    t, d = tp + ts, D_MODEL
    npt, nst = tp // ROW_TILE, ts // ROW_TILE
    tile = lambda i: (i, 0)
    if split_x:
        x_specs = [pl.BlockSpec((ROW_TILE, d), lambda i: (jnp.minimum(i, npt - 1), 0)),
                   pl.BlockSpec((ROW_TILE, d), lambda i: (jnp.maximum(i - npt, 0), 0))]
    else:
        x_specs = [pl.BlockSpec((ROW_TILE, d), tile)]
    in_specs = x_specs + [
        pl.BlockSpec((ROW_TILE, d), lambda i: (jnp.minimum(i, npt - 1), 0)),
        pl.BlockSpec((ROW_TILE, d), lambda i: (jnp.maximum(i - npt, 0), 0)),
        pl.BlockSpec((1, 6, d), lambda i: (_mod_row(i, layer, npt, tiles_per_sample), 0, 0)),
        pl.BlockSpec((1, d), lambda i: (0, 0)),
        pl.BlockSpec((d, d), lambda i: (0, 0)),
        pl.BlockSpec((d, LANES), lambda i: (0, 0)),
        pl.BlockSpec((d, LANES), lambda i: (0, 0))]
    return pl.pallas_call(
        functools.partial(_post_kernel, split_x=split_x, n_prompt_tiles=npt),
        grid=(npt + nst,),
        in_specs=in_specs,
        out_specs=(pl.BlockSpec((ROW_TILE, d), tile),
                   pl.BlockSpec((ROW_TILE * TOKEN_ROWS, LANES), tile),
                   pl.BlockSpec((1, 2, ROW_TILE), lambda i: (i, 0, 0)),
                   pl.BlockSpec((1, 2, ROW_TILE), lambda i: (i, 0, 0)),
                   pl.BlockSpec((1, 8, LANES), lambda i: (i, 0, 0))),
        out_shape=(jax.ShapeDtypeStruct((t, d), F32),
                   jax.ShapeDtypeStruct((t * TOKEN_ROWS, LANES), F32),
                   jax.ShapeDtypeStruct((npt + nst, 2, ROW_TILE), I32),
                   jax.ShapeDtypeStruct((npt + nst, 2, ROW_TILE), F32),
                   jax.ShapeDtypeStruct((npt + nst, 8, LANES), F32)),
        scratch_shapes=[pltpu.VMEM((1, LANES), F32)],
        compiler_params=_params(1),
        name=f"post{layer}",
    )(*x_args, mixed_p, mixed_s, mods, norm_w, w_out_bf16, wr_hi, wr_lo)


def _for_blocks(tab_ref, fn):
    block_rows = MOVE_BLOCK * TOKEN_ROWS
    count = tab_ref[0, 0, MAX_BLOCKS]

    def call(k, parity):
        fn(pl.multiple_of(k * block_rows, block_rows),
           pl.multiple_of(tab_ref[0, 0, k], TOKEN_ROWS), parity)

    def body(k2, c):
        call(2 * k2, 0)

        @pl.when(2 * k2 + 1 < count)
        def _():
            call(2 * k2 + 1, 1)
        return c

    lax.fori_loop(0, _cdiv(count, 2), body, 0)


def _wait_blocks(tab_ref, copy):
    def body(k, c):
        copy.wait()
        return c

    lax.fori_loop(0, tab_ref[0, 0, MAX_BLOCKS], body, 0)


def _dispatch_kernel(zero_ref, tab_ref, prev_tab_ref, q_ref, h2_ref, hs_ref, zero_buf, stage, sem):
    j = pl.program_id(0)
    slot = j % 2
    block_rows = MOVE_BLOCK * TOKEN_ROWS

    @pl.when(j == 0)
    def _():
        zero_buf[...] = jnp.zeros_like(zero_buf)

        def zero_copy(k):
            start = pl.multiple_of(zero_ref[k], EXPERT_TILE * TOKEN_ROWS)
            return pltpu.make_async_copy(
                zero_buf, hs_ref.at[pl.ds(start, EXPERT_TILE * TOKEN_ROWS)], sem.at[0])

        def start_zero(k, c):
            @pl.when(zero_ref[k] >= 0)
            def _():
                zero_copy(k).start()
            return c

        def wait_zero(k, c):
            @pl.when(zero_ref[k] >= 0)
            def _():
                zero_copy(k).wait()
            return c

        lax.fori_loop(0, zero_ref.shape[0], start_zero, 0)
        lax.fori_loop(0, zero_ref.shape[0], wait_zero, 0)

        stage[...] = jnp.zeros_like(stage)

    def place(r, c):
        tok = h2_ref[pl.ds(pl.multiple_of(r * TOKEN_ROWS, TOKEN_ROWS), TOKEN_ROWS), :]
        for s in range(2):
            row = pl.multiple_of(q_ref[0, s, r], TOKEN_ROWS)
            stage[slot, pl.ds(row, TOKEN_ROWS), :] = tok
        return c

    lax.fori_loop(0, ROW_TILE, place, 0, unroll=8)

    def block_copy(buf, stage_row, sorted_row):
        return pltpu.make_async_copy(stage.at[buf, pl.ds(stage_row, block_rows)],
                                     hs_ref.at[pl.ds(sorted_row, block_rows)], sem.at[buf])

    @pl.when(j > 0)
    def _():
        _wait_blocks(prev_tab_ref, block_copy(1 - slot, 0, 0))

    _for_blocks(tab_ref, lambda a, b, parity: block_copy(slot, a, b).start(priority=parity))

    @pl.when(j == pl.num_programs(0) - 1)
    def _():
        _wait_blocks(tab_ref, block_copy(slot, 0, 0))


def _dispatch(zero_tiles, block_tab, slots, h2, n_rows):
    t = h2.shape[0] // TOKEN_ROWS
    nt = t // ROW_TILE
    smem_tile = lambda shape: pl.BlockSpec((1,) + shape, lambda j, *_: (j, 0, 0),
                                           memory_space=pltpu.SMEM)
    grid_spec = pltpu.PrefetchScalarGridSpec(
        num_scalar_prefetch=1,
        grid=(nt,),
        in_specs=[smem_tile((1, LANES)),
                  pl.BlockSpec((1, 1, LANES), lambda j, *_: (jnp.maximum(j - 1, 0), 0, 0),
                               memory_space=pltpu.SMEM),
                  smem_tile((2, ROW_TILE)),
                  pl.BlockSpec((ROW_TILE * TOKEN_ROWS, LANES), lambda j, *_: (j, 0))],
        out_specs=pl.BlockSpec(memory_space=pl.ANY),
        scratch_shapes=[pltpu.VMEM((EXPERT_TILE * TOKEN_ROWS, LANES), F32),
                        pltpu.VMEM((2, STAGE_TOKENS * TOKEN_ROWS, LANES), F32),
                        pltpu.SemaphoreType.DMA((2,))])
    return pl.pallas_call(
        _dispatch_kernel,
        grid_spec=grid_spec,
        out_shape=jax.ShapeDtypeStruct((n_rows * TOKEN_ROWS, LANES), F32),
        compiler_params=_params(1),
        name="dispatch",
    )(zero_tiles, block_tab, block_tab, slots, h2)


def _expert_kernel(te_ref, src_ref, nv_ref, run_ref, nxt_ref, hs_ref, w1_hbm, w3_hbm, w2_hbm,
                   ys_ref, w1f, w3f, w2f, w1b, w3b, w2b, sem, *, layer):
    i = pl.program_id(0)

    def weight_copies(e, buf):
        return [pltpu.make_async_copy(src.at[layer, e], dst.at[buf], sem.at[buf])
                for src, dst in ((w1_hbm, w1f), (w3_hbm, w3f), (w2_hbm, w2f))]

    @pl.when(i == 0)
    def _():
        for c in weight_copies(te_ref[0], 0):
            c.start()

    first = (i == 0) | (run_ref[i] != run_ref[jnp.maximum(i - 1, 0)])

    @pl.when(first)
    def _():
        buf = run_ref[i] % 2
        for c in weight_copies(te_ref[i], buf):
            c.wait()

        @pl.when(nxt_ref[i] >= 0)
        def _():
            for c in weight_copies(nxt_ref[i], 1 - buf):
                c.start()

        w1b[...] = w1f[buf].astype(BF16)
        w3b[...] = w3f[buf].astype(BF16)
        w2b[...] = w2f[buf].astype(BF16)

    @pl.when(nv_ref[i] > 0)
    def _():
        h = _from_token_major(hs_ref, EXPERT_TILE).astype(BF16)
        g = _silu(_dot(h, w1b[...])) * _dot(h, w3b[...])
        _to_token_major(ys_ref, _dot(g.astype(BF16), w2b[...]))

    @pl.when(nv_ref[i] == 0)
    def _():
        ys_ref[...] = jnp.zeros_like(ys_ref)


def _experts(tile_expert, tile_src, tile_rows, hs, w1, w3, w2, layer):
    n_rows, d = hs.shape[0] // TOKEN_ROWS, D_MODEL
    nt = n_rows // EXPERT_TILE
    hid = w1.shape[-1]
    tok_tile = (EXPERT_TILE * TOKEN_ROWS, LANES)
    changed = jnp.concatenate([jnp.zeros((1,), I32),
                               (tile_expert[1:] != tile_expert[:-1]).astype(I32)])
    run = jnp.cumsum(changed).astype(I32)
    later = jnp.where(run[None, :] > run[:, None], tile_expert[None, :], N_EXPERTS)
    next_expert = jnp.min(later, axis=1)
    next_expert = jnp.where(next_expert < N_EXPERTS, next_expert, -1).astype(I32)
    grid_spec = pltpu.PrefetchScalarGridSpec(
        num_scalar_prefetch=5,
        grid=(nt,),
        in_specs=[pl.BlockSpec(tok_tile, lambda i, te, src, nv, run, nxt: (src[i], 0)),
                  pl.BlockSpec(memory_space=pl.ANY), pl.BlockSpec(memory_space=pl.ANY),
                  pl.BlockSpec(memory_space=pl.ANY)],
        out_specs=pl.BlockSpec(tok_tile, lambda i, te, src, nv, run, nxt: (i, 0)),
        scratch_shapes=[pltpu.VMEM((2, d, hid), F32), pltpu.VMEM((2, d, hid), F32),
                        pltpu.VMEM((2, hid, d), F32),
                        pltpu.VMEM((d, hid), BF16), pltpu.VMEM((d, hid), BF16),
                        pltpu.VMEM((hid, d), BF16), pltpu.SemaphoreType.DMA((2,))])
    return pl.pallas_call(
        functools.partial(_expert_kernel, layer=layer),
        grid_spec=grid_spec,
        out_shape=jax.ShapeDtypeStruct(hs.shape, F32),
        compiler_params=_params(1),
        name=f"experts{layer}",
    )(tile_expert, tile_src, tile_rows, run, next_expert, hs, w1, w3, w2)


def _combine_kernel(tab_ref, next_tab_ref, q_ref, w_ref, x1_ref, mod_ref, fw_ref, ys_ref, out_ref,
                    stage, y_tok, sem, *, final_norm):
    i = pl.program_id(0)
    slot = i % 2
    block_rows = MOVE_BLOCK * TOKEN_ROWS

    def block_copy(buf, stage_row, sorted_row):
        return pltpu.make_async_copy(ys_ref.at[pl.ds(sorted_row, block_rows)],
                                     stage.at[buf, pl.ds(stage_row, block_rows)], sem.at[buf])

    def fetch(tab, buf):
        _for_blocks(tab, lambda a, b, parity: block_copy(buf, a, b).start(priority=parity))

    @pl.when(i == 0)
    def _():
        fetch(tab_ref, slot)

    @pl.when(i + 1 < pl.num_programs(0))
    def _():
        fetch(next_tab_ref, 1 - slot)

    _wait_blocks(tab_ref, block_copy(slot, 0, 0))

    def pick(r, c):
        rows = [stage[slot, pl.ds(pl.multiple_of(q_ref[0, s, r], TOKEN_ROWS), TOKEN_ROWS), :]
                for s in range(2)]
        y_tok[pl.ds(pl.multiple_of(r * TOKEN_ROWS, TOKEN_ROWS), TOKEN_ROWS), :] = (
            w_ref[0, 0, r] * rows[0] + w_ref[0, 1, r] * rows[1])
        return c

    lax.fori_loop(0, ROW_TILE, pick, 0, unroll=8)
    x2 = x1_ref[...] + mod_ref[0, 5:6, :] * _from_token_major(y_tok, ROW_TILE)
    if final_norm:
        x2 = _rms(x2) * fw_ref[...]
    out_ref[...] = x2


def _combine(block_tab, slots, weights, x1, mods, layer, final_w, ys, tile0, n_tiles,
             n_prompt_tiles, tiles_per_sample, final_norm):
    d = D_MODEL
    tile = lambda i: (tile0 + i, 0)
    mod_map = lambda i: (_mod_row(tile0 + i, layer, n_prompt_tiles, tiles_per_sample), 0, 0)
    smem_tile = lambda shape: pl.BlockSpec((1,) + shape, lambda i: (tile0 + i, 0, 0),
                                           memory_space=pltpu.SMEM)
    return pl.pallas_call(
        functools.partial(_combine_kernel, final_norm=final_norm),
        grid=(n_tiles,),
        in_specs=[smem_tile((1, LANES)),
                  pl.BlockSpec((1, 1, LANES),
                               lambda i: (tile0 + jnp.minimum(i + 1, n_tiles - 1), 0, 0),
                               memory_space=pltpu.SMEM),
                  smem_tile((2, ROW_TILE)), smem_tile((2, ROW_TILE)),
                  pl.BlockSpec((ROW_TILE, d), tile),
                  pl.BlockSpec((1, 6, d), mod_map),
                  pl.BlockSpec((1, d), lambda i: (0, 0)),
                  pl.BlockSpec(memory_space=pl.ANY)],
        out_specs=pl.BlockSpec((ROW_TILE, d), lambda i: (i, 0)),
        out_shape=jax.ShapeDtypeStruct((n_tiles * ROW_TILE, d), F32),
        scratch_shapes=[pltpu.VMEM((2, STAGE_TOKENS * TOKEN_ROWS, LANES), F32),
                        pltpu.VMEM((ROW_TILE * TOKEN_ROWS, LANES), F32),
                        pltpu.SemaphoreType.DMA((2,))],
        compiler_params=_params(1),
        name=f"combine{layer}_{tile0}",
    )(block_tab, block_tab, slots, weights, x1, mods, final_w, ys)


def _moe(h2, slots, weights, tile_tab, w1, w3, w2, layer):
    t = h2.shape[0] // TOKEN_ROWS
    n_tiles = t // ROW_TILE
    extra_tiles = N_EXPERTS + _cdiv(N_EXPERTS * MOVE_BLOCK, EXPERT_TILE)
    n_rows = 2 * t + extra_tiles * EXPERT_TILE
    nt = n_rows // EXPERT_TILE
    tab = tile_tab[:, :, :N_EXPERTS].astype(I32)
    cnt = tab[-1, 0] + tab[-1, 1]
    tight = _cdiv(cnt, EXPERT_TILE) * EXPERT_TILE
    padded = jnp.where(cnt > 0, _cdiv(cnt + MOVE_BLOCK - 1, EXPERT_TILE) * EXPERT_TILE, 0)
    ends = jnp.cumsum(padded)
    offsets = ends - padded
    tails = jnp.where(cnt > 0, ends - EXPERT_TILE, -1)
    tails2 = jnp.where(padded > tight, ends - 2 * EXPERT_TILE, -1)
    used = ends[-1] // EXPERT_TILE
    tile_start = jnp.arange(nt, dtype=I32) * EXPERT_TILE
    unused = (used + jnp.arange(extra_tiles, dtype=I32)) * EXPERT_TILE
    zero_tiles = jnp.concatenate([tails, tails2, jnp.where(unused < n_rows, unused, -1)])
    zero_tiles = jnp.where(zero_tiles >= 0, zero_tiles * TOKEN_ROWS, -1).astype(I32)
    tile_src = jnp.minimum(jnp.arange(nt, dtype=I32), used - 1)
    tile_expert = jnp.sum((tile_src * EXPERT_TILE)[:, None] >= ends[None, :], axis=1).astype(I32)
    tile_rows = jnp.where(tile_start < ends[-1],
                          jnp.clip(cnt[tile_expert] - (tile_start - offsets[tile_expert]),
                                   0, EXPERT_TILE), 0).astype(I32)
    n_blocks = _cdiv(tab[:, 0], MOVE_BLOCK)
    blocks_through = jnp.cumsum(n_blocks, axis=1)
    k = jnp.arange(MAX_BLOCKS, dtype=I32)
    owner = jnp.sum(blocks_through[:, None, :] <= k[None, :, None], axis=2)
    is_owner = owner[:, :, None] == jnp.arange(N_EXPERTS, dtype=I32)[None, None, :]
    pick = lambda v: jnp.sum(jnp.where(is_owner, v[:, None, :], 0), axis=2)
    run_first = pick(offsets[None, :] + tab[:, 1])
    block_in_run = k[None, :] - pick(blocks_through - n_blocks)
    sorted_row = (run_first + block_in_run * MOVE_BLOCK) * TOKEN_ROWS
    block_tab = jnp.concatenate(
        [sorted_row, blocks_through[:, -1:],
         jnp.zeros((n_tiles, LANES - MAX_BLOCKS - 1), I32)], axis=1).astype(I32)[:, None, :]
    hs = _dispatch(zero_tiles, block_tab, slots, h2, n_rows)
    ys = _experts(tile_expert, tile_src, tile_rows, hs, w1, w3, w2, layer)
    return ys, (block_tab, slots, weights)


def _rope(x, cos, sin_signed):
    lane = lax.broadcasted_iota(I32, (x.shape[0], LANES), 1)
    low = (lane % 32) < 16
    outs = []
    for j in range(x.shape[1] // LANES):
        xb = x[:, j * LANES:(j + 1) * LANES]
        partner = jnp.where(low, pltpu.roll(xb, LANES - 16, 1), pltpu.roll(xb, 16, 1))
        outs.append(xb * cos + partner * sin_signed)
    return jnp.concatenate(outs, axis=1)


def _inproj1_prompt_kernel(x_ref, mod_ref, nw_ref, w_ref, q_ref, k_ref, v_ref, kc_ref, vc_ref):
    d = D_MODEL
    h = _modulate(x_ref[...], nw_ref[...], mod_ref[0, 0:1, :], mod_ref[0, 1:2, :]).astype(BF16)
    q_ref[...] = (_dot(h, w_ref[:, 0:d]) * (DIFF_HD ** -0.5)).astype(BF16)
    k = _dot(h, w_ref[:, d:2 * d])
    v = _dot(h, w_ref[:, 2 * d:3 * d])
    k_ref[...] = k.astype(BF16)
    v_ref[...] = v.astype(BF16)
    kc_ref[...] = k.T
    _to_token_major(vc_ref, v)


def _inproj1_sample_kernel(x_ref, mod_ref, nw_ref, w_ref, cos_ref, sin_ref, q_ref, k_ref, v_ref):
    d = D_MODEL
    h = _modulate(x_ref[...], nw_ref[...], mod_ref[0, 0:1, :], mod_ref[0, 1:2, :]).astype(BF16)
    cos, sin = cos_ref[...], sin_ref[...]
    q_ref[...] = (_rope(_dot(h, w_ref[:, 0:d]), cos, sin) * (DIFF_HD ** -0.5)).astype(BF16)
    k_ref[...] = _rope(_dot(h, w_ref[:, d:2 * d]), cos, sin).astype(BF16)
    v_ref[...] = _dot(h, w_ref[:, 2 * d:3 * d]).astype(BF16)


def _inproj1(x, mods, norm_w, w_bf16, n_prompt_tiles, n_sample_tiles, tiles_per_sample,
             cos_t, sin_t):
    d = D_MODEL
    npt, nst = n_prompt_tiles, n_sample_tiles
    common = [pl.BlockSpec((1, d), lambda i: (0, 0)), pl.BlockSpec((d, 3 * d), lambda i: (0, 0))]
    tile = lambda i: (i, 0)
    out_specs = tuple(pl.BlockSpec((ROW_TILE, d), tile) for _ in range(3))
    qp, kp, vp, k_cache, v_cache = pl.pallas_call(
        _inproj1_prompt_kernel,
        grid=(npt,),
        in_specs=[pl.BlockSpec((ROW_TILE, d), tile),
                  pl.BlockSpec((1, 6, d), lambda i: (8, 0, 0))] + common,
        out_specs=out_specs + (pl.BlockSpec((d, ROW_TILE), tile),
                               pl.BlockSpec((ROW_TILE * TOKEN_ROWS, LANES), tile)),
        out_shape=tuple(jax.ShapeDtypeStruct((npt * ROW_TILE, d), BF16) for _ in range(3))
        + (jax.ShapeDtypeStruct((npt * d, ROW_TILE), F32),
           jax.ShapeDtypeStruct((npt * ROW_TILE * TOKEN_ROWS, LANES), F32)),
        compiler_params=_params(1),
        name="inproj1_prompt",
    )(x, mods, norm_w, w_bf16)
    rope_tile = lambda i: (i % tiles_per_sample, 0)
    qs, ks, vs = pl.pallas_call(
        _inproj1_sample_kernel,
        grid=(nst,),
        in_specs=[pl.BlockSpec((ROW_TILE, d), lambda i: (npt + i, 0)),
                  pl.BlockSpec((1, 6, d), lambda i: (8 + 1 + i // tiles_per_sample, 0, 0))]
        + common + [pl.BlockSpec((ROW_TILE, LANES), rope_tile),
                    pl.BlockSpec((ROW_TILE, LANES), rope_tile)],
        out_specs=out_specs,
        out_shape=tuple(jax.ShapeDtypeStruct((nst * ROW_TILE, d), BF16) for _ in range(3)),
        compiler_params=_params(1),
        name="inproj1_sample",
    )(x, mods, norm_w, w_bf16, cos_t, sin_t)
    return (qp, kp, vp), (qs, ks, vs), (k_cache, v_cache)


def _rope_tables(n_tok):
    half = DIFF_HD // 4
    pos = np.arange(n_tok)
    lane = np.arange(LANES)
    sub = lane % DIFF_HD
    p = np.where(sub[None, :] < DIFF_HD // 2, (pos // GRID_W)[:, None], (pos % GRID_W)[:, None])
    inv = jnp.asarray(ROPE_THETA, F32) ** (-jnp.asarray(sub % half, F32) / half)
    ang = jnp.asarray(p, F32) * inv[None, :]
    sign = np.where((lane % (2 * half)) < half, -1.0, 1.0).astype(np.float32)
    return jnp.cos(ang), jnp.sin(ang) * sign[None, :]


def _diffattn_kernel(*refs, has_cache, lam_init):
    if has_cache:
        q_ref, k_ref, v_ref, ck_ref, cv_ref, lam_ref, sw_ref, o_ref = refs
    else:
        q_ref, k_ref, v_ref, lam_ref, sw_ref, o_ref = refs
    hd2 = 2 * DIFF_HD
    lv = lam_ref[...]
    lam = (jnp.exp(jnp.sum(lv[0:1] * lv[1:2], axis=1, keepdims=True))
           - jnp.exp(jnp.sum(lv[2:3] * lv[3:4], axis=1, keepdims=True)) + lam_init)
    lane = lax.broadcasted_iota(I32, (q_ref.shape[0], hd2), 1)
    for h in range(DIFF_HEADS):
        cols = slice(h * hd2, (h + 1) * hd2)
        q = q_ref[:, cols]
        zero = jnp.zeros_like(q)
        k_new = k_ref[:, cols].astype(BF16)
        values = [v_ref[:, cols].astype(BF16)]
        if has_cache:
            past = ck_ref.shape[1]
            k_past_t = ck_ref[cols, :].astype(BF16)
            values.append(cv_ref[pl.ds(h, past, stride=DIFF_HEADS), :].astype(BF16))
        o = None
        for c in range(2):
            qc = jnp.where((lane < DIFF_HD) == (c == 0), q, zero)
            s = [_dot_nt(qc, k_new)]
            if has_cache:
                s.append(_dot(qc, k_past_t))
            mx = functools.reduce(jnp.maximum, [jnp.max(si, axis=1, keepdims=True) for si in s])
            e = [jnp.exp(si - mx) for si in s]
            z = functools.reduce(jnp.add, [jnp.sum(ei, axis=1, keepdims=True) for ei in e])
            pv = functools.reduce(jnp.add, [_dot(ei.astype(BF16), v) for ei, v in zip(e, values)])
            pv = pv * (1.0 / z)
            o = pv if c == 0 else o - lam * pv
        o_ref[:, cols] = ((_rms(o) * sw_ref[...]) * (1.0 - lam_init)).astype(BF16)


def _diffattn(q, k, v, lam_vecs, subln_w, batch, seq_len, q_block, lam_init, cache=None):
    d = D_MODEL
    nq = seq_len // q_block
    has_cache = cache is not None
    kv_spec = pl.BlockSpec((seq_len, d), lambda b, qi: (b, 0))
    in_specs = [pl.BlockSpec((q_block, d), lambda b, qi: (b * nq + qi, 0)), kv_spec, kv_spec]
    args = [q, k, v]
    if has_cache:
        past = cache[0].shape[1]
        in_specs += [pl.BlockSpec((d, past), lambda b, qi: (b, 0)),
                     pl.BlockSpec((past * DIFF_HEADS, 2 * DIFF_HD), lambda b, qi: (b, 0))]
        args += list(cache)
    in_specs += [pl.BlockSpec((4, DIFF_HD), lambda b, qi: (0, 0)),
                 pl.BlockSpec((1, 2 * DIFF_HD), lambda b, qi: (0, 0))]
    args += [lam_vecs, subln_w]
    return pl.pallas_call(
        functools.partial(_diffattn_kernel, has_cache=has_cache, lam_init=lam_init),
        grid=(batch, nq),
        in_specs=in_specs,
        out_specs=pl.BlockSpec((q_block, d), lambda b, qi: (b * nq + qi, 0)),
        out_shape=jax.ShapeDtypeStruct((batch * seq_len, d), BF16),
        compiler_params=_params(2),
        name="diffattn_cache" if has_cache else "diffattn",
    )(*args)


def _router_weights(router_group, router_expert):
    w = jnp.concatenate([router_group, router_expert], axis=1)
    w = jnp.pad(w, ((0, 0), (0, LANES - w.shape[1])))
    hi = w.astype(BF16)
    return hi, (w - hi.astype(F32)).astype(BF16)


def _inproj0_weights(w_in):
    gq, gk, gv, gg, gaf, gab, hq, hff, hfb, hi, hg = jnp.split(
        w_in, [256, 512, 1024, 1536, 1552, 1568, 1824, 2080, 2336, 2848], axis=1)
    w = jnp.concatenate([gq, gk, gv, gg, hq, hff, hfb, hi, hg, gaf, gab], axis=1)
    return jnp.pad(w, ((0, 0), (0, AB_COLS - w.shape[1]))).astype(BF16)


def kernel(x_prompt, x_sample, state_gla, state_hgrn, cache_diff_k, cache_diff_v, c, c_ctx,
           w_ada, b_ada, norm1_w, norm2_w, w_in_ab, gla_a2, gla_a_bias, hgrn_lb, gla_onorm_w,
           hgrn_onorm_w, w_out_ab, w_in_c, lam_q1, lam_k1, lam_q2, lam_k2, diff_subln_w, w_out_c,
           router_group, router_expert, moe_w1, moe_w3, moe_w2, final_norm_w):
    bp, lp, d = x_prompt.shape
    bs, ls, _ = x_sample.shape
    depth = w_ada.shape[0]
    assert depth == 2 and d == D_MODEL and bs <= 7
    tp, ts = bp * lp, bs * ls
    npt, nst = tp // ROW_TILE, ts // ROW_TILE
    tps = ls // ROW_TILE
    xp = x_prompt.reshape(tp, d)
    xs = x_sample.reshape(ts, d)

    cond8 = jnp.concatenate([c_ctx[None, :], c, jnp.zeros((7 - bs, d), F32)], axis=0)
    mods = _adaln(cond8, w_ada, b_ada).reshape(depth * 8, 6, d)

    proj = _inproj0(xp, xs, mods, norm1_w[0:1], _inproj0_weights(w_in_ab[0]), tps)
    a_bias = gla_a_bias[0][:, None, :]
    scan_args = (gla_a2[0], a_bias, hgrn_lb, gla_onorm_w[0:1], hgrn_onorm_w[0:1])
    mixed_p, s_fin = _scan(proj, 0, bp, lp, *scan_args)
    s0 = jnp.concatenate([state_gla[:, 0], state_hgrn[:, 0]], axis=2).swapaxes(-1, -2)
    s0 = s0.reshape(bs, 2, SCAN_PAIRS, 2, HEAD_DV, HEAD_DK)
    zero = jnp.zeros_like(s0[:, :, :, 0])
    s0 = jnp.concatenate([jnp.concatenate([s0[:, :, :, 0], zero], axis=-1),
                          jnp.concatenate([zero, s0[:, :, :, 1]], axis=-1)], axis=-2)
    mixed_s = _scan(proj, tp, bs, ls, *scan_args, s0=s0)
    new_state_gla = s_fin[:, None, :, :GLA_HEADS]
    new_state_hgrn = s_fin[:, None, :, GLA_HEADS:]

    wr = _router_weights(router_group[0], router_expert[0])
    x1, *routed = _post((xp, xs), mixed_p, mixed_s, mods, 0, norm2_w[0:1],
                        w_out_ab[0].astype(BF16), *wr, tps)
    ys, tables = _moe(*routed, moe_w1, moe_w3, moe_w2, 0)
    x2 = _combine(*tables, x1, mods, 0, final_norm_w[None, :], ys, 0, npt + nst, npt, tps, False)

    lam_init = 0.8 - 0.6 * math.exp(-0.3 * 1)
    cos_t, sin_t = _rope_tables(ls)
    (qp, kp, vp), (qs, ks, vs), (k_cache, v_cache) = _inproj1(
        x2, mods, norm1_w[1:2], w_in_c[0].astype(BF16), npt, nst, tps, cos_t, sin_t)
    lam_vecs = jnp.stack([lam_q1[0], lam_k1[0], lam_q2[0], lam_k2[0]])
    att_p = _diffattn(qp, kp, vp, lam_vecs, diff_subln_w[0:1], bp, lp, lp, lam_init)
    past = cache_diff_k.shape[2]
    assert lp == ROW_TILE and DIFF_HEADS == TOKEN_ROWS
    cache = (cache_diff_k[:, 0].transpose(0, 2, 3, 4, 1).reshape(bs * d, past),
             cache_diff_v[:, 0].reshape(bs * past * DIFF_HEADS, 2 * DIFF_HD))
    att_s = _diffattn(qs, ks, vs, lam_vecs, diff_subln_w[0:1], bs, ls, ROW_TILE, lam_init, cache)

    wr = _router_weights(router_group[1], router_expert[1])
    x3, *routed = _post((x2,), att_p, att_s, mods, 1, norm2_w[1:2],
                        w_out_c[0].astype(BF16), *wr, tps)
    ys, tables = _moe(*routed, moe_w1, moe_w3, moe_w2, 1)
    fw = final_norm_w[None, :]
    y_p = _combine(*tables, x3, mods, 1, fw, ys, 0, npt, npt, tps, True)
    y_s = _combine(*tables, x3, mods, 1, fw, ys, npt, nst, npt, tps, True)

    return (y_p.reshape(bp, lp, d), y_s.reshape(bs, ls, d), new_state_gla, new_state_hgrn,
            k_cache.reshape(bp, 1, DIFF_HEADS, 2, DIFF_HD, lp).transpose(0, 1, 5, 2, 3, 4),
            v_cache.reshape(bp, 1, lp, DIFF_HEADS, 2 * DIFF_HD))
```

```python
import functools
import math

import jax
import jax.numpy as jnp
import numpy as np
from jax import lax
from jax.experimental import pallas as pl
from jax.experimental.pallas import tpu as pltpu

F32 = jnp.float32
BF16 = jnp.bfloat16
I32 = jnp.int32

D_MODEL = 1024
GLA_HEADS = 4
HGRN_HEADS = 4
SCAN_HEADS = GLA_HEADS + HGRN_HEADS
SCAN_PAIRS = SCAN_HEADS // 2
HEAD_DK = 64
HEAD_DV = 128
GATE_RANK = 16
GLA_GATE_NORM = 16.0
DIFF_HEADS = 8
DIFF_HD = 64
GRID_W = 64
ROPE_THETA = 10000.0
N_GROUPS = 4
EXPERTS_PER_GROUP = 8
N_EXPERTS = N_GROUPS * EXPERTS_PER_GROUP
MOE_HIDDEN = 512
EPS = 1e-6
LANES = 128
TOKEN_ROWS = D_MODEL // LANES
NEG_BIG = -1e30

ROW_TILE = 256
INPROJ0_TILE = 512
SCAN_CHUNK = 64
EXPERT_TILE = 256
MOVE_BLOCK = 16
STAGE_TOKENS = 2 * ROW_TILE + N_EXPERTS * MOVE_BLOCK
MAX_BLOCKS = STAGE_TOKENS // MOVE_BLOCK
VMEM_LIMIT = 56 * 1024 * 1024

_C_GQ, _C_GK, _C_GV, _C_GG = 0, 256, 512, 1024
_C_HQ, _C_HFF, _C_HFB, _C_HI, _C_HG = 1536, 1792, 2048, 2304, 2816
_C_GAF, _C_GAB = 3328, 3344
AB_COLS = 3456


def _params(n_axes, vmem=VMEM_LIMIT):
    return pltpu.CompilerParams(dimension_semantics=("arbitrary",) * n_axes,
                                vmem_limit_bytes=vmem)


def _cdiv(a, b):
    return (a + b - 1) // b


def _dot(a, b):
    return jnp.dot(a, b, preferred_element_type=F32)


def _dot_nt(a, b):
    return lax.dot_general(a, b, (((1,), (1,)), ((), ())), preferred_element_type=F32)


def _dot_tn(a, b):
    return lax.dot_general(a, b, (((0,), (0,)), ((), ())), preferred_element_type=F32)


def _split_bf16(x):
    hi = x.astype(BF16)
    lo = (x - hi.astype(F32)).astype(BF16)
    return hi, lo


def _silu(x):
    return x * jax.nn.sigmoid(x)


def _log_sigmoid(x):
    return jnp.minimum(x, 0.0) - jnp.log(1.0 + jnp.exp(-jnp.abs(x)))


def _rms(x):
    return x * lax.rsqrt(jnp.mean(x * x, axis=-1, keepdims=True) + EPS)


def _modulate(x, norm_w, shift, scale):
    return (_rms(x) * norm_w) * (1.0 + scale) + shift


def _to_token_major(dst_ref, x, row0=0):
    n = x.shape[0]
    for s in range(TOKEN_ROWS):
        dst_ref[pl.ds(row0 + s, n, stride=TOKEN_ROWS), :] = x[:, s * LANES:(s + 1) * LANES]


def _from_token_major(src_ref, n, row0=0):
    return jnp.concatenate([src_ref[pl.ds(row0 + s, n, stride=TOKEN_ROWS), :]
                            for s in range(TOKEN_ROWS)], axis=1)


def _ada_kernel(c_ref, w_ref, b_ref, o_ref):
    s = _silu(c_ref[...])
    o_ref[0] = _dot(s.astype(BF16), w_ref[0].astype(BF16)) + b_ref[0]


def _adaln(cond8, w_ada, b_ada):
    depth, d, n = w_ada.shape
    tn = 1536
    return pl.pallas_call(
        _ada_kernel,
        grid=(depth, n // tn),
        in_specs=[pl.BlockSpec((8, d), lambda l, j: (0, 0)),
                  pl.BlockSpec((1, d, tn), lambda l, j: (l, 0, j)),
                  pl.BlockSpec((1, 1, tn), lambda l, j: (l, 0, j))],
        out_specs=pl.BlockSpec((1, 8, tn), lambda l, j: (l, 0, j)),
        out_shape=jax.ShapeDtypeStruct((depth, 8, n), F32),
        compiler_params=_params(2),
        name="adaln",
    )(cond8, w_ada, b_ada.reshape(depth, 1, n))


def _mod_row(i, layer, n_prompt_tiles, tiles_per_sample):
    r = jnp.where(i < n_prompt_tiles, 0, 1 + (i - n_prompt_tiles) // tiles_per_sample)
    return layer * 8 + r


def _inproj0_kernel(xp_ref, xs_ref, mod_ref, nw_ref, w_ref, o_ref, *, n_prompt_tiles):
    i = pl.program_id(0)
    x = jnp.where(i < n_prompt_tiles, xp_ref[...], xs_ref[...])
    h = _modulate(x, nw_ref[...], mod_ref[0, 0:1, :], mod_ref[0, 1:2, :])
    o_ref[...] = _dot(h.astype(BF16), w_ref[...])


def _inproj0(xp, xs, mods, norm_w, w_bf16, sample_len):
    tp, d = xp.shape
    ts = xs.shape[0]
    n = w_bf16.shape[1]
    tile = INPROJ0_TILE
    npt, nst = tp // tile, ts // tile
    mod_map = lambda i: (_mod_row(i, 0, npt, sample_len // tile), 0, 0)
    return pl.pallas_call(
        functools.partial(_inproj0_kernel, n_prompt_tiles=npt),
        grid=(npt + nst,),
        in_specs=[pl.BlockSpec((tile, d), lambda i: (jnp.minimum(i, npt - 1), 0)),
                  pl.BlockSpec((tile, d), lambda i: (jnp.maximum(i - npt, 0), 0)),
                  pl.BlockSpec((1, 6, d), mod_map),
                  pl.BlockSpec((1, d), lambda i: (0, 0)),
                  pl.BlockSpec((d, n), lambda i: (0, 0))],
        out_specs=pl.BlockSpec((tile, n), lambda i: (i, 0)),
        out_shape=jax.ShapeDtypeStruct((tp + ts, n), F32),
        compiler_params=_params(1),
        name="inproj0",
    )(xp, xs, mods, norm_w, w_bf16)


def _scan_kernel(*refs, seq_len, has_state):
    if has_state:
        (p_ref, a2_ref, ab_ref, lb_ref, ong_ref, onh_ref, s0_ref,
         mixed_ref, qf, kf, qb, kb, vv, dec_f, dec_b, o_f, o_b, st_f, st_b) = refs
        sfin_ref = None
    else:
        (p_ref, a2_ref, ab_ref, lb_ref, ong_ref, onh_ref,
         mixed_ref, sfin_ref, qf, kf, qb, kb, vv, dec_f, dec_b, o_f, o_b, st_f, st_b) = refs
        s0_ref = None
    C = SCAN_CHUNK
    n_chunks = seq_len // C
    gqk = GLA_HEADS * HEAD_DK

    row = lax.broadcasted_iota(I32, (C, C), 0)
    col = lax.broadcasted_iota(I32, (C, C), 1)
    lower = col <= row
    upper = col >= row
    tri_lo = jnp.where(lower, 1.0, 0.0).astype(BF16)
    tri_up = jnp.where(upper, 1.0, 0.0).astype(BF16)

    lbp = lb_ref[...]
    lb_max = jnp.maximum(lbp[0], lbp[1])
    lb_e0 = jnp.exp(lbp[0] - lb_max)
    lb_e1 = jnp.exp(lbp[1] - lb_max)
    lb = lb_e0 / (lb_e0 + lb_e1)

    def cumsum_chunk(tri, la):
        hi, lo = _split_bf16(la)
        return _dot(tri, hi) + _dot(tri, lo)

    def prep(n, carry):
        r0 = pl.multiple_of(n * C, C)
        rows = pl.ds(r0, C)
        gq = p_ref[rows, _C_GQ:_C_GQ + gqk] * (HEAD_DK ** -0.5)
        gk = p_ref[rows, _C_GK:_C_GK + gqk]
        hq = _silu(p_ref[rows, _C_HQ:_C_HQ + gqk]) * (HEAD_DK ** -0.5)
        for d_i, (q_s, k_s, dec_s, tri, last) in enumerate(
                ((qf, kf, dec_f, tri_lo, C - 1), (qb, kb, dec_b, tri_up, 0))):
            c_ga = _C_GAF if d_i == 0 else _C_GAB
            c_hf = _C_HFF if d_i == 0 else _C_HFB
            ga = p_ref[rows, c_ga:c_ga + GATE_RANK]
            xg = _dot(ga.astype(BF16), a2_ref[d_i].astype(BF16)) + ab_ref[d_i]
            la_g = _log_sigmoid(xg) / GLA_GATE_NORM
            f = lb[d_i:d_i + 1, :] + (1.0 - lb[d_i:d_i + 1, :]) * jax.nn.sigmoid(
                p_ref[rows, c_hf:c_hf + gqk])
            la_h = jnp.log(f)
            for q, k, la, c0 in ((gq, gk, la_g, 0), (hq, 1.0 - f, la_h, gqk)):
                b = cumsum_chunk(tri, la)
                q_s[rows, c0:c0 + gqk] = (q * jnp.exp(b)).astype(BF16)
                k_s[rows, c0:c0 + gqk] = (k * jnp.exp(-b)).astype(BF16)
                dec_s[n, :, c0:c0 + gqk] = jnp.exp(b[last:last + 1, :])
        vv[rows, 0:512] = p_ref[rows, _C_GV:_C_GV + 512].astype(BF16)
        vv[rows, 512:1024] = p_ref[rows, _C_HI:_C_HI + 512].astype(BF16)
        return carry

    lax.fori_loop(0, n_chunks, prep, 0, unroll=2)

    for p in range(SCAN_PAIRS):
        if has_state:
            st_f[p] = s0_ref[0, 0, p]
            st_b[p] = s0_ref[0, 1, p]
        else:
            st_f[p] = jnp.zeros((2 * HEAD_DV, 2 * HEAD_DK), F32)
            st_b[p] = jnp.zeros((2 * HEAD_DV, 2 * HEAD_DK), F32)

    first_head = lax.broadcasted_iota(I32, (C, 2 * HEAD_DK), 1) < HEAD_DK
    row2 = lax.broadcasted_iota(I32, (2 * C, C), 0) % C
    col2 = lax.broadcasted_iota(I32, (2 * C, C), 1)
    lower2 = col2 <= row2
    upper2 = col2 >= row2

    def per_head_rows(x):
        z = jnp.zeros_like(x)
        return jnp.concatenate([jnp.where(first_head, x, z), jnp.where(first_head, z, x)], axis=0)

    def put_out(o_ref, rows, p, res):
        c0 = p * 2 * HEAD_DV
        o_ref[rows, c0:c0 + HEAD_DV] = res[0:C, 0:HEAD_DV]
        o_ref[rows, c0 + HEAD_DV:c0 + 2 * HEAD_DV] = res[C:2 * C, HEAD_DV:2 * HEAD_DV]

    def sweep(n, carry):
        m = n_chunks - 1 - n
        rows = pl.ds(pl.multiple_of(n * C, C), C)
        rows_m = pl.ds(pl.multiple_of(m * C, C), C)
        decay_f, decay_b = dec_f[n], dec_b[m]
        for p in range(SCAN_PAIRS):
            ks = slice(p * 2 * HEAD_DK, (p + 1) * 2 * HEAD_DK)
            vs = slice(p * 2 * HEAD_DV, (p + 1) * 2 * HEAD_DV)
            qd, kd, vh = per_head_rows(qf[rows, ks]), kf[rows, ks], vv[rows, vs]
            s_f = st_f[p]
            sc = (jnp.where(lower2, _dot_nt(qd, kd), 0.0)
                  + jnp.where(upper2, _dot_nt(per_head_rows(qb[rows, ks]), kb[rows, ks]), 0.0))
            put_out(o_f, rows, p, _dot_nt(qd, s_f.astype(BF16)) + _dot(sc.astype(BF16), vh))
            st_f[p] = decay_f[:, ks] * (s_f + _dot_tn(vh, kd))
            s_b = st_b[p]
            vm, kbm = vv[rows_m, vs], kb[rows_m, ks]
            put_out(o_b, rows_m, p, _dot_nt(per_head_rows(qb[rows_m, ks]), s_b.astype(BF16)))
            st_b[p] = decay_b[:, ks] * (s_b + _dot_tn(vm, kbm))
        return carry

    lax.fori_loop(0, n_chunks, sweep, 0)

    def finish(n, carry):
        rows = pl.ds(pl.multiple_of(n * C, C), C)
        for h in range(SCAN_HEADS):
            vs = slice(h * HEAD_DV, (h + 1) * HEAD_DV)
            if h < GLA_HEADS:
                gate = p_ref[rows, _C_GG + h * HEAD_DV:_C_GG + (h + 1) * HEAD_DV]
                onw = ong_ref[...]
            else:
                hh = h - GLA_HEADS
                gate = p_ref[rows, _C_HG + hh * HEAD_DV:_C_HG + (hh + 1) * HEAD_DV]
                onw = onh_ref[...]
            o = o_f[rows, vs] + o_b[rows, vs]
            mixed_ref[rows, vs] = ((_rms(o) * onw) * _silu(gate)).astype(BF16)
        return carry

    lax.fori_loop(0, n_chunks, finish, 0, unroll=2)

    if sfin_ref is not None:
        for d_i, st in enumerate((st_f, st_b)):
            for p in range(SCAN_PAIRS):
                s_pair = st[p].T
                sfin_ref[0, d_i, 2 * p] = s_pair[0:HEAD_DK, 0:HEAD_DV]
                sfin_ref[0, d_i, 2 * p + 1] = s_pair[HEAD_DK:2 * HEAD_DK, HEAD_DV:2 * HEAD_DV]


def _scan(p, row0, batch, seq_len, a2, a_bias, lb, onorm_g, onorm_h, s0=None):
    n = p.shape[1]
    assert row0 % seq_len == 0
    blk0 = row0 // seq_len
    has_state = s0 is not None
    n_chunks = seq_len // SCAN_CHUNK
    st_shape = (1, 2, SCAN_HEADS, HEAD_DK, HEAD_DV)
    pair_shape = (SCAN_PAIRS, 2 * HEAD_DV, 2 * HEAD_DK)
    in_specs = [pl.BlockSpec((seq_len, n), lambda b: (blk0 + b, 0)),
                pl.BlockSpec(a2.shape, lambda b: (0, 0, 0)),
                pl.BlockSpec(a_bias.shape, lambda b: (0, 0, 0)),
                pl.BlockSpec(lb.shape, lambda b: (0, 0, 0)),
                pl.BlockSpec((1, HEAD_DV), lambda b: (0, 0)),
                pl.BlockSpec((1, HEAD_DV), lambda b: (0, 0))]
    args = [p, a2, a_bias, lb, onorm_g, onorm_h]
    mixed_shape = jax.ShapeDtypeStruct((batch * seq_len, D_MODEL), BF16)
    mixed_spec = pl.BlockSpec((seq_len, D_MODEL), lambda b: (b, 0))
    if has_state:
        in_specs.append(pl.BlockSpec((1, 2) + pair_shape, lambda b: (b, 0, 0, 0, 0)))
        args.append(s0)
        out_shape, out_specs = mixed_shape, mixed_spec
    else:
        out_shape = (mixed_shape, jax.ShapeDtypeStruct((batch,) + st_shape[1:], F32))
        out_specs = (mixed_spec, pl.BlockSpec(st_shape, lambda b: (b, 0, 0, 0, 0)))
    scratch = [pltpu.VMEM((seq_len, 512), BF16) for _ in range(4)]
    scratch += [pltpu.VMEM((seq_len, D_MODEL), BF16),
                pltpu.VMEM((n_chunks, 1, 512), F32), pltpu.VMEM((n_chunks, 1, 512), F32),
                pltpu.VMEM((seq_len, D_MODEL), F32), pltpu.VMEM((seq_len, D_MODEL), F32),
                pltpu.VMEM(pair_shape, F32), pltpu.VMEM(pair_shape, F32)]
    return pl.pallas_call(
        functools.partial(_scan_kernel, seq_len=seq_len, has_state=has_state),
        grid=(batch,),
        in_specs=in_specs, out_specs=out_specs, out_shape=out_shape,
        scratch_shapes=scratch,
        compiler_params=_params(1),
        name="scan_state" if has_state else "scan_fresh",
    )(*args)


def _post_kernel(*refs, split_x, n_prompt_tiles):
    if split_x:
        xp_ref, xs_ref = refs[0], refs[1]
        refs = refs[2:]
    else:
        x_ref = refs[0]
        refs = refs[1:]
    (mp_ref, ms_ref, mod_ref, nw_ref, wo_ref, wrh_ref, wrl_ref,
     x1_ref, h2_ref, slot_ref, wgt_ref, tab_ref, carry) = refs
    i = pl.program_id(0)
    is_prompt = i < n_prompt_tiles
    if split_x:
        x = jnp.where(is_prompt, xp_ref[...], xs_ref[...])
    else:
        x = x_ref[...]
    mixed = jnp.where(is_prompt, mp_ref[...], ms_ref[...])
    x1 = x + mod_ref[0, 2:3, :] * _dot(mixed, wo_ref[...])
    x1_ref[...] = x1
    h2 = _modulate(x1, nw_ref[...], mod_ref[0, 3:4, :], mod_ref[0, 4:5, :])
    _to_token_major(h2_ref, h2)

    hh, hl = _split_bf16(h2)
    logits = _dot(hh, wrh_ref[...]) + _dot(hl, wrh_ref[...]) + _dot(hh, wrl_ref[...])
    tm = logits.shape[0]
    lane = lax.broadcasted_iota(I32, (tm, LANES), 1).astype(F32)

    def first_max(v):
        mx = jnp.max(v, axis=1, keepdims=True)
        idx = jnp.min(jnp.where(v == mx, lane, float(LANES)), axis=1, keepdims=True)
        return mx, idx

    gl = jnp.where(lane < N_GROUPS, logits, NEG_BIG)
    gmax, gidx = first_max(gl)
    g_val = 1.0 / jnp.sum(jnp.exp(gl - gmax), axis=1, keepdims=True)
    lo = N_GROUPS + EXPERTS_PER_GROUP * gidx
    el = jnp.where((lane >= lo) & (lane < lo + EXPERTS_PER_GROUP), logits, NEG_BIG)
    emax, l1 = first_max(el)
    esum = jnp.sum(jnp.exp(el - emax), axis=1, keepdims=True)
    e2max, l2 = first_max(jnp.where(lane == l1, NEG_BIG, el))
    p1 = 1.0 / esum
    p2 = jnp.exp(e2max - emax) / esum
    w1 = g_val * (p1 / (p1 + p2))
    w2 = g_val * (p2 / (p1 + p2))
    id1 = l1 - N_GROUPS
    id2 = l2 - N_GROUPS

    @pl.when(i == 0)
    def _():
        carry[...] = jnp.zeros_like(carry)

    sel1 = lane == id1
    sel2 = lane == id2
    onehot = jnp.where(sel1 | sel2, 1.0, 0.0)
    row = lax.broadcasted_iota(I32, (tm, tm), 0)
    col = lax.broadcasted_iota(I32, (tm, tm), 1)
    earlier = jnp.where(col < row, 1.0, 0.0).astype(BF16)
    before = _dot(earlier, onehot.astype(BF16))
    count = jnp.sum(onehot, axis=0, keepdims=True)
    blocks = jnp.floor((count + (MOVE_BLOCK - 1.0)) * (1.0 / MOVE_BLOCK)) * MOVE_BLOCK
    e_row = lax.broadcasted_iota(I32, (LANES, LANES), 0)
    e_col = lax.broadcasted_iota(I32, (LANES, LANES), 1)
    lower_experts = jnp.where(e_row < e_col, 1.0, 0.0).astype(BF16)
    run_start = _dot(jnp.broadcast_to(blocks, (8, LANES)).astype(BF16), lower_experts)[0:1]
    slot = before + run_start
    q1 = jnp.sum(jnp.where(sel1, slot, 0.0), axis=1, keepdims=True)
    q2 = jnp.sum(jnp.where(sel2, slot, 0.0), axis=1, keepdims=True)
    tab_row = lax.broadcasted_iota(I32, (8, LANES), 0)
    tab_ref[0] = jnp.where(tab_row == 0, count,
                           jnp.where(tab_row == 1, carry[...],
                                     jnp.where(tab_row == 2, run_start, 0.0)))
    carry[...] = carry[...] + count

    cols = jnp.zeros((tm, LANES), F32)
    for k, v in enumerate((q1 * TOKEN_ROWS, q2 * TOKEN_ROWS, w1, w2)):
        cols = jnp.where(lane == k, v, cols)
    rows = cols.T
    slot_ref[0] = rows[0:2].astype(I32)
    wgt_ref[0] = rows[2:4]


def _post(x_args, mixed_p, mixed_s, mods, layer, norm_w, w_out_bf16, wr_hi, wr_lo,
          tiles_per_sample):
    split_x = len(x_args) == 2
    tp, ts = mixed_p.shape[0], mixed_s.shape[0]
    t, d = tp + ts, D_MODEL
    npt, nst = tp // ROW_TILE, ts // ROW_TILE
    tile = lambda i: (i, 0)
    if split_x:
        x_specs = [pl.BlockSpec((ROW_TILE, d), lambda i: (jnp.minimum(i, npt - 1), 0)),
                   pl.BlockSpec((ROW_TILE, d), lambda i: (jnp.maximum(i - npt, 0), 0))]
    else:
        x_specs = [pl.BlockSpec((ROW_TILE, d), tile)]
    in_specs = x_specs + [
        pl.BlockSpec((ROW_TILE, d), lambda i: (jnp.minimum(i, npt - 1), 0)),
        pl.BlockSpec((ROW_TILE, d), lambda i: (jnp.maximum(i - npt, 0), 0)),
        pl.BlockSpec((1, 6, d), lambda i: (_mod_row(i, layer, npt, tiles_per_sample), 0, 0)),
        pl.BlockSpec((1, d), lambda i: (0, 0)),
        pl.BlockSpec((d, d), lambda i: (0, 0)),
        pl.BlockSpec((d, LANES), lambda i: (0, 0)),
        pl.BlockSpec((d, LANES), lambda i: (0, 0))]
    return pl.pallas_call(
        functools.partial(_post_kernel, split_x=split_x, n_prompt_tiles=npt),
        grid=(npt + nst,),
        in_specs=in_specs,
        out_specs=(pl.BlockSpec((ROW_TILE, d), tile),
                   pl.BlockSpec((ROW_TILE * TOKEN_ROWS, LANES), tile),
                   pl.BlockSpec((1, 2, ROW_TILE), lambda i: (i, 0, 0)),
                   pl.BlockSpec((1, 2, ROW_TILE), lambda i: (i, 0, 0)),
                   pl.BlockSpec((1, 8, LANES), lambda i: (i, 0, 0))),
        out_shape=(jax.ShapeDtypeStruct((t, d), F32),
                   jax.ShapeDtypeStruct((t * TOKEN_ROWS, LANES), F32),
                   jax.ShapeDtypeStruct((npt + nst, 2, ROW_TILE), I32),
                   jax.ShapeDtypeStruct((npt + nst, 2, ROW_TILE), F32),
                   jax.ShapeDtypeStruct((npt + nst, 8, LANES), F32)),
        scratch_shapes=[pltpu.VMEM((1, LANES), F32)],
        compiler_params=_params(1),
        name=f"post{layer}",
    )(*x_args, mixed_p, mixed_s, mods, norm_w, w_out_bf16, wr_hi, wr_lo)


def _for_blocks(tab_ref, fn):
    block_rows = MOVE_BLOCK * TOKEN_ROWS
    count = tab_ref[0, 0, MAX_BLOCKS]

    def call(k, parity):
        fn(pl.multiple_of(k * block_rows, block_rows),
           pl.multiple_of(tab_ref[0, 0, k], TOKEN_ROWS), parity)

    def body(k2, c):
        call(2 * k2, 0)

        @pl.when(2 * k2 + 1 < count)
        def _():
            call(2 * k2 + 1, 1)
        return c

    lax.fori_loop(0, _cdiv(count, 2), body, 0)


def _wait_blocks(tab_ref, copy):
    def body(k, c):
        copy.wait()
        return c

    lax.fori_loop(0, tab_ref[0, 0, MAX_BLOCKS], body, 0)


def _dispatch_kernel(zero_ref, tab_ref, prev_tab_ref, q_ref, h2_ref, hs_ref, zero_buf, stage, sem):
    j = pl.program_id(0)
    slot = j % 2
    block_rows = MOVE_BLOCK * TOKEN_ROWS

    @pl.when(j == 0)
    def _():
        zero_buf[...] = jnp.zeros_like(zero_buf)

        def zero_copy(k):
            start = pl.multiple_of(zero_ref[k], EXPERT_TILE * TOKEN_ROWS)
            return pltpu.make_async_copy(
                zero_buf, hs_ref.at[pl.ds(start, EXPERT_TILE * TOKEN_ROWS)], sem.at[0])

        def start_zero(k, c):
            @pl.when(zero_ref[k] >= 0)
            def _():
                zero_copy(k).start()
            return c

        def wait_zero(k, c):
            @pl.when(zero_ref[k] >= 0)
            def _():
                zero_copy(k).wait()
            return c

        lax.fori_loop(0, zero_ref.shape[0], start_zero, 0)
        lax.fori_loop(0, zero_ref.shape[0], wait_zero, 0)

        stage[...] = jnp.zeros_like(stage)

    def place(r, c):
        tok = h2_ref[pl.ds(pl.multiple_of(r * TOKEN_ROWS, TOKEN_ROWS), TOKEN_ROWS), :]
        for s in range(2):
            row = pl.multiple_of(q_ref[0, s, r], TOKEN_ROWS)
            stage[slot, pl.ds(row, TOKEN_ROWS), :] = tok
        return c

    lax.fori_loop(0, ROW_TILE, place, 0, unroll=8)

    def block_copy(buf, stage_row, sorted_row):
        return pltpu.make_async_copy(stage.at[buf, pl.ds(stage_row, block_rows)],
                                     hs_ref.at[pl.ds(sorted_row, block_rows)], sem.at[buf])

    @pl.when(j > 0)
    def _():
        _wait_blocks(prev_tab_ref, block_copy(1 - slot, 0, 0))

    _for_blocks(tab_ref, lambda a, b, parity: block_copy(slot, a, b).start(priority=parity))

    @pl.when(j == pl.num_programs(0) - 1)
    def _():
        _wait_blocks(tab_ref, block_copy(slot, 0, 0))


def _dispatch(zero_tiles, block_tab, slots, h2, n_rows):
    t = h2.shape[0] // TOKEN_ROWS
    nt = t // ROW_TILE
    smem_tile = lambda shape: pl.BlockSpec((1,) + shape, lambda j, *_: (j, 0, 0),
                                           memory_space=pltpu.SMEM)
    grid_spec = pltpu.PrefetchScalarGridSpec(
        num_scalar_prefetch=1,
        grid=(nt,),
        in_specs=[smem_tile((1, LANES)),
                  pl.BlockSpec((1, 1, LANES), lambda j, *_: (jnp.maximum(j - 1, 0), 0, 0),
                               memory_space=pltpu.SMEM),
                  smem_tile((2, ROW_TILE)),
                  pl.BlockSpec((ROW_TILE * TOKEN_ROWS, LANES), lambda j, *_: (j, 0))],
        out_specs=pl.BlockSpec(memory_space=pl.ANY),
        scratch_shapes=[pltpu.VMEM((EXPERT_TILE * TOKEN_ROWS, LANES), F32),
                        pltpu.VMEM((2, STAGE_TOKENS * TOKEN_ROWS, LANES), F32),
                        pltpu.SemaphoreType.DMA((2,))])
    return pl.pallas_call(
        _dispatch_kernel,
        grid_spec=grid_spec,
        out_shape=jax.ShapeDtypeStruct((n_rows * TOKEN_ROWS, LANES), F32),
        compiler_params=_params(1),
        name="dispatch",
    )(zero_tiles, block_tab, block_tab, slots, h2)


def _expert_kernel(te_ref, src_ref, nv_ref, run_ref, nxt_ref, hs_ref, w1_hbm, w3_hbm, w2_hbm,
                   ys_ref, w1f, w3f, w2f, w1b, w3b, w2b, sem, *, layer):
    i = pl.program_id(0)

    def weight_copies(e, buf):
        return [pltpu.make_async_copy(src.at[layer, e], dst.at[buf], sem.at[buf])
                for src, dst in ((w1_hbm, w1f), (w3_hbm, w3f), (w2_hbm, w2f))]

    @pl.when(i == 0)
    def _():
        for c in weight_copies(te_ref[0], 0):
            c.start()

    first = (i == 0) | (run_ref[i] != run_ref[jnp.maximum(i - 1, 0)])

    @pl.when(first)
    def _():
        buf = run_ref[i] % 2
        for c in weight_copies(te_ref[i], buf):
            c.wait()

        @pl.when(nxt_ref[i] >= 0)
        def _():
            for c in weight_copies(nxt_ref[i], 1 - buf):
                c.start()

        w1b[...] = w1f[buf].astype(BF16)
        w3b[...] = w3f[buf].astype(BF16)
        w2b[...] = w2f[buf].astype(BF16)

    @pl.when(nv_ref[i] > 0)
    def _():
        h = _from_token_major(hs_ref, EXPERT_TILE).astype(BF16)
        g = _silu(_dot(h, w1b[...])) * _dot(h, w3b[...])
        _to_token_major(ys_ref, _dot(g.astype(BF16), w2b[...]))

    @pl.when(nv_ref[i] == 0)
    def _():
        ys_ref[...] = jnp.zeros_like(ys_ref)


def _experts(tile_expert, tile_src, tile_rows, hs, w1, w3, w2, layer):
    n_rows, d = hs.shape[0] // TOKEN_ROWS, D_MODEL
    nt = n_rows // EXPERT_TILE
    hid = w1.shape[-1]
    tok_tile = (EXPERT_TILE * TOKEN_ROWS, LANES)
    changed = jnp.concatenate([jnp.zeros((1,), I32),
                               (tile_expert[1:] != tile_expert[:-1]).astype(I32)])
    run = jnp.cumsum(changed).astype(I32)
    later = jnp.where(run[None, :] > run[:, None], tile_expert[None, :], N_EXPERTS)
    next_expert = jnp.min(later, axis=1)
    next_expert = jnp.where(next_expert < N_EXPERTS, next_expert, -1).astype(I32)
    grid_spec = pltpu.PrefetchScalarGridSpec(
        num_scalar_prefetch=5,
        grid=(nt,),
        in_specs=[pl.BlockSpec(tok_tile, lambda i, te, src, nv, run, nxt: (src[i], 0)),
                  pl.BlockSpec(memory_space=pl.ANY), pl.BlockSpec(memory_space=pl.ANY),
                  pl.BlockSpec(memory_space=pl.ANY)],
        out_specs=pl.BlockSpec(tok_tile, lambda i, te, src, nv, run, nxt: (i, 0)),
        scratch_shapes=[pltpu.VMEM((2, d, hid), F32), pltpu.VMEM((2, d, hid), F32),
                        pltpu.VMEM((2, hid, d), F32),
                        pltpu.VMEM((d, hid), BF16), pltpu.VMEM((d, hid), BF16),
                        pltpu.VMEM((hid, d), BF16), pltpu.SemaphoreType.DMA((2,))])
    return pl.pallas_call(
        functools.partial(_expert_kernel, layer=layer),
        grid_spec=grid_spec,
        out_shape=jax.ShapeDtypeStruct(hs.shape, F32),
        compiler_params=_params(1),
        name=f"experts{layer}",
    )(tile_expert, tile_src, tile_rows, run, next_expert, hs, w1, w3, w2)


def _combine_kernel(tab_ref, next_tab_ref, q_ref, w_ref, x1_ref, mod_ref, fw_ref, ys_ref, out_ref,
                    stage, y_tok, sem, *, final_norm):
    i = pl.program_id(0)
    slot = i % 2
    block_rows = MOVE_BLOCK * TOKEN_ROWS

    def block_copy(buf, stage_row, sorted_row):
        return pltpu.make_async_copy(ys_ref.at[pl.ds(sorted_row, block_rows)],
                                     stage.at[buf, pl.ds(stage_row, block_rows)], sem.at[buf])

    def fetch(tab, buf):
        _for_blocks(tab, lambda a, b, parity: block_copy(buf, a, b).start(priority=parity))

    @pl.when(i == 0)
    def _():
        fetch(tab_ref, slot)

    @pl.when(i + 1 < pl.num_programs(0))
    def _():
        fetch(next_tab_ref, 1 - slot)

    _wait_blocks(tab_ref, block_copy(slot, 0, 0))

    def pick(r, c):
        rows = [stage[slot, pl.ds(pl.multiple_of(q_ref[0, s, r], TOKEN_ROWS), TOKEN_ROWS), :]
                for s in range(2)]
        y_tok[pl.ds(pl.multiple_of(r * TOKEN_ROWS, TOKEN_ROWS), TOKEN_ROWS), :] = (
            w_ref[0, 0, r] * rows[0] + w_ref[0, 1, r] * rows[1])
        return c

    lax.fori_loop(0, ROW_TILE, pick, 0, unroll=8)
    x2 = x1_ref[...] + mod_ref[0, 5:6, :] * _from_token_major(y_tok, ROW_TILE)
    if final_norm:
        x2 = _rms(x2) * fw_ref[...]
    out_ref[...] = x2


def _combine(block_tab, slots, weights, x1, mods, layer, final_w, ys, tile0, n_tiles,
             n_prompt_tiles, tiles_per_sample, final_norm):
    d = D_MODEL
    tile = lambda i: (tile0 + i, 0)
    mod_map = lambda i: (_mod_row(tile0 + i, layer, n_prompt_tiles, tiles_per_sample), 0, 0)
    smem_tile = lambda shape: pl.BlockSpec((1,) + shape, lambda i: (tile0 + i, 0, 0),
                                           memory_space=pltpu.SMEM)
    return pl.pallas_call(
        functools.partial(_combine_kernel, final_norm=final_norm),
        grid=(n_tiles,),
        in_specs=[smem_tile((1, LANES)),
                  pl.BlockSpec((1, 1, LANES),
                               lambda i: (tile0 + jnp.minimum(i + 1, n_tiles - 1), 0, 0),
                               memory_space=pltpu.SMEM),
                  smem_tile((2, ROW_TILE)), smem_tile((2, ROW_TILE)),
                  pl.BlockSpec((ROW_TILE, d), tile),
                  pl.BlockSpec((1, 6, d), mod_map),
                  pl.BlockSpec((1, d), lambda i: (0, 0)),
                  pl.BlockSpec(memory_space=pl.ANY)],
        out_specs=pl.BlockSpec((ROW_TILE, d), lambda i: (i, 0)),
        out_shape=jax.ShapeDtypeStruct((n_tiles * ROW_TILE, d), F32),
        scratch_shapes=[pltpu.VMEM((2, STAGE_TOKENS * TOKEN_ROWS, LANES), F32),
                        pltpu.VMEM((ROW_TILE * TOKEN_ROWS, LANES), F32),
                        pltpu.SemaphoreType.DMA((2,))],
        compiler_params=_params(1),
        name=f"combine{layer}_{tile0}",
    )(block_tab, block_tab, slots, weights, x1, mods, final_w, ys)


def _moe(h2, slots, weights, tile_tab, w1, w3, w2, layer):
    t = h2.shape[0] // TOKEN_ROWS
    n_tiles = t // ROW_TILE
    extra_tiles = N_EXPERTS + _cdiv(N_EXPERTS * MOVE_BLOCK, EXPERT_TILE)
    n_rows = 2 * t + extra_tiles * EXPERT_TILE
    nt = n_rows // EXPERT_TILE
    tab = tile_tab[:, :, :N_EXPERTS].astype(I32)
    cnt = tab[-1, 0] + tab[-1, 1]
    tight = _cdiv(cnt, EXPERT_TILE) * EXPERT_TILE
    padded = jnp.where(cnt > 0, _cdiv(cnt + MOVE_BLOCK - 1, EXPERT_TILE) * EXPERT_TILE, 0)
    ends = jnp.cumsum(padded)
    offsets = ends - padded
    tails = jnp.where(cnt > 0, ends - EXPERT_TILE, -1)
    tails2 = jnp.where(padded > tight, ends - 2 * EXPERT_TILE, -1)
    used = ends[-1] // EXPERT_TILE
    tile_start = jnp.arange(nt, dtype=I32) * EXPERT_TILE
    unused = (used + jnp.arange(extra_tiles, dtype=I32)) * EXPERT_TILE
    zero_tiles = jnp.concatenate([tails, tails2, jnp.where(unused < n_rows, unused, -1)])
    zero_tiles = jnp.where(zero_tiles >= 0, zero_tiles * TOKEN_ROWS, -1).astype(I32)
    tile_src = jnp.minimum(jnp.arange(nt, dtype=I32), used - 1)
    tile_expert = jnp.sum((tile_src * EXPERT_TILE)[:, None] >= ends[None, :], axis=1).astype(I32)
    tile_rows = jnp.where(tile_start < ends[-1],
                          jnp.clip(cnt[tile_expert] - (tile_start - offsets[tile_expert]),
                                   0, EXPERT_TILE), 0).astype(I32)
    n_blocks = _cdiv(tab[:, 0], MOVE_BLOCK)
    blocks_through = jnp.cumsum(n_blocks, axis=1)
    k = jnp.arange(MAX_BLOCKS, dtype=I32)
    owner = jnp.sum(blocks_through[:, None, :] <= k[None, :, None], axis=2)
    is_owner = owner[:, :, None] == jnp.arange(N_EXPERTS, dtype=I32)[None, None, :]
    pick = lambda v: jnp.sum(jnp.where(is_owner, v[:, None, :], 0), axis=2)
    run_first = pick(offsets[None, :] + tab[:, 1])
    block_in_run = k[None, :] - pick(blocks_through - n_blocks)
    sorted_row = (run_first + block_in_run * MOVE_BLOCK) * TOKEN_ROWS
    block_tab = jnp.concatenate(
        [sorted_row, blocks_through[:, -1:],
         jnp.zeros((n_tiles, LANES - MAX_BLOCKS - 1), I32)], axis=1).astype(I32)[:, None, :]
    hs = _dispatch(zero_tiles, block_tab, slots, h2, n_rows)
    ys = _experts(tile_expert, tile_src, tile_rows, hs, w1, w3, w2, layer)
    return ys, (block_tab, slots, weights)


def _rope(x, cos, sin_signed):
    lane = lax.broadcasted_iota(I32, (x.shape[0], LANES), 1)
    low = (lane % 32) < 16
    outs = []
    for j in range(x.shape[1] // LANES):
        xb = x[:, j * LANES:(j + 1) * LANES]
        partner = jnp.where(low, pltpu.roll(xb, LANES - 16, 1), pltpu.roll(xb, 16, 1))
        outs.append(xb * cos + partner * sin_signed)
    return jnp.concatenate(outs, axis=1)


def _inproj1_prompt_kernel(x_ref, mod_ref, nw_ref, w_ref, q_ref, k_ref, v_ref, kc_ref, vc_ref):
    d = D_MODEL
    h = _modulate(x_ref[...], nw_ref[...], mod_ref[0, 0:1, :], mod_ref[0, 1:2, :]).astype(BF16)
    q_ref[...] = (_dot(h, w_ref[:, 0:d]) * (DIFF_HD ** -0.5)).astype(BF16)
    k = _dot(h, w_ref[:, d:2 * d])
    v = _dot(h, w_ref[:, 2 * d:3 * d])
    k_ref[...] = k.astype(BF16)
    v_ref[...] = v.astype(BF16)
    kc_ref[...] = k.T
    _to_token_major(vc_ref, v)


def _inproj1_sample_kernel(x_ref, mod_ref, nw_ref, w_ref, cos_ref, sin_ref, q_ref, k_ref, v_ref):
    d = D_MODEL
    h = _modulate(x_ref[...], nw_ref[...], mod_ref[0, 0:1, :], mod_ref[0, 1:2, :]).astype(BF16)
    cos, sin = cos_ref[...], sin_ref[...]
    q_ref[...] = (_rope(_dot(h, w_ref[:, 0:d]), cos, sin) * (DIFF_HD ** -0.5)).astype(BF16)
    k_ref[...] = _rope(_dot(h, w_ref[:, d:2 * d]), cos, sin).astype(BF16)
    v_ref[...] = _dot(h, w_ref[:, 2 * d:3 * d]).astype(BF16)


def _inproj1(x, mods, norm_w, w_bf16, n_prompt_tiles, n_sample_tiles, tiles_per_sample,
             cos_t, sin_t):
    d = D_MODEL
    npt, nst = n_prompt_tiles, n_sample_tiles
    common = [pl.BlockSpec((1, d), lambda i: (0, 0)), pl.BlockSpec((d, 3 * d), lambda i: (0, 0))]
    tile = lambda i: (i, 0)
    out_specs = tuple(pl.BlockSpec((ROW_TILE, d), tile) for _ in range(3))
    qp, kp, vp, k_cache, v_cache = pl.pallas_call(
        _inproj1_prompt_kernel,
        grid=(npt,),
        in_specs=[pl.BlockSpec((ROW_TILE, d), tile),
                  pl.BlockSpec((1, 6, d), lambda i: (8, 0, 0))] + common,
        out_specs=out_specs + (pl.BlockSpec((d, ROW_TILE), tile),
                               pl.BlockSpec((ROW_TILE * TOKEN_ROWS, LANES), tile)),
        out_shape=tuple(jax.ShapeDtypeStruct((npt * ROW_TILE, d), BF16) for _ in range(3))
        + (jax.ShapeDtypeStruct((npt * d, ROW_TILE), F32),
           jax.ShapeDtypeStruct((npt * ROW_TILE * TOKEN_ROWS, LANES), F32)),
        compiler_params=_params(1),
        name="inproj1_prompt",
    )(x, mods, norm_w, w_bf16)
    rope_tile = lambda i: (i % tiles_per_sample, 0)
    qs, ks, vs = pl.pallas_call(
        _inproj1_sample_kernel,
        grid=(nst,),
        in_specs=[pl.BlockSpec((ROW_TILE, d), lambda i: (npt + i, 0)),
                  pl.BlockSpec((1, 6, d), lambda i: (8 + 1 + i // tiles_per_sample, 0, 0))]
        + common + [pl.BlockSpec((ROW_TILE, LANES), rope_tile),
                    pl.BlockSpec((ROW_TILE, LANES), rope_tile)],
        out_specs=out_specs,
        out_shape=tuple(jax.ShapeDtypeStruct((nst * ROW_TILE, d), BF16) for _ in range(3)),
        compiler_params=_params(1),
        name="inproj1_sample",
    )(x, mods, norm_w, w_bf16, cos_t, sin_t)
    return (qp, kp, vp), (qs, ks, vs), (k_cache, v_cache)


def _rope_tables(n_tok):
    half = DIFF_HD // 4
    pos = np.arange(n_tok)
    lane = np.arange(LANES)
    sub = lane % DIFF_HD
    p = np.where(sub[None, :] < DIFF_HD // 2, (pos // GRID_W)[:, None], (pos % GRID_W)[:, None])
    inv = jnp.asarray(ROPE_THETA, F32) ** (-jnp.asarray(sub % half, F32) / half)
    ang = jnp.asarray(p, F32) * inv[None, :]
    sign = np.where((lane % (2 * half)) < half, -1.0, 1.0).astype(np.float32)
    return jnp.cos(ang), jnp.sin(ang) * sign[None, :]


def _diffattn_kernel(*refs, has_cache, lam_init):
    if has_cache:
        q_ref, k_ref, v_ref, ck_ref, cv_ref, lam_ref, sw_ref, o_ref = refs
    else:
        q_ref, k_ref, v_ref, lam_ref, sw_ref, o_ref = refs
    hd2 = 2 * DIFF_HD
    lv = lam_ref[...]
    lam = (jnp.exp(jnp.sum(lv[0:1] * lv[1:2], axis=1, keepdims=True))
           - jnp.exp(jnp.sum(lv[2:3] * lv[3:4], axis=1, keepdims=True)) + lam_init)
    lane = lax.broadcasted_iota(I32, (q_ref.shape[0], hd2), 1)
    for h in range(DIFF_HEADS):
        cols = slice(h * hd2, (h + 1) * hd2)
        q = q_ref[:, cols]
        zero = jnp.zeros_like(q)
        k_new = k_ref[:, cols].astype(BF16)
        values = [v_ref[:, cols].astype(BF16)]
        if has_cache:
            past = ck_ref.shape[1]
            k_past_t = ck_ref[cols, :].astype(BF16)
            values.append(cv_ref[pl.ds(h, past, stride=DIFF_HEADS), :].astype(BF16))
        o = None
        for c in range(2):
            qc = jnp.where((lane < DIFF_HD) == (c == 0), q, zero)
            s = [_dot_nt(qc, k_new)]
            if has_cache:
                s.append(_dot(qc, k_past_t))
            mx = functools.reduce(jnp.maximum, [jnp.max(si, axis=1, keepdims=True) for si in s])
            e = [jnp.exp(si - mx) for si in s]
            z = functools.reduce(jnp.add, [jnp.sum(ei, axis=1, keepdims=True) for ei in e])
            pv = functools.reduce(jnp.add, [_dot(ei.astype(BF16), v) for ei, v in zip(e, values)])
            pv = pv * (1.0 / z)
            o = pv if c == 0 else o - lam * pv
        o_ref[:, cols] = ((_rms(o) * sw_ref[...]) * (1.0 - lam_init)).astype(BF16)


def _diffattn(q, k, v, lam_vecs, subln_w, batch, seq_len, q_block, lam_init, cache=None):
    d = D_MODEL
    nq = seq_len // q_block
    has_cache = cache is not None
    kv_spec = pl.BlockSpec((seq_len, d), lambda b, qi: (b, 0))
    in_specs = [pl.BlockSpec((q_block, d), lambda b, qi: (b * nq + qi, 0)), kv_spec, kv_spec]
    args = [q, k, v]
    if has_cache:
        past = cache[0].shape[1]
        in_specs += [pl.BlockSpec((d, past), lambda b, qi: (b, 0)),
                     pl.BlockSpec((past * DIFF_HEADS, 2 * DIFF_HD), lambda b, qi: (b, 0))]
        args += list(cache)
    in_specs += [pl.BlockSpec((4, DIFF_HD), lambda b, qi: (0, 0)),
                 pl.BlockSpec((1, 2 * DIFF_HD), lambda b, qi: (0, 0))]
    args += [lam_vecs, subln_w]
    return pl.pallas_call(
        functools.partial(_diffattn_kernel, has_cache=has_cache, lam_init=lam_init),
        grid=(batch, nq),
        in_specs=in_specs,
        out_specs=pl.BlockSpec((q_block, d), lambda b, qi: (b * nq + qi, 0)),
        out_shape=jax.ShapeDtypeStruct((batch * seq_len, d), BF16),
        compiler_params=_params(2),
        name="diffattn_cache" if has_cache else "diffattn",
    )(*args)


def _router_weights(router_group, router_expert):
    w = jnp.concatenate([router_group, router_expert], axis=1)
    w = jnp.pad(w, ((0, 0), (0, LANES - w.shape[1])))
    hi = w.astype(BF16)
    return hi, (w - hi.astype(F32)).astype(BF16)


def _inproj0_weights(w_in):
    gq, gk, gv, gg, gaf, gab, hq, hff, hfb, hi, hg = jnp.split(
        w_in, [256, 512, 1024, 1536, 1552, 1568, 1824, 2080, 2336, 2848], axis=1)
    w = jnp.concatenate([gq, gk, gv, gg, hq, hff, hfb, hi, hg, gaf, gab], axis=1)
    return jnp.pad(w, ((0, 0), (0, AB_COLS - w.shape[1]))).astype(BF16)


def kernel(x_prompt, x_sample, state_gla, state_hgrn, cache_diff_k, cache_diff_v, c, c_ctx,
           w_ada, b_ada, norm1_w, norm2_w, w_in_ab, gla_a2, gla_a_bias, hgrn_lb, gla_onorm_w,
           hgrn_onorm_w, w_out_ab, w_in_c, lam_q1, lam_k1, lam_q2, lam_k2, diff_subln_w, w_out_c,
           router_group, router_expert, moe_w1, moe_w3, moe_w2, final_norm_w):
    bp, lp, d = x_prompt.shape
    bs, ls, _ = x_sample.shape
    depth = w_ada.shape[0]
    assert depth == 2 and d == D_MODEL and bs <= 7
    tp, ts = bp * lp, bs * ls
    npt, nst = tp // ROW_TILE, ts // ROW_TILE
    tps = ls // ROW_TILE
    xp = x_prompt.reshape(tp, d)
    xs = x_sample.reshape(ts, d)

    cond8 = jnp.concatenate([c_ctx[None, :], c, jnp.zeros((7 - bs, d), F32)], axis=0)
    mods = _adaln(cond8, w_ada, b_ada).reshape(depth * 8, 6, d)

    proj = _inproj0(xp, xs, mods, norm1_w[0:1], _inproj0_weights(w_in_ab[0]), ls)
    a_bias = gla_a_bias[0][:, None, :]
    scan_args = (gla_a2[0], a_bias, hgrn_lb, gla_onorm_w[0:1], hgrn_onorm_w[0:1])
    mixed_p, s_fin = _scan(proj, 0, bp, lp, *scan_args)
    s0 = jnp.concatenate([state_gla[:, 0], state_hgrn[:, 0]], axis=2).swapaxes(-1, -2)
    s0 = s0.reshape(bs, 2, SCAN_PAIRS, 2, HEAD_DV, HEAD_DK)
    zero = jnp.zeros_like(s0[:, :, :, 0])
    s0 = jnp.concatenate([jnp.concatenate([s0[:, :, :, 0], zero], axis=-1),
                          jnp.concatenate([zero, s0[:, :, :, 1]], axis=-1)], axis=-2)
    mixed_s = _scan(proj, tp, bs, ls, *scan_args, s0=s0)
    new_state_gla = s_fin[:, None, :, :GLA_HEADS]
    new_state_hgrn = s_fin[:, None, :, GLA_HEADS:]

    wr = _router_weights(router_group[0], router_expert[0])
    x1, *routed = _post((xp, xs), mixed_p, mixed_s, mods, 0, norm2_w[0:1],
                        w_out_ab[0].astype(BF16), *wr, tps)
    ys, tables = _moe(*routed, moe_w1, moe_w3, moe_w2, 0)
    x2 = _combine(*tables, x1, mods, 0, final_norm_w[None, :], ys, 0, npt + nst, npt, tps, False)

    lam_init = 0.8 - 0.6 * math.exp(-0.3 * 1)
    cos_t, sin_t = _rope_tables(ls)
    (qp, kp, vp), (qs, ks, vs), (k_cache, v_cache) = _inproj1(
        x2, mods, norm1_w[1:2], w_in_c[0].astype(BF16), npt, nst, tps, cos_t, sin_t)
    lam_vecs = jnp.stack([lam_q1[0], lam_k1[0], lam_q2[0], lam_k2[0]])
    att_p = _diffattn(qp, kp, vp, lam_vecs, diff_subln_w[0:1], bp, lp, lp, lam_init)
    past = cache_diff_k.shape[2]
    assert lp == ROW_TILE and DIFF_HEADS == TOKEN_ROWS
    cache = (cache_diff_k[:, 0].transpose(0, 2, 3, 4, 1).reshape(bs * d, past),
             cache_diff_v[:, 0].reshape(bs * past * DIFF_HEADS, 2 * DIFF_HD))
    att_s = _diffattn(qs, ks, vs, lam_vecs, diff_subln_w[0:1], bs, ls, ROW_TILE, lam_init, cache)

    wr = _router_weights(router_group[1], router_expert[1])
    x3, *routed = _post((x2,), att_p, att_s, mods, 1, norm2_w[1:2],
                        w_out_c[0].astype(BF16), *wr, tps)
    ys, tables = _moe(*routed, moe_w1, moe_w3, moe_w2, 1)
    fw = final_norm_w[None, :]
    y_p = _combine(*tables, x3, mods, 1, fw, ys, 0, npt, npt, tps, True)
    y_s = _combine(*tables, x3, mods, 1, fw, ys, npt, nst, npt, tps, True)

    return (y_p.reshape(bp, lp, d), y_s.reshape(bs, ls, d), new_state_gla, new_state_hgrn,
            k_cache.reshape(bp, 1, DIFF_HEADS, 2, DIFF_HD, lp).transpose(0, 1, 5, 2, 3, 4),
            v_cache.reshape(bp, 1, lp, DIFF_HEADS, 2 * DIFF_HD))
```

```python
import functools
import math

import jax
import jax.numpy as jnp
import numpy as np
from jax import lax
from jax.experimental import pallas as pl
from jax.experimental.pallas import tpu as pltpu

F32 = jnp.float32
BF16 = jnp.bfloat16
I32 = jnp.int32

D_MODEL = 1024
GLA_HEADS = 4
HGRN_HEADS = 4
SCAN_HEADS = GLA_HEADS + HGRN_HEADS
SCAN_PAIRS = SCAN_HEADS // 2
HEAD_DK = 64
HEAD_DV = 128
GATE_RANK = 16
GLA_GATE_NORM = 16.0
DIFF_HEADS = 8
DIFF_HD = 64
GRID_W = 64
ROPE_THETA = 10000.0
N_GROUPS = 4
EXPERTS_PER_GROUP = 8
N_EXPERTS = N_GROUPS * EXPERTS_PER_GROUP
MOE_HIDDEN = 512
EPS = 1e-6
LANES = 128
TOKEN_ROWS = D_MODEL // LANES
NEG_BIG = -1e30

ROW_TILE = 256
ADA_TILE = 1536
INPROJ0_TILE = 512
SCAN_CHUNK = 64
EXPERT_TILE = 256
MOVE_BLOCK = 16
STAGE_TOKENS = 2 * ROW_TILE + N_EXPERTS * MOVE_BLOCK
MAX_BLOCKS = STAGE_TOKENS // MOVE_BLOCK
VMEM_LIMIT = 56 * 1024 * 1024
SCAN_INPUT_DOUBLE_BUFFER_BYTES = 8 * 1024 * 1024

_C_GQ, _C_GK, _C_GV, _C_GG = 0, 256, 512, 1024
_C_HQ, _C_HFF, _C_HFB, _C_HI, _C_HG = 1536, 1792, 2048, 2304, 2816
_C_GAF, _C_GAB = 3328, 3344
AB_COLS = 3456


def _params(n_axes, vmem=VMEM_LIMIT):
    return pltpu.CompilerParams(dimension_semantics=("arbitrary",) * n_axes,
                                vmem_limit_bytes=vmem)


def _cdiv(a, b):
    return (a + b - 1) // b


def _dot(a, b):
    return jnp.dot(a, b, preferred_element_type=F32)


def _dot_nt(a, b):
    return lax.dot_general(a, b, (((1,), (1,)), ((), ())), preferred_element_type=F32)


def _dot_tn(a, b):
    return lax.dot_general(a, b, (((0,), (0,)), ((), ())), preferred_element_type=F32)


def _split_bf16(x):
    hi = x.astype(BF16)
    lo = (x - hi.astype(F32)).astype(BF16)
    return hi, lo


def _silu(x):
    return x * jax.nn.sigmoid(x)


def _log_sigmoid(x):
    return jnp.minimum(x, 0.0) - jnp.log(1.0 + jnp.exp(-jnp.abs(x)))


def _rms(x):
    return x * lax.rsqrt(jnp.mean(x * x, axis=-1, keepdims=True) + EPS)


def _modulate(x, norm_w, shift, scale):
    return (_rms(x) * norm_w) * (1.0 + scale) + shift


def _to_token_major(dst_ref, x, row0=0):
    n = x.shape[0]
    for s in range(TOKEN_ROWS):
        dst_ref[pl.ds(row0 + s, n, stride=TOKEN_ROWS), :] = x[:, s * LANES:(s + 1) * LANES]


def _from_token_major(src_ref, n, row0=0):
    return jnp.concatenate([src_ref[pl.ds(row0 + s, n, stride=TOKEN_ROWS), :]
                            for s in range(TOKEN_ROWS)], axis=1)


def _ada_kernel(c_ref, w_ref, b_ref, o_ref):
    s = _silu(c_ref[...])
    o_ref[0] = _dot(s.astype(BF16), w_ref[0].astype(BF16)) + b_ref[0]


def _adaln(cond8, w_ada, b_ada):
    depth, d, n = w_ada.shape
    tn = ADA_TILE
    return pl.pallas_call(
        _ada_kernel,
        grid=(depth, n // tn),
        in_specs=[pl.BlockSpec((8, d), lambda l, j: (0, 0)),
                  pl.BlockSpec((1, d, tn), lambda l, j: (l, 0, j)),
                  pl.BlockSpec((1, 1, tn), lambda l, j: (l, 0, j))],
        out_specs=pl.BlockSpec((1, 8, tn), lambda l, j: (l, 0, j)),
        out_shape=jax.ShapeDtypeStruct((depth, 8, n), F32),
        compiler_params=_params(2),
        name="adaln",
    )(cond8, w_ada, b_ada.reshape(depth, 1, n))


def _mod_row(i, layer, n_prompt_tiles, tiles_per_sample):
    r = jnp.where(i < n_prompt_tiles, 0, 1 + (i - n_prompt_tiles) // tiles_per_sample)
    return layer * 8 + r


def _inproj0_kernel(xp_ref, xs_ref, mod_ref, nw_ref, w_ref, o_ref, *, n_prompt_tiles):
    i = pl.program_id(0)
    x = jnp.where(i < n_prompt_tiles, xp_ref[...], xs_ref[...])
    h = _modulate(x, nw_ref[...], mod_ref[0, 0:1, :], mod_ref[0, 1:2, :])
    o_ref[...] = _dot(h.astype(BF16), w_ref[...])


def _inproj0(xp, xs, mods, norm_w, w_bf16, sample_len):
    tp, d = xp.shape
    ts = xs.shape[0]
    n = w_bf16.shape[1]
    tile = INPROJ0_TILE
    npt, nst = tp // tile, ts // tile
    mod_map = lambda i: (_mod_row(i, 0, npt, sample_len // tile), 0, 0)
    return pl.pallas_call(
        functools.partial(_inproj0_kernel, n_prompt_tiles=npt),
        grid=(npt + nst,),
        in_specs=[pl.BlockSpec((tile, d), lambda i: (jnp.minimum(i, npt - 1), 0)),
                  pl.BlockSpec((tile, d), lambda i: (jnp.maximum(i - npt, 0), 0)),
                  pl.BlockSpec((1, 6, d), mod_map),
                  pl.BlockSpec((1, d), lambda i: (0, 0)),
                  pl.BlockSpec((d, n), lambda i: (0, 0))],
        out_specs=pl.BlockSpec((tile, n), lambda i: (i, 0)),
        out_shape=jax.ShapeDtypeStruct((tp + ts, n), F32),
        compiler_params=_params(1),
        name="inproj0",
    )(xp, xs, mods, norm_w, w_bf16)


def _scan_kernel(*refs, seq_len, has_state):
    if has_state:
        (p_ref, a2_ref, ab_ref, lb_ref, ong_ref, onh_ref, s0_ref, mixed_ref, *scratch) = refs
        sfin_ref = None
    else:
        (p_ref, a2_ref, ab_ref, lb_ref, ong_ref, onh_ref, mixed_ref, sfin_ref, *scratch) = refs
        s0_ref = None
    (qi_f, ki_f, qo_f, ko_f, qi_b, ki_b, qo_b, ko_b,
     vv, dec_f, dec_b, o_f, o_b, st_f, st_b) = scratch
    C = SCAN_CHUNK
    n_chunks = seq_len // C
    gqk = GLA_HEADS * HEAD_DK

    row = lax.broadcasted_iota(I32, (C, C), 0)
    col = lax.broadcasted_iota(I32, (C, C), 1)
    lower = col <= row
    upper = col >= row
    tri_lo = jnp.where(lower, 1.0, 0.0).astype(BF16)
    tri_up = jnp.where(upper, 1.0, 0.0).astype(BF16)

    lbp = lb_ref[...]
    lb_max = jnp.maximum(lbp[0], lbp[1])
    lb_e0 = jnp.exp(lbp[0] - lb_max)
    lb_e1 = jnp.exp(lbp[1] - lb_max)
    lb = lb_e0 / (lb_e0 + lb_e1)

    def cumsum_chunk(tri, la):
        hi, lo = _split_bf16(la)
        return _dot(tri, hi) + _dot(tri, lo)

    def prep(n, carry):
        r0 = pl.multiple_of(n * C, C)
        rows = pl.ds(r0, C)
        gq = p_ref[rows, _C_GQ:_C_GQ + gqk] * (HEAD_DK ** -0.5)
        gk = p_ref[rows, _C_GK:_C_GK + gqk]
        hq = _silu(p_ref[rows, _C_HQ:_C_HQ + gqk]) * (HEAD_DK ** -0.5)
        for d_i, (qi_s, ki_s, qo_s, ko_s, dec_s, tri, last, mid) in enumerate(
                ((qi_f, ki_f, qo_f, ko_f, dec_f, tri_lo, C - 1, C // 2 - 1),
                 (qi_b, ki_b, qo_b, ko_b, dec_b, tri_up, 0, C // 2))):
            c_ga = _C_GAF if d_i == 0 else _C_GAB
            c_hf = _C_HFF if d_i == 0 else _C_HFB
            ga = p_ref[rows, c_ga:c_ga + GATE_RANK]
            xg = _dot(ga.astype(BF16), a2_ref[d_i].astype(BF16)) + ab_ref[d_i]
            la_g = _log_sigmoid(xg) / GLA_GATE_NORM
            f = lb[d_i:d_i + 1, :] + (1.0 - lb[d_i:d_i + 1, :]) * jax.nn.sigmoid(
                p_ref[rows, c_hf:c_hf + gqk])
            la_h = jnp.log(f)
            for q, k, la, c0 in ((gq, gk, la_g, 0), (hq, 1.0 - f, la_h, gqk)):
                b = cumsum_chunk(tri, la)
                b_mid, b_end = b[mid:mid + 1, :], b[last:last + 1, :]
                cs = slice(c0, c0 + gqk)
                qi_s[rows, cs] = (q * jnp.exp(b - b_mid)).astype(BF16)
                ki_s[rows, cs] = (k * jnp.exp(b_mid - b)).astype(BF16)
                qo_s[rows, cs] = (q * jnp.exp(b)).astype(BF16)
                ko_s[rows, cs] = (k * jnp.exp(b_end - b)).astype(BF16)
                dec_s[n, :, cs] = jnp.exp(b_end)
        gv_cols = GLA_HEADS * HEAD_DV
        vv[rows, 0:gv_cols] = p_ref[rows, _C_GV:_C_GV + gv_cols].astype(BF16)
        vv[rows, gv_cols:] = p_ref[rows, _C_HI:_C_HI + HGRN_HEADS * HEAD_DV].astype(BF16)
        return carry

    lax.fori_loop(0, n_chunks, prep, 0, unroll=2)

    for p in range(SCAN_PAIRS):
        if has_state:
            st_f[p] = s0_ref[0, 0, p]
            st_b[p] = s0_ref[0, 1, p]
        else:
            st_f[p] = jnp.zeros((2 * HEAD_DV, 2 * HEAD_DK), F32)
            st_b[p] = jnp.zeros((2 * HEAD_DV, 2 * HEAD_DK), F32)

    first_head = lax.broadcasted_iota(I32, (C, 2 * HEAD_DK), 1) < HEAD_DK
    row2 = lax.broadcasted_iota(I32, (2 * C, C), 0) % C
    col2 = lax.broadcasted_iota(I32, (2 * C, C), 1)
    lower2 = col2 <= row2
    upper2 = col2 >= row2

    def per_head_rows(x):
        z = jnp.zeros_like(x)
        return jnp.concatenate([jnp.where(first_head, x, z), jnp.where(first_head, z, x)], axis=0)

    def put_out(o_ref, rows, p, res):
        c0 = p * 2 * HEAD_DV
        o_ref[rows, c0:c0 + HEAD_DV] = res[0:C, 0:HEAD_DV]
        o_ref[rows, c0 + HEAD_DV:c0 + 2 * HEAD_DV] = res[C:2 * C, HEAD_DV:2 * HEAD_DV]

    def sweep(n, carry):
        m = n_chunks - 1 - n
        rows = pl.ds(pl.multiple_of(n * C, C), C)
        rows_m = pl.ds(pl.multiple_of(m * C, C), C)
        decay_f, decay_b = dec_f[n], dec_b[m]
        for p in range(SCAN_PAIRS):
            ks = slice(p * 2 * HEAD_DK, (p + 1) * 2 * HEAD_DK)
            vs = slice(p * 2 * HEAD_DV, (p + 1) * 2 * HEAD_DV)
            vh = vv[rows, vs]
            s_f = st_f[p]
            sc = (jnp.where(lower2, _dot_nt(per_head_rows(qi_f[rows, ks]), ki_f[rows, ks]), 0.0)
                  + jnp.where(upper2, _dot_nt(per_head_rows(qi_b[rows, ks]), ki_b[rows, ks]), 0.0))
            put_out(o_f, rows, p, _dot_nt(per_head_rows(qo_f[rows, ks]), s_f.astype(BF16))
                    + _dot(sc.astype(BF16), vh))
            st_f[p] = decay_f[:, ks] * s_f + _dot_tn(vh, ko_f[rows, ks])
            s_b = st_b[p]
            put_out(o_b, rows_m, p, _dot_nt(per_head_rows(qo_b[rows_m, ks]), s_b.astype(BF16)))
            st_b[p] = decay_b[:, ks] * s_b + _dot_tn(vv[rows_m, vs], ko_b[rows_m, ks])
        return carry

    lax.fori_loop(0, n_chunks, sweep, 0)

    def finish(n, carry):
        rows = pl.ds(pl.multiple_of(n * C, C), C)
        for h in range(SCAN_HEADS):
            vs = slice(h * HEAD_DV, (h + 1) * HEAD_DV)
            if h < GLA_HEADS:
                gate = p_ref[rows, _C_GG + h * HEAD_DV:_C_GG + (h + 1) * HEAD_DV]
                onw = ong_ref[...]
            else:
                hh = h - GLA_HEADS
                gate = p_ref[rows, _C_HG + hh * HEAD_DV:_C_HG + (hh + 1) * HEAD_DV]
                onw = onh_ref[...]
            o = o_f[rows, vs] + o_b[rows, vs]
            mixed_ref[rows, vs] = ((_rms(o) * onw) * _silu(gate)).astype(BF16)
        return carry

    lax.fori_loop(0, n_chunks, finish, 0, unroll=2)

    if sfin_ref is not None:
        for d_i, st in enumerate((st_f, st_b)):
            for p in range(SCAN_PAIRS):
                s_pair = st[p].T
                sfin_ref[0, d_i, 2 * p] = s_pair[0:HEAD_DK, 0:HEAD_DV]
                sfin_ref[0, d_i, 2 * p + 1] = s_pair[HEAD_DK:2 * HEAD_DK, HEAD_DV:2 * HEAD_DV]


def _scan(p, row0, batch, seq_len, a2, a_bias, lb, onorm_g, onorm_h, s0=None):
    n = p.shape[1]
    assert row0 % seq_len == 0
    blk0 = row0 // seq_len
    has_state = s0 is not None
    n_chunks = seq_len // SCAN_CHUNK
    st_shape = (1, 2, SCAN_HEADS, HEAD_DK, HEAD_DV)
    pair_shape = (SCAN_PAIRS, 2 * HEAD_DV, 2 * HEAD_DK)
    p_mode = dict(pipeline_mode=pl.Buffered(1)) if seq_len * n * 4 > SCAN_INPUT_DOUBLE_BUFFER_BYTES else {}
    in_specs = [pl.BlockSpec((seq_len, n), lambda b: (blk0 + b, 0), **p_mode),
                pl.BlockSpec(a2.shape, lambda b: (0, 0, 0)),
                pl.BlockSpec(a_bias.shape, lambda b: (0, 0, 0)),
                pl.BlockSpec(lb.shape, lambda b: (0, 0, 0)),
                pl.BlockSpec((1, HEAD_DV), lambda b: (0, 0)),
                pl.BlockSpec((1, HEAD_DV), lambda b: (0, 0))]
    args = [p, a2, a_bias, lb, onorm_g, onorm_h]
    mixed_shape = jax.ShapeDtypeStruct((batch * seq_len, D_MODEL), BF16)
    mixed_spec = pl.BlockSpec((seq_len, D_MODEL), lambda b: (b, 0))
    if has_state:
        in_specs.append(pl.BlockSpec((1, 2) + pair_shape, lambda b: (b, 0, 0, 0, 0)))
        args.append(s0)
        out_shape, out_specs = mixed_shape, mixed_spec
    else:
        out_shape = (mixed_shape, jax.ShapeDtypeStruct((batch,) + st_shape[1:], F32))
        out_specs = (mixed_spec, pl.BlockSpec(st_shape, lambda b: (b, 0, 0, 0, 0)))
    qk_cols = SCAN_HEADS * HEAD_DK
    scratch = [pltpu.VMEM((seq_len, qk_cols), BF16) for _ in range(8)]
    scratch += [pltpu.VMEM((seq_len, D_MODEL), BF16),
                pltpu.VMEM((n_chunks, 1, qk_cols), F32), pltpu.VMEM((n_chunks, 1, qk_cols), F32),
                pltpu.VMEM((seq_len, D_MODEL), F32), pltpu.VMEM((seq_len, D_MODEL), F32),
                pltpu.VMEM(pair_shape, F32), pltpu.VMEM(pair_shape, F32)]
    return pl.pallas_call(
        functools.partial(_scan_kernel, seq_len=seq_len, has_state=has_state),
        grid=(batch,),
        in_specs=in_specs, out_specs=out_specs, out_shape=out_shape,
        scratch_shapes=scratch,
        compiler_params=_params(1),
        name="scan_state" if has_state else "scan_fresh",
    )(*args)


def _post_kernel(*refs, split_x, n_prompt_tiles):
    if split_x:
        xp_ref, xs_ref = refs[0], refs[1]
        refs = refs[2:]
    else:
        x_ref = refs[0]
        refs = refs[1:]
    (mp_ref, ms_ref, mod_ref, nw_ref, wo_ref, wrh_ref, wrl_ref,
     x1_ref, h2_ref, slot_ref, wgt_ref, tab_ref, carry) = refs
    i = pl.program_id(0)
    is_prompt = i < n_prompt_tiles
    if split_x:
        x = jnp.where(is_prompt, xp_ref[...], xs_ref[...])
    else:
        x = x_ref[...]
    mixed = jnp.where(is_prompt, mp_ref[...], ms_ref[...])
    x1 = x + mod_ref[0, 2:3, :] * _dot(mixed, wo_ref[...])
    x1_ref[...] = x1
    h2 = _modulate(x1, nw_ref[...], mod_ref[0, 3:4, :], mod_ref[0, 4:5, :])
    _to_token_major(h2_ref, h2)

    hh, hl = _split_bf16(h2)
    logits = _dot(hh, wrh_ref[...]) + _dot(hl, wrh_ref[...]) + _dot(hh, wrl_ref[...])
    tm = logits.shape[0]
    lane = lax.broadcasted_iota(I32, (tm, LANES), 1).astype(F32)

    def first_max(v):
        mx = jnp.max(v, axis=1, keepdims=True)
        idx = jnp.min(jnp.where(v == mx, lane, float(LANES)), axis=1, keepdims=True)
        return mx, idx

    gl = jnp.where(lane < N_GROUPS, logits, NEG_BIG)
    gmax, gidx = first_max(gl)
    g_val = 1.0 / jnp.sum(jnp.exp(gl - gmax), axis=1, keepdims=True)
    lo = N_GROUPS + EXPERTS_PER_GROUP * gidx
    el = jnp.where((lane >= lo) & (lane < lo + EXPERTS_PER_GROUP), logits, NEG_BIG)
    emax, l1 = first_max(el)
    esum = jnp.sum(jnp.exp(el - emax), axis=1, keepdims=True)
    e2max, l2 = first_max(jnp.where(lane == l1, NEG_BIG, el))
    p1 = 1.0 / esum
    p2 = jnp.exp(e2max - emax) / esum
    w1 = g_val * (p1 / (p1 + p2))
    w2 = g_val * (p2 / (p1 + p2))
    id1 = l1 - N_GROUPS
    id2 = l2 - N_GROUPS

    @pl.when(i == 0)
    def _():
        carry[...] = jnp.zeros_like(carry)

    sel1 = lane == id1
    sel2 = lane == id2
    onehot = jnp.where(sel1 | sel2, 1.0, 0.0)
    row = lax.broadcasted_iota(I32, (tm, tm), 0)
    col = lax.broadcasted_iota(I32, (tm, tm), 1)
    earlier = jnp.where(col < row, 1.0, 0.0).astype(BF16)
    before = _dot(earlier, onehot.astype(BF16))
    count = jnp.sum(onehot, axis=0, keepdims=True)
    blocks = jnp.floor((count + (MOVE_BLOCK - 1.0)) * (1.0 / MOVE_BLOCK)) * MOVE_BLOCK
    e_row = lax.broadcasted_iota(I32, (LANES, LANES), 0)
    e_col = lax.broadcasted_iota(I32, (LANES, LANES), 1)
    lower_experts = jnp.where(e_row < e_col, 1.0, 0.0).astype(BF16)
    run_start = _dot(jnp.broadcast_to(blocks, (8, LANES)).astype(BF16), lower_experts)[0:1]
    slot = before + run_start
    q1 = jnp.sum(jnp.where(sel1, slot, 0.0), axis=1, keepdims=True)
    q2 = jnp.sum(jnp.where(sel2, slot, 0.0), axis=1, keepdims=True)
    tab_row = lax.broadcasted_iota(I32, (8, LANES), 0)
    tab_ref[0] = jnp.where(tab_row == 0, count,
                           jnp.where(tab_row == 1, carry[...],
                                     jnp.where(tab_row == 2, run_start, 0.0)))
    carry[...] = carry[...] + count

    cols = jnp.zeros((tm, LANES), F32)
    for k, v in enumerate((q1 * TOKEN_ROWS, q2 * TOKEN_ROWS, w1, w2)):
        cols = jnp.where(lane == k, v, cols)
    rows = cols.T
    slot_ref[0] = rows[0:2].astype(I32)
    wgt_ref[0] = rows[2:4]


def _post(x_args, mixed_p, mixed_s, mods, layer, norm_w, w_out_bf16, wr_hi, wr_lo,
          tiles_per_sample):
    split_x = len(x_args) == 2
    tp, ts = mixed_p.shape[0], mixed_s.shape[0]
    t, d = tp + ts, D_MODEL
    npt, nst = tp // ROW_TILE, ts // ROW_TILE
    tile = lambda i: (i, 0)
    if split_x:
        x_specs = [pl.BlockSpec((ROW_TILE, d), lambda i: (jnp.minimum(i, npt - 1), 0)),
                   pl.BlockSpec((ROW_TILE, d), lambda i: (jnp.maximum(i - npt, 0), 0))]
    else:
        x_specs = [pl.BlockSpec((ROW_TILE, d), tile)]
    in_specs = x_specs + [
        pl.BlockSpec((ROW_TILE, d), lambda i: (jnp.minimum(i, npt - 1), 0)),
        pl.BlockSpec((ROW_TILE, d), lambda i: (jnp.maximum(i - npt, 0), 0)),
        pl.BlockSpec((1, 6, d), lambda i: (_mod_row(i, layer, npt, tiles_per_sample), 0, 0)),
        pl.BlockSpec((1, d), lambda i: (0, 0)),
        pl.BlockSpec((d, d), lambda i: (0, 0)),
        pl.BlockSpec((d, LANES), lambda i: (0, 0)),
        pl.BlockSpec((d, LANES), lambda i: (0, 0))]
    return pl.pallas_call(
        functools.partial(_post_kernel, split_x=split_x, n_prompt_tiles=npt),
        grid=(npt + nst,),
        in_specs=in_specs,
        out_specs=(pl.BlockSpec((ROW_TILE, d), tile),
                   pl.BlockSpec((ROW_TILE * TOKEN_ROWS, LANES), tile),
                   pl.BlockSpec((1, 2, ROW_TILE), lambda i: (i, 0, 0)),
                   pl.BlockSpec((1, 2, ROW_TILE), lambda i: (i, 0, 0)),
                   pl.BlockSpec((1, 8, LANES), lambda i: (i, 0, 0))),
        out_shape=(jax.ShapeDtypeStruct((t, d), F32),
                   jax.ShapeDtypeStruct((t * TOKEN_ROWS, LANES), F32),
                   jax.ShapeDtypeStruct((npt + nst, 2, ROW_TILE), I32),
                   jax.ShapeDtypeStruct((npt + nst, 2, ROW_TILE), F32),
                   jax.ShapeDtypeStruct((npt + nst, 8, LANES), F32)),
        scratch_shapes=[pltpu.VMEM((1, LANES), F32)],
        compiler_params=_params(1),
        name=f"post{layer}",
    )(*x_args, mixed_p, mixed_s, mods, norm_w, w_out_bf16, wr_hi, wr_lo)


def _for_blocks(tab_ref, fn):
    block_rows = MOVE_BLOCK * TOKEN_ROWS
    count = tab_ref[0, 0, MAX_BLOCKS]

    def call(k, parity):
        fn(pl.multiple_of(k * block_rows, block_rows),
           pl.multiple_of(tab_ref[0, 0, k], TOKEN_ROWS), parity)

    def body(k2, c):
        call(2 * k2, 0)

        @pl.when(2 * k2 + 1 < count)
        def _():
            call(2 * k2 + 1, 1)
        return c

    lax.fori_loop(0, _cdiv(count, 2), body, 0)


def _wait_blocks(tab_ref, copy):
    def body(k, c):
        copy.wait()
        return c

    lax.fori_loop(0, tab_ref[0, 0, MAX_BLOCKS], body, 0)


def _dispatch_kernel(zero_ref, tab_ref, prev_tab_ref, q_ref, h2_ref, hs_ref, zero_buf, stage, sem):
    j = pl.program_id(0)
    slot = j % 2
    block_rows = MOVE_BLOCK * TOKEN_ROWS

    @pl.when(j == 0)
    def _():
        zero_buf[...] = jnp.zeros_like(zero_buf)

        def zero_copy(k):
            start = pl.multiple_of(zero_ref[k], EXPERT_TILE * TOKEN_ROWS)
            return pltpu.make_async_copy(
                zero_buf, hs_ref.at[pl.ds(start, EXPERT_TILE * TOKEN_ROWS)], sem.at[0])

        def start_zero(k, c):
            @pl.when(zero_ref[k] >= 0)
            def _():
                zero_copy(k).start()
            return c

        def wait_zero(k, c):
            @pl.when(zero_ref[k] >= 0)
            def _():
                zero_copy(k).wait()
            return c

        lax.fori_loop(0, zero_ref.shape[0], start_zero, 0)
        lax.fori_loop(0, zero_ref.shape[0], wait_zero, 0)

        stage[...] = jnp.zeros_like(stage)

    def place(r, c):
        tok = h2_ref[pl.ds(pl.multiple_of(r * TOKEN_ROWS, TOKEN_ROWS), TOKEN_ROWS), :]
        for s in range(2):
            row = pl.multiple_of(q_ref[0, s, r], TOKEN_ROWS)
            stage[slot, pl.ds(row, TOKEN_ROWS), :] = tok
        return c

    lax.fori_loop(0, ROW_TILE, place, 0, unroll=8)

    def block_copy(buf, stage_row, sorted_row):
        return pltpu.make_async_copy(stage.at[buf, pl.ds(stage_row, block_rows)],
                                     hs_ref.at[pl.ds(sorted_row, block_rows)], sem.at[buf])

    @pl.when(j > 0)
    def _():
        _wait_blocks(prev_tab_ref, block_copy(1 - slot, 0, 0))

    _for_blocks(tab_ref, lambda a, b, parity: block_copy(slot, a, b).start(priority=parity))

    @pl.when(j == pl.num_programs(0) - 1)
    def _():
        _wait_blocks(tab_ref, block_copy(slot, 0, 0))


def _dispatch(zero_tiles, block_tab, slots, h2, n_rows):
    t = h2.shape[0] // TOKEN_ROWS
    nt = t // ROW_TILE
    smem_tile = lambda shape: pl.BlockSpec((1,) + shape, lambda j, *_: (j, 0, 0),
                                           memory_space=pltpu.SMEM)
    grid_spec = pltpu.PrefetchScalarGridSpec(
        num_scalar_prefetch=1,
        grid=(nt,),
        in_specs=[smem_tile((1, LANES)),
                  pl.BlockSpec((1, 1, LANES), lambda j, *_: (jnp.maximum(j - 1, 0), 0, 0),
                               memory_space=pltpu.SMEM),
                  smem_tile((2, ROW_TILE)),
                  pl.BlockSpec((ROW_TILE * TOKEN_ROWS, LANES), lambda j, *_: (j, 0))],
        out_specs=pl.BlockSpec(memory_space=pl.ANY),
        scratch_shapes=[pltpu.VMEM((EXPERT_TILE * TOKEN_ROWS, LANES), F32),
                        pltpu.VMEM((2, STAGE_TOKENS * TOKEN_ROWS, LANES), F32),
                        pltpu.SemaphoreType.DMA((2,))])
    return pl.pallas_call(
        _dispatch_kernel,
        grid_spec=grid_spec,
        out_shape=jax.ShapeDtypeStruct((n_rows * TOKEN_ROWS, LANES), F32),
        compiler_params=_params(1),
        name="dispatch",
    )(zero_tiles, block_tab, block_tab, slots, h2)


def _expert_kernel(te_ref, src_ref, nv_ref, run_ref, nxt_ref, hs_ref, w1_hbm, w3_hbm, w2_hbm,
                   ys_ref, w1f, w3f, w2f, w1b, w3b, w2b, sem, *, layer):
    i = pl.program_id(0)

    def weight_copies(e, buf):
        return [pltpu.make_async_copy(src.at[layer, e], dst.at[buf], sem.at[buf])
                for src, dst in ((w1_hbm, w1f), (w3_hbm, w3f), (w2_hbm, w2f))]

    @pl.when(i == 0)
    def _():
        for c in weight_copies(te_ref[0], 0):
            c.start()

    first = (i == 0) | (run_ref[i] != run_ref[jnp.maximum(i - 1, 0)])

    @pl.when(first)
    def _():
        buf = run_ref[i] % 2
        for c in weight_copies(te_ref[i], buf):
            c.wait()

        @pl.when(nxt_ref[i] >= 0)
        def _():
            for c in weight_copies(nxt_ref[i], 1 - buf):
                c.start()

        w1b[...] = w1f[buf].astype(BF16)
        w3b[...] = w3f[buf].astype(BF16)
        w2b[...] = w2f[buf].astype(BF16)

    @pl.when(nv_ref[i] > 0)
    def _():
        h = _from_token_major(hs_ref, EXPERT_TILE).astype(BF16)
        g = _silu(_dot(h, w1b[...])) * _dot(h, w3b[...])
        _to_token_major(ys_ref, _dot(g.astype(BF16), w2b[...]))

    @pl.when(nv_ref[i] == 0)
    def _():
        ys_ref[...] = jnp.zeros_like(ys_ref)


def _experts(tile_expert, tile_src, tile_rows, hs, w1, w3, w2, layer):
    n_rows, d = hs.shape[0] // TOKEN_ROWS, D_MODEL
    nt = n_rows // EXPERT_TILE
    hid = w1.shape[-1]
    tok_tile = (EXPERT_TILE * TOKEN_ROWS, LANES)
    changed = jnp.concatenate([jnp.zeros((1,), I32),
                               (tile_expert[1:] != tile_expert[:-1]).astype(I32)])
    run = jnp.cumsum(changed).astype(I32)
    later = jnp.where(run[None, :] > run[:, None], tile_expert[None, :], N_EXPERTS)
    next_expert = jnp.min(later, axis=1)
    next_expert = jnp.where(next_expert < N_EXPERTS, next_expert, -1).astype(I32)
    grid_spec = pltpu.PrefetchScalarGridSpec(
        num_scalar_prefetch=5,
        grid=(nt,),
        in_specs=[pl.BlockSpec(tok_tile, lambda i, te, src, nv, run, nxt: (src[i], 0)),
                  pl.BlockSpec(memory_space=pl.ANY), pl.BlockSpec(memory_space=pl.ANY),
                  pl.BlockSpec(memory_space=pl.ANY)],
        out_specs=pl.BlockSpec(tok_tile, lambda i, te, src, nv, run, nxt: (i, 0)),
        scratch_shapes=[pltpu.VMEM((2, d, hid), F32), pltpu.VMEM((2, d, hid), F32),
                        pltpu.VMEM((2, hid, d), F32),
                        pltpu.VMEM((d, hid), BF16), pltpu.VMEM((d, hid), BF16),
                        pltpu.VMEM((hid, d), BF16), pltpu.SemaphoreType.DMA((2,))])
    return pl.pallas_call(
        functools.partial(_expert_kernel, layer=layer),
        grid_spec=grid_spec,
        out_shape=jax.ShapeDtypeStruct(hs.shape, F32),
        compiler_params=_params(1),
        name=f"experts{layer}",
    )(tile_expert, tile_src, tile_rows, run, next_expert, hs, w1, w3, w2)


def _combine_kernel(tab_ref, next_tab_ref, q_ref, w_ref, x1_ref, mod_ref, fw_ref, ys_ref, out_ref,
                    stage, y_tok, sem, *, final_norm):
    i = pl.program_id(0)
    slot = i % 2
    block_rows = MOVE_BLOCK * TOKEN_ROWS

    def block_copy(buf, stage_row, sorted_row):
        return pltpu.make_async_copy(ys_ref.at[pl.ds(sorted_row, block_rows)],
                                     stage.at[buf, pl.ds(stage_row, block_rows)], sem.at[buf])

    def fetch(tab, buf):
        _for_blocks(tab, lambda a, b, parity: block_copy(buf, a, b).start(priority=parity))

    @pl.when(i == 0)
    def _():
        fetch(tab_ref, slot)

    @pl.when(i + 1 < pl.num_programs(0))
    def _():
        fetch(next_tab_ref, 1 - slot)

    _wait_blocks(tab_ref, block_copy(slot, 0, 0))

    def pick(r, c):
        rows = [stage[slot, pl.ds(pl.multiple_of(q_ref[0, s, r], TOKEN_ROWS), TOKEN_ROWS), :]
                for s in range(2)]
        y_tok[pl.ds(pl.multiple_of(r * TOKEN_ROWS, TOKEN_ROWS), TOKEN_ROWS), :] = (
            w_ref[0, 0, r] * rows[0] + w_ref[0, 1, r] * rows[1])
        return c

    lax.fori_loop(0, ROW_TILE, pick, 0, unroll=8)
    x2 = x1_ref[...] + mod_ref[0, 5:6, :] * _from_token_major(y_tok, ROW_TILE)
    if final_norm:
        x2 = _rms(x2) * fw_ref[...]
    out_ref[...] = x2


def _combine(block_tab, slots, weights, x1, mods, layer, final_w, ys, tile0, n_tiles,
             n_prompt_tiles, tiles_per_sample, final_norm):
    d = D_MODEL
    tile = lambda i: (tile0 + i, 0)
    mod_map = lambda i: (_mod_row(tile0 + i, layer, n_prompt_tiles, tiles_per_sample), 0, 0)
    smem_tile = lambda shape: pl.BlockSpec((1,) + shape, lambda i: (tile0 + i, 0, 0),
                                           memory_space=pltpu.SMEM)
    return pl.pallas_call(
        functools.partial(_combine_kernel, final_norm=final_norm),
        grid=(n_tiles,),
        in_specs=[smem_tile((1, LANES)),
                  pl.BlockSpec((1, 1, LANES),
                               lambda i: (tile0 + jnp.minimum(i + 1, n_tiles - 1), 0, 0),
                               memory_space=pltpu.SMEM),
                  smem_tile((2, ROW_TILE)), smem_tile((2, ROW_TILE)),
                  pl.BlockSpec((ROW_TILE, d), tile),
                  pl.BlockSpec((1, 6, d), mod_map),
                  pl.BlockSpec((1, d), lambda i: (0, 0)),
                  pl.BlockSpec(memory_space=pl.ANY)],
        out_specs=pl.BlockSpec((ROW_TILE, d), lambda i: (i, 0)),
        out_shape=jax.ShapeDtypeStruct((n_tiles * ROW_TILE, d), F32),
        scratch_shapes=[pltpu.VMEM((2, STAGE_TOKENS * TOKEN_ROWS, LANES), F32),
                        pltpu.VMEM((ROW_TILE * TOKEN_ROWS, LANES), F32),
                        pltpu.SemaphoreType.DMA((2,))],
        compiler_params=_params(1),
        name=f"combine{layer}_{tile0}",
    )(block_tab, block_tab, slots, weights, x1, mods, final_w, ys)


def _moe(h2, slots, weights, tile_tab, w1, w3, w2, layer):
    t = h2.shape[0] // TOKEN_ROWS
    n_tiles = t // ROW_TILE
    extra_tiles = N_EXPERTS + _cdiv(N_EXPERTS * MOVE_BLOCK, EXPERT_TILE)
    n_rows = 2 * t + extra_tiles * EXPERT_TILE
    nt = n_rows // EXPERT_TILE
    tab = tile_tab[:, :, :N_EXPERTS].astype(I32)
    cnt = tab[-1, 0] + tab[-1, 1]
    tight = _cdiv(cnt, EXPERT_TILE) * EXPERT_TILE
    padded = jnp.where(cnt > 0, _cdiv(cnt + MOVE_BLOCK - 1, EXPERT_TILE) * EXPERT_TILE, 0)
    ends = jnp.cumsum(padded)
    offsets = ends - padded
    tails = jnp.where(cnt > 0, ends - EXPERT_TILE, -1)
    tails2 = jnp.where(padded > tight, ends - 2 * EXPERT_TILE, -1)
    used = ends[-1] // EXPERT_TILE
    tile_start = jnp.arange(nt, dtype=I32) * EXPERT_TILE
    unused = (used + jnp.arange(extra_tiles, dtype=I32)) * EXPERT_TILE
    zero_tiles = jnp.concatenate([tails, tails2, jnp.where(unused < n_rows, unused, -1)])
    zero_tiles = jnp.where(zero_tiles >= 0, zero_tiles * TOKEN_ROWS, -1).astype(I32)
    tile_src = jnp.minimum(jnp.arange(nt, dtype=I32), used - 1)
    tile_expert = jnp.sum((tile_src * EXPERT_TILE)[:, None] >= ends[None, :], axis=1).astype(I32)
    tile_rows = jnp.where(tile_start < ends[-1],
                          jnp.clip(cnt[tile_expert] - (tile_start - offsets[tile_expert]),
                                   0, EXPERT_TILE), 0).astype(I32)
    n_blocks = _cdiv(tab[:, 0], MOVE_BLOCK)
    blocks_through = jnp.cumsum(n_blocks, axis=1)
    k = jnp.arange(MAX_BLOCKS, dtype=I32)
    owner = jnp.sum(blocks_through[:, None, :] <= k[None, :, None], axis=2)
    is_owner = owner[:, :, None] == jnp.arange(N_EXPERTS, dtype=I32)[None, None, :]
    pick = lambda v: jnp.sum(jnp.where(is_owner, v[:, None, :], 0), axis=2)
    run_first = pick(offsets[None, :] + tab[:, 1])
    block_in_run = k[None, :] - pick(blocks_through - n_blocks)
    sorted_row = (run_first + block_in_run * MOVE_BLOCK) * TOKEN_ROWS
    block_tab = jnp.concatenate(
        [sorted_row, blocks_through[:, -1:],
         jnp.zeros((n_tiles, LANES - MAX_BLOCKS - 1), I32)], axis=1).astype(I32)[:, None, :]
    hs = _dispatch(zero_tiles, block_tab, slots, h2, n_rows)
    ys = _experts(tile_expert, tile_src, tile_rows, hs, w1, w3, w2, layer)
    return ys, (block_tab, slots, weights)


def _rope(x, cos, sin_signed):
    lane = lax.broadcasted_iota(I32, (x.shape[0], LANES), 1)
    low = (lane % 32) < 16
    outs = []
    for j in range(x.shape[1] // LANES):
        xb = x[:, j * LANES:(j + 1) * LANES]
        partner = jnp.where(low, pltpu.roll(xb, LANES - 16, 1), pltpu.roll(xb, 16, 1))
        outs.append(xb * cos + partner * sin_signed)
    return jnp.concatenate(outs, axis=1)


def _inproj1_prompt_kernel(x_ref, mod_ref, nw_ref, w_ref, q_ref, k_ref, v_ref, kc_ref, vc_ref):
    d = D_MODEL
    h = _modulate(x_ref[...], nw_ref[...], mod_ref[0, 0:1, :], mod_ref[0, 1:2, :]).astype(BF16)
    q_ref[...] = (_dot(h, w_ref[:, 0:d]) * (DIFF_HD ** -0.5)).astype(BF16)
    k = _dot(h, w_ref[:, d:2 * d])
    v = _dot(h, w_ref[:, 2 * d:3 * d])
    k_ref[...] = k.astype(BF16)
    v_ref[...] = v.astype(BF16)
    kc_ref[...] = k.T
    _to_token_major(vc_ref, v)


def _inproj1_sample_kernel(x_ref, mod_ref, nw_ref, w_ref, cos_ref, sin_ref, q_ref, k_ref, v_ref):
    d = D_MODEL
    h = _modulate(x_ref[...], nw_ref[...], mod_ref[0, 0:1, :], mod_ref[0, 1:2, :]).astype(BF16)
    cos, sin = cos_ref[...], sin_ref[...]
    q_ref[...] = (_rope(_dot(h, w_ref[:, 0:d]), cos, sin) * (DIFF_HD ** -0.5)).astype(BF16)
    k_ref[...] = _rope(_dot(h, w_ref[:, d:2 * d]), cos, sin).astype(BF16)
    v_ref[...] = _dot(h, w_ref[:, 2 * d:3 * d]).astype(BF16)


def _inproj1(x, mods, norm_w, w_bf16, n_prompt_tiles, n_sample_tiles, tiles_per_sample,
             cos_t, sin_t):
    d = D_MODEL
    npt, nst = n_prompt_tiles, n_sample_tiles
    common = [pl.BlockSpec((1, d), lambda i: (0, 0)), pl.BlockSpec((d, 3 * d), lambda i: (0, 0))]
    tile = lambda i: (i, 0)
    out_specs = tuple(pl.BlockSpec((ROW_TILE, d), tile) for _ in range(3))
    qp, kp, vp, k_cache, v_cache = pl.pallas_call(
        _inproj1_prompt_kernel,
        grid=(npt,),
        in_specs=[pl.BlockSpec((ROW_TILE, d), tile),
                  pl.BlockSpec((1, 6, d), lambda i: (8, 0, 0))] + common,
        out_specs=out_specs + (pl.BlockSpec((d, ROW_TILE), tile),
                               pl.BlockSpec((ROW_TILE * TOKEN_ROWS, LANES), tile)),
        out_shape=tuple(jax.ShapeDtypeStruct((npt * ROW_TILE, d), BF16) for _ in range(3))
        + (jax.ShapeDtypeStruct((npt * d, ROW_TILE), F32),
           jax.ShapeDtypeStruct((npt * ROW_TILE * TOKEN_ROWS, LANES), F32)),
        compiler_params=_params(1),
        name="inproj1_prompt",
    )(x, mods, norm_w, w_bf16)
    rope_tile = lambda i: (i % tiles_per_sample, 0)
    qs, ks, vs = pl.pallas_call(
        _inproj1_sample_kernel,
        grid=(nst,),
        in_specs=[pl.BlockSpec((ROW_TILE, d), lambda i: (npt + i, 0)),
                  pl.BlockSpec((1, 6, d), lambda i: (8 + 1 + i // tiles_per_sample, 0, 0))]
        + common + [pl.BlockSpec((ROW_TILE, LANES), rope_tile),
                    pl.BlockSpec((ROW_TILE, LANES), rope_tile)],
        out_specs=out_specs,
        out_shape=tuple(jax.ShapeDtypeStruct((nst * ROW_TILE, d), BF16) for _ in range(3)),
        compiler_params=_params(1),
        name="inproj1_sample",
    )(x, mods, norm_w, w_bf16, cos_t, sin_t)
    return (qp, kp, vp), (qs, ks, vs), (k_cache, v_cache)


def _rope_tables(n_tok):
    half = DIFF_HD // 4
    pos = np.arange(n_tok)
    lane = np.arange(LANES)
    sub = lane % DIFF_HD
    p = np.where(sub[None, :] < DIFF_HD // 2, (pos // GRID_W)[:, None], (pos % GRID_W)[:, None])
    inv = jnp.asarray(ROPE_THETA, F32) ** (-jnp.asarray(sub % half, F32) / half)
    ang = jnp.asarray(p, F32) * inv[None, :]
    sign = np.where((lane % (2 * half)) < half, -1.0, 1.0).astype(np.float32)
    return jnp.cos(ang), jnp.sin(ang) * sign[None, :]


def _diffattn_kernel(*refs, has_cache, lam_init):
    if has_cache:
        q_ref, k_ref, v_ref, ck_ref, cv_ref, lam_ref, sw_ref, o_ref = refs
    else:
        q_ref, k_ref, v_ref, lam_ref, sw_ref, o_ref = refs
    hd2 = 2 * DIFF_HD
    lv = lam_ref[...]
    lam = (jnp.exp(jnp.sum(lv[0:1] * lv[1:2], axis=1, keepdims=True))
           - jnp.exp(jnp.sum(lv[2:3] * lv[3:4], axis=1, keepdims=True)) + lam_init)
    lane = lax.broadcasted_iota(I32, (q_ref.shape[0], hd2), 1)
    for h in range(DIFF_HEADS):
        cols = slice(h * hd2, (h + 1) * hd2)
        q = q_ref[:, cols]
        zero = jnp.zeros_like(q)
        k_new = k_ref[:, cols].astype(BF16)
        values = [v_ref[:, cols].astype(BF16)]
        if has_cache:
            past = ck_ref.shape[1]
            k_past_t = ck_ref[cols, :].astype(BF16)
            values.append(cv_ref[pl.ds(h, past, stride=DIFF_HEADS), :].astype(BF16))
        o = None
        for c in range(2):
            qc = jnp.where((lane < DIFF_HD) == (c == 0), q, zero)
            s = [_dot_nt(qc, k_new)]
            if has_cache:
                s.append(_dot(qc, k_past_t))
            mx = functools.reduce(jnp.maximum, [jnp.max(si, axis=1, keepdims=True) for si in s])
            e = [jnp.exp(si - mx) for si in s]
            z = functools.reduce(jnp.add, [jnp.sum(ei, axis=1, keepdims=True) for ei in e])
            pv = functools.reduce(jnp.add, [_dot(ei.astype(BF16), v) for ei, v in zip(e, values)])
            pv = pv * (1.0 / z)
            o = pv if c == 0 else o - lam * pv
        o_ref[:, cols] = ((_rms(o) * sw_ref[...]) * (1.0 - lam_init)).astype(BF16)


def _diffattn(q, k, v, lam_vecs, subln_w, batch, seq_len, q_block, lam_init, cache=None):
    d = D_MODEL
    nq = seq_len // q_block
    has_cache = cache is not None
    kv_spec = pl.BlockSpec((seq_len, d), lambda b, qi: (b, 0))
    in_specs = [pl.BlockSpec((q_block, d), lambda b, qi: (b * nq + qi, 0)), kv_spec, kv_spec]
    args = [q, k, v]
    if has_cache:
        past = cache[0].shape[1]
        in_specs += [pl.BlockSpec((d, past), lambda b, qi: (b, 0)),
                     pl.BlockSpec((past * DIFF_HEADS, 2 * DIFF_HD), lambda b, qi: (b, 0))]
        args += list(cache)
    in_specs += [pl.BlockSpec((4, DIFF_HD), lambda b, qi: (0, 0)),
                 pl.BlockSpec((1, 2 * DIFF_HD), lambda b, qi: (0, 0))]
    args += [lam_vecs, subln_w]
    return pl.pallas_call(
        functools.partial(_diffattn_kernel, has_cache=has_cache, lam_init=lam_init),
        grid=(batch, nq),
        in_specs=in_specs,
        out_specs=pl.BlockSpec((q_block, d), lambda b, qi: (b * nq + qi, 0)),
        out_shape=jax.ShapeDtypeStruct((batch * seq_len, d), BF16),
        compiler_params=_params(2),
        name="diffattn_cache" if has_cache else "diffattn",
    )(*args)


def _router_weights(router_group, router_expert):
    w = jnp.concatenate([router_group, router_expert], axis=1)
    w = jnp.pad(w, ((0, 0), (0, LANES - w.shape[1])))
    hi = w.astype(BF16)
    return hi, (w - hi.astype(F32)).astype(BF16)


def _inproj0_weights(w_in):
    gq, gk, gv, gg, gaf, gab, hq, hff, hfb, hi, hg = jnp.split(
        w_in, [256, 512, 1024, 1536, 1552, 1568, 1824, 2080, 2336, 2848], axis=1)
    w = jnp.concatenate([gq, gk, gv, gg, hq, hff, hfb, hi, hg, gaf, gab], axis=1)
    return jnp.pad(w, ((0, 0), (0, AB_COLS - w.shape[1]))).astype(BF16)


def kernel(x_prompt, x_sample, state_gla, state_hgrn, cache_diff_k, cache_diff_v, c, c_ctx,
           w_ada, b_ada, norm1_w, norm2_w, w_in_ab, gla_a2, gla_a_bias, hgrn_lb, gla_onorm_w,
           hgrn_onorm_w, w_out_ab, w_in_c, lam_q1, lam_k1, lam_q2, lam_k2, diff_subln_w, w_out_c,
           router_group, router_expert, moe_w1, moe_w3, moe_w2, final_norm_w):
    bp, lp, d = x_prompt.shape
    bs, ls, _ = x_sample.shape
    depth = w_ada.shape[0]
    assert depth == 2 and d == D_MODEL and bs <= 7
    tp, ts = bp * lp, bs * ls
    npt, nst = tp // ROW_TILE, ts // ROW_TILE
    tps = ls // ROW_TILE
    xp = x_prompt.reshape(tp, d)
    xs = x_sample.reshape(ts, d)

    cond8 = jnp.concatenate([c_ctx[None, :], c, jnp.zeros((7 - bs, d), F32)], axis=0)
    mods = _adaln(cond8, w_ada, b_ada).reshape(depth * 8, 6, d)

    proj = _inproj0(xp, xs, mods, norm1_w[0:1], _inproj0_weights(w_in_ab[0]), ls)
    a_bias = gla_a_bias[0][:, None, :]
    scan_args = (gla_a2[0], a_bias, hgrn_lb, gla_onorm_w[0:1], hgrn_onorm_w[0:1])
    mixed_p, s_fin = _scan(proj, 0, bp, lp, *scan_args)
    s0 = jnp.concatenate([state_gla[:, 0], state_hgrn[:, 0]], axis=2).swapaxes(-1, -2)
    s0 = s0.reshape(bs, 2, SCAN_PAIRS, 2, HEAD_DV, HEAD_DK)
    zero = jnp.zeros_like(s0[:, :, :, 0])
    s0 = jnp.concatenate([jnp.concatenate([s0[:, :, :, 0], zero], axis=-1),
                          jnp.concatenate([zero, s0[:, :, :, 1]], axis=-1)], axis=-2)
    mixed_s = _scan(proj, tp, bs, ls, *scan_args, s0=s0)
    new_state_gla = s_fin[:, None, :, :GLA_HEADS]
    new_state_hgrn = s_fin[:, None, :, GLA_HEADS:]

    wr = _router_weights(router_group[0], router_expert[0])
    x1, *routed = _post((xp, xs), mixed_p, mixed_s, mods, 0, norm2_w[0:1],
                        w_out_ab[0].astype(BF16), *wr, tps)
    ys, tables = _moe(*routed, moe_w1, moe_w3, moe_w2, 0)
    x2 = _combine(*tables, x1, mods, 0, final_norm_w[None, :], ys, 0, npt + nst, npt, tps, False)

    lam_init = 0.8 - 0.6 * math.exp(-0.3 * 1)
    cos_t, sin_t = _rope_tables(ls)
    (qp, kp, vp), (qs, ks, vs), (k_cache, v_cache) = _inproj1(
        x2, mods, norm1_w[1:2], w_in_c[0].astype(BF16), npt, nst, tps, cos_t, sin_t)
    lam_vecs = jnp.stack([lam_q1[0], lam_k1[0], lam_q2[0], lam_k2[0]])
    att_p = _diffattn(qp, kp, vp, lam_vecs, diff_subln_w[0:1], bp, lp, lp, lam_init)
    past = cache_diff_k.shape[2]
    assert lp == ROW_TILE and DIFF_HEADS == TOKEN_ROWS
    cache = (cache_diff_k[:, 0].transpose(0, 2, 3, 4, 1).reshape(bs * d, past),
             cache_diff_v[:, 0].reshape(bs * past * DIFF_HEADS, 2 * DIFF_HD))
    att_s = _diffattn(qs, ks, vs, lam_vecs, diff_subln_w[0:1], bs, ls, ROW_TILE, lam_init, cache)

    wr = _router_weights(router_group[1], router_expert[1])
    x3, *routed = _post((x2,), att_p, att_s, mods, 1, norm2_w[1:2],
                        w_out_c[0].astype(BF16), *wr, tps)
    ys, tables = _moe(*routed, moe_w1, moe_w3, moe_w2, 1)
    fw = final_norm_w[None, :]
    y_p = _combine(*tables, x3, mods, 1, fw, ys, 0, npt, npt, tps, True)
    y_s = _combine(*tables, x3, mods, 1, fw, ys, npt, nst, npt, tps, True)

    return (y_p.reshape(bp, lp, d), y_s.reshape(bs, ls, d), new_state_gla, new_state_hgrn,
            k_cache.reshape(bp, 1, DIFF_HEADS, 2, DIFF_HD, lp).transpose(0, 1, 5, 2, 3, 4),
            v_cache.reshape(bp, 1, lp, DIFF_HEADS, 2 * DIFF_HD))
```

```python
import functools
import math

import jax
import jax.numpy as jnp
import numpy as np
from jax import lax
from jax.experimental import pallas as pl
from jax.experimental.pallas import tpu as pltpu

F32 = jnp.float32
BF16 = jnp.bfloat16
I32 = jnp.int32

D_MODEL = 1024
GLA_HEADS = 4
HGRN_HEADS = 4
SCAN_HEADS = GLA_HEADS + HGRN_HEADS
SCAN_PAIRS = SCAN_HEADS // 2
HEAD_DK = 64
HEAD_DV = 128
GATE_RANK = 16
GLA_GATE_NORM = 16.0
DIFF_HEADS = 8
DIFF_HD = 64
GRID_W = 64
ROPE_THETA = 10000.0
N_GROUPS = 4
EXPERTS_PER_GROUP = 8
N_EXPERTS = N_GROUPS * EXPERTS_PER_GROUP
MOE_HIDDEN = 512
EPS = 1e-6
LANES = 128
TOKEN_ROWS = D_MODEL // LANES
NEG_BIG = -1e30

ROW_TILE = 512
ATTN_TILE = 256
ADA_TILE = 1536
INPROJ0_TILE = 512
SCAN_CHUNK = 64
EXPERT_TILE = 256
MOVE_BLOCK = 16
STAGE_TOKENS = 2 * ROW_TILE + N_EXPERTS * MOVE_BLOCK
MAX_BLOCKS = STAGE_TOKENS // MOVE_BLOCK
VMEM_LIMIT = 56 * 1024 * 1024
SCAN_INPUT_DOUBLE_BUFFER_BYTES = 8 * 1024 * 1024

_C_GQ, _C_GK, _C_GV, _C_GG = 0, 256, 512, 1024
_C_HQ, _C_HFF, _C_HFB, _C_HI, _C_HG = 1536, 1792, 2048, 2304, 2816
_C_GAF, _C_GAB = 3328, 3344
AB_COLS = 3456


def _params(n_axes, vmem=VMEM_LIMIT):
    return pltpu.CompilerParams(dimension_semantics=("arbitrary",) * n_axes,
                                vmem_limit_bytes=vmem)


def _cdiv(a, b):
    return (a + b - 1) // b


def _dot(a, b):
    return jnp.dot(a, b, preferred_element_type=F32)


def _dot_nt(a, b):
    return lax.dot_general(a, b, (((1,), (1,)), ((), ())), preferred_element_type=F32)


def _dot_tn(a, b):
    return lax.dot_general(a, b, (((0,), (0,)), ((), ())), preferred_element_type=F32)


def _split_bf16(x):
    hi = x.astype(BF16)
    lo = (x - hi.astype(F32)).astype(BF16)
    return hi, lo


def _silu(x):
    return x * jax.nn.sigmoid(x)


def _log_sigmoid(x):
    return jnp.minimum(x, 0.0) - jnp.log(1.0 + jnp.exp(-jnp.abs(x)))


def _rms(x):
    return x * lax.rsqrt(jnp.mean(x * x, axis=-1, keepdims=True) + EPS)


def _modulate(x, norm_w, shift, scale):
    return (_rms(x) * norm_w) * (1.0 + scale) + shift


def _to_token_major(dst_ref, x, row0=0):
    n = x.shape[0]
    for s in range(TOKEN_ROWS):
        dst_ref[pl.ds(row0 + s, n, stride=TOKEN_ROWS), :] = x[:, s * LANES:(s + 1) * LANES]


def _from_token_major(src_ref, n, row0=0):
    return jnp.concatenate([src_ref[pl.ds(row0 + s, n, stride=TOKEN_ROWS), :]
                            for s in range(TOKEN_ROWS)], axis=1)


def _ada_kernel(c_ref, w_ref, b_ref, o_ref):
    s = _silu(c_ref[...])
    o_ref[0] = _dot(s.astype(BF16), w_ref[0].astype(BF16)) + b_ref[0]


def _adaln(cond8, w_ada, b_ada):
    depth, d, n = w_ada.shape
    tn = ADA_TILE
    return pl.pallas_call(
        _ada_kernel,
        grid=(depth, n // tn),
        in_specs=[pl.BlockSpec((8, d), lambda l, j: (0, 0)),
                  pl.BlockSpec((1, d, tn), lambda l, j: (l, 0, j)),
                  pl.BlockSpec((1, 1, tn), lambda l, j: (l, 0, j))],
        out_specs=pl.BlockSpec((1, 8, tn), lambda l, j: (l, 0, j)),
        out_shape=jax.ShapeDtypeStruct((depth, 8, n), F32),
        compiler_params=_params(2),
        name="adaln",
    )(cond8, w_ada, b_ada.reshape(depth, 1, n))


def _mod_row(i, layer, n_prompt_tiles, tiles_per_sample):
    r = jnp.where(i < n_prompt_tiles, 0, 1 + (i - n_prompt_tiles) // tiles_per_sample)
    return layer * 8 + r


def _inproj0_kernel(xp_ref, xs_ref, mod_ref, nw_ref, w_ref, o_ref, *, n_prompt_tiles):
    i = pl.program_id(0)
    x = jnp.where(i < n_prompt_tiles, xp_ref[...], xs_ref[...])
    h = _modulate(x, nw_ref[...], mod_ref[0, 0:1, :], mod_ref[0, 1:2, :])
    o_ref[...] = _dot(h.astype(BF16), w_ref[...])


def _inproj0(xp, xs, mods, norm_w, w_bf16, sample_len):
    tp, d = xp.shape
    ts = xs.shape[0]
    n = w_bf16.shape[1]
    tile = INPROJ0_TILE
    npt, nst = tp // tile, ts // tile
    mod_map = lambda i: (_mod_row(i, 0, npt, sample_len // tile), 0, 0)
    return pl.pallas_call(
        functools.partial(_inproj0_kernel, n_prompt_tiles=npt),
        grid=(npt + nst,),
        in_specs=[pl.BlockSpec((tile, d), lambda i: (jnp.minimum(i, npt - 1), 0)),
                  pl.BlockSpec((tile, d), lambda i: (jnp.maximum(i - npt, 0), 0)),
                  pl.BlockSpec((1, 6, d), mod_map),
                  pl.BlockSpec((1, d), lambda i: (0, 0)),
                  pl.BlockSpec((d, n), lambda i: (0, 0))],
        out_specs=pl.BlockSpec((tile, n), lambda i: (i, 0)),
        out_shape=jax.ShapeDtypeStruct((tp + ts, n), F32),
        compiler_params=_params(1),
        name="inproj0",
    )(xp, xs, mods, norm_w, w_bf16)


def _scan_kernel(*refs, seq_len, has_state):
    if has_state:
        (p_ref, a2_ref, ab_ref, lb_ref, ong_ref, onh_ref, s0_ref, mixed_ref, *scratch) = refs
        sfin_ref = None
    else:
        (p_ref, a2_ref, ab_ref, lb_ref, ong_ref, onh_ref, mixed_ref, sfin_ref, *scratch) = refs
        s0_ref = None
    (qi_f, ki_f, qo_f, ko_f, qi_b, ki_b, qo_b, ko_b,
     vv, dec_f, dec_b, o_f, o_b, st_f, st_b) = scratch
    C = SCAN_CHUNK
    n_chunks = seq_len // C
    gqk = GLA_HEADS * HEAD_DK

    row = lax.broadcasted_iota(I32, (C, C), 0)
    col = lax.broadcasted_iota(I32, (C, C), 1)
    lower = col <= row
    upper = col >= row
    tri_lo = jnp.where(lower, 1.0, 0.0).astype(BF16)
    tri_up = jnp.where(upper, 1.0, 0.0).astype(BF16)

    lbp = lb_ref[...]
    lb_max = jnp.maximum(lbp[0], lbp[1])
    lb_e0 = jnp.exp(lbp[0] - lb_max)
    lb_e1 = jnp.exp(lbp[1] - lb_max)
    lb = lb_e0 / (lb_e0 + lb_e1)

    def cumsum_chunk(tri, la):
        hi, lo = _split_bf16(la)
        return _dot(tri, hi) + _dot(tri, lo)

    def prep(n, carry):
        r0 = pl.multiple_of(n * C, C)
        rows = pl.ds(r0, C)
        gq = p_ref[rows, _C_GQ:_C_GQ + gqk] * (HEAD_DK ** -0.5)
        gk = p_ref[rows, _C_GK:_C_GK + gqk]
        hq = _silu(p_ref[rows, _C_HQ:_C_HQ + gqk]) * (HEAD_DK ** -0.5)
        for d_i, (qi_s, ki_s, qo_s, ko_s, dec_s, tri, last, mid) in enumerate(
                ((qi_f, ki_f, qo_f, ko_f, dec_f, tri_lo, C - 1, C // 2 - 1),
                 (qi_b, ki_b, qo_b, ko_b, dec_b, tri_up, 0, C // 2))):
            c_ga = _C_GAF if d_i == 0 else _C_GAB
            c_hf = _C_HFF if d_i == 0 else _C_HFB
            ga = p_ref[rows, c_ga:c_ga + GATE_RANK]
            xg = _dot(ga.astype(BF16), a2_ref[d_i].astype(BF16)) + ab_ref[d_i]
            la_g = _log_sigmoid(xg) / GLA_GATE_NORM
            f = lb[d_i:d_i + 1, :] + (1.0 - lb[d_i:d_i + 1, :]) * jax.nn.sigmoid(
                p_ref[rows, c_hf:c_hf + gqk])
            la_h = jnp.log(f)
            for q, k, la, c0 in ((gq, gk, la_g, 0), (hq, 1.0 - f, la_h, gqk)):
                b = cumsum_chunk(tri, la)
                b_mid, b_end = b[mid:mid + 1, :], b[last:last + 1, :]
                cs = slice(c0, c0 + gqk)
                qi_s[rows, cs] = (q * jnp.exp(b - b_mid)).astype(BF16)
                ki_s[rows, cs] = (k * jnp.exp(b_mid - b)).astype(BF16)
                qo_s[rows, cs] = (q * jnp.exp(b)).astype(BF16)
                ko_s[rows, cs] = (k * jnp.exp(b_end - b)).astype(BF16)
                dec_s[n, :, cs] = jnp.exp(b_end)
        gv_cols = GLA_HEADS * HEAD_DV
        vv[rows, 0:gv_cols] = p_ref[rows, _C_GV:_C_GV + gv_cols].astype(BF16)
        vv[rows, gv_cols:] = p_ref[rows, _C_HI:_C_HI + HGRN_HEADS * HEAD_DV].astype(BF16)
        return carry

    lax.fori_loop(0, n_chunks, prep, 0, unroll=2)

    for p in range(SCAN_PAIRS):
        if has_state:
            st_f[p] = s0_ref[0, 0, p]
            st_b[p] = s0_ref[0, 1, p]
        else:
            st_f[p] = jnp.zeros((2 * HEAD_DV, 2 * HEAD_DK), F32)
            st_b[p] = jnp.zeros((2 * HEAD_DV, 2 * HEAD_DK), F32)

    first_head = lax.broadcasted_iota(I32, (C, 2 * HEAD_DK), 1) < HEAD_DK
    row2 = lax.broadcasted_iota(I32, (2 * C, C), 0) % C
    col2 = lax.broadcasted_iota(I32, (2 * C, C), 1)
    lower2 = col2 <= row2
    upper2 = col2 >= row2

    def per_head_rows(x):
        z = jnp.zeros_like(x)
        return jnp.concatenate([jnp.where(first_head, x, z), jnp.where(first_head, z, x)], axis=0)

    def put_out(o_ref, rows, p, res):
        c0 = p * 2 * HEAD_DV
        o_ref[rows, c0:c0 + HEAD_DV] = res[0:C, 0:HEAD_DV]
        o_ref[rows, c0 + HEAD_DV:c0 + 2 * HEAD_DV] = res[C:2 * C, HEAD_DV:2 * HEAD_DV]

    def sweep(n, carry):
        m = n_chunks - 1 - n
        rows = pl.ds(pl.multiple_of(n * C, C), C)
        rows_m = pl.ds(pl.multiple_of(m * C, C), C)
        decay_f, decay_b = dec_f[n], dec_b[m]
        for p in range(SCAN_PAIRS):
            ks = slice(p * 2 * HEAD_DK, (p + 1) * 2 * HEAD_DK)
            vs = slice(p * 2 * HEAD_DV, (p + 1) * 2 * HEAD_DV)
            vh = vv[rows, vs]
            s_f = st_f[p]
            sc = (jnp.where(lower2, _dot_nt(per_head_rows(qi_f[rows, ks]), ki_f[rows, ks]), 0.0)
                  + jnp.where(upper2, _dot_nt(per_head_rows(qi_b[rows, ks]), ki_b[rows, ks]), 0.0))
            put_out(o_f, rows, p, _dot_nt(per_head_rows(qo_f[rows, ks]), s_f.astype(BF16))
                    + _dot(sc.astype(BF16), vh))
            st_f[p] = decay_f[:, ks] * s_f + _dot_tn(vh, ko_f[rows, ks])
            s_b = st_b[p]
            put_out(o_b, rows_m, p, _dot_nt(per_head_rows(qo_b[rows_m, ks]), s_b.astype(BF16)))
            st_b[p] = decay_b[:, ks] * s_b + _dot_tn(vv[rows_m, vs], ko_b[rows_m, ks])
        return carry

    lax.fori_loop(0, n_chunks, sweep, 0)

    def finish(n, carry):
        rows = pl.ds(pl.multiple_of(n * C, C), C)
        for h in range(SCAN_HEADS):
            vs = slice(h * HEAD_DV, (h + 1) * HEAD_DV)
            if h < GLA_HEADS:
                gate = p_ref[rows, _C_GG + h * HEAD_DV:_C_GG + (h + 1) * HEAD_DV]
                onw = ong_ref[...]
            else:
                hh = h - GLA_HEADS
                gate = p_ref[rows, _C_HG + hh * HEAD_DV:_C_HG + (hh + 1) * HEAD_DV]
                onw = onh_ref[...]
            o = o_f[rows, vs] + o_b[rows, vs]
            mixed_ref[rows, vs] = ((_rms(o) * onw) * _silu(gate)).astype(BF16)
        return carry

    lax.fori_loop(0, n_chunks, finish, 0, unroll=2)

    if sfin_ref is not None:
        for d_i, st in enumerate((st_f, st_b)):
            for p in range(SCAN_PAIRS):
                s_pair = st[p].T
                sfin_ref[0, d_i, 2 * p] = s_pair[0:HEAD_DK, 0:HEAD_DV]
                sfin_ref[0, d_i, 2 * p + 1] = s_pair[HEAD_DK:2 * HEAD_DK, HEAD_DV:2 * HEAD_DV]


def _scan(p, row0, batch, seq_len, a2, a_bias, lb, onorm_g, onorm_h, s0=None):
    n = p.shape[1]
    assert row0 % seq_len == 0
    blk0 = row0 // seq_len
    has_state = s0 is not None
    n_chunks = seq_len // SCAN_CHUNK
    st_shape = (1, 2, SCAN_HEADS, HEAD_DK, HEAD_DV)
    pair_shape = (SCAN_PAIRS, 2 * HEAD_DV, 2 * HEAD_DK)
    p_mode = dict(pipeline_mode=pl.Buffered(1)) if seq_len * n * 4 > SCAN_INPUT_DOUBLE_BUFFER_BYTES else {}
    in_specs = [pl.BlockSpec((seq_len, n), lambda b: (blk0 + b, 0), **p_mode),
                pl.BlockSpec(a2.shape, lambda b: (0, 0, 0)),
                pl.BlockSpec(a_bias.shape, lambda b: (0, 0, 0)),
                pl.BlockSpec(lb.shape, lambda b: (0, 0, 0)),
                pl.BlockSpec((1, HEAD_DV), lambda b: (0, 0)),
                pl.BlockSpec((1, HEAD_DV), lambda b: (0, 0))]
    args = [p, a2, a_bias, lb, onorm_g, onorm_h]
    mixed_shape = jax.ShapeDtypeStruct((batch * seq_len, D_MODEL), BF16)
    mixed_spec = pl.BlockSpec((seq_len, D_MODEL), lambda b: (b, 0))
    if has_state:
        in_specs.append(pl.BlockSpec((1, 2) + pair_shape, lambda b: (b, 0, 0, 0, 0)))
        args.append(s0)
        out_shape, out_specs = mixed_shape, mixed_spec
    else:
        out_shape = (mixed_shape, jax.ShapeDtypeStruct((batch,) + st_shape[1:], F32))
        out_specs = (mixed_spec, pl.BlockSpec(st_shape, lambda b: (b, 0, 0, 0, 0)))
    qk_cols = SCAN_HEADS * HEAD_DK
    scratch = [pltpu.VMEM((seq_len, qk_cols), BF16) for _ in range(8)]
    scratch += [pltpu.VMEM((seq_len, D_MODEL), BF16),
                pltpu.VMEM((n_chunks, 1, qk_cols), F32), pltpu.VMEM((n_chunks, 1, qk_cols), F32),
                pltpu.VMEM((seq_len, D_MODEL), F32), pltpu.VMEM((seq_len, D_MODEL), F32),
                pltpu.VMEM(pair_shape, F32), pltpu.VMEM(pair_shape, F32)]
    return pl.pallas_call(
        functools.partial(_scan_kernel, seq_len=seq_len, has_state=has_state),
        grid=(batch,),
        in_specs=in_specs, out_specs=out_specs, out_shape=out_shape,
        scratch_shapes=scratch,
        compiler_params=_params(1),
        name="scan_state" if has_state else "scan_fresh",
    )(*args)


def _post_kernel(*refs, split_x, n_prompt_tiles):
    if split_x:
        xp_ref, xs_ref = refs[0], refs[1]
        refs = refs[2:]
    else:
        x_ref = refs[0]
        refs = refs[1:]
    (mp_ref, ms_ref, mod_ref, nw_ref, wo_ref, wrh_ref, wrl_ref,
     x1_ref, h2_ref, slot_ref, wgt_ref, tab_ref, carry) = refs
    i = pl.program_id(0)
    is_prompt = i < n_prompt_tiles
    if split_x:
        x = jnp.where(is_prompt, xp_ref[...], xs_ref[...])
    else:
        x = x_ref[...]
    mixed = jnp.where(is_prompt, mp_ref[...], ms_ref[...])
    x1 = x + mod_ref[0, 2:3, :] * _dot(mixed, wo_ref[...])
    x1_ref[...] = x1
    h2 = _modulate(x1, nw_ref[...], mod_ref[0, 3:4, :], mod_ref[0, 4:5, :])
    _to_token_major(h2_ref, h2)

    hh, hl = _split_bf16(h2)
    logits = _dot(hh, wrh_ref[...]) + _dot(hl, wrh_ref[...]) + _dot(hh, wrl_ref[...])
    tm = logits.shape[0]
    lane = lax.broadcasted_iota(I32, (tm, LANES), 1).astype(F32)

    def first_max(v):
        mx = jnp.max(v, axis=1, keepdims=True)
        idx = jnp.min(jnp.where(v == mx, lane, float(LANES)), axis=1, keepdims=True)
        return mx, idx

    gl = jnp.where(lane < N_GROUPS, logits, NEG_BIG)
    gmax, gidx = first_max(gl)
    g_val = 1.0 / jnp.sum(jnp.exp(gl - gmax), axis=1, keepdims=True)
    lo = N_GROUPS + EXPERTS_PER_GROUP * gidx
    el = jnp.where((lane >= lo) & (lane < lo + EXPERTS_PER_GROUP), logits, NEG_BIG)
    emax, l1 = first_max(el)
    esum = jnp.sum(jnp.exp(el - emax), axis=1, keepdims=True)
    e2max, l2 = first_max(jnp.where(lane == l1, NEG_BIG, el))
    p1 = 1.0 / esum
    p2 = jnp.exp(e2max - emax) / esum
    w1 = g_val * (p1 / (p1 + p2))
    w2 = g_val * (p2 / (p1 + p2))
    id1 = l1 - N_GROUPS
    id2 = l2 - N_GROUPS

    @pl.when(i == 0)
    def _():
        carry[...] = jnp.zeros_like(carry)

    sel1 = lane == id1
    sel2 = lane == id2
    onehot = jnp.where(sel1 | sel2, 1.0, 0.0)
    row = lax.broadcasted_iota(I32, (tm, tm), 0)
    col = lax.broadcasted_iota(I32, (tm, tm), 1)
    earlier = jnp.where(col < row, 1.0, 0.0).astype(BF16)
    before = _dot(earlier, onehot.astype(BF16))
    count = jnp.sum(onehot, axis=0, keepdims=True)
    blocks = jnp.floor((count + (MOVE_BLOCK - 1.0)) * (1.0 / MOVE_BLOCK)) * MOVE_BLOCK
    e_row = lax.broadcasted_iota(I32, (LANES, LANES), 0)
    e_col = lax.broadcasted_iota(I32, (LANES, LANES), 1)
    lower_experts = jnp.where(e_row < e_col, 1.0, 0.0).astype(BF16)
    run_start = _dot(jnp.broadcast_to(blocks, (8, LANES)).astype(BF16), lower_experts)[0:1]
    slot = before + run_start
    q1 = jnp.sum(jnp.where(sel1, slot, 0.0), axis=1, keepdims=True)
    q2 = jnp.sum(jnp.where(sel2, slot, 0.0), axis=1, keepdims=True)
    tab_row = lax.broadcasted_iota(I32, (8, LANES), 0)
    tab_ref[0] = jnp.where(tab_row == 0, count,
                           jnp.where(tab_row == 1, carry[...],
                                     jnp.where(tab_row == 2, run_start, 0.0)))
    carry[...] = carry[...] + count

    cols = jnp.zeros((tm, LANES), F32)
    for k, v in enumerate((q1 * TOKEN_ROWS, q2 * TOKEN_ROWS, w1, w2)):
        cols = jnp.where(lane == k, v, cols)
    rows = cols.T
    slot_ref[0] = rows[0:2].astype(I32)
    wgt_ref[0] = rows[2:4]


def _post(x_args, mixed_p, mixed_s, mods, layer, norm_w, w_out_bf16, wr_hi, wr_lo,
          tiles_per_sample):
    split_x = len(x_args) == 2
    tp, ts = mixed_p.shape[0], mixed_s.shape[0]
    t, d = tp + ts, D_MODEL
    npt, nst = tp // ROW_TILE, ts // ROW_TILE
    tile = lambda i: (i, 0)
    if split_x:
        x_specs = [pl.BlockSpec((ROW_TILE, d), lambda i: (jnp.minimum(i, npt - 1), 0)),
                   pl.BlockSpec((ROW_TILE, d), lambda i: (jnp.maximum(i - npt, 0), 0))]
    else:
        x_specs = [pl.BlockSpec((ROW_TILE, d), tile)]
    in_specs = x_specs + [
        pl.BlockSpec((ROW_TILE, d), lambda i: (jnp.minimum(i, npt - 1), 0)),
        pl.BlockSpec((ROW_TILE, d), lambda i: (jnp.maximum(i - npt, 0), 0)),
        pl.BlockSpec((1, 6, d), lambda i: (_mod_row(i, layer, npt, tiles_per_sample), 0, 0)),
        pl.BlockSpec((1, d), lambda i: (0, 0)),
        pl.BlockSpec((d, d), lambda i: (0, 0)),
        pl.BlockSpec((d, LANES), lambda i: (0, 0)),
        pl.BlockSpec((d, LANES), lambda i: (0, 0))]
    return pl.pallas_call(
        functools.partial(_post_kernel, split_x=split_x, n_prompt_tiles=npt),
        grid=(npt + nst,),
        in_specs=in_specs,
        out_specs=(pl.BlockSpec((ROW_TILE, d), tile),
                   pl.BlockSpec((ROW_TILE * TOKEN_ROWS, LANES), tile),
                   pl.BlockSpec((1, 2, ROW_TILE), lambda i: (i, 0, 0)),
                   pl.BlockSpec((1, 2, ROW_TILE), lambda i: (i, 0, 0)),
                   pl.BlockSpec((1, 8, LANES), lambda i: (i, 0, 0))),
        out_shape=(jax.ShapeDtypeStruct((t, d), F32),
                   jax.ShapeDtypeStruct((t * TOKEN_ROWS, LANES), F32),
                   jax.ShapeDtypeStruct((npt + nst, 2, ROW_TILE), I32),
                   jax.ShapeDtypeStruct((npt + nst, 2, ROW_TILE), F32),
                   jax.ShapeDtypeStruct((npt + nst, 8, LANES), F32)),
        scratch_shapes=[pltpu.VMEM((1, LANES), F32)],
        compiler_params=_params(1),
        name=f"post{layer}",
    )(*x_args, mixed_p, mixed_s, mods, norm_w, w_out_bf16, wr_hi, wr_lo)


def _for_blocks(tab_ref, fn):
    block_rows = MOVE_BLOCK * TOKEN_ROWS
    count = tab_ref[0, 0, MAX_BLOCKS]

    def call(k, parity):
        fn(pl.multiple_of(k * block_rows, block_rows),
           pl.multiple_of(tab_ref[0, 0, k], TOKEN_ROWS), parity)

    def body(k2, c):
        call(2 * k2, 0)

        @pl.when(2 * k2 + 1 < count)
        def _():
            call(2 * k2 + 1, 1)
        return c

    lax.fori_loop(0, _cdiv(count, 2), body, 0)


def _wait_blocks(tab_ref, copy):
    def body(k, c):
        copy.wait()
        return c

    lax.fori_loop(0, tab_ref[0, 0, MAX_BLOCKS], body, 0)


def _dispatch_kernel(zero_ref, tab_ref, prev_tab_ref, q_ref, h2_ref, hs_ref, zero_buf, stage, sem):
    j = pl.program_id(0)
    slot = j % 2
    block_rows = MOVE_BLOCK * TOKEN_ROWS

    @pl.when(j == 0)
    def _():
        zero_buf[...] = jnp.zeros_like(zero_buf)

        def zero_copy(k):
            start = pl.multiple_of(zero_ref[k], EXPERT_TILE * TOKEN_ROWS)
            return pltpu.make_async_copy(
                zero_buf, hs_ref.at[pl.ds(start, EXPERT_TILE * TOKEN_ROWS)], sem.at[0])

        def start_zero(k, c):
            @pl.when(zero_ref[k] >= 0)
            def _():
                zero_copy(k).start()
            return c

        def wait_zero(k, c):
            @pl.when(zero_ref[k] >= 0)
            def _():
                zero_copy(k).wait()
            return c

        lax.fori_loop(0, zero_ref.shape[0], start_zero, 0)
        lax.fori_loop(0, zero_ref.shape[0], wait_zero, 0)

        stage[...] = jnp.zeros_like(stage)

    def place(r, c):
        tok = h2_ref[pl.ds(pl.multiple_of(r * TOKEN_ROWS, TOKEN_ROWS), TOKEN_ROWS), :]
        for s in range(2):
            row = pl.multiple_of(q_ref[0, s, r], TOKEN_ROWS)
            stage[slot, pl.ds(row, TOKEN_ROWS), :] = tok
        return c

    lax.fori_loop(0, ROW_TILE, place, 0, unroll=8)

    def block_copy(buf, stage_row, sorted_row):
        return pltpu.make_async_copy(stage.at[buf, pl.ds(stage_row, block_rows)],
                                     hs_ref.at[pl.ds(sorted_row, block_rows)], sem.at[buf])

    @pl.when(j > 0)
    def _():
        _wait_blocks(prev_tab_ref, block_copy(1 - slot, 0, 0))

    _for_blocks(tab_ref, lambda a, b, parity: block_copy(slot, a, b).start(priority=parity))

    @pl.when(j == pl.num_programs(0) - 1)
    def _():
        _wait_blocks(tab_ref, block_copy(slot, 0, 0))


def _dispatch(zero_tiles, block_tab, slots, h2, n_rows):
    t = h2.shape[0] // TOKEN_ROWS
    nt = t // ROW_TILE
    smem_tile = lambda shape: pl.BlockSpec((1,) + shape, lambda j, *_: (j, 0, 0),
                                           memory_space=pltpu.SMEM)
    grid_spec = pltpu.PrefetchScalarGridSpec(
        num_scalar_prefetch=1,
        grid=(nt,),
        in_specs=[smem_tile((1, LANES)),
                  pl.BlockSpec((1, 1, LANES), lambda j, *_: (jnp.maximum(j - 1, 0), 0, 0),
                               memory_space=pltpu.SMEM),
                  smem_tile((2, ROW_TILE)),
                  pl.BlockSpec((ROW_TILE * TOKEN_ROWS, LANES), lambda j, *_: (j, 0))],
        out_specs=pl.BlockSpec(memory_space=pl.ANY),
        scratch_shapes=[pltpu.VMEM((EXPERT_TILE * TOKEN_ROWS, LANES), F32),
                        pltpu.VMEM((2, STAGE_TOKENS * TOKEN_ROWS, LANES), F32),
                        pltpu.SemaphoreType.DMA((2,))])
    return pl.pallas_call(
        _dispatch_kernel,
        grid_spec=grid_spec,
        out_shape=jax.ShapeDtypeStruct((n_rows * TOKEN_ROWS, LANES), F32),
        compiler_params=_params(1),
        name="dispatch",
    )(zero_tiles, block_tab, block_tab, slots, h2)


def _expert_kernel(te_ref, src_ref, nv_ref, run_ref, nxt_ref, hs_ref, w1_hbm, w3_hbm, w2_hbm,
                   ys_ref, w1f, w3f, w2f, w1b, w3b, w2b, sem, *, layer):
    i = pl.program_id(0)

    def weight_copies(e, buf):
        return [pltpu.make_async_copy(src.at[layer, e], dst.at[buf], sem.at[buf])
                for src, dst in ((w1_hbm, w1f), (w3_hbm, w3f), (w2_hbm, w2f))]

    @pl.when(i == 0)
    def _():
        for c in weight_copies(te_ref[0], 0):
            c.start()

    first = (i == 0) | (run_ref[i] != run_ref[jnp.maximum(i - 1, 0)])

    @pl.when(first)
    def _():
        buf = run_ref[i] % 2
        for c in weight_copies(te_ref[i], buf):
            c.wait()

        @pl.when(nxt_ref[i] >= 0)
        def _():
            for c in weight_copies(nxt_ref[i], 1 - buf):
                c.start()

        w1b[...] = w1f[buf].astype(BF16)
        w3b[...] = w3f[buf].astype(BF16)
        w2b[...] = w2f[buf].astype(BF16)

    @pl.when(nv_ref[i] > 0)
    def _():
        h = _from_token_major(hs_ref, EXPERT_TILE).astype(BF16)
        g = _silu(_dot(h, w1b[...])) * _dot(h, w3b[...])
        _to_token_major(ys_ref, _dot(g.astype(BF16), w2b[...]))

    @pl.when(nv_ref[i] == 0)
    def _():
        ys_ref[...] = jnp.zeros_like(ys_ref)


def _experts(tile_expert, tile_src, tile_rows, hs, w1, w3, w2, layer):
    n_rows, d = hs.shape[0] // TOKEN_ROWS, D_MODEL
    nt = n_rows // EXPERT_TILE
    hid = w1.shape[-1]
    tok_tile = (EXPERT_TILE * TOKEN_ROWS, LANES)
    changed = jnp.concatenate([jnp.zeros((1,), I32),
                               (tile_expert[1:] != tile_expert[:-1]).astype(I32)])
    run = jnp.cumsum(changed).astype(I32)
    later = jnp.where(run[None, :] > run[:, None], tile_expert[None, :], N_EXPERTS)
    next_expert = jnp.min(later, axis=1)
    next_expert = jnp.where(next_expert < N_EXPERTS, next_expert, -1).astype(I32)
    grid_spec = pltpu.PrefetchScalarGridSpec(
        num_scalar_prefetch=5,
        grid=(nt,),
        in_specs=[pl.BlockSpec(tok_tile, lambda i, te, src, nv, run, nxt: (src[i], 0)),
                  pl.BlockSpec(memory_space=pl.ANY), pl.BlockSpec(memory_space=pl.ANY),
                  pl.BlockSpec(memory_space=pl.ANY)],
        out_specs=pl.BlockSpec(tok_tile, lambda i, te, src, nv, run, nxt: (i, 0)),
        scratch_shapes=[pltpu.VMEM((2, d, hid), F32), pltpu.VMEM((2, d, hid), F32),
                        pltpu.VMEM((2, hid, d), F32),
                        pltpu.VMEM((d, hid), BF16), pltpu.VMEM((d, hid), BF16),
                        pltpu.VMEM((hid, d), BF16), pltpu.SemaphoreType.DMA((2,))])
    return pl.pallas_call(
        functools.partial(_expert_kernel, layer=layer),
        grid_spec=grid_spec,
        out_shape=jax.ShapeDtypeStruct(hs.shape, F32),
        compiler_params=_params(1),
        name=f"experts{layer}",
    )(tile_expert, tile_src, tile_rows, run, next_expert, hs, w1, w3, w2)


def _combine_kernel(tab_ref, next_tab_ref, q_ref, w_ref, x1_ref, mod_ref, fw_ref, ys_ref, out_ref,
                    stage, y_tok, sem, *, final_norm):
    i = pl.program_id(0)
    slot = i % 2
    block_rows = MOVE_BLOCK * TOKEN_ROWS

    def block_copy(buf, stage_row, sorted_row):
        return pltpu.make_async_copy(ys_ref.at[pl.ds(sorted_row, block_rows)],
                                     stage.at[buf, pl.ds(stage_row, block_rows)], sem.at[buf])

    def fetch(tab, buf):
        _for_blocks(tab, lambda a, b, parity: block_copy(buf, a, b).start(priority=parity))

    @pl.when(i == 0)
    def _():
        fetch(tab_ref, slot)

    @pl.when(i + 1 < pl.num_programs(0))
    def _():
        fetch(next_tab_ref, 1 - slot)

    _wait_blocks(tab_ref, block_copy(slot, 0, 0))

    def pick(r, c):
        rows = [stage[slot, pl.ds(pl.multiple_of(q_ref[0, s, r], TOKEN_ROWS), TOKEN_ROWS), :]
                for s in range(2)]
        y_tok[pl.ds(pl.multiple_of(r * TOKEN_ROWS, TOKEN_ROWS), TOKEN_ROWS), :] = (
            w_ref[0, 0, r] * rows[0] + w_ref[0, 1, r] * rows[1])
        return c

    lax.fori_loop(0, ROW_TILE, pick, 0, unroll=8)
    x2 = x1_ref[...] + mod_ref[0, 5:6, :] * _from_token_major(y_tok, ROW_TILE)
    if final_norm:
        x2 = _rms(x2) * fw_ref[...]
    out_ref[...] = x2


def _combine(block_tab, slots, weights, x1, mods, layer, final_w, ys, tile0, n_tiles,
             n_prompt_tiles, tiles_per_sample, final_norm):
    d = D_MODEL
    tile = lambda i: (tile0 + i, 0)
    mod_map = lambda i: (_mod_row(tile0 + i, layer, n_prompt_tiles, tiles_per_sample), 0, 0)
    smem_tile = lambda shape: pl.BlockSpec((1,) + shape, lambda i: (tile0 + i, 0, 0),
                                           memory_space=pltpu.SMEM)
    return pl.pallas_call(
        functools.partial(_combine_kernel, final_norm=final_norm),
        grid=(n_tiles,),
        in_specs=[smem_tile((1, LANES)),
                  pl.BlockSpec((1, 1, LANES),
                               lambda i: (tile0 + jnp.minimum(i + 1, n_tiles - 1), 0, 0),
                               memory_space=pltpu.SMEM),
                  smem_tile((2, ROW_TILE)), smem_tile((2, ROW_TILE)),
                  pl.BlockSpec((ROW_TILE, d), tile),
                  pl.BlockSpec((1, 6, d), mod_map),
                  pl.BlockSpec((1, d), lambda i: (0, 0)),
                  pl.BlockSpec(memory_space=pl.ANY)],
        out_specs=pl.BlockSpec((ROW_TILE, d), lambda i: (i, 0)),
        out_shape=jax.ShapeDtypeStruct((n_tiles * ROW_TILE, d), F32),
        scratch_shapes=[pltpu.VMEM((2, STAGE_TOKENS * TOKEN_ROWS, LANES), F32),
                        pltpu.VMEM((ROW_TILE * TOKEN_ROWS, LANES), F32),
                        pltpu.SemaphoreType.DMA((2,))],
        compiler_params=_params(1),
        name=f"combine{layer}_{tile0}",
    )(block_tab, block_tab, slots, weights, x1, mods, final_w, ys)


def _moe(h2, slots, weights, tile_tab, w1, w3, w2, layer):
    t = h2.shape[0] // TOKEN_ROWS
    n_tiles = t // ROW_TILE
    extra_tiles = N_EXPERTS + _cdiv(N_EXPERTS * MOVE_BLOCK, EXPERT_TILE)
    n_rows = 2 * t + extra_tiles * EXPERT_TILE
    nt = n_rows // EXPERT_TILE
    tab = tile_tab[:, :, :N_EXPERTS].astype(I32)
    cnt = tab[-1, 0] + tab[-1, 1]
    tight = _cdiv(cnt, EXPERT_TILE) * EXPERT_TILE
    padded = jnp.where(cnt > 0, _cdiv(cnt + MOVE_BLOCK - 1, EXPERT_TILE) * EXPERT_TILE, 0)
    ends = jnp.cumsum(padded)
    offsets = ends - padded
    tails = jnp.where(cnt > 0, ends - EXPERT_TILE, -1)
    tails2 = jnp.where(padded > tight, ends - 2 * EXPERT_TILE, -1)
    used = ends[-1] // EXPERT_TILE
    tile_start = jnp.arange(nt, dtype=I32) * EXPERT_TILE
    unused = (used + jnp.arange(extra_tiles, dtype=I32)) * EXPERT_TILE
    zero_tiles = jnp.concatenate([tails, tails2, jnp.where(unused < n_rows, unused, -1)])
    zero_tiles = jnp.where(zero_tiles >= 0, zero_tiles * TOKEN_ROWS, -1).astype(I32)
    tile_src = jnp.minimum(jnp.arange(nt, dtype=I32), used - 1)
    tile_expert = jnp.sum((tile_src * EXPERT_TILE)[:, None] >= ends[None, :], axis=1).astype(I32)
    tile_rows = jnp.where(tile_start < ends[-1],
                          jnp.clip(cnt[tile_expert] - (tile_start - offsets[tile_expert]),
                                   0, EXPERT_TILE), 0).astype(I32)
    n_blocks = _cdiv(tab[:, 0], MOVE_BLOCK)
    blocks_through = jnp.cumsum(n_blocks, axis=1)
    k = jnp.arange(MAX_BLOCKS, dtype=I32)
    owner = jnp.sum(blocks_through[:, None, :] <= k[None, :, None], axis=2)
    is_owner = owner[:, :, None] == jnp.arange(N_EXPERTS, dtype=I32)[None, None, :]
    pick = lambda v: jnp.sum(jnp.where(is_owner, v[:, None, :], 0), axis=2)
    run_first = pick(offsets[None, :] + tab[:, 1])
    block_in_run = k[None, :] - pick(blocks_through - n_blocks)
    sorted_row = (run_first + block_in_run * MOVE_BLOCK) * TOKEN_ROWS
    block_tab = jnp.concatenate(
        [sorted_row, blocks_through[:, -1:],
         jnp.zeros((n_tiles, LANES - MAX_BLOCKS - 1), I32)], axis=1).astype(I32)[:, None, :]
    hs = _dispatch(zero_tiles, block_tab, slots, h2, n_rows)
    ys = _experts(tile_expert, tile_src, tile_rows, hs, w1, w3, w2, layer)
    return ys, (block_tab, slots, weights)


def _rope(x, cos, sin_signed):
    lane = lax.broadcasted_iota(I32, (x.shape[0], LANES), 1)
    low = (lane % 32) < 16
    outs = []
    for j in range(x.shape[1] // LANES):
        xb = x[:, j * LANES:(j + 1) * LANES]
        partner = jnp.where(low, pltpu.roll(xb, LANES - 16, 1), pltpu.roll(xb, 16, 1))
        outs.append(xb * cos + partner * sin_signed)
    return jnp.concatenate(outs, axis=1)


def _inproj1_prompt_kernel(x_ref, mod_ref, nw_ref, w_ref, q_ref, k_ref, v_ref, kc_ref, vc_ref):
    d = D_MODEL
    h = _modulate(x_ref[...], nw_ref[...], mod_ref[0, 0:1, :], mod_ref[0, 1:2, :]).astype(BF16)
    q_ref[...] = (_dot(h, w_ref[:, 0:d]) * (DIFF_HD ** -0.5)).astype(BF16)
    k = _dot(h, w_ref[:, d:2 * d])
    v = _dot(h, w_ref[:, 2 * d:3 * d])
    k_ref[...] = k.astype(BF16)
    v_ref[...] = v.astype(BF16)
    kc_ref[...] = k.T
    _to_token_major(vc_ref, v)


def _inproj1_sample_kernel(x_ref, mod_ref, nw_ref, w_ref, cos_ref, sin_ref, q_ref, k_ref, v_ref):
    d = D_MODEL
    h = _modulate(x_ref[...], nw_ref[...], mod_ref[0, 0:1, :], mod_ref[0, 1:2, :]).astype(BF16)
    cos, sin = cos_ref[...], sin_ref[...]
    q_ref[...] = (_rope(_dot(h, w_ref[:, 0:d]), cos, sin) * (DIFF_HD ** -0.5)).astype(BF16)
    k_ref[...] = _rope(_dot(h, w_ref[:, d:2 * d]), cos, sin).astype(BF16)
    v_ref[...] = _dot(h, w_ref[:, 2 * d:3 * d]).astype(BF16)


def _inproj1(x, mods, norm_w, w_bf16, n_prompt_tiles, n_sample_tiles, tiles_per_sample,
             cos_t, sin_t):
    d = D_MODEL
    npt, nst = n_prompt_tiles, n_sample_tiles
    common = [pl.BlockSpec((1, d), lambda i: (0, 0)), pl.BlockSpec((d, 3 * d), lambda i: (0, 0))]
    tile = lambda i: (i, 0)
    out_specs = tuple(pl.BlockSpec((ATTN_TILE, d), tile) for _ in range(3))
    qp, kp, vp, k_cache, v_cache = pl.pallas_call(
        _inproj1_prompt_kernel,
        grid=(npt,),
        in_specs=[pl.BlockSpec((ATTN_TILE, d), tile),
                  pl.BlockSpec((1, 6, d), lambda i: (8, 0, 0))] + common,
        out_specs=out_specs + (pl.BlockSpec((d, ATTN_TILE), tile),
                               pl.BlockSpec((ATTN_TILE * TOKEN_ROWS, LANES), tile)),
        out_shape=tuple(jax.ShapeDtypeStruct((npt * ATTN_TILE, d), BF16) for _ in range(3))
        + (jax.ShapeDtypeStruct((npt * d, ATTN_TILE), F32),
           jax.ShapeDtypeStruct((npt * ATTN_TILE * TOKEN_ROWS, LANES), F32)),
        compiler_params=_params(1),
        name="inproj1_prompt",
    )(x, mods, norm_w, w_bf16)
    rope_tile = lambda i: (i % tiles_per_sample, 0)
    qs, ks, vs = pl.pallas_call(
        _inproj1_sample_kernel,
        grid=(nst,),
        in_specs=[pl.BlockSpec((ATTN_TILE, d), lambda i: (npt + i, 0)),
                  pl.BlockSpec((1, 6, d), lambda i: (8 + 1 + i // tiles_per_sample, 0, 0))]
        + common + [pl.BlockSpec((ATTN_TILE, LANES), rope_tile),
                    pl.BlockSpec((ATTN_TILE, LANES), rope_tile)],
        out_specs=out_specs,
        out_shape=tuple(jax.ShapeDtypeStruct((nst * ATTN_TILE, d), BF16) for _ in range(3)),
        compiler_params=_params(1),
        name="inproj1_sample",
    )(x, mods, norm_w, w_bf16, cos_t, sin_t)
    return (qp, kp, vp), (qs, ks, vs), (k_cache, v_cache)


def _rope_tables(n_tok):
    half = DIFF_HD // 4
    pos = np.arange(n_tok)
    lane = np.arange(LANES)
    sub = lane % DIFF_HD
    p = np.where(sub[None, :] < DIFF_HD // 2, (pos // GRID_W)[:, None], (pos % GRID_W)[:, None])
    inv = jnp.asarray(ROPE_THETA, F32) ** (-jnp.asarray(sub % half, F32) / half)
    ang = jnp.asarray(p, F32) * inv[None, :]
    sign = np.where((lane % (2 * half)) < half, -1.0, 1.0).astype(np.float32)
    return jnp.cos(ang), jnp.sin(ang) * sign[None, :]


def _diffattn_kernel(*refs, has_cache, lam_init):
    if has_cache:
        q_ref, k_ref, v_ref, ck_ref, cv_ref, lam_ref, sw_ref, o_ref = refs
    else:
        q_ref, k_ref, v_ref, lam_ref, sw_ref, o_ref = refs
    hd2 = 2 * DIFF_HD
    lv = lam_ref[...]
    lam = (jnp.exp(jnp.sum(lv[0:1] * lv[1:2], axis=1, keepdims=True))
           - jnp.exp(jnp.sum(lv[2:3] * lv[3:4], axis=1, keepdims=True)) + lam_init)
    lane = lax.broadcasted_iota(I32, (q_ref.shape[0], hd2), 1)
    for h in range(DIFF_HEADS):
        cols = slice(h * hd2, (h + 1) * hd2)
        q = q_ref[:, cols]
        zero = jnp.zeros_like(q)
        k_new = k_ref[:, cols].astype(BF16)
        values = [v_ref[:, cols].astype(BF16)]
        if has_cache:
            past = ck_ref.shape[1]
            k_past_t = ck_ref[cols, :].astype(BF16)
            values.append(cv_ref[pl.ds(h, past, stride=DIFF_HEADS), :].astype(BF16))
        o = None
        for c in range(2):
            qc = jnp.where((lane < DIFF_HD) == (c == 0), q, zero)
            s = [_dot_nt(qc, k_new)]
            if has_cache:
                s.append(_dot(qc, k_past_t))
            mx = functools.reduce(jnp.maximum, [jnp.max(si, axis=1, keepdims=True) for si in s])
            e = [jnp.exp(si - mx) for si in s]
            z = functools.reduce(jnp.add, [jnp.sum(ei, axis=1, keepdims=True) for ei in e])
            pv = functools.reduce(jnp.add, [_dot(ei.astype(BF16), v) for ei, v in zip(e, values)])
            pv = pv * (1.0 / z)
            o = pv if c == 0 else o - lam * pv
        o_ref[:, cols] = ((_rms(o) * sw_ref[...]) * (1.0 - lam_init)).astype(BF16)


def _diffattn(q, k, v, lam_vecs, subln_w, batch, seq_len, q_block, lam_init, cache=None):
    d = D_MODEL
    nq = seq_len // q_block
    has_cache = cache is not None
    kv_spec = pl.BlockSpec((seq_len, d), lambda b, qi: (b, 0))
    in_specs = [pl.BlockSpec((q_block, d), lambda b, qi: (b * nq + qi, 0)), kv_spec, kv_spec]
    args = [q, k, v]
    if has_cache:
        past = cache[0].shape[1]
        in_specs += [pl.BlockSpec((d, past), lambda b, qi: (b, 0)),
                     pl.BlockSpec((past * DIFF_HEADS, 2 * DIFF_HD), lambda b, qi: (b, 0))]
        args += list(cache)
    in_specs += [pl.BlockSpec((4, DIFF_HD), lambda b, qi: (0, 0)),
                 pl.BlockSpec((1, 2 * DIFF_HD), lambda b, qi: (0, 0))]
    args += [lam_vecs, subln_w]
    return pl.pallas_call(
        functools.partial(_diffattn_kernel, has_cache=has_cache, lam_init=lam_init),
        grid=(batch, nq),
        in_specs=in_specs,
        out_specs=pl.BlockSpec((q_block, d), lambda b, qi: (b * nq + qi, 0)),
        out_shape=jax.ShapeDtypeStruct((batch * seq_len, d), BF16),
        compiler_params=_params(2),
        name="diffattn_cache" if has_cache else "diffattn",
    )(*args)


def _router_weights(router_group, router_expert):
    w = jnp.concatenate([router_group, router_expert], axis=1)
    w = jnp.pad(w, ((0, 0), (0, LANES - w.shape[1])))
    hi = w.astype(BF16)
    return hi, (w - hi.astype(F32)).astype(BF16)


def _inproj0_weights(w_in):
    gq, gk, gv, gg, gaf, gab, hq, hff, hfb, hi, hg = jnp.split(
        w_in, [256, 512, 1024, 1536, 1552, 1568, 1824, 2080, 2336, 2848], axis=1)
    w = jnp.concatenate([gq, gk, gv, gg, hq, hff, hfb, hi, hg, gaf, gab], axis=1)
    return jnp.pad(w, ((0, 0), (0, AB_COLS - w.shape[1]))).astype(BF16)


def kernel(x_prompt, x_sample, state_gla, state_hgrn, cache_diff_k, cache_diff_v, c, c_ctx,
           w_ada, b_ada, norm1_w, norm2_w, w_in_ab, gla_a2, gla_a_bias, hgrn_lb, gla_onorm_w,
           hgrn_onorm_w, w_out_ab, w_in_c, lam_q1, lam_k1, lam_q2, lam_k2, diff_subln_w, w_out_c,
           router_group, router_expert, moe_w1, moe_w3, moe_w2, final_norm_w):
    bp, lp, d = x_prompt.shape
    bs, ls, _ = x_sample.shape
    depth = w_ada.shape[0]
    assert depth == 2 and d == D_MODEL and bs <= 7
    tp, ts = bp * lp, bs * ls
    npt, nst = tp // ROW_TILE, ts // ROW_TILE
    tps = ls // ROW_TILE
    xp = x_prompt.reshape(tp, d)
    xs = x_sample.reshape(ts, d)

    cond8 = jnp.concatenate([c_ctx[None, :], c, jnp.zeros((7 - bs, d), F32)], axis=0)
    mods = _adaln(cond8, w_ada, b_ada).reshape(depth * 8, 6, d)

    proj = _inproj0(xp, xs, mods, norm1_w[0:1], _inproj0_weights(w_in_ab[0]), ls)
    a_bias = gla_a_bias[0][:, None, :]
    scan_args = (gla_a2[0], a_bias, hgrn_lb, gla_onorm_w[0:1], hgrn_onorm_w[0:1])
    mixed_p, s_fin = _scan(proj, 0, bp, lp, *scan_args)
    s0 = jnp.concatenate([state_gla[:, 0], state_hgrn[:, 0]], axis=2).swapaxes(-1, -2)
    s0 = s0.reshape(bs, 2, SCAN_PAIRS, 2, HEAD_DV, HEAD_DK)
    zero = jnp.zeros_like(s0[:, :, :, 0])
    s0 = jnp.concatenate([jnp.concatenate([s0[:, :, :, 0], zero], axis=-1),
                          jnp.concatenate([zero, s0[:, :, :, 1]], axis=-1)], axis=-2)
    mixed_s = _scan(proj, tp, bs, ls, *scan_args, s0=s0)
    new_state_gla = s_fin[:, None, :, :GLA_HEADS]
    new_state_hgrn = s_fin[:, None, :, GLA_HEADS:]

    wr = _router_weights(router_group[0], router_expert[0])
    x1, *routed = _post((xp, xs), mixed_p, mixed_s, mods, 0, norm2_w[0:1],
                        w_out_ab[0].astype(BF16), *wr, tps)
    ys, tables = _moe(*routed, moe_w1, moe_w3, moe_w2, 0)
    x2 = _combine(*tables, x1, mods, 0, final_norm_w[None, :], ys, 0, npt + nst, npt, tps, False)

    lam_init = 0.8 - 0.6 * math.exp(-0.3 * 1)
    cos_t, sin_t = _rope_tables(ls)
    (qp, kp, vp), (qs, ks, vs), (k_cache, v_cache) = _inproj1(
        x2, mods, norm1_w[1:2], w_in_c[0].astype(BF16), tp // ATTN_TILE, ts // ATTN_TILE,
        ls // ATTN_TILE, cos_t, sin_t)
    lam_vecs = jnp.stack([lam_q1[0], lam_k1[0], lam_q2[0], lam_k2[0]])
    att_p = _diffattn(qp, kp, vp, lam_vecs, diff_subln_w[0:1], bp, lp, lp, lam_init)
    past = cache_diff_k.shape[2]
    assert lp == ATTN_TILE and DIFF_HEADS == TOKEN_ROWS
    cache = (cache_diff_k[:, 0].transpose(0, 2, 3, 4, 1).reshape(bs * d, past),
             cache_diff_v[:, 0].reshape(bs * past * DIFF_HEADS, 2 * DIFF_HD))
    att_s = _diffattn(qs, ks, vs, lam_vecs, diff_subln_w[0:1], bs, ls, ATTN_TILE, lam_init, cache)

    wr = _router_weights(router_group[1], router_expert[1])
    x3, *routed = _post((x2,), att_p, att_s, mods, 1, norm2_w[1:2],
                        w_out_c[0].astype(BF16), *wr, tps)
    ys, tables = _moe(*routed, moe_w1, moe_w3, moe_w2, 1)
    fw = final_norm_w[None, :]
    y_p = _combine(*tables, x3, mods, 1, fw, ys, 0, npt, npt, tps, True)
    y_s = _combine(*tables, x3, mods, 1, fw, ys, npt, nst, npt, tps, True)

    return (y_p.reshape(bp, lp, d), y_s.reshape(bs, ls, d), new_state_gla, new_state_hgrn,
            k_cache.reshape(bp, 1, DIFF_HEADS, 2, DIFF_HD, lp).transpose(0, 1, 5, 2, 3, 4),
            v_cache.reshape(bp, 1, lp, DIFF_HEADS, 2 * DIFF_HD))
```

```python
import functools
import math

import jax
import jax.numpy as jnp
import numpy as np
from jax import lax
from jax.experimental import pallas as pl
from jax.experimental.pallas import tpu as pltpu

F32 = jnp.float32
BF16 = jnp.bfloat16
I32 = jnp.int32

D_MODEL = 1024
GLA_HEADS = 4
HGRN_HEADS = 4
SCAN_HEADS = GLA_HEADS + HGRN_HEADS
SCAN_PAIRS = SCAN_HEADS // 2
HEAD_DK = 64
HEAD_DV = 128
GATE_RANK = 16
GLA_GATE_NORM = 16.0
DIFF_HEADS = 8
DIFF_HD = 64
GRID_W = 64
ROPE_THETA = 10000.0
N_GROUPS = 4
EXPERTS_PER_GROUP = 8
N_EXPERTS = N_GROUPS * EXPERTS_PER_GROUP
MOE_HIDDEN = 512
EPS = 1e-6
LANES = 128
TOKEN_ROWS = D_MODEL // LANES
NEG_BIG = -1e30

ROW_TILE = 512
ATTN_TILE = 256
SAMPLE_Q_BLOCK = 512
ADA_TILE = 1536
INPROJ0_TILE = 512
SCAN_CHUNK = 64
EXPERT_TILE = 256
MOVE_BLOCK = 16
STAGE_TOKENS = 2 * ROW_TILE + N_EXPERTS * MOVE_BLOCK
MAX_BLOCKS = STAGE_TOKENS // MOVE_BLOCK
VMEM_LIMIT = 56 * 1024 * 1024
SCAN_INPUT_DOUBLE_BUFFER_BYTES = 8 * 1024 * 1024

_C_GQ, _C_GK, _C_GV, _C_GG = 0, 256, 512, 1024
_C_HQ, _C_HFF, _C_HFB, _C_HI, _C_HG = 1536, 1792, 2048, 2304, 2816
_C_GAF, _C_GAB = 3328, 3344
AB_COLS = 3456


def _params(n_axes, vmem=VMEM_LIMIT):
    return pltpu.CompilerParams(dimension_semantics=("arbitrary",) * n_axes,
                                vmem_limit_bytes=vmem)


def _cdiv(a, b):
    return (a + b - 1) // b


def _dot(a, b):
    return jnp.dot(a, b, preferred_element_type=F32)


def _dot_nt(a, b):
    return lax.dot_general(a, b, (((1,), (1,)), ((), ())), preferred_element_type=F32)


def _dot_tn(a, b):
    return lax.dot_general(a, b, (((0,), (0,)), ((), ())), preferred_element_type=F32)


def _split_bf16(x):
    hi = x.astype(BF16)
    lo = (x - hi.astype(F32)).astype(BF16)
    return hi, lo


def _silu(x):
    return x * jax.nn.sigmoid(x)


def _log_sigmoid(x):
    return jnp.minimum(x, 0.0) - jnp.log(1.0 + jnp.exp(-jnp.abs(x)))


def _rms(x):
    return x * lax.rsqrt(jnp.mean(x * x, axis=-1, keepdims=True) + EPS)


def _modulate(x, norm_w, shift, scale):
    return (_rms(x) * norm_w) * (1.0 + scale) + shift


def _to_token_major(dst_ref, x, row0=0):
    n = x.shape[0]
    for s in range(TOKEN_ROWS):
        dst_ref[pl.ds(row0 + s, n, stride=TOKEN_ROWS), :] = x[:, s * LANES:(s + 1) * LANES]


def _from_token_major(src_ref, n, row0=0):
    return jnp.concatenate([src_ref[pl.ds(row0 + s, n, stride=TOKEN_ROWS), :]
                            for s in range(TOKEN_ROWS)], axis=1)


def _ada_kernel(c_ref, w_ref, b_ref, o_ref):
    s = _silu(c_ref[...])
    o_ref[0] = _dot(s.astype(BF16), w_ref[0].astype(BF16)) + b_ref[0]


def _adaln(cond8, w_ada, b_ada):
    depth, d, n = w_ada.shape
    tn = ADA_TILE
    return pl.pallas_call(
        _ada_kernel,
        grid=(depth, n // tn),
        in_specs=[pl.BlockSpec((8, d), lambda l, j: (0, 0)),
                  pl.BlockSpec((1, d, tn), lambda l, j: (l, 0, j)),
                  pl.BlockSpec((1, 1, tn), lambda l, j: (l, 0, j))],
        out_specs=pl.BlockSpec((1, 8, tn), lambda l, j: (l, 0, j)),
        out_shape=jax.ShapeDtypeStruct((depth, 8, n), F32),
        compiler_params=_params(2),
        name="adaln",
    )(cond8, w_ada, b_ada.reshape(depth, 1, n))


def _mod_row(i, layer, n_prompt_tiles, tiles_per_sample):
    r = jnp.where(i < n_prompt_tiles, 0, 1 + (i - n_prompt_tiles) // tiles_per_sample)
    return layer * 8 + r


def _inproj0_kernel(xp_ref, xs_ref, mod_ref, nw_ref, w_ref, o_ref, *, n_prompt_tiles):
    i = pl.program_id(0)
    x = jnp.where(i < n_prompt_tiles, xp_ref[...], xs_ref[...])
    h = _modulate(x, nw_ref[...], mod_ref[0, 0:1, :], mod_ref[0, 1:2, :])
    o_ref[...] = _dot(h.astype(BF16), w_ref[...])


def _inproj0(xp, xs, mods, norm_w, w_bf16, sample_len):
    tp, d = xp.shape
    ts = xs.shape[0]
    n = w_bf16.shape[1]
    tile = INPROJ0_TILE
    npt, nst = tp // tile, ts // tile
    mod_map = lambda i: (_mod_row(i, 0, npt, sample_len // tile), 0, 0)
    return pl.pallas_call(
        functools.partial(_inproj0_kernel, n_prompt_tiles=npt),
        grid=(npt + nst,),
        in_specs=[pl.BlockSpec((tile, d), lambda i: (jnp.minimum(i, npt - 1), 0)),
                  pl.BlockSpec((tile, d), lambda i: (jnp.maximum(i - npt, 0), 0)),
                  pl.BlockSpec((1, 6, d), mod_map),
                  pl.BlockSpec((1, d), lambda i: (0, 0)),
                  pl.BlockSpec((d, n), lambda i: (0, 0))],
        out_specs=pl.BlockSpec((tile, n), lambda i: (i, 0)),
        out_shape=jax.ShapeDtypeStruct((tp + ts, n), F32),
        compiler_params=_params(1),
        name="inproj0",
    )(xp, xs, mods, norm_w, w_bf16)


def _scan_kernel(*refs, seq_len, has_state):
    if has_state:
        (p_ref, a2_ref, ab_ref, lb_ref, ong_ref, onh_ref, s0_ref, mixed_ref, *scratch) = refs
        sfin_ref = None
    else:
        (p_ref, a2_ref, ab_ref, lb_ref, ong_ref, onh_ref, mixed_ref, sg_ref, sh_ref,
         *scratch) = refs
        s0_ref = None
        sfin_ref = (sg_ref, sh_ref)
    (qi_f, ki_f, qo_f, ko_f, qi_b, ki_b, qo_b, ko_b,
     vv, dec_f, dec_b, o_f, o_b, st_f, st_b) = scratch
    C = SCAN_CHUNK
    n_chunks = seq_len // C
    gqk = GLA_HEADS * HEAD_DK

    row = lax.broadcasted_iota(I32, (C, C), 0)
    col = lax.broadcasted_iota(I32, (C, C), 1)
    lower = col <= row
    upper = col >= row
    tri_lo = jnp.where(lower, 1.0, 0.0).astype(BF16)
    tri_up = jnp.where(upper, 1.0, 0.0).astype(BF16)

    lbp = lb_ref[...]
    lb_max = jnp.maximum(lbp[0], lbp[1])
    lb_e0 = jnp.exp(lbp[0] - lb_max)
    lb_e1 = jnp.exp(lbp[1] - lb_max)
    lb = lb_e0 / (lb_e0 + lb_e1)

    def cumsum_chunk(tri, la):
        hi, lo = _split_bf16(la)
        return _dot(tri, hi) + _dot(tri, lo)

    def prep(n, carry):
        r0 = pl.multiple_of(n * C, C)
        rows = pl.ds(r0, C)
        gq = p_ref[rows, _C_GQ:_C_GQ + gqk] * (HEAD_DK ** -0.5)
        gk = p_ref[rows, _C_GK:_C_GK + gqk]
        hq = _silu(p_ref[rows, _C_HQ:_C_HQ + gqk]) * (HEAD_DK ** -0.5)
        for d_i, (qi_s, ki_s, qo_s, ko_s, dec_s, tri, last, mid) in enumerate(
                ((qi_f, ki_f, qo_f, ko_f, dec_f, tri_lo, C - 1, C // 2 - 1),
                 (qi_b, ki_b, qo_b, ko_b, dec_b, tri_up, 0, C // 2))):
            c_ga = _C_GAF if d_i == 0 else _C_GAB
            c_hf = _C_HFF if d_i == 0 else _C_HFB
            ga = p_ref[rows, c_ga:c_ga + GATE_RANK]
            xg = _dot(ga.astype(BF16), a2_ref[d_i].astype(BF16)) + ab_ref[d_i]
            la_g = _log_sigmoid(xg) / GLA_GATE_NORM
            f = lb[d_i:d_i + 1, :] + (1.0 - lb[d_i:d_i + 1, :]) * jax.nn.sigmoid(
                p_ref[rows, c_hf:c_hf + gqk])
            la_h = jnp.log(f)
            for q, k, la, c0 in ((gq, gk, la_g, 0), (hq, 1.0 - f, la_h, gqk)):
                b = cumsum_chunk(tri, la)
                b_mid, b_end = b[mid:mid + 1, :], b[last:last + 1, :]
                cs = slice(c0, c0 + gqk)
                qi_s[rows, cs] = (q * jnp.exp(b - b_mid)).astype(BF16)
                ki_s[rows, cs] = (k * jnp.exp(b_mid - b)).astype(BF16)
                qo_s[rows, cs] = (q * jnp.exp(b)).astype(BF16)
                ko_s[rows, cs] = (k * jnp.exp(b_end - b)).astype(BF16)
                dec_s[n, :, cs] = jnp.exp(b_end)
        gv_cols = GLA_HEADS * HEAD_DV
        vv[rows, 0:gv_cols] = p_ref[rows, _C_GV:_C_GV + gv_cols].astype(BF16)
        vv[rows, gv_cols:] = p_ref[rows, _C_HI:_C_HI + HGRN_HEADS * HEAD_DV].astype(BF16)
        return carry

    lax.fori_loop(0, n_chunks, prep, 0, unroll=2)

    for p in range(SCAN_PAIRS):
        if has_state:
            st_f[p] = s0_ref[0, 0, p]
            st_b[p] = s0_ref[0, 1, p]
        else:
            st_f[p] = jnp.zeros((2 * HEAD_DV, 2 * HEAD_DK), F32)
            st_b[p] = jnp.zeros((2 * HEAD_DV, 2 * HEAD_DK), F32)

    first_head = lax.broadcasted_iota(I32, (C, 2 * HEAD_DK), 1) < HEAD_DK
    row2 = lax.broadcasted_iota(I32, (2 * C, C), 0) % C
    col2 = lax.broadcasted_iota(I32, (2 * C, C), 1)
    lower2 = col2 <= row2
    upper2 = col2 >= row2

    def per_head_rows(x):
        z = jnp.zeros_like(x)
        return jnp.concatenate([jnp.where(first_head, x, z), jnp.where(first_head, z, x)], axis=0)

    def put_out(o_ref, rows, p, res):
        c0 = p * 2 * HEAD_DV
        o_ref[rows, c0:c0 + HEAD_DV] = res[0:C, 0:HEAD_DV]
        o_ref[rows, c0 + HEAD_DV:c0 + 2 * HEAD_DV] = res[C:2 * C, HEAD_DV:2 * HEAD_DV]

    def sweep(n, carry):
        m = n_chunks - 1 - n
        rows = pl.ds(pl.multiple_of(n * C, C), C)
        rows_m = pl.ds(pl.multiple_of(m * C, C), C)
        decay_f, decay_b = dec_f[n], dec_b[m]
        for p in range(SCAN_PAIRS):
            ks = slice(p * 2 * HEAD_DK, (p + 1) * 2 * HEAD_DK)
            vs = slice(p * 2 * HEAD_DV, (p + 1) * 2 * HEAD_DV)
            vh = vv[rows, vs]
            s_f = st_f[p]
            sc = (jnp.where(lower2, _dot_nt(per_head_rows(qi_f[rows, ks]), ki_f[rows, ks]), 0.0)
                  + jnp.where(upper2, _dot_nt(per_head_rows(qi_b[rows, ks]), ki_b[rows, ks]), 0.0))
            put_out(o_f, rows, p, _dot_nt(per_head_rows(qo_f[rows, ks]), s_f.astype(BF16))
                    + _dot(sc.astype(BF16), vh))
            st_f[p] = decay_f[:, ks] * s_f + _dot_tn(vh, ko_f[rows, ks])
            s_b = st_b[p]
            put_out(o_b, rows_m, p, _dot_nt(per_head_rows(qo_b[rows_m, ks]), s_b.astype(BF16)))
            st_b[p] = decay_b[:, ks] * s_b + _dot_tn(vv[rows_m, vs], ko_b[rows_m, ks])
        return carry

    lax.fori_loop(0, n_chunks, sweep, 0)

    def finish(n, carry):
        rows = pl.ds(pl.multiple_of(n * C, C), C)
        for h in range(SCAN_HEADS):
            vs = slice(h * HEAD_DV, (h + 1) * HEAD_DV)
            if h < GLA_HEADS:
                gate = p_ref[rows, _C_GG + h * HEAD_DV:_C_GG + (h + 1) * HEAD_DV]
                onw = ong_ref[...]
            else:
                hh = h - GLA_HEADS
                gate = p_ref[rows, _C_HG + hh * HEAD_DV:_C_HG + (hh + 1) * HEAD_DV]
                onw = onh_ref[...]
            o = o_f[rows, vs] + o_b[rows, vs]
            mixed_ref[rows, vs] = ((_rms(o) * onw) * _silu(gate)).astype(BF16)
        return carry

    lax.fori_loop(0, n_chunks, finish, 0, unroll=2)

    if sfin_ref is not None:
        for d_i, st in enumerate((st_f, st_b)):
            for p in range(SCAN_PAIRS):
                s_pair = st[p].T
                out_ref = sfin_ref[(2 * p) // GLA_HEADS]
                h0 = (2 * p) % GLA_HEADS
                out_ref[0, 0, d_i, h0] = s_pair[0:HEAD_DK, 0:HEAD_DV]
                out_ref[0, 0, d_i, h0 + 1] = s_pair[HEAD_DK:2 * HEAD_DK, HEAD_DV:2 * HEAD_DV]


def _scan(p, row0, batch, seq_len, a2, a_bias, lb, onorm_g, onorm_h, s0=None):
    n = p.shape[1]
    assert row0 % seq_len == 0
    blk0 = row0 // seq_len
    has_state = s0 is not None
    n_chunks = seq_len // SCAN_CHUNK
    assert GLA_HEADS == HGRN_HEADS and GLA_HEADS % 2 == 0
    st_shape = (1, 1, 2, GLA_HEADS, HEAD_DK, HEAD_DV)
    pair_shape = (SCAN_PAIRS, 2 * HEAD_DV, 2 * HEAD_DK)
    p_mode = dict(pipeline_mode=pl.Buffered(1)) if seq_len * n * 4 > SCAN_INPUT_DOUBLE_BUFFER_BYTES else {}
    in_specs = [pl.BlockSpec((seq_len, n), lambda b: (blk0 + b, 0), **p_mode),
                pl.BlockSpec(a2.shape, lambda b: (0, 0, 0)),
                pl.BlockSpec(a_bias.shape, lambda b: (0, 0, 0)),
                pl.BlockSpec(lb.shape, lambda b: (0, 0, 0)),
                pl.BlockSpec((1, HEAD_DV), lambda b: (0, 0)),
                pl.BlockSpec((1, HEAD_DV), lambda b: (0, 0))]
    args = [p, a2, a_bias, lb, onorm_g, onorm_h]
    mixed_shape = jax.ShapeDtypeStruct((batch * seq_len, D_MODEL), BF16)
    mixed_spec = pl.BlockSpec((seq_len, D_MODEL), lambda b: (b, 0))
    if has_state:
        in_specs.append(pl.BlockSpec((1, 2) + pair_shape, lambda b: (b, 0, 0, 0, 0)))
        args.append(s0)
        out_shape, out_specs = mixed_shape, mixed_spec
    else:
        st_struct = jax.ShapeDtypeStruct((batch,) + st_shape[1:], F32)
        st_spec = pl.BlockSpec(st_shape, lambda b: (b, 0, 0, 0, 0, 0))
        out_shape = (mixed_shape, st_struct, st_struct)
        out_specs = (mixed_spec, st_spec, st_spec)
    qk_cols = SCAN_HEADS * HEAD_DK
    scratch = [pltpu.VMEM((seq_len, qk_cols), BF16) for _ in range(8)]
    scratch += [pltpu.VMEM((seq_len, D_MODEL), BF16),
                pltpu.VMEM((n_chunks, 1, qk_cols), F32), pltpu.VMEM((n_chunks, 1, qk_cols), F32),
                pltpu.VMEM((seq_len, D_MODEL), F32), pltpu.VMEM((seq_len, D_MODEL), F32),
                pltpu.VMEM(pair_shape, F32), pltpu.VMEM(pair_shape, F32)]
    return pl.pallas_call(
        functools.partial(_scan_kernel, seq_len=seq_len, has_state=has_state),
        grid=(batch,),
        in_specs=in_specs, out_specs=out_specs, out_shape=out_shape,
        scratch_shapes=scratch,
        compiler_params=_params(1),
        name="scan_state" if has_state else "scan_fresh",
    )(*args)


def _post_kernel(*refs, split_x, n_prompt_tiles):
    if split_x:
        xp_ref, xs_ref = refs[0], refs[1]
        refs = refs[2:]
    else:
        x_ref = refs[0]
        refs = refs[1:]
    (mp_ref, ms_ref, mod_ref, nw_ref, wo_ref, wrh_ref, wrl_ref,
     x1_ref, h2_ref, slot_ref, wgt_ref, tab_ref, carry) = refs
    i = pl.program_id(0)
    is_prompt = i < n_prompt_tiles
    if split_x:
        x = jnp.where(is_prompt, xp_ref[...], xs_ref[...])
    else:
        x = x_ref[...]
    mixed = jnp.where(is_prompt, mp_ref[...], ms_ref[...])
    x1 = x + mod_ref[0, 2:3, :] * _dot(mixed, wo_ref[...])
    x1_ref[...] = x1
    h2 = _modulate(x1, nw_ref[...], mod_ref[0, 3:4, :], mod_ref[0, 4:5, :])
    _to_token_major(h2_ref, h2)

    hh, hl = _split_bf16(h2)
    logits = _dot(hh, wrh_ref[...]) + _dot(hl, wrh_ref[...]) + _dot(hh, wrl_ref[...])
    tm = logits.shape[0]
    lane = lax.broadcasted_iota(I32, (tm, LANES), 1).astype(F32)

    def first_max(v):
        mx = jnp.max(v, axis=1, keepdims=True)
        idx = jnp.min(jnp.where(v == mx, lane, float(LANES)), axis=1, keepdims=True)
        return mx, idx

    gl = jnp.where(lane < N_GROUPS, logits, NEG_BIG)
    gmax, gidx = first_max(gl)
    g_val = 1.0 / jnp.sum(jnp.exp(gl - gmax), axis=1, keepdims=True)
    lo = N_GROUPS + EXPERTS_PER_GROUP * gidx
    el = jnp.where((lane >= lo) & (lane < lo + EXPERTS_PER_GROUP), logits, NEG_BIG)
    emax, l1 = first_max(el)
    esum = jnp.sum(jnp.exp(el - emax), axis=1, keepdims=True)
    e2max, l2 = first_max(jnp.where(lane == l1, NEG_BIG, el))
    p1 = 1.0 / esum
    p2 = jnp.exp(e2max - emax) / esum
    w1 = g_val * (p1 / (p1 + p2))
    w2 = g_val * (p2 / (p1 + p2))
    id1 = l1 - N_GROUPS
    id2 = l2 - N_GROUPS

    @pl.when(i == 0)
    def _():
        carry[...] = jnp.zeros_like(carry)

    sel1 = lane == id1
    sel2 = lane == id2
    onehot = jnp.where(sel1 | sel2, 1.0, 0.0)
    row = lax.broadcasted_iota(I32, (tm, tm), 0)
    col = lax.broadcasted_iota(I32, (tm, tm), 1)
    earlier = jnp.where(col < row, 1.0, 0.0).astype(BF16)
    before = _dot(earlier, onehot.astype(BF16))
    count = jnp.sum(onehot, axis=0, keepdims=True)
    blocks = jnp.floor((count + (MOVE_BLOCK - 1.0)) * (1.0 / MOVE_BLOCK)) * MOVE_BLOCK
    e_row = lax.broadcasted_iota(I32, (LANES, LANES), 0)
    e_col = lax.broadcasted_iota(I32, (LANES, LANES), 1)
    lower_experts = jnp.where(e_row < e_col, 1.0, 0.0).astype(BF16)
    run_start = _dot(jnp.broadcast_to(blocks, (8, LANES)).astype(BF16), lower_experts)[0:1]
    slot = before + run_start
    q1 = jnp.sum(jnp.where(sel1, slot, 0.0), axis=1, keepdims=True)
    q2 = jnp.sum(jnp.where(sel2, slot, 0.0), axis=1, keepdims=True)
    tab_row = lax.broadcasted_iota(I32, (8, LANES), 0)
    tab_ref[0] = jnp.where(tab_row == 0, count,
                           jnp.where(tab_row == 1, carry[...],
                                     jnp.where(tab_row == 2, run_start, 0.0)))
    carry[...] = carry[...] + count

    cols = jnp.zeros((tm, LANES), F32)
    for k, v in enumerate((q1 * TOKEN_ROWS, q2 * TOKEN_ROWS, w1, w2)):
        cols = jnp.where(lane == k, v, cols)
    rows = cols.T
    slot_ref[0] = rows[0:2].astype(I32)
    wgt_ref[0] = rows[2:4]


def _post(x_args, mixed_p, mixed_s, mods, layer, norm_w, w_out_bf16, wr_hi, wr_lo,
          tiles_per_sample):
    split_x = len(x_args) == 2
    tp, ts = mixed_p.shape[0], mixed_s.shape[0]
    t, d = tp + ts, D_MODEL
    npt, nst = tp // ROW_TILE, ts // ROW_TILE
    tile = lambda i: (i, 0)
    if split_x:
        x_specs = [pl.BlockSpec((ROW_TILE, d), lambda i: (jnp.minimum(i, npt - 1), 0)),
                   pl.BlockSpec((ROW_TILE, d), lambda i: (jnp.maximum(i - npt, 0), 0))]
    else:
        x_specs = [pl.BlockSpec((ROW_TILE, d), tile)]
    in_specs = x_specs + [
        pl.BlockSpec((ROW_TILE, d), lambda i: (jnp.minimum(i, npt - 1), 0)),
        pl.BlockSpec((ROW_TILE, d), lambda i: (jnp.maximum(i - npt, 0), 0)),
        pl.BlockSpec((1, 6, d), lambda i: (_mod_row(i, layer, npt, tiles_per_sample), 0, 0)),
        pl.BlockSpec((1, d), lambda i: (0, 0)),
        pl.BlockSpec((d, d), lambda i: (0, 0)),
        pl.BlockSpec((d, LANES), lambda i: (0, 0)),
        pl.BlockSpec((d, LANES), lambda i: (0, 0))]
    return pl.pallas_call(
        functools.partial(_post_kernel, split_x=split_x, n_prompt_tiles=npt),
        grid=(npt + nst,),
        in_specs=in_specs,
        out_specs=(pl.BlockSpec((ROW_TILE, d), tile),
                   pl.BlockSpec((ROW_TILE * TOKEN_ROWS, LANES), tile),
                   pl.BlockSpec((1, 2, ROW_TILE), lambda i: (i, 0, 0)),
                   pl.BlockSpec((1, 2, ROW_TILE), lambda i: (i, 0, 0)),
                   pl.BlockSpec((1, 8, LANES), lambda i: (i, 0, 0))),
        out_shape=(jax.ShapeDtypeStruct((t, d), F32),
                   jax.ShapeDtypeStruct((t * TOKEN_ROWS, LANES), F32),
                   jax.ShapeDtypeStruct((npt + nst, 2, ROW_TILE), I32),
                   jax.ShapeDtypeStruct((npt + nst, 2, ROW_TILE), F32),
                   jax.ShapeDtypeStruct((npt + nst, 8, LANES), F32)),
        scratch_shapes=[pltpu.VMEM((1, LANES), F32)],
        compiler_params=_params(1),
        name=f"post{layer}",
    )(*x_args, mixed_p, mixed_s, mods, norm_w, w_out_bf16, wr_hi, wr_lo)


def _for_blocks(tab_ref, fn):
    block_rows = MOVE_BLOCK * TOKEN_ROWS
    count = tab_ref[0, 0, MAX_BLOCKS]

    def call(k, parity):
        fn(pl.multiple_of(k * block_rows, block_rows),
           pl.multiple_of(tab_ref[0, 0, k], TOKEN_ROWS), parity)

    def body(k2, c):
        call(2 * k2, 0)

        @pl.when(2 * k2 + 1 < count)
        def _():
            call(2 * k2 + 1, 1)
        return c

    lax.fori_loop(0, _cdiv(count, 2), body, 0)


def _wait_blocks(tab_ref, copy):
    def body(k, c):
        copy.wait()
        return c

    lax.fori_loop(0, tab_ref[0, 0, MAX_BLOCKS], body, 0)


def _dispatch_kernel(zero_ref, tab_ref, prev_tab_ref, q_ref, h2_ref, hs_ref, zero_buf, stage, sem):
    j = pl.program_id(0)
    slot = j % 2
    block_rows = MOVE_BLOCK * TOKEN_ROWS

    @pl.when(j == 0)
    def _():
        zero_buf[...] = jnp.zeros_like(zero_buf)

        def zero_copy(k):
            start = pl.multiple_of(zero_ref[k], EXPERT_TILE * TOKEN_ROWS)
            return pltpu.make_async_copy(
                zero_buf, hs_ref.at[pl.ds(start, EXPERT_TILE * TOKEN_ROWS)], sem.at[0])

        def start_zero(k, c):
            @pl.when(zero_ref[k] >= 0)
            def _():
                zero_copy(k).start()
            return c

        def wait_zero(k, c):
            @pl.when(zero_ref[k] >= 0)
            def _():
                zero_copy(k).wait()
            return c

        lax.fori_loop(0, zero_ref.shape[0], start_zero, 0)
        lax.fori_loop(0, zero_ref.shape[0], wait_zero, 0)

        stage[...] = jnp.zeros_like(stage)

    def place(r, c):
        tok = h2_ref[pl.ds(pl.multiple_of(r * TOKEN_ROWS, TOKEN_ROWS), TOKEN_ROWS), :]
        for s in range(2):
            row = pl.multiple_of(q_ref[0, s, r], TOKEN_ROWS)
            stage[slot, pl.ds(row, TOKEN_ROWS), :] = tok
        return c

    lax.fori_loop(0, ROW_TILE, place, 0, unroll=8)

    def block_copy(buf, stage_row, sorted_row):
        return pltpu.make_async_copy(stage.at[buf, pl.ds(stage_row, block_rows)],
                                     hs_ref.at[pl.ds(sorted_row, block_rows)], sem.at[buf])

    @pl.when(j > 0)
    def _():
        _wait_blocks(prev_tab_ref, block_copy(1 - slot, 0, 0))

    _for_blocks(tab_ref, lambda a, b, parity: block_copy(slot, a, b).start(priority=parity))

    @pl.when(j == pl.num_programs(0) - 1)
    def _():
        _wait_blocks(tab_ref, block_copy(slot, 0, 0))


def _dispatch(zero_tiles, block_tab, slots, h2, n_rows):
    t = h2.shape[0] // TOKEN_ROWS
    nt = t // ROW_TILE
    smem_tile = lambda shape: pl.BlockSpec((1,) + shape, lambda j, *_: (j, 0, 0),
                                           memory_space=pltpu.SMEM)
    grid_spec = pltpu.PrefetchScalarGridSpec(
        num_scalar_prefetch=1,
        grid=(nt,),
        in_specs=[smem_tile((1, LANES)),
                  pl.BlockSpec((1, 1, LANES), lambda j, *_: (jnp.maximum(j - 1, 0), 0, 0),
                               memory_space=pltpu.SMEM),
                  smem_tile((2, ROW_TILE)),
                  pl.BlockSpec((ROW_TILE * TOKEN_ROWS, LANES), lambda j, *_: (j, 0))],
        out_specs=pl.BlockSpec(memory_space=pl.ANY),
        scratch_shapes=[pltpu.VMEM((EXPERT_TILE * TOKEN_ROWS, LANES), F32),
                        pltpu.VMEM((2, STAGE_TOKENS * TOKEN_ROWS, LANES), F32),
                        pltpu.SemaphoreType.DMA((2,))])
    return pl.pallas_call(
        _dispatch_kernel,
        grid_spec=grid_spec,
        out_shape=jax.ShapeDtypeStruct((n_rows * TOKEN_ROWS, LANES), F32),
        compiler_params=_params(1),
        name="dispatch",
    )(zero_tiles, block_tab, block_tab, slots, h2)


def _expert_kernel(te_ref, src_ref, nv_ref, run_ref, nxt_ref, hs_ref, w1_hbm, w3_hbm, w2_hbm,
                   ys_ref, w1f, w3f, w2f, w1b, w3b, w2b, sem, *, layer):
    i = pl.program_id(0)

    def weight_copies(e, buf):
        return [pltpu.make_async_copy(src.at[layer, e], dst.at[buf], sem.at[buf])
                for src, dst in ((w1_hbm, w1f), (w3_hbm, w3f), (w2_hbm, w2f))]

    @pl.when(i == 0)
    def _():
        for c in weight_copies(te_ref[0], 0):
            c.start()

    first = (i == 0) | (run_ref[i] != run_ref[jnp.maximum(i - 1, 0)])

    @pl.when(first)
    def _():
        buf = run_ref[i] % 2
        for c in weight_copies(te_ref[i], buf):
            c.wait()

        @pl.when(nxt_ref[i] >= 0)
        def _():
            for c in weight_copies(nxt_ref[i], 1 - buf):
                c.start()

        w1b[...] = w1f[buf].astype(BF16)
        w3b[...] = w3f[buf].astype(BF16)
        w2b[...] = w2f[buf].astype(BF16)

    @pl.when(nv_ref[i] > 0)
    def _():
        h = _from_token_major(hs_ref, EXPERT_TILE).astype(BF16)
        g = _silu(_dot(h, w1b[...])) * _dot(h, w3b[...])
        _to_token_major(ys_ref, _dot(g.astype(BF16), w2b[...]))

    @pl.when(nv_ref[i] == 0)
    def _():
        ys_ref[...] = jnp.zeros_like(ys_ref)


def _experts(tile_expert, tile_src, tile_rows, hs, w1, w3, w2, layer):
    n_rows, d = hs.shape[0] // TOKEN_ROWS, D_MODEL
    nt = n_rows // EXPERT_TILE
    hid = w1.shape[-1]
    tok_tile = (EXPERT_TILE * TOKEN_ROWS, LANES)
    changed = jnp.concatenate([jnp.zeros((1,), I32),
                               (tile_expert[1:] != tile_expert[:-1]).astype(I32)])
    run = jnp.cumsum(changed).astype(I32)
    later = jnp.where(run[None, :] > run[:, None], tile_expert[None, :], N_EXPERTS)
    next_expert = jnp.min(later, axis=1)
    next_expert = jnp.where(next_expert < N_EXPERTS, next_expert, -1).astype(I32)
    grid_spec = pltpu.PrefetchScalarGridSpec(
        num_scalar_prefetch=5,
        grid=(nt,),
        in_specs=[pl.BlockSpec(tok_tile, lambda i, te, src, nv, run, nxt: (src[i], 0)),
                  pl.BlockSpec(memory_space=pl.ANY), pl.BlockSpec(memory_space=pl.ANY),
                  pl.BlockSpec(memory_space=pl.ANY)],
        out_specs=pl.BlockSpec(tok_tile, lambda i, te, src, nv, run, nxt: (i, 0)),
        scratch_shapes=[pltpu.VMEM((2, d, hid), F32), pltpu.VMEM((2, d, hid), F32),
                        pltpu.VMEM((2, hid, d), F32),
                        pltpu.VMEM((d, hid), BF16), pltpu.VMEM((d, hid), BF16),
                        pltpu.VMEM((hid, d), BF16), pltpu.SemaphoreType.DMA((2,))])
    return pl.pallas_call(
        functools.partial(_expert_kernel, layer=layer),
        grid_spec=grid_spec,
        out_shape=jax.ShapeDtypeStruct(hs.shape, F32),
        compiler_params=_params(1),
        name=f"experts{layer}",
    )(tile_expert, tile_src, tile_rows, run, next_expert, hs, w1, w3, w2)


def _combine_kernel(tab_ref, next_tab_ref, q_ref, w_ref, x1_ref, mod_ref, fw_ref, ys_ref, out_ref,
                    stage, y_tok, sem, *, final_norm):
    i = pl.program_id(0)
    slot = i % 2
    block_rows = MOVE_BLOCK * TOKEN_ROWS

    def block_copy(buf, stage_row, sorted_row):
        return pltpu.make_async_copy(ys_ref.at[pl.ds(sorted_row, block_rows)],
                                     stage.at[buf, pl.ds(stage_row, block_rows)], sem.at[buf])

    def fetch(tab, buf):
        _for_blocks(tab, lambda a, b, parity: block_copy(buf, a, b).start(priority=parity))

    @pl.when(i == 0)
    def _():
        fetch(tab_ref, slot)

    @pl.when(i + 1 < pl.num_programs(0))
    def _():
        fetch(next_tab_ref, 1 - slot)

    _wait_blocks(tab_ref, block_copy(slot, 0, 0))

    def pick(r, c):
        rows = [stage[slot, pl.ds(pl.multiple_of(q_ref[0, s, r], TOKEN_ROWS), TOKEN_ROWS), :]
                for s in range(2)]
        y_tok[pl.ds(pl.multiple_of(r * TOKEN_ROWS, TOKEN_ROWS), TOKEN_ROWS), :] = (
            w_ref[0, 0, r] * rows[0] + w_ref[0, 1, r] * rows[1])
        return c

    lax.fori_loop(0, ROW_TILE, pick, 0, unroll=8)
    x2 = x1_ref[...] + mod_ref[0, 5:6, :] * _from_token_major(y_tok, ROW_TILE)
    if final_norm:
        x2 = _rms(x2) * fw_ref[...]
    out_ref[...] = x2


def _combine(block_tab, slots, weights, x1, mods, layer, final_w, ys, tile0, n_tiles,
             n_prompt_tiles, tiles_per_sample, final_norm):
    d = D_MODEL
    tile = lambda i: (tile0 + i, 0)
    mod_map = lambda i: (_mod_row(tile0 + i, layer, n_prompt_tiles, tiles_per_sample), 0, 0)
    smem_tile = lambda shape: pl.BlockSpec((1,) + shape, lambda i: (tile0 + i, 0, 0),
                                           memory_space=pltpu.SMEM)
    return pl.pallas_call(
        functools.partial(_combine_kernel, final_norm=final_norm),
        grid=(n_tiles,),
        in_specs=[smem_tile((1, LANES)),
                  pl.BlockSpec((1, 1, LANES),
                               lambda i: (tile0 + jnp.minimum(i + 1, n_tiles - 1), 0, 0),
                               memory_space=pltpu.SMEM),
                  smem_tile((2, ROW_TILE)), smem_tile((2, ROW_TILE)),
                  pl.BlockSpec((ROW_TILE, d), tile),
                  pl.BlockSpec((1, 6, d), mod_map),
                  pl.BlockSpec((1, d), lambda i: (0, 0)),
                  pl.BlockSpec(memory_space=pl.ANY)],
        out_specs=pl.BlockSpec((ROW_TILE, d), lambda i: (i, 0)),
        out_shape=jax.ShapeDtypeStruct((n_tiles * ROW_TILE, d), F32),
        scratch_shapes=[pltpu.VMEM((2, STAGE_TOKENS * TOKEN_ROWS, LANES), F32),
                        pltpu.VMEM((ROW_TILE * TOKEN_ROWS, LANES), F32),
                        pltpu.SemaphoreType.DMA((2,))],
        compiler_params=_params(1),
        name=f"combine{layer}_{tile0}",
    )(block_tab, block_tab, slots, weights, x1, mods, final_w, ys)


def _moe(h2, slots, weights, tile_tab, w1, w3, w2, layer):
    t = h2.shape[0] // TOKEN_ROWS
    n_tiles = t // ROW_TILE
    extra_tiles = N_EXPERTS + _cdiv(N_EXPERTS * MOVE_BLOCK, EXPERT_TILE)
    n_rows = 2 * t + extra_tiles * EXPERT_TILE
    nt = n_rows // EXPERT_TILE
    tab = tile_tab[:, :, :N_EXPERTS].astype(I32)
    cnt = tab[-1, 0] + tab[-1, 1]
    tight = _cdiv(cnt, EXPERT_TILE) * EXPERT_TILE
    padded = jnp.where(cnt > 0, _cdiv(cnt + MOVE_BLOCK - 1, EXPERT_TILE) * EXPERT_TILE, 0)
    ends = jnp.cumsum(padded)
    offsets = ends - padded
    tails = jnp.where(cnt > 0, ends - EXPERT_TILE, -1)
    tails2 = jnp.where(padded > tight, ends - 2 * EXPERT_TILE, -1)
    used = ends[-1] // EXPERT_TILE
    tile_start = jnp.arange(nt, dtype=I32) * EXPERT_TILE
    unused = (used + jnp.arange(extra_tiles, dtype=I32)) * EXPERT_TILE
    zero_tiles = jnp.concatenate([tails, tails2, jnp.where(unused < n_rows, unused, -1)])
    zero_tiles = jnp.where(zero_tiles >= 0, zero_tiles * TOKEN_ROWS, -1).astype(I32)
    tile_src = jnp.minimum(jnp.arange(nt, dtype=I32), used - 1)
    tile_expert = jnp.sum((tile_src * EXPERT_TILE)[:, None] >= ends[None, :], axis=1).astype(I32)
    tile_rows = jnp.where(tile_start < ends[-1],
                          jnp.clip(cnt[tile_expert] - (tile_start - offsets[tile_expert]),
                                   0, EXPERT_TILE), 0).astype(I32)
    n_blocks = _cdiv(tab[:, 0], MOVE_BLOCK)
    blocks_through = jnp.cumsum(n_blocks, axis=1)
    k = jnp.arange(MAX_BLOCKS, dtype=I32)
    owner = jnp.sum(blocks_through[:, None, :] <= k[None, :, None], axis=2)
    is_owner = owner[:, :, None] == jnp.arange(N_EXPERTS, dtype=I32)[None, None, :]
    pick = lambda v: jnp.sum(jnp.where(is_owner, v[:, None, :], 0), axis=2)
    run_first = pick(offsets[None, :] + tab[:, 1])
    block_in_run = k[None, :] - pick(blocks_through - n_blocks)
    sorted_row = (run_first + block_in_run * MOVE_BLOCK) * TOKEN_ROWS
    block_tab = jnp.concatenate(
        [sorted_row, blocks_through[:, -1:],
         jnp.zeros((n_tiles, LANES - MAX_BLOCKS - 1), I32)], axis=1).astype(I32)[:, None, :]
    hs = _dispatch(zero_tiles, block_tab, slots, h2, n_rows)
    ys = _experts(tile_expert, tile_src, tile_rows, hs, w1, w3, w2, layer)
    return ys, (block_tab, slots, weights)


def _rope(x, cos, sin_signed):
    lane = lax.broadcasted_iota(I32, (x.shape[0], LANES), 1)
    low = (lane % 32) < 16
    outs = []
    for j in range(x.shape[1] // LANES):
        xb = x[:, j * LANES:(j + 1) * LANES]
        partner = jnp.where(low, pltpu.roll(xb, LANES - 16, 1), pltpu.roll(xb, 16, 1))
        outs.append(xb * cos + partner * sin_signed)
    return jnp.concatenate(outs, axis=1)


def _inproj1_prompt_kernel(x_ref, mod_ref, nw_ref, w_ref, q_ref, k_ref, v_ref, kc_ref, vc_ref):
    d = D_MODEL
    h = _modulate(x_ref[...], nw_ref[...], mod_ref[0, 0:1, :], mod_ref[0, 1:2, :]).astype(BF16)
    q_ref[...] = (_dot(h, w_ref[:, 0:d]) * (DIFF_HD ** -0.5)).astype(BF16)
    k = _dot(h, w_ref[:, d:2 * d])
    v = _dot(h, w_ref[:, 2 * d:3 * d])
    k_ref[...] = k.astype(BF16)
    v_ref[...] = v.astype(BF16)
    kc_ref[...] = k.T
    _to_token_major(vc_ref, v)


def _inproj1_sample_kernel(x_ref, mod_ref, nw_ref, w_ref, cos_ref, sin_ref, q_ref, k_ref, v_ref):
    d = D_MODEL
    h = _modulate(x_ref[...], nw_ref[...], mod_ref[0, 0:1, :], mod_ref[0, 1:2, :]).astype(BF16)
    cos, sin = cos_ref[...], sin_ref[...]
    q_ref[...] = (_rope(_dot(h, w_ref[:, 0:d]), cos, sin) * (DIFF_HD ** -0.5)).astype(BF16)
    k_ref[...] = _rope(_dot(h, w_ref[:, d:2 * d]), cos, sin).astype(BF16)
    v_ref[...] = _dot(h, w_ref[:, 2 * d:3 * d]).astype(BF16)


def _inproj1(x, mods, norm_w, w_bf16, n_prompt_tiles, n_sample_tiles, tiles_per_sample,
             cos_t, sin_t):
    d = D_MODEL
    npt, nst = n_prompt_tiles, n_sample_tiles
    common = [pl.BlockSpec((1, d), lambda i: (0, 0)), pl.BlockSpec((d, 3 * d), lambda i: (0, 0))]
    tile = lambda i: (i, 0)
    out_specs = tuple(pl.BlockSpec((ATTN_TILE, d), tile) for _ in range(3))
    qp, kp, vp, k_cache, v_cache = pl.pallas_call(
        _inproj1_prompt_kernel,
        grid=(npt,),
        in_specs=[pl.BlockSpec((ATTN_TILE, d), tile),
                  pl.BlockSpec((1, 6, d), lambda i: (8, 0, 0))] + common,
        out_specs=out_specs + (pl.BlockSpec((d, ATTN_TILE), tile),
                               pl.BlockSpec((ATTN_TILE * TOKEN_ROWS, LANES), tile)),
        out_shape=tuple(jax.ShapeDtypeStruct((npt * ATTN_TILE, d), BF16) for _ in range(3))
        + (jax.ShapeDtypeStruct((npt * d, ATTN_TILE), F32),
           jax.ShapeDtypeStruct((npt * ATTN_TILE * TOKEN_ROWS, LANES), F32)),
        compiler_params=_params(1),
        name="inproj1_prompt",
    )(x, mods, norm_w, w_bf16)
    rope_tile = lambda i: (i % tiles_per_sample, 0)
    qs, ks, vs = pl.pallas_call(
        _inproj1_sample_kernel,
        grid=(nst,),
        in_specs=[pl.BlockSpec((ATTN_TILE, d), lambda i: (npt + i, 0)),
                  pl.BlockSpec((1, 6, d), lambda i: (8 + 1 + i // tiles_per_sample, 0, 0))]
        + common + [pl.BlockSpec((ATTN_TILE, LANES), rope_tile),
                    pl.BlockSpec((ATTN_TILE, LANES), rope_tile)],
        out_specs=out_specs,
        out_shape=tuple(jax.ShapeDtypeStruct((nst * ATTN_TILE, d), BF16) for _ in range(3)),
        compiler_params=_params(1),
        name="inproj1_sample",
    )(x, mods, norm_w, w_bf16, cos_t, sin_t)
    return (qp, kp, vp), (qs, ks, vs), (k_cache, v_cache)


def _rope_tables(n_tok):
    half = DIFF_HD // 4
    pos = np.arange(n_tok)
    lane = np.arange(LANES)
    sub = lane % DIFF_HD
    p = np.where(sub[None, :] < DIFF_HD // 2, (pos // GRID_W)[:, None], (pos % GRID_W)[:, None])
    inv = jnp.asarray(ROPE_THETA, F32) ** (-jnp.asarray(sub % half, F32) / half)
    ang = jnp.asarray(p, F32) * inv[None, :]
    sign = np.where((lane % (2 * half)) < half, -1.0, 1.0).astype(np.float32)
    return jnp.cos(ang), jnp.sin(ang) * sign[None, :]


def _diffattn_kernel(*refs, has_cache, lam_init):
    if has_cache:
        q_ref, k_ref, v_ref, ck_ref, cv_ref, lam_ref, sw_ref, o_ref = refs
    else:
        q_ref, k_ref, v_ref, lam_ref, sw_ref, o_ref = refs
    hd2 = 2 * DIFF_HD
    lv = lam_ref[...]
    lam = (jnp.exp(jnp.sum(lv[0:1] * lv[1:2], axis=1, keepdims=True))
           - jnp.exp(jnp.sum(lv[2:3] * lv[3:4], axis=1, keepdims=True)) + lam_init)
    lane = lax.broadcasted_iota(I32, (q_ref.shape[0], hd2), 1)
    for h in range(DIFF_HEADS):
        cols = slice(h * hd2, (h + 1) * hd2)
        q = q_ref[:, cols]
        zero = jnp.zeros_like(q)
        k_new = k_ref[:, cols].astype(BF16)
        values = [v_ref[:, cols].astype(BF16)]
        if has_cache:
            past = ck_ref.shape[1]
            k_past_t = ck_ref[cols, :].astype(BF16)
            values.append(cv_ref[pl.ds(h, past, stride=DIFF_HEADS), :].astype(BF16))
        o = None
        for c in range(2):
            qc = jnp.where((lane < DIFF_HD) == (c == 0), q, zero)
            s = [_dot_nt(qc, k_new)]
            if has_cache:
                s.append(_dot(qc, k_past_t))
            mx = functools.reduce(jnp.maximum, [jnp.max(si, axis=1, keepdims=True) for si in s])
            e = [jnp.exp(si - mx) for si in s]
            z = functools.reduce(jnp.add, [jnp.sum(ei, axis=1, keepdims=True) for ei in e])
            pv = functools.reduce(jnp.add, [_dot(ei.astype(BF16), v) for ei, v in zip(e, values)])
            pv = pv * (1.0 / z)
            o = pv if c == 0 else o - lam * pv
        o_ref[:, cols] = ((_rms(o) * sw_ref[...]) * (1.0 - lam_init)).astype(BF16)


def _diffattn(q, k, v, lam_vecs, subln_w, batch, seq_len, q_block, lam_init, cache=None):
    d = D_MODEL
    nq = seq_len // q_block
    has_cache = cache is not None
    kv_spec = pl.BlockSpec((seq_len, d), lambda b, qi: (b, 0))
    in_specs = [pl.BlockSpec((q_block, d), lambda b, qi: (b * nq + qi, 0)), kv_spec, kv_spec]
    args = [q, k, v]
    if has_cache:
        past = cache[0].shape[1]
        in_specs += [pl.BlockSpec((d, past), lambda b, qi: (b, 0)),
                     pl.BlockSpec((past * DIFF_HEADS, 2 * DIFF_HD), lambda b, qi: (b, 0))]
        args += list(cache)
    in_specs += [pl.BlockSpec((4, DIFF_HD), lambda b, qi: (0, 0)),
                 pl.BlockSpec((1, 2 * DIFF_HD), lambda b, qi: (0, 0))]
    args += [lam_vecs, subln_w]
    return pl.pallas_call(
        functools.partial(_diffattn_kernel, has_cache=has_cache, lam_init=lam_init),
        grid=(batch, nq),
        in_specs=in_specs,
        out_specs=pl.BlockSpec((q_block, d), lambda b, qi: (b * nq + qi, 0)),
        out_shape=jax.ShapeDtypeStruct((batch * seq_len, d), BF16),
        compiler_params=_params(2),
        name="diffattn_cache" if has_cache else "diffattn",
    )(*args)


def _router_weights(router_group, router_expert):
    w = jnp.concatenate([router_group, router_expert], axis=1)
    w = jnp.pad(w, ((0, 0), (0, LANES - w.shape[1])))
    hi = w.astype(BF16)
    return hi, (w - hi.astype(F32)).astype(BF16)


def _inproj0_weights(w_in):
    gq, gk, gv, gg, gaf, gab, hq, hff, hfb, hi, hg = jnp.split(
        w_in, [256, 512, 1024, 1536, 1552, 1568, 1824, 2080, 2336, 2848], axis=1)
    w = jnp.concatenate([gq, gk, gv, gg, hq, hff, hfb, hi, hg, gaf, gab], axis=1)
    return jnp.pad(w, ((0, 0), (0, AB_COLS - w.shape[1]))).astype(BF16)


def kernel(x_prompt, x_sample, state_gla, state_hgrn, cache_diff_k, cache_diff_v, c, c_ctx,
           w_ada, b_ada, norm1_w, norm2_w, w_in_ab, gla_a2, gla_a_bias, hgrn_lb, gla_onorm_w,
           hgrn_onorm_w, w_out_ab, w_in_c, lam_q1, lam_k1, lam_q2, lam_k2, diff_subln_w, w_out_c,
           router_group, router_expert, moe_w1, moe_w3, moe_w2, final_norm_w):
    bp, lp, d = x_prompt.shape
    bs, ls, _ = x_sample.shape
    depth = w_ada.shape[0]
    assert depth == 2 and d == D_MODEL and bs <= 7
    tp, ts = bp * lp, bs * ls
    npt, nst = tp // ROW_TILE, ts // ROW_TILE
    tps = ls // ROW_TILE
    xp = x_prompt.reshape(tp, d)
    xs = x_sample.reshape(ts, d)

    cond8 = jnp.concatenate([c_ctx[None, :], c, jnp.zeros((7 - bs, d), F32)], axis=0)
    mods = _adaln(cond8, w_ada, b_ada).reshape(depth * 8, 6, d)

    proj = _inproj0(xp, xs, mods, norm1_w[0:1], _inproj0_weights(w_in_ab[0]), ls)
    a_bias = gla_a_bias[0][:, None, :]
    scan_args = (gla_a2[0], a_bias, hgrn_lb, gla_onorm_w[0:1], hgrn_onorm_w[0:1])
    mixed_p, new_state_gla, new_state_hgrn = _scan(proj, 0, bp, lp, *scan_args)
    s0 = jnp.concatenate([state_gla[:, 0], state_hgrn[:, 0]], axis=2).swapaxes(-1, -2)
    s0 = s0.reshape(bs, 2, SCAN_PAIRS, 2, HEAD_DV, HEAD_DK)
    zero = jnp.zeros_like(s0[:, :, :, 0])
    s0 = jnp.concatenate([jnp.concatenate([s0[:, :, :, 0], zero], axis=-1),
                          jnp.concatenate([zero, s0[:, :, :, 1]], axis=-1)], axis=-2)
    mixed_s = _scan(proj, tp, bs, ls, *scan_args, s0=s0)

    wr = _router_weights(router_group[0], router_expert[0])
    x1, *routed = _post((xp, xs), mixed_p, mixed_s, mods, 0, norm2_w[0:1],
                        w_out_ab[0].astype(BF16), *wr, tps)
    ys, tables = _moe(*routed, moe_w1, moe_w3, moe_w2, 0)
    x2 = _combine(*tables, x1, mods, 0, final_norm_w[None, :], ys, 0, npt + nst, npt, tps, False)

    lam_init = 0.8 - 0.6 * math.exp(-0.3 * 1)
    cos_t, sin_t = _rope_tables(ls)
    (qp, kp, vp), (qs, ks, vs), (k_cache, v_cache) = _inproj1(
        x2, mods, norm1_w[1:2], w_in_c[0].astype(BF16), tp // ATTN_TILE, ts // ATTN_TILE,
        ls // ATTN_TILE, cos_t, sin_t)
    lam_vecs = jnp.stack([lam_q1[0], lam_k1[0], lam_q2[0], lam_k2[0]])
    att_p = _diffattn(qp, kp, vp, lam_vecs, diff_subln_w[0:1], bp, lp, lp, lam_init)
    past = cache_diff_k.shape[2]
    assert lp == ATTN_TILE and DIFF_HEADS == TOKEN_ROWS
    cache = (cache_diff_k[:, 0].transpose(0, 2, 3, 4, 1).reshape(bs * d, past),
             cache_diff_v[:, 0].reshape(bs * past * DIFF_HEADS, 2 * DIFF_HD))
    att_s = _diffattn(qs, ks, vs, lam_vecs, diff_subln_w[0:1], bs, ls, SAMPLE_Q_BLOCK, lam_init,
                      cache)

    wr = _router_weights(router_group[1], router_expert[1])
    x3, *routed = _post((x2,), att_p, att_s, mods, 1, norm2_w[1:2],
                        w_out_c[0].astype(BF16), *wr, tps)
    ys, tables = _moe(*routed, moe_w1, moe_w3, moe_w2, 1)
    fw = final_norm_w[None, :]
    y_p = _combine(*tables, x3, mods, 1, fw, ys, 0, npt, npt, tps, True)
    y_s = _combine(*tables, x3, mods, 1, fw, ys, npt, nst, npt, tps, True)

    return (y_p.reshape(bp, lp, d), y_s.reshape(bs, ls, d), new_state_gla, new_state_hgrn,
            k_cache.reshape(bp, 1, DIFF_HEADS, 2, DIFF_HD, lp).transpose(0, 1, 5, 2, 3, 4),
            v_cache.reshape(bp, 1, lp, DIFF_HEADS, 2 * DIFF_HD))
```

```python
import functools
import math

import jax
import jax.numpy as jnp
import numpy as np
from jax import lax
from jax.experimental import pallas as pl
from jax.experimental.pallas import tpu as pltpu

F32 = jnp.float32
BF16 = jnp.bfloat16
I32 = jnp.int32

D_MODEL = 1024
GLA_HEADS = 4
HGRN_HEADS = 4
SCAN_HEADS = GLA_HEADS + HGRN_HEADS
SCAN_PAIRS = SCAN_HEADS // 2
HEAD_DK = 64
HEAD_DV = 128
GATE_RANK = 16
GLA_GATE_NORM = 16.0
DIFF_HEADS = 8
DIFF_HD = 64
GRID_W = 64
ROPE_THETA = 10000.0
N_GROUPS = 4
EXPERTS_PER_GROUP = 8
N_EXPERTS = N_GROUPS * EXPERTS_PER_GROUP
MOE_HIDDEN = 512
EPS = 1e-6
LANES = 128
TOKEN_ROWS = D_MODEL // LANES
NEG_BIG = -1e30
ROUTER_ROWS = 48

ROW_TILE = 512
ATTN_TILE = 256
SAMPLE_Q_BLOCK = 512
ADA_TILE = 1536
INPROJ0_TILE = 512
SCAN_CHUNK = 64
EXPERT_TILE = 256
MOVE_BLOCK = 16
STAGE_TOKENS = 2 * ROW_TILE + N_EXPERTS * MOVE_BLOCK
MAX_BLOCKS = STAGE_TOKENS // MOVE_BLOCK
VMEM_LIMIT = 56 * 1024 * 1024
SCAN_INPUT_DOUBLE_BUFFER_BYTES = 8 * 1024 * 1024

_C_GQ, _C_GK, _C_GV, _C_GG = 0, 256, 512, 1024
_C_HQ, _C_HFF, _C_HFB, _C_HI, _C_HG = 1536, 1792, 2048, 2304, 2816
_C_GAF, _C_GAB = 3328, 3344
AB_COLS = 3456


def _params(n_axes, vmem=VMEM_LIMIT):
    return pltpu.CompilerParams(dimension_semantics=("arbitrary",) * n_axes,
                                vmem_limit_bytes=vmem)


def _cdiv(a, b):
    return (a + b - 1) // b


def _dot(a, b):
    return jnp.dot(a, b, preferred_element_type=F32)


def _dot_nt(a, b):
    return lax.dot_general(a, b, (((1,), (1,)), ((), ())), preferred_element_type=F32)


def _dot_tn(a, b):
    return lax.dot_general(a, b, (((0,), (0,)), ((), ())), preferred_element_type=F32)


def _split_bf16(x):
    hi = x.astype(BF16)
    lo = (x - hi.astype(F32)).astype(BF16)
    return hi, lo


def _silu(x):
    return x * jax.nn.sigmoid(x)


def _log_sigmoid(x):
    return jnp.minimum(x, 0.0) - jnp.log(1.0 + jnp.exp(-jnp.abs(x)))


def _rms(x):
    return x * lax.rsqrt(jnp.mean(x * x, axis=-1, keepdims=True) + EPS)


def _modulate(x, norm_w, shift, scale):
    return (_rms(x) * norm_w) * (1.0 + scale) + shift


def _to_token_major(dst_ref, x, row0=0):
    n = x.shape[0]
    for s in range(TOKEN_ROWS):
        dst_ref[pl.ds(row0 + s, n, stride=TOKEN_ROWS), :] = x[:, s * LANES:(s + 1) * LANES]


def _from_token_major(src_ref, n, row0=0):
    return jnp.concatenate([src_ref[pl.ds(row0 + s, n, stride=TOKEN_ROWS), :]
                            for s in range(TOKEN_ROWS)], axis=1)


def _ada_kernel(c_ref, w_ref, b_ref, o_ref):
    s = _silu(c_ref[...])
    o_ref[0] = _dot(s.astype(BF16), w_ref[0].astype(BF16)) + b_ref[0]


def _adaln(cond8, w_ada, b_ada):
    depth, d, n = w_ada.shape
    tn = ADA_TILE
    return pl.pallas_call(
        _ada_kernel,
        grid=(depth, n // tn),
        in_specs=[pl.BlockSpec((8, d), lambda l, j: (0, 0)),
                  pl.BlockSpec((1, d, tn), lambda l, j: (l, 0, j)),
                  pl.BlockSpec((1, 1, tn), lambda l, j: (l, 0, j))],
        out_specs=pl.BlockSpec((1, 8, tn), lambda l, j: (l, 0, j)),
        out_shape=jax.ShapeDtypeStruct((depth, 8, n), F32),
        compiler_params=_params(2),
        name="adaln",
    )(cond8, w_ada, b_ada.reshape(depth, 1, n))


def _mod_row(i, layer, n_prompt_tiles, tiles_per_sample):
    r = jnp.where(i < n_prompt_tiles, 0, 1 + (i - n_prompt_tiles) // tiles_per_sample)
    return layer * 8 + r


def _inproj0_kernel(xp_ref, xs_ref, mod_ref, nw_ref, w_ref, o_ref, *, n_prompt_tiles):
    i = pl.program_id(0)
    x = jnp.where(i < n_prompt_tiles, xp_ref[...], xs_ref[...])
    h = _modulate(x, nw_ref[...], mod_ref[0, 0:1, :], mod_ref[0, 1:2, :])
    o_ref[...] = _dot(h.astype(BF16), w_ref[...])


def _inproj0(xp, xs, mods, norm_w, w_bf16, sample_len):
    tp, d = xp.shape
    ts = xs.shape[0]
    n = w_bf16.shape[1]
    tile = INPROJ0_TILE
    npt, nst = tp // tile, ts // tile
    mod_map = lambda i: (_mod_row(i, 0, npt, sample_len // tile), 0, 0)
    return pl.pallas_call(
        functools.partial(_inproj0_kernel, n_prompt_tiles=npt),
        grid=(npt + nst,),
        in_specs=[pl.BlockSpec((tile, d), lambda i: (jnp.minimum(i, npt - 1), 0)),
                  pl.BlockSpec((tile, d), lambda i: (jnp.maximum(i - npt, 0), 0)),
                  pl.BlockSpec((1, 6, d), mod_map),
                  pl.BlockSpec((1, d), lambda i: (0, 0)),
                  pl.BlockSpec((d, n), lambda i: (0, 0))],
        out_specs=pl.BlockSpec((tile, n), lambda i: (i, 0)),
        out_shape=jax.ShapeDtypeStruct((tp + ts, n), F32),
        compiler_params=_params(1),
        name="inproj0",
    )(xp, xs, mods, norm_w, w_bf16)


def _scan_kernel(*refs, seq_len, has_state):
    if has_state:
        (p_ref, a2_ref, ab_ref, lb_ref, ong_ref, onh_ref, s0_ref, mixed_ref, *scratch) = refs
        sfin_ref = None
    else:
        (p_ref, a2_ref, ab_ref, lb_ref, ong_ref, onh_ref, mixed_ref, sg_ref, sh_ref,
         *scratch) = refs
        s0_ref = None
        sfin_ref = (sg_ref, sh_ref)
    (qi_f, ki_f, qo_f, ko_f, qi_b, ki_b, qo_b, ko_b,
     vv, dec_f, dec_b, o_f, o_b, st_f, st_b) = scratch
    C = SCAN_CHUNK
    n_chunks = seq_len // C
    gqk = GLA_HEADS * HEAD_DK

    row = lax.broadcasted_iota(I32, (C, C), 0)
    col = lax.broadcasted_iota(I32, (C, C), 1)
    lower = col <= row
    upper = col >= row
    tri_lo = jnp.where(lower, 1.0, 0.0).astype(BF16)
    tri_up = jnp.where(upper, 1.0, 0.0).astype(BF16)

    lbp = lb_ref[...]
    lb_max = jnp.maximum(lbp[0], lbp[1])
    lb_e0 = jnp.exp(lbp[0] - lb_max)
    lb_e1 = jnp.exp(lbp[1] - lb_max)
    lb = lb_e0 / (lb_e0 + lb_e1)

    def cumsum_chunk(tri, la):
        hi, lo = _split_bf16(la)
        return _dot(tri, hi) + _dot(tri, lo)

    def prep(n, carry):
        r0 = pl.multiple_of(n * C, C)
        rows = pl.ds(r0, C)
        gq = p_ref[rows, _C_GQ:_C_GQ + gqk] * (HEAD_DK ** -0.5)
        gk = p_ref[rows, _C_GK:_C_GK + gqk]
        hq = _silu(p_ref[rows, _C_HQ:_C_HQ + gqk]) * (HEAD_DK ** -0.5)
        for d_i, (qi_s, ki_s, qo_s, ko_s, dec_s, tri, last, mid) in enumerate(
                ((qi_f, ki_f, qo_f, ko_f, dec_f, tri_lo, C - 1, C // 2 - 1),
                 (qi_b, ki_b, qo_b, ko_b, dec_b, tri_up, 0, C // 2))):
            c_ga = _C_GAF if d_i == 0 else _C_GAB
            c_hf = _C_HFF if d_i == 0 else _C_HFB
            ga = p_ref[rows, c_ga:c_ga + GATE_RANK]
            xg = _dot(ga.astype(BF16), a2_ref[d_i].astype(BF16)) + ab_ref[d_i]
            la_g = _log_sigmoid(xg) / GLA_GATE_NORM
            f = lb[d_i:d_i + 1, :] + (1.0 - lb[d_i:d_i + 1, :]) * jax.nn.sigmoid(
                p_ref[rows, c_hf:c_hf + gqk])
            la_h = jnp.log(f)
            for q, k, la, c0 in ((gq, gk, la_g, 0), (hq, 1.0 - f, la_h, gqk)):
                b = cumsum_chunk(tri, la)
                b_mid, b_end = b[mid:mid + 1, :], b[last:last + 1, :]
                cs = slice(c0, c0 + gqk)
                qi_s[rows, cs] = (q * jnp.exp(b - b_mid)).astype(BF16)
                ki_s[rows, cs] = (k * jnp.exp(b_mid - b)).astype(BF16)
                qo_s[rows, cs] = (q * jnp.exp(b)).astype(BF16)
                ko_s[rows, cs] = (k * jnp.exp(b_end - b)).astype(BF16)
                dec_s[n, :, cs] = jnp.exp(b_end)
        gv_cols = GLA_HEADS * HEAD_DV
        vv[rows, 0:gv_cols] = p_ref[rows, _C_GV:_C_GV + gv_cols].astype(BF16)
        vv[rows, gv_cols:] = p_ref[rows, _C_HI:_C_HI + HGRN_HEADS * HEAD_DV].astype(BF16)
        return carry

    lax.fori_loop(0, n_chunks, prep, 0, unroll=2)

    for p in range(SCAN_PAIRS):
        if has_state:
            st_f[p] = s0_ref[0, 0, p]
            st_b[p] = s0_ref[0, 1, p]
        else:
            st_f[p] = jnp.zeros((2 * HEAD_DV, 2 * HEAD_DK), F32)
            st_b[p] = jnp.zeros((2 * HEAD_DV, 2 * HEAD_DK), F32)

    first_head = lax.broadcasted_iota(I32, (C, 2 * HEAD_DK), 1) < HEAD_DK
    row2 = lax.broadcasted_iota(I32, (2 * C, C), 0) % C
    col2 = lax.broadcasted_iota(I32, (2 * C, C), 1)
    lower2 = col2 <= row2
    upper2 = col2 >= row2

    def per_head_rows(x):
        z = jnp.zeros_like(x)
        return jnp.concatenate([jnp.where(first_head, x, z), jnp.where(first_head, z, x)], axis=0)

    def put_out(o_ref, rows, p, res):
        c0 = p * 2 * HEAD_DV
        o_ref[rows, c0:c0 + HEAD_DV] = res[0:C, 0:HEAD_DV]
        o_ref[rows, c0 + HEAD_DV:c0 + 2 * HEAD_DV] = res[C:2 * C, HEAD_DV:2 * HEAD_DV]

    def sweep(n, carry):
        m = n_chunks - 1 - n
        rows = pl.ds(pl.multiple_of(n * C, C), C)
        rows_m = pl.ds(pl.multiple_of(m * C, C), C)
        decay_f, decay_b = dec_f[n], dec_b[m]
        for p in range(SCAN_PAIRS):
            ks = slice(p * 2 * HEAD_DK, (p + 1) * 2 * HEAD_DK)
            vs = slice(p * 2 * HEAD_DV, (p + 1) * 2 * HEAD_DV)
            vh = vv[rows, vs]
            s_f = st_f[p]
            sc = (jnp.where(lower2, _dot_nt(per_head_rows(qi_f[rows, ks]), ki_f[rows, ks]), 0.0)
                  + jnp.where(upper2, _dot_nt(per_head_rows(qi_b[rows, ks]), ki_b[rows, ks]), 0.0))
            put_out(o_f, rows, p, _dot_nt(per_head_rows(qo_f[rows, ks]), s_f.astype(BF16))
                    + _dot(sc.astype(BF16), vh))
            st_f[p] = decay_f[:, ks] * s_f + _dot_tn(vh, ko_f[rows, ks])
            s_b = st_b[p]
            put_out(o_b, rows_m, p, _dot_nt(per_head_rows(qo_b[rows_m, ks]), s_b.astype(BF16)))
            st_b[p] = decay_b[:, ks] * s_b + _dot_tn(vv[rows_m, vs], ko_b[rows_m, ks])
        return carry

    lax.fori_loop(0, n_chunks, sweep, 0)

    def finish(n, carry):
        rows = pl.ds(pl.multiple_of(n * C, C), C)
        for h in range(SCAN_HEADS):
            vs = slice(h * HEAD_DV, (h + 1) * HEAD_DV)
            if h < GLA_HEADS:
                gate = p_ref[rows, _C_GG + h * HEAD_DV:_C_GG + (h + 1) * HEAD_DV]
                onw = ong_ref[...]
            else:
                hh = h - GLA_HEADS
                gate = p_ref[rows, _C_HG + hh * HEAD_DV:_C_HG + (hh + 1) * HEAD_DV]
                onw = onh_ref[...]
            o = o_f[rows, vs] + o_b[rows, vs]
            mixed_ref[rows, vs] = ((_rms(o) * onw) * _silu(gate)).astype(BF16)
        return carry

    lax.fori_loop(0, n_chunks, finish, 0, unroll=2)

    if sfin_ref is not None:
        for d_i, st in enumerate((st_f, st_b)):
            for p in range(SCAN_PAIRS):
                s_pair = st[p].T
                out_ref = sfin_ref[(2 * p) // GLA_HEADS]
                h0 = (2 * p) % GLA_HEADS
                out_ref[0, 0, d_i, h0] = s_pair[0:HEAD_DK, 0:HEAD_DV]
                out_ref[0, 0, d_i, h0 + 1] = s_pair[HEAD_DK:2 * HEAD_DK, HEAD_DV:2 * HEAD_DV]


def _scan(p, row0, batch, seq_len, a2, a_bias, lb, onorm_g, onorm_h, s0=None):
    n = p.shape[1]
    assert row0 % seq_len == 0
    blk0 = row0 // seq_len
    has_state = s0 is not None
    n_chunks = seq_len // SCAN_CHUNK
    assert GLA_HEADS == HGRN_HEADS and GLA_HEADS % 2 == 0
    st_shape = (1, 1, 2, GLA_HEADS, HEAD_DK, HEAD_DV)
    pair_shape = (SCAN_PAIRS, 2 * HEAD_DV, 2 * HEAD_DK)
    p_mode = dict(pipeline_mode=pl.Buffered(1)) if seq_len * n * 4 > SCAN_INPUT_DOUBLE_BUFFER_BYTES else {}
    in_specs = [pl.BlockSpec((seq_len, n), lambda b: (blk0 + b, 0), **p_mode),
                pl.BlockSpec(a2.shape, lambda b: (0, 0, 0)),
                pl.BlockSpec(a_bias.shape, lambda b: (0, 0, 0)),
                pl.BlockSpec(lb.shape, lambda b: (0, 0, 0)),
                pl.BlockSpec((1, HEAD_DV), lambda b: (0, 0)),
                pl.BlockSpec((1, HEAD_DV), lambda b: (0, 0))]
    args = [p, a2, a_bias, lb, onorm_g, onorm_h]
    mixed_shape = jax.ShapeDtypeStruct((batch * seq_len, D_MODEL), BF16)
    mixed_spec = pl.BlockSpec((seq_len, D_MODEL), lambda b: (b, 0))
    if has_state:
        in_specs.append(pl.BlockSpec((1, 2) + pair_shape, lambda b: (b, 0, 0, 0, 0)))
        args.append(s0)
        out_shape, out_specs = mixed_shape, mixed_spec
    else:
        st_struct = jax.ShapeDtypeStruct((batch,) + st_shape[1:], F32)
        st_spec = pl.BlockSpec(st_shape, lambda b: (b, 0, 0, 0, 0, 0))
        out_shape = (mixed_shape, st_struct, st_struct)
        out_specs = (mixed_spec, st_spec, st_spec)
    qk_cols = SCAN_HEADS * HEAD_DK
    scratch = [pltpu.VMEM((seq_len, qk_cols), BF16) for _ in range(8)]
    scratch += [pltpu.VMEM((seq_len, D_MODEL), BF16),
                pltpu.VMEM((n_chunks, 1, qk_cols), F32), pltpu.VMEM((n_chunks, 1, qk_cols), F32),
                pltpu.VMEM((seq_len, D_MODEL), F32), pltpu.VMEM((seq_len, D_MODEL), F32),
                pltpu.VMEM(pair_shape, F32), pltpu.VMEM(pair_shape, F32)]
    return pl.pallas_call(
        functools.partial(_scan_kernel, seq_len=seq_len, has_state=has_state),
        grid=(batch,),
        in_specs=in_specs, out_specs=out_specs, out_shape=out_shape,
        scratch_shapes=scratch,
        compiler_params=_params(1),
        name="scan_state" if has_state else "scan_fresh",
    )(*args)


def _post_kernel(*refs, split_x, n_prompt_tiles):
    if split_x:
        xp_ref, xs_ref = refs[0], refs[1]
        refs = refs[2:]
    else:
        x_ref = refs[0]
        refs = refs[1:]
    (mp_ref, ms_ref, mod_ref, nw_ref, wo_ref, wrh_ref, wrl_ref,
     x1_ref, h2_ref, slot_ref, wgt_ref, tab_ref, carry) = refs
    i = pl.program_id(0)
    is_prompt = i < n_prompt_tiles
    if split_x:
        x = jnp.where(is_prompt, xp_ref[...], xs_ref[...])
    else:
        x = x_ref[...]
    mixed = jnp.where(is_prompt, mp_ref[...], ms_ref[...])
    x1 = x + mod_ref[0, 2:3, :] * _dot(mixed, wo_ref[...])
    x1_ref[...] = x1
    h2 = _modulate(x1, nw_ref[...], mod_ref[0, 3:4, :], mod_ref[0, 4:5, :])
    _to_token_major(h2_ref, h2)

    hh, hl = _split_bf16(h2)
    logits = _dot(hh, wrh_ref[...]) + _dot(hl, wrh_ref[...]) + _dot(hh, wrl_ref[...])
    tm = logits.shape[0]
    lt = logits.T[0:ROUTER_ROWS]
    ridx = lax.broadcasted_iota(I32, (ROUTER_ROWS, tm), 0).astype(F32)

    def first_max(v):
        mx = jnp.max(v, axis=0, keepdims=True)
        idx = jnp.min(jnp.where(v == mx, ridx, float(ROUTER_ROWS)), axis=0, keepdims=True)
        return mx, idx

    gl = jnp.where(ridx < N_GROUPS, lt, NEG_BIG)
    gmax, gidx = first_max(gl)
    g_val = 1.0 / jnp.sum(jnp.exp(gl - gmax), axis=0, keepdims=True)
    lo = N_GROUPS + EXPERTS_PER_GROUP * gidx
    el = jnp.where((ridx >= lo) & (ridx < lo + EXPERTS_PER_GROUP), lt, NEG_BIG)
    emax, l1 = first_max(el)
    esum = jnp.sum(jnp.exp(el - emax), axis=0, keepdims=True)
    e2max, l2 = first_max(jnp.where(ridx == l1, NEG_BIG, el))
    p1 = 1.0 / esum
    p2 = jnp.exp(e2max - emax) / esum
    w1 = g_val * (p1 / (p1 + p2))
    w2 = g_val * (p2 / (p1 + p2))

    @pl.when(i == 0)
    def _():
        carry[...] = jnp.zeros_like(carry)

    sel1 = ridx == l1
    sel2 = ridx == l2
    onehot = jnp.where(sel1 | sel2, 1.0, 0.0)
    t_row = lax.broadcasted_iota(I32, (tm, tm), 0)
    t_col = lax.broadcasted_iota(I32, (tm, tm), 1)
    earlier = jnp.where(t_row < t_col, 1.0, 0.0).astype(BF16)
    before = _dot(onehot.astype(BF16), earlier)
    count = jnp.sum(onehot, axis=1, keepdims=True)
    blocks = jnp.floor((count + (MOVE_BLOCK - 1.0)) * (1.0 / MOVE_BLOCK)) * MOVE_BLOCK
    r_row = lax.broadcasted_iota(I32, (ROUTER_ROWS, ROUTER_ROWS), 0)
    r_col = lax.broadcasted_iota(I32, (ROUTER_ROWS, ROUTER_ROWS), 1)
    lower_rows = jnp.where(r_col < r_row, 1.0, 0.0).astype(BF16)
    run_start = _dot(lower_rows,
                     jnp.broadcast_to(blocks, (ROUTER_ROWS, LANES)).astype(BF16))[:, 0:1]
    slot = before + run_start
    q1 = jnp.sum(jnp.where(sel1, slot, 0.0), axis=0, keepdims=True)
    q2 = jnp.sum(jnp.where(sel2, slot, 0.0), axis=0, keepdims=True)
    tab_lane = lax.broadcasted_iota(I32, (ROUTER_ROWS, LANES), 1)
    tab_ref[0] = jnp.where(tab_lane == 0, count,
                           jnp.where(tab_lane == 1, carry[...],
                                     jnp.where(tab_lane == 2, run_start, 0.0)))
    carry[...] = carry[...] + count

    slot_ref[0, 0:1, :] = (q1 * TOKEN_ROWS).astype(I32)
    slot_ref[0, 1:2, :] = (q2 * TOKEN_ROWS).astype(I32)
    wgt_ref[0, 0:1, :] = w1
    wgt_ref[0, 1:2, :] = w2


def _post(x_args, mixed_p, mixed_s, mods, layer, norm_w, w_out_bf16, wr_hi, wr_lo,
          tiles_per_sample):
    split_x = len(x_args) == 2
    tp, ts = mixed_p.shape[0], mixed_s.shape[0]
    t, d = tp + ts, D_MODEL
    npt, nst = tp // ROW_TILE, ts // ROW_TILE
    tile = lambda i: (i, 0)
    if split_x:
        x_specs = [pl.BlockSpec((ROW_TILE, d), lambda i: (jnp.minimum(i, npt - 1), 0)),
                   pl.BlockSpec((ROW_TILE, d), lambda i: (jnp.maximum(i - npt, 0), 0))]
    else:
        x_specs = [pl.BlockSpec((ROW_TILE, d), tile)]
    in_specs = x_specs + [
        pl.BlockSpec((ROW_TILE, d), lambda i: (jnp.minimum(i, npt - 1), 0)),
        pl.BlockSpec((ROW_TILE, d), lambda i: (jnp.maximum(i - npt, 0), 0)),
        pl.BlockSpec((1, 6, d), lambda i: (_mod_row(i, layer, npt, tiles_per_sample), 0, 0)),
        pl.BlockSpec((1, d), lambda i: (0, 0)),
        pl.BlockSpec((d, d), lambda i: (0, 0)),
        pl.BlockSpec((d, LANES), lambda i: (0, 0)),
        pl.BlockSpec((d, LANES), lambda i: (0, 0))]
    return pl.pallas_call(
        functools.partial(_post_kernel, split_x=split_x, n_prompt_tiles=npt),
        grid=(npt + nst,),
        in_specs=in_specs,
        out_specs=(pl.BlockSpec((ROW_TILE, d), tile),
                   pl.BlockSpec((ROW_TILE * TOKEN_ROWS, LANES), tile),
                   pl.BlockSpec((1, 2, ROW_TILE), lambda i: (i, 0, 0)),
                   pl.BlockSpec((1, 2, ROW_TILE), lambda i: (i, 0, 0)),
                   pl.BlockSpec((1, ROUTER_ROWS, LANES), lambda i: (i, 0, 0))),
        out_shape=(jax.ShapeDtypeStruct((t, d), F32),
                   jax.ShapeDtypeStruct((t * TOKEN_ROWS, LANES), F32),
                   jax.ShapeDtypeStruct((npt + nst, 2, ROW_TILE), I32),
                   jax.ShapeDtypeStruct((npt + nst, 2, ROW_TILE), F32),
                   jax.ShapeDtypeStruct((npt + nst, ROUTER_ROWS, LANES), F32)),
        scratch_shapes=[pltpu.VMEM((ROUTER_ROWS, LANES), F32)],
        compiler_params=_params(1),
        name=f"post{layer}",
    )(*x_args, mixed_p, mixed_s, mods, norm_w, w_out_bf16, wr_hi, wr_lo)


def _for_blocks(tab_ref, fn):
    block_rows = MOVE_BLOCK * TOKEN_ROWS
    count = tab_ref[0, 0, MAX_BLOCKS]

    def call(k, parity):
        fn(pl.multiple_of(k * block_rows, block_rows),
           pl.multiple_of(tab_ref[0, 0, k], TOKEN_ROWS), parity)

    def body(k2, c):
        call(2 * k2, 0)

        @pl.when(2 * k2 + 1 < count)
        def _():
            call(2 * k2 + 1, 1)
        return c

    lax.fori_loop(0, _cdiv(count, 2), body, 0)


def _wait_blocks(tab_ref, copy):
    def body(k, c):
        copy.wait()
        return c

    lax.fori_loop(0, tab_ref[0, 0, MAX_BLOCKS], body, 0)


def _dispatch_kernel(zero_ref, tab_ref, prev_tab_ref, q_ref, h2_ref, hs_ref, zero_buf, stage, sem):
    j = pl.program_id(0)
    slot = j % 2
    block_rows = MOVE_BLOCK * TOKEN_ROWS

    @pl.when(j == 0)
    def _():
        zero_buf[...] = jnp.zeros_like(zero_buf)

        def zero_copy(k):
            start = pl.multiple_of(zero_ref[k], EXPERT_TILE * TOKEN_ROWS)
            return pltpu.make_async_copy(
                zero_buf, hs_ref.at[pl.ds(start, EXPERT_TILE * TOKEN_ROWS)], sem.at[0])

        def start_zero(k, c):
            @pl.when(zero_ref[k] >= 0)
            def _():
                zero_copy(k).start()
            return c

        def wait_zero(k, c):
            @pl.when(zero_ref[k] >= 0)
            def _():
                zero_copy(k).wait()
            return c

        lax.fori_loop(0, zero_ref.shape[0], start_zero, 0)
        lax.fori_loop(0, zero_ref.shape[0], wait_zero, 0)

        stage[...] = jnp.zeros_like(stage)

    def place(r, c):
        tok = h2_ref[pl.ds(pl.multiple_of(r * TOKEN_ROWS, TOKEN_ROWS), TOKEN_ROWS), :]
        for s in range(2):
            row = pl.multiple_of(q_ref[0, s, r], TOKEN_ROWS)
            stage[slot, pl.ds(row, TOKEN_ROWS), :] = tok
        return c

    lax.fori_loop(0, ROW_TILE, place, 0, unroll=8)

    def block_copy(buf, stage_row, sorted_row):
        return pltpu.make_async_copy(stage.at[buf, pl.ds(stage_row, block_rows)],
                                     hs_ref.at[pl.ds(sorted_row, block_rows)], sem.at[buf])

    @pl.when(j > 0)
    def _():
        _wait_blocks(prev_tab_ref, block_copy(1 - slot, 0, 0))

    _for_blocks(tab_ref, lambda a, b, parity: block_copy(slot, a, b).start(priority=parity))

    @pl.when(j == pl.num_programs(0) - 1)
    def _():
        _wait_blocks(tab_ref, block_copy(slot, 0, 0))


def _dispatch(zero_tiles, block_tab, slots, h2, n_rows):
    t = h2.shape[0] // TOKEN_ROWS
    nt = t // ROW_TILE
    smem_tile = lambda shape: pl.BlockSpec((1,) + shape, lambda j, *_: (j, 0, 0),
                                           memory_space=pltpu.SMEM)
    grid_spec = pltpu.PrefetchScalarGridSpec(
        num_scalar_prefetch=1,
        grid=(nt,),
        in_specs=[smem_tile((1, LANES)),
                  pl.BlockSpec((1, 1, LANES), lambda j, *_: (jnp.maximum(j - 1, 0), 0, 0),
                               memory_space=pltpu.SMEM),
                  smem_tile((2, ROW_TILE)),
                  pl.BlockSpec((ROW_TILE * TOKEN_ROWS, LANES), lambda j, *_: (j, 0))],
        out_specs=pl.BlockSpec(memory_space=pl.ANY),
        scratch_shapes=[pltpu.VMEM((EXPERT_TILE * TOKEN_ROWS, LANES), F32),
                        pltpu.VMEM((2, STAGE_TOKENS * TOKEN_ROWS, LANES), F32),
                        pltpu.SemaphoreType.DMA((2,))])
    return pl.pallas_call(
        _dispatch_kernel,
        grid_spec=grid_spec,
        out_shape=jax.ShapeDtypeStruct((n_rows * TOKEN_ROWS, LANES), F32),
        compiler_params=_params(1),
        name="dispatch",
    )(zero_tiles, block_tab, block_tab, slots, h2)


def _expert_kernel(te_ref, src_ref, nv_ref, run_ref, nxt_ref, hs_ref, w1_hbm, w3_hbm, w2_hbm,
                   ys_ref, w1f, w3f, w2f, w1b, w3b, w2b, sem, *, layer):
    i = pl.program_id(0)

    def weight_copies(e, buf):
        return [pltpu.make_async_copy(src.at[layer, e], dst.at[buf], sem.at[buf])
                for src, dst in ((w1_hbm, w1f), (w3_hbm, w3f), (w2_hbm, w2f))]

    @pl.when(i == 0)
    def _():
        for c in weight_copies(te_ref[0], 0):
            c.start()

    first = (i == 0) | (run_ref[i] != run_ref[jnp.maximum(i - 1, 0)])

    @pl.when(first)
    def _():
        buf = run_ref[i] % 2
        for c in weight_copies(te_ref[i], buf):
            c.wait()

        @pl.when(nxt_ref[i] >= 0)
        def _():
            for c in weight_copies(nxt_ref[i], 1 - buf):
                c.start()

        w1b[...] = w1f[buf].astype(BF16)
        w3b[...] = w3f[buf].astype(BF16)
        w2b[...] = w2f[buf].astype(BF16)

    @pl.when(nv_ref[i] > 0)
    def _():
        h = _from_token_major(hs_ref, EXPERT_TILE).astype(BF16)
        g = _silu(_dot(h, w1b[...])) * _dot(h, w3b[...])
        _to_token_major(ys_ref, _dot(g.astype(BF16), w2b[...]))

    @pl.when(nv_ref[i] == 0)
    def _():
        ys_ref[...] = jnp.zeros_like(ys_ref)


def _experts(tile_expert, tile_src, tile_rows, hs, w1, w3, w2, layer):
    n_rows, d = hs.shape[0] // TOKEN_ROWS, D_MODEL
    nt = n_rows // EXPERT_TILE
    hid = w1.shape[-1]
    tok_tile = (EXPERT_TILE * TOKEN_ROWS, LANES)
    changed = jnp.concatenate([jnp.zeros((1,), I32),
                               (tile_expert[1:] != tile_expert[:-1]).astype(I32)])
    run = jnp.cumsum(changed).astype(I32)
    later = jnp.where(run[None, :] > run[:, None], tile_expert[None, :], N_EXPERTS)
    next_expert = jnp.min(later, axis=1)
    next_expert = jnp.where(next_expert < N_EXPERTS, next_expert, -1).astype(I32)
    grid_spec = pltpu.PrefetchScalarGridSpec(
        num_scalar_prefetch=5,
        grid=(nt,),
        in_specs=[pl.BlockSpec(tok_tile, lambda i, te, src, nv, run, nxt: (src[i], 0)),
                  pl.BlockSpec(memory_space=pl.ANY), pl.BlockSpec(memory_space=pl.ANY),
                  pl.BlockSpec(memory_space=pl.ANY)],
        out_specs=pl.BlockSpec(tok_tile, lambda i, te, src, nv, run, nxt: (i, 0)),
        scratch_shapes=[pltpu.VMEM((2, d, hid), F32), pltpu.VMEM((2, d, hid), F32),
                        pltpu.VMEM((2, hid, d), F32),
                        pltpu.VMEM((d, hid), BF16), pltpu.VMEM((d, hid), BF16),
                        pltpu.VMEM((hid, d), BF16), pltpu.SemaphoreType.DMA((2,))])
    return pl.pallas_call(
        functools.partial(_expert_kernel, layer=layer),
        grid_spec=grid_spec,
        out_shape=jax.ShapeDtypeStruct(hs.shape, F32),
        compiler_params=_params(1),
        name=f"experts{layer}",
    )(tile_expert, tile_src, tile_rows, run, next_expert, hs, w1, w3, w2)


def _combine_kernel(tab_ref, next_tab_ref, q_ref, w_ref, x1_ref, mod_ref, fw_ref, ys_ref, out_ref,
                    stage, y_tok, sem, *, final_norm):
    i = pl.program_id(0)
    slot = i % 2
    block_rows = MOVE_BLOCK * TOKEN_ROWS

    def block_copy(buf, stage_row, sorted_row):
        return pltpu.make_async_copy(ys_ref.at[pl.ds(sorted_row, block_rows)],
                                     stage.at[buf, pl.ds(stage_row, block_rows)], sem.at[buf])

    def fetch(tab, buf):
        _for_blocks(tab, lambda a, b, parity: block_copy(buf, a, b).start(priority=parity))

    @pl.when(i == 0)
    def _():
        fetch(tab_ref, slot)

    @pl.when(i + 1 < pl.num_programs(0))
    def _():
        fetch(next_tab_ref, 1 - slot)

    _wait_blocks(tab_ref, block_copy(slot, 0, 0))

    def pick(r, c):
        rows = [stage[slot, pl.ds(pl.multiple_of(q_ref[0, s, r], TOKEN_ROWS), TOKEN_ROWS), :]
                for s in range(2)]
        y_tok[pl.ds(pl.multiple_of(r * TOKEN_ROWS, TOKEN_ROWS), TOKEN_ROWS), :] = (
            w_ref[0, 0, r] * rows[0] + w_ref[0, 1, r] * rows[1])
        return c

    lax.fori_loop(0, ROW_TILE, pick, 0, unroll=8)
    x2 = x1_ref[...] + mod_ref[0, 5:6, :] * _from_token_major(y_tok, ROW_TILE)
    if final_norm:
        x2 = _rms(x2) * fw_ref[...]
    out_ref[...] = x2


def _combine(block_tab, slots, weights, x1, mods, layer, final_w, ys, tile0, n_tiles,
             n_prompt_tiles, tiles_per_sample, final_norm):
    d = D_MODEL
    tile = lambda i: (tile0 + i, 0)
    mod_map = lambda i: (_mod_row(tile0 + i, layer, n_prompt_tiles, tiles_per_sample), 0, 0)
    smem_tile = lambda shape: pl.BlockSpec((1,) + shape, lambda i: (tile0 + i, 0, 0),
                                           memory_space=pltpu.SMEM)
    return pl.pallas_call(
        functools.partial(_combine_kernel, final_norm=final_norm),
        grid=(n_tiles,),
        in_specs=[smem_tile((1, LANES)),
                  pl.BlockSpec((1, 1, LANES),
                               lambda i: (tile0 + jnp.minimum(i + 1, n_tiles - 1), 0, 0),
                               memory_space=pltpu.SMEM),
                  smem_tile((2, ROW_TILE)), smem_tile((2, ROW_TILE)),
                  pl.BlockSpec((ROW_TILE, d), tile),
                  pl.BlockSpec((1, 6, d), mod_map),
                  pl.BlockSpec((1, d), lambda i: (0, 0)),
                  pl.BlockSpec(memory_space=pl.ANY)],
        out_specs=pl.BlockSpec((ROW_TILE, d), lambda i: (i, 0)),
        out_shape=jax.ShapeDtypeStruct((n_tiles * ROW_TILE, d), F32),
        scratch_shapes=[pltpu.VMEM((2, STAGE_TOKENS * TOKEN_ROWS, LANES), F32),
                        pltpu.VMEM((ROW_TILE * TOKEN_ROWS, LANES), F32),
                        pltpu.SemaphoreType.DMA((2,))],
        compiler_params=_params(1),
        name=f"combine{layer}_{tile0}",
    )(block_tab, block_tab, slots, weights, x1, mods, final_w, ys)


def _moe(h2, slots, weights, tile_tab, w1, w3, w2, layer):
    t = h2.shape[0] // TOKEN_ROWS
    n_tiles = t // ROW_TILE
    extra_tiles = N_EXPERTS + _cdiv(N_EXPERTS * MOVE_BLOCK, EXPERT_TILE)
    n_rows = 2 * t + extra_tiles * EXPERT_TILE
    nt = n_rows // EXPERT_TILE
    tab = tile_tab[:, N_GROUPS:N_GROUPS + N_EXPERTS, 0:3].transpose(0, 2, 1).astype(I32)
    cnt = tab[-1, 0] + tab[-1, 1]
    tight = _cdiv(cnt, EXPERT_TILE) * EXPERT_TILE
    padded = jnp.where(cnt > 0, _cdiv(cnt + MOVE_BLOCK - 1, EXPERT_TILE) * EXPERT_TILE, 0)
    ends = jnp.cumsum(padded)
    offsets = ends - padded
    tails = jnp.where(cnt > 0, ends - EXPERT_TILE, -1)
    tails2 = jnp.where(padded > tight, ends - 2 * EXPERT_TILE, -1)
    used = ends[-1] // EXPERT_TILE
    tile_start = jnp.arange(nt, dtype=I32) * EXPERT_TILE
    unused = (used + jnp.arange(extra_tiles, dtype=I32)) * EXPERT_TILE
    zero_tiles = jnp.concatenate([tails, tails2, jnp.where(unused < n_rows, unused, -1)])
    zero_tiles = jnp.where(zero_tiles >= 0, zero_tiles * TOKEN_ROWS, -1).astype(I32)
    tile_src = jnp.minimum(jnp.arange(nt, dtype=I32), used - 1)
    tile_expert = jnp.sum((tile_src * EXPERT_TILE)[:, None] >= ends[None, :], axis=1).astype(I32)
    tile_rows = jnp.where(tile_start < ends[-1],
                          jnp.clip(cnt[tile_expert] - (tile_start - offsets[tile_expert]),
                                   0, EXPERT_TILE), 0).astype(I32)
    n_blocks = _cdiv(tab[:, 0], MOVE_BLOCK)
    blocks_through = jnp.cumsum(n_blocks, axis=1)
    k = jnp.arange(MAX_BLOCKS, dtype=I32)
    owner = jnp.sum(blocks_through[:, None, :] <= k[None, :, None], axis=2)
    is_owner = owner[:, :, None] == jnp.arange(N_EXPERTS, dtype=I32)[None, None, :]
    pick = lambda v: jnp.sum(jnp.where(is_owner, v[:, None, :], 0), axis=2)
    run_first = pick(offsets[None, :] + tab[:, 1])
    block_in_run = k[None, :] - pick(blocks_through - n_blocks)
    sorted_row = (run_first + block_in_run * MOVE_BLOCK) * TOKEN_ROWS
    block_tab = jnp.concatenate(
        [sorted_row, blocks_through[:, -1:],
         jnp.zeros((n_tiles, LANES - MAX_BLOCKS - 1), I32)], axis=1).astype(I32)[:, None, :]
    hs = _dispatch(zero_tiles, block_tab, slots, h2, n_rows)
    ys = _experts(tile_expert, tile_src, tile_rows, hs, w1, w3, w2, layer)
    return ys, (block_tab, slots, weights)


def _rope(x, cos, sin_signed):
    lane = lax.broadcasted_iota(I32, (x.shape[0], LANES), 1)
    low = (lane % 32) < 16
    outs = []
    for j in range(x.shape[1] // LANES):
        xb = x[:, j * LANES:(j + 1) * LANES]
        partner = jnp.where(low, pltpu.roll(xb, LANES - 16, 1), pltpu.roll(xb, 16, 1))
        outs.append(xb * cos + partner * sin_signed)
    return jnp.concatenate(outs, axis=1)


def _inproj1_prompt_kernel(x_ref, mod_ref, nw_ref, w_ref, q_ref, k_ref, v_ref, kc_ref, vc_ref):
    d = D_MODEL
    h = _modulate(x_ref[...], nw_ref[...], mod_ref[0, 0:1, :], mod_ref[0, 1:2, :]).astype(BF16)
    q_ref[...] = (_dot(h, w_ref[:, 0:d]) * (DIFF_HD ** -0.5)).astype(BF16)
    k = _dot(h, w_ref[:, d:2 * d])
    v = _dot(h, w_ref[:, 2 * d:3 * d])
    k_ref[...] = k.astype(BF16)
    v_ref[...] = v.astype(BF16)
    kc_ref[...] = k.T
    _to_token_major(vc_ref, v)


def _inproj1_sample_kernel(x_ref, mod_ref, nw_ref, w_ref, cos_ref, sin_ref, q_ref, k_ref, v_ref):
    d = D_MODEL
    h = _modulate(x_ref[...], nw_ref[...], mod_ref[0, 0:1, :], mod_ref[0, 1:2, :]).astype(BF16)
    cos, sin = cos_ref[...], sin_ref[...]
    q_ref[...] = (_rope(_dot(h, w_ref[:, 0:d]), cos, sin) * (DIFF_HD ** -0.5)).astype(BF16)
    k_ref[...] = _rope(_dot(h, w_ref[:, d:2 * d]), cos, sin).astype(BF16)
    v_ref[...] = _dot(h, w_ref[:, 2 * d:3 * d]).astype(BF16)


def _inproj1(x, mods, norm_w, w_bf16, n_prompt_tiles, n_sample_tiles, tiles_per_sample,
             cos_t, sin_t):
    d = D_MODEL
    npt, nst = n_prompt_tiles, n_sample_tiles
    common = [pl.BlockSpec((1, d), lambda i: (0, 0)), pl.BlockSpec((d, 3 * d), lambda i: (0, 0))]
    tile = lambda i: (i, 0)
    out_specs = tuple(pl.BlockSpec((ATTN_TILE, d), tile) for _ in range(3))
    qp, kp, vp, k_cache, v_cache = pl.pallas_call(
        _inproj1_prompt_kernel,
        grid=(npt,),
        in_specs=[pl.BlockSpec((ATTN_TILE, d), tile),
                  pl.BlockSpec((1, 6, d), lambda i: (8, 0, 0))] + common,
        out_specs=out_specs + (pl.BlockSpec((d, ATTN_TILE), tile),
                               pl.BlockSpec((ATTN_TILE * TOKEN_ROWS, LANES), tile)),
        out_shape=tuple(jax.ShapeDtypeStruct((npt * ATTN_TILE, d), BF16) for _ in range(3))
        + (jax.ShapeDtypeStruct((npt * d, ATTN_TILE), F32),
           jax.ShapeDtypeStruct((npt * ATTN_TILE * TOKEN_ROWS, LANES), F32)),
        compiler_params=_params(1),
        name="inproj1_prompt",
    )(x, mods, norm_w, w_bf16)
    rope_tile = lambda i: (i % tiles_per_sample, 0)
    qs, ks, vs = pl.pallas_call(
        _inproj1_sample_kernel,
        grid=(nst,),
        in_specs=[pl.BlockSpec((ATTN_TILE, d), lambda i: (npt + i, 0)),
                  pl.BlockSpec((1, 6, d), lambda i: (8 + 1 + i // tiles_per_sample, 0, 0))]
        + common + [pl.BlockSpec((ATTN_TILE, LANES), rope_tile),
                    pl.BlockSpec((ATTN_TILE, LANES), rope_tile)],
        out_specs=out_specs,
        out_shape=tuple(jax.ShapeDtypeStruct((nst * ATTN_TILE, d), BF16) for _ in range(3)),
        compiler_params=_params(1),
        name="inproj1_sample",
    )(x, mods, norm_w, w_bf16, cos_t, sin_t)
    return (qp, kp, vp), (qs, ks, vs), (k_cache, v_cache)


def _rope_tables(n_tok):
    half = DIFF_HD // 4
    pos = np.arange(n_tok)
    lane = np.arange(LANES)
    sub = lane % DIFF_HD
    p = np.where(sub[None, :] < DIFF_HD // 2, (pos // GRID_W)[:, None], (pos % GRID_W)[:, None])
    inv = jnp.asarray(ROPE_THETA, F32) ** (-jnp.asarray(sub % half, F32) / half)
    ang = jnp.asarray(p, F32) * inv[None, :]
    sign = np.where((lane % (2 * half)) < half, -1.0, 1.0).astype(np.float32)
    return jnp.cos(ang), jnp.sin(ang) * sign[None, :]


def _diffattn_kernel(*refs, has_cache, lam_init):
    if has_cache:
        q_ref, k_ref, v_ref, ck_ref, cv_ref, lam_ref, sw_ref, o_ref = refs
    else:
        q_ref, k_ref, v_ref, lam_ref, sw_ref, o_ref = refs
    hd2 = 2 * DIFF_HD
    lv = lam_ref[...]
    lam = (jnp.exp(jnp.sum(lv[0:1] * lv[1:2], axis=1, keepdims=True))
           - jnp.exp(jnp.sum(lv[2:3] * lv[3:4], axis=1, keepdims=True)) + lam_init)
    lane = lax.broadcasted_iota(I32, (q_ref.shape[0], hd2), 1)
    for h in range(DIFF_HEADS):
        cols = slice(h * hd2, (h + 1) * hd2)
        q = q_ref[:, cols]
        zero = jnp.zeros_like(q)
        k_new = k_ref[:, cols].astype(BF16)
        values = [v_ref[:, cols].astype(BF16)]
        if has_cache:
            past = ck_ref.shape[1]
            k_past_t = ck_ref[cols, :].astype(BF16)
            values.append(cv_ref[pl.ds(h, past, stride=DIFF_HEADS), :].astype(BF16))
        o = None
        for c in range(2):
            qc = jnp.where((lane < DIFF_HD) == (c == 0), q, zero)
            s = [_dot_nt(qc, k_new)]
            if has_cache:
                s.append(_dot(qc, k_past_t))
            mx = functools.reduce(jnp.maximum, [jnp.max(si, axis=1, keepdims=True) for si in s])
            e = [jnp.exp(si - mx) for si in s]
            z = functools.reduce(jnp.add, [jnp.sum(ei, axis=1, keepdims=True) for ei in e])
            pv = functools.reduce(jnp.add, [_dot(ei.astype(BF16), v) for ei, v in zip(e, values)])
            pv = pv * (1.0 / z)
            o = pv if c == 0 else o - lam * pv
        o_ref[:, cols] = ((_rms(o) * sw_ref[...]) * (1.0 - lam_init)).astype(BF16)


def _diffattn(q, k, v, lam_vecs, subln_w, batch, seq_len, q_block, lam_init, cache=None):
    d = D_MODEL
    nq = seq_len // q_block
    has_cache = cache is not None
    kv_spec = pl.BlockSpec((seq_len, d), lambda b, qi: (b, 0))
    in_specs = [pl.BlockSpec((q_block, d), lambda b, qi: (b * nq + qi, 0)), kv_spec, kv_spec]
    args = [q, k, v]
    if has_cache:
        past = cache[0].shape[1]
        in_specs += [pl.BlockSpec((d, past), lambda b, qi: (b, 0)),
                     pl.BlockSpec((past * DIFF_HEADS, 2 * DIFF_HD), lambda b, qi: (b, 0))]
        args += list(cache)
    in_specs += [pl.BlockSpec((4, DIFF_HD), lambda b, qi: (0, 0)),
                 pl.BlockSpec((1, 2 * DIFF_HD), lambda b, qi: (0, 0))]
    args += [lam_vecs, subln_w]
    return pl.pallas_call(
        functools.partial(_diffattn_kernel, has_cache=has_cache, lam_init=lam_init),
        grid=(batch, nq),
        in_specs=in_specs,
        out_specs=pl.BlockSpec((q_block, d), lambda b, qi: (b * nq + qi, 0)),
        out_shape=jax.ShapeDtypeStruct((batch * seq_len, d), BF16),
        compiler_params=_params(2),
        name="diffattn_cache" if has_cache else "diffattn",
    )(*args)


def _router_weights(router_group, router_expert):
    w = jnp.concatenate([router_group, router_expert], axis=1)
    w = jnp.pad(w, ((0, 0), (0, LANES - w.shape[1])))
    hi = w.astype(BF16)
    return hi, (w - hi.astype(F32)).astype(BF16)


def _inproj0_weights(w_in):
    gq, gk, gv, gg, gaf, gab, hq, hff, hfb, hi, hg = jnp.split(
        w_in, [256, 512, 1024, 1536, 1552, 1568, 1824, 2080, 2336, 2848], axis=1)
    w = jnp.concatenate([gq, gk, gv, gg, hq, hff, hfb, hi, hg, gaf, gab], axis=1)
    return jnp.pad(w, ((0, 0), (0, AB_COLS - w.shape[1]))).astype(BF16)


def kernel(x_prompt, x_sample, state_gla, state_hgrn, cache_diff_k, cache_diff_v, c, c_ctx,
           w_ada, b_ada, norm1_w, norm2_w, w_in_ab, gla_a2, gla_a_bias, hgrn_lb, gla_onorm_w,
           hgrn_onorm_w, w_out_ab, w_in_c, lam_q1, lam_k1, lam_q2, lam_k2, diff_subln_w, w_out_c,
           router_group, router_expert, moe_w1, moe_w3, moe_w2, final_norm_w):
    bp, lp, d = x_prompt.shape
    bs, ls, _ = x_sample.shape
    depth = w_ada.shape[0]
    assert depth == 2 and d == D_MODEL and bs <= 7
    tp, ts = bp * lp, bs * ls
    npt, nst = tp // ROW_TILE, ts // ROW_TILE
    tps = ls // ROW_TILE
    xp = x_prompt.reshape(tp, d)
    xs = x_sample.reshape(ts, d)

    cond8 = jnp.concatenate([c_ctx[None, :], c, jnp.zeros((7 - bs, d), F32)], axis=0)
    mods = _adaln(cond8, w_ada, b_ada).reshape(depth * 8, 6, d)

    proj = _inproj0(xp, xs, mods, norm1_w[0:1], _inproj0_weights(w_in_ab[0]), ls)
    a_bias = gla_a_bias[0][:, None, :]
    scan_args = (gla_a2[0], a_bias, hgrn_lb, gla_onorm_w[0:1], hgrn_onorm_w[0:1])
    mixed_p, new_state_gla, new_state_hgrn = _scan(proj, 0, bp, lp, *scan_args)
    s0 = jnp.concatenate([state_gla[:, 0], state_hgrn[:, 0]], axis=2).swapaxes(-1, -2)
    s0 = s0.reshape(bs, 2, SCAN_PAIRS, 2, HEAD_DV, HEAD_DK)
    zero = jnp.zeros_like(s0[:, :, :, 0])
    s0 = jnp.concatenate([jnp.concatenate([s0[:, :, :, 0], zero], axis=-1),
                          jnp.concatenate([zero, s0[:, :, :, 1]], axis=-1)], axis=-2)
    mixed_s = _scan(proj, tp, bs, ls, *scan_args, s0=s0)

    wr = _router_weights(router_group[0], router_expert[0])
    x1, *routed = _post((xp, xs), mixed_p, mixed_s, mods, 0, norm2_w[0:1],
                        w_out_ab[0].astype(BF16), *wr, tps)
    ys, tables = _moe(*routed, moe_w1, moe_w3, moe_w2, 0)
    x2 = _combine(*tables, x1, mods, 0, final_norm_w[None, :], ys, 0, npt + nst, npt, tps, False)

    lam_init = 0.8 - 0.6 * math.exp(-0.3 * 1)
    cos_t, sin_t = _rope_tables(ls)
    (qp, kp, vp), (qs, ks, vs), (k_cache, v_cache) = _inproj1(
        x2, mods, norm1_w[1:2], w_in_c[0].astype(BF16), tp // ATTN_TILE, ts // ATTN_TILE,
        ls // ATTN_TILE, cos_t, sin_t)
    lam_vecs = jnp.stack([lam_q1[0], lam_k1[0], lam_q2[0], lam_k2[0]])
    att_p = _diffattn(qp, kp, vp, lam_vecs, diff_subln_w[0:1], bp, lp, lp, lam_init)
    past = cache_diff_k.shape[2]
    assert lp == ATTN_TILE and DIFF_HEADS == TOKEN_ROWS
    cache = (cache_diff_k[:, 0].transpose(0, 2, 3, 4, 1).reshape(bs * d, past),
             cache_diff_v[:, 0].reshape(bs * past * DIFF_HEADS, 2 * DIFF_HD))
    att_s = _diffattn(qs, ks, vs, lam_vecs, diff_subln_w[0:1], bs, ls, SAMPLE_Q_BLOCK, lam_init,
                      cache)

    wr = _router_weights(router_group[1], router_expert[1])
    x3, *routed = _post((x2,), att_p, att_s, mods, 1, norm2_w[1:2],
                        w_out_c[0].astype(BF16), *wr, tps)
    ys, tables = _moe(*routed, moe_w1, moe_w3, moe_w2, 1)
    fw = final_norm_w[None, :]
    y_p = _combine(*tables, x3, mods, 1, fw, ys, 0, npt, npt, tps, True)
    y_s = _combine(*tables, x3, mods, 1, fw, ys, npt, nst, npt, tps, True)

    return (y_p.reshape(bp, lp, d), y_s.reshape(bs, ls, d), new_state_gla, new_state_hgrn,
            k_cache.reshape(bp, 1, DIFF_HEADS, 2, DIFF_HD, lp).transpose(0, 1, 5, 2, 3, 4),
            v_cache.reshape(bp, 1, lp, DIFF_HEADS, 2 * DIFF_HD))
```

```python
import functools
import math

import jax
import jax.numpy as jnp
import numpy as np
from jax import lax
from jax.experimental import pallas as pl
from jax.experimental.pallas import tpu as pltpu

F32 = jnp.float32
BF16 = jnp.bfloat16
I32 = jnp.int32

D_MODEL = 1024
GLA_HEADS = 4
HGRN_HEADS = 4
SCAN_HEADS = GLA_HEADS + HGRN_HEADS
SCAN_PAIRS = SCAN_HEADS // 2
HEAD_DK = 64
HEAD_DV = 128
GATE_RANK = 16
GLA_GATE_NORM = 16.0
DIFF_HEADS = 8
DIFF_HD = 64
GRID_W = 64
ROPE_THETA = 10000.0
N_GROUPS = 4
EXPERTS_PER_GROUP = 8
N_EXPERTS = N_GROUPS * EXPERTS_PER_GROUP
MOE_HIDDEN = 512
EPS = 1e-6
LANES = 128
TOKEN_ROWS = D_MODEL // LANES
NEG_BIG = -1e30
ROUTER_ROWS = 48

ROW_TILE = 512
ATTN_TILE = 256
SAMPLE_Q_BLOCK = 512
ADA_TILE = 1536
INPROJ0_TILE = 512
SCAN_CHUNK = 64
EXPERT_TILE = 256
MOVE_BLOCK = 16
STAGE_TOKENS = 2 * ROW_TILE + N_EXPERTS * MOVE_BLOCK
MAX_BLOCKS = STAGE_TOKENS // MOVE_BLOCK
VMEM_LIMIT = 56 * 1024 * 1024
SCAN_INPUT_DOUBLE_BUFFER_BYTES = 16 * 1024 * 1024

_C_GQ, _C_GK, _C_GV, _C_GG = 0, 256, 512, 1024
_C_HQ, _C_HFF, _C_HFB, _C_HI, _C_HG = 1536, 1792, 2048, 2304, 2816
_C_GAF, _C_GAB = 3328, 3344
AB_COLS = 3456


def _params(n_axes, vmem=VMEM_LIMIT):
    return pltpu.CompilerParams(dimension_semantics=("arbitrary",) * n_axes,
                                vmem_limit_bytes=vmem)


def _cdiv(a, b):
    return (a + b - 1) // b


def _dot(a, b):
    return jnp.dot(a, b, preferred_element_type=F32)


def _dot_nt(a, b):
    return lax.dot_general(a, b, (((1,), (1,)), ((), ())), preferred_element_type=F32)


def _dot_tn(a, b):
    return lax.dot_general(a, b, (((0,), (0,)), ((), ())), preferred_element_type=F32)


def _split_bf16(x):
    hi = x.astype(BF16)
    lo = (x - hi.astype(F32)).astype(BF16)
    return hi, lo


def _silu(x):
    return x * jax.nn.sigmoid(x)


def _log_sigmoid(x):
    return jnp.minimum(x, 0.0) - jnp.log(1.0 + jnp.exp(-jnp.abs(x)))


def _rms(x):
    return x * lax.rsqrt(jnp.mean(x * x, axis=-1, keepdims=True) + EPS)


def _modulate(x, norm_w, shift, scale):
    return (_rms(x) * norm_w) * (1.0 + scale) + shift


def _to_token_major(dst_ref, x, row0=0):
    n = x.shape[0]
    for s in range(TOKEN_ROWS):
        dst_ref[pl.ds(row0 + s, n, stride=TOKEN_ROWS), :] = x[:, s * LANES:(s + 1) * LANES]


def _from_token_major(src_ref, n, row0=0):
    return jnp.concatenate([src_ref[pl.ds(row0 + s, n, stride=TOKEN_ROWS), :]
                            for s in range(TOKEN_ROWS)], axis=1)


def _ada_kernel(c_ref, w_ref, b_ref, o_ref):
    s = _silu(c_ref[...])
    o_ref[0] = _dot(s.astype(BF16), w_ref[0].astype(BF16)) + b_ref[0]


def _adaln(cond8, w_ada, b_ada):
    depth, d, n = w_ada.shape
    tn = ADA_TILE
    return pl.pallas_call(
        _ada_kernel,
        grid=(depth, n // tn),
        in_specs=[pl.BlockSpec((8, d), lambda l, j: (0, 0)),
                  pl.BlockSpec((1, d, tn), lambda l, j: (l, 0, j)),
                  pl.BlockSpec((1, 1, tn), lambda l, j: (l, 0, j))],
        out_specs=pl.BlockSpec((1, 8, tn), lambda l, j: (l, 0, j)),
        out_shape=jax.ShapeDtypeStruct((depth, 8, n), F32),
        compiler_params=_params(2),
        name="adaln",
    )(cond8, w_ada, b_ada.reshape(depth, 1, n))


def _mod_row(i, layer, n_prompt_tiles, tiles_per_sample):
    r = jnp.where(i < n_prompt_tiles, 0, 1 + (i - n_prompt_tiles) // tiles_per_sample)
    return layer * 8 + r


def _inproj0_kernel(xp_ref, xs_ref, mod_ref, nw_ref, w_ref, o_ref, *, n_prompt_tiles):
    i = pl.program_id(0)
    x = jnp.where(i < n_prompt_tiles, xp_ref[...], xs_ref[...])
    h = _modulate(x, nw_ref[...], mod_ref[0, 0:1, :], mod_ref[0, 1:2, :])
    o_ref[...] = _dot(h.astype(BF16), w_ref[...])


def _inproj0(xp, xs, mods, norm_w, w_bf16, sample_len):
    tp, d = xp.shape
    ts = xs.shape[0]
    n = w_bf16.shape[1]
    tile = INPROJ0_TILE
    npt, nst = tp // tile, ts // tile
    mod_map = lambda i: (_mod_row(i, 0, npt, sample_len // tile), 0, 0)
    return pl.pallas_call(
        functools.partial(_inproj0_kernel, n_prompt_tiles=npt),
        grid=(npt + nst,),
        in_specs=[pl.BlockSpec((tile, d), lambda i: (jnp.minimum(i, npt - 1), 0)),
                  pl.BlockSpec((tile, d), lambda i: (jnp.maximum(i - npt, 0), 0)),
                  pl.BlockSpec((1, 6, d), mod_map),
                  pl.BlockSpec((1, d), lambda i: (0, 0)),
                  pl.BlockSpec((d, n), lambda i: (0, 0))],
        out_specs=pl.BlockSpec((tile, n), lambda i: (i, 0)),
        out_shape=jax.ShapeDtypeStruct((tp + ts, n), F32),
        compiler_params=_params(1),
        name="inproj0",
    )(xp, xs, mods, norm_w, w_bf16)


def _scan_kernel(*refs, seq_len, has_state):
    if has_state:
        (p_ref, a2_ref, ab_ref, lb_ref, ong_ref, onh_ref, s0_ref, mixed_ref, *scratch) = refs
        sfin_ref = None
    else:
        (p_ref, a2_ref, ab_ref, lb_ref, ong_ref, onh_ref, mixed_ref, sg_ref, sh_ref,
         *scratch) = refs
        s0_ref = None
        sfin_ref = (sg_ref, sh_ref)
    (qi_f, ki_f, qo_f, ko_f, qi_b, ki_b, qo_b, ko_b,
     vv, dec_f, dec_b, o_f, o_b, st_f, st_b) = scratch
    C = SCAN_CHUNK
    n_chunks = seq_len // C
    gqk = GLA_HEADS * HEAD_DK

    row = lax.broadcasted_iota(I32, (C, C), 0)
    col = lax.broadcasted_iota(I32, (C, C), 1)
    lower = col <= row
    upper = col >= row
    tri_lo = jnp.where(lower, 1.0, 0.0).astype(BF16)
    tri_up = jnp.where(upper, 1.0, 0.0).astype(BF16)

    lbp = lb_ref[...]
    lb_max = jnp.maximum(lbp[0], lbp[1])
    lb_e0 = jnp.exp(lbp[0] - lb_max)
    lb_e1 = jnp.exp(lbp[1] - lb_max)
    lb = lb_e0 / (lb_e0 + lb_e1)

    def cumsum_chunk(tri, la):
        hi, lo = _split_bf16(la)
        return _dot(tri, hi) + _dot(tri, lo)

    def prep(n, carry):
        r0 = pl.multiple_of(n * C, C)
        rows = pl.ds(r0, C)
        gq = p_ref[rows, _C_GQ:_C_GQ + gqk] * (HEAD_DK ** -0.5)
        gk = p_ref[rows, _C_GK:_C_GK + gqk]
        hq = _silu(p_ref[rows, _C_HQ:_C_HQ + gqk]) * (HEAD_DK ** -0.5)
        for d_i, (qi_s, ki_s, qo_s, ko_s, dec_s, tri, last, mid) in enumerate(
                ((qi_f, ki_f, qo_f, ko_f, dec_f, tri_lo, C - 1, C // 2 - 1),
                 (qi_b, ki_b, qo_b, ko_b, dec_b, tri_up, 0, C // 2))):
            c_ga = _C_GAF if d_i == 0 else _C_GAB
            c_hf = _C_HFF if d_i == 0 else _C_HFB
            ga = p_ref[rows, c_ga:c_ga + GATE_RANK]
            xg = _dot(ga.astype(BF16), a2_ref[d_i].astype(BF16)) + ab_ref[d_i]
            la_g = _log_sigmoid(xg) / GLA_GATE_NORM
            f = lb[d_i:d_i + 1, :] + (1.0 - lb[d_i:d_i + 1, :]) * jax.nn.sigmoid(
                p_ref[rows, c_hf:c_hf + gqk])
            la_h = jnp.log(f)
            for q, k, la, c0 in ((gq, gk, la_g, 0), (hq, 1.0 - f, la_h, gqk)):
                b = cumsum_chunk(tri, la)
                b_mid, b_end = b[mid:mid + 1, :], b[last:last + 1, :]
                cs = slice(c0, c0 + gqk)
                qi_s[rows, cs] = (q * jnp.exp(b - b_mid)).astype(BF16)
                ki_s[rows, cs] = (k * jnp.exp(b_mid - b)).astype(BF16)
                qo_s[rows, cs] = (q * jnp.exp(b)).astype(BF16)
                ko_s[rows, cs] = (k * jnp.exp(b_end - b)).astype(BF16)
                dec_s[n, :, cs] = jnp.exp(b_end)
        gv_cols = GLA_HEADS * HEAD_DV
        vv[rows, 0:gv_cols] = p_ref[rows, _C_GV:_C_GV + gv_cols].astype(BF16)
        vv[rows, gv_cols:] = p_ref[rows, _C_HI:_C_HI + HGRN_HEADS * HEAD_DV].astype(BF16)
        return carry

    lax.fori_loop(0, n_chunks, prep, 0, unroll=2)

    for p in range(SCAN_PAIRS):
        if has_state:
            st_f[p] = s0_ref[0, 0, p]
            st_b[p] = s0_ref[0, 1, p]
        else:
            st_f[p] = jnp.zeros((2 * HEAD_DV, 2 * HEAD_DK), F32)
            st_b[p] = jnp.zeros((2 * HEAD_DV, 2 * HEAD_DK), F32)

    first_head = lax.broadcasted_iota(I32, (C, 2 * HEAD_DK), 1) < HEAD_DK
    row2 = lax.broadcasted_iota(I32, (2 * C, C), 0) % C
    col2 = lax.broadcasted_iota(I32, (2 * C, C), 1)
    lower2 = col2 <= row2
    upper2 = col2 >= row2

    def per_head_rows(x):
        z = jnp.zeros_like(x)
        return jnp.concatenate([jnp.where(first_head, x, z), jnp.where(first_head, z, x)], axis=0)

    def put_out(o_ref, rows, p, res):
        c0 = p * 2 * HEAD_DV
        o_ref[rows, c0:c0 + HEAD_DV] = res[0:C, 0:HEAD_DV]
        o_ref[rows, c0 + HEAD_DV:c0 + 2 * HEAD_DV] = res[C:2 * C, HEAD_DV:2 * HEAD_DV]

    def sweep(n, carry):
        m = n_chunks - 1 - n
        rows = pl.ds(pl.multiple_of(n * C, C), C)
        rows_m = pl.ds(pl.multiple_of(m * C, C), C)
        decay_f, decay_b = dec_f[n], dec_b[m]
        for p in range(SCAN_PAIRS):
            ks = slice(p * 2 * HEAD_DK, (p + 1) * 2 * HEAD_DK)
            vs = slice(p * 2 * HEAD_DV, (p + 1) * 2 * HEAD_DV)
            vh = vv[rows, vs]
            s_f = st_f[p]
            sc = (jnp.where(lower2, _dot_nt(per_head_rows(qi_f[rows, ks]), ki_f[rows, ks]), 0.0)
                  + jnp.where(upper2, _dot_nt(per_head_rows(qi_b[rows, ks]), ki_b[rows, ks]), 0.0))
            put_out(o_f, rows, p, _dot_nt(per_head_rows(qo_f[rows, ks]), s_f.astype(BF16))
                    + _dot(sc.astype(BF16), vh))
            st_f[p] = decay_f[:, ks] * s_f + _dot_tn(vh, ko_f[rows, ks])
            s_b = st_b[p]
            put_out(o_b, rows_m, p, _dot_nt(per_head_rows(qo_b[rows_m, ks]), s_b.astype(BF16)))
            st_b[p] = decay_b[:, ks] * s_b + _dot_tn(vv[rows_m, vs], ko_b[rows_m, ks])
        return carry

    lax.fori_loop(0, n_chunks, sweep, 0)

    def finish(n, carry):
        rows = pl.ds(pl.multiple_of(n * C, C), C)
        for h in range(SCAN_HEADS):
            vs = slice(h * HEAD_DV, (h + 1) * HEAD_DV)
            if h < GLA_HEADS:
                gate = p_ref[rows, _C_GG + h * HEAD_DV:_C_GG + (h + 1) * HEAD_DV]
                onw = ong_ref[...]
            else:
                hh = h - GLA_HEADS
                gate = p_ref[rows, _C_HG + hh * HEAD_DV:_C_HG + (hh + 1) * HEAD_DV]
                onw = onh_ref[...]
            o = o_f[rows, vs] + o_b[rows, vs]
            mixed_ref[rows, vs] = ((_rms(o) * onw) * _silu(gate)).astype(BF16)
        return carry

    lax.fori_loop(0, n_chunks, finish, 0, unroll=2)

    if sfin_ref is not None:
        for d_i, st in enumerate((st_f, st_b)):
            for p in range(SCAN_PAIRS):
                s_pair = st[p].T
                out_ref = sfin_ref[(2 * p) // GLA_HEADS]
                h0 = (2 * p) % GLA_HEADS
                out_ref[0, 0, d_i, h0] = s_pair[0:HEAD_DK, 0:HEAD_DV]
                out_ref[0, 0, d_i, h0 + 1] = s_pair[HEAD_DK:2 * HEAD_DK, HEAD_DV:2 * HEAD_DV]


def _scan(p, row0, batch, seq_len, a2, a_bias, lb, onorm_g, onorm_h, s0=None):
    n = p.shape[1]
    assert row0 % seq_len == 0
    blk0 = row0 // seq_len
    has_state = s0 is not None
    n_chunks = seq_len // SCAN_CHUNK
    assert GLA_HEADS == HGRN_HEADS and GLA_HEADS % 2 == 0
    st_shape = (1, 1, 2, GLA_HEADS, HEAD_DK, HEAD_DV)
    pair_shape = (SCAN_PAIRS, 2 * HEAD_DV, 2 * HEAD_DK)
    p_mode = dict(pipeline_mode=pl.Buffered(1)) if seq_len * n * 4 > SCAN_INPUT_DOUBLE_BUFFER_BYTES else {}
    in_specs = [pl.BlockSpec((seq_len, n), lambda b: (blk0 + b, 0), **p_mode),
                pl.BlockSpec(a2.shape, lambda b: (0, 0, 0)),
                pl.BlockSpec(a_bias.shape, lambda b: (0, 0, 0)),
                pl.BlockSpec(lb.shape, lambda b: (0, 0, 0)),
                pl.BlockSpec((1, HEAD_DV), lambda b: (0, 0)),
                pl.BlockSpec((1, HEAD_DV), lambda b: (0, 0))]
    args = [p, a2, a_bias, lb, onorm_g, onorm_h]
    mixed_shape = jax.ShapeDtypeStruct((batch * seq_len, D_MODEL), BF16)
    mixed_spec = pl.BlockSpec((seq_len, D_MODEL), lambda b: (b, 0))
    if has_state:
        in_specs.append(pl.BlockSpec((1, 2) + pair_shape, lambda b: (b, 0, 0, 0, 0)))
        args.append(s0)
        out_shape, out_specs = mixed_shape, mixed_spec
    else:
        st_struct = jax.ShapeDtypeStruct((batch,) + st_shape[1:], F32)
        st_spec = pl.BlockSpec(st_shape, lambda b: (b, 0, 0, 0, 0, 0))
        out_shape = (mixed_shape, st_struct, st_struct)
        out_specs = (mixed_spec, st_spec, st_spec)
    qk_cols = SCAN_HEADS * HEAD_DK
    scratch = [pltpu.VMEM((seq_len, qk_cols), BF16) for _ in range(8)]
    scratch += [pltpu.VMEM((seq_len, D_MODEL), BF16),
                pltpu.VMEM((n_chunks, 1, qk_cols), F32), pltpu.VMEM((n_chunks, 1, qk_cols), F32),
                pltpu.VMEM((seq_len, D_MODEL), F32), pltpu.VMEM((seq_len, D_MODEL), F32),
                pltpu.VMEM(pair_shape, F32), pltpu.VMEM(pair_shape, F32)]
    return pl.pallas_call(
        functools.partial(_scan_kernel, seq_len=seq_len, has_state=has_state),
        grid=(batch,),
        in_specs=in_specs, out_specs=out_specs, out_shape=out_shape,
        scratch_shapes=scratch,
        compiler_params=_params(1),
        name="scan_state" if has_state else "scan_fresh",
    )(*args)


def _post_kernel(*refs, split_x, n_prompt_tiles):
    if split_x:
        xp_ref, xs_ref = refs[0], refs[1]
        refs = refs[2:]
    else:
        x_ref = refs[0]
        refs = refs[1:]
    (mp_ref, ms_ref, mod_ref, nw_ref, wo_ref, wrh_ref, wrl_ref,
     x1_ref, h2_ref, slot_ref, wgt_ref, tab_ref, carry, earlier) = refs
    i = pl.program_id(0)
    is_prompt = i < n_prompt_tiles
    if split_x:
        x = jnp.where(is_prompt, xp_ref[...], xs_ref[...])
    else:
        x = x_ref[...]
    mixed = jnp.where(is_prompt, mp_ref[...], ms_ref[...])
    x1 = x + mod_ref[0, 2:3, :] * _dot(mixed, wo_ref[...])
    x1_ref[...] = x1
    h2 = _modulate(x1, nw_ref[...], mod_ref[0, 3:4, :], mod_ref[0, 4:5, :])
    _to_token_major(h2_ref, h2)

    hh, hl = _split_bf16(h2)
    logits = _dot(hh, wrh_ref[...]) + _dot(hl, wrh_ref[...]) + _dot(hh, wrl_ref[...])
    tm = logits.shape[0]
    lt = logits.T[0:ROUTER_ROWS]
    ridx = lax.broadcasted_iota(I32, (ROUTER_ROWS, tm), 0).astype(F32)

    def first_max(v):
        mx = jnp.max(v, axis=0, keepdims=True)
        idx = jnp.min(jnp.where(v == mx, ridx, float(ROUTER_ROWS)), axis=0, keepdims=True)
        return mx, idx

    gl = jnp.where(ridx < N_GROUPS, lt, NEG_BIG)
    gmax, gidx = first_max(gl)
    g_val = 1.0 / jnp.sum(jnp.exp(gl - gmax), axis=0, keepdims=True)
    lo = N_GROUPS + EXPERTS_PER_GROUP * gidx
    el = jnp.where((ridx >= lo) & (ridx < lo + EXPERTS_PER_GROUP), lt, NEG_BIG)
    emax, l1 = first_max(el)
    esum = jnp.sum(jnp.exp(el - emax), axis=0, keepdims=True)
    e2max, l2 = first_max(jnp.where(ridx == l1, NEG_BIG, el))
    p1 = 1.0 / esum
    p2 = jnp.exp(e2max - emax) / esum
    w1 = g_val * (p1 / (p1 + p2))
    w2 = g_val * (p2 / (p1 + p2))

    @pl.when(i == 0)
    def _():
        carry[...] = jnp.zeros_like(carry)
        t_row = lax.broadcasted_iota(I32, earlier.shape, 0)
        t_col = lax.broadcasted_iota(I32, earlier.shape, 1)
        earlier[...] = jnp.where(t_row < t_col, 1.0, 0.0).astype(BF16)

    sel1 = ridx == l1
    sel2 = ridx == l2
    onehot = jnp.where(sel1 | sel2, 1.0, 0.0)
    before = _dot(onehot.astype(BF16), earlier[...])
    count = jnp.sum(onehot, axis=1, keepdims=True)
    blocks = jnp.floor((count + (MOVE_BLOCK - 1.0)) * (1.0 / MOVE_BLOCK)) * MOVE_BLOCK
    r_row = lax.broadcasted_iota(I32, (ROUTER_ROWS, ROUTER_ROWS), 0)
    r_col = lax.broadcasted_iota(I32, (ROUTER_ROWS, ROUTER_ROWS), 1)
    lower_rows = jnp.where(r_col < r_row, 1.0, 0.0).astype(BF16)
    run_start = _dot(lower_rows,
                     jnp.broadcast_to(blocks, (ROUTER_ROWS, LANES)).astype(BF16))[:, 0:1]
    slot = before + run_start
    q1 = jnp.sum(jnp.where(sel1, slot, 0.0), axis=0, keepdims=True)
    q2 = jnp.sum(jnp.where(sel2, slot, 0.0), axis=0, keepdims=True)
    tab_lane = lax.broadcasted_iota(I32, (ROUTER_ROWS, LANES), 1)
    tab_ref[0] = jnp.where(tab_lane == 0, count,
                           jnp.where(tab_lane == 1, carry[...],
                                     jnp.where(tab_lane == 2, run_start, 0.0)))
    carry[...] = carry[...] + count

    slot_ref[0, 0:1, :] = (q1 * TOKEN_ROWS).astype(I32)
    slot_ref[0, 1:2, :] = (q2 * TOKEN_ROWS).astype(I32)
    wgt_ref[0, 0:1, :] = w1
    wgt_ref[0, 1:2, :] = w2


def _post(x_args, mixed_p, mixed_s, mods, layer, norm_w, w_out_bf16, wr_hi, wr_lo,
          tiles_per_sample):
    split_x = len(x_args) == 2
    tp, ts = mixed_p.shape[0], mixed_s.shape[0]
    t, d = tp + ts, D_MODEL
    npt, nst = tp // ROW_TILE, ts // ROW_TILE
    tile = lambda i: (i, 0)
    if split_x:
        x_specs = [pl.BlockSpec((ROW_TILE, d), lambda i: (jnp.minimum(i, npt - 1), 0)),
                   pl.BlockSpec((ROW_TILE, d), lambda i: (jnp.maximum(i - npt, 0), 0))]
    else:
        x_specs = [pl.BlockSpec((ROW_TILE, d), tile)]
    in_specs = x_specs + [
        pl.BlockSpec((ROW_TILE, d), lambda i: (jnp.minimum(i, npt - 1), 0)),
        pl.BlockSpec((ROW_TILE, d), lambda i: (jnp.maximum(i - npt, 0), 0)),
        pl.BlockSpec((1, 6, d), lambda i: (_mod_row(i, layer, npt, tiles_per_sample), 0, 0)),
        pl.BlockSpec((1, d), lambda i: (0, 0)),
        pl.BlockSpec((d, d), lambda i: (0, 0)),
        pl.BlockSpec((d, LANES), lambda i: (0, 0)),
        pl.BlockSpec((d, LANES), lambda i: (0, 0))]
    return pl.pallas_call(
        functools.partial(_post_kernel, split_x=split_x, n_prompt_tiles=npt),
        grid=(npt + nst,),
        in_specs=in_specs,
        out_specs=(pl.BlockSpec((ROW_TILE, d), tile),
                   pl.BlockSpec((ROW_TILE * TOKEN_ROWS, LANES), tile),
                   pl.BlockSpec((1, 2, ROW_TILE), lambda i: (i, 0, 0)),
                   pl.BlockSpec((1, 2, ROW_TILE), lambda i: (i, 0, 0)),
                   pl.BlockSpec((1, ROUTER_ROWS, LANES), lambda i: (i, 0, 0))),
        out_shape=(jax.ShapeDtypeStruct((t, d), F32),
                   jax.ShapeDtypeStruct((t * TOKEN_ROWS, LANES), F32),
                   jax.ShapeDtypeStruct((npt + nst, 2, ROW_TILE), I32),
                   jax.ShapeDtypeStruct((npt + nst, 2, ROW_TILE), F32),
                   jax.ShapeDtypeStruct((npt + nst, ROUTER_ROWS, LANES), F32)),
        scratch_shapes=[pltpu.VMEM((ROUTER_ROWS, LANES), F32),
                        pltpu.VMEM((ROW_TILE, ROW_TILE), BF16)],
        compiler_params=_params(1),
        name=f"post{layer}",
    )(*x_args, mixed_p, mixed_s, mods, norm_w, w_out_bf16, wr_hi, wr_lo)


def _for_blocks(tab_ref, fn):
    block_rows = MOVE_BLOCK * TOKEN_ROWS
    count = tab_ref[0, 0, MAX_BLOCKS]

    def call(k, parity):
        fn(pl.multiple_of(k * block_rows, block_rows),
           pl.multiple_of(tab_ref[0, 0, k], TOKEN_ROWS), parity)

    def body(k2, c):
        call(2 * k2, 0)

        @pl.when(2 * k2 + 1 < count)
        def _():
            call(2 * k2 + 1, 1)
        return c

    lax.fori_loop(0, _cdiv(count, 2), body, 0)


def _wait_blocks(tab_ref, copy):
    def body(k, c):
        copy.wait()
        return c

    lax.fori_loop(0, tab_ref[0, 0, MAX_BLOCKS], body, 0)


def _dispatch_kernel(zero_ref, tab_ref, prev_tab_ref, q_ref, h2_ref, hs_ref, zero_buf, stage, sem):
    j = pl.program_id(0)
    slot = j % 2
    block_rows = MOVE_BLOCK * TOKEN_ROWS

    @pl.when(j == 0)
    def _():
        zero_buf[...] = jnp.zeros_like(zero_buf)

        def zero_copy(k):
            start = pl.multiple_of(zero_ref[k], EXPERT_TILE * TOKEN_ROWS)
            return pltpu.make_async_copy(
                zero_buf, hs_ref.at[pl.ds(start, EXPERT_TILE * TOKEN_ROWS)], sem.at[0])

        def start_zero(k, c):
            @pl.when(zero_ref[k] >= 0)
            def _():
                zero_copy(k).start()
            return c

        def wait_zero(k, c):
            @pl.when(zero_ref[k] >= 0)
            def _():
                zero_copy(k).wait()
            return c

        lax.fori_loop(0, zero_ref.shape[0], start_zero, 0)
        lax.fori_loop(0, zero_ref.shape[0], wait_zero, 0)

        stage[...] = jnp.zeros_like(stage)

    def place(r, c):
        tok = h2_ref[pl.ds(pl.multiple_of(r * TOKEN_ROWS, TOKEN_ROWS), TOKEN_ROWS), :]
        for s in range(2):
            row = pl.multiple_of(q_ref[0, s, r], TOKEN_ROWS)
            stage[slot, pl.ds(row, TOKEN_ROWS), :] = tok
        return c

    lax.fori_loop(0, ROW_TILE, place, 0, unroll=8)

    def block_copy(buf, stage_row, sorted_row):
        return pltpu.make_async_copy(stage.at[buf, pl.ds(stage_row, block_rows)],
                                     hs_ref.at[pl.ds(sorted_row, block_rows)], sem.at[buf])

    @pl.when(j > 0)
    def _():
        _wait_blocks(prev_tab_ref, block_copy(1 - slot, 0, 0))

    _for_blocks(tab_ref, lambda a, b, parity: block_copy(slot, a, b).start(priority=parity))

    @pl.when(j == pl.num_programs(0) - 1)
    def _():
        _wait_blocks(tab_ref, block_copy(slot, 0, 0))


def _dispatch(zero_tiles, block_tab, slots, h2, n_rows):
    t = h2.shape[0] // TOKEN_ROWS
    nt = t // ROW_TILE
    smem_tile = lambda shape: pl.BlockSpec((1,) + shape, lambda j, *_: (j, 0, 0),
                                           memory_space=pltpu.SMEM)
    grid_spec = pltpu.PrefetchScalarGridSpec(
        num_scalar_prefetch=1,
        grid=(nt,),
        in_specs=[smem_tile((1, LANES)),
                  pl.BlockSpec((1, 1, LANES), lambda j, *_: (jnp.maximum(j - 1, 0), 0, 0),
                               memory_space=pltpu.SMEM),
                  smem_tile((2, ROW_TILE)),
                  pl.BlockSpec((ROW_TILE * TOKEN_ROWS, LANES), lambda j, *_: (j, 0))],
        out_specs=pl.BlockSpec(memory_space=pl.ANY),
        scratch_shapes=[pltpu.VMEM((EXPERT_TILE * TOKEN_ROWS, LANES), F32),
                        pltpu.VMEM((2, STAGE_TOKENS * TOKEN_ROWS, LANES), F32),
                        pltpu.SemaphoreType.DMA((2,))])
    return pl.pallas_call(
        _dispatch_kernel,
        grid_spec=grid_spec,
        out_shape=jax.ShapeDtypeStruct((n_rows * TOKEN_ROWS, LANES), F32),
        compiler_params=_params(1),
        name="dispatch",
    )(zero_tiles, block_tab, block_tab, slots, h2)


def _expert_kernel(te_ref, src_ref, nv_ref, run_ref, nxt_ref, hs_ref, w1_hbm, w3_hbm, w2_hbm,
                   ys_ref, w1f, w3f, w2f, w1b, w3b, w2b, sem, *, layer):
    i = pl.program_id(0)

    def weight_copies(e, buf):
        return [pltpu.make_async_copy(src.at[layer, e], dst.at[buf], sem.at[buf])
                for src, dst in ((w1_hbm, w1f), (w3_hbm, w3f), (w2_hbm, w2f))]

    @pl.when(i == 0)
    def _():
        for c in weight_copies(te_ref[0], 0):
            c.start()

    first = (i == 0) | (run_ref[i] != run_ref[jnp.maximum(i - 1, 0)])

    @pl.when(first)
    def _():
        buf = run_ref[i] % 2
        for c in weight_copies(te_ref[i], buf):
            c.wait()

        @pl.when(nxt_ref[i] >= 0)
        def _():
            for c in weight_copies(nxt_ref[i], 1 - buf):
                c.start()

        w1b[...] = w1f[buf].astype(BF16)
        w3b[...] = w3f[buf].astype(BF16)
        w2b[...] = w2f[buf].astype(BF16)

    @pl.when(nv_ref[i] > 0)
    def _():
        h = _from_token_major(hs_ref, EXPERT_TILE).astype(BF16)
        g = _silu(_dot(h, w1b[...])) * _dot(h, w3b[...])
        _to_token_major(ys_ref, _dot(g.astype(BF16), w2b[...]))

    @pl.when(nv_ref[i] == 0)
    def _():
        ys_ref[...] = jnp.zeros_like(ys_ref)


def _experts(tile_expert, tile_src, tile_rows, hs, w1, w3, w2, layer):
    n_rows, d = hs.shape[0] // TOKEN_ROWS, D_MODEL
    nt = n_rows // EXPERT_TILE
    hid = w1.shape[-1]
    tok_tile = (EXPERT_TILE * TOKEN_ROWS, LANES)
    changed = jnp.concatenate([jnp.zeros((1,), I32),
                               (tile_expert[1:] != tile_expert[:-1]).astype(I32)])
    run = jnp.cumsum(changed).astype(I32)
    later = jnp.where(run[None, :] > run[:, None], tile_expert[None, :], N_EXPERTS)
    next_expert = jnp.min(later, axis=1)
    next_expert = jnp.where(next_expert < N_EXPERTS, next_expert, -1).astype(I32)
    grid_spec = pltpu.PrefetchScalarGridSpec(
        num_scalar_prefetch=5,
        grid=(nt,),
        in_specs=[pl.BlockSpec(tok_tile, lambda i, te, src, nv, run, nxt: (src[i], 0)),
                  pl.BlockSpec(memory_space=pl.ANY), pl.BlockSpec(memory_space=pl.ANY),
                  pl.BlockSpec(memory_space=pl.ANY)],
        out_specs=pl.BlockSpec(tok_tile, lambda i, te, src, nv, run, nxt: (i, 0)),
        scratch_shapes=[pltpu.VMEM((2, d, hid), F32), pltpu.VMEM((2, d, hid), F32),
                        pltpu.VMEM((2, hid, d), F32),
                        pltpu.VMEM((d, hid), BF16), pltpu.VMEM((d, hid), BF16),
                        pltpu.VMEM((hid, d), BF16), pltpu.SemaphoreType.DMA((2,))])
    return pl.pallas_call(
        functools.partial(_expert_kernel, layer=layer),
        grid_spec=grid_spec,
        out_shape=jax.ShapeDtypeStruct(hs.shape, F32),
        compiler_params=_params(1),
        name=f"experts{layer}",
    )(tile_expert, tile_src, tile_rows, run, next_expert, hs, w1, w3, w2)


def _combine_kernel(tab_ref, next_tab_ref, q_ref, w_ref, x1_ref, mod_ref, fw_ref, ys_ref, out_ref,
                    stage, y_tok, sem, *, final_norm):
    i = pl.program_id(0)
    slot = i % 2
    block_rows = MOVE_BLOCK * TOKEN_ROWS

    def block_copy(buf, stage_row, sorted_row):
        return pltpu.make_async_copy(ys_ref.at[pl.ds(sorted_row, block_rows)],
                                     stage.at[buf, pl.ds(stage_row, block_rows)], sem.at[buf])

    def fetch(tab, buf):
        _for_blocks(tab, lambda a, b, parity: block_copy(buf, a, b).start(priority=parity))

    @pl.when(i == 0)
    def _():
        fetch(tab_ref, slot)

    @pl.when(i + 1 < pl.num_programs(0))
    def _():
        fetch(next_tab_ref, 1 - slot)

    _wait_blocks(tab_ref, block_copy(slot, 0, 0))

    def pick(r, c):
        rows = [stage[slot, pl.ds(pl.multiple_of(q_ref[0, s, r], TOKEN_ROWS), TOKEN_ROWS), :]
                for s in range(2)]
        y_tok[pl.ds(pl.multiple_of(r * TOKEN_ROWS, TOKEN_ROWS), TOKEN_ROWS), :] = (
            w_ref[0, 0, r] * rows[0] + w_ref[0, 1, r] * rows[1])
        return c

    lax.fori_loop(0, ROW_TILE, pick, 0, unroll=8)
    x2 = x1_ref[...] + mod_ref[0, 5:6, :] * _from_token_major(y_tok, ROW_TILE)
    if final_norm:
        x2 = _rms(x2) * fw_ref[...]
    out_ref[...] = x2


def _combine(block_tab, slots, weights, x1, mods, layer, final_w, ys, tile0, n_tiles,
             n_prompt_tiles, tiles_per_sample, final_norm):
    d = D_MODEL
    tile = lambda i: (tile0 + i, 0)
    mod_map = lambda i: (_mod_row(tile0 + i, layer, n_prompt_tiles, tiles_per_sample), 0, 0)
    smem_tile = lambda shape: pl.BlockSpec((1,) + shape, lambda i: (tile0 + i, 0, 0),
                                           memory_space=pltpu.SMEM)
    return pl.pallas_call(
        functools.partial(_combine_kernel, final_norm=final_norm),
        grid=(n_tiles,),
        in_specs=[smem_tile((1, LANES)),
                  pl.BlockSpec((1, 1, LANES),
                               lambda i: (tile0 + jnp.minimum(i + 1, n_tiles - 1), 0, 0),
                               memory_space=pltpu.SMEM),
                  smem_tile((2, ROW_TILE)), smem_tile((2, ROW_TILE)),
                  pl.BlockSpec((ROW_TILE, d), tile),
                  pl.BlockSpec((1, 6, d), mod_map),
                  pl.BlockSpec((1, d), lambda i: (0, 0)),
                  pl.BlockSpec(memory_space=pl.ANY)],
        out_specs=pl.BlockSpec((ROW_TILE, d), lambda i: (i, 0)),
        out_shape=jax.ShapeDtypeStruct((n_tiles * ROW_TILE, d), F32),
        scratch_shapes=[pltpu.VMEM((2, STAGE_TOKENS * TOKEN_ROWS, LANES), F32),
                        pltpu.VMEM((ROW_TILE * TOKEN_ROWS, LANES), F32),
                        pltpu.SemaphoreType.DMA((2,))],
        compiler_params=_params(1),
        name=f"combine{layer}_{tile0}",
    )(block_tab, block_tab, slots, weights, x1, mods, final_w, ys)


def _moe(h2, slots, weights, tile_tab, w1, w3, w2, layer):
    t = h2.shape[0] // TOKEN_ROWS
    n_tiles = t // ROW_TILE
    extra_tiles = N_EXPERTS + _cdiv(N_EXPERTS * MOVE_BLOCK, EXPERT_TILE)
    n_rows = 2 * t + extra_tiles * EXPERT_TILE
    nt = n_rows // EXPERT_TILE
    tab = tile_tab[:, N_GROUPS:N_GROUPS + N_EXPERTS, 0:3].transpose(0, 2, 1).astype(I32)
    cnt = tab[-1, 0] + tab[-1, 1]
    tight = _cdiv(cnt, EXPERT_TILE) * EXPERT_TILE
    padded = jnp.where(cnt > 0, _cdiv(cnt + MOVE_BLOCK - 1, EXPERT_TILE) * EXPERT_TILE, 0)
    ends = jnp.cumsum(padded)
    offsets = ends - padded
    tails = jnp.where(cnt > 0, ends - EXPERT_TILE, -1)
    tails2 = jnp.where(padded > tight, ends - 2 * EXPERT_TILE, -1)
    used = ends[-1] // EXPERT_TILE
    tile_start = jnp.arange(nt, dtype=I32) * EXPERT_TILE
    unused = (used + jnp.arange(extra_tiles, dtype=I32)) * EXPERT_TILE
    zero_tiles = jnp.concatenate([tails, tails2, jnp.where(unused < n_rows, unused, -1)])
    zero_tiles = jnp.where(zero_tiles >= 0, zero_tiles * TOKEN_ROWS, -1).astype(I32)
    tile_src = jnp.minimum(jnp.arange(nt, dtype=I32), used - 1)
    tile_expert = jnp.sum((tile_src * EXPERT_TILE)[:, None] >= ends[None, :], axis=1).astype(I32)
    tile_rows = jnp.where(tile_start < ends[-1],
                          jnp.clip(cnt[tile_expert] - (tile_start - offsets[tile_expert]),
                                   0, EXPERT_TILE), 0).astype(I32)
    n_blocks = _cdiv(tab[:, 0], MOVE_BLOCK)
    blocks_through = jnp.cumsum(n_blocks, axis=1)
    k = jnp.arange(MAX_BLOCKS, dtype=I32)
    owner = jnp.sum(blocks_through[:, None, :] <= k[None, :, None], axis=2)
    is_owner = owner[:, :, None] == jnp.arange(N_EXPERTS, dtype=I32)[None, None, :]
    pick = lambda v: jnp.sum(jnp.where(is_owner, v[:, None, :], 0), axis=2)
    run_first = pick(offsets[None, :] + tab[:, 1])
    block_in_run = k[None, :] - pick(blocks_through - n_blocks)
    sorted_row = (run_first + block_in_run * MOVE_BLOCK) * TOKEN_ROWS
    block_tab = jnp.concatenate(
        [sorted_row, blocks_through[:, -1:],
         jnp.zeros((n_tiles, LANES - MAX_BLOCKS - 1), I32)], axis=1).astype(I32)[:, None, :]
    hs = _dispatch(zero_tiles, block_tab, slots, h2, n_rows)
    ys = _experts(tile_expert, tile_src, tile_rows, hs, w1, w3, w2, layer)
    return ys, (block_tab, slots, weights)


def _rope(x, cos, sin_signed):
    lane = lax.broadcasted_iota(I32, (x.shape[0], LANES), 1)
    low = (lane % 32) < 16
    outs = []
    for j in range(x.shape[1] // LANES):
        xb = x[:, j * LANES:(j + 1) * LANES]
        partner = jnp.where(low, pltpu.roll(xb, LANES - 16, 1), pltpu.roll(xb, 16, 1))
        outs.append(xb * cos + partner * sin_signed)
    return jnp.concatenate(outs, axis=1)


def _inproj1_prompt_kernel(x_ref, mod_ref, nw_ref, w_ref, q_ref, k_ref, v_ref, kc_ref, vc_ref):
    d = D_MODEL
    h = _modulate(x_ref[...], nw_ref[...], mod_ref[0, 0:1, :], mod_ref[0, 1:2, :]).astype(BF16)
    q_ref[...] = (_dot(h, w_ref[:, 0:d]) * (DIFF_HD ** -0.5)).astype(BF16)
    k = _dot(h, w_ref[:, d:2 * d])
    v = _dot(h, w_ref[:, 2 * d:3 * d])
    k_ref[...] = k.astype(BF16)
    v_ref[...] = v.astype(BF16)
    kc_ref[...] = k.T
    _to_token_major(vc_ref, v)


def _inproj1_sample_kernel(x_ref, mod_ref, nw_ref, w_ref, cos_ref, sin_ref, q_ref, k_ref, v_ref):
    d = D_MODEL
    h = _modulate(x_ref[...], nw_ref[...], mod_ref[0, 0:1, :], mod_ref[0, 1:2, :]).astype(BF16)
    cos, sin = cos_ref[...], sin_ref[...]
    q_ref[...] = (_rope(_dot(h, w_ref[:, 0:d]), cos, sin) * (DIFF_HD ** -0.5)).astype(BF16)
    k_ref[...] = _rope(_dot(h, w_ref[:, d:2 * d]), cos, sin).astype(BF16)
    v_ref[...] = _dot(h, w_ref[:, 2 * d:3 * d]).astype(BF16)


def _inproj1(x, mods, norm_w, w_bf16, n_prompt_tiles, n_sample_tiles, tiles_per_sample,
             cos_t, sin_t):
    d = D_MODEL
    npt, nst = n_prompt_tiles, n_sample_tiles
    common = [pl.BlockSpec((1, d), lambda i: (0, 0)), pl.BlockSpec((d, 3 * d), lambda i: (0, 0))]
    tile = lambda i: (i, 0)
    out_specs = tuple(pl.BlockSpec((ATTN_TILE, d), tile) for _ in range(3))
    qp, kp, vp, k_cache, v_cache = pl.pallas_call(
        _inproj1_prompt_kernel,
        grid=(npt,),
        in_specs=[pl.BlockSpec((ATTN_TILE, d), tile),
                  pl.BlockSpec((1, 6, d), lambda i: (8, 0, 0))] + common,
        out_specs=out_specs + (pl.BlockSpec((d, ATTN_TILE), tile),
                               pl.BlockSpec((ATTN_TILE * TOKEN_ROWS, LANES), tile)),
        out_shape=tuple(jax.ShapeDtypeStruct((npt * ATTN_TILE, d), BF16) for _ in range(3))
        + (jax.ShapeDtypeStruct((npt * d, ATTN_TILE), F32),
           jax.ShapeDtypeStruct((npt * ATTN_TILE * TOKEN_ROWS, LANES), F32)),
        compiler_params=_params(1),
        name="inproj1_prompt",
    )(x, mods, norm_w, w_bf16)
    rope_tile = lambda i: (i % tiles_per_sample, 0)
    qs, ks, vs = pl.pallas_call(
        _inproj1_sample_kernel,
        grid=(nst,),
        in_specs=[pl.BlockSpec((ATTN_TILE, d), lambda i: (npt + i, 0)),
                  pl.BlockSpec((1, 6, d), lambda i: (8 + 1 + i // tiles_per_sample, 0, 0))]
        + common + [pl.BlockSpec((ATTN_TILE, LANES), rope_tile),
                    pl.BlockSpec((ATTN_TILE, LANES), rope_tile)],
        out_specs=out_specs,
        out_shape=tuple(jax.ShapeDtypeStruct((nst * ATTN_TILE, d), BF16) for _ in range(3)),
        compiler_params=_params(1),
        name="inproj1_sample",
    )(x, mods, norm_w, w_bf16, cos_t, sin_t)
    return (qp, kp, vp), (qs, ks, vs), (k_cache, v_cache)


def _rope_tables(n_tok):
    half = DIFF_HD // 4
    pos = np.arange(n_tok)
    lane = np.arange(LANES)
    sub = lane % DIFF_HD
    p = np.where(sub[None, :] < DIFF_HD // 2, (pos // GRID_W)[:, None], (pos % GRID_W)[:, None])
    inv = jnp.asarray(ROPE_THETA, F32) ** (-jnp.asarray(sub % half, F32) / half)
    ang = jnp.asarray(p, F32) * inv[None, :]
    sign = np.where((lane % (2 * half)) < half, -1.0, 1.0).astype(np.float32)
    return jnp.cos(ang), jnp.sin(ang) * sign[None, :]


def _diffattn_kernel(*refs, has_cache, lam_init):
    if has_cache:
        q_ref, k_ref, v_ref, ck_ref, cv_ref, lam_ref, sw_ref, o_ref = refs
    else:
        q_ref, k_ref, v_ref, lam_ref, sw_ref, o_ref = refs
    hd2 = 2 * DIFF_HD
    lv = lam_ref[...]
    lam = (jnp.exp(jnp.sum(lv[0:1] * lv[1:2], axis=1, keepdims=True))
           - jnp.exp(jnp.sum(lv[2:3] * lv[3:4], axis=1, keepdims=True)) + lam_init)
    lane = lax.broadcasted_iota(I32, (q_ref.shape[0], hd2), 1)
    for h in range(DIFF_HEADS):
        cols = slice(h * hd2, (h + 1) * hd2)
        q = q_ref[:, cols]
        zero = jnp.zeros_like(q)
        k_new = k_ref[:, cols].astype(BF16)
        values = [v_ref[:, cols].astype(BF16)]
        if has_cache:
            past = ck_ref.shape[1]
            k_past_t = ck_ref[cols, :].astype(BF16)
            values.append(cv_ref[pl.ds(h, past, stride=DIFF_HEADS), :].astype(BF16))
        o = None
        for c in range(2):
            qc = jnp.where((lane < DIFF_HD) == (c == 0), q, zero)
            s = [_dot_nt(qc, k_new)]
            if has_cache:
                s.append(_dot(qc, k_past_t))
            mx = functools.reduce(jnp.maximum, [jnp.max(si, axis=1, keepdims=True) for si in s])
            e = [jnp.exp(si - mx) for si in s]
            z = functools.reduce(jnp.add, [jnp.sum(ei, axis=1, keepdims=True) for ei in e])
            pv = functools.reduce(jnp.add, [_dot(ei.astype(BF16), v) for ei, v in zip(e, values)])
            pv = pv * (1.0 / z)
            o = pv if c == 0 else o - lam * pv
        o_ref[:, cols] = ((_rms(o) * sw_ref[...]) * (1.0 - lam_init)).astype(BF16)


def _diffattn(q, k, v, lam_vecs, subln_w, batch, seq_len, q_block, lam_init, cache=None):
    d = D_MODEL
    nq = seq_len // q_block
    has_cache = cache is not None
    kv_spec = pl.BlockSpec((seq_len, d), lambda b, qi: (b, 0))
    in_specs = [pl.BlockSpec((q_block, d), lambda b, qi: (b * nq + qi, 0)), kv_spec, kv_spec]
    args = [q, k, v]
    if has_cache:
        past = cache[0].shape[1]
        in_specs += [pl.BlockSpec((d, past), lambda b, qi: (b, 0)),
                     pl.BlockSpec((past * DIFF_HEADS, 2 * DIFF_HD), lambda b, qi: (b, 0))]
        args += list(cache)
    in_specs += [pl.BlockSpec((4, DIFF_HD), lambda b, qi: (0, 0)),
                 pl.BlockSpec((1, 2 * DIFF_HD), lambda b, qi: (0, 0))]
    args += [lam_vecs, subln_w]
    return pl.pallas_call(
        functools.partial(_diffattn_kernel, has_cache=has_cache, lam_init=lam_init),
        grid=(batch, nq),
        in_specs=in_specs,
        out_specs=pl.BlockSpec((q_block, d), lambda b, qi: (b * nq + qi, 0)),
        out_shape=jax.ShapeDtypeStruct((batch * seq_len, d), BF16),
        compiler_params=_params(2),
        name="diffattn_cache" if has_cache else "diffattn",
    )(*args)


def _router_weights(router_group, router_expert):
    w = jnp.concatenate([router_group, router_expert], axis=1)
    w = jnp.pad(w, ((0, 0), (0, LANES - w.shape[1])))
    hi = w.astype(BF16)
    return hi, (w - hi.astype(F32)).astype(BF16)


def _inproj0_weights(w_in):
    gq, gk, gv, gg, gaf, gab, hq, hff, hfb, hi, hg = jnp.split(
        w_in, [256, 512, 1024, 1536, 1552, 1568, 1824, 2080, 2336, 2848], axis=1)
    w = jnp.concatenate([gq, gk, gv, gg, hq, hff, hfb, hi, hg, gaf, gab], axis=1)
    return jnp.pad(w, ((0, 0), (0, AB_COLS - w.shape[1]))).astype(BF16)


def kernel(x_prompt, x_sample, state_gla, state_hgrn, cache_diff_k, cache_diff_v, c, c_ctx,
           w_ada, b_ada, norm1_w, norm2_w, w_in_ab, gla_a2, gla_a_bias, hgrn_lb, gla_onorm_w,
           hgrn_onorm_w, w_out_ab, w_in_c, lam_q1, lam_k1, lam_q2, lam_k2, diff_subln_w, w_out_c,
           router_group, router_expert, moe_w1, moe_w3, moe_w2, final_norm_w):
    bp, lp, d = x_prompt.shape
    bs, ls, _ = x_sample.shape
    depth = w_ada.shape[0]
    assert depth == 2 and d == D_MODEL and bs <= 7
    tp, ts = bp * lp, bs * ls
    npt, nst = tp // ROW_TILE, ts // ROW_TILE
    tps = ls // ROW_TILE
    xp = x_prompt.reshape(tp, d)
    xs = x_sample.reshape(ts, d)

    cond8 = jnp.concatenate([c_ctx[None, :], c, jnp.zeros((7 - bs, d), F32)], axis=0)
    mods = _adaln(cond8, w_ada, b_ada).reshape(depth * 8, 6, d)

    proj = _inproj0(xp, xs, mods, norm1_w[0:1], _inproj0_weights(w_in_ab[0]), ls)
    a_bias = gla_a_bias[0][:, None, :]
    scan_args = (gla_a2[0], a_bias, hgrn_lb, gla_onorm_w[0:1], hgrn_onorm_w[0:1])
    mixed_p, new_state_gla, new_state_hgrn = _scan(proj, 0, bp, lp, *scan_args)
    s0 = jnp.concatenate([state_gla[:, 0], state_hgrn[:, 0]], axis=2).swapaxes(-1, -2)
    s0 = s0.reshape(bs, 2, SCAN_PAIRS, 2, HEAD_DV, HEAD_DK)
    zero = jnp.zeros_like(s0[:, :, :, 0])
    s0 = jnp.concatenate([jnp.concatenate([s0[:, :, :, 0], zero], axis=-1),
                          jnp.concatenate([zero, s0[:, :, :, 1]], axis=-1)], axis=-2)
    mixed_s = _scan(proj, tp, bs, ls, *scan_args, s0=s0)

    wr = _router_weights(router_group[0], router_expert[0])
    x1, *routed = _post((xp, xs), mixed_p, mixed_s, mods, 0, norm2_w[0:1],
                        w_out_ab[0].astype(BF16), *wr, tps)
    ys, tables = _moe(*routed, moe_w1, moe_w3, moe_w2, 0)
    x2 = _combine(*tables, x1, mods, 0, final_norm_w[None, :], ys, 0, npt + nst, npt, tps, False)

    lam_init = 0.8 - 0.6 * math.exp(-0.3 * 1)
    cos_t, sin_t = _rope_tables(ls)
    (qp, kp, vp), (qs, ks, vs), (k_cache, v_cache) = _inproj1(
        x2, mods, norm1_w[1:2], w_in_c[0].astype(BF16), tp // ATTN_TILE, ts // ATTN_TILE,
        ls // ATTN_TILE, cos_t, sin_t)
    lam_vecs = jnp.stack([lam_q1[0], lam_k1[0], lam_q2[0], lam_k2[0]])
    att_p = _diffattn(qp, kp, vp, lam_vecs, diff_subln_w[0:1], bp, lp, lp, lam_init)
    past = cache_diff_k.shape[2]
    assert lp == ATTN_TILE and DIFF_HEADS == TOKEN_ROWS
    cache = (cache_diff_k[:, 0].transpose(0, 2, 3, 4, 1).reshape(bs * d, past),
             cache_diff_v[:, 0].reshape(bs * past * DIFF_HEADS, 2 * DIFF_HD))
    att_s = _diffattn(qs, ks, vs, lam_vecs, diff_subln_w[0:1], bs, ls, SAMPLE_Q_BLOCK, lam_init,
                      cache)

    wr = _router_weights(router_group[1], router_expert[1])
    x3, *routed = _post((x2,), att_p, att_s, mods, 1, norm2_w[1:2],
                        w_out_c[0].astype(BF16), *wr, tps)
    ys, tables = _moe(*routed, moe_w1, moe_w3, moe_w2, 1)
    fw = final_norm_w[None, :]
    y_p = _combine(*tables, x3, mods, 1, fw, ys, 0, npt, npt, tps, True)
    y_s = _combine(*tables, x3, mods, 1, fw, ys, npt, nst, npt, tps, True)

    return (y_p.reshape(bp, lp, d), y_s.reshape(bs, ls, d), new_state_gla, new_state_hgrn,
            k_cache.reshape(bp, 1, DIFF_HEADS, 2, DIFF_HD, lp).transpose(0, 1, 5, 2, 3, 4),
            v_cache.reshape(bp, 1, lp, DIFF_HEADS, 2 * DIFF_HD))
```

```python
import functools
import math

import jax
import jax.numpy as jnp
import numpy as np
from jax import lax
from jax.experimental import pallas as pl
from jax.experimental.pallas import tpu as pltpu

F32 = jnp.float32
BF16 = jnp.bfloat16
I32 = jnp.int32

D_MODEL = 1024
GLA_HEADS = 4
HGRN_HEADS = 4
SCAN_HEADS = GLA_HEADS + HGRN_HEADS
SCAN_PAIRS = SCAN_HEADS // 2
HEAD_DK = 64
HEAD_DV = 128
GATE_RANK = 16
GLA_GATE_NORM = 16.0
DIFF_HEADS = 8
DIFF_HD = 64
GRID_W = 64
ROPE_THETA = 10000.0
N_GROUPS = 4
EXPERTS_PER_GROUP = 8
N_EXPERTS = N_GROUPS * EXPERTS_PER_GROUP
MOE_HIDDEN = 512
EPS = 1e-6
LANES = 128
TOKEN_ROWS = D_MODEL // LANES
NEG_BIG = -1e30
ROUTER_ROWS = 48

ROW_TILE = 512
ATTN_TILE = 256
SAMPLE_Q_BLOCK = 512
ADA_TILE = 1536
INPROJ0_TILE = 512
INPROJ1_TILE = 512
SCAN_CHUNK = 64
EXPERT_TILE = 256
MOVE_BLOCK = 16
STAGE_TOKENS = 2 * ROW_TILE + N_EXPERTS * MOVE_BLOCK
MAX_BLOCKS = STAGE_TOKENS // MOVE_BLOCK
VMEM_LIMIT = 56 * 1024 * 1024
SCAN_INPUT_DOUBLE_BUFFER_BYTES = 16 * 1024 * 1024

_C_GQ, _C_GK, _C_GV, _C_GG = 0, 256, 512, 1024
_C_HQ, _C_HFF, _C_HFB, _C_HI, _C_HG = 1536, 1792, 2048, 2304, 2816
_C_GAF, _C_GAB = 3328, 3344
AB_COLS = 3456


def _params(n_axes, vmem=VMEM_LIMIT):
    return pltpu.CompilerParams(dimension_semantics=("arbitrary",) * n_axes,
                                vmem_limit_bytes=vmem)


def _cdiv(a, b):
    return (a + b - 1) // b


def _dot(a, b):
    return jnp.dot(a, b, preferred_element_type=F32)


def _dot_nt(a, b):
    return lax.dot_general(a, b, (((1,), (1,)), ((), ())), preferred_element_type=F32)


def _dot_tn(a, b):
    return lax.dot_general(a, b, (((0,), (0,)), ((), ())), preferred_element_type=F32)


def _split_bf16(x):
    hi = x.astype(BF16)
    lo = (x - hi.astype(F32)).astype(BF16)
    return hi, lo


def _silu(x):
    return x * jax.nn.sigmoid(x)


def _log_sigmoid(x):
    return jnp.minimum(x, 0.0) - jnp.log(1.0 + jnp.exp(-jnp.abs(x)))


def _rms(x):
    return x * lax.rsqrt(jnp.mean(x * x, axis=-1, keepdims=True) + EPS)


def _modulate(x, norm_w, shift, scale):
    return (_rms(x) * norm_w) * (1.0 + scale) + shift


def _to_token_major(dst_ref, x, row0=0):
    n = x.shape[0]
    for s in range(TOKEN_ROWS):
        dst_ref[pl.ds(row0 + s, n, stride=TOKEN_ROWS), :] = x[:, s * LANES:(s + 1) * LANES]


def _from_token_major(src_ref, n, row0=0):
    return jnp.concatenate([src_ref[pl.ds(row0 + s, n, stride=TOKEN_ROWS), :]
                            for s in range(TOKEN_ROWS)], axis=1)


def _ada_kernel(c_ref, w_ref, b_ref, o_ref):
    s = _silu(c_ref[...])
    o_ref[0] = _dot(s.astype(BF16), w_ref[0].astype(BF16)) + b_ref[0]


def _adaln(cond8, w_ada, b_ada):
    depth, d, n = w_ada.shape
    tn = ADA_TILE
    return pl.pallas_call(
        _ada_kernel,
        grid=(depth, n // tn),
        in_specs=[pl.BlockSpec((8, d), lambda l, j: (0, 0)),
                  pl.BlockSpec((1, d, tn), lambda l, j: (l, 0, j)),
                  pl.BlockSpec((1, 1, tn), lambda l, j: (l, 0, j))],
        out_specs=pl.BlockSpec((1, 8, tn), lambda l, j: (l, 0, j)),
        out_shape=jax.ShapeDtypeStruct((depth, 8, n), F32),
        compiler_params=_params(2),
        name="adaln",
    )(cond8, w_ada, b_ada.reshape(depth, 1, n))


def _mod_row(i, layer, n_prompt_tiles, tiles_per_sample):
    r = jnp.where(i < n_prompt_tiles, 0, 1 + (i - n_prompt_tiles) // tiles_per_sample)
    return layer * 8 + r


def _inproj0_kernel(xp_ref, xs_ref, mod_ref, nw_ref, w_ref, o_ref, *, n_prompt_tiles):
    i = pl.program_id(0)
    x = jnp.where(i < n_prompt_tiles, xp_ref[...], xs_ref[...])
    h = _modulate(x, nw_ref[...], mod_ref[0, 0:1, :], mod_ref[0, 1:2, :])
    o_ref[...] = _dot(h.astype(BF16), w_ref[...])


def _inproj0(xp, xs, mods, norm_w, w_bf16, sample_len):
    tp, d = xp.shape
    ts = xs.shape[0]
    n = w_bf16.shape[1]
    tile = INPROJ0_TILE
    npt, nst = tp // tile, ts // tile
    mod_map = lambda i: (_mod_row(i, 0, npt, sample_len // tile), 0, 0)
    return pl.pallas_call(
        functools.partial(_inproj0_kernel, n_prompt_tiles=npt),
        grid=(npt + nst,),
        in_specs=[pl.BlockSpec((tile, d), lambda i: (jnp.minimum(i, npt - 1), 0)),
                  pl.BlockSpec((tile, d), lambda i: (jnp.maximum(i - npt, 0), 0)),
                  pl.BlockSpec((1, 6, d), mod_map),
                  pl.BlockSpec((1, d), lambda i: (0, 0)),
                  pl.BlockSpec((d, n), lambda i: (0, 0))],
        out_specs=pl.BlockSpec((tile, n), lambda i: (i, 0)),
        out_shape=jax.ShapeDtypeStruct((tp + ts, n), F32),
        compiler_params=_params(1),
        name="inproj0",
    )(xp, xs, mods, norm_w, w_bf16)


def _scan_kernel(*refs, seq_len, has_state):
    if has_state:
        (p_ref, a2_ref, ab_ref, lb_ref, ong_ref, onh_ref, s0_ref, mixed_ref, *scratch) = refs
        sfin_ref = None
    else:
        (p_ref, a2_ref, ab_ref, lb_ref, ong_ref, onh_ref, mixed_ref, sg_ref, sh_ref,
         *scratch) = refs
        s0_ref = None
        sfin_ref = (sg_ref, sh_ref)
    (qi_f, ki_f, qo_f, ko_f, qi_b, ki_b, qo_b, ko_b,
     vv, dec_f, dec_b, o_f, o_b, st_f, st_b) = scratch
    C = SCAN_CHUNK
    n_chunks = seq_len // C
    gqk = GLA_HEADS * HEAD_DK

    row = lax.broadcasted_iota(I32, (C, C), 0)
    col = lax.broadcasted_iota(I32, (C, C), 1)
    lower = col <= row
    upper = col >= row
    tri_lo = jnp.where(lower, 1.0, 0.0).astype(BF16)
    tri_up = jnp.where(upper, 1.0, 0.0).astype(BF16)

    lbp = lb_ref[...]
    lb_max = jnp.maximum(lbp[0], lbp[1])
    lb_e0 = jnp.exp(lbp[0] - lb_max)
    lb_e1 = jnp.exp(lbp[1] - lb_max)
    lb = lb_e0 / (lb_e0 + lb_e1)

    def cumsum_chunk(tri, la):
        hi, lo = _split_bf16(la)
        return _dot(tri, hi) + _dot(tri, lo)

    def prep(n, carry):
        r0 = pl.multiple_of(n * C, C)
        rows = pl.ds(r0, C)
        gq = p_ref[rows, _C_GQ:_C_GQ + gqk] * (HEAD_DK ** -0.5)
        gk = p_ref[rows, _C_GK:_C_GK + gqk]
        hq = _silu(p_ref[rows, _C_HQ:_C_HQ + gqk]) * (HEAD_DK ** -0.5)
        for d_i, (qi_s, ki_s, qo_s, ko_s, dec_s, tri, last, mid) in enumerate(
                ((qi_f, ki_f, qo_f, ko_f, dec_f, tri_lo, C - 1, C // 2 - 1),
                 (qi_b, ki_b, qo_b, ko_b, dec_b, tri_up, 0, C // 2))):
            c_ga = _C_GAF if d_i == 0 else _C_GAB
            c_hf = _C_HFF if d_i == 0 else _C_HFB
            ga = p_ref[rows, c_ga:c_ga + GATE_RANK]
            xg = _dot(ga.astype(BF16), a2_ref[d_i].astype(BF16)) + ab_ref[d_i]
            la_g = _log_sigmoid(xg) / GLA_GATE_NORM
            f = lb[d_i:d_i + 1, :] + (1.0 - lb[d_i:d_i + 1, :]) * jax.nn.sigmoid(
                p_ref[rows, c_hf:c_hf + gqk])
            la_h = jnp.log(f)
            for q, k, la, c0 in ((gq, gk, la_g, 0), (hq, 1.0 - f, la_h, gqk)):
                b = cumsum_chunk(tri, la)
                b_mid, b_end = b[mid:mid + 1, :], b[last:last + 1, :]
                cs = slice(c0, c0 + gqk)
                qi_s[rows, cs] = (q * jnp.exp(b - b_mid)).astype(BF16)
                ki_s[rows, cs] = (k * jnp.exp(b_mid - b)).astype(BF16)
                qo_s[rows, cs] = (q * jnp.exp(b)).astype(BF16)
                ko_s[rows, cs] = (k * jnp.exp(b_end - b)).astype(BF16)
                dec_s[n, :, cs] = jnp.exp(b_end)
        gv_cols = GLA_HEADS * HEAD_DV
        vv[rows, 0:gv_cols] = p_ref[rows, _C_GV:_C_GV + gv_cols].astype(BF16)
        vv[rows, gv_cols:] = p_ref[rows, _C_HI:_C_HI + HGRN_HEADS * HEAD_DV].astype(BF16)
        return carry

    lax.fori_loop(0, n_chunks, prep, 0, unroll=2)

    for p in range(SCAN_PAIRS):
        if has_state:
            st_f[p] = s0_ref[0, 0, p]
            st_b[p] = s0_ref[0, 1, p]
        else:
            st_f[p] = jnp.zeros((2 * HEAD_DV, 2 * HEAD_DK), F32)
            st_b[p] = jnp.zeros((2 * HEAD_DV, 2 * HEAD_DK), F32)

    first_head = lax.broadcasted_iota(I32, (C, 2 * HEAD_DK), 1) < HEAD_DK
    row2 = lax.broadcasted_iota(I32, (2 * C, C), 0) % C
    col2 = lax.broadcasted_iota(I32, (2 * C, C), 1)
    lower2 = col2 <= row2
    upper2 = col2 >= row2

    def per_head_rows(x):
        z = jnp.zeros_like(x)
        return jnp.concatenate([jnp.where(first_head, x, z), jnp.where(first_head, z, x)], axis=0)

    def put_out(o_ref, rows, p, res):
        c0 = p * 2 * HEAD_DV
        o_ref[rows, c0:c0 + HEAD_DV] = res[0:C, 0:HEAD_DV]
        o_ref[rows, c0 + HEAD_DV:c0 + 2 * HEAD_DV] = res[C:2 * C, HEAD_DV:2 * HEAD_DV]

    def sweep(n, carry):
        m = n_chunks - 1 - n
        rows = pl.ds(pl.multiple_of(n * C, C), C)
        rows_m = pl.ds(pl.multiple_of(m * C, C), C)
        decay_f, decay_b = dec_f[n], dec_b[m]
        for p in range(SCAN_PAIRS):
            ks = slice(p * 2 * HEAD_DK, (p + 1) * 2 * HEAD_DK)
            vs = slice(p * 2 * HEAD_DV, (p + 1) * 2 * HEAD_DV)
            vh = vv[rows, vs]
            s_f = st_f[p]
            sc = (jnp.where(lower2, _dot_nt(per_head_rows(qi_f[rows, ks]), ki_f[rows, ks]), 0.0)
                  + jnp.where(upper2, _dot_nt(per_head_rows(qi_b[rows, ks]), ki_b[rows, ks]), 0.0))
            put_out(o_f, rows, p, _dot_nt(per_head_rows(qo_f[rows, ks]), s_f.astype(BF16))
                    + _dot(sc.astype(BF16), vh))
            st_f[p] = decay_f[:, ks] * s_f + _dot_tn(vh, ko_f[rows, ks])
            s_b = st_b[p]
            put_out(o_b, rows_m, p, _dot_nt(per_head_rows(qo_b[rows_m, ks]), s_b.astype(BF16)))
            st_b[p] = decay_b[:, ks] * s_b + _dot_tn(vv[rows_m, vs], ko_b[rows_m, ks])
        return carry

    lax.fori_loop(0, n_chunks, sweep, 0)

    def finish(n, carry):
        rows = pl.ds(pl.multiple_of(n * C, C), C)
        for h in range(SCAN_HEADS):
            vs = slice(h * HEAD_DV, (h + 1) * HEAD_DV)
            if h < GLA_HEADS:
                gate = p_ref[rows, _C_GG + h * HEAD_DV:_C_GG + (h + 1) * HEAD_DV]
                onw = ong_ref[...]
            else:
                hh = h - GLA_HEADS
                gate = p_ref[rows, _C_HG + hh * HEAD_DV:_C_HG + (hh + 1) * HEAD_DV]
                onw = onh_ref[...]
            o = o_f[rows, vs] + o_b[rows, vs]
            mixed_ref[rows, vs] = ((_rms(o) * onw) * _silu(gate)).astype(BF16)
        return carry

    lax.fori_loop(0, n_chunks, finish, 0, unroll=2)

    if sfin_ref is not None:
        for d_i, st in enumerate((st_f, st_b)):
            for p in range(SCAN_PAIRS):
                s_pair = st[p].T
                out_ref = sfin_ref[(2 * p) // GLA_HEADS]
                h0 = (2 * p) % GLA_HEADS
                out_ref[0, 0, d_i, h0] = s_pair[0:HEAD_DK, 0:HEAD_DV]
                out_ref[0, 0, d_i, h0 + 1] = s_pair[HEAD_DK:2 * HEAD_DK, HEAD_DV:2 * HEAD_DV]


def _scan(p, row0, batch, seq_len, a2, a_bias, lb, onorm_g, onorm_h, s0=None):
    n = p.shape[1]
    assert row0 % seq_len == 0
    blk0 = row0 // seq_len
    has_state = s0 is not None
    n_chunks = seq_len // SCAN_CHUNK
    assert GLA_HEADS == HGRN_HEADS and GLA_HEADS % 2 == 0
    st_shape = (1, 1, 2, GLA_HEADS, HEAD_DK, HEAD_DV)
    pair_shape = (SCAN_PAIRS, 2 * HEAD_DV, 2 * HEAD_DK)
    p_mode = dict(pipeline_mode=pl.Buffered(1)) if seq_len * n * 4 > SCAN_INPUT_DOUBLE_BUFFER_BYTES else {}
    in_specs = [pl.BlockSpec((seq_len, n), lambda b: (blk0 + b, 0), **p_mode),
                pl.BlockSpec(a2.shape, lambda b: (0, 0, 0)),
                pl.BlockSpec(a_bias.shape, lambda b: (0, 0, 0)),
                pl.BlockSpec(lb.shape, lambda b: (0, 0, 0)),
                pl.BlockSpec((1, HEAD_DV), lambda b: (0, 0)),
                pl.BlockSpec((1, HEAD_DV), lambda b: (0, 0))]
    args = [p, a2, a_bias, lb, onorm_g, onorm_h]
    mixed_shape = jax.ShapeDtypeStruct((batch * seq_len, D_MODEL), BF16)
    mixed_spec = pl.BlockSpec((seq_len, D_MODEL), lambda b: (b, 0))
    if has_state:
        in_specs.append(pl.BlockSpec((1, 2) + pair_shape, lambda b: (b, 0, 0, 0, 0)))
        args.append(s0)
        out_shape, out_specs = mixed_shape, mixed_spec
    else:
        st_struct = jax.ShapeDtypeStruct((batch,) + st_shape[1:], F32)
        st_spec = pl.BlockSpec(st_shape, lambda b: (b, 0, 0, 0, 0, 0))
        out_shape = (mixed_shape, st_struct, st_struct)
        out_specs = (mixed_spec, st_spec, st_spec)
    qk_cols = SCAN_HEADS * HEAD_DK
    scratch = [pltpu.VMEM((seq_len, qk_cols), BF16) for _ in range(8)]
    scratch += [pltpu.VMEM((seq_len, D_MODEL), BF16),
                pltpu.VMEM((n_chunks, 1, qk_cols), F32), pltpu.VMEM((n_chunks, 1, qk_cols), F32),
                pltpu.VMEM((seq_len, D_MODEL), F32), pltpu.VMEM((seq_len, D_MODEL), F32),
                pltpu.VMEM(pair_shape, F32), pltpu.VMEM(pair_shape, F32)]
    return pl.pallas_call(
        functools.partial(_scan_kernel, seq_len=seq_len, has_state=has_state),
        grid=(batch,),
        in_specs=in_specs, out_specs=out_specs, out_shape=out_shape,
        scratch_shapes=scratch,
        compiler_params=_params(1),
        name="scan_state" if has_state else "scan_fresh",
    )(*args)


def _post_kernel(*refs, split_x, n_prompt_tiles):
    if split_x:
        xp_ref, xs_ref = refs[0], refs[1]
        refs = refs[2:]
    else:
        x_ref = refs[0]
        refs = refs[1:]
    (mp_ref, ms_ref, mod_ref, nw_ref, wo_ref, wrh_ref, wrl_ref,
     x1_ref, h2_ref, slot_ref, wgt_ref, tab_ref, carry, earlier) = refs
    i = pl.program_id(0)
    is_prompt = i < n_prompt_tiles
    if split_x:
        x = jnp.where(is_prompt, xp_ref[...], xs_ref[...])
    else:
        x = x_ref[...]
    mixed = jnp.where(is_prompt, mp_ref[...], ms_ref[...])
    x1 = x + mod_ref[0, 2:3, :] * _dot(mixed, wo_ref[...])
    x1_ref[...] = x1
    h2 = _modulate(x1, nw_ref[...], mod_ref[0, 3:4, :], mod_ref[0, 4:5, :])
    _to_token_major(h2_ref, h2)

    hh, hl = _split_bf16(h2)
    logits = _dot(hh, wrh_ref[...]) + _dot(hl, wrh_ref[...]) + _dot(hh, wrl_ref[...])
    tm = logits.shape[0]
    lt = logits.T[0:ROUTER_ROWS]
    ridx = lax.broadcasted_iota(I32, (ROUTER_ROWS, tm), 0).astype(F32)

    def first_max(v):
        mx = jnp.max(v, axis=0, keepdims=True)
        idx = jnp.min(jnp.where(v == mx, ridx, float(ROUTER_ROWS)), axis=0, keepdims=True)
        return mx, idx

    gl = jnp.where(ridx < N_GROUPS, lt, NEG_BIG)
    gmax, gidx = first_max(gl)
    g_val = 1.0 / jnp.sum(jnp.exp(gl - gmax), axis=0, keepdims=True)
    lo = N_GROUPS + EXPERTS_PER_GROUP * gidx
    el = jnp.where((ridx >= lo) & (ridx < lo + EXPERTS_PER_GROUP), lt, NEG_BIG)
    emax, l1 = first_max(el)
    esum = jnp.sum(jnp.exp(el - emax), axis=0, keepdims=True)
    e2max, l2 = first_max(jnp.where(ridx == l1, NEG_BIG, el))
    p1 = 1.0 / esum
    p2 = jnp.exp(e2max - emax) / esum
    w1 = g_val * (p1 / (p1 + p2))
    w2 = g_val * (p2 / (p1 + p2))

    @pl.when(i == 0)
    def _():
        carry[...] = jnp.zeros_like(carry)
        t_row = lax.broadcasted_iota(I32, earlier.shape, 0)
        t_col = lax.broadcasted_iota(I32, earlier.shape, 1)
        earlier[...] = jnp.where(t_row < t_col, 1.0, 0.0).astype(BF16)

    sel1 = ridx == l1
    sel2 = ridx == l2
    onehot = jnp.where(sel1 | sel2, 1.0, 0.0)
    before = _dot(onehot.astype(BF16), earlier[...])
    count = jnp.sum(onehot, axis=1, keepdims=True)
    blocks = jnp.floor((count + (MOVE_BLOCK - 1.0)) * (1.0 / MOVE_BLOCK)) * MOVE_BLOCK
    r_row = lax.broadcasted_iota(I32, (ROUTER_ROWS, ROUTER_ROWS), 0)
    r_col = lax.broadcasted_iota(I32, (ROUTER_ROWS, ROUTER_ROWS), 1)
    lower_rows = jnp.where(r_col < r_row, 1.0, 0.0).astype(BF16)
    run_start = _dot(lower_rows,
                     jnp.broadcast_to(blocks, (ROUTER_ROWS, LANES)).astype(BF16))[:, 0:1]
    slot = before + run_start
    q1 = jnp.sum(jnp.where(sel1, slot, 0.0), axis=0, keepdims=True)
    q2 = jnp.sum(jnp.where(sel2, slot, 0.0), axis=0, keepdims=True)
    tab_lane = lax.broadcasted_iota(I32, (ROUTER_ROWS, LANES), 1)
    tab_ref[0] = jnp.where(tab_lane == 0, count,
                           jnp.where(tab_lane == 1, carry[...],
                                     jnp.where(tab_lane == 2, run_start, 0.0)))
    carry[...] = carry[...] + count

    slot_ref[0, 0:1, :] = (q1 * TOKEN_ROWS).astype(I32)
    slot_ref[0, 1:2, :] = (q2 * TOKEN_ROWS).astype(I32)
    wgt_ref[0, 0:1, :] = w1
    wgt_ref[0, 1:2, :] = w2


def _post(x_args, mixed_p, mixed_s, mods, layer, norm_w, w_out_bf16, wr_hi, wr_lo,
          tiles_per_sample):
    split_x = len(x_args) == 2
    tp, ts = mixed_p.shape[0], mixed_s.shape[0]
    t, d = tp + ts, D_MODEL
    npt, nst = tp // ROW_TILE, ts // ROW_TILE
    tile = lambda i: (i, 0)
    if split_x:
        x_specs = [pl.BlockSpec((ROW_TILE, d), lambda i: (jnp.minimum(i, npt - 1), 0)),
                   pl.BlockSpec((ROW_TILE, d), lambda i: (jnp.maximum(i - npt, 0), 0))]
    else:
        x_specs = [pl.BlockSpec((ROW_TILE, d), tile)]
    in_specs = x_specs + [
        pl.BlockSpec((ROW_TILE, d), lambda i: (jnp.minimum(i, npt - 1), 0)),
        pl.BlockSpec((ROW_TILE, d), lambda i: (jnp.maximum(i - npt, 0), 0)),
        pl.BlockSpec((1, 6, d), lambda i: (_mod_row(i, layer, npt, tiles_per_sample), 0, 0)),
        pl.BlockSpec((1, d), lambda i: (0, 0)),
        pl.BlockSpec((d, d), lambda i: (0, 0)),
        pl.BlockSpec((d, LANES), lambda i: (0, 0)),
        pl.BlockSpec((d, LANES), lambda i: (0, 0))]
    return pl.pallas_call(
        functools.partial(_post_kernel, split_x=split_x, n_prompt_tiles=npt),
        grid=(npt + nst,),
        in_specs=in_specs,
        out_specs=(pl.BlockSpec((ROW_TILE, d), tile),
                   pl.BlockSpec((ROW_TILE * TOKEN_ROWS, LANES), tile),
                   pl.BlockSpec((1, 2, ROW_TILE), lambda i: (i, 0, 0)),
                   pl.BlockSpec((1, 2, ROW_TILE), lambda i: (i, 0, 0)),
                   pl.BlockSpec((1, ROUTER_ROWS, LANES), lambda i: (i, 0, 0))),
        out_shape=(jax.ShapeDtypeStruct((t, d), F32),
                   jax.ShapeDtypeStruct((t * TOKEN_ROWS, LANES), F32),
                   jax.ShapeDtypeStruct((npt + nst, 2, ROW_TILE), I32),
                   jax.ShapeDtypeStruct((npt + nst, 2, ROW_TILE), F32),
                   jax.ShapeDtypeStruct((npt + nst, ROUTER_ROWS, LANES), F32)),
        scratch_shapes=[pltpu.VMEM((ROUTER_ROWS, LANES), F32),
                        pltpu.VMEM((ROW_TILE, ROW_TILE), BF16)],
        compiler_params=_params(1),
        name=f"post{layer}",
    )(*x_args, mixed_p, mixed_s, mods, norm_w, w_out_bf16, wr_hi, wr_lo)


def _for_blocks(tab_ref, fn):
    block_rows = MOVE_BLOCK * TOKEN_ROWS
    count = tab_ref[0, 0, MAX_BLOCKS]

    def call(k, parity):
        fn(pl.multiple_of(k * block_rows, block_rows),
           pl.multiple_of(tab_ref[0, 0, k], TOKEN_ROWS), parity)

    def body(k2, c):
        call(2 * k2, 0)

        @pl.when(2 * k2 + 1 < count)
        def _():
            call(2 * k2 + 1, 1)
        return c

    lax.fori_loop(0, _cdiv(count, 2), body, 0)


def _wait_blocks(tab_ref, copy):
    def body(k, c):
        copy.wait()
        return c

    lax.fori_loop(0, tab_ref[0, 0, MAX_BLOCKS], body, 0)


def _dispatch_kernel(zero_ref, tab_ref, prev_tab_ref, q_ref, h2_ref, hs_ref, zero_buf, stage, sem):
    j = pl.program_id(0)
    slot = j % 2
    block_rows = MOVE_BLOCK * TOKEN_ROWS

    @pl.when(j == 0)
    def _():
        zero_buf[...] = jnp.zeros_like(zero_buf)

        def zero_copy(k):
            start = pl.multiple_of(zero_ref[k], EXPERT_TILE * TOKEN_ROWS)
            return pltpu.make_async_copy(
                zero_buf, hs_ref.at[pl.ds(start, EXPERT_TILE * TOKEN_ROWS)], sem.at[0])

        def start_zero(k2, c):
            for parity in range(2):
                @pl.when(zero_ref[2 * k2 + parity] >= 0)
                def _():
                    zero_copy(2 * k2 + parity).start(priority=parity)
            return c

        def wait_zero(k, c):
            @pl.when(zero_ref[k] >= 0)
            def _():
                zero_copy(k).wait()
            return c

        lax.fori_loop(0, zero_ref.shape[0] // 2, start_zero, 0)
        stage[...] = jnp.zeros_like(stage)
        lax.fori_loop(0, zero_ref.shape[0], wait_zero, 0)

    def place(r, c):
        tok = h2_ref[pl.ds(pl.multiple_of(r * TOKEN_ROWS, TOKEN_ROWS), TOKEN_ROWS), :]
        for s in range(2):
            row = pl.multiple_of(q_ref[0, s, r], TOKEN_ROWS)
            stage[slot, pl.ds(row, TOKEN_ROWS), :] = tok
        return c

    lax.fori_loop(0, ROW_TILE, place, 0, unroll=8)

    def block_copy(buf, stage_row, sorted_row):
        return pltpu.make_async_copy(stage.at[buf, pl.ds(stage_row, block_rows)],
                                     hs_ref.at[pl.ds(sorted_row, block_rows)], sem.at[buf])

    @pl.when(j > 0)
    def _():
        _wait_blocks(prev_tab_ref, block_copy(1 - slot, 0, 0))

    _for_blocks(tab_ref, lambda a, b, parity: block_copy(slot, a, b).start(priority=parity))

    @pl.when(j == pl.num_programs(0) - 1)
    def _():
        _wait_blocks(tab_ref, block_copy(slot, 0, 0))


def _dispatch(zero_tiles, block_tab, slots, h2, n_rows):
    t = h2.shape[0] // TOKEN_ROWS
    nt = t // ROW_TILE
    assert zero_tiles.shape[0] % 2 == 0
    smem_tile = lambda shape: pl.BlockSpec((1,) + shape, lambda j, *_: (j, 0, 0),
                                           memory_space=pltpu.SMEM)
    grid_spec = pltpu.PrefetchScalarGridSpec(
        num_scalar_prefetch=1,
        grid=(nt,),
        in_specs=[smem_tile((1, LANES)),
                  pl.BlockSpec((1, 1, LANES), lambda j, *_: (jnp.maximum(j - 1, 0), 0, 0),
                               memory_space=pltpu.SMEM),
                  smem_tile((2, ROW_TILE)),
                  pl.BlockSpec((ROW_TILE * TOKEN_ROWS, LANES), lambda j, *_: (j, 0))],
        out_specs=pl.BlockSpec(memory_space=pl.ANY),
        scratch_shapes=[pltpu.VMEM((EXPERT_TILE * TOKEN_ROWS, LANES), F32),
                        pltpu.VMEM((2, STAGE_TOKENS * TOKEN_ROWS, LANES), F32),
                        pltpu.SemaphoreType.DMA((2,))])
    return pl.pallas_call(
        _dispatch_kernel,
        grid_spec=grid_spec,
        out_shape=jax.ShapeDtypeStruct((n_rows * TOKEN_ROWS, LANES), F32),
        compiler_params=_params(1),
        name="dispatch",
    )(zero_tiles, block_tab, block_tab, slots, h2)


def _expert_kernel(te_ref, src_ref, nv_ref, run_ref, nxt_ref, hs_ref, w1_hbm, w3_hbm, w2_hbm,
                   ys_ref, w1f, w3f, w2f, w1b, w3b, w2b, sem, *, layer):
    i = pl.program_id(0)

    def weight_copies(e, buf):
        return [pltpu.make_async_copy(src.at[layer, e], dst.at[buf], sem.at[buf])
                for src, dst in ((w1_hbm, w1f), (w3_hbm, w3f), (w2_hbm, w2f))]

    @pl.when(i == 0)
    def _():
        for c in weight_copies(te_ref[0], 0):
            c.start()

    first = (i == 0) | (run_ref[i] != run_ref[jnp.maximum(i - 1, 0)])

    @pl.when(first)
    def _():
        buf = run_ref[i] % 2
        for c in weight_copies(te_ref[i], buf):
            c.wait()

        @pl.when(nxt_ref[i] >= 0)
        def _():
            for c in weight_copies(nxt_ref[i], 1 - buf):
                c.start()

        w1b[...] = w1f[buf].astype(BF16)
        w3b[...] = w3f[buf].astype(BF16)
        w2b[...] = w2f[buf].astype(BF16)

    @pl.when(nv_ref[i] > 0)
    def _():
        h = _from_token_major(hs_ref, EXPERT_TILE).astype(BF16)
        g = _silu(_dot(h, w1b[...])) * _dot(h, w3b[...])
        _to_token_major(ys_ref, _dot(g.astype(BF16), w2b[...]))

    @pl.when(nv_ref[i] == 0)
    def _():
        ys_ref[...] = jnp.zeros_like(ys_ref)


def _experts(tile_expert, tile_src, tile_rows, hs, w1, w3, w2, layer):
    n_rows, d = hs.shape[0] // TOKEN_ROWS, D_MODEL
    nt = n_rows // EXPERT_TILE
    hid = w1.shape[-1]
    tok_tile = (EXPERT_TILE * TOKEN_ROWS, LANES)
    changed = jnp.concatenate([jnp.zeros((1,), I32),
                               (tile_expert[1:] != tile_expert[:-1]).astype(I32)])
    run = jnp.cumsum(changed).astype(I32)
    later = jnp.where(run[None, :] > run[:, None], tile_expert[None, :], N_EXPERTS)
    next_expert = jnp.min(later, axis=1)
    next_expert = jnp.where(next_expert < N_EXPERTS, next_expert, -1).astype(I32)
    grid_spec = pltpu.PrefetchScalarGridSpec(
        num_scalar_prefetch=5,
        grid=(nt,),
        in_specs=[pl.BlockSpec(tok_tile, lambda i, te, src, nv, run, nxt: (src[i], 0)),
                  pl.BlockSpec(memory_space=pl.ANY), pl.BlockSpec(memory_space=pl.ANY),
                  pl.BlockSpec(memory_space=pl.ANY)],
        out_specs=pl.BlockSpec(tok_tile, lambda i, te, src, nv, run, nxt: (i, 0)),
        scratch_shapes=[pltpu.VMEM((2, d, hid), F32), pltpu.VMEM((2, d, hid), F32),
                        pltpu.VMEM((2, hid, d), F32),
                        pltpu.VMEM((d, hid), BF16), pltpu.VMEM((d, hid), BF16),
                        pltpu.VMEM((hid, d), BF16), pltpu.SemaphoreType.DMA((2,))])
    return pl.pallas_call(
        functools.partial(_expert_kernel, layer=layer),
        grid_spec=grid_spec,
        out_shape=jax.ShapeDtypeStruct(hs.shape, F32),
        compiler_params=_params(1),
        name=f"experts{layer}",
    )(tile_expert, tile_src, tile_rows, run, next_expert, hs, w1, w3, w2)


def _combine_kernel(tab_ref, next_tab_ref, q_ref, w_ref, x1_ref, mod_ref, fw_ref, ys_ref, out_ref,
                    stage, y_tok, sem, *, final_norm):
    i = pl.program_id(0)
    slot = i % 2
    block_rows = MOVE_BLOCK * TOKEN_ROWS

    def block_copy(buf, stage_row, sorted_row):
        return pltpu.make_async_copy(ys_ref.at[pl.ds(sorted_row, block_rows)],
                                     stage.at[buf, pl.ds(stage_row, block_rows)], sem.at[buf])

    def fetch(tab, buf):
        _for_blocks(tab, lambda a, b, parity: block_copy(buf, a, b).start(priority=parity))

    @pl.when(i == 0)
    def _():
        fetch(tab_ref, slot)

    @pl.when(i + 1 < pl.num_programs(0))
    def _():
        fetch(next_tab_ref, 1 - slot)

    _wait_blocks(tab_ref, block_copy(slot, 0, 0))

    def pick(r, c):
        rows = [stage[slot, pl.ds(pl.multiple_of(q_ref[0, s, r], TOKEN_ROWS), TOKEN_ROWS), :]
                for s in range(2)]
        y_tok[pl.ds(pl.multiple_of(r * TOKEN_ROWS, TOKEN_ROWS), TOKEN_ROWS), :] = (
            w_ref[0, 0, r] * rows[0] + w_ref[0, 1, r] * rows[1])
        return c

    lax.fori_loop(0, ROW_TILE, pick, 0, unroll=8)
    x2 = x1_ref[...] + mod_ref[0, 5:6, :] * _from_token_major(y_tok, ROW_TILE)
    if final_norm:
        x2 = _rms(x2) * fw_ref[...]
    out_ref[...] = x2


def _combine(block_tab, slots, weights, x1, mods, layer, final_w, ys, tile0, n_tiles,
             n_prompt_tiles, tiles_per_sample, final_norm):
    d = D_MODEL
    tile = lambda i: (tile0 + i, 0)
    mod_map = lambda i: (_mod_row(tile0 + i, layer, n_prompt_tiles, tiles_per_sample), 0, 0)
    smem_tile = lambda shape: pl.BlockSpec((1,) + shape, lambda i: (tile0 + i, 0, 0),
                                           memory_space=pltpu.SMEM)
    return pl.pallas_call(
        functools.partial(_combine_kernel, final_norm=final_norm),
        grid=(n_tiles,),
        in_specs=[smem_tile((1, LANES)),
                  pl.BlockSpec((1, 1, LANES),
                               lambda i: (tile0 + jnp.minimum(i + 1, n_tiles - 1), 0, 0),
                               memory_space=pltpu.SMEM),
                  smem_tile((2, ROW_TILE)), smem_tile((2, ROW_TILE)),
                  pl.BlockSpec((ROW_TILE, d), tile),
                  pl.BlockSpec((1, 6, d), mod_map),
                  pl.BlockSpec((1, d), lambda i: (0, 0)),
                  pl.BlockSpec(memory_space=pl.ANY)],
        out_specs=pl.BlockSpec((ROW_TILE, d), lambda i: (i, 0)),
        out_shape=jax.ShapeDtypeStruct((n_tiles * ROW_TILE, d), F32),
        scratch_shapes=[pltpu.VMEM((2, STAGE_TOKENS * TOKEN_ROWS, LANES), F32),
                        pltpu.VMEM((ROW_TILE * TOKEN_ROWS, LANES), F32),
                        pltpu.SemaphoreType.DMA((2,))],
        compiler_params=_params(1),
        name=f"combine{layer}_{tile0}",
    )(block_tab, block_tab, slots, weights, x1, mods, final_w, ys)


def _moe(h2, slots, weights, tile_tab, w1, w3, w2, layer):
    t = h2.shape[0] // TOKEN_ROWS
    n_tiles = t // ROW_TILE
    extra_tiles = N_EXPERTS + _cdiv(N_EXPERTS * MOVE_BLOCK, EXPERT_TILE)
    n_rows = 2 * t + extra_tiles * EXPERT_TILE
    nt = n_rows // EXPERT_TILE
    tab = tile_tab[:, N_GROUPS:N_GROUPS + N_EXPERTS, 0:3].transpose(0, 2, 1).astype(I32)
    cnt = tab[-1, 0] + tab[-1, 1]
    tight = _cdiv(cnt, EXPERT_TILE) * EXPERT_TILE
    padded = jnp.where(cnt > 0, _cdiv(cnt + MOVE_BLOCK - 1, EXPERT_TILE) * EXPERT_TILE, 0)
    ends = jnp.cumsum(padded)
    offsets = ends - padded
    tails = jnp.where(cnt > 0, ends - EXPERT_TILE, -1)
    tails2 = jnp.where(padded > tight, ends - 2 * EXPERT_TILE, -1)
    used = ends[-1] // EXPERT_TILE
    tile_start = jnp.arange(nt, dtype=I32) * EXPERT_TILE
    unused = (used + jnp.arange(extra_tiles, dtype=I32)) * EXPERT_TILE
    zero_tiles = jnp.concatenate([tails, tails2, jnp.where(unused < n_rows, unused, -1)])
    zero_tiles = jnp.where(zero_tiles >= 0, zero_tiles * TOKEN_ROWS, -1).astype(I32)
    tile_src = jnp.minimum(jnp.arange(nt, dtype=I32), used - 1)
    tile_expert = jnp.sum((tile_src * EXPERT_TILE)[:, None] >= ends[None, :], axis=1).astype(I32)
    tile_rows = jnp.where(tile_start < ends[-1],
                          jnp.clip(cnt[tile_expert] - (tile_start - offsets[tile_expert]),
                                   0, EXPERT_TILE), 0).astype(I32)
    n_blocks = _cdiv(tab[:, 0], MOVE_BLOCK)
    blocks_through = jnp.cumsum(n_blocks, axis=1)
    k = jnp.arange(MAX_BLOCKS, dtype=I32)
    owner = jnp.sum(blocks_through[:, None, :] <= k[None, :, None], axis=2)
    is_owner = owner[:, :, None] == jnp.arange(N_EXPERTS, dtype=I32)[None, None, :]
    pick = lambda v: jnp.sum(jnp.where(is_owner, v[:, None, :], 0), axis=2)
    run_first = pick(offsets[None, :] + tab[:, 1])
    block_in_run = k[None, :] - pick(blocks_through - n_blocks)
    sorted_row = (run_first + block_in_run * MOVE_BLOCK) * TOKEN_ROWS
    block_tab = jnp.concatenate(
        [sorted_row, blocks_through[:, -1:],
         jnp.zeros((n_tiles, LANES - MAX_BLOCKS - 1), I32)], axis=1).astype(I32)[:, None, :]
    hs = _dispatch(zero_tiles, block_tab, slots, h2, n_rows)
    ys = _experts(tile_expert, tile_src, tile_rows, hs, w1, w3, w2, layer)
    return ys, (block_tab, slots, weights)


def _rope(x, cos, sin_signed):
    lane = lax.broadcasted_iota(I32, (x.shape[0], LANES), 1)
    low = (lane % 32) < 16
    outs = []
    for j in range(x.shape[1] // LANES):
        xb = x[:, j * LANES:(j + 1) * LANES]
        partner = jnp.where(low, pltpu.roll(xb, LANES - 16, 1), pltpu.roll(xb, 16, 1))
        outs.append(xb * cos + partner * sin_signed)
    return jnp.concatenate(outs, axis=1)


def _inproj1_prompt_kernel(x_ref, mod_ref, nw_ref, w_ref, q_ref, k_ref, v_ref, kc_ref, vc_ref):
    d = D_MODEL
    h = _modulate(x_ref[...], nw_ref[...], mod_ref[0, 0:1, :], mod_ref[0, 1:2, :]).astype(BF16)
    q_ref[...] = (_dot(h, w_ref[:, 0:d]) * (DIFF_HD ** -0.5)).astype(BF16)
    k = _dot(h, w_ref[:, d:2 * d])
    v = _dot(h, w_ref[:, 2 * d:3 * d])
    k_ref[...] = k.astype(BF16)
    v_ref[...] = v.astype(BF16)
    for b in range(k.shape[0] // ATTN_TILE):
        kc_ref[b * d:(b + 1) * d, :] = k[b * ATTN_TILE:(b + 1) * ATTN_TILE].T
    _to_token_major(vc_ref, v)


def _inproj1_sample_kernel(x_ref, mod_ref, nw_ref, w_ref, cos_ref, sin_ref, q_ref, k_ref, v_ref):
    d = D_MODEL
    h = _modulate(x_ref[...], nw_ref[...], mod_ref[0, 0:1, :], mod_ref[0, 1:2, :]).astype(BF16)
    cos, sin = cos_ref[...], sin_ref[...]
    q_ref[...] = (_rope(_dot(h, w_ref[:, 0:d]), cos, sin) * (DIFF_HD ** -0.5)).astype(BF16)
    k_ref[...] = _rope(_dot(h, w_ref[:, d:2 * d]), cos, sin).astype(BF16)
    v_ref[...] = _dot(h, w_ref[:, 2 * d:3 * d]).astype(BF16)


def _inproj1(x, mods, norm_w, w_bf16, n_prompt_tiles, n_sample_tiles, tiles_per_sample,
             cos_t, sin_t):
    d = D_MODEL
    rows = INPROJ1_TILE
    npt, nst = n_prompt_tiles, n_sample_tiles
    common = [pl.BlockSpec((1, d), lambda i: (0, 0)), pl.BlockSpec((d, 3 * d), lambda i: (0, 0))]
    tile = lambda i: (i, 0)
    out_specs = tuple(pl.BlockSpec((rows, d), tile) for _ in range(3))
    qp, kp, vp, k_cache, v_cache = pl.pallas_call(
        _inproj1_prompt_kernel,
        grid=(npt,),
        in_specs=[pl.BlockSpec((rows, d), tile),
                  pl.BlockSpec((1, 6, d), lambda i: (8, 0, 0))] + common,
        out_specs=out_specs + (pl.BlockSpec((rows // ATTN_TILE * d, ATTN_TILE), tile),
                               pl.BlockSpec((rows * TOKEN_ROWS, LANES), tile)),
        out_shape=tuple(jax.ShapeDtypeStruct((npt * rows, d), BF16) for _ in range(3))
        + (jax.ShapeDtypeStruct((npt * rows // ATTN_TILE * d, ATTN_TILE), F32),
           jax.ShapeDtypeStruct((npt * rows * TOKEN_ROWS, LANES), F32)),
        compiler_params=_params(1),
        name="inproj1_prompt",
    )(x, mods, norm_w, w_bf16)
    rope_tile = lambda i: (i % tiles_per_sample, 0)
    qs, ks, vs = pl.pallas_call(
        _inproj1_sample_kernel,
        grid=(nst,),
        in_specs=[pl.BlockSpec((rows, d), lambda i: (npt + i, 0)),
                  pl.BlockSpec((1, 6, d), lambda i: (8 + 1 + i // tiles_per_sample, 0, 0))]
        + common + [pl.BlockSpec((rows, LANES), rope_tile),
                    pl.BlockSpec((rows, LANES), rope_tile)],
        out_specs=out_specs,
        out_shape=tuple(jax.ShapeDtypeStruct((nst * rows, d), BF16) for _ in range(3)),
        compiler_params=_params(1),
        name="inproj1_sample",
    )(x, mods, norm_w, w_bf16, cos_t, sin_t)
    return (qp, kp, vp), (qs, ks, vs), (k_cache, v_cache)


def _rope_tables(n_tok):
    half = DIFF_HD // 4
    pos = np.arange(n_tok)
    lane = np.arange(LANES)
    sub = lane % DIFF_HD
    p = np.where(sub[None, :] < DIFF_HD // 2, (pos // GRID_W)[:, None], (pos % GRID_W)[:, None])
    inv = jnp.asarray(ROPE_THETA, F32) ** (-jnp.asarray(sub % half, F32) / half)
    ang = jnp.asarray(p, F32) * inv[None, :]
    sign = np.where((lane % (2 * half)) < half, -1.0, 1.0).astype(np.float32)
    return jnp.cos(ang), jnp.sin(ang) * sign[None, :]


def _diffattn_kernel(*refs, has_cache, lam_init):
    if has_cache:
        q_ref, k_ref, v_ref, ck_ref, cv_ref, lam_ref, sw_ref, o_ref = refs
    else:
        q_ref, k_ref, v_ref, lam_ref, sw_ref, o_ref = refs
    hd2 = 2 * DIFF_HD
    lv = lam_ref[...]
    lam = (jnp.exp(jnp.sum(lv[0:1] * lv[1:2], axis=1, keepdims=True))
           - jnp.exp(jnp.sum(lv[2:3] * lv[3:4], axis=1, keepdims=True)) + lam_init)
    lane = lax.broadcasted_iota(I32, (q_ref.shape[0], hd2), 1)
    for h in range(DIFF_HEADS):
        cols = slice(h * hd2, (h + 1) * hd2)
        q = q_ref[:, cols]
        zero = jnp.zeros_like(q)
        k_new = k_ref[:, cols].astype(BF16)
        values = [v_ref[:, cols].astype(BF16)]
        if has_cache:
            past = ck_ref.shape[1]
            k_past_t = ck_ref[cols, :].astype(BF16)
            values.append(cv_ref[pl.ds(h, past, stride=DIFF_HEADS), :].astype(BF16))
        o = None
        for c in range(2):
            qc = jnp.where((lane < DIFF_HD) == (c == 0), q, zero)
            s = [_dot_nt(qc, k_new)]
            if has_cache:
                s.append(_dot(qc, k_past_t))
            mx = functools.reduce(jnp.maximum, [jnp.max(si, axis=1, keepdims=True) for si in s])
            e = [jnp.exp(si - mx) for si in s]
            z = functools.reduce(jnp.add, [jnp.sum(ei, axis=1, keepdims=True) for ei in e])
            pv = functools.reduce(jnp.add, [_dot(ei.astype(BF16), v) for ei, v in zip(e, values)])
            pv = pv * (1.0 / z)
            o = pv if c == 0 else o - lam * pv
        o_ref[:, cols] = ((_rms(o) * sw_ref[...]) * (1.0 - lam_init)).astype(BF16)


def _diffattn(q, k, v, lam_vecs, subln_w, batch, seq_len, q_block, lam_init, cache=None):
    d = D_MODEL
    nq = seq_len // q_block
    has_cache = cache is not None
    kv_spec = pl.BlockSpec((seq_len, d), lambda b, qi: (b, 0))
    in_specs = [pl.BlockSpec((q_block, d), lambda b, qi: (b * nq + qi, 0)), kv_spec, kv_spec]
    args = [q, k, v]
    if has_cache:
        past = cache[0].shape[1]
        in_specs += [pl.BlockSpec((d, past), lambda b, qi: (b, 0)),
                     pl.BlockSpec((past * DIFF_HEADS, 2 * DIFF_HD), lambda b, qi: (b, 0))]
        args += list(cache)
    in_specs += [pl.BlockSpec((4, DIFF_HD), lambda b, qi: (0, 0)),
                 pl.BlockSpec((1, 2 * DIFF_HD), lambda b, qi: (0, 0))]
    args += [lam_vecs, subln_w]
    return pl.pallas_call(
        functools.partial(_diffattn_kernel, has_cache=has_cache, lam_init=lam_init),
        grid=(batch, nq),
        in_specs=in_specs,
        out_specs=pl.BlockSpec((q_block, d), lambda b, qi: (b * nq + qi, 0)),
        out_shape=jax.ShapeDtypeStruct((batch * seq_len, d), BF16),
        compiler_params=_params(2),
        name="diffattn_cache" if has_cache else "diffattn",
    )(*args)


def _router_weights(router_group, router_expert):
    w = jnp.concatenate([router_group, router_expert], axis=1)
    w = jnp.pad(w, ((0, 0), (0, LANES - w.shape[1])))
    hi = w.astype(BF16)
    return hi, (w - hi.astype(F32)).astype(BF16)


def _inproj0_weights(w_in):
    gq, gk, gv, gg, gaf, gab, hq, hff, hfb, hi, hg = jnp.split(
        w_in, [256, 512, 1024, 1536, 1552, 1568, 1824, 2080, 2336, 2848], axis=1)
    w = jnp.concatenate([gq, gk, gv, gg, hq, hff, hfb, hi, hg, gaf, gab], axis=1)
    return jnp.pad(w, ((0, 0), (0, AB_COLS - w.shape[1]))).astype(BF16)


def kernel(x_prompt, x_sample, state_gla, state_hgrn, cache_diff_k, cache_diff_v, c, c_ctx,
           w_ada, b_ada, norm1_w, norm2_w, w_in_ab, gla_a2, gla_a_bias, hgrn_lb, gla_onorm_w,
           hgrn_onorm_w, w_out_ab, w_in_c, lam_q1, lam_k1, lam_q2, lam_k2, diff_subln_w, w_out_c,
           router_group, router_expert, moe_w1, moe_w3, moe_w2, final_norm_w):
    bp, lp, d = x_prompt.shape
    bs, ls, _ = x_sample.shape
    depth = w_ada.shape[0]
    assert depth == 2 and d == D_MODEL and bs <= 7
    tp, ts = bp * lp, bs * ls
    npt, nst = tp // ROW_TILE, ts // ROW_TILE
    tps = ls // ROW_TILE
    xp = x_prompt.reshape(tp, d)
    xs = x_sample.reshape(ts, d)

    cond8 = jnp.concatenate([c_ctx[None, :], c, jnp.zeros((7 - bs, d), F32)], axis=0)
    mods = _adaln(cond8, w_ada, b_ada).reshape(depth * 8, 6, d)

    proj = _inproj0(xp, xs, mods, norm1_w[0:1], _inproj0_weights(w_in_ab[0]), ls)
    a_bias = gla_a_bias[0][:, None, :]
    scan_args = (gla_a2[0], a_bias, hgrn_lb, gla_onorm_w[0:1], hgrn_onorm_w[0:1])
    mixed_p, new_state_gla, new_state_hgrn = _scan(proj, 0, bp, lp, *scan_args)
    s0 = jnp.concatenate([state_gla[:, 0], state_hgrn[:, 0]], axis=2).swapaxes(-1, -2)
    s0 = s0.reshape(bs, 2, SCAN_PAIRS, 2, HEAD_DV, HEAD_DK)
    zero = jnp.zeros_like(s0[:, :, :, 0])
    s0 = jnp.concatenate([jnp.concatenate([s0[:, :, :, 0], zero], axis=-1),
                          jnp.concatenate([zero, s0[:, :, :, 1]], axis=-1)], axis=-2)
    mixed_s = _scan(proj, tp, bs, ls, *scan_args, s0=s0)

    wr = _router_weights(router_group[0], router_expert[0])
    x1, *routed = _post((xp, xs), mixed_p, mixed_s, mods, 0, norm2_w[0:1],
                        w_out_ab[0].astype(BF16), *wr, tps)
    ys, tables = _moe(*routed, moe_w1, moe_w3, moe_w2, 0)
    x2 = _combine(*tables, x1, mods, 0, final_norm_w[None, :], ys, 0, npt + nst, npt, tps, False)

    lam_init = 0.8 - 0.6 * math.exp(-0.3 * 1)
    cos_t, sin_t = _rope_tables(ls)
    (qp, kp, vp), (qs, ks, vs), (k_cache, v_cache) = _inproj1(
        x2, mods, norm1_w[1:2], w_in_c[0].astype(BF16), tp // INPROJ1_TILE, ts // INPROJ1_TILE,
        ls // INPROJ1_TILE, cos_t, sin_t)
    lam_vecs = jnp.stack([lam_q1[0], lam_k1[0], lam_q2[0], lam_k2[0]])
    att_p = _diffattn(qp, kp, vp, lam_vecs, diff_subln_w[0:1], bp, lp, lp, lam_init)
    past = cache_diff_k.shape[2]
    assert lp == ATTN_TILE and DIFF_HEADS == TOKEN_ROWS
    cache = (cache_diff_k[:, 0].transpose(0, 2, 3, 4, 1).reshape(bs * d, past),
             cache_diff_v[:, 0].reshape(bs * past * DIFF_HEADS, 2 * DIFF_HD))
    att_s = _diffattn(qs, ks, vs, lam_vecs, diff_subln_w[0:1], bs, ls, SAMPLE_Q_BLOCK, lam_init,
                      cache)

    wr = _router_weights(router_group[1], router_expert[1])
    x3, *routed = _post((x2,), att_p, att_s, mods, 1, norm2_w[1:2],
                        w_out_c[0].astype(BF16), *wr, tps)
    ys, tables = _moe(*routed, moe_w1, moe_w3, moe_w2, 1)
    fw = final_norm_w[None, :]
    y_p = _combine(*tables, x3, mods, 1, fw, ys, 0, npt, npt, tps, True)
    y_s = _combine(*tables, x3, mods, 1, fw, ys, npt, nst, npt, tps, True)

    return (y_p.reshape(bp, lp, d), y_s.reshape(bs, ls, d), new_state_gla, new_state_hgrn,
            k_cache.reshape(bp, 1, DIFF_HEADS, 2, DIFF_HD, lp).transpose(0, 1, 5, 2, 3, 4),
            v_cache.reshape(bp, 1, lp, DIFF_HEADS, 2 * DIFF_HD))
```

```python
import functools
import math

import jax
import jax.numpy as jnp
import numpy as np
from jax import lax
from jax.experimental import pallas as pl
from jax.experimental.pallas import tpu as pltpu

F32 = jnp.float32
BF16 = jnp.bfloat16
I32 = jnp.int32

D_MODEL = 1024
GLA_HEADS = 4
HGRN_HEADS = 4
SCAN_HEADS = GLA_HEADS + HGRN_HEADS
SCAN_PAIRS = SCAN_HEADS // 2
HEAD_DK = 64
HEAD_DV = 128
GATE_RANK = 16
GLA_GATE_NORM = 16.0
DIFF_HEADS = 8
DIFF_HD = 64
GRID_W = 64
ROPE_THETA = 10000.0
N_GROUPS = 4
EXPERTS_PER_GROUP = 8
N_EXPERTS = N_GROUPS * EXPERTS_PER_GROUP
MOE_HIDDEN = 512
EPS = 1e-6
LANES = 128
TOKEN_ROWS = D_MODEL // LANES
NEG_BIG = -1e30
ROUTER_ROWS = 48

ROW_TILE = 512
ATTN_TILE = 256
SAMPLE_Q_BLOCK = 512
ADA_TILE = 1536
INPROJ0_TILE = 512
INPROJ1_TILE = 512
SCAN_CHUNK = 64
EXPERT_TILE = 256
MOVE_BLOCK = 16
STAGE_TOKENS = 2 * ROW_TILE + N_EXPERTS * MOVE_BLOCK
MAX_BLOCKS = STAGE_TOKENS // MOVE_BLOCK
VMEM_LIMIT = 56 * 1024 * 1024
SCAN_INPUT_DOUBLE_BUFFER_BYTES = 16 * 1024 * 1024

_C_GQ, _C_GK, _C_GV, _C_GG = 0, 256, 512, 1024
_C_HQ, _C_HFF, _C_HFB, _C_HI, _C_HG = 1536, 1792, 2048, 2304, 2816
_C_GAF, _C_GAB = 3328, 3344
AB_COLS = 3456


def _params(n_axes, vmem=VMEM_LIMIT):
    return pltpu.CompilerParams(dimension_semantics=("arbitrary",) * n_axes,
                                vmem_limit_bytes=vmem)


def _cdiv(a, b):
    return (a + b - 1) // b


def _dot(a, b):
    return jnp.dot(a, b, preferred_element_type=F32)


def _dot_nt(a, b):
    return lax.dot_general(a, b, (((1,), (1,)), ((), ())), preferred_element_type=F32)


def _dot_tn(a, b):
    return lax.dot_general(a, b, (((0,), (0,)), ((), ())), preferred_element_type=F32)


def _split_bf16(x):
    hi = x.astype(BF16)
    lo = (x - hi.astype(F32)).astype(BF16)
    return hi, lo


def _silu(x):
    return x * jax.nn.sigmoid(x)


def _log_sigmoid(x):
    return jnp.minimum(x, 0.0) - jnp.log(1.0 + jnp.exp(-jnp.abs(x)))


def _rms(x):
    return x * lax.rsqrt(jnp.mean(x * x, axis=-1, keepdims=True) + EPS)


def _modulate(x, norm_w, shift, scale):
    return (_rms(x) * norm_w) * (1.0 + scale) + shift


def _to_token_major(dst_ref, x, row0=0):
    n = x.shape[0]
    for s in range(TOKEN_ROWS):
        dst_ref[pl.ds(row0 + s, n, stride=TOKEN_ROWS), :] = x[:, s * LANES:(s + 1) * LANES]


def _from_token_major(src_ref, n, row0=0):
    return jnp.concatenate([src_ref[pl.ds(row0 + s, n, stride=TOKEN_ROWS), :]
                            for s in range(TOKEN_ROWS)], axis=1)


def _ada_kernel(c_ref, w_ref, b_ref, o_ref):
    s = _silu(c_ref[...])
    o_ref[0] = _dot(s.astype(BF16), w_ref[0].astype(BF16)) + b_ref[0]


def _adaln(cond8, w_ada, b_ada):
    depth, d, n = w_ada.shape
    tn = ADA_TILE
    return pl.pallas_call(
        _ada_kernel,
        grid=(depth, n // tn),
        in_specs=[pl.BlockSpec((8, d), lambda l, j: (0, 0)),
                  pl.BlockSpec((1, d, tn), lambda l, j: (l, 0, j)),
                  pl.BlockSpec((1, 1, tn), lambda l, j: (l, 0, j))],
        out_specs=pl.BlockSpec((1, 8, tn), lambda l, j: (l, 0, j)),
        out_shape=jax.ShapeDtypeStruct((depth, 8, n), F32),
        compiler_params=_params(2),
        name="adaln",
    )(cond8, w_ada, b_ada.reshape(depth, 1, n))


def _mod_row(i, layer, n_prompt_tiles, tiles_per_sample):
    r = jnp.where(i < n_prompt_tiles, 0, 1 + (i - n_prompt_tiles) // tiles_per_sample)
    return layer * 8 + r


def _inproj0_kernel(xp_ref, xs_ref, mod_ref, nw_ref, w_ref, o_ref, *, n_prompt_tiles):
    i = pl.program_id(0)
    x = jnp.where(i < n_prompt_tiles, xp_ref[...], xs_ref[...])
    h = _modulate(x, nw_ref[...], mod_ref[0, 0:1, :], mod_ref[0, 1:2, :])
    o_ref[...] = _dot(h.astype(BF16), w_ref[...])


def _inproj0(xp, xs, mods, norm_w, w_bf16, sample_len):
    tp, d = xp.shape
    ts = xs.shape[0]
    n = w_bf16.shape[1]
    tile = INPROJ0_TILE
    npt, nst = tp // tile, ts // tile
    mod_map = lambda i: (_mod_row(i, 0, npt, sample_len // tile), 0, 0)
    return pl.pallas_call(
        functools.partial(_inproj0_kernel, n_prompt_tiles=npt),
        grid=(npt + nst,),
        in_specs=[pl.BlockSpec((tile, d), lambda i: (jnp.minimum(i, npt - 1), 0)),
                  pl.BlockSpec((tile, d), lambda i: (jnp.maximum(i - npt, 0), 0)),
                  pl.BlockSpec((1, 6, d), mod_map),
                  pl.BlockSpec((1, d), lambda i: (0, 0)),
                  pl.BlockSpec((d, n), lambda i: (0, 0))],
        out_specs=pl.BlockSpec((tile, n), lambda i: (i, 0)),
        out_shape=jax.ShapeDtypeStruct((tp + ts, n), F32),
        compiler_params=_params(1),
        name="inproj0",
    )(xp, xs, mods, norm_w, w_bf16)


def _scan_kernel(*refs, seq_len, has_state):
    if has_state:
        (p_ref, a2_ref, ab_ref, lb_ref, ong_ref, onh_ref, s0_ref, mixed_ref, *scratch) = refs
        sfin_ref = None
    else:
        (p_ref, a2_ref, ab_ref, lb_ref, ong_ref, onh_ref, mixed_ref, sg_ref, sh_ref,
         *scratch) = refs
        s0_ref = None
        sfin_ref = (sg_ref, sh_ref)
    (qi_f, ki_f, qo_f, ko_f, qi_b, ki_b, qo_b, ko_b,
     vv, dec_f, dec_b, o_f, o_b, st_f, st_b) = scratch
    C = SCAN_CHUNK
    n_chunks = seq_len // C
    gqk = GLA_HEADS * HEAD_DK

    row = lax.broadcasted_iota(I32, (C, C), 0)
    col = lax.broadcasted_iota(I32, (C, C), 1)
    lower = col <= row
    upper = col >= row
    tri_lo = jnp.where(lower, 1.0, 0.0).astype(BF16)
    tri_up = jnp.where(upper, 1.0, 0.0).astype(BF16)

    lbp = lb_ref[...]
    lb_max = jnp.maximum(lbp[0], lbp[1])
    lb_e0 = jnp.exp(lbp[0] - lb_max)
    lb_e1 = jnp.exp(lbp[1] - lb_max)
    lb = lb_e0 / (lb_e0 + lb_e1)

    def cumsum_chunk(tri, la):
        hi, lo = _split_bf16(la)
        return _dot(tri, hi) + _dot(tri, lo)

    def prep(n, carry):
        r0 = pl.multiple_of(n * C, C)
        rows = pl.ds(r0, C)
        gq = p_ref[rows, _C_GQ:_C_GQ + gqk] * (HEAD_DK ** -0.5)
        gk = p_ref[rows, _C_GK:_C_GK + gqk]
        hq = _silu(p_ref[rows, _C_HQ:_C_HQ + gqk]) * (HEAD_DK ** -0.5)
        for d_i, (qi_s, ki_s, qo_s, ko_s, dec_s, tri, last, mid) in enumerate(
                ((qi_f, ki_f, qo_f, ko_f, dec_f, tri_lo, C - 1, C // 2 - 1),
                 (qi_b, ki_b, qo_b, ko_b, dec_b, tri_up, 0, C // 2))):
            c_ga = _C_GAF if d_i == 0 else _C_GAB
            c_hf = _C_HFF if d_i == 0 else _C_HFB
            ga = p_ref[rows, c_ga:c_ga + GATE_RANK]
            xg = _dot(ga.astype(BF16), a2_ref[d_i].astype(BF16)) + ab_ref[d_i]
            la_g = _log_sigmoid(xg) / GLA_GATE_NORM
            f = lb[d_i:d_i + 1, :] + (1.0 - lb[d_i:d_i + 1, :]) * jax.nn.sigmoid(
                p_ref[rows, c_hf:c_hf + gqk])
            la_h = jnp.log(f)
            for q, k, la, c0 in ((gq, gk, la_g, 0), (hq, 1.0 - f, la_h, gqk)):
                b = cumsum_chunk(tri, la)
                b_mid, b_end = b[mid:mid + 1, :], b[last:last + 1, :]
                cs = slice(c0, c0 + gqk)
                qi_s[rows, cs] = (q * jnp.exp(b - b_mid)).astype(BF16)
                ki_s[rows, cs] = (k * jnp.exp(b_mid - b)).astype(BF16)
                qo_s[rows, cs] = (q * jnp.exp(b)).astype(BF16)
                ko_s[rows, cs] = (k * jnp.exp(b_end - b)).astype(BF16)
                dec_s[n, :, cs] = jnp.exp(b_end)
        gv_cols = GLA_HEADS * HEAD_DV
        vv[rows, 0:gv_cols] = p_ref[rows, _C_GV:_C_GV + gv_cols].astype(BF16)
        vv[rows, gv_cols:] = p_ref[rows, _C_HI:_C_HI + HGRN_HEADS * HEAD_DV].astype(BF16)
        return carry

    lax.fori_loop(0, n_chunks, prep, 0, unroll=4)

    for p in range(SCAN_PAIRS):
        if has_state:
            st_f[p] = s0_ref[0, 0, p]
            st_b[p] = s0_ref[0, 1, p]
        else:
            st_f[p] = jnp.zeros((2 * HEAD_DV, 2 * HEAD_DK), F32)
            st_b[p] = jnp.zeros((2 * HEAD_DV, 2 * HEAD_DK), F32)

    first_head = lax.broadcasted_iota(I32, (C, 2 * HEAD_DK), 1) < HEAD_DK
    row2 = lax.broadcasted_iota(I32, (2 * C, C), 0) % C
    col2 = lax.broadcasted_iota(I32, (2 * C, C), 1)
    lower2 = col2 <= row2
    upper2 = col2 >= row2

    def per_head_rows(x):
        z = jnp.zeros_like(x)
        return jnp.concatenate([jnp.where(first_head, x, z), jnp.where(first_head, z, x)], axis=0)

    def put_out(o_ref, rows, p, res):
        c0 = p * 2 * HEAD_DV
        o_ref[rows, c0:c0 + HEAD_DV] = res[0:C, 0:HEAD_DV]
        o_ref[rows, c0 + HEAD_DV:c0 + 2 * HEAD_DV] = res[C:2 * C, HEAD_DV:2 * HEAD_DV]

    def sweep(n, carry):
        m = n_chunks - 1 - n
        rows = pl.ds(pl.multiple_of(n * C, C), C)
        rows_m = pl.ds(pl.multiple_of(m * C, C), C)
        decay_f, decay_b = dec_f[n], dec_b[m]
        for p in range(SCAN_PAIRS):
            ks = slice(p * 2 * HEAD_DK, (p + 1) * 2 * HEAD_DK)
            vs = slice(p * 2 * HEAD_DV, (p + 1) * 2 * HEAD_DV)
            vh = vv[rows, vs]
            s_f = st_f[p]
            sc = (jnp.where(lower2, _dot_nt(per_head_rows(qi_f[rows, ks]), ki_f[rows, ks]), 0.0)
                  + jnp.where(upper2, _dot_nt(per_head_rows(qi_b[rows, ks]), ki_b[rows, ks]), 0.0))
            put_out(o_f, rows, p, _dot_nt(per_head_rows(qo_f[rows, ks]), s_f.astype(BF16))
                    + _dot(sc.astype(BF16), vh))
            st_f[p] = decay_f[:, ks] * s_f + _dot_tn(vh, ko_f[rows, ks])
            s_b = st_b[p]
            put_out(o_b, rows_m, p, _dot_nt(per_head_rows(qo_b[rows_m, ks]), s_b.astype(BF16)))
            st_b[p] = decay_b[:, ks] * s_b + _dot_tn(vv[rows_m, vs], ko_b[rows_m, ks])
        return carry

    lax.fori_loop(0, n_chunks, sweep, 0, unroll=4)

    def finish(n, carry):
        rows = pl.ds(pl.multiple_of(n * C, C), C)
        for h in range(SCAN_HEADS):
            vs = slice(h * HEAD_DV, (h + 1) * HEAD_DV)
            if h < GLA_HEADS:
                gate = p_ref[rows, _C_GG + h * HEAD_DV:_C_GG + (h + 1) * HEAD_DV]
                onw = ong_ref[...]
            else:
                hh = h - GLA_HEADS
                gate = p_ref[rows, _C_HG + hh * HEAD_DV:_C_HG + (hh + 1) * HEAD_DV]
                onw = onh_ref[...]
            o = o_f[rows, vs] + o_b[rows, vs]
            mixed_ref[rows, vs] = ((_rms(o) * onw) * _silu(gate)).astype(BF16)
        return carry

    lax.fori_loop(0, n_chunks, finish, 0, unroll=4)

    if sfin_ref is not None:
        for d_i, st in enumerate((st_f, st_b)):
            for p in range(SCAN_PAIRS):
                s_pair = st[p].T
                out_ref = sfin_ref[(2 * p) // GLA_HEADS]
                h0 = (2 * p) % GLA_HEADS
                out_ref[0, 0, d_i, h0] = s_pair[0:HEAD_DK, 0:HEAD_DV]
                out_ref[0, 0, d_i, h0 + 1] = s_pair[HEAD_DK:2 * HEAD_DK, HEAD_DV:2 * HEAD_DV]


def _scan(p, row0, batch, seq_len, a2, a_bias, lb, onorm_g, onorm_h, s0=None):
    n = p.shape[1]
    assert row0 % seq_len == 0
    blk0 = row0 // seq_len
    has_state = s0 is not None
    n_chunks = seq_len // SCAN_CHUNK
    assert GLA_HEADS == HGRN_HEADS and GLA_HEADS % 2 == 0
    st_shape = (1, 1, 2, GLA_HEADS, HEAD_DK, HEAD_DV)
    pair_shape = (SCAN_PAIRS, 2 * HEAD_DV, 2 * HEAD_DK)
    p_mode = dict(pipeline_mode=pl.Buffered(1)) if seq_len * n * 4 > SCAN_INPUT_DOUBLE_BUFFER_BYTES else {}
    in_specs = [pl.BlockSpec((seq_len, n), lambda b: (blk0 + b, 0), **p_mode),
                pl.BlockSpec(a2.shape, lambda b: (0, 0, 0)),
                pl.BlockSpec(a_bias.shape, lambda b: (0, 0, 0)),
                pl.BlockSpec(lb.shape, lambda b: (0, 0, 0)),
                pl.BlockSpec((1, HEAD_DV), lambda b: (0, 0)),
                pl.BlockSpec((1, HEAD_DV), lambda b: (0, 0))]
    args = [p, a2, a_bias, lb, onorm_g, onorm_h]
    mixed_shape = jax.ShapeDtypeStruct((batch * seq_len, D_MODEL), BF16)
    mixed_spec = pl.BlockSpec((seq_len, D_MODEL), lambda b: (b, 0))
    if has_state:
        in_specs.append(pl.BlockSpec((1, 2) + pair_shape, lambda b: (b, 0, 0, 0, 0)))
        args.append(s0)
        out_shape, out_specs = mixed_shape, mixed_spec
    else:
        st_struct = jax.ShapeDtypeStruct((batch,) + st_shape[1:], F32)
        st_spec = pl.BlockSpec(st_shape, lambda b: (b, 0, 0, 0, 0, 0))
        out_shape = (mixed_shape, st_struct, st_struct)
        out_specs = (mixed_spec, st_spec, st_spec)
    qk_cols = SCAN_HEADS * HEAD_DK
    scratch = [pltpu.VMEM((seq_len, qk_cols), BF16) for _ in range(8)]
    scratch += [pltpu.VMEM((seq_len, D_MODEL), BF16),
                pltpu.VMEM((n_chunks, 1, qk_cols), F32), pltpu.VMEM((n_chunks, 1, qk_cols), F32),
                pltpu.VMEM((seq_len, D_MODEL), F32), pltpu.VMEM((seq_len, D_MODEL), F32),
                pltpu.VMEM(pair_shape, F32), pltpu.VMEM(pair_shape, F32)]
    return pl.pallas_call(
        functools.partial(_scan_kernel, seq_len=seq_len, has_state=has_state),
        grid=(batch,),
        in_specs=in_specs, out_specs=out_specs, out_shape=out_shape,
        scratch_shapes=scratch,
        compiler_params=_params(1),
        name="scan_state" if has_state else "scan_fresh",
    )(*args)


def _post_kernel(*refs, split_x, n_prompt_tiles):
    if split_x:
        xp_ref, xs_ref = refs[0], refs[1]
        refs = refs[2:]
    else:
        x_ref = refs[0]
        refs = refs[1:]
    (mp_ref, ms_ref, mod_ref, nw_ref, wo_ref, wrh_ref, wrl_ref,
     x1_ref, h2_ref, slot_ref, wgt_ref, tab_ref, carry, earlier) = refs
    i = pl.program_id(0)
    is_prompt = i < n_prompt_tiles
    if split_x:
        x = jnp.where(is_prompt, xp_ref[...], xs_ref[...])
    else:
        x = x_ref[...]
    mixed = jnp.where(is_prompt, mp_ref[...], ms_ref[...])
    x1 = x + mod_ref[0, 2:3, :] * _dot(mixed, wo_ref[...])
    x1_ref[...] = x1
    h2 = _modulate(x1, nw_ref[...], mod_ref[0, 3:4, :], mod_ref[0, 4:5, :])
    _to_token_major(h2_ref, h2)

    hh, hl = _split_bf16(h2)
    logits = _dot(hh, wrh_ref[...]) + _dot(hl, wrh_ref[...]) + _dot(hh, wrl_ref[...])
    tm = logits.shape[0]
    lt = logits.T[0:ROUTER_ROWS]
    ridx = lax.broadcasted_iota(I32, (ROUTER_ROWS, tm), 0).astype(F32)

    def first_max(v):
        mx = jnp.max(v, axis=0, keepdims=True)
        idx = jnp.min(jnp.where(v == mx, ridx, float(ROUTER_ROWS)), axis=0, keepdims=True)
        return mx, idx

    gl = jnp.where(ridx < N_GROUPS, lt, NEG_BIG)
    gmax, gidx = first_max(gl)
    g_val = 1.0 / jnp.sum(jnp.exp(gl - gmax), axis=0, keepdims=True)
    lo = N_GROUPS + EXPERTS_PER_GROUP * gidx
    el = jnp.where((ridx >= lo) & (ridx < lo + EXPERTS_PER_GROUP), lt, NEG_BIG)
    emax, l1 = first_max(el)
    esum = jnp.sum(jnp.exp(el - emax), axis=0, keepdims=True)
    e2max, l2 = first_max(jnp.where(ridx == l1, NEG_BIG, el))
    p1 = 1.0 / esum
    p2 = jnp.exp(e2max - emax) / esum
    w1 = g_val * (p1 / (p1 + p2))
    w2 = g_val * (p2 / (p1 + p2))

    @pl.when(i == 0)
    def _():
        carry[...] = jnp.zeros_like(carry)
        t_row = lax.broadcasted_iota(I32, earlier.shape, 0)
        t_col = lax.broadcasted_iota(I32, earlier.shape, 1)
        earlier[...] = jnp.where(t_row < t_col, 1.0, 0.0).astype(BF16)

    sel1 = ridx == l1
    sel2 = ridx == l2
    onehot = jnp.where(sel1 | sel2, 1.0, 0.0)
    before = _dot(onehot.astype(BF16), earlier[...])
    count = jnp.sum(onehot, axis=1, keepdims=True)
    blocks = jnp.floor((count + (MOVE_BLOCK - 1.0)) * (1.0 / MOVE_BLOCK)) * MOVE_BLOCK
    r_row = lax.broadcasted_iota(I32, (ROUTER_ROWS, ROUTER_ROWS), 0)
    r_col = lax.broadcasted_iota(I32, (ROUTER_ROWS, ROUTER_ROWS), 1)
    lower_rows = jnp.where(r_col < r_row, 1.0, 0.0).astype(BF16)
    run_start = _dot(lower_rows,
                     jnp.broadcast_to(blocks, (ROUTER_ROWS, LANES)).astype(BF16))[:, 0:1]
    slot = before + run_start
    q1 = jnp.sum(jnp.where(sel1, slot, 0.0), axis=0, keepdims=True)
    q2 = jnp.sum(jnp.where(sel2, slot, 0.0), axis=0, keepdims=True)
    tab_lane = lax.broadcasted_iota(I32, (ROUTER_ROWS, LANES), 1)
    tab_ref[0] = jnp.where(tab_lane == 0, count,
                           jnp.where(tab_lane == 1, carry[...],
                                     jnp.where(tab_lane == 2, run_start, 0.0)))
    carry[...] = carry[...] + count

    slot_ref[0, 0:1, :] = (q1 * TOKEN_ROWS).astype(I32)
    slot_ref[0, 1:2, :] = (q2 * TOKEN_ROWS).astype(I32)
    wgt_ref[0, 0:1, :] = w1
    wgt_ref[0, 1:2, :] = w2


def _post(x_args, mixed_p, mixed_s, mods, layer, norm_w, w_out_bf16, wr_hi, wr_lo,
          tiles_per_sample):
    split_x = len(x_args) == 2
    tp, ts = mixed_p.shape[0], mixed_s.shape[0]
    t, d = tp + ts, D_MODEL
    npt, nst = tp // ROW_TILE, ts // ROW_TILE
    tile = lambda i: (i, 0)
    if split_x:
        x_specs = [pl.BlockSpec((ROW_TILE, d), lambda i: (jnp.minimum(i, npt - 1), 0)),
                   pl.BlockSpec((ROW_TILE, d), lambda i: (jnp.maximum(i - npt, 0), 0))]
    else:
        x_specs = [pl.BlockSpec((ROW_TILE, d), tile)]
    in_specs = x_specs + [
        pl.BlockSpec((ROW_TILE, d), lambda i: (jnp.minimum(i, npt - 1), 0)),
        pl.BlockSpec((ROW_TILE, d), lambda i: (jnp.maximum(i - npt, 0), 0)),
        pl.BlockSpec((1, 6, d), lambda i: (_mod_row(i, layer, npt, tiles_per_sample), 0, 0)),
        pl.BlockSpec((1, d), lambda i: (0, 0)),
        pl.BlockSpec((d, d), lambda i: (0, 0)),
        pl.BlockSpec((d, LANES), lambda i: (0, 0)),
        pl.BlockSpec((d, LANES), lambda i: (0, 0))]
    return pl.pallas_call(
        functools.partial(_post_kernel, split_x=split_x, n_prompt_tiles=npt),
        grid=(npt + nst,),
        in_specs=in_specs,
        out_specs=(pl.BlockSpec((ROW_TILE, d), tile),
                   pl.BlockSpec((ROW_TILE * TOKEN_ROWS, LANES), tile),
                   pl.BlockSpec((1, 2, ROW_TILE), lambda i: (i, 0, 0)),
                   pl.BlockSpec((1, 2, ROW_TILE), lambda i: (i, 0, 0)),
                   pl.BlockSpec((1, ROUTER_ROWS, LANES), lambda i: (i, 0, 0))),
        out_shape=(jax.ShapeDtypeStruct((t, d), F32),
                   jax.ShapeDtypeStruct((t * TOKEN_ROWS, LANES), F32),
                   jax.ShapeDtypeStruct((npt + nst, 2, ROW_TILE), I32),
                   jax.ShapeDtypeStruct((npt + nst, 2, ROW_TILE), F32),
                   jax.ShapeDtypeStruct((npt + nst, ROUTER_ROWS, LANES), F32)),
        scratch_shapes=[pltpu.VMEM((ROUTER_ROWS, LANES), F32),
                        pltpu.VMEM((ROW_TILE, ROW_TILE), BF16)],
        compiler_params=_params(1),
        name=f"post{layer}",
    )(*x_args, mixed_p, mixed_s, mods, norm_w, w_out_bf16, wr_hi, wr_lo)


def _for_blocks(tab_ref, fn):
    block_rows = MOVE_BLOCK * TOKEN_ROWS
    count = tab_ref[0, 0, MAX_BLOCKS]

    def call(k, parity):
        fn(pl.multiple_of(k * block_rows, block_rows),
           pl.multiple_of(tab_ref[0, 0, k], TOKEN_ROWS), parity)

    def body(k2, c):
        call(2 * k2, 0)

        @pl.when(2 * k2 + 1 < count)
        def _():
            call(2 * k2 + 1, 1)
        return c

    lax.fori_loop(0, _cdiv(count, 2), body, 0)


def _wait_blocks(tab_ref, copy):
    def body(k, c):
        copy.wait()
        return c

    lax.fori_loop(0, tab_ref[0, 0, MAX_BLOCKS], body, 0)


def _dispatch_kernel(zero_ref, tab_ref, prev_tab_ref, q_ref, h2_ref, hs_ref, zero_buf, stage, sem):
    j = pl.program_id(0)
    slot = j % 2
    block_rows = MOVE_BLOCK * TOKEN_ROWS

    @pl.when(j == 0)
    def _():
        zero_buf[...] = jnp.zeros_like(zero_buf)

        def zero_copy(k):
            start = pl.multiple_of(zero_ref[k], EXPERT_TILE * TOKEN_ROWS)
            return pltpu.make_async_copy(
                zero_buf, hs_ref.at[pl.ds(start, EXPERT_TILE * TOKEN_ROWS)], sem.at[0])

        def start_zero(k2, c):
            for parity in range(2):
                @pl.when(zero_ref[2 * k2 + parity] >= 0)
                def _():
                    zero_copy(2 * k2 + parity).start(priority=parity)
            return c

        def wait_zero(k, c):
            @pl.when(zero_ref[k] >= 0)
            def _():
                zero_copy(k).wait()
            return c

        lax.fori_loop(0, zero_ref.shape[0] // 2, start_zero, 0)
        stage[...] = jnp.zeros_like(stage)
        lax.fori_loop(0, zero_ref.shape[0], wait_zero, 0)

    def place(r, c):
        tok = h2_ref[pl.ds(pl.multiple_of(r * TOKEN_ROWS, TOKEN_ROWS), TOKEN_ROWS), :]
        for s in range(2):
            row = pl.multiple_of(q_ref[0, s, r], TOKEN_ROWS)
            stage[slot, pl.ds(row, TOKEN_ROWS), :] = tok
        return c

    lax.fori_loop(0, ROW_TILE, place, 0, unroll=8)

    def block_copy(buf, stage_row, sorted_row):
        return pltpu.make_async_copy(stage.at[buf, pl.ds(stage_row, block_rows)],
                                     hs_ref.at[pl.ds(sorted_row, block_rows)], sem.at[buf])

    @pl.when(j > 0)
    def _():
        _wait_blocks(prev_tab_ref, block_copy(1 - slot, 0, 0))

    _for_blocks(tab_ref, lambda a, b, parity: block_copy(slot, a, b).start(priority=parity))

    @pl.when(j == pl.num_programs(0) - 1)
    def _():
        _wait_blocks(tab_ref, block_copy(slot, 0, 0))


def _dispatch(zero_tiles, block_tab, slots, h2, n_rows):
    t = h2.shape[0] // TOKEN_ROWS
    nt = t // ROW_TILE
    assert zero_tiles.shape[0] % 2 == 0
    smem_tile = lambda shape: pl.BlockSpec((1,) + shape, lambda j, *_: (j, 0, 0),
                                           memory_space=pltpu.SMEM)
    grid_spec = pltpu.PrefetchScalarGridSpec(
        num_scalar_prefetch=1,
        grid=(nt,),
        in_specs=[smem_tile((1, LANES)),
                  pl.BlockSpec((1, 1, LANES), lambda j, *_: (jnp.maximum(j - 1, 0), 0, 0),
                               memory_space=pltpu.SMEM),
                  smem_tile((2, ROW_TILE)),
                  pl.BlockSpec((ROW_TILE * TOKEN_ROWS, LANES), lambda j, *_: (j, 0))],
        out_specs=pl.BlockSpec(memory_space=pl.ANY),
        scratch_shapes=[pltpu.VMEM((EXPERT_TILE * TOKEN_ROWS, LANES), F32),
                        pltpu.VMEM((2, STAGE_TOKENS * TOKEN_ROWS, LANES), F32),
                        pltpu.SemaphoreType.DMA((2,))])
    return pl.pallas_call(
        _dispatch_kernel,
        grid_spec=grid_spec,
        out_shape=jax.ShapeDtypeStruct((n_rows * TOKEN_ROWS, LANES), F32),
        compiler_params=_params(1),
        name="dispatch",
    )(zero_tiles, block_tab, block_tab, slots, h2)


def _expert_kernel(te_ref, src_ref, nv_ref, run_ref, nxt_ref, hs_ref, w1_hbm, w3_hbm, w2_hbm,
                   ys_ref, w1f, w3f, w2f, w1b, w3b, w2b, sem, *, layer):
    i = pl.program_id(0)

    def weight_copies(e, buf):
        return [pltpu.make_async_copy(src.at[layer, e], dst.at[buf], sem.at[buf])
                for src, dst in ((w1_hbm, w1f), (w3_hbm, w3f), (w2_hbm, w2f))]

    @pl.when(i == 0)
    def _():
        for c in weight_copies(te_ref[0], 0):
            c.start()

    first = (i == 0) | (run_ref[i] != run_ref[jnp.maximum(i - 1, 0)])

    @pl.when(first)
    def _():
        buf = run_ref[i] % 2
        for c in weight_copies(te_ref[i], buf):
            c.wait()

        @pl.when(nxt_ref[i] >= 0)
        def _():
            for c in weight_copies(nxt_ref[i], 1 - buf):
                c.start()

        w1b[...] = w1f[buf].astype(BF16)
        w3b[...] = w3f[buf].astype(BF16)
        w2b[...] = w2f[buf].astype(BF16)

    @pl.when(nv_ref[i] > 0)
    def _():
        h = _from_token_major(hs_ref, EXPERT_TILE).astype(BF16)
        g = _silu(_dot(h, w1b[...])) * _dot(h, w3b[...])
        _to_token_major(ys_ref, _dot(g.astype(BF16), w2b[...]))

    @pl.when(nv_ref[i] == 0)
    def _():
        ys_ref[...] = jnp.zeros_like(ys_ref)


def _experts(tile_expert, tile_src, tile_rows, hs, w1, w3, w2, layer):
    n_rows, d = hs.shape[0] // TOKEN_ROWS, D_MODEL
    nt = n_rows // EXPERT_TILE
    hid = w1.shape[-1]
    tok_tile = (EXPERT_TILE * TOKEN_ROWS, LANES)
    changed = jnp.concatenate([jnp.zeros((1,), I32),
                               (tile_expert[1:] != tile_expert[:-1]).astype(I32)])
    run = jnp.cumsum(changed).astype(I32)
    later = jnp.where(run[None, :] > run[:, None], tile_expert[None, :], N_EXPERTS)
    next_expert = jnp.min(later, axis=1)
    next_expert = jnp.where(next_expert < N_EXPERTS, next_expert, -1).astype(I32)
    grid_spec = pltpu.PrefetchScalarGridSpec(
        num_scalar_prefetch=5,
        grid=(nt,),
        in_specs=[pl.BlockSpec(tok_tile, lambda i, te, src, nv, run, nxt: (src[i], 0)),
                  pl.BlockSpec(memory_space=pl.ANY), pl.BlockSpec(memory_space=pl.ANY),
                  pl.BlockSpec(memory_space=pl.ANY)],
        out_specs=pl.BlockSpec(tok_tile, lambda i, te, src, nv, run, nxt: (i, 0)),
        scratch_shapes=[pltpu.VMEM((2, d, hid), F32), pltpu.VMEM((2, d, hid), F32),
                        pltpu.VMEM((2, hid, d), F32),
                        pltpu.VMEM((d, hid), BF16), pltpu.VMEM((d, hid), BF16),
                        pltpu.VMEM((hid, d), BF16), pltpu.SemaphoreType.DMA((2,))])
    return pl.pallas_call(
        functools.partial(_expert_kernel, layer=layer),
        grid_spec=grid_spec,
        out_shape=jax.ShapeDtypeStruct(hs.shape, F32),
        compiler_params=_params(1),
        name=f"experts{layer}",
    )(tile_expert, tile_src, tile_rows, run, next_expert, hs, w1, w3, w2)


def _combine_kernel(tab_ref, next_tab_ref, q_ref, w_ref, x1_ref, mod_ref, fw_ref, ys_ref, out_ref,
                    stage, y_tok, sem, *, final_norm):
    i = pl.program_id(0)
    slot = i % 2
    block_rows = MOVE_BLOCK * TOKEN_ROWS

    def block_copy(buf, stage_row, sorted_row):
        return pltpu.make_async_copy(ys_ref.at[pl.ds(sorted_row, block_rows)],
                                     stage.at[buf, pl.ds(stage_row, block_rows)], sem.at[buf])

    def fetch(tab, buf):
        _for_blocks(tab, lambda a, b, parity: block_copy(buf, a, b).start(priority=parity))

    @pl.when(i == 0)
    def _():
        fetch(tab_ref, slot)

    @pl.when(i + 1 < pl.num_programs(0))
    def _():
        fetch(next_tab_ref, 1 - slot)

    _wait_blocks(tab_ref, block_copy(slot, 0, 0))

    def pick(r, c):
        rows = [stage[slot, pl.ds(pl.multiple_of(q_ref[0, s, r], TOKEN_ROWS), TOKEN_ROWS), :]
                for s in range(2)]
        y_tok[pl.ds(pl.multiple_of(r * TOKEN_ROWS, TOKEN_ROWS), TOKEN_ROWS), :] = (
            w_ref[0, 0, r] * rows[0] + w_ref[0, 1, r] * rows[1])
        return c

    lax.fori_loop(0, ROW_TILE, pick, 0, unroll=8)
    x2 = x1_ref[...] + mod_ref[0, 5:6, :] * _from_token_major(y_tok, ROW_TILE)
    if final_norm:
        x2 = _rms(x2) * fw_ref[...]
    out_ref[...] = x2


def _combine(block_tab, slots, weights, x1, mods, layer, final_w, ys, tile0, n_tiles,
             n_prompt_tiles, tiles_per_sample, final_norm):
    d = D_MODEL
    tile = lambda i: (tile0 + i, 0)
    mod_map = lambda i: (_mod_row(tile0 + i, layer, n_prompt_tiles, tiles_per_sample), 0, 0)
    smem_tile = lambda shape: pl.BlockSpec((1,) + shape, lambda i: (tile0 + i, 0, 0),
                                           memory_space=pltpu.SMEM)
    return pl.pallas_call(
        functools.partial(_combine_kernel, final_norm=final_norm),
        grid=(n_tiles,),
        in_specs=[smem_tile((1, LANES)),
                  pl.BlockSpec((1, 1, LANES),
                               lambda i: (tile0 + jnp.minimum(i + 1, n_tiles - 1), 0, 0),
                               memory_space=pltpu.SMEM),
                  smem_tile((2, ROW_TILE)), smem_tile((2, ROW_TILE)),
                  pl.BlockSpec((ROW_TILE, d), tile),
                  pl.BlockSpec((1, 6, d), mod_map),
                  pl.BlockSpec((1, d), lambda i: (0, 0)),
                  pl.BlockSpec(memory_space=pl.ANY)],
        out_specs=pl.BlockSpec((ROW_TILE, d), lambda i: (i, 0)),
        out_shape=jax.ShapeDtypeStruct((n_tiles * ROW_TILE, d), F32),
        scratch_shapes=[pltpu.VMEM((2, STAGE_TOKENS * TOKEN_ROWS, LANES), F32),
                        pltpu.VMEM((ROW_TILE * TOKEN_ROWS, LANES), F32),
                        pltpu.SemaphoreType.DMA((2,))],
        compiler_params=_params(1),
        name=f"combine{layer}_{tile0}",
    )(block_tab, block_tab, slots, weights, x1, mods, final_w, ys)


def _moe(h2, slots, weights, tile_tab, w1, w3, w2, layer):
    t = h2.shape[0] // TOKEN_ROWS
    n_tiles = t // ROW_TILE
    extra_tiles = N_EXPERTS + _cdiv(N_EXPERTS * MOVE_BLOCK, EXPERT_TILE)
    n_rows = 2 * t + extra_tiles * EXPERT_TILE
    nt = n_rows // EXPERT_TILE
    tab = tile_tab[:, N_GROUPS:N_GROUPS + N_EXPERTS, 0:3].transpose(0, 2, 1).astype(I32)
    cnt = tab[-1, 0] + tab[-1, 1]
    tight = _cdiv(cnt, EXPERT_TILE) * EXPERT_TILE
    padded = jnp.where(cnt > 0, _cdiv(cnt + MOVE_BLOCK - 1, EXPERT_TILE) * EXPERT_TILE, 0)
    ends = jnp.cumsum(padded)
    offsets = ends - padded
    tails = jnp.where(cnt > 0, ends - EXPERT_TILE, -1)
    tails2 = jnp.where(padded > tight, ends - 2 * EXPERT_TILE, -1)
    used = ends[-1] // EXPERT_TILE
    tile_start = jnp.arange(nt, dtype=I32) * EXPERT_TILE
    unused = (used + jnp.arange(extra_tiles, dtype=I32)) * EXPERT_TILE
    zero_tiles = jnp.concatenate([tails, tails2, jnp.where(unused < n_rows, unused, -1)])
    zero_tiles = jnp.where(zero_tiles >= 0, zero_tiles * TOKEN_ROWS, -1).astype(I32)
    tile_src = jnp.minimum(jnp.arange(nt, dtype=I32), used - 1)
    tile_expert = jnp.sum((tile_src * EXPERT_TILE)[:, None] >= ends[None, :], axis=1).astype(I32)
    tile_rows = jnp.where(tile_start < ends[-1],
                          jnp.clip(cnt[tile_expert] - (tile_start - offsets[tile_expert]),
                                   0, EXPERT_TILE), 0).astype(I32)
    n_blocks = _cdiv(tab[:, 0], MOVE_BLOCK)
    blocks_through = jnp.cumsum(n_blocks, axis=1)
    k = jnp.arange(MAX_BLOCKS, dtype=I32)
    owner = jnp.sum(blocks_through[:, None, :] <= k[None, :, None], axis=2)
    is_owner = owner[:, :, None] == jnp.arange(N_EXPERTS, dtype=I32)[None, None, :]
    pick = lambda v: jnp.sum(jnp.where(is_owner, v[:, None, :], 0), axis=2)
    run_first = pick(offsets[None, :] + tab[:, 1])
    block_in_run = k[None, :] - pick(blocks_through - n_blocks)
    sorted_row = (run_first + block_in_run * MOVE_BLOCK) * TOKEN_ROWS
    block_tab = jnp.concatenate(
        [sorted_row, blocks_through[:, -1:],
         jnp.zeros((n_tiles, LANES - MAX_BLOCKS - 1), I32)], axis=1).astype(I32)[:, None, :]
    hs = _dispatch(zero_tiles, block_tab, slots, h2, n_rows)
    ys = _experts(tile_expert, tile_src, tile_rows, hs, w1, w3, w2, layer)
    return ys, (block_tab, slots, weights)


def _rope(x, cos, sin_signed):
    lane = lax.broadcasted_iota(I32, (x.shape[0], LANES), 1)
    low = (lane % 32) < 16
    outs = []
    for j in range(x.shape[1] // LANES):
        xb = x[:, j * LANES:(j + 1) * LANES]
        partner = jnp.where(low, pltpu.roll(xb, LANES - 16, 1), pltpu.roll(xb, 16, 1))
        outs.append(xb * cos + partner * sin_signed)
    return jnp.concatenate(outs, axis=1)


def _inproj1_prompt_kernel(x_ref, mod_ref, nw_ref, w_ref, q_ref, k_ref, v_ref, kc_ref, vc_ref):
    d = D_MODEL
    h = _modulate(x_ref[...], nw_ref[...], mod_ref[0, 0:1, :], mod_ref[0, 1:2, :]).astype(BF16)
    q_ref[...] = (_dot(h, w_ref[:, 0:d]) * (DIFF_HD ** -0.5)).astype(BF16)
    k = _dot(h, w_ref[:, d:2 * d])
    v = _dot(h, w_ref[:, 2 * d:3 * d])
    k_ref[...] = k.astype(BF16)
    v_ref[...] = v.astype(BF16)
    for b in range(k.shape[0] // ATTN_TILE):
        kc_ref[b * d:(b + 1) * d, :] = k[b * ATTN_TILE:(b + 1) * ATTN_TILE].T
    _to_token_major(vc_ref, v)


def _inproj1_sample_kernel(x_ref, mod_ref, nw_ref, w_ref, cos_ref, sin_ref, q_ref, k_ref, v_ref):
    d = D_MODEL
    h = _modulate(x_ref[...], nw_ref[...], mod_ref[0, 0:1, :], mod_ref[0, 1:2, :]).astype(BF16)
    cos, sin = cos_ref[...], sin_ref[...]
    q_ref[...] = (_rope(_dot(h, w_ref[:, 0:d]), cos, sin) * (DIFF_HD ** -0.5)).astype(BF16)
    k_ref[...] = _rope(_dot(h, w_ref[:, d:2 * d]), cos, sin).astype(BF16)
    v_ref[...] = _dot(h, w_ref[:, 2 * d:3 * d]).astype(BF16)


def _inproj1(x, mods, norm_w, w_bf16, n_prompt_tiles, n_sample_tiles, tiles_per_sample,
             cos_t, sin_t):
    d = D_MODEL
    rows = INPROJ1_TILE
    npt, nst = n_prompt_tiles, n_sample_tiles
    common = [pl.BlockSpec((1, d), lambda i: (0, 0)), pl.BlockSpec((d, 3 * d), lambda i: (0, 0))]
    tile = lambda i: (i, 0)
    out_specs = tuple(pl.BlockSpec((rows, d), tile) for _ in range(3))
    qp, kp, vp, k_cache, v_cache = pl.pallas_call(
        _inproj1_prompt_kernel,
        grid=(npt,),
        in_specs=[pl.BlockSpec((rows, d), tile),
                  pl.BlockSpec((1, 6, d), lambda i: (8, 0, 0))] + common,
        out_specs=out_specs + (pl.BlockSpec((rows // ATTN_TILE * d, ATTN_TILE), tile),
                               pl.BlockSpec((rows * TOKEN_ROWS, LANES), tile)),
        out_shape=tuple(jax.ShapeDtypeStruct((npt * rows, d), BF16) for _ in range(3))
        + (jax.ShapeDtypeStruct((npt * rows // ATTN_TILE * d, ATTN_TILE), F32),
           jax.ShapeDtypeStruct((npt * rows * TOKEN_ROWS, LANES), F32)),
        compiler_params=_params(1),
        name="inproj1_prompt",
    )(x, mods, norm_w, w_bf16)
    rope_tile = lambda i: (i % tiles_per_sample, 0)
    qs, ks, vs = pl.pallas_call(
        _inproj1_sample_kernel,
        grid=(nst,),
        in_specs=[pl.BlockSpec((rows, d), lambda i: (npt + i, 0)),
                  pl.BlockSpec((1, 6, d), lambda i: (8 + 1 + i // tiles_per_sample, 0, 0))]
        + common + [pl.BlockSpec((rows, LANES), rope_tile),
                    pl.BlockSpec((rows, LANES), rope_tile)],
        out_specs=out_specs,
        out_shape=tuple(jax.ShapeDtypeStruct((nst * rows, d), BF16) for _ in range(3)),
        compiler_params=_params(1),
        name="inproj1_sample",
    )(x, mods, norm_w, w_bf16, cos_t, sin_t)
    return (qp, kp, vp), (qs, ks, vs), (k_cache, v_cache)


def _rope_tables(n_tok):
    half = DIFF_HD // 4
    pos = np.arange(n_tok)
    lane = np.arange(LANES)
    sub = lane % DIFF_HD
    p = np.where(sub[None, :] < DIFF_HD // 2, (pos // GRID_W)[:, None], (pos % GRID_W)[:, None])
    inv = jnp.asarray(ROPE_THETA, F32) ** (-jnp.asarray(sub % half, F32) / half)
    ang = jnp.asarray(p, F32) * inv[None, :]
    sign = np.where((lane % (2 * half)) < half, -1.0, 1.0).astype(np.float32)
    return jnp.cos(ang), jnp.sin(ang) * sign[None, :]


def _diffattn_kernel(*refs, has_cache, lam_init):
    if has_cache:
        q_ref, k_ref, v_ref, ck_ref, cv_ref, lam_ref, sw_ref, o_ref = refs
    else:
        q_ref, k_ref, v_ref, lam_ref, sw_ref, o_ref = refs
    hd2 = 2 * DIFF_HD
    lv = lam_ref[...]
    lam = (jnp.exp(jnp.sum(lv[0:1] * lv[1:2], axis=1, keepdims=True))
           - jnp.exp(jnp.sum(lv[2:3] * lv[3:4], axis=1, keepdims=True)) + lam_init)
    lane = lax.broadcasted_iota(I32, (q_ref.shape[0], hd2), 1)
    for h in range(DIFF_HEADS):
        cols = slice(h * hd2, (h + 1) * hd2)
        q = q_ref[:, cols]
        zero = jnp.zeros_like(q)
        k_new = k_ref[:, cols].astype(BF16)
        values = [v_ref[:, cols].astype(BF16)]
        if has_cache:
            past = ck_ref.shape[1]
            k_past_t = ck_ref[cols, :].astype(BF16)
            values.append(cv_ref[pl.ds(h, past, stride=DIFF_HEADS), :].astype(BF16))
        o = None
        for c in range(2):
            qc = jnp.where((lane < DIFF_HD) == (c == 0), q, zero)
            s = [_dot_nt(qc, k_new)]
            if has_cache:
                s.append(_dot(qc, k_past_t))
            mx = functools.reduce(jnp.maximum, [jnp.max(si, axis=1, keepdims=True) for si in s])
            e = [jnp.exp(si - mx) for si in s]
            z = functools.reduce(jnp.add, [jnp.sum(ei, axis=1, keepdims=True) for ei in e])
            pv = functools.reduce(jnp.add, [_dot(ei.astype(BF16), v) for ei, v in zip(e, values)])
            pv = pv * (1.0 / z)
            o = pv if c == 0 else o - lam * pv
        o_ref[:, cols] = ((_rms(o) * sw_ref[...]) * (1.0 - lam_init)).astype(BF16)


def _diffattn(q, k, v, lam_vecs, subln_w, batch, seq_len, q_block, lam_init, cache=None):
    d = D_MODEL
    nq = seq_len // q_block
    has_cache = cache is not None
    kv_spec = pl.BlockSpec((seq_len, d), lambda b, qi: (b, 0))
    in_specs = [pl.BlockSpec((q_block, d), lambda b, qi: (b * nq + qi, 0)), kv_spec, kv_spec]
    args = [q, k, v]
    if has_cache:
        past = cache[0].shape[1]
        in_specs += [pl.BlockSpec((d, past), lambda b, qi: (b, 0)),
                     pl.BlockSpec((past * DIFF_HEADS, 2 * DIFF_HD), lambda b, qi: (b, 0))]
        args += list(cache)
    in_specs += [pl.BlockSpec((4, DIFF_HD), lambda b, qi: (0, 0)),
                 pl.BlockSpec((1, 2 * DIFF_HD), lambda b, qi: (0, 0))]
    args += [lam_vecs, subln_w]
    return pl.pallas_call(
        functools.partial(_diffattn_kernel, has_cache=has_cache, lam_init=lam_init),
        grid=(batch, nq),
        in_specs=in_specs,
        out_specs=pl.BlockSpec((q_block, d), lambda b, qi: (b * nq + qi, 0)),
        out_shape=jax.ShapeDtypeStruct((batch * seq_len, d), BF16),
        compiler_params=_params(2),
        name="diffattn_cache" if has_cache else "diffattn",
    )(*args)


def _router_weights(router_group, router_expert):
    w = jnp.concatenate([router_group, router_expert], axis=1)
    w = jnp.pad(w, ((0, 0), (0, LANES - w.shape[1])))
    hi = w.astype(BF16)
    return hi, (w - hi.astype(F32)).astype(BF16)


def _inproj0_weights(w_in):
    gq, gk, gv, gg, gaf, gab, hq, hff, hfb, hi, hg = jnp.split(
        w_in, [256, 512, 1024, 1536, 1552, 1568, 1824, 2080, 2336, 2848], axis=1)
    w = jnp.concatenate([gq, gk, gv, gg, hq, hff, hfb, hi, hg, gaf, gab], axis=1)
    return jnp.pad(w, ((0, 0), (0, AB_COLS - w.shape[1]))).astype(BF16)


def kernel(x_prompt, x_sample, state_gla, state_hgrn, cache_diff_k, cache_diff_v, c, c_ctx,
           w_ada, b_ada, norm1_w, norm2_w, w_in_ab, gla_a2, gla_a_bias, hgrn_lb, gla_onorm_w,
           hgrn_onorm_w, w_out_ab, w_in_c, lam_q1, lam_k1, lam_q2, lam_k2, diff_subln_w, w_out_c,
           router_group, router_expert, moe_w1, moe_w3, moe_w2, final_norm_w):
    bp, lp, d = x_prompt.shape
    bs, ls, _ = x_sample.shape
    depth = w_ada.shape[0]
    assert depth == 2 and d == D_MODEL and bs <= 7
    tp, ts = bp * lp, bs * ls
    npt, nst = tp // ROW_TILE, ts // ROW_TILE
    tps = ls // ROW_TILE
    xp = x_prompt.reshape(tp, d)
    xs = x_sample.reshape(ts, d)

    cond8 = jnp.concatenate([c_ctx[None, :], c, jnp.zeros((7 - bs, d), F32)], axis=0)
    mods = _adaln(cond8, w_ada, b_ada).reshape(depth * 8, 6, d)

    proj = _inproj0(xp, xs, mods, norm1_w[0:1], _inproj0_weights(w_in_ab[0]), ls)
    a_bias = gla_a_bias[0][:, None, :]
    scan_args = (gla_a2[0], a_bias, hgrn_lb, gla_onorm_w[0:1], hgrn_onorm_w[0:1])
    mixed_p, new_state_gla, new_state_hgrn = _scan(proj, 0, bp, lp, *scan_args)
    s0 = jnp.concatenate([state_gla[:, 0], state_hgrn[:, 0]], axis=2).swapaxes(-1, -2)
    s0 = s0.reshape(bs, 2, SCAN_PAIRS, 2, HEAD_DV, HEAD_DK)
    zero = jnp.zeros_like(s0[:, :, :, 0])
    s0 = jnp.concatenate([jnp.concatenate([s0[:, :, :, 0], zero], axis=-1),
                          jnp.concatenate([zero, s0[:, :, :, 1]], axis=-1)], axis=-2)
    mixed_s = _scan(proj, tp, bs, ls, *scan_args, s0=s0)

    wr = _router_weights(router_group[0], router_expert[0])
    x1, *routed = _post((xp, xs), mixed_p, mixed_s, mods, 0, norm2_w[0:1],
                        w_out_ab[0].astype(BF16), *wr, tps)
    ys, tables = _moe(*routed, moe_w1, moe_w3, moe_w2, 0)
    x2 = _combine(*tables, x1, mods, 0, final_norm_w[None, :], ys, 0, npt + nst, npt, tps, False)

    lam_init = 0.8 - 0.6 * math.exp(-0.3 * 1)
    cos_t, sin_t = _rope_tables(ls)
    (qp, kp, vp), (qs, ks, vs), (k_cache, v_cache) = _inproj1(
        x2, mods, norm1_w[1:2], w_in_c[0].astype(BF16), tp // INPROJ1_TILE, ts // INPROJ1_TILE,
        ls // INPROJ1_TILE, cos_t, sin_t)
    lam_vecs = jnp.stack([lam_q1[0], lam_k1[0], lam_q2[0], lam_k2[0]])
    att_p = _diffattn(qp, kp, vp, lam_vecs, diff_subln_w[0:1], bp, lp, lp, lam_init)
    past = cache_diff_k.shape[2]
    assert lp == ATTN_TILE and DIFF_HEADS == TOKEN_ROWS
    cache = (cache_diff_k[:, 0].transpose(0, 2, 3, 4, 1).reshape(bs * d, past),
             cache_diff_v[:, 0].reshape(bs * past * DIFF_HEADS, 2 * DIFF_HD))
    att_s = _diffattn(qs, ks, vs, lam_vecs, diff_subln_w[0:1], bs, ls, SAMPLE_Q_BLOCK, lam_init,
                      cache)

    wr = _router_weights(router_group[1], router_expert[1])
    x3, *routed = _post((x2,), att_p, att_s, mods, 1, norm2_w[1:2],
                        w_out_c[0].astype(BF16), *wr, tps)
    ys, tables = _moe(*routed, moe_w1, moe_w3, moe_w2, 1)
    fw = final_norm_w[None, :]
    y_p = _combine(*tables, x3, mods, 1, fw, ys, 0, npt, npt, tps, True)
    y_s = _combine(*tables, x3, mods, 1, fw, ys, npt, nst, npt, tps, True)

    return (y_p.reshape(bp, lp, d), y_s.reshape(bs, ls, d), new_state_gla, new_state_hgrn,
            k_cache.reshape(bp, 1, DIFF_HEADS, 2, DIFF_HD, lp).transpose(0, 1, 5, 2, 3, 4),
            v_cache.reshape(bp, 1, lp, DIFF_HEADS, 2 * DIFF_HD))
```

```python
import functools
import math

import jax
import jax.numpy as jnp
import numpy as np
from jax import lax
from jax.experimental import pallas as pl
from jax.experimental.pallas import tpu as pltpu

F32 = jnp.float32
BF16 = jnp.bfloat16
I32 = jnp.int32

D_MODEL = 1024
GLA_HEADS = 4
HGRN_HEADS = 4
SCAN_HEADS = GLA_HEADS + HGRN_HEADS
SCAN_PAIRS = SCAN_HEADS // 2
HEAD_DK = 64
HEAD_DV = 128
GATE_RANK = 16
GLA_GATE_NORM = 16.0
DIFF_HEADS = 8
DIFF_HD = 64
GRID_W = 64
ROPE_THETA = 10000.0
N_GROUPS = 4
EXPERTS_PER_GROUP = 8
N_EXPERTS = N_GROUPS * EXPERTS_PER_GROUP
MOE_HIDDEN = 512
EPS = 1e-6
LANES = 128
TOKEN_ROWS = D_MODEL // LANES
NEG_BIG = -1e30
ROUTER_ROWS = 48

ROW_TILE = 512
ATTN_TILE = 256
SAMPLE_Q_BLOCK = 512
ADA_TILE = 1536
INPROJ0_TILE = 512
INPROJ1_TILE = 512
SCAN_CHUNK = 64
EXPERT_TILE = 256
MOVE_BLOCK = 16
STAGE_TOKENS = 2 * ROW_TILE + N_EXPERTS * MOVE_BLOCK
MAX_BLOCKS = STAGE_TOKENS // MOVE_BLOCK
VMEM_LIMIT = 56 * 1024 * 1024
SCAN_INPUT_DOUBLE_BUFFER_BYTES = 16 * 1024 * 1024

_C_GQ, _C_GK, _C_GV, _C_GG = 0, 256, 512, 1024
_C_HQ, _C_HFF, _C_HFB, _C_HI, _C_HG = 1536, 1792, 2048, 2304, 2816
_C_GAF, _C_GAB = 3328, 3344
AB_COLS = 3456


def _params(n_axes, vmem=VMEM_LIMIT):
    return pltpu.CompilerParams(dimension_semantics=("arbitrary",) * n_axes,
                                vmem_limit_bytes=vmem)


def _cdiv(a, b):
    return (a + b - 1) // b


def _dot(a, b):
    return jnp.dot(a, b, preferred_element_type=F32)


def _dot_nt(a, b):
    return lax.dot_general(a, b, (((1,), (1,)), ((), ())), preferred_element_type=F32)


def _dot_tn(a, b):
    return lax.dot_general(a, b, (((0,), (0,)), ((), ())), preferred_element_type=F32)


def _split_bf16(x):
    hi = x.astype(BF16)
    lo = (x - hi.astype(F32)).astype(BF16)
    return hi, lo


def _silu(x):
    return x * jax.nn.sigmoid(x)


def _log_sigmoid(x):
    return jnp.minimum(x, 0.0) - jnp.log(1.0 + jnp.exp(-jnp.abs(x)))


def _rms(x):
    return x * lax.rsqrt(jnp.mean(x * x, axis=-1, keepdims=True) + EPS)


def _modulate(x, norm_w, shift, scale):
    return (_rms(x) * norm_w) * (1.0 + scale) + shift


def _to_token_major(dst_ref, x, row0=0):
    n = x.shape[0]
    for s in range(TOKEN_ROWS):
        dst_ref[pl.ds(row0 + s, n, stride=TOKEN_ROWS), :] = x[:, s * LANES:(s + 1) * LANES]


def _from_token_major(src_ref, n, row0=0):
    return jnp.concatenate([src_ref[pl.ds(row0 + s, n, stride=TOKEN_ROWS), :]
                            for s in range(TOKEN_ROWS)], axis=1)


def _ada_kernel(c_ref, w_ref, b_ref, o_ref):
    s = _silu(c_ref[...])
    o_ref[0] = _dot(s.astype(BF16), w_ref[0].astype(BF16)) + b_ref[0]


def _adaln(cond8, w_ada, b_ada):
    depth, d, n = w_ada.shape
    tn = ADA_TILE
    return pl.pallas_call(
        _ada_kernel,
        grid=(depth, n // tn),
        in_specs=[pl.BlockSpec((8, d), lambda l, j: (0, 0)),
                  pl.BlockSpec((1, d, tn), lambda l, j: (l, 0, j)),
                  pl.BlockSpec((1, 1, tn), lambda l, j: (l, 0, j))],
        out_specs=pl.BlockSpec((1, 8, tn), lambda l, j: (l, 0, j)),
        out_shape=jax.ShapeDtypeStruct((depth, 8, n), F32),
        compiler_params=_params(2),
        name="adaln",
    )(cond8, w_ada, b_ada.reshape(depth, 1, n))


def _mod_row(i, layer, n_prompt_tiles, tiles_per_sample):
    r = jnp.where(i < n_prompt_tiles, 0, 1 + (i - n_prompt_tiles) // tiles_per_sample)
    return layer * 8 + r


def _inproj0_kernel(xp_ref, xs_ref, mod_ref, nw_ref, w_ref, o_ref, *, n_prompt_tiles):
    i = pl.program_id(0)
    x = jnp.where(i < n_prompt_tiles, xp_ref[...], xs_ref[...])
    h = _modulate(x, nw_ref[...], mod_ref[0, 0:1, :], mod_ref[0, 1:2, :])
    o_ref[...] = _dot(h.astype(BF16), w_ref[...])


def _inproj0(xp, xs, mods, norm_w, w_bf16, sample_len):
    tp, d = xp.shape
    ts = xs.shape[0]
    n = w_bf16.shape[1]
    tile = INPROJ0_TILE
    npt, nst = tp // tile, ts // tile
    mod_map = lambda i: (_mod_row(i, 0, npt, sample_len // tile), 0, 0)
    return pl.pallas_call(
        functools.partial(_inproj0_kernel, n_prompt_tiles=npt),
        grid=(npt + nst,),
        in_specs=[pl.BlockSpec((tile, d), lambda i: (jnp.minimum(i, npt - 1), 0)),
                  pl.BlockSpec((tile, d), lambda i: (jnp.maximum(i - npt, 0), 0)),
                  pl.BlockSpec((1, 6, d), mod_map),
                  pl.BlockSpec((1, d), lambda i: (0, 0)),
                  pl.BlockSpec((d, n), lambda i: (0, 0))],
        out_specs=pl.BlockSpec((tile, n), lambda i: (i, 0)),
        out_shape=jax.ShapeDtypeStruct((tp + ts, n), F32),
        compiler_params=_params(1),
        name="inproj0",
    )(xp, xs, mods, norm_w, w_bf16)


def _scan_kernel(*refs, seq_len, has_state):
    if has_state:
        (p_ref, a2_ref, ab_ref, lb_ref, ong_ref, onh_ref, s0_ref, mixed_ref, *scratch) = refs
        sfin_ref = None
    else:
        (p_ref, a2_ref, ab_ref, lb_ref, ong_ref, onh_ref, mixed_ref, sg_ref, sh_ref,
         *scratch) = refs
        s0_ref = None
        sfin_ref = (sg_ref, sh_ref)
    (qi_f, ki_f, qo_f, ko_f, qi_b, ki_b, qo_b, ko_b,
     vv, dec_f, dec_b, o_f, o_b, st_f, st_b) = scratch
    C = SCAN_CHUNK
    n_chunks = seq_len // C
    gqk = GLA_HEADS * HEAD_DK

    row = lax.broadcasted_iota(I32, (C, C), 0)
    col = lax.broadcasted_iota(I32, (C, C), 1)
    lower = col <= row
    upper = col >= row
    tri_lo = jnp.where(lower, 1.0, 0.0).astype(BF16)
    tri_up = jnp.where(upper, 1.0, 0.0).astype(BF16)

    lbp = lb_ref[...]
    lb_max = jnp.maximum(lbp[0], lbp[1])
    lb_e0 = jnp.exp(lbp[0] - lb_max)
    lb_e1 = jnp.exp(lbp[1] - lb_max)
    lb = lb_e0 / (lb_e0 + lb_e1)

    def cumsum_chunk(tri, la):
        hi, lo = _split_bf16(la)
        return _dot(tri, hi) + _dot(tri, lo)

    def prep(n, carry):
        r0 = pl.multiple_of(n * C, C)
        rows = pl.ds(r0, C)
        gq = p_ref[rows, _C_GQ:_C_GQ + gqk] * (HEAD_DK ** -0.5)
        gk = p_ref[rows, _C_GK:_C_GK + gqk]
        hq = _silu(p_ref[rows, _C_HQ:_C_HQ + gqk]) * (HEAD_DK ** -0.5)
        for d_i, (qi_s, ki_s, qo_s, ko_s, dec_s, tri, last, mid) in enumerate(
                ((qi_f, ki_f, qo_f, ko_f, dec_f, tri_lo, C - 1, C // 2 - 1),
                 (qi_b, ki_b, qo_b, ko_b, dec_b, tri_up, 0, C // 2))):
            c_ga = _C_GAF if d_i == 0 else _C_GAB
            c_hf = _C_HFF if d_i == 0 else _C_HFB
            ga = p_ref[rows, c_ga:c_ga + GATE_RANK]
            xg = _dot(ga.astype(BF16), a2_ref[d_i].astype(BF16)) + ab_ref[d_i]
            la_g = _log_sigmoid(xg) / GLA_GATE_NORM
            f = lb[d_i:d_i + 1, :] + (1.0 - lb[d_i:d_i + 1, :]) * jax.nn.sigmoid(
                p_ref[rows, c_hf:c_hf + gqk])
            la_h = jnp.log(f)
            for q, k, la, c0 in ((gq, gk, la_g, 0), (hq, 1.0 - f, la_h, gqk)):
                b = cumsum_chunk(tri, la)
                b_mid, b_end = b[mid:mid + 1, :], b[last:last + 1, :]
                cs = slice(c0, c0 + gqk)
                qi_s[rows, cs] = (q * jnp.exp(b - b_mid)).astype(BF16)
                ki_s[rows, cs] = (k * jnp.exp(b_mid - b)).astype(BF16)
                qo_s[rows, cs] = (q * jnp.exp(b)).astype(BF16)
                ko_s[rows, cs] = (k * jnp.exp(b_end - b)).astype(BF16)
                dec_s[n, :, cs] = jnp.exp(b_end)
        gv_cols = GLA_HEADS * HEAD_DV
        vv[rows, 0:gv_cols] = p_ref[rows, _C_GV:_C_GV + gv_cols].astype(BF16)
        vv[rows, gv_cols:] = p_ref[rows, _C_HI:_C_HI + HGRN_HEADS * HEAD_DV].astype(BF16)
        return carry

    lax.fori_loop(0, n_chunks, prep, 0, unroll=8)

    for p in range(SCAN_PAIRS):
        if has_state:
            st_f[p] = s0_ref[0, 0, p]
            st_b[p] = s0_ref[0, 1, p]
        else:
            st_f[p] = jnp.zeros((2 * HEAD_DV, 2 * HEAD_DK), F32)
            st_b[p] = jnp.zeros((2 * HEAD_DV, 2 * HEAD_DK), F32)

    first_head = lax.broadcasted_iota(I32, (C, 2 * HEAD_DK), 1) < HEAD_DK
    row2 = lax.broadcasted_iota(I32, (2 * C, C), 0) % C
    col2 = lax.broadcasted_iota(I32, (2 * C, C), 1)
    lower2 = col2 <= row2
    upper2 = col2 >= row2

    def per_head_rows(x):
        z = jnp.zeros_like(x)
        return jnp.concatenate([jnp.where(first_head, x, z), jnp.where(first_head, z, x)], axis=0)

    def put_out(o_ref, rows, p, res):
        c0 = p * 2 * HEAD_DV
        o_ref[rows, c0:c0 + HEAD_DV] = res[0:C, 0:HEAD_DV]
        o_ref[rows, c0 + HEAD_DV:c0 + 2 * HEAD_DV] = res[C:2 * C, HEAD_DV:2 * HEAD_DV]

    def sweep(n, carry):
        m = n_chunks - 1 - n
        rows = pl.ds(pl.multiple_of(n * C, C), C)
        rows_m = pl.ds(pl.multiple_of(m * C, C), C)
        decay_f, decay_b = dec_f[n], dec_b[m]
        for p in range(SCAN_PAIRS):
            ks = slice(p * 2 * HEAD_DK, (p + 1) * 2 * HEAD_DK)
            vs = slice(p * 2 * HEAD_DV, (p + 1) * 2 * HEAD_DV)
            vh = vv[rows, vs]
            s_f = st_f[p]
            sc = (jnp.where(lower2, _dot_nt(per_head_rows(qi_f[rows, ks]), ki_f[rows, ks]), 0.0)
                  + jnp.where(upper2, _dot_nt(per_head_rows(qi_b[rows, ks]), ki_b[rows, ks]), 0.0))
            put_out(o_f, rows, p, _dot_nt(per_head_rows(qo_f[rows, ks]), s_f.astype(BF16))
                    + _dot(sc.astype(BF16), vh))
            st_f[p] = decay_f[:, ks] * s_f + _dot_tn(vh, ko_f[rows, ks])
            s_b = st_b[p]
            put_out(o_b, rows_m, p, _dot_nt(per_head_rows(qo_b[rows_m, ks]), s_b.astype(BF16)))
            st_b[p] = decay_b[:, ks] * s_b + _dot_tn(vv[rows_m, vs], ko_b[rows_m, ks])
        return carry

    lax.fori_loop(0, n_chunks, sweep, 0, unroll=8)

    def finish(n, carry):
        rows = pl.ds(pl.multiple_of(n * C, C), C)
        for h in range(SCAN_HEADS):
            vs = slice(h * HEAD_DV, (h + 1) * HEAD_DV)
            if h < GLA_HEADS:
                gate = p_ref[rows, _C_GG + h * HEAD_DV:_C_GG + (h + 1) * HEAD_DV]
                onw = ong_ref[...]
            else:
                hh = h - GLA_HEADS
                gate = p_ref[rows, _C_HG + hh * HEAD_DV:_C_HG + (hh + 1) * HEAD_DV]
                onw = onh_ref[...]
            o = o_f[rows, vs] + o_b[rows, vs]
            mixed_ref[rows, vs] = ((_rms(o) * onw) * _silu(gate)).astype(BF16)
        return carry

    lax.fori_loop(0, n_chunks, finish, 0, unroll=4)

    if sfin_ref is not None:
        for d_i, st in enumerate((st_f, st_b)):
            for p in range(SCAN_PAIRS):
                s_pair = st[p].T
                out_ref = sfin_ref[(2 * p) // GLA_HEADS]
                h0 = (2 * p) % GLA_HEADS
                out_ref[0, 0, d_i, h0] = s_pair[0:HEAD_DK, 0:HEAD_DV]
                out_ref[0, 0, d_i, h0 + 1] = s_pair[HEAD_DK:2 * HEAD_DK, HEAD_DV:2 * HEAD_DV]


def _scan(p, row0, batch, seq_len, a2, a_bias, lb, onorm_g, onorm_h, s0=None):
    n = p.shape[1]
    assert row0 % seq_len == 0
    blk0 = row0 // seq_len
    has_state = s0 is not None
    n_chunks = seq_len // SCAN_CHUNK
    assert GLA_HEADS == HGRN_HEADS and GLA_HEADS % 2 == 0
    st_shape = (1, 1, 2, GLA_HEADS, HEAD_DK, HEAD_DV)
    pair_shape = (SCAN_PAIRS, 2 * HEAD_DV, 2 * HEAD_DK)
    p_mode = dict(pipeline_mode=pl.Buffered(1)) if seq_len * n * 4 > SCAN_INPUT_DOUBLE_BUFFER_BYTES else {}
    in_specs = [pl.BlockSpec((seq_len, n), lambda b: (blk0 + b, 0), **p_mode),
                pl.BlockSpec(a2.shape, lambda b: (0, 0, 0)),
                pl.BlockSpec(a_bias.shape, lambda b: (0, 0, 0)),
                pl.BlockSpec(lb.shape, lambda b: (0, 0, 0)),
                pl.BlockSpec((1, HEAD_DV), lambda b: (0, 0)),
                pl.BlockSpec((1, HEAD_DV), lambda b: (0, 0))]
    args = [p, a2, a_bias, lb, onorm_g, onorm_h]
    mixed_shape = jax.ShapeDtypeStruct((batch * seq_len, D_MODEL), BF16)
    mixed_spec = pl.BlockSpec((seq_len, D_MODEL), lambda b: (b, 0))
    if has_state:
        in_specs.append(pl.BlockSpec((1, 2) + pair_shape, lambda b: (b, 0, 0, 0, 0)))
        args.append(s0)
        out_shape, out_specs = mixed_shape, mixed_spec
    else:
        st_struct = jax.ShapeDtypeStruct((batch,) + st_shape[1:], F32)
        st_spec = pl.BlockSpec(st_shape, lambda b: (b, 0, 0, 0, 0, 0))
        out_shape = (mixed_shape, st_struct, st_struct)
        out_specs = (mixed_spec, st_spec, st_spec)
    qk_cols = SCAN_HEADS * HEAD_DK
    scratch = [pltpu.VMEM((seq_len, qk_cols), BF16) for _ in range(8)]
    scratch += [pltpu.VMEM((seq_len, D_MODEL), BF16),
                pltpu.VMEM((n_chunks, 1, qk_cols), F32), pltpu.VMEM((n_chunks, 1, qk_cols), F32),
                pltpu.VMEM((seq_len, D_MODEL), F32), pltpu.VMEM((seq_len, D_MODEL), F32),
                pltpu.VMEM(pair_shape, F32), pltpu.VMEM(pair_shape, F32)]
    return pl.pallas_call(
        functools.partial(_scan_kernel, seq_len=seq_len, has_state=has_state),
        grid=(batch,),
        in_specs=in_specs, out_specs=out_specs, out_shape=out_shape,
        scratch_shapes=scratch,
        compiler_params=_params(1),
        name="scan_state" if has_state else "scan_fresh",
    )(*args)


def _post_kernel(*refs, split_x, n_prompt_tiles):
    if split_x:
        xp_ref, xs_ref = refs[0], refs[1]
        refs = refs[2:]
    else:
        x_ref = refs[0]
        refs = refs[1:]
    (mp_ref, ms_ref, mod_ref, nw_ref, wo_ref, wrh_ref, wrl_ref,
     x1_ref, h2_ref, slot_ref, wgt_ref, tab_ref, carry, earlier) = refs
    i = pl.program_id(0)
    is_prompt = i < n_prompt_tiles
    if split_x:
        x = jnp.where(is_prompt, xp_ref[...], xs_ref[...])
    else:
        x = x_ref[...]
    mixed = jnp.where(is_prompt, mp_ref[...], ms_ref[...])
    x1 = x + mod_ref[0, 2:3, :] * _dot(mixed, wo_ref[...])
    x1_ref[...] = x1
    h2 = _modulate(x1, nw_ref[...], mod_ref[0, 3:4, :], mod_ref[0, 4:5, :])
    _to_token_major(h2_ref, h2)

    hh, hl = _split_bf16(h2)
    logits = _dot(hh, wrh_ref[...]) + _dot(hl, wrh_ref[...]) + _dot(hh, wrl_ref[...])
    tm = logits.shape[0]
    lt = logits.T[0:ROUTER_ROWS]
    ridx = lax.broadcasted_iota(I32, (ROUTER_ROWS, tm), 0).astype(F32)

    def first_max(v):
        mx = jnp.max(v, axis=0, keepdims=True)
        idx = jnp.min(jnp.where(v == mx, ridx, float(ROUTER_ROWS)), axis=0, keepdims=True)
        return mx, idx

    gl = jnp.where(ridx < N_GROUPS, lt, NEG_BIG)
    gmax, gidx = first_max(gl)
    g_val = 1.0 / jnp.sum(jnp.exp(gl - gmax), axis=0, keepdims=True)
    lo = N_GROUPS + EXPERTS_PER_GROUP * gidx
    el = jnp.where((ridx >= lo) & (ridx < lo + EXPERTS_PER_GROUP), lt, NEG_BIG)
    emax, l1 = first_max(el)
    esum = jnp.sum(jnp.exp(el - emax), axis=0, keepdims=True)
    e2max, l2 = first_max(jnp.where(ridx == l1, NEG_BIG, el))
    p1 = 1.0 / esum
    p2 = jnp.exp(e2max - emax) / esum
    w1 = g_val * (p1 / (p1 + p2))
    w2 = g_val * (p2 / (p1 + p2))

    @pl.when(i == 0)
    def _():
        carry[...] = jnp.zeros_like(carry)
        t_row = lax.broadcasted_iota(I32, earlier.shape, 0)
        t_col = lax.broadcasted_iota(I32, earlier.shape, 1)
        earlier[...] = jnp.where(t_row < t_col, 1.0, 0.0).astype(BF16)

    sel1 = ridx == l1
    sel2 = ridx == l2
    onehot = jnp.where(sel1 | sel2, 1.0, 0.0)
    before = _dot(onehot.astype(BF16), earlier[...])
    count = jnp.sum(onehot, axis=1, keepdims=True)
    blocks = jnp.floor((count + (MOVE_BLOCK - 1.0)) * (1.0 / MOVE_BLOCK)) * MOVE_BLOCK
    r_row = lax.broadcasted_iota(I32, (ROUTER_ROWS, ROUTER_ROWS), 0)
    r_col = lax.broadcasted_iota(I32, (ROUTER_ROWS, ROUTER_ROWS), 1)
    lower_rows = jnp.where(r_col < r_row, 1.0, 0.0).astype(BF16)
    run_start = _dot(lower_rows,
                     jnp.broadcast_to(blocks, (ROUTER_ROWS, LANES)).astype(BF16))[:, 0:1]
    slot = before + run_start
    q1 = jnp.sum(jnp.where(sel1, slot, 0.0), axis=0, keepdims=True)
    q2 = jnp.sum(jnp.where(sel2, slot, 0.0), axis=0, keepdims=True)
    tab_lane = lax.broadcasted_iota(I32, (ROUTER_ROWS, LANES), 1)
    tab_ref[0] = jnp.where(tab_lane == 0, count,
                           jnp.where(tab_lane == 1, carry[...],
                                     jnp.where(tab_lane == 2, run_start, 0.0)))
    carry[...] = carry[...] + count

    slot_ref[0, 0:1, :] = (q1 * TOKEN_ROWS).astype(I32)
    slot_ref[0, 1:2, :] = (q2 * TOKEN_ROWS).astype(I32)
    wgt_ref[0, 0:1, :] = w1
    wgt_ref[0, 1:2, :] = w2


def _post(x_args, mixed_p, mixed_s, mods, layer, norm_w, w_out_bf16, wr_hi, wr_lo,
          tiles_per_sample):
    split_x = len(x_args) == 2
    tp, ts = mixed_p.shape[0], mixed_s.shape[0]
    t, d = tp + ts, D_MODEL
    npt, nst = tp // ROW_TILE, ts // ROW_TILE
    tile = lambda i: (i, 0)
    if split_x:
        x_specs = [pl.BlockSpec((ROW_TILE, d), lambda i: (jnp.minimum(i, npt - 1), 0)),
                   pl.BlockSpec((ROW_TILE, d), lambda i: (jnp.maximum(i - npt, 0), 0))]
    else:
        x_specs = [pl.BlockSpec((ROW_TILE, d), tile)]
    in_specs = x_specs + [
        pl.BlockSpec((ROW_TILE, d), lambda i: (jnp.minimum(i, npt - 1), 0)),
        pl.BlockSpec((ROW_TILE, d), lambda i: (jnp.maximum(i - npt, 0), 0)),
        pl.BlockSpec((1, 6, d), lambda i: (_mod_row(i, layer, npt, tiles_per_sample), 0, 0)),
        pl.BlockSpec((1, d), lambda i: (0, 0)),
        pl.BlockSpec((d, d), lambda i: (0, 0)),
        pl.BlockSpec((d, LANES), lambda i: (0, 0)),
        pl.BlockSpec((d, LANES), lambda i: (0, 0))]
    return pl.pallas_call(
        functools.partial(_post_kernel, split_x=split_x, n_prompt_tiles=npt),
        grid=(npt + nst,),
        in_specs=in_specs,
        out_specs=(pl.BlockSpec((ROW_TILE, d), tile),
                   pl.BlockSpec((ROW_TILE * TOKEN_ROWS, LANES), tile),
                   pl.BlockSpec((1, 2, ROW_TILE), lambda i: (i, 0, 0)),
                   pl.BlockSpec((1, 2, ROW_TILE), lambda i: (i, 0, 0)),
                   pl.BlockSpec((1, ROUTER_ROWS, LANES), lambda i: (i, 0, 0))),
        out_shape=(jax.ShapeDtypeStruct((t, d), F32),
                   jax.ShapeDtypeStruct((t * TOKEN_ROWS, LANES), F32),
                   jax.ShapeDtypeStruct((npt + nst, 2, ROW_TILE), I32),
                   jax.ShapeDtypeStruct((npt + nst, 2, ROW_TILE), F32),
                   jax.ShapeDtypeStruct((npt + nst, ROUTER_ROWS, LANES), F32)),
        scratch_shapes=[pltpu.VMEM((ROUTER_ROWS, LANES), F32),
                        pltpu.VMEM((ROW_TILE, ROW_TILE), BF16)],
        compiler_params=_params(1),
        name=f"post{layer}",
    )(*x_args, mixed_p, mixed_s, mods, norm_w, w_out_bf16, wr_hi, wr_lo)


def _for_blocks(tab_ref, fn):
    block_rows = MOVE_BLOCK * TOKEN_ROWS
    count = tab_ref[0, 0, MAX_BLOCKS]

    def call(k, parity):
        fn(pl.multiple_of(k * block_rows, block_rows),
           pl.multiple_of(tab_ref[0, 0, k], TOKEN_ROWS), parity)

    def body(k2, c):
        call(2 * k2, 0)

        @pl.when(2 * k2 + 1 < count)
        def _():
            call(2 * k2 + 1, 1)
        return c

    lax.fori_loop(0, _cdiv(count, 2), body, 0)


def _wait_blocks(tab_ref, copy):
    def body(k, c):
        copy.wait()
        return c

    lax.fori_loop(0, tab_ref[0, 0, MAX_BLOCKS], body, 0)


def _dispatch_kernel(zero_ref, tab_ref, prev_tab_ref, q_ref, h2_ref, hs_ref, zero_buf, stage, sem):
    j = pl.program_id(0)
    slot = j % 2
    block_rows = MOVE_BLOCK * TOKEN_ROWS

    @pl.when(j == 0)
    def _():
        zero_buf[...] = jnp.zeros_like(zero_buf)

        def zero_copy(k):
            start = pl.multiple_of(zero_ref[k], EXPERT_TILE * TOKEN_ROWS)
            return pltpu.make_async_copy(
                zero_buf, hs_ref.at[pl.ds(start, EXPERT_TILE * TOKEN_ROWS)], sem.at[0])

        def start_zero(k2, c):
            for parity in range(2):
                @pl.when(zero_ref[2 * k2 + parity] >= 0)
                def _():
                    zero_copy(2 * k2 + parity).start(priority=parity)
            return c

        def wait_zero(k, c):
            @pl.when(zero_ref[k] >= 0)
            def _():
                zero_copy(k).wait()
            return c

        lax.fori_loop(0, zero_ref.shape[0] // 2, start_zero, 0)
        stage[...] = jnp.zeros_like(stage)
        lax.fori_loop(0, zero_ref.shape[0], wait_zero, 0)

    def place(r, c):
        tok = h2_ref[pl.ds(pl.multiple_of(r * TOKEN_ROWS, TOKEN_ROWS), TOKEN_ROWS), :]
        for s in range(2):
            row = pl.multiple_of(q_ref[0, s, r], TOKEN_ROWS)
            stage[slot, pl.ds(row, TOKEN_ROWS), :] = tok
        return c

    lax.fori_loop(0, ROW_TILE, place, 0, unroll=8)

    def block_copy(buf, stage_row, sorted_row):
        return pltpu.make_async_copy(stage.at[buf, pl.ds(stage_row, block_rows)],
                                     hs_ref.at[pl.ds(sorted_row, block_rows)], sem.at[buf])

    @pl.when(j > 0)
    def _():
        _wait_blocks(prev_tab_ref, block_copy(1 - slot, 0, 0))

    _for_blocks(tab_ref, lambda a, b, parity: block_copy(slot, a, b).start(priority=parity))

    @pl.when(j == pl.num_programs(0) - 1)
    def _():
        _wait_blocks(tab_ref, block_copy(slot, 0, 0))


def _dispatch(zero_tiles, block_tab, slots, h2, n_rows):
    t = h2.shape[0] // TOKEN_ROWS
    nt = t // ROW_TILE
    assert zero_tiles.shape[0] % 2 == 0
    smem_tile = lambda shape: pl.BlockSpec((1,) + shape, lambda j, *_: (j, 0, 0),
                                           memory_space=pltpu.SMEM)
    grid_spec = pltpu.PrefetchScalarGridSpec(
        num_scalar_prefetch=1,
        grid=(nt,),
        in_specs=[smem_tile((1, LANES)),
                  pl.BlockSpec((1, 1, LANES), lambda j, *_: (jnp.maximum(j - 1, 0), 0, 0),
                               memory_space=pltpu.SMEM),
                  smem_tile((2, ROW_TILE)),
                  pl.BlockSpec((ROW_TILE * TOKEN_ROWS, LANES), lambda j, *_: (j, 0))],
        out_specs=pl.BlockSpec(memory_space=pl.ANY),
        scratch_shapes=[pltpu.VMEM((EXPERT_TILE * TOKEN_ROWS, LANES), F32),
                        pltpu.VMEM((2, STAGE_TOKENS * TOKEN_ROWS, LANES), F32),
                        pltpu.SemaphoreType.DMA((2,))])
    return pl.pallas_call(
        _dispatch_kernel,
        grid_spec=grid_spec,
        out_shape=jax.ShapeDtypeStruct((n_rows * TOKEN_ROWS, LANES), F32),
        compiler_params=_params(1),
        name="dispatch",
    )(zero_tiles, block_tab, block_tab, slots, h2)


def _expert_kernel(te_ref, src_ref, nv_ref, run_ref, nxt_ref, hs_ref, w1_hbm, w3_hbm, w2_hbm,
                   ys_ref, w1f, w3f, w2f, w1b, w3b, w2b, sem, *, layer):
    i = pl.program_id(0)

    def weight_copies(e, buf):
        return [pltpu.make_async_copy(src.at[layer, e], dst.at[buf], sem.at[buf])
                for src, dst in ((w1_hbm, w1f), (w3_hbm, w3f), (w2_hbm, w2f))]

    @pl.when(i == 0)
    def _():
        for c in weight_copies(te_ref[0], 0):
            c.start()

    first = (i == 0) | (run_ref[i] != run_ref[jnp.maximum(i - 1, 0)])

    @pl.when(first)
    def _():
        buf = run_ref[i] % 2
        for c in weight_copies(te_ref[i], buf):
            c.wait()

        @pl.when(nxt_ref[i] >= 0)
        def _():
            for c in weight_copies(nxt_ref[i], 1 - buf):
                c.start()

        w1b[...] = w1f[buf].astype(BF16)
        w3b[...] = w3f[buf].astype(BF16)
        w2b[...] = w2f[buf].astype(BF16)

    @pl.when(nv_ref[i] > 0)
    def _():
        h = _from_token_major(hs_ref, EXPERT_TILE).astype(BF16)
        g = _silu(_dot(h, w1b[...])) * _dot(h, w3b[...])
        _to_token_major(ys_ref, _dot(g.astype(BF16), w2b[...]))

    @pl.when(nv_ref[i] == 0)
    def _():
        ys_ref[...] = jnp.zeros_like(ys_ref)


def _experts(tile_expert, tile_src, tile_rows, hs, w1, w3, w2, layer):
    n_rows, d = hs.shape[0] // TOKEN_ROWS, D_MODEL
    nt = n_rows // EXPERT_TILE
    hid = w1.shape[-1]
    tok_tile = (EXPERT_TILE * TOKEN_ROWS, LANES)
    changed = jnp.concatenate([jnp.zeros((1,), I32),
                               (tile_expert[1:] != tile_expert[:-1]).astype(I32)])
    run = jnp.cumsum(changed).astype(I32)
    later = jnp.where(run[None, :] > run[:, None], tile_expert[None, :], N_EXPERTS)
    next_expert = jnp.min(later, axis=1)
    next_expert = jnp.where(next_expert < N_EXPERTS, next_expert, -1).astype(I32)
    grid_spec = pltpu.PrefetchScalarGridSpec(
        num_scalar_prefetch=5,
        grid=(nt,),
        in_specs=[pl.BlockSpec(tok_tile, lambda i, te, src, nv, run, nxt: (src[i], 0)),
                  pl.BlockSpec(memory_space=pl.ANY), pl.BlockSpec(memory_space=pl.ANY),
                  pl.BlockSpec(memory_space=pl.ANY)],
        out_specs=pl.BlockSpec(tok_tile, lambda i, te, src, nv, run, nxt: (i, 0)),
        scratch_shapes=[pltpu.VMEM((2, d, hid), F32), pltpu.VMEM((2, d, hid), F32),
                        pltpu.VMEM((2, hid, d), F32),
                        pltpu.VMEM((d, hid), BF16), pltpu.VMEM((d, hid), BF16),
                        pltpu.VMEM((hid, d), BF16), pltpu.SemaphoreType.DMA((2,))])
    return pl.pallas_call(
        functools.partial(_expert_kernel, layer=layer),
        grid_spec=grid_spec,
        out_shape=jax.ShapeDtypeStruct(hs.shape, F32),
        compiler_params=_params(1),
        name=f"experts{layer}",
    )(tile_expert, tile_src, tile_rows, run, next_expert, hs, w1, w3, w2)


def _combine_kernel(tab_ref, next_tab_ref, q_ref, w_ref, x1_ref, mod_ref, fw_ref, ys_ref, out_ref,
                    stage, y_tok, sem, *, final_norm):
    i = pl.program_id(0)
    slot = i % 2
    block_rows = MOVE_BLOCK * TOKEN_ROWS

    def block_copy(buf, stage_row, sorted_row):
        return pltpu.make_async_copy(ys_ref.at[pl.ds(sorted_row, block_rows)],
                                     stage.at[buf, pl.ds(stage_row, block_rows)], sem.at[buf])

    def fetch(tab, buf):
        _for_blocks(tab, lambda a, b, parity: block_copy(buf, a, b).start(priority=parity))

    @pl.when(i == 0)
    def _():
        fetch(tab_ref, slot)

    @pl.when(i + 1 < pl.num_programs(0))
    def _():
        fetch(next_tab_ref, 1 - slot)

    _wait_blocks(tab_ref, block_copy(slot, 0, 0))

    def pick(r, c):
        rows = [stage[slot, pl.ds(pl.multiple_of(q_ref[0, s, r], TOKEN_ROWS), TOKEN_ROWS), :]
                for s in range(2)]
        y_tok[pl.ds(pl.multiple_of(r * TOKEN_ROWS, TOKEN_ROWS), TOKEN_ROWS), :] = (
            w_ref[0, 0, r] * rows[0] + w_ref[0, 1, r] * rows[1])
        return c

    lax.fori_loop(0, ROW_TILE, pick, 0, unroll=8)
    x2 = x1_ref[...] + mod_ref[0, 5:6, :] * _from_token_major(y_tok, ROW_TILE)
    if final_norm:
        x2 = _rms(x2) * fw_ref[...]
    out_ref[...] = x2


def _combine(block_tab, slots, weights, x1, mods, layer, final_w, ys, tile0, n_tiles,
             n_prompt_tiles, tiles_per_sample, final_norm):
    d = D_MODEL
    tile = lambda i: (tile0 + i, 0)
    mod_map = lambda i: (_mod_row(tile0 + i, layer, n_prompt_tiles, tiles_per_sample), 0, 0)
    smem_tile = lambda shape: pl.BlockSpec((1,) + shape, lambda i: (tile0 + i, 0, 0),
                                           memory_space=pltpu.SMEM)
    return pl.pallas_call(
        functools.partial(_combine_kernel, final_norm=final_norm),
        grid=(n_tiles,),
        in_specs=[smem_tile((1, LANES)),
                  pl.BlockSpec((1, 1, LANES),
                               lambda i: (tile0 + jnp.minimum(i + 1, n_tiles - 1), 0, 0),
                               memory_space=pltpu.SMEM),
                  smem_tile((2, ROW_TILE)), smem_tile((2, ROW_TILE)),
                  pl.BlockSpec((ROW_TILE, d), tile),
                  pl.BlockSpec((1, 6, d), mod_map),
                  pl.BlockSpec((1, d), lambda i: (0, 0)),
                  pl.BlockSpec(memory_space=pl.ANY)],
        out_specs=pl.BlockSpec((ROW_TILE, d), lambda i: (i, 0)),
        out_shape=jax.ShapeDtypeStruct((n_tiles * ROW_TILE, d), F32),
        scratch_shapes=[pltpu.VMEM((2, STAGE_TOKENS * TOKEN_ROWS, LANES), F32),
                        pltpu.VMEM((ROW_TILE * TOKEN_ROWS, LANES), F32),
                        pltpu.SemaphoreType.DMA((2,))],
        compiler_params=_params(1),
        name=f"combine{layer}_{tile0}",
    )(block_tab, block_tab, slots, weights, x1, mods, final_w, ys)


def _moe(h2, slots, weights, tile_tab, w1, w3, w2, layer):
    t = h2.shape[0] // TOKEN_ROWS
    n_tiles = t // ROW_TILE
    extra_tiles = N_EXPERTS + _cdiv(N_EXPERTS * MOVE_BLOCK, EXPERT_TILE)
    n_rows = 2 * t + extra_tiles * EXPERT_TILE
    nt = n_rows // EXPERT_TILE
    tab = tile_tab[:, N_GROUPS:N_GROUPS + N_EXPERTS, 0:3].transpose(0, 2, 1).astype(I32)
    cnt = tab[-1, 0] + tab[-1, 1]
    tight = _cdiv(cnt, EXPERT_TILE) * EXPERT_TILE
    padded = jnp.where(cnt > 0, _cdiv(cnt + MOVE_BLOCK - 1, EXPERT_TILE) * EXPERT_TILE, 0)
    ends = jnp.cumsum(padded)
    offsets = ends - padded
    tails = jnp.where(cnt > 0, ends - EXPERT_TILE, -1)
    tails2 = jnp.where(padded > tight, ends - 2 * EXPERT_TILE, -1)
    used = ends[-1] // EXPERT_TILE
    tile_start = jnp.arange(nt, dtype=I32) * EXPERT_TILE
    unused = (used + jnp.arange(extra_tiles, dtype=I32)) * EXPERT_TILE
    zero_tiles = jnp.concatenate([tails, tails2, jnp.where(unused < n_rows, unused, -1)])
    zero_tiles = jnp.where(zero_tiles >= 0, zero_tiles * TOKEN_ROWS, -1).astype(I32)
    tile_src = jnp.minimum(jnp.arange(nt, dtype=I32), used - 1)
    tile_expert = jnp.sum((tile_src * EXPERT_TILE)[:, None] >= ends[None, :], axis=1).astype(I32)
    tile_rows = jnp.where(tile_start < ends[-1],
                          jnp.clip(cnt[tile_expert] - (tile_start - offsets[tile_expert]),
                                   0, EXPERT_TILE), 0).astype(I32)
    n_blocks = _cdiv(tab[:, 0], MOVE_BLOCK)
    blocks_through = jnp.cumsum(n_blocks, axis=1)
    k = jnp.arange(MAX_BLOCKS, dtype=I32)
    owner = jnp.sum(blocks_through[:, None, :] <= k[None, :, None], axis=2)
    is_owner = owner[:, :, None] == jnp.arange(N_EXPERTS, dtype=I32)[None, None, :]
    pick = lambda v: jnp.sum(jnp.where(is_owner, v[:, None, :], 0), axis=2)
    run_first = pick(offsets[None, :] + tab[:, 1])
    block_in_run = k[None, :] - pick(blocks_through - n_blocks)
    sorted_row = (run_first + block_in_run * MOVE_BLOCK) * TOKEN_ROWS
    block_tab = jnp.concatenate(
        [sorted_row, blocks_through[:, -1:],
         jnp.zeros((n_tiles, LANES - MAX_BLOCKS - 1), I32)], axis=1).astype(I32)[:, None, :]
    hs = _dispatch(zero_tiles, block_tab, slots, h2, n_rows)
    ys = _experts(tile_expert, tile_src, tile_rows, hs, w1, w3, w2, layer)
    return ys, (block_tab, slots, weights)


def _rope(x, cos, sin_signed):
    lane = lax.broadcasted_iota(I32, (x.shape[0], LANES), 1)
    low = (lane % 32) < 16
    outs = []
    for j in range(x.shape[1] // LANES):
        xb = x[:, j * LANES:(j + 1) * LANES]
        partner = jnp.where(low, pltpu.roll(xb, LANES - 16, 1), pltpu.roll(xb, 16, 1))
        outs.append(xb * cos + partner * sin_signed)
    return jnp.concatenate(outs, axis=1)


def _inproj1_prompt_kernel(x_ref, mod_ref, nw_ref, w_ref, q_ref, k_ref, v_ref, kc_ref, vc_ref):
    d = D_MODEL
    h = _modulate(x_ref[...], nw_ref[...], mod_ref[0, 0:1, :], mod_ref[0, 1:2, :]).astype(BF16)
    q_ref[...] = (_dot(h, w_ref[:, 0:d]) * (DIFF_HD ** -0.5)).astype(BF16)
    k = _dot(h, w_ref[:, d:2 * d])
    v = _dot(h, w_ref[:, 2 * d:3 * d])
    k_ref[...] = k.astype(BF16)
    v_ref[...] = v.astype(BF16)
    for b in range(k.shape[0] // ATTN_TILE):
        kc_ref[b * d:(b + 1) * d, :] = k[b * ATTN_TILE:(b + 1) * ATTN_TILE].T
    _to_token_major(vc_ref, v)


def _inproj1_sample_kernel(x_ref, mod_ref, nw_ref, w_ref, cos_ref, sin_ref, q_ref, k_ref, v_ref):
    d = D_MODEL
    h = _modulate(x_ref[...], nw_ref[...], mod_ref[0, 0:1, :], mod_ref[0, 1:2, :]).astype(BF16)
    cos, sin = cos_ref[...], sin_ref[...]
    q_ref[...] = (_rope(_dot(h, w_ref[:, 0:d]), cos, sin) * (DIFF_HD ** -0.5)).astype(BF16)
    k_ref[...] = _rope(_dot(h, w_ref[:, d:2 * d]), cos, sin).astype(BF16)
    v_ref[...] = _dot(h, w_ref[:, 2 * d:3 * d]).astype(BF16)


def _inproj1(x, mods, norm_w, w_bf16, n_prompt_tiles, n_sample_tiles, tiles_per_sample,
             cos_t, sin_t):
    d = D_MODEL
    rows = INPROJ1_TILE
    npt, nst = n_prompt_tiles, n_sample_tiles
    common = [pl.BlockSpec((1, d), lambda i: (0, 0)), pl.BlockSpec((d, 3 * d), lambda i: (0, 0))]
    tile = lambda i: (i, 0)
    out_specs = tuple(pl.BlockSpec((rows, d), tile) for _ in range(3))
    qp, kp, vp, k_cache, v_cache = pl.pallas_call(
        _inproj1_prompt_kernel,
        grid=(npt,),
        in_specs=[pl.BlockSpec((rows, d), tile),
                  pl.BlockSpec((1, 6, d), lambda i: (8, 0, 0))] + common,
        out_specs=out_specs + (pl.BlockSpec((rows // ATTN_TILE * d, ATTN_TILE), tile),
                               pl.BlockSpec((rows * TOKEN_ROWS, LANES), tile)),
        out_shape=tuple(jax.ShapeDtypeStruct((npt * rows, d), BF16) for _ in range(3))
        + (jax.ShapeDtypeStruct((npt * rows // ATTN_TILE * d, ATTN_TILE), F32),
           jax.ShapeDtypeStruct((npt * rows * TOKEN_ROWS, LANES), F32)),
        compiler_params=_params(1),
        name="inproj1_prompt",
    )(x, mods, norm_w, w_bf16)
    rope_tile = lambda i: (i % tiles_per_sample, 0)
    qs, ks, vs = pl.pallas_call(
        _inproj1_sample_kernel,
        grid=(nst,),
        in_specs=[pl.BlockSpec((rows, d), lambda i: (npt + i, 0)),
                  pl.BlockSpec((1, 6, d), lambda i: (8 + 1 + i // tiles_per_sample, 0, 0))]
        + common + [pl.BlockSpec((rows, LANES), rope_tile),
                    pl.BlockSpec((rows, LANES), rope_tile)],
        out_specs=out_specs,
        out_shape=tuple(jax.ShapeDtypeStruct((nst * rows, d), BF16) for _ in range(3)),
        compiler_params=_params(1),
        name="inproj1_sample",
    )(x, mods, norm_w, w_bf16, cos_t, sin_t)
    return (qp, kp, vp), (qs, ks, vs), (k_cache, v_cache)


def _rope_tables(n_tok):
    half = DIFF_HD // 4
    pos = np.arange(n_tok)
    lane = np.arange(LANES)
    sub = lane % DIFF_HD
    p = np.where(sub[None, :] < DIFF_HD // 2, (pos // GRID_W)[:, None], (pos % GRID_W)[:, None])
    inv = jnp.asarray(ROPE_THETA, F32) ** (-jnp.asarray(sub % half, F32) / half)
    ang = jnp.asarray(p, F32) * inv[None, :]
    sign = np.where((lane % (2 * half)) < half, -1.0, 1.0).astype(np.float32)
    return jnp.cos(ang), jnp.sin(ang) * sign[None, :]


def _diffattn_kernel(*refs, has_cache, lam_init):
    if has_cache:
        q_ref, k_ref, v_ref, ck_ref, cv_ref, lam_ref, sw_ref, o_ref = refs
    else:
        q_ref, k_ref, v_ref, lam_ref, sw_ref, o_ref = refs
    hd2 = 2 * DIFF_HD
    lv = lam_ref[...]
    lam = (jnp.exp(jnp.sum(lv[0:1] * lv[1:2], axis=1, keepdims=True))
           - jnp.exp(jnp.sum(lv[2:3] * lv[3:4], axis=1, keepdims=True)) + lam_init)
    lane = lax.broadcasted_iota(I32, (q_ref.shape[0], hd2), 1)
    for h in range(DIFF_HEADS):
        cols = slice(h * hd2, (h + 1) * hd2)
        q = q_ref[:, cols]
        zero = jnp.zeros_like(q)
        k_new = k_ref[:, cols].astype(BF16)
        values = [v_ref[:, cols].astype(BF16)]
        if has_cache:
            past = ck_ref.shape[1]
            k_past_t = ck_ref[cols, :].astype(BF16)
            values.append(cv_ref[pl.ds(h, past, stride=DIFF_HEADS), :].astype(BF16))
        o = None
        for c in range(2):
            qc = jnp.where((lane < DIFF_HD) == (c == 0), q, zero)
            s = [_dot_nt(qc, k_new)]
            if has_cache:
                s.append(_dot(qc, k_past_t))
            mx = functools.reduce(jnp.maximum, [jnp.max(si, axis=1, keepdims=True) for si in s])
            e = [jnp.exp(si - mx) for si in s]
            z = functools.reduce(jnp.add, [jnp.sum(ei, axis=1, keepdims=True) for ei in e])
            pv = functools.reduce(jnp.add, [_dot(ei.astype(BF16), v) for ei, v in zip(e, values)])
            pv = pv * (1.0 / z)
            o = pv if c == 0 else o - lam * pv
        o_ref[:, cols] = ((_rms(o) * sw_ref[...]) * (1.0 - lam_init)).astype(BF16)


def _diffattn(q, k, v, lam_vecs, subln_w, batch, seq_len, q_block, lam_init, cache=None):
    d = D_MODEL
    nq = seq_len // q_block
    has_cache = cache is not None
    kv_spec = pl.BlockSpec((seq_len, d), lambda b, qi: (b, 0))
    in_specs = [pl.BlockSpec((q_block, d), lambda b, qi: (b * nq + qi, 0)), kv_spec, kv_spec]
    args = [q, k, v]
    if has_cache:
        past = cache[0].shape[1]
        in_specs += [pl.BlockSpec((d, past), lambda b, qi: (b, 0)),
                     pl.BlockSpec((past * DIFF_HEADS, 2 * DIFF_HD), lambda b, qi: (b, 0))]
        args += list(cache)
    in_specs += [pl.BlockSpec((4, DIFF_HD), lambda b, qi: (0, 0)),
                 pl.BlockSpec((1, 2 * DIFF_HD), lambda b, qi: (0, 0))]
    args += [lam_vecs, subln_w]
    return pl.pallas_call(
        functools.partial(_diffattn_kernel, has_cache=has_cache, lam_init=lam_init),
        grid=(batch, nq),
        in_specs=in_specs,
        out_specs=pl.BlockSpec((q_block, d), lambda b, qi: (b * nq + qi, 0)),
        out_shape=jax.ShapeDtypeStruct((batch * seq_len, d), BF16),
        compiler_params=_params(2),
        name="diffattn_cache" if has_cache else "diffattn",
    )(*args)


def _router_weights(router_group, router_expert):
    w = jnp.concatenate([router_group, router_expert], axis=1)
    w = jnp.pad(w, ((0, 0), (0, LANES - w.shape[1])))
    hi = w.astype(BF16)
    return hi, (w - hi.astype(F32)).astype(BF16)


def _inproj0_weights(w_in):
    gq, gk, gv, gg, gaf, gab, hq, hff, hfb, hi, hg = jnp.split(
        w_in, [256, 512, 1024, 1536, 1552, 1568, 1824, 2080, 2336, 2848], axis=1)
    w = jnp.concatenate([gq, gk, gv, gg, hq, hff, hfb, hi, hg, gaf, gab], axis=1)
    return jnp.pad(w, ((0, 0), (0, AB_COLS - w.shape[1]))).astype(BF16)


def kernel(x_prompt, x_sample, state_gla, state_hgrn, cache_diff_k, cache_diff_v, c, c_ctx,
           w_ada, b_ada, norm1_w, norm2_w, w_in_ab, gla_a2, gla_a_bias, hgrn_lb, gla_onorm_w,
           hgrn_onorm_w, w_out_ab, w_in_c, lam_q1, lam_k1, lam_q2, lam_k2, diff_subln_w, w_out_c,
           router_group, router_expert, moe_w1, moe_w3, moe_w2, final_norm_w):
    bp, lp, d = x_prompt.shape
    bs, ls, _ = x_sample.shape
    depth = w_ada.shape[0]
    assert depth == 2 and d == D_MODEL and bs <= 7
    tp, ts = bp * lp, bs * ls
    npt, nst = tp // ROW_TILE, ts // ROW_TILE
    tps = ls // ROW_TILE
    xp = x_prompt.reshape(tp, d)
    xs = x_sample.reshape(ts, d)

    cond8 = jnp.concatenate([c_ctx[None, :], c, jnp.zeros((7 - bs, d), F32)], axis=0)
    mods = _adaln(cond8, w_ada, b_ada).reshape(depth * 8, 6, d)

    proj = _inproj0(xp, xs, mods, norm1_w[0:1], _inproj0_weights(w_in_ab[0]), ls)
    a_bias = gla_a_bias[0][:, None, :]
    scan_args = (gla_a2[0], a_bias, hgrn_lb, gla_onorm_w[0:1], hgrn_onorm_w[0:1])
    mixed_p, new_state_gla, new_state_hgrn = _scan(proj, 0, bp, lp, *scan_args)
    s0 = jnp.concatenate([state_gla[:, 0], state_hgrn[:, 0]], axis=2).swapaxes(-1, -2)
    s0 = s0.reshape(bs, 2, SCAN_PAIRS, 2, HEAD_DV, HEAD_DK)
    zero = jnp.zeros_like(s0[:, :, :, 0])
    s0 = jnp.concatenate([jnp.concatenate([s0[:, :, :, 0], zero], axis=-1),
                          jnp.concatenate([zero, s0[:, :, :, 1]], axis=-1)], axis=-2)
    mixed_s = _scan(proj, tp, bs, ls, *scan_args, s0=s0)

    wr = _router_weights(router_group[0], router_expert[0])
    x1, *routed = _post((xp, xs), mixed_p, mixed_s, mods, 0, norm2_w[0:1],
                        w_out_ab[0].astype(BF16), *wr, tps)
    ys, tables = _moe(*routed, moe_w1, moe_w3, moe_w2, 0)
    x2 = _combine(*tables, x1, mods, 0, final_norm_w[None, :], ys, 0, npt + nst, npt, tps, False)

    lam_init = 0.8 - 0.6 * math.exp(-0.3 * 1)
    cos_t, sin_t = _rope_tables(ls)
    (qp, kp, vp), (qs, ks, vs), (k_cache, v_cache) = _inproj1(
        x2, mods, norm1_w[1:2], w_in_c[0].astype(BF16), tp // INPROJ1_TILE, ts // INPROJ1_TILE,
        ls // INPROJ1_TILE, cos_t, sin_t)
    lam_vecs = jnp.stack([lam_q1[0], lam_k1[0], lam_q2[0], lam_k2[0]])
    att_p = _diffattn(qp, kp, vp, lam_vecs, diff_subln_w[0:1], bp, lp, lp, lam_init)
    past = cache_diff_k.shape[2]
    assert lp == ATTN_TILE and DIFF_HEADS == TOKEN_ROWS
    cache = (cache_diff_k[:, 0].transpose(0, 2, 3, 4, 1).reshape(bs * d, past),
             cache_diff_v[:, 0].reshape(bs * past * DIFF_HEADS, 2 * DIFF_HD))
    att_s = _diffattn(qs, ks, vs, lam_vecs, diff_subln_w[0:1], bs, ls, SAMPLE_Q_BLOCK, lam_init,
                      cache)

    wr = _router_weights(router_group[1], router_expert[1])
    x3, *routed = _post((x2,), att_p, att_s, mods, 1, norm2_w[1:2],
                        w_out_c[0].astype(BF16), *wr, tps)
    ys, tables = _moe(*routed, moe_w1, moe_w3, moe_w2, 1)
    fw = final_norm_w[None, :]
    y_p = _combine(*tables, x3, mods, 1, fw, ys, 0, npt, npt, tps, True)
    y_s = _combine(*tables, x3, mods, 1, fw, ys, npt, nst, npt, tps, True)

    return (y_p.reshape(bp, lp, d), y_s.reshape(bs, ls, d), new_state_gla, new_state_hgrn,
            k_cache.reshape(bp, 1, DIFF_HEADS, 2, DIFF_HD, lp).transpose(0, 1, 5, 2, 3, 4),
            v_cache.reshape(bp, 1, lp, DIFF_HEADS, 2 * DIFF_HD))
```

```python
import functools
import math

import jax
import jax.numpy as jnp
import numpy as np
from jax import lax
from jax.experimental import pallas as pl
from jax.experimental.pallas import tpu as pltpu

F32 = jnp.float32
BF16 = jnp.bfloat16
I32 = jnp.int32

D_MODEL = 1024
GLA_HEADS = 4
HGRN_HEADS = 4
SCAN_HEADS = GLA_HEADS + HGRN_HEADS
SCAN_PAIRS = SCAN_HEADS // 2
HEAD_DK = 64
HEAD_DV = 128
GATE_RANK = 16
GLA_GATE_NORM = 16.0
DIFF_HEADS = 8
DIFF_HD = 64
GRID_W = 64
ROPE_THETA = 10000.0
N_GROUPS = 4
EXPERTS_PER_GROUP = 8
N_EXPERTS = N_GROUPS * EXPERTS_PER_GROUP
MOE_HIDDEN = 512
EPS = 1e-6
LANES = 128
TOKEN_ROWS = D_MODEL // LANES
NEG_BIG = -1e30
ROUTER_ROWS = 48

ROW_TILE = 512
ATTN_TILE = 256
SAMPLE_Q_BLOCK = 512
PROMPT_SEQS_PER_STEP = 2
ADA_TILE = 1536
INPROJ0_TILE = 512
INPROJ1_TILE = 512
SCAN_CHUNK = 64
EXPERT_TILE = 256
MOVE_BLOCK = 16
STAGE_TOKENS = 2 * ROW_TILE + N_EXPERTS * MOVE_BLOCK
MAX_BLOCKS = STAGE_TOKENS // MOVE_BLOCK
VMEM_LIMIT = 56 * 1024 * 1024
SCAN_INPUT_DOUBLE_BUFFER_BYTES = 16 * 1024 * 1024

_C_GQ, _C_GK, _C_GV, _C_GG = 0, 256, 512, 1024
_C_HQ, _C_HFF, _C_HFB, _C_HI, _C_HG = 1536, 1792, 2048, 2304, 2816
_C_GAF, _C_GAB = 3328, 3344
AB_COLS = 3456


def _params(n_axes, vmem=VMEM_LIMIT):
    return pltpu.CompilerParams(dimension_semantics=("arbitrary",) * n_axes,
                                vmem_limit_bytes=vmem)


def _cdiv(a, b):
    return (a + b - 1) // b


def _dot(a, b):
    return jnp.dot(a, b, preferred_element_type=F32)


def _dot_nt(a, b):
    return lax.dot_general(a, b, (((1,), (1,)), ((), ())), preferred_element_type=F32)


def _dot_tn(a, b):
    return lax.dot_general(a, b, (((0,), (0,)), ((), ())), preferred_element_type=F32)


def _split_bf16(x):
    hi = x.astype(BF16)
    lo = (x - hi.astype(F32)).astype(BF16)
    return hi, lo


def _silu(x):
    return x * jax.nn.sigmoid(x)


def _log_sigmoid(x):
    return jnp.minimum(x, 0.0) - jnp.log(1.0 + jnp.exp(-jnp.abs(x)))


def _rms(x):
    return x * lax.rsqrt(jnp.mean(x * x, axis=-1, keepdims=True) + EPS)


def _modulate(x, norm_w, shift, scale):
    return (_rms(x) * norm_w) * (1.0 + scale) + shift


def _to_token_major(dst_ref, x, row0=0):
    n = x.shape[0]
    for s in range(TOKEN_ROWS):
        dst_ref[pl.ds(row0 + s, n, stride=TOKEN_ROWS), :] = x[:, s * LANES:(s + 1) * LANES]


def _from_token_major(src_ref, n, row0=0):
    return jnp.concatenate([src_ref[pl.ds(row0 + s, n, stride=TOKEN_ROWS), :]
                            for s in range(TOKEN_ROWS)], axis=1)


def _ada_kernel(c_ref, w_ref, b_ref, o_ref):
    s = _silu(c_ref[...])
    o_ref[0] = _dot(s.astype(BF16), w_ref[0].astype(BF16)) + b_ref[0]


def _adaln(cond8, w_ada, b_ada):
    depth, d, n = w_ada.shape
    tn = ADA_TILE
    return pl.pallas_call(
        _ada_kernel,
        grid=(depth, n // tn),
        in_specs=[pl.BlockSpec((8, d), lambda l, j: (0, 0)),
                  pl.BlockSpec((1, d, tn), lambda l, j: (l, 0, j)),
                  pl.BlockSpec((1, 1, tn), lambda l, j: (l, 0, j))],
        out_specs=pl.BlockSpec((1, 8, tn), lambda l, j: (l, 0, j)),
        out_shape=jax.ShapeDtypeStruct((depth, 8, n), F32),
        compiler_params=_params(2),
        name="adaln",
    )(cond8, w_ada, b_ada.reshape(depth, 1, n))


def _mod_row(i, layer, n_prompt_tiles, tiles_per_sample):
    r = jnp.where(i < n_prompt_tiles, 0, 1 + (i - n_prompt_tiles) // tiles_per_sample)
    return layer * 8 + r


def _inproj0_kernel(xp_ref, xs_ref, mod_ref, nw_ref, w_ref, o_ref, *, n_prompt_tiles):
    i = pl.program_id(0)
    x = jnp.where(i < n_prompt_tiles, xp_ref[...], xs_ref[...])
    h = _modulate(x, nw_ref[...], mod_ref[0, 0:1, :], mod_ref[0, 1:2, :])
    o_ref[...] = _dot(h.astype(BF16), w_ref[...])


def _inproj0(xp, xs, mods, norm_w, w_bf16, sample_len):
    tp, d = xp.shape
    ts = xs.shape[0]
    n = w_bf16.shape[1]
    tile = INPROJ0_TILE
    npt, nst = tp // tile, ts // tile
    mod_map = lambda i: (_mod_row(i, 0, npt, sample_len // tile), 0, 0)
    return pl.pallas_call(
        functools.partial(_inproj0_kernel, n_prompt_tiles=npt),
        grid=(npt + nst,),
        in_specs=[pl.BlockSpec((tile, d), lambda i: (jnp.minimum(i, npt - 1), 0)),
                  pl.BlockSpec((tile, d), lambda i: (jnp.maximum(i - npt, 0), 0)),
                  pl.BlockSpec((1, 6, d), mod_map),
                  pl.BlockSpec((1, d), lambda i: (0, 0)),
                  pl.BlockSpec((d, n), lambda i: (0, 0))],
        out_specs=pl.BlockSpec((tile, n), lambda i: (i, 0)),
        out_shape=jax.ShapeDtypeStruct((tp + ts, n), F32),
        compiler_params=_params(1),
        name="inproj0",
    )(xp, xs, mods, norm_w, w_bf16)


def _scan_kernel(*refs, seq_len, has_state):
    if has_state:
        (p_ref, a2_ref, ab_ref, lb_ref, ong_ref, onh_ref, s0_ref, mixed_ref, *scratch) = refs
        sfin_ref = None
    else:
        (p_ref, a2_ref, ab_ref, lb_ref, ong_ref, onh_ref, mixed_ref, sg_ref, sh_ref,
         *scratch) = refs
        s0_ref = None
        sfin_ref = (sg_ref, sh_ref)
    (qi_f, ki_f, qo_f, ko_f, qi_b, ki_b, qo_b, ko_b,
     vv, dec_f, dec_b, o_f, o_b, st_f, st_b) = scratch
    C = SCAN_CHUNK
    n_chunks = seq_len // C
    gqk = GLA_HEADS * HEAD_DK

    row = lax.broadcasted_iota(I32, (C, C), 0)
    col = lax.broadcasted_iota(I32, (C, C), 1)
    lower = col <= row
    upper = col >= row
    tri_lo = jnp.where(lower, 1.0, 0.0).astype(BF16)
    tri_up = jnp.where(upper, 1.0, 0.0).astype(BF16)

    lbp = lb_ref[...]
    lb_max = jnp.maximum(lbp[0], lbp[1])
    lb_e0 = jnp.exp(lbp[0] - lb_max)
    lb_e1 = jnp.exp(lbp[1] - lb_max)
    lb = lb_e0 / (lb_e0 + lb_e1)

    def cumsum_chunk(tri, la):
        hi, lo = _split_bf16(la)
        return _dot(tri, hi) + _dot(tri, lo)

    def prep(n, carry):
        r0 = pl.multiple_of(n * C, C)
        rows = pl.ds(r0, C)
        gq = p_ref[rows, _C_GQ:_C_GQ + gqk] * (HEAD_DK ** -0.5)
        gk = p_ref[rows, _C_GK:_C_GK + gqk]
        hq = _silu(p_ref[rows, _C_HQ:_C_HQ + gqk]) * (HEAD_DK ** -0.5)
        for d_i, (qi_s, ki_s, qo_s, ko_s, dec_s, tri, last, mid) in enumerate(
                ((qi_f, ki_f, qo_f, ko_f, dec_f, tri_lo, C - 1, C // 2 - 1),
                 (qi_b, ki_b, qo_b, ko_b, dec_b, tri_up, 0, C // 2))):
            c_ga = _C_GAF if d_i == 0 else _C_GAB
            c_hf = _C_HFF if d_i == 0 else _C_HFB
            ga = p_ref[rows, c_ga:c_ga + GATE_RANK]
            xg = _dot(ga.astype(BF16), a2_ref[d_i].astype(BF16)) + ab_ref[d_i]
            la_g = _log_sigmoid(xg) / GLA_GATE_NORM
            f = lb[d_i:d_i + 1, :] + (1.0 - lb[d_i:d_i + 1, :]) * jax.nn.sigmoid(
                p_ref[rows, c_hf:c_hf + gqk])
            la_h = jnp.log(f)
            for q, k, la, c0 in ((gq, gk, la_g, 0), (hq, 1.0 - f, la_h, gqk)):
                b = cumsum_chunk(tri, la)
                b_mid, b_end = b[mid:mid + 1, :], b[last:last + 1, :]
                cs = slice(c0, c0 + gqk)
                qi_s[rows, cs] = (q * jnp.exp(b - b_mid)).astype(BF16)
                ki_s[rows, cs] = (k * jnp.exp(b_mid - b)).astype(BF16)
                qo_s[rows, cs] = (q * jnp.exp(b)).astype(BF16)
                ko_s[rows, cs] = (k * jnp.exp(b_end - b)).astype(BF16)
                dec_s[n, :, cs] = jnp.exp(b_end)
        gv_cols = GLA_HEADS * HEAD_DV
        vv[rows, 0:gv_cols] = p_ref[rows, _C_GV:_C_GV + gv_cols].astype(BF16)
        vv[rows, gv_cols:] = p_ref[rows, _C_HI:_C_HI + HGRN_HEADS * HEAD_DV].astype(BF16)
        return carry

    lax.fori_loop(0, n_chunks, prep, 0, unroll=8)

    for p in range(SCAN_PAIRS):
        if has_state:
            st_f[p] = s0_ref[0, 0, p]
            st_b[p] = s0_ref[0, 1, p]
        else:
            st_f[p] = jnp.zeros((2 * HEAD_DV, 2 * HEAD_DK), F32)
            st_b[p] = jnp.zeros((2 * HEAD_DV, 2 * HEAD_DK), F32)

    first_head = lax.broadcasted_iota(I32, (C, 2 * HEAD_DK), 1) < HEAD_DK
    row2 = lax.broadcasted_iota(I32, (2 * C, C), 0) % C
    col2 = lax.broadcasted_iota(I32, (2 * C, C), 1)
    lower2 = col2 <= row2
    upper2 = col2 >= row2

    def per_head_rows(x):
        z = jnp.zeros_like(x)
        return jnp.concatenate([jnp.where(first_head, x, z), jnp.where(first_head, z, x)], axis=0)

    def put_out(o_ref, rows, p, res):
        c0 = p * 2 * HEAD_DV
        o_ref[rows, c0:c0 + HEAD_DV] = res[0:C, 0:HEAD_DV]
        o_ref[rows, c0 + HEAD_DV:c0 + 2 * HEAD_DV] = res[C:2 * C, HEAD_DV:2 * HEAD_DV]

    def sweep(n, carry):
        m = n_chunks - 1 - n
        rows = pl.ds(pl.multiple_of(n * C, C), C)
        rows_m = pl.ds(pl.multiple_of(m * C, C), C)
        decay_f, decay_b = dec_f[n], dec_b[m]
        for p in range(SCAN_PAIRS):
            ks = slice(p * 2 * HEAD_DK, (p + 1) * 2 * HEAD_DK)
            vs = slice(p * 2 * HEAD_DV, (p + 1) * 2 * HEAD_DV)
            vh = vv[rows, vs]
            s_f = st_f[p]
            sc = (jnp.where(lower2, _dot_nt(per_head_rows(qi_f[rows, ks]), ki_f[rows, ks]), 0.0)
                  + jnp.where(upper2, _dot_nt(per_head_rows(qi_b[rows, ks]), ki_b[rows, ks]), 0.0))
            put_out(o_f, rows, p, _dot_nt(per_head_rows(qo_f[rows, ks]), s_f.astype(BF16))
                    + _dot(sc.astype(BF16), vh))
            st_f[p] = decay_f[:, ks] * s_f + _dot_tn(vh, ko_f[rows, ks])
            s_b = st_b[p]
            put_out(o_b, rows_m, p, _dot_nt(per_head_rows(qo_b[rows_m, ks]), s_b.astype(BF16)))
            st_b[p] = decay_b[:, ks] * s_b + _dot_tn(vv[rows_m, vs], ko_b[rows_m, ks])
        return carry

    lax.fori_loop(0, n_chunks, sweep, 0, unroll=8)

    def finish(n, carry):
        rows = pl.ds(pl.multiple_of(n * C, C), C)
        for h in range(SCAN_HEADS):
            vs = slice(h * HEAD_DV, (h + 1) * HEAD_DV)
            if h < GLA_HEADS:
                gate = p_ref[rows, _C_GG + h * HEAD_DV:_C_GG + (h + 1) * HEAD_DV]
                onw = ong_ref[...]
            else:
                hh = h - GLA_HEADS
                gate = p_ref[rows, _C_HG + hh * HEAD_DV:_C_HG + (hh + 1) * HEAD_DV]
                onw = onh_ref[...]
            o = o_f[rows, vs] + o_b[rows, vs]
            mixed_ref[rows, vs] = ((_rms(o) * onw) * _silu(gate)).astype(BF16)
        return carry

    lax.fori_loop(0, n_chunks, finish, 0, unroll=4)

    if sfin_ref is not None:
        for d_i, st in enumerate((st_f, st_b)):
            for p in range(SCAN_PAIRS):
                s_pair = st[p].T
                out_ref = sfin_ref[(2 * p) // GLA_HEADS]
                h0 = (2 * p) % GLA_HEADS
                out_ref[0, 0, d_i, h0] = s_pair[0:HEAD_DK, 0:HEAD_DV]
                out_ref[0, 0, d_i, h0 + 1] = s_pair[HEAD_DK:2 * HEAD_DK, HEAD_DV:2 * HEAD_DV]


def _scan(p, row0, batch, seq_len, a2, a_bias, lb, onorm_g, onorm_h, s0=None):
    n = p.shape[1]
    assert row0 % seq_len == 0
    blk0 = row0 // seq_len
    has_state = s0 is not None
    n_chunks = seq_len // SCAN_CHUNK
    assert GLA_HEADS == HGRN_HEADS and GLA_HEADS % 2 == 0
    st_shape = (1, 1, 2, GLA_HEADS, HEAD_DK, HEAD_DV)
    pair_shape = (SCAN_PAIRS, 2 * HEAD_DV, 2 * HEAD_DK)
    p_mode = dict(pipeline_mode=pl.Buffered(1)) if seq_len * n * 4 > SCAN_INPUT_DOUBLE_BUFFER_BYTES else {}
    in_specs = [pl.BlockSpec((seq_len, n), lambda b: (blk0 + b, 0), **p_mode),
                pl.BlockSpec(a2.shape, lambda b: (0, 0, 0)),
                pl.BlockSpec(a_bias.shape, lambda b: (0, 0, 0)),
                pl.BlockSpec(lb.shape, lambda b: (0, 0, 0)),
                pl.BlockSpec((1, HEAD_DV), lambda b: (0, 0)),
                pl.BlockSpec((1, HEAD_DV), lambda b: (0, 0))]
    args = [p, a2, a_bias, lb, onorm_g, onorm_h]
    mixed_shape = jax.ShapeDtypeStruct((batch * seq_len, D_MODEL), BF16)
    mixed_spec = pl.BlockSpec((seq_len, D_MODEL), lambda b: (b, 0))
    if has_state:
        in_specs.append(pl.BlockSpec((1, 2) + pair_shape, lambda b: (b, 0, 0, 0, 0)))
        args.append(s0)
        out_shape, out_specs = mixed_shape, mixed_spec
    else:
        st_struct = jax.ShapeDtypeStruct((batch,) + st_shape[1:], F32)
        st_spec = pl.BlockSpec(st_shape, lambda b: (b, 0, 0, 0, 0, 0))
        out_shape = (mixed_shape, st_struct, st_struct)
        out_specs = (mixed_spec, st_spec, st_spec)
    qk_cols = SCAN_HEADS * HEAD_DK
    scratch = [pltpu.VMEM((seq_len, qk_cols), BF16) for _ in range(8)]
    scratch += [pltpu.VMEM((seq_len, D_MODEL), BF16),
                pltpu.VMEM((n_chunks, 1, qk_cols), F32), pltpu.VMEM((n_chunks, 1, qk_cols), F32),
                pltpu.VMEM((seq_len, D_MODEL), F32), pltpu.VMEM((seq_len, D_MODEL), F32),
                pltpu.VMEM(pair_shape, F32), pltpu.VMEM(pair_shape, F32)]
    return pl.pallas_call(
        functools.partial(_scan_kernel, seq_len=seq_len, has_state=has_state),
        grid=(batch,),
        in_specs=in_specs, out_specs=out_specs, out_shape=out_shape,
        scratch_shapes=scratch,
        compiler_params=_params(1),
        name="scan_state" if has_state else "scan_fresh",
    )(*args)


def _post_kernel(*refs, split_x, n_prompt_tiles):
    if split_x:
        xp_ref, xs_ref = refs[0], refs[1]
        refs = refs[2:]
    else:
        x_ref = refs[0]
        refs = refs[1:]
    (mp_ref, ms_ref, mod_ref, nw_ref, wo_ref, wrh_ref, wrl_ref,
     x1_ref, h2_ref, slot_ref, wgt_ref, tab_ref, carry, earlier) = refs
    i = pl.program_id(0)
    is_prompt = i < n_prompt_tiles
    if split_x:
        x = jnp.where(is_prompt, xp_ref[...], xs_ref[...])
    else:
        x = x_ref[...]
    mixed = jnp.where(is_prompt, mp_ref[...], ms_ref[...])
    x1 = x + mod_ref[0, 2:3, :] * _dot(mixed, wo_ref[...])
    x1_ref[...] = x1
    h2 = _modulate(x1, nw_ref[...], mod_ref[0, 3:4, :], mod_ref[0, 4:5, :])
    _to_token_major(h2_ref, h2)

    hh, hl = _split_bf16(h2)
    logits = _dot(hh, wrh_ref[...]) + _dot(hl, wrh_ref[...]) + _dot(hh, wrl_ref[...])
    tm = logits.shape[0]
    lt = logits.T[0:ROUTER_ROWS]
    ridx = lax.broadcasted_iota(I32, (ROUTER_ROWS, tm), 0).astype(F32)

    def first_max(v):
        mx = jnp.max(v, axis=0, keepdims=True)
        idx = jnp.min(jnp.where(v == mx, ridx, float(ROUTER_ROWS)), axis=0, keepdims=True)
        return mx, idx

    gl = jnp.where(ridx < N_GROUPS, lt, NEG_BIG)
    gmax, gidx = first_max(gl)
    g_val = 1.0 / jnp.sum(jnp.exp(gl - gmax), axis=0, keepdims=True)
    lo = N_GROUPS + EXPERTS_PER_GROUP * gidx
    el = jnp.where((ridx >= lo) & (ridx < lo + EXPERTS_PER_GROUP), lt, NEG_BIG)
    emax, l1 = first_max(el)
    esum = jnp.sum(jnp.exp(el - emax), axis=0, keepdims=True)
    e2max, l2 = first_max(jnp.where(ridx == l1, NEG_BIG, el))
    p1 = 1.0 / esum
    p2 = jnp.exp(e2max - emax) / esum
    w1 = g_val * (p1 / (p1 + p2))
    w2 = g_val * (p2 / (p1 + p2))

    @pl.when(i == 0)
    def _():
        carry[...] = jnp.zeros_like(carry)
        t_row = lax.broadcasted_iota(I32, earlier.shape, 0)
        t_col = lax.broadcasted_iota(I32, earlier.shape, 1)
        earlier[...] = jnp.where(t_row < t_col, 1.0, 0.0).astype(BF16)

    sel1 = ridx == l1
    sel2 = ridx == l2
    onehot = jnp.where(sel1 | sel2, 1.0, 0.0)
    before = _dot(onehot.astype(BF16), earlier[...])
    count = jnp.sum(onehot, axis=1, keepdims=True)
    blocks = jnp.floor((count + (MOVE_BLOCK - 1.0)) * (1.0 / MOVE_BLOCK)) * MOVE_BLOCK
    r_row = lax.broadcasted_iota(I32, (ROUTER_ROWS, ROUTER_ROWS), 0)
    r_col = lax.broadcasted_iota(I32, (ROUTER_ROWS, ROUTER_ROWS), 1)
    lower_rows = jnp.where(r_col < r_row, 1.0, 0.0).astype(BF16)
    run_start = _dot(lower_rows,
                     jnp.broadcast_to(blocks, (ROUTER_ROWS, LANES)).astype(BF16))[:, 0:1]
    slot = before + run_start
    q1 = jnp.sum(jnp.where(sel1, slot, 0.0), axis=0, keepdims=True)
    q2 = jnp.sum(jnp.where(sel2, slot, 0.0), axis=0, keepdims=True)
    tab_lane = lax.broadcasted_iota(I32, (ROUTER_ROWS, LANES), 1)
    tab_ref[0] = jnp.where(tab_lane == 0, count,
                           jnp.where(tab_lane == 1, carry[...],
                                     jnp.where(tab_lane == 2, run_start, 0.0)))
    carry[...] = carry[...] + count

    slot_ref[0, 0:1, :] = (q1 * TOKEN_ROWS).astype(I32)
    slot_ref[0, 1:2, :] = (q2 * TOKEN_ROWS).astype(I32)
    wgt_ref[0, 0:1, :] = w1
    wgt_ref[0, 1:2, :] = w2


def _post(x_args, mixed_p, mixed_s, mods, layer, norm_w, w_out_bf16, wr_hi, wr_lo,
          tiles_per_sample):
    split_x = len(x_args) == 2
    tp, ts = mixed_p.shape[0], mixed_s.shape[0]
    t, d = tp + ts, D_MODEL
    npt, nst = tp // ROW_TILE, ts // ROW_TILE
    tile = lambda i: (i, 0)
    if split_x:
        x_specs = [pl.BlockSpec((ROW_TILE, d), lambda i: (jnp.minimum(i, npt - 1), 0)),
                   pl.BlockSpec((ROW_TILE, d), lambda i: (jnp.maximum(i - npt, 0), 0))]
    else:
        x_specs = [pl.BlockSpec((ROW_TILE, d), tile)]
    in_specs = x_specs + [
        pl.BlockSpec((ROW_TILE, d), lambda i: (jnp.minimum(i, npt - 1), 0)),
        pl.BlockSpec((ROW_TILE, d), lambda i: (jnp.maximum(i - npt, 0), 0)),
        pl.BlockSpec((1, 6, d), lambda i: (_mod_row(i, layer, npt, tiles_per_sample), 0, 0)),
        pl.BlockSpec((1, d), lambda i: (0, 0)),
        pl.BlockSpec((d, d), lambda i: (0, 0)),
        pl.BlockSpec((d, LANES), lambda i: (0, 0)),
        pl.BlockSpec((d, LANES), lambda i: (0, 0))]
    return pl.pallas_call(
        functools.partial(_post_kernel, split_x=split_x, n_prompt_tiles=npt),
        grid=(npt + nst,),
        in_specs=in_specs,
        out_specs=(pl.BlockSpec((ROW_TILE, d), tile),
                   pl.BlockSpec((ROW_TILE * TOKEN_ROWS, LANES), tile),
                   pl.BlockSpec((1, 2, ROW_TILE), lambda i: (i, 0, 0)),
                   pl.BlockSpec((1, 2, ROW_TILE), lambda i: (i, 0, 0)),
                   pl.BlockSpec((1, ROUTER_ROWS, LANES), lambda i: (i, 0, 0))),
        out_shape=(jax.ShapeDtypeStruct((t, d), F32),
                   jax.ShapeDtypeStruct((t * TOKEN_ROWS, LANES), F32),
                   jax.ShapeDtypeStruct((npt + nst, 2, ROW_TILE), I32),
                   jax.ShapeDtypeStruct((npt + nst, 2, ROW_TILE), F32),
                   jax.ShapeDtypeStruct((npt + nst, ROUTER_ROWS, LANES), F32)),
        scratch_shapes=[pltpu.VMEM((ROUTER_ROWS, LANES), F32),
                        pltpu.VMEM((ROW_TILE, ROW_TILE), BF16)],
        compiler_params=_params(1),
        name=f"post{layer}",
    )(*x_args, mixed_p, mixed_s, mods, norm_w, w_out_bf16, wr_hi, wr_lo)


def _for_blocks(tab_ref, fn):
    block_rows = MOVE_BLOCK * TOKEN_ROWS
    count = tab_ref[0, 0, MAX_BLOCKS]

    def call(k, parity):
        fn(pl.multiple_of(k * block_rows, block_rows),
           pl.multiple_of(tab_ref[0, 0, k], TOKEN_ROWS), parity)

    def body(k2, c):
        call(2 * k2, 0)

        @pl.when(2 * k2 + 1 < count)
        def _():
            call(2 * k2 + 1, 1)
        return c

    lax.fori_loop(0, _cdiv(count, 2), body, 0)


def _wait_blocks(tab_ref, copy):
    def body(k, c):
        copy.wait()
        return c

    lax.fori_loop(0, tab_ref[0, 0, MAX_BLOCKS], body, 0)


def _dispatch_kernel(zero_ref, tab_ref, prev_tab_ref, q_ref, h2_ref, hs_ref, zero_buf, stage, sem):
    j = pl.program_id(0)
    slot = j % 2
    block_rows = MOVE_BLOCK * TOKEN_ROWS

    @pl.when(j == 0)
    def _():
        zero_buf[...] = jnp.zeros_like(zero_buf)

        def zero_copy(k):
            start = pl.multiple_of(zero_ref[k], EXPERT_TILE * TOKEN_ROWS)
            return pltpu.make_async_copy(
                zero_buf, hs_ref.at[pl.ds(start, EXPERT_TILE * TOKEN_ROWS)], sem.at[0])

        def start_zero(k2, c):
            for parity in range(2):
                @pl.when(zero_ref[2 * k2 + parity] >= 0)
                def _():
                    zero_copy(2 * k2 + parity).start(priority=parity)
            return c

        def wait_zero(k, c):
            @pl.when(zero_ref[k] >= 0)
            def _():
                zero_copy(k).wait()
            return c

        lax.fori_loop(0, zero_ref.shape[0] // 2, start_zero, 0)
        stage[...] = jnp.zeros_like(stage)
        lax.fori_loop(0, zero_ref.shape[0], wait_zero, 0)

    def place(r, c):
        tok = h2_ref[pl.ds(pl.multiple_of(r * TOKEN_ROWS, TOKEN_ROWS), TOKEN_ROWS), :]
        for s in range(2):
            row = pl.multiple_of(q_ref[0, s, r], TOKEN_ROWS)
            stage[slot, pl.ds(row, TOKEN_ROWS), :] = tok
        return c

    lax.fori_loop(0, ROW_TILE, place, 0, unroll=8)

    def block_copy(buf, stage_row, sorted_row):
        return pltpu.make_async_copy(stage.at[buf, pl.ds(stage_row, block_rows)],
                                     hs_ref.at[pl.ds(sorted_row, block_rows)], sem.at[buf])

    @pl.when(j > 0)
    def _():
        _wait_blocks(prev_tab_ref, block_copy(1 - slot, 0, 0))

    _for_blocks(tab_ref, lambda a, b, parity: block_copy(slot, a, b).start(priority=parity))

    @pl.when(j == pl.num_programs(0) - 1)
    def _():
        _wait_blocks(tab_ref, block_copy(slot, 0, 0))


def _dispatch(zero_tiles, block_tab, slots, h2, n_rows):
    t = h2.shape[0] // TOKEN_ROWS
    nt = t // ROW_TILE
    assert zero_tiles.shape[0] % 2 == 0
    smem_tile = lambda shape: pl.BlockSpec((1,) + shape, lambda j, *_: (j, 0, 0),
                                           memory_space=pltpu.SMEM)
    grid_spec = pltpu.PrefetchScalarGridSpec(
        num_scalar_prefetch=1,
        grid=(nt,),
        in_specs=[smem_tile((1, LANES)),
                  pl.BlockSpec((1, 1, LANES), lambda j, *_: (jnp.maximum(j - 1, 0), 0, 0),
                               memory_space=pltpu.SMEM),
                  smem_tile((2, ROW_TILE)),
                  pl.BlockSpec((ROW_TILE * TOKEN_ROWS, LANES), lambda j, *_: (j, 0))],
        out_specs=pl.BlockSpec(memory_space=pl.ANY),
        scratch_shapes=[pltpu.VMEM((EXPERT_TILE * TOKEN_ROWS, LANES), F32),
                        pltpu.VMEM((2, STAGE_TOKENS * TOKEN_ROWS, LANES), F32),
                        pltpu.SemaphoreType.DMA((2,))])
    return pl.pallas_call(
        _dispatch_kernel,
        grid_spec=grid_spec,
        out_shape=jax.ShapeDtypeStruct((n_rows * TOKEN_ROWS, LANES), F32),
        compiler_params=_params(1),
        name="dispatch",
    )(zero_tiles, block_tab, block_tab, slots, h2)


def _expert_kernel(te_ref, src_ref, nv_ref, run_ref, nxt_ref, hs_ref, w1_hbm, w3_hbm, w2_hbm,
                   ys_ref, w1f, w3f, w2f, w1b, w3b, w2b, sem, *, layer):
    i = pl.program_id(0)

    def weight_copies(e, buf):
        return [pltpu.make_async_copy(src.at[layer, e], dst.at[buf], sem.at[buf])
                for src, dst in ((w1_hbm, w1f), (w3_hbm, w3f), (w2_hbm, w2f))]

    @pl.when(i == 0)
    def _():
        for c in weight_copies(te_ref[0], 0):
            c.start()

    first = (i == 0) | (run_ref[i] != run_ref[jnp.maximum(i - 1, 0)])

    @pl.when(first)
    def _():
        buf = run_ref[i] % 2
        for c in weight_copies(te_ref[i], buf):
            c.wait()

        @pl.when(nxt_ref[i] >= 0)
        def _():
            for c in weight_copies(nxt_ref[i], 1 - buf):
                c.start()

        w1b[...] = w1f[buf].astype(BF16)
        w3b[...] = w3f[buf].astype(BF16)
        w2b[...] = w2f[buf].astype(BF16)

    @pl.when(nv_ref[i] > 0)
    def _():
        h = _from_token_major(hs_ref, EXPERT_TILE).astype(BF16)
        g = _silu(_dot(h, w1b[...])) * _dot(h, w3b[...])
        _to_token_major(ys_ref, _dot(g.astype(BF16), w2b[...]))

    @pl.when(nv_ref[i] == 0)
    def _():
        ys_ref[...] = jnp.zeros_like(ys_ref)


def _experts(tile_expert, tile_src, tile_rows, hs, w1, w3, w2, layer):
    n_rows, d = hs.shape[0] // TOKEN_ROWS, D_MODEL
    nt = n_rows // EXPERT_TILE
    hid = w1.shape[-1]
    tok_tile = (EXPERT_TILE * TOKEN_ROWS, LANES)
    changed = jnp.concatenate([jnp.zeros((1,), I32),
                               (tile_expert[1:] != tile_expert[:-1]).astype(I32)])
    run = jnp.cumsum(changed).astype(I32)
    later = jnp.where(run[None, :] > run[:, None], tile_expert[None, :], N_EXPERTS)
    next_expert = jnp.min(later, axis=1)
    next_expert = jnp.where(next_expert < N_EXPERTS, next_expert, -1).astype(I32)
    grid_spec = pltpu.PrefetchScalarGridSpec(
        num_scalar_prefetch=5,
        grid=(nt,),
        in_specs=[pl.BlockSpec(tok_tile, lambda i, te, src, nv, run, nxt: (src[i], 0)),
                  pl.BlockSpec(memory_space=pl.ANY), pl.BlockSpec(memory_space=pl.ANY),
                  pl.BlockSpec(memory_space=pl.ANY)],
        out_specs=pl.BlockSpec(tok_tile, lambda i, te, src, nv, run, nxt: (i, 0)),
        scratch_shapes=[pltpu.VMEM((2, d, hid), F32), pltpu.VMEM((2, d, hid), F32),
                        pltpu.VMEM((2, hid, d), F32),
                        pltpu.VMEM((d, hid), BF16), pltpu.VMEM((d, hid), BF16),
                        pltpu.VMEM((hid, d), BF16), pltpu.SemaphoreType.DMA((2,))])
    return pl.pallas_call(
        functools.partial(_expert_kernel, layer=layer),
        grid_spec=grid_spec,
        out_shape=jax.ShapeDtypeStruct(hs.shape, F32),
        compiler_params=_params(1),
        name=f"experts{layer}",
    )(tile_expert, tile_src, tile_rows, run, next_expert, hs, w1, w3, w2)


def _combine_kernel(tab_ref, next_tab_ref, q_ref, w_ref, x1_ref, mod_ref, fw_ref, ys_ref, out_ref,
                    stage, y_tok, sem, *, final_norm):
    i = pl.program_id(0)
    slot = i % 2
    block_rows = MOVE_BLOCK * TOKEN_ROWS

    def block_copy(buf, stage_row, sorted_row):
        return pltpu.make_async_copy(ys_ref.at[pl.ds(sorted_row, block_rows)],
                                     stage.at[buf, pl.ds(stage_row, block_rows)], sem.at[buf])

    def fetch(tab, buf):
        _for_blocks(tab, lambda a, b, parity: block_copy(buf, a, b).start(priority=parity))

    @pl.when(i == 0)
    def _():
        fetch(tab_ref, slot)

    @pl.when(i + 1 < pl.num_programs(0))
    def _():
        fetch(next_tab_ref, 1 - slot)

    _wait_blocks(tab_ref, block_copy(slot, 0, 0))

    def pick(r, c):
        rows = [stage[slot, pl.ds(pl.multiple_of(q_ref[0, s, r], TOKEN_ROWS), TOKEN_ROWS), :]
                for s in range(2)]
        y_tok[pl.ds(pl.multiple_of(r * TOKEN_ROWS, TOKEN_ROWS), TOKEN_ROWS), :] = (
            w_ref[0, 0, r] * rows[0] + w_ref[0, 1, r] * rows[1])
        return c

    lax.fori_loop(0, ROW_TILE, pick, 0, unroll=8)
    x2 = x1_ref[...] + mod_ref[0, 5:6, :] * _from_token_major(y_tok, ROW_TILE)
    if final_norm:
        x2 = _rms(x2) * fw_ref[...]
    out_ref[...] = x2


def _combine(block_tab, slots, weights, x1, mods, layer, final_w, ys, tile0, n_tiles,
             n_prompt_tiles, tiles_per_sample, final_norm):
    d = D_MODEL
    tile = lambda i: (tile0 + i, 0)
    mod_map = lambda i: (_mod_row(tile0 + i, layer, n_prompt_tiles, tiles_per_sample), 0, 0)
    smem_tile = lambda shape: pl.BlockSpec((1,) + shape, lambda i: (tile0 + i, 0, 0),
                                           memory_space=pltpu.SMEM)
    return pl.pallas_call(
        functools.partial(_combine_kernel, final_norm=final_norm),
        grid=(n_tiles,),
        in_specs=[smem_tile((1, LANES)),
                  pl.BlockSpec((1, 1, LANES),
                               lambda i: (tile0 + jnp.minimum(i + 1, n_tiles - 1), 0, 0),
                               memory_space=pltpu.SMEM),
                  smem_tile((2, ROW_TILE)), smem_tile((2, ROW_TILE)),
                  pl.BlockSpec((ROW_TILE, d), tile),
                  pl.BlockSpec((1, 6, d), mod_map),
                  pl.BlockSpec((1, d), lambda i: (0, 0)),
                  pl.BlockSpec(memory_space=pl.ANY)],
        out_specs=pl.BlockSpec((ROW_TILE, d), lambda i: (i, 0)),
        out_shape=jax.ShapeDtypeStruct((n_tiles * ROW_TILE, d), F32),
        scratch_shapes=[pltpu.VMEM((2, STAGE_TOKENS * TOKEN_ROWS, LANES), F32),
                        pltpu.VMEM((ROW_TILE * TOKEN_ROWS, LANES), F32),
                        pltpu.SemaphoreType.DMA((2,))],
        compiler_params=_params(1),
        name=f"combine{layer}_{tile0}",
    )(block_tab, block_tab, slots, weights, x1, mods, final_w, ys)


def _moe(h2, slots, weights, tile_tab, w1, w3, w2, layer):
    t = h2.shape[0] // TOKEN_ROWS
    n_tiles = t // ROW_TILE
    extra_tiles = N_EXPERTS + _cdiv(N_EXPERTS * MOVE_BLOCK, EXPERT_TILE)
    n_rows = 2 * t + extra_tiles * EXPERT_TILE
    nt = n_rows // EXPERT_TILE
    tab = tile_tab[:, N_GROUPS:N_GROUPS + N_EXPERTS, 0:3].transpose(0, 2, 1).astype(I32)
    cnt = tab[-1, 0] + tab[-1, 1]
    tight = _cdiv(cnt, EXPERT_TILE) * EXPERT_TILE
    padded = jnp.where(cnt > 0, _cdiv(cnt + MOVE_BLOCK - 1, EXPERT_TILE) * EXPERT_TILE, 0)
    ends = jnp.cumsum(padded)
    offsets = ends - padded
    tails = jnp.where(cnt > 0, ends - EXPERT_TILE, -1)
    tails2 = jnp.where(padded > tight, ends - 2 * EXPERT_TILE, -1)
    used = ends[-1] // EXPERT_TILE
    tile_start = jnp.arange(nt, dtype=I32) * EXPERT_TILE
    unused = (used + jnp.arange(extra_tiles, dtype=I32)) * EXPERT_TILE
    zero_tiles = jnp.concatenate([tails, tails2, jnp.where(unused < n_rows, unused, -1)])
    zero_tiles = jnp.where(zero_tiles >= 0, zero_tiles * TOKEN_ROWS, -1).astype(I32)
    tile_src = jnp.minimum(jnp.arange(nt, dtype=I32), used - 1)
    tile_expert = jnp.sum((tile_src * EXPERT_TILE)[:, None] >= ends[None, :], axis=1).astype(I32)
    tile_rows = jnp.where(tile_start < ends[-1],
                          jnp.clip(cnt[tile_expert] - (tile_start - offsets[tile_expert]),
                                   0, EXPERT_TILE), 0).astype(I32)
    n_blocks = _cdiv(tab[:, 0], MOVE_BLOCK)
    blocks_through = jnp.cumsum(n_blocks, axis=1)
    k = jnp.arange(MAX_BLOCKS, dtype=I32)
    owner = jnp.sum(blocks_through[:, None, :] <= k[None, :, None], axis=2)
    is_owner = owner[:, :, None] == jnp.arange(N_EXPERTS, dtype=I32)[None, None, :]
    pick = lambda v: jnp.sum(jnp.where(is_owner, v[:, None, :], 0), axis=2)
    run_first = pick(offsets[None, :] + tab[:, 1])
    block_in_run = k[None, :] - pick(blocks_through - n_blocks)
    sorted_row = (run_first + block_in_run * MOVE_BLOCK) * TOKEN_ROWS
    block_tab = jnp.concatenate(
        [sorted_row, blocks_through[:, -1:],
         jnp.zeros((n_tiles, LANES - MAX_BLOCKS - 1), I32)], axis=1).astype(I32)[:, None, :]
    hs = _dispatch(zero_tiles, block_tab, slots, h2, n_rows)
    ys = _experts(tile_expert, tile_src, tile_rows, hs, w1, w3, w2, layer)
    return ys, (block_tab, slots, weights)


def _rope(x, cos, sin_signed):
    lane = lax.broadcasted_iota(I32, (x.shape[0], LANES), 1)
    low = (lane % 32) < 16
    outs = []
    for j in range(x.shape[1] // LANES):
        xb = x[:, j * LANES:(j + 1) * LANES]
        partner = jnp.where(low, pltpu.roll(xb, LANES - 16, 1), pltpu.roll(xb, 16, 1))
        outs.append(xb * cos + partner * sin_signed)
    return jnp.concatenate(outs, axis=1)


def _inproj1_prompt_kernel(x_ref, mod_ref, nw_ref, w_ref, q_ref, k_ref, v_ref, kc_ref, vc_ref):
    d = D_MODEL
    h = _modulate(x_ref[...], nw_ref[...], mod_ref[0, 0:1, :], mod_ref[0, 1:2, :]).astype(BF16)
    q_ref[...] = (_dot(h, w_ref[:, 0:d]) * (DIFF_HD ** -0.5)).astype(BF16)
    k = _dot(h, w_ref[:, d:2 * d])
    v = _dot(h, w_ref[:, 2 * d:3 * d])
    k_ref[...] = k.astype(BF16)
    v_ref[...] = v.astype(BF16)
    for b in range(k.shape[0] // ATTN_TILE):
        kc_ref[b * d:(b + 1) * d, :] = k[b * ATTN_TILE:(b + 1) * ATTN_TILE].T
    _to_token_major(vc_ref, v)


def _inproj1_sample_kernel(x_ref, mod_ref, nw_ref, w_ref, cos_ref, sin_ref, q_ref, k_ref, v_ref):
    d = D_MODEL
    h = _modulate(x_ref[...], nw_ref[...], mod_ref[0, 0:1, :], mod_ref[0, 1:2, :]).astype(BF16)
    cos, sin = cos_ref[...], sin_ref[...]
    q_ref[...] = (_rope(_dot(h, w_ref[:, 0:d]), cos, sin) * (DIFF_HD ** -0.5)).astype(BF16)
    k_ref[...] = _rope(_dot(h, w_ref[:, d:2 * d]), cos, sin).astype(BF16)
    v_ref[...] = _dot(h, w_ref[:, 2 * d:3 * d]).astype(BF16)


def _inproj1(x, mods, norm_w, w_bf16, n_prompt_tiles, n_sample_tiles, tiles_per_sample,
             cos_t, sin_t):
    d = D_MODEL
    rows = INPROJ1_TILE
    npt, nst = n_prompt_tiles, n_sample_tiles
    common = [pl.BlockSpec((1, d), lambda i: (0, 0)), pl.BlockSpec((d, 3 * d), lambda i: (0, 0))]
    tile = lambda i: (i, 0)
    out_specs = tuple(pl.BlockSpec((rows, d), tile) for _ in range(3))
    qp, kp, vp, k_cache, v_cache = pl.pallas_call(
        _inproj1_prompt_kernel,
        grid=(npt,),
        in_specs=[pl.BlockSpec((rows, d), tile),
                  pl.BlockSpec((1, 6, d), lambda i: (8, 0, 0))] + common,
        out_specs=out_specs + (pl.BlockSpec((rows // ATTN_TILE * d, ATTN_TILE), tile),
                               pl.BlockSpec((rows * TOKEN_ROWS, LANES), tile)),
        out_shape=tuple(jax.ShapeDtypeStruct((npt * rows, d), BF16) for _ in range(3))
        + (jax.ShapeDtypeStruct((npt * rows // ATTN_TILE * d, ATTN_TILE), F32),
           jax.ShapeDtypeStruct((npt * rows * TOKEN_ROWS, LANES), F32)),
        compiler_params=_params(1),
        name="inproj1_prompt",
    )(x, mods, norm_w, w_bf16)
    rope_tile = lambda i: (i % tiles_per_sample, 0)
    qs, ks, vs = pl.pallas_call(
        _inproj1_sample_kernel,
        grid=(nst,),
        in_specs=[pl.BlockSpec((rows, d), lambda i: (npt + i, 0)),
                  pl.BlockSpec((1, 6, d), lambda i: (8 + 1 + i // tiles_per_sample, 0, 0))]
        + common + [pl.BlockSpec((rows, LANES), rope_tile),
                    pl.BlockSpec((rows, LANES), rope_tile)],
        out_specs=out_specs,
        out_shape=tuple(jax.ShapeDtypeStruct((nst * rows, d), BF16) for _ in range(3)),
        compiler_params=_params(1),
        name="inproj1_sample",
    )(x, mods, norm_w, w_bf16, cos_t, sin_t)
    return (qp, kp, vp), (qs, ks, vs), (k_cache, v_cache)


def _rope_tables(n_tok):
    half = DIFF_HD // 4
    pos = np.arange(n_tok)
    lane = np.arange(LANES)
    sub = lane % DIFF_HD
    p = np.where(sub[None, :] < DIFF_HD // 2, (pos // GRID_W)[:, None], (pos % GRID_W)[:, None])
    inv = jnp.asarray(ROPE_THETA, F32) ** (-jnp.asarray(sub % half, F32) / half)
    ang = jnp.asarray(p, F32) * inv[None, :]
    sign = np.where((lane % (2 * half)) < half, -1.0, 1.0).astype(np.float32)
    return jnp.cos(ang), jnp.sin(ang) * sign[None, :]


def _diffattn_kernel(*refs, has_cache, lam_init, seqs):
    if has_cache:
        q_ref, k_ref, v_ref, ck_ref, cv_ref, lam_ref, sw_ref, o_ref = refs
    else:
        q_ref, k_ref, v_ref, lam_ref, sw_ref, o_ref = refs
    hd2 = 2 * DIFF_HD
    lv = lam_ref[...]
    lam = (jnp.exp(jnp.sum(lv[0:1] * lv[1:2], axis=1, keepdims=True))
           - jnp.exp(jnp.sum(lv[2:3] * lv[3:4], axis=1, keepdims=True)) + lam_init)
    nq, nk = q_ref.shape[0] // seqs, k_ref.shape[0] // seqs
    lane = lax.broadcasted_iota(I32, (nq, hd2), 1)
    for sq, h in ((sq, h) for sq in range(seqs) for h in range(DIFF_HEADS)):
        cols = slice(h * hd2, (h + 1) * hd2)
        q_rows, k_rows = slice(sq * nq, (sq + 1) * nq), slice(sq * nk, (sq + 1) * nk)
        q = q_ref[q_rows, cols]
        zero = jnp.zeros_like(q)
        k_new = k_ref[k_rows, cols].astype(BF16)
        values = [v_ref[k_rows, cols].astype(BF16)]
        if has_cache:
            past = ck_ref.shape[1]
            k_past_t = ck_ref[cols, :].astype(BF16)
            values.append(cv_ref[pl.ds(h, past, stride=DIFF_HEADS), :].astype(BF16))
        o = None
        for c in range(2):
            qc = jnp.where((lane < DIFF_HD) == (c == 0), q, zero)
            s = [_dot_nt(qc, k_new)]
            if has_cache:
                s.append(_dot(qc, k_past_t))
            mx = functools.reduce(jnp.maximum, [jnp.max(si, axis=1, keepdims=True) for si in s])
            e = [jnp.exp(si - mx) for si in s]
            z = functools.reduce(jnp.add, [jnp.sum(ei, axis=1, keepdims=True) for ei in e])
            pv = functools.reduce(jnp.add, [_dot(ei.astype(BF16), v) for ei, v in zip(e, values)])
            pv = pv * (1.0 / z)
            o = pv if c == 0 else o - lam * pv
        o_ref[q_rows, cols] = ((_rms(o) * sw_ref[...]) * (1.0 - lam_init)).astype(BF16)


def _diffattn(q, k, v, lam_vecs, subln_w, batch, seq_len, q_block, lam_init, cache=None, seqs=1):
    d = D_MODEL
    nq = seq_len // q_block
    has_cache = cache is not None
    assert seqs == 1 or (nq == 1 and not has_cache and batch % seqs == 0)
    kv_spec = pl.BlockSpec((seqs * seq_len, d), lambda b, qi: (b, 0))
    in_specs = [pl.BlockSpec((seqs * q_block, d), lambda b, qi: (b * nq + qi, 0)), kv_spec, kv_spec]
    args = [q, k, v]
    if has_cache:
        past = cache[0].shape[1]
        in_specs += [pl.BlockSpec((d, past), lambda b, qi: (b, 0)),
                     pl.BlockSpec((past * DIFF_HEADS, 2 * DIFF_HD), lambda b, qi: (b, 0))]
        args += list(cache)
    in_specs += [pl.BlockSpec((4, DIFF_HD), lambda b, qi: (0, 0)),
                 pl.BlockSpec((1, 2 * DIFF_HD), lambda b, qi: (0, 0))]
    args += [lam_vecs, subln_w]
    return pl.pallas_call(
        functools.partial(_diffattn_kernel, has_cache=has_cache, lam_init=lam_init, seqs=seqs),
        grid=(batch // seqs, nq),
        in_specs=in_specs,
        out_specs=pl.BlockSpec((seqs * q_block, d), lambda b, qi: (b * nq + qi, 0)),
        out_shape=jax.ShapeDtypeStruct((batch * seq_len, d), BF16),
        compiler_params=_params(2),
        name="diffattn_cache" if has_cache else "diffattn",
    )(*args)


def _router_weights(router_group, router_expert):
    w = jnp.concatenate([router_group, router_expert], axis=1)
    w = jnp.pad(w, ((0, 0), (0, LANES - w.shape[1])))
    hi = w.astype(BF16)
    return hi, (w - hi.astype(F32)).astype(BF16)


def _inproj0_weights(w_in):
    gq, gk, gv, gg, gaf, gab, hq, hff, hfb, hi, hg = jnp.split(
        w_in, [256, 512, 1024, 1536, 1552, 1568, 1824, 2080, 2336, 2848], axis=1)
    w = jnp.concatenate([gq, gk, gv, gg, hq, hff, hfb, hi, hg, gaf, gab], axis=1)
    return jnp.pad(w, ((0, 0), (0, AB_COLS - w.shape[1]))).astype(BF16)


def kernel(x_prompt, x_sample, state_gla, state_hgrn, cache_diff_k, cache_diff_v, c, c_ctx,
           w_ada, b_ada, norm1_w, norm2_w, w_in_ab, gla_a2, gla_a_bias, hgrn_lb, gla_onorm_w,
           hgrn_onorm_w, w_out_ab, w_in_c, lam_q1, lam_k1, lam_q2, lam_k2, diff_subln_w, w_out_c,
           router_group, router_expert, moe_w1, moe_w3, moe_w2, final_norm_w):
    bp, lp, d = x_prompt.shape
    bs, ls, _ = x_sample.shape
    depth = w_ada.shape[0]
    assert depth == 2 and d == D_MODEL and bs <= 7
    tp, ts = bp * lp, bs * ls
    npt, nst = tp // ROW_TILE, ts // ROW_TILE
    tps = ls // ROW_TILE
    xp = x_prompt.reshape(tp, d)
    xs = x_sample.reshape(ts, d)

    cond8 = jnp.concatenate([c_ctx[None, :], c, jnp.zeros((7 - bs, d), F32)], axis=0)
    mods = _adaln(cond8, w_ada, b_ada).reshape(depth * 8, 6, d)

    proj = _inproj0(xp, xs, mods, norm1_w[0:1], _inproj0_weights(w_in_ab[0]), ls)
    a_bias = gla_a_bias[0][:, None, :]
    scan_args = (gla_a2[0], a_bias, hgrn_lb, gla_onorm_w[0:1], hgrn_onorm_w[0:1])
    mixed_p, new_state_gla, new_state_hgrn = _scan(proj, 0, bp, lp, *scan_args)
    s0 = jnp.concatenate([state_gla[:, 0], state_hgrn[:, 0]], axis=2).swapaxes(-1, -2)
    s0 = s0.reshape(bs, 2, SCAN_PAIRS, 2, HEAD_DV, HEAD_DK)
    zero = jnp.zeros_like(s0[:, :, :, 0])
    s0 = jnp.concatenate([jnp.concatenate([s0[:, :, :, 0], zero], axis=-1),
                          jnp.concatenate([zero, s0[:, :, :, 1]], axis=-1)], axis=-2)
    mixed_s = _scan(proj, tp, bs, ls, *scan_args, s0=s0)

    wr = _router_weights(router_group[0], router_expert[0])
    x1, *routed = _post((xp, xs), mixed_p, mixed_s, mods, 0, norm2_w[0:1],
                        w_out_ab[0].astype(BF16), *wr, tps)
    ys, tables = _moe(*routed, moe_w1, moe_w3, moe_w2, 0)
    x2 = _combine(*tables, x1, mods, 0, final_norm_w[None, :], ys, 0, npt + nst, npt, tps, False)

    lam_init = 0.8 - 0.6 * math.exp(-0.3 * 1)
    cos_t, sin_t = _rope_tables(ls)
    (qp, kp, vp), (qs, ks, vs), (k_cache, v_cache) = _inproj1(
        x2, mods, norm1_w[1:2], w_in_c[0].astype(BF16), tp // INPROJ1_TILE, ts // INPROJ1_TILE,
        ls // INPROJ1_TILE, cos_t, sin_t)
    lam_vecs = jnp.stack([lam_q1[0], lam_k1[0], lam_q2[0], lam_k2[0]])
    att_p = _diffattn(qp, kp, vp, lam_vecs, diff_subln_w[0:1], bp, lp, lp, lam_init,
                      seqs=PROMPT_SEQS_PER_STEP)
    past = cache_diff_k.shape[2]
    assert lp == ATTN_TILE and DIFF_HEADS == TOKEN_ROWS
    cache = (cache_diff_k[:, 0].transpose(0, 2, 3, 4, 1).reshape(bs * d, past),
             cache_diff_v[:, 0].reshape(bs * past * DIFF_HEADS, 2 * DIFF_HD))
    att_s = _diffattn(qs, ks, vs, lam_vecs, diff_subln_w[0:1], bs, ls, SAMPLE_Q_BLOCK, lam_init,
                      cache)

    wr = _router_weights(router_group[1], router_expert[1])
    x3, *routed = _post((x2,), att_p, att_s, mods, 1, norm2_w[1:2],
                        w_out_c[0].astype(BF16), *wr, tps)
    ys, tables = _moe(*routed, moe_w1, moe_w3, moe_w2, 1)
    fw = final_norm_w[None, :]
    y_p = _combine(*tables, x3, mods, 1, fw, ys, 0, npt, npt, tps, True)
    y_s = _combine(*tables, x3, mods, 1, fw, ys, npt, nst, npt, tps, True)

    return (y_p.reshape(bp, lp, d), y_s.reshape(bs, ls, d), new_state_gla, new_state_hgrn,
            k_cache.reshape(bp, 1, DIFF_HEADS, 2, DIFF_HD, lp).transpose(0, 1, 5, 2, 3, 4),
            v_cache.reshape(bp, 1, lp, DIFF_HEADS, 2 * DIFF_HD))
```

```python
import functools
import math

import jax
import jax.numpy as jnp
import numpy as np
from jax import lax
from jax.experimental import pallas as pl
from jax.experimental.pallas import tpu as pltpu

F32 = jnp.float32
BF16 = jnp.bfloat16
I32 = jnp.int32

D_MODEL = 1024
GLA_HEADS = 4
HGRN_HEADS = 4
SCAN_HEADS = GLA_HEADS + HGRN_HEADS
SCAN_PAIRS = SCAN_HEADS // 2
HEAD_DK = 64
HEAD_DV = 128
GATE_RANK = 16
GLA_GATE_NORM = 16.0
DIFF_HEADS = 8
DIFF_HD = 64
GRID_W = 64
ROPE_THETA = 10000.0
N_GROUPS = 4
EXPERTS_PER_GROUP = 8
N_EXPERTS = N_GROUPS * EXPERTS_PER_GROUP
MOE_HIDDEN = 512
EPS = 1e-6
LANES = 128
TOKEN_ROWS = D_MODEL // LANES
NEG_BIG = -1e30
ROUTER_ROWS = 48

ROW_TILE = 512
ATTN_TILE = 256
SAMPLE_Q_BLOCK = 512
PROMPT_SEQS_PER_STEP = 2
ADA_TILE = 1536
INPROJ0_TILE = 512
INPROJ1_TILE = 512
SCAN_CHUNK = 64
EXPERT_TILE = 256
MOVE_BLOCK = 16
STAGE_TOKENS = 2 * ROW_TILE + N_EXPERTS * MOVE_BLOCK
MAX_BLOCKS = STAGE_TOKENS // MOVE_BLOCK
VMEM_LIMIT = 56 * 1024 * 1024
SCAN_INPUT_DOUBLE_BUFFER_BYTES = 16 * 1024 * 1024

_C_GQ, _C_GK, _C_GV, _C_GG = 0, 256, 512, 1024
_C_HQ, _C_HFF, _C_HFB, _C_HI, _C_HG = 1536, 1792, 2048, 2304, 2816
_C_GAF, _C_GAB = 3328, 3344
AB_COLS = 3456


def _params(n_axes, vmem=VMEM_LIMIT):
    return pltpu.CompilerParams(dimension_semantics=("arbitrary",) * n_axes,
                                vmem_limit_bytes=vmem)


def _cdiv(a, b):
    return (a + b - 1) // b


def _dot(a, b):
    return jnp.dot(a, b, preferred_element_type=F32)


def _dot_nt(a, b):
    return lax.dot_general(a, b, (((1,), (1,)), ((), ())), preferred_element_type=F32)


def _dot_tn(a, b):
    return lax.dot_general(a, b, (((0,), (0,)), ((), ())), preferred_element_type=F32)


def _split_bf16(x):
    hi = x.astype(BF16)
    lo = (x - hi.astype(F32)).astype(BF16)
    return hi, lo


def _silu(x):
    return x * jax.nn.sigmoid(x)


def _log_sigmoid(x):
    return jnp.minimum(x, 0.0) - jnp.log(1.0 + jnp.exp(-jnp.abs(x)))


def _rms(x):
    return x * lax.rsqrt(jnp.mean(x * x, axis=-1, keepdims=True) + EPS)


def _modulate(x, norm_w, shift, scale):
    return (_rms(x) * norm_w) * (1.0 + scale) + shift


def _to_token_major(dst_ref, x, row0=0):
    n = x.shape[0]
    for s in range(TOKEN_ROWS):
        dst_ref[pl.ds(row0 + s, n, stride=TOKEN_ROWS), :] = x[:, s * LANES:(s + 1) * LANES]


def _from_token_major(src_ref, n, row0=0):
    return jnp.concatenate([src_ref[pl.ds(row0 + s, n, stride=TOKEN_ROWS), :]
                            for s in range(TOKEN_ROWS)], axis=1)


def _ada_kernel(c_ref, w_ref, b_ref, o_ref):
    s = _silu(c_ref[...])
    o_ref[0] = _dot(s.astype(BF16), w_ref[0].astype(BF16)) + b_ref[0]


def _adaln(cond8, w_ada, b_ada):
    depth, d, n = w_ada.shape
    tn = ADA_TILE
    return pl.pallas_call(
        _ada_kernel,
        grid=(depth, n // tn),
        in_specs=[pl.BlockSpec((8, d), lambda l, j: (0, 0)),
                  pl.BlockSpec((1, d, tn), lambda l, j: (l, 0, j)),
                  pl.BlockSpec((1, 1, tn), lambda l, j: (l, 0, j))],
        out_specs=pl.BlockSpec((1, 8, tn), lambda l, j: (l, 0, j)),
        out_shape=jax.ShapeDtypeStruct((depth, 8, n), F32),
        compiler_params=_params(2),
        name="adaln",
    )(cond8, w_ada, b_ada.reshape(depth, 1, n))


def _mod_row(i, layer, n_prompt_tiles, tiles_per_sample):
    r = jnp.where(i < n_prompt_tiles, 0, 1 + (i - n_prompt_tiles) // tiles_per_sample)
    return layer * 8 + r


def _inproj0_kernel(xp_ref, xs_ref, mod_ref, nw_ref, w_ref, o_ref, *, n_prompt_tiles):
    i = pl.program_id(0)
    x = jnp.where(i < n_prompt_tiles, xp_ref[...], xs_ref[...])
    h = _modulate(x, nw_ref[...], mod_ref[0, 0:1, :], mod_ref[0, 1:2, :])
    o_ref[...] = _dot(h.astype(BF16), w_ref[...])


def _inproj0(xp, xs, mods, norm_w, w_bf16, sample_len):
    tp, d = xp.shape
    ts = xs.shape[0]
    n = w_bf16.shape[1]
    tile = INPROJ0_TILE
    npt, nst = tp // tile, ts // tile
    mod_map = lambda i: (_mod_row(i, 0, npt, sample_len // tile), 0, 0)
    return pl.pallas_call(
        functools.partial(_inproj0_kernel, n_prompt_tiles=npt),
        grid=(npt + nst,),
        in_specs=[pl.BlockSpec((tile, d), lambda i: (jnp.minimum(i, npt - 1), 0)),
                  pl.BlockSpec((tile, d), lambda i: (jnp.maximum(i - npt, 0), 0)),
                  pl.BlockSpec((1, 6, d), mod_map),
                  pl.BlockSpec((1, d), lambda i: (0, 0)),
                  pl.BlockSpec((d, n), lambda i: (0, 0))],
        out_specs=pl.BlockSpec((tile, n), lambda i: (i, 0)),
        out_shape=jax.ShapeDtypeStruct((tp + ts, n), F32),
        compiler_params=_params(1),
        name="inproj0",
    )(xp, xs, mods, norm_w, w_bf16)


def _scan_kernel(*refs, seq_len, has_state):
    if has_state:
        (p_ref, a2_ref, ab_ref, lb_ref, ong_ref, onh_ref, s0_ref, mixed_ref, *scratch) = refs
        sfin_ref = None
    else:
        (p_ref, a2_ref, ab_ref, lb_ref, ong_ref, onh_ref, mixed_ref, sg_ref, sh_ref,
         *scratch) = refs
        s0_ref = None
        sfin_ref = (sg_ref, sh_ref)
    (qi_f, ki_f, qo_f, ko_f, qi_b, ki_b, qo_b, ko_b,
     vv, dec_f, dec_b, o_f, o_b, st_f, st_b) = scratch
    C = SCAN_CHUNK
    n_chunks = seq_len // C
    gqk = GLA_HEADS * HEAD_DK

    row = lax.broadcasted_iota(I32, (C, C), 0)
    col = lax.broadcasted_iota(I32, (C, C), 1)
    lower = col <= row
    upper = col >= row
    tri_lo = jnp.where(lower, 1.0, 0.0).astype(BF16)
    tri_up = jnp.where(upper, 1.0, 0.0).astype(BF16)

    lbp = lb_ref[...]
    lb_max = jnp.maximum(lbp[0], lbp[1])
    lb_e0 = jnp.exp(lbp[0] - lb_max)
    lb_e1 = jnp.exp(lbp[1] - lb_max)
    lb = lb_e0 / (lb_e0 + lb_e1)

    def cumsum_chunk(tri, la):
        hi, lo = _split_bf16(la)
        return _dot(tri, hi) + _dot(tri, lo)

    def prep(n, carry):
        r0 = pl.multiple_of(n * C, C)
        rows = pl.ds(r0, C)
        gq = p_ref[rows, _C_GQ:_C_GQ + gqk] * (HEAD_DK ** -0.5)
        gk = p_ref[rows, _C_GK:_C_GK + gqk]
        hq = _silu(p_ref[rows, _C_HQ:_C_HQ + gqk]) * (HEAD_DK ** -0.5)
        for d_i, (qi_s, ki_s, qo_s, ko_s, dec_s, tri, last, mid) in enumerate(
                ((qi_f, ki_f, qo_f, ko_f, dec_f, tri_lo, C - 1, C // 2 - 1),
                 (qi_b, ki_b, qo_b, ko_b, dec_b, tri_up, 0, C // 2))):
            c_ga = _C_GAF if d_i == 0 else _C_GAB
            c_hf = _C_HFF if d_i == 0 else _C_HFB
            ga = p_ref[rows, c_ga:c_ga + GATE_RANK]
            xg = _dot(ga.astype(BF16), a2_ref[d_i].astype(BF16)) + ab_ref[d_i]
            la_g = _log_sigmoid(xg) / GLA_GATE_NORM
            f = lb[d_i:d_i + 1, :] + (1.0 - lb[d_i:d_i + 1, :]) * jax.nn.sigmoid(
                p_ref[rows, c_hf:c_hf + gqk])
            la_h = jnp.log(f)
            for q, k, la, c0 in ((gq, gk, la_g, 0), (hq, 1.0 - f, la_h, gqk)):
                b = cumsum_chunk(tri, la)
                b_mid, b_end = b[mid:mid + 1, :], b[last:last + 1, :]
                cs = slice(c0, c0 + gqk)
                qi_s[rows, cs] = (q * jnp.exp(b - b_mid)).astype(BF16)
                ki_s[rows, cs] = (k * jnp.exp(b_mid - b)).astype(BF16)
                qo_s[rows, cs] = (q * jnp.exp(b)).astype(BF16)
                ko_s[rows, cs] = (k * jnp.exp(b_end - b)).astype(BF16)
                dec_s[n, :, cs] = jnp.exp(b_end)
        gv_cols = GLA_HEADS * HEAD_DV
        vv[rows, 0:gv_cols] = p_ref[rows, _C_GV:_C_GV + gv_cols].astype(BF16)
        vv[rows, gv_cols:] = p_ref[rows, _C_HI:_C_HI + HGRN_HEADS * HEAD_DV].astype(BF16)
        return carry

    lax.fori_loop(0, n_chunks, prep, 0, unroll=8)

    for p in range(SCAN_PAIRS):
        if has_state:
            st_f[p] = s0_ref[0, 0, p]
            st_b[p] = s0_ref[0, 1, p]
        else:
            st_f[p] = jnp.zeros((2 * HEAD_DV, 2 * HEAD_DK), F32)
            st_b[p] = jnp.zeros((2 * HEAD_DV, 2 * HEAD_DK), F32)

    first_head = lax.broadcasted_iota(I32, (C, 2 * HEAD_DK), 1) < HEAD_DK
    row2 = lax.broadcasted_iota(I32, (2 * C, C), 0) % C
    col2 = lax.broadcasted_iota(I32, (2 * C, C), 1)
    lower2 = col2 <= row2
    upper2 = col2 >= row2

    def per_head_rows(x):
        z = jnp.zeros_like(x)
        return jnp.concatenate([jnp.where(first_head, x, z), jnp.where(first_head, z, x)], axis=0)

    def put_out(o_ref, rows, p, res):
        c0 = p * 2 * HEAD_DV
        o_ref[rows, c0:c0 + HEAD_DV] = res[0:C, 0:HEAD_DV]
        o_ref[rows, c0 + HEAD_DV:c0 + 2 * HEAD_DV] = res[C:2 * C, HEAD_DV:2 * HEAD_DV]

    def sweep(n, carry):
        m = n_chunks - 1 - n
        rows = pl.ds(pl.multiple_of(n * C, C), C)
        rows_m = pl.ds(pl.multiple_of(m * C, C), C)
        decay_f, decay_b = dec_f[n], dec_b[m]
        for p in range(SCAN_PAIRS):
            ks = slice(p * 2 * HEAD_DK, (p + 1) * 2 * HEAD_DK)
            vs = slice(p * 2 * HEAD_DV, (p + 1) * 2 * HEAD_DV)
            vh = vv[rows, vs]
            s_f = st_f[p]
            sc = (jnp.where(lower2, _dot_nt(per_head_rows(qi_f[rows, ks]), ki_f[rows, ks]), 0.0)
                  + jnp.where(upper2, _dot_nt(per_head_rows(qi_b[rows, ks]), ki_b[rows, ks]), 0.0))
            put_out(o_f, rows, p, _dot_nt(per_head_rows(qo_f[rows, ks]), s_f.astype(BF16))
                    + _dot(sc.astype(BF16), vh))
            st_f[p] = decay_f[:, ks] * s_f + _dot_tn(vh, ko_f[rows, ks])
            s_b = st_b[p]
            put_out(o_b, rows_m, p, _dot_nt(per_head_rows(qo_b[rows_m, ks]), s_b.astype(BF16)))
            st_b[p] = decay_b[:, ks] * s_b + _dot_tn(vv[rows_m, vs], ko_b[rows_m, ks])
        return carry

    lax.fori_loop(0, n_chunks, sweep, 0, unroll=8)

    def finish(n, carry):
        rows = pl.ds(pl.multiple_of(n * C, C), C)
        for h in range(SCAN_HEADS):
            vs = slice(h * HEAD_DV, (h + 1) * HEAD_DV)
            if h < GLA_HEADS:
                gate = p_ref[rows, _C_GG + h * HEAD_DV:_C_GG + (h + 1) * HEAD_DV]
                onw = ong_ref[...]
            else:
                hh = h - GLA_HEADS
                gate = p_ref[rows, _C_HG + hh * HEAD_DV:_C_HG + (hh + 1) * HEAD_DV]
                onw = onh_ref[...]
            o = o_f[rows, vs] + o_b[rows, vs]
            mixed_ref[rows, vs] = ((_rms(o) * onw) * _silu(gate)).astype(BF16)
        return carry

    lax.fori_loop(0, n_chunks, finish, 0, unroll=4)

    if sfin_ref is not None:
        for d_i, st in enumerate((st_f, st_b)):
            for p in range(SCAN_PAIRS):
                s_pair = st[p].T
                out_ref = sfin_ref[(2 * p) // GLA_HEADS]
                h0 = (2 * p) % GLA_HEADS
                out_ref[0, 0, d_i, h0] = s_pair[0:HEAD_DK, 0:HEAD_DV]
                out_ref[0, 0, d_i, h0 + 1] = s_pair[HEAD_DK:2 * HEAD_DK, HEAD_DV:2 * HEAD_DV]


def _scan(p, row0, batch, seq_len, a2, a_bias, lb, onorm_g, onorm_h, s0=None):
    n = p.shape[1]
    assert row0 % seq_len == 0
    blk0 = row0 // seq_len
    has_state = s0 is not None
    n_chunks = seq_len // SCAN_CHUNK
    assert GLA_HEADS == HGRN_HEADS and GLA_HEADS % 2 == 0
    st_shape = (1, 1, 2, GLA_HEADS, HEAD_DK, HEAD_DV)
    pair_shape = (SCAN_PAIRS, 2 * HEAD_DV, 2 * HEAD_DK)
    p_mode = dict(pipeline_mode=pl.Buffered(1)) if seq_len * n * 4 > SCAN_INPUT_DOUBLE_BUFFER_BYTES else {}
    in_specs = [pl.BlockSpec((seq_len, n), lambda b: (blk0 + b, 0), **p_mode),
                pl.BlockSpec(a2.shape, lambda b: (0, 0, 0)),
                pl.BlockSpec(a_bias.shape, lambda b: (0, 0, 0)),
                pl.BlockSpec(lb.shape, lambda b: (0, 0, 0)),
                pl.BlockSpec((1, HEAD_DV), lambda b: (0, 0)),
                pl.BlockSpec((1, HEAD_DV), lambda b: (0, 0))]
    args = [p, a2, a_bias, lb, onorm_g, onorm_h]
    mixed_shape = jax.ShapeDtypeStruct((batch * seq_len, D_MODEL), BF16)
    mixed_spec = pl.BlockSpec((seq_len, D_MODEL), lambda b: (b, 0))
    if has_state:
        in_specs.append(pl.BlockSpec((1, 2) + pair_shape, lambda b: (b, 0, 0, 0, 0)))
        args.append(s0)
        out_shape, out_specs = mixed_shape, mixed_spec
    else:
        st_struct = jax.ShapeDtypeStruct((batch,) + st_shape[1:], F32)
        st_spec = pl.BlockSpec(st_shape, lambda b: (b, 0, 0, 0, 0, 0))
        out_shape = (mixed_shape, st_struct, st_struct)
        out_specs = (mixed_spec, st_spec, st_spec)
    qk_cols = SCAN_HEADS * HEAD_DK
    scratch = [pltpu.VMEM((seq_len, qk_cols), BF16) for _ in range(8)]
    scratch += [pltpu.VMEM((seq_len, D_MODEL), BF16),
                pltpu.VMEM((n_chunks, 1, qk_cols), F32), pltpu.VMEM((n_chunks, 1, qk_cols), F32),
                pltpu.VMEM((seq_len, D_MODEL), F32), pltpu.VMEM((seq_len, D_MODEL), F32),
                pltpu.VMEM(pair_shape, F32), pltpu.VMEM(pair_shape, F32)]
    return pl.pallas_call(
        functools.partial(_scan_kernel, seq_len=seq_len, has_state=has_state),
        grid=(batch,),
        in_specs=in_specs, out_specs=out_specs, out_shape=out_shape,
        scratch_shapes=scratch,
        compiler_params=_params(1),
        name="scan_state" if has_state else "scan_fresh",
    )(*args)


def _post_kernel(*refs, split_x, n_prompt_tiles):
    if split_x:
        xp_ref, xs_ref = refs[0], refs[1]
        refs = refs[2:]
    else:
        x_ref = refs[0]
        refs = refs[1:]
    (mp_ref, ms_ref, mod_ref, nw_ref, wo_ref, wr_ref,
     x1_ref, h2_ref, slot_ref, wgt_ref, tab_ref, carry, earlier) = refs
    i = pl.program_id(0)
    is_prompt = i < n_prompt_tiles
    if split_x:
        x = jnp.where(is_prompt, xp_ref[...], xs_ref[...])
    else:
        x = x_ref[...]
    mixed = jnp.where(is_prompt, mp_ref[...], ms_ref[...])
    x1 = x + mod_ref[0, 2:3, :] * _dot(mixed, wo_ref[...])
    x1_ref[...] = x1
    h2 = _modulate(x1, nw_ref[...], mod_ref[0, 3:4, :], mod_ref[0, 4:5, :])
    _to_token_major(h2_ref, h2)

    logits = _dot(h2.astype(BF16), wr_ref[...])
    tm = logits.shape[0]
    lt = logits.T[0:ROUTER_ROWS]
    ridx = lax.broadcasted_iota(I32, (ROUTER_ROWS, tm), 0).astype(F32)

    def first_max(v):
        mx = jnp.max(v, axis=0, keepdims=True)
        idx = jnp.min(jnp.where(v == mx, ridx, float(ROUTER_ROWS)), axis=0, keepdims=True)
        return mx, idx

    gl = jnp.where(ridx < N_GROUPS, lt, NEG_BIG)
    gmax, gidx = first_max(gl)
    g_val = 1.0 / jnp.sum(jnp.exp(gl - gmax), axis=0, keepdims=True)
    lo = N_GROUPS + EXPERTS_PER_GROUP * gidx
    el = jnp.where((ridx >= lo) & (ridx < lo + EXPERTS_PER_GROUP), lt, NEG_BIG)
    emax, l1 = first_max(el)
    esum = jnp.sum(jnp.exp(el - emax), axis=0, keepdims=True)
    e2max, l2 = first_max(jnp.where(ridx == l1, NEG_BIG, el))
    p1 = 1.0 / esum
    p2 = jnp.exp(e2max - emax) / esum
    w1 = g_val * (p1 / (p1 + p2))
    w2 = g_val * (p2 / (p1 + p2))

    @pl.when(i == 0)
    def _():
        carry[...] = jnp.zeros_like(carry)
        t_row = lax.broadcasted_iota(I32, earlier.shape, 0)
        t_col = lax.broadcasted_iota(I32, earlier.shape, 1)
        earlier[...] = jnp.where(t_row < t_col, 1.0, 0.0).astype(BF16)

    sel1 = ridx == l1
    sel2 = ridx == l2
    onehot = jnp.where(sel1 | sel2, 1.0, 0.0)
    before = _dot(onehot.astype(BF16), earlier[...])
    count = jnp.sum(onehot, axis=1, keepdims=True)
    blocks = jnp.floor((count + (MOVE_BLOCK - 1.0)) * (1.0 / MOVE_BLOCK)) * MOVE_BLOCK
    r_row = lax.broadcasted_iota(I32, (ROUTER_ROWS, ROUTER_ROWS), 0)
    r_col = lax.broadcasted_iota(I32, (ROUTER_ROWS, ROUTER_ROWS), 1)
    lower_rows = jnp.where(r_col < r_row, 1.0, 0.0).astype(BF16)
    run_start = _dot(lower_rows,
                     jnp.broadcast_to(blocks, (ROUTER_ROWS, LANES)).astype(BF16))[:, 0:1]
    slot = before + run_start
    q1 = jnp.sum(jnp.where(sel1, slot, 0.0), axis=0, keepdims=True)
    q2 = jnp.sum(jnp.where(sel2, slot, 0.0), axis=0, keepdims=True)
    tab_lane = lax.broadcasted_iota(I32, (ROUTER_ROWS, LANES), 1)
    tab_ref[0] = jnp.where(tab_lane == 0, count,
                           jnp.where(tab_lane == 1, carry[...],
                                     jnp.where(tab_lane == 2, run_start, 0.0)))
    carry[...] = carry[...] + count

    slot_ref[0, 0:1, :] = (q1 * TOKEN_ROWS).astype(I32)
    slot_ref[0, 1:2, :] = (q2 * TOKEN_ROWS).astype(I32)
    wgt_ref[0, 0:1, :] = w1
    wgt_ref[0, 1:2, :] = w2


def _post(x_args, mixed_p, mixed_s, mods, layer, norm_w, w_out_bf16, w_router,
          tiles_per_sample):
    split_x = len(x_args) == 2
    tp, ts = mixed_p.shape[0], mixed_s.shape[0]
    t, d = tp + ts, D_MODEL
    npt, nst = tp // ROW_TILE, ts // ROW_TILE
    tile = lambda i: (i, 0)
    if split_x:
        x_specs = [pl.BlockSpec((ROW_TILE, d), lambda i: (jnp.minimum(i, npt - 1), 0)),
                   pl.BlockSpec((ROW_TILE, d), lambda i: (jnp.maximum(i - npt, 0), 0))]
    else:
        x_specs = [pl.BlockSpec((ROW_TILE, d), tile)]
    in_specs = x_specs + [
        pl.BlockSpec((ROW_TILE, d), lambda i: (jnp.minimum(i, npt - 1), 0)),
        pl.BlockSpec((ROW_TILE, d), lambda i: (jnp.maximum(i - npt, 0), 0)),
        pl.BlockSpec((1, 6, d), lambda i: (_mod_row(i, layer, npt, tiles_per_sample), 0, 0)),
        pl.BlockSpec((1, d), lambda i: (0, 0)),
        pl.BlockSpec((d, d), lambda i: (0, 0)),
        pl.BlockSpec((d, LANES), lambda i: (0, 0))]
    return pl.pallas_call(
        functools.partial(_post_kernel, split_x=split_x, n_prompt_tiles=npt),
        grid=(npt + nst,),
        in_specs=in_specs,
        out_specs=(pl.BlockSpec((ROW_TILE, d), tile),
                   pl.BlockSpec((ROW_TILE * TOKEN_ROWS, LANES), tile),
                   pl.BlockSpec((1, 2, ROW_TILE), lambda i: (i, 0, 0)),
                   pl.BlockSpec((1, 2, ROW_TILE), lambda i: (i, 0, 0)),
                   pl.BlockSpec((1, ROUTER_ROWS, LANES), lambda i: (i, 0, 0))),
        out_shape=(jax.ShapeDtypeStruct((t, d), F32),
                   jax.ShapeDtypeStruct((t * TOKEN_ROWS, LANES), F32),
                   jax.ShapeDtypeStruct((npt + nst, 2, ROW_TILE), I32),
                   jax.ShapeDtypeStruct((npt + nst, 2, ROW_TILE), F32),
                   jax.ShapeDtypeStruct((npt + nst, ROUTER_ROWS, LANES), F32)),
        scratch_shapes=[pltpu.VMEM((ROUTER_ROWS, LANES), F32),
                        pltpu.VMEM((ROW_TILE, ROW_TILE), BF16)],
        compiler_params=_params(1),
        name=f"post{layer}",
    )(*x_args, mixed_p, mixed_s, mods, norm_w, w_out_bf16, w_router)


def _for_blocks(tab_ref, fn):
    block_rows = MOVE_BLOCK * TOKEN_ROWS
    count = tab_ref[0, 0, MAX_BLOCKS]

    def call(k, parity):
        fn(pl.multiple_of(k * block_rows, block_rows),
           pl.multiple_of(tab_ref[0, 0, k], TOKEN_ROWS), parity)

    def body(k2, c):
        call(2 * k2, 0)

        @pl.when(2 * k2 + 1 < count)
        def _():
            call(2 * k2 + 1, 1)
        return c

    lax.fori_loop(0, _cdiv(count, 2), body, 0)


def _wait_blocks(tab_ref, copy):
    def body(k, c):
        copy.wait()
        return c

    lax.fori_loop(0, tab_ref[0, 0, MAX_BLOCKS], body, 0)


def _dispatch_kernel(zero_ref, tab_ref, prev_tab_ref, q_ref, h2_ref, hs_ref, zero_buf, stage, sem):
    j = pl.program_id(0)
    slot = j % 2
    block_rows = MOVE_BLOCK * TOKEN_ROWS

    @pl.when(j == 0)
    def _():
        zero_buf[...] = jnp.zeros_like(zero_buf)

        def zero_copy(k):
            start = pl.multiple_of(zero_ref[k], EXPERT_TILE * TOKEN_ROWS)
            return pltpu.make_async_copy(
                zero_buf, hs_ref.at[pl.ds(start, EXPERT_TILE * TOKEN_ROWS)], sem.at[0])

        def start_zero(k2, c):
            for parity in range(2):
                @pl.when(zero_ref[2 * k2 + parity] >= 0)
                def _():
                    zero_copy(2 * k2 + parity).start(priority=parity)
            return c

        def wait_zero(k, c):
            @pl.when(zero_ref[k] >= 0)
            def _():
                zero_copy(k).wait()
            return c

        lax.fori_loop(0, zero_ref.shape[0] // 2, start_zero, 0)
        stage[...] = jnp.zeros_like(stage)
        lax.fori_loop(0, zero_ref.shape[0], wait_zero, 0)

    def place(r, c):
        tok = h2_ref[pl.ds(pl.multiple_of(r * TOKEN_ROWS, TOKEN_ROWS), TOKEN_ROWS), :]
        for s in range(2):
            row = pl.multiple_of(q_ref[0, s, r], TOKEN_ROWS)
            stage[slot, pl.ds(row, TOKEN_ROWS), :] = tok
        return c

    lax.fori_loop(0, ROW_TILE, place, 0, unroll=8)

    def block_copy(buf, stage_row, sorted_row):
        return pltpu.make_async_copy(stage.at[buf, pl.ds(stage_row, block_rows)],
                                     hs_ref.at[pl.ds(sorted_row, block_rows)], sem.at[buf])

    @pl.when(j > 0)
    def _():
        _wait_blocks(prev_tab_ref, block_copy(1 - slot, 0, 0))

    _for_blocks(tab_ref, lambda a, b, parity: block_copy(slot, a, b).start(priority=parity))

    @pl.when(j == pl.num_programs(0) - 1)
    def _():
        _wait_blocks(tab_ref, block_copy(slot, 0, 0))


def _dispatch(zero_tiles, block_tab, slots, h2, n_rows):
    t = h2.shape[0] // TOKEN_ROWS
    nt = t // ROW_TILE
    assert zero_tiles.shape[0] % 2 == 0
    smem_tile = lambda shape: pl.BlockSpec((1,) + shape, lambda j, *_: (j, 0, 0),
                                           memory_space=pltpu.SMEM)
    grid_spec = pltpu.PrefetchScalarGridSpec(
        num_scalar_prefetch=1,
        grid=(nt,),
        in_specs=[smem_tile((1, LANES)),
                  pl.BlockSpec((1, 1, LANES), lambda j, *_: (jnp.maximum(j - 1, 0), 0, 0),
                               memory_space=pltpu.SMEM),
                  smem_tile((2, ROW_TILE)),
                  pl.BlockSpec((ROW_TILE * TOKEN_ROWS, LANES), lambda j, *_: (j, 0))],
        out_specs=pl.BlockSpec(memory_space=pl.ANY),
        scratch_shapes=[pltpu.VMEM((EXPERT_TILE * TOKEN_ROWS, LANES), F32),
                        pltpu.VMEM((2, STAGE_TOKENS * TOKEN_ROWS, LANES), F32),
                        pltpu.SemaphoreType.DMA((2,))])
    return pl.pallas_call(
        _dispatch_kernel,
        grid_spec=grid_spec,
        out_shape=jax.ShapeDtypeStruct((n_rows * TOKEN_ROWS, LANES), F32),
        compiler_params=_params(1),
        name="dispatch",
    )(zero_tiles, block_tab, block_tab, slots, h2)


def _expert_kernel(te_ref, src_ref, nv_ref, run_ref, nxt_ref, hs_ref, w1_hbm, w3_hbm, w2_hbm,
                   ys_ref, w1f, w3f, w2f, w1b, w3b, w2b, sem, *, layer):
    i = pl.program_id(0)

    def weight_copies(e, buf):
        return [pltpu.make_async_copy(src.at[layer, e], dst.at[buf], sem.at[buf])
                for src, dst in ((w1_hbm, w1f), (w3_hbm, w3f), (w2_hbm, w2f))]

    @pl.when(i == 0)
    def _():
        for c in weight_copies(te_ref[0], 0):
            c.start()

    first = (i == 0) | (run_ref[i] != run_ref[jnp.maximum(i - 1, 0)])

    @pl.when(first)
    def _():
        buf = run_ref[i] % 2
        for c in weight_copies(te_ref[i], buf):
            c.wait()

        @pl.when(nxt_ref[i] >= 0)
        def _():
            for c in weight_copies(nxt_ref[i], 1 - buf):
                c.start()

        w1b[...] = w1f[buf].astype(BF16)
        w3b[...] = w3f[buf].astype(BF16)
        w2b[...] = w2f[buf].astype(BF16)

    @pl.when(nv_ref[i] > 0)
    def _():
        h = _from_token_major(hs_ref, EXPERT_TILE).astype(BF16)
        g = _silu(_dot(h, w1b[...])) * _dot(h, w3b[...])
        _to_token_major(ys_ref, _dot(g.astype(BF16), w2b[...]))

    @pl.when(nv_ref[i] == 0)
    def _():
        ys_ref[...] = jnp.zeros_like(ys_ref)


def _experts(tile_expert, tile_src, tile_rows, hs, w1, w3, w2, layer):
    n_rows, d = hs.shape[0] // TOKEN_ROWS, D_MODEL
    nt = n_rows // EXPERT_TILE
    hid = w1.shape[-1]
    tok_tile = (EXPERT_TILE * TOKEN_ROWS, LANES)
    changed = jnp.concatenate([jnp.zeros((1,), I32),
                               (tile_expert[1:] != tile_expert[:-1]).astype(I32)])
    run = jnp.cumsum(changed).astype(I32)
    later = jnp.where(run[None, :] > run[:, None], tile_expert[None, :], N_EXPERTS)
    next_expert = jnp.min(later, axis=1)
    next_expert = jnp.where(next_expert < N_EXPERTS, next_expert, -1).astype(I32)
    grid_spec = pltpu.PrefetchScalarGridSpec(
        num_scalar_prefetch=5,
        grid=(nt,),
        in_specs=[pl.BlockSpec(tok_tile, lambda i, te, src, nv, run, nxt: (src[i], 0)),
                  pl.BlockSpec(memory_space=pl.ANY), pl.BlockSpec(memory_space=pl.ANY),
                  pl.BlockSpec(memory_space=pl.ANY)],
        out_specs=pl.BlockSpec(tok_tile, lambda i, te, src, nv, run, nxt: (i, 0)),
        scratch_shapes=[pltpu.VMEM((2, d, hid), F32), pltpu.VMEM((2, d, hid), F32),
                        pltpu.VMEM((2, hid, d), F32),
                        pltpu.VMEM((d, hid), BF16), pltpu.VMEM((d, hid), BF16),
                        pltpu.VMEM((hid, d), BF16), pltpu.SemaphoreType.DMA((2,))])
    return pl.pallas_call(
        functools.partial(_expert_kernel, layer=layer),
        grid_spec=grid_spec,
        out_shape=jax.ShapeDtypeStruct(hs.shape, F32),
        compiler_params=_params(1),
        name=f"experts{layer}",
    )(tile_expert, tile_src, tile_rows, run, next_expert, hs, w1, w3, w2)


def _combine_kernel(tab_ref, next_tab_ref, q_ref, w_ref, x1_ref, mod_ref, fw_ref, ys_ref, out_ref,
                    stage, y_tok, sem, *, final_norm):
    i = pl.program_id(0)
    slot = i % 2
    block_rows = MOVE_BLOCK * TOKEN_ROWS

    def block_copy(buf, stage_row, sorted_row):
        return pltpu.make_async_copy(ys_ref.at[pl.ds(sorted_row, block_rows)],
                                     stage.at[buf, pl.ds(stage_row, block_rows)], sem.at[buf])

    def fetch(tab, buf):
        _for_blocks(tab, lambda a, b, parity: block_copy(buf, a, b).start(priority=parity))

    @pl.when(i == 0)
    def _():
        fetch(tab_ref, slot)

    @pl.when(i + 1 < pl.num_programs(0))
    def _():
        fetch(next_tab_ref, 1 - slot)

    _wait_blocks(tab_ref, block_copy(slot, 0, 0))

    def pick(r, c):
        rows = [stage[slot, pl.ds(pl.multiple_of(q_ref[0, s, r], TOKEN_ROWS), TOKEN_ROWS), :]
                for s in range(2)]
        y_tok[pl.ds(pl.multiple_of(r * TOKEN_ROWS, TOKEN_ROWS), TOKEN_ROWS), :] = (
            w_ref[0, 0, r] * rows[0] + w_ref[0, 1, r] * rows[1])
        return c

    lax.fori_loop(0, ROW_TILE, pick, 0, unroll=8)
    x2 = x1_ref[...] + mod_ref[0, 5:6, :] * _from_token_major(y_tok, ROW_TILE)
    if final_norm:
        x2 = _rms(x2) * fw_ref[...]
    out_ref[...] = x2


def _combine(block_tab, slots, weights, x1, mods, layer, final_w, ys, tile0, n_tiles,
             n_prompt_tiles, tiles_per_sample, final_norm):
    d = D_MODEL
    tile = lambda i: (tile0 + i, 0)
    mod_map = lambda i: (_mod_row(tile0 + i, layer, n_prompt_tiles, tiles_per_sample), 0, 0)
    smem_tile = lambda shape: pl.BlockSpec((1,) + shape, lambda i: (tile0 + i, 0, 0),
                                           memory_space=pltpu.SMEM)
    return pl.pallas_call(
        functools.partial(_combine_kernel, final_norm=final_norm),
        grid=(n_tiles,),
        in_specs=[smem_tile((1, LANES)),
                  pl.BlockSpec((1, 1, LANES),
                               lambda i: (tile0 + jnp.minimum(i + 1, n_tiles - 1), 0, 0),
                               memory_space=pltpu.SMEM),
                  smem_tile((2, ROW_TILE)), smem_tile((2, ROW_TILE)),
                  pl.BlockSpec((ROW_TILE, d), tile),
                  pl.BlockSpec((1, 6, d), mod_map),
                  pl.BlockSpec((1, d), lambda i: (0, 0)),
                  pl.BlockSpec(memory_space=pl.ANY)],
        out_specs=pl.BlockSpec((ROW_TILE, d), lambda i: (i, 0)),
        out_shape=jax.ShapeDtypeStruct((n_tiles * ROW_TILE, d), F32),
        scratch_shapes=[pltpu.VMEM((2, STAGE_TOKENS * TOKEN_ROWS, LANES), F32),
                        pltpu.VMEM((ROW_TILE * TOKEN_ROWS, LANES), F32),
                        pltpu.SemaphoreType.DMA((2,))],
        compiler_params=_params(1),
        name=f"combine{layer}_{tile0}",
    )(block_tab, block_tab, slots, weights, x1, mods, final_w, ys)


def _moe(h2, slots, weights, tile_tab, w1, w3, w2, layer):
    t = h2.shape[0] // TOKEN_ROWS
    n_tiles = t // ROW_TILE
    extra_tiles = N_EXPERTS + _cdiv(N_EXPERTS * MOVE_BLOCK, EXPERT_TILE)
    n_rows = 2 * t + extra_tiles * EXPERT_TILE
    nt = n_rows // EXPERT_TILE
    tab = tile_tab[:, N_GROUPS:N_GROUPS + N_EXPERTS, 0:3].transpose(0, 2, 1).astype(I32)
    cnt = tab[-1, 0] + tab[-1, 1]
    tight = _cdiv(cnt, EXPERT_TILE) * EXPERT_TILE
    padded = jnp.where(cnt > 0, _cdiv(cnt + MOVE_BLOCK - 1, EXPERT_TILE) * EXPERT_TILE, 0)
    ends = jnp.cumsum(padded)
    offsets = ends - padded
    tails = jnp.where(cnt > 0, ends - EXPERT_TILE, -1)
    tails2 = jnp.where(padded > tight, ends - 2 * EXPERT_TILE, -1)
    used = ends[-1] // EXPERT_TILE
    tile_start = jnp.arange(nt, dtype=I32) * EXPERT_TILE
    unused = (used + jnp.arange(extra_tiles, dtype=I32)) * EXPERT_TILE
    zero_tiles = jnp.concatenate([tails, tails2, jnp.where(unused < n_rows, unused, -1)])
    zero_tiles = jnp.where(zero_tiles >= 0, zero_tiles * TOKEN_ROWS, -1).astype(I32)
    tile_src = jnp.minimum(jnp.arange(nt, dtype=I32), used - 1)
    tile_expert = jnp.sum((tile_src * EXPERT_TILE)[:, None] >= ends[None, :], axis=1).astype(I32)
    tile_rows = jnp.where(tile_start < ends[-1],
                          jnp.clip(cnt[tile_expert] - (tile_start - offsets[tile_expert]),
                                   0, EXPERT_TILE), 0).astype(I32)
    n_blocks = _cdiv(tab[:, 0], MOVE_BLOCK)
    blocks_through = jnp.cumsum(n_blocks, axis=1)
    k = jnp.arange(MAX_BLOCKS, dtype=I32)
    owner = jnp.sum(blocks_through[:, None, :] <= k[None, :, None], axis=2)
    is_owner = owner[:, :, None] == jnp.arange(N_EXPERTS, dtype=I32)[None, None, :]
    pick = lambda v: jnp.sum(jnp.where(is_owner, v[:, None, :], 0), axis=2)
    run_first = pick(offsets[None, :] + tab[:, 1])
    block_in_run = k[None, :] - pick(blocks_through - n_blocks)
    sorted_row = (run_first + block_in_run * MOVE_BLOCK) * TOKEN_ROWS
    block_tab = jnp.concatenate(
        [sorted_row, blocks_through[:, -1:],
         jnp.zeros((n_tiles, LANES - MAX_BLOCKS - 1), I32)], axis=1).astype(I32)[:, None, :]
    hs = _dispatch(zero_tiles, block_tab, slots, h2, n_rows)
    ys = _experts(tile_expert, tile_src, tile_rows, hs, w1, w3, w2, layer)
    return ys, (block_tab, slots, weights)


def _rope(x, cos, sin_signed):
    lane = lax.broadcasted_iota(I32, (x.shape[0], LANES), 1)
    low = (lane % 32) < 16
    outs = []
    for j in range(x.shape[1] // LANES):
        xb = x[:, j * LANES:(j + 1) * LANES]
        partner = jnp.where(low, pltpu.roll(xb, LANES - 16, 1), pltpu.roll(xb, 16, 1))
        outs.append(xb * cos + partner * sin_signed)
    return jnp.concatenate(outs, axis=1)


def _inproj1_prompt_kernel(x_ref, mod_ref, nw_ref, w_ref, q_ref, k_ref, v_ref, kc_ref, vc_ref):
    d = D_MODEL
    h = _modulate(x_ref[...], nw_ref[...], mod_ref[0, 0:1, :], mod_ref[0, 1:2, :]).astype(BF16)
    q_ref[...] = (_dot(h, w_ref[:, 0:d]) * (DIFF_HD ** -0.5)).astype(BF16)
    k = _dot(h, w_ref[:, d:2 * d])
    v = _dot(h, w_ref[:, 2 * d:3 * d])
    k_ref[...] = k.astype(BF16)
    v_ref[...] = v.astype(BF16)
    for b in range(k.shape[0] // ATTN_TILE):
        kc_ref[b * d:(b + 1) * d, :] = k[b * ATTN_TILE:(b + 1) * ATTN_TILE].T
    _to_token_major(vc_ref, v)


def _inproj1_sample_kernel(x_ref, mod_ref, nw_ref, w_ref, cos_ref, sin_ref, q_ref, k_ref, v_ref):
    d = D_MODEL
    h = _modulate(x_ref[...], nw_ref[...], mod_ref[0, 0:1, :], mod_ref[0, 1:2, :]).astype(BF16)
    cos, sin = cos_ref[...], sin_ref[...]
    q_ref[...] = (_rope(_dot(h, w_ref[:, 0:d]), cos, sin) * (DIFF_HD ** -0.5)).astype(BF16)
    k_ref[...] = _rope(_dot(h, w_ref[:, d:2 * d]), cos, sin).astype(BF16)
    v_ref[...] = _dot(h, w_ref[:, 2 * d:3 * d]).astype(BF16)


def _inproj1(x, mods, norm_w, w_bf16, n_prompt_tiles, n_sample_tiles, tiles_per_sample,
             cos_t, sin_t):
    d = D_MODEL
    rows = INPROJ1_TILE
    npt, nst = n_prompt_tiles, n_sample_tiles
    common = [pl.BlockSpec((1, d), lambda i: (0, 0)), pl.BlockSpec((d, 3 * d), lambda i: (0, 0))]
    tile = lambda i: (i, 0)
    out_specs = tuple(pl.BlockSpec((rows, d), tile) for _ in range(3))
    qp, kp, vp, k_cache, v_cache = pl.pallas_call(
        _inproj1_prompt_kernel,
        grid=(npt,),
        in_specs=[pl.BlockSpec((rows, d), tile),
                  pl.BlockSpec((1, 6, d), lambda i: (8, 0, 0))] + common,
        out_specs=out_specs + (pl.BlockSpec((rows // ATTN_TILE * d, ATTN_TILE), tile),
                               pl.BlockSpec((rows * TOKEN_ROWS, LANES), tile)),
        out_shape=tuple(jax.ShapeDtypeStruct((npt * rows, d), BF16) for _ in range(3))
        + (jax.ShapeDtypeStruct((npt * rows // ATTN_TILE * d, ATTN_TILE), F32),
           jax.ShapeDtypeStruct((npt * rows * TOKEN_ROWS, LANES), F32)),
        compiler_params=_params(1),
        name="inproj1_prompt",
    )(x, mods, norm_w, w_bf16)
    rope_tile = lambda i: (i % tiles_per_sample, 0)
    qs, ks, vs = pl.pallas_call(
        _inproj1_sample_kernel,
        grid=(nst,),
        in_specs=[pl.BlockSpec((rows, d), lambda i: (npt + i, 0)),
                  pl.BlockSpec((1, 6, d), lambda i: (8 + 1 + i // tiles_per_sample, 0, 0))]
        + common + [pl.BlockSpec((rows, LANES), rope_tile),
                    pl.BlockSpec((rows, LANES), rope_tile)],
        out_specs=out_specs,
        out_shape=tuple(jax.ShapeDtypeStruct((nst * rows, d), BF16) for _ in range(3)),
        compiler_params=_params(1),
        name="inproj1_sample",
    )(x, mods, norm_w, w_bf16, cos_t, sin_t)
    return (qp, kp, vp), (qs, ks, vs), (k_cache, v_cache)


def _rope_tables(n_tok):
    half = DIFF_HD // 4
    pos = np.arange(n_tok)
    lane = np.arange(LANES)
    sub = lane % DIFF_HD
    p = np.where(sub[None, :] < DIFF_HD // 2, (pos // GRID_W)[:, None], (pos % GRID_W)[:, None])
    inv = jnp.asarray(ROPE_THETA, F32) ** (-jnp.asarray(sub % half, F32) / half)
    ang = jnp.asarray(p, F32) * inv[None, :]
    sign = np.where((lane % (2 * half)) < half, -1.0, 1.0).astype(np.float32)
    return jnp.cos(ang), jnp.sin(ang) * sign[None, :]


def _diffattn_kernel(*refs, has_cache, lam_init, seqs):
    if has_cache:
        q_ref, k_ref, v_ref, ck_ref, cv_ref, lam_ref, sw_ref, o_ref = refs
    else:
        q_ref, k_ref, v_ref, lam_ref, sw_ref, o_ref = refs
    hd2 = 2 * DIFF_HD
    lv = lam_ref[...]
    lam = (jnp.exp(jnp.sum(lv[0:1] * lv[1:2], axis=1, keepdims=True))
           - jnp.exp(jnp.sum(lv[2:3] * lv[3:4], axis=1, keepdims=True)) + lam_init)
    nq, nk = q_ref.shape[0] // seqs, k_ref.shape[0] // seqs
    lane = lax.broadcasted_iota(I32, (nq, hd2), 1)
    for sq, h in ((sq, h) for sq in range(seqs) for h in range(DIFF_HEADS)):
        cols = slice(h * hd2, (h + 1) * hd2)
        q_rows, k_rows = slice(sq * nq, (sq + 1) * nq), slice(sq * nk, (sq + 1) * nk)
        q = q_ref[q_rows, cols]
        zero = jnp.zeros_like(q)
        k_new = k_ref[k_rows, cols].astype(BF16)
        values = [v_ref[k_rows, cols].astype(BF16)]
        if has_cache:
            past = ck_ref.shape[1]
            k_past_t = ck_ref[cols, :].astype(BF16)
            values.append(cv_ref[pl.ds(h, past, stride=DIFF_HEADS), :].astype(BF16))
        o = None
        for c in range(2):
            qc = jnp.where((lane < DIFF_HD) == (c == 0), q, zero)
            s = [_dot_nt(qc, k_new)]
            if has_cache:
                s.append(_dot(qc, k_past_t))
            mx = functools.reduce(jnp.maximum, [jnp.max(si, axis=1, keepdims=True) for si in s])
            e = [jnp.exp(si - mx) for si in s]
            z = functools.reduce(jnp.add, [jnp.sum(ei, axis=1, keepdims=True) for ei in e])
            pv = functools.reduce(jnp.add, [_dot(ei.astype(BF16), v) for ei, v in zip(e, values)])
            pv = pv * (1.0 / z)
            o = pv if c == 0 else o - lam * pv
        o_ref[q_rows, cols] = ((_rms(o) * sw_ref[...]) * (1.0 - lam_init)).astype(BF16)


def _diffattn(q, k, v, lam_vecs, subln_w, batch, seq_len, q_block, lam_init, cache=None, seqs=1):
    d = D_MODEL
    nq = seq_len // q_block
    has_cache = cache is not None
    assert seqs == 1 or (nq == 1 and not has_cache and batch % seqs == 0)
    kv_spec = pl.BlockSpec((seqs * seq_len, d), lambda b, qi: (b, 0))
    in_specs = [pl.BlockSpec((seqs * q_block, d), lambda b, qi: (b * nq + qi, 0)), kv_spec, kv_spec]
    args = [q, k, v]
    if has_cache:
        past = cache[0].shape[1]
        in_specs += [pl.BlockSpec((d, past), lambda b, qi: (b, 0)),
                     pl.BlockSpec((past * DIFF_HEADS, 2 * DIFF_HD), lambda b, qi: (b, 0))]
        args += list(cache)
    in_specs += [pl.BlockSpec((4, DIFF_HD), lambda b, qi: (0, 0)),
                 pl.BlockSpec((1, 2 * DIFF_HD), lambda b, qi: (0, 0))]
    args += [lam_vecs, subln_w]
    return pl.pallas_call(
        functools.partial(_diffattn_kernel, has_cache=has_cache, lam_init=lam_init, seqs=seqs),
        grid=(batch // seqs, nq),
        in_specs=in_specs,
        out_specs=pl.BlockSpec((seqs * q_block, d), lambda b, qi: (b * nq + qi, 0)),
        out_shape=jax.ShapeDtypeStruct((batch * seq_len, d), BF16),
        compiler_params=_params(2),
        name="diffattn_cache" if has_cache else "diffattn",
    )(*args)


def _router_weights(router_group, router_expert):
    w = jnp.concatenate([router_group, router_expert], axis=1)
    return jnp.pad(w, ((0, 0), (0, LANES - w.shape[1]))).astype(BF16)


def _inproj0_weights(w_in):
    gq, gk, gv, gg, gaf, gab, hq, hff, hfb, hi, hg = jnp.split(
        w_in, [256, 512, 1024, 1536, 1552, 1568, 1824, 2080, 2336, 2848], axis=1)
    w = jnp.concatenate([gq, gk, gv, gg, hq, hff, hfb, hi, hg, gaf, gab], axis=1)
    return jnp.pad(w, ((0, 0), (0, AB_COLS - w.shape[1]))).astype(BF16)


def kernel(x_prompt, x_sample, state_gla, state_hgrn, cache_diff_k, cache_diff_v, c, c_ctx,
           w_ada, b_ada, norm1_w, norm2_w, w_in_ab, gla_a2, gla_a_bias, hgrn_lb, gla_onorm_w,
           hgrn_onorm_w, w_out_ab, w_in_c, lam_q1, lam_k1, lam_q2, lam_k2, diff_subln_w, w_out_c,
           router_group, router_expert, moe_w1, moe_w3, moe_w2, final_norm_w):
    bp, lp, d = x_prompt.shape
    bs, ls, _ = x_sample.shape
    depth = w_ada.shape[0]
    assert depth == 2 and d == D_MODEL and bs <= 7
    tp, ts = bp * lp, bs * ls
    npt, nst = tp // ROW_TILE, ts // ROW_TILE
    tps = ls // ROW_TILE
    xp = x_prompt.reshape(tp, d)
    xs = x_sample.reshape(ts, d)

    cond8 = jnp.concatenate([c_ctx[None, :], c, jnp.zeros((7 - bs, d), F32)], axis=0)
    mods = _adaln(cond8, w_ada, b_ada).reshape(depth * 8, 6, d)

    proj = _inproj0(xp, xs, mods, norm1_w[0:1], _inproj0_weights(w_in_ab[0]), ls)
    a_bias = gla_a_bias[0][:, None, :]
    scan_args = (gla_a2[0], a_bias, hgrn_lb, gla_onorm_w[0:1], hgrn_onorm_w[0:1])
    mixed_p, new_state_gla, new_state_hgrn = _scan(proj, 0, bp, lp, *scan_args)
    s0 = jnp.concatenate([state_gla[:, 0], state_hgrn[:, 0]], axis=2).swapaxes(-1, -2)
    s0 = s0.reshape(bs, 2, SCAN_PAIRS, 2, HEAD_DV, HEAD_DK)
    zero = jnp.zeros_like(s0[:, :, :, 0])
    s0 = jnp.concatenate([jnp.concatenate([s0[:, :, :, 0], zero], axis=-1),
                          jnp.concatenate([zero, s0[:, :, :, 1]], axis=-1)], axis=-2)
    mixed_s = _scan(proj, tp, bs, ls, *scan_args, s0=s0)

    wr = _router_weights(router_group[0], router_expert[0])
    x1, *routed = _post((xp, xs), mixed_p, mixed_s, mods, 0, norm2_w[0:1],
                        w_out_ab[0].astype(BF16), wr, tps)
    ys, tables = _moe(*routed, moe_w1, moe_w3, moe_w2, 0)
    x2 = _combine(*tables, x1, mods, 0, final_norm_w[None, :], ys, 0, npt + nst, npt, tps, False)

    lam_init = 0.8 - 0.6 * math.exp(-0.3 * 1)
    cos_t, sin_t = _rope_tables(ls)
    (qp, kp, vp), (qs, ks, vs), (k_cache, v_cache) = _inproj1(
        x2, mods, norm1_w[1:2], w_in_c[0].astype(BF16), tp // INPROJ1_TILE, ts // INPROJ1_TILE,
        ls // INPROJ1_TILE, cos_t, sin_t)
    lam_vecs = jnp.stack([lam_q1[0], lam_k1[0], lam_q2[0], lam_k2[0]])
    att_p = _diffattn(qp, kp, vp, lam_vecs, diff_subln_w[0:1], bp, lp, lp, lam_init,
                      seqs=PROMPT_SEQS_PER_STEP)
    past = cache_diff_k.shape[2]
    assert lp == ATTN_TILE and DIFF_HEADS == TOKEN_ROWS
    cache = (cache_diff_k[:, 0].transpose(0, 2, 3, 4, 1).reshape(bs * d, past),
             cache_diff_v[:, 0].reshape(bs * past * DIFF_HEADS, 2 * DIFF_HD))
    att_s = _diffattn(qs, ks, vs, lam_vecs, diff_subln_w[0:1], bs, ls, SAMPLE_Q_BLOCK, lam_init,
                      cache)

    wr = _router_weights(router_group[1], router_expert[1])
    x3, *routed = _post((x2,), att_p, att_s, mods, 1, norm2_w[1:2],
                        w_out_c[0].astype(BF16), wr, tps)
    ys, tables = _moe(*routed, moe_w1, moe_w3, moe_w2, 1)
    fw = final_norm_w[None, :]
    y_p = _combine(*tables, x3, mods, 1, fw, ys, 0, npt, npt, tps, True)
    y_s = _combine(*tables, x3, mods, 1, fw, ys, npt, nst, npt, tps, True)

    return (y_p.reshape(bp, lp, d), y_s.reshape(bs, ls, d), new_state_gla, new_state_hgrn,
            k_cache.reshape(bp, 1, DIFF_HEADS, 2, DIFF_HD, lp).transpose(0, 1, 5, 2, 3, 4),
            v_cache.reshape(bp, 1, lp, DIFF_HEADS, 2 * DIFF_HD))
```

```python
import functools
import math

import jax
import jax.numpy as jnp
import numpy as np
from jax import lax
from jax.experimental import pallas as pl
from jax.experimental.pallas import tpu as pltpu

F32 = jnp.float32
BF16 = jnp.bfloat16
I32 = jnp.int32

D_MODEL = 1024
GLA_HEADS = 4
HGRN_HEADS = 4
SCAN_HEADS = GLA_HEADS + HGRN_HEADS
SCAN_PAIRS = SCAN_HEADS // 2
HEAD_DK = 64
HEAD_DV = 128
GATE_RANK = 16
GLA_GATE_NORM = 16.0
DIFF_HEADS = 8
DIFF_HD = 64
GRID_W = 64
ROPE_THETA = 10000.0
N_GROUPS = 4
EXPERTS_PER_GROUP = 8
N_EXPERTS = N_GROUPS * EXPERTS_PER_GROUP
MOE_HIDDEN = 512
EPS = 1e-6
LANES = 128
TOKEN_ROWS = D_MODEL // LANES
NEG_BIG = -1e30
ROUTER_ROWS = 48

ROW_TILE = 512
ATTN_TILE = 256
SAMPLE_Q_BLOCK = 512
PROMPT_SEQS_PER_STEP = 2
ADA_TILE = 1536
INPROJ0_TILE = 512
INPROJ1_TILE = 512
SCAN_CHUNK = 64
EXPERT_TILE = 256
MOVE_BLOCK = 16
STAGE_TOKENS = 2 * ROW_TILE + N_EXPERTS * MOVE_BLOCK
MAX_BLOCKS = STAGE_TOKENS // MOVE_BLOCK
VMEM_LIMIT = 56 * 1024 * 1024
SCAN_INPUT_DOUBLE_BUFFER_BYTES = 16 * 1024 * 1024

_C_GQ, _C_GK, _C_GV, _C_GG = 0, 256, 512, 1024
_C_HQ, _C_HFF, _C_HFB, _C_HI, _C_HG = 1536, 1792, 2048, 2304, 2816
_C_GAF, _C_GAB = 3328, 3344
AB_COLS = 3456


def _params(n_axes, vmem=VMEM_LIMIT):
    return pltpu.CompilerParams(dimension_semantics=("arbitrary",) * n_axes,
                                vmem_limit_bytes=vmem)


def _cdiv(a, b):
    return (a + b - 1) // b


def _dot(a, b):
    return jnp.dot(a, b, preferred_element_type=F32)


def _dot_nt(a, b):
    return lax.dot_general(a, b, (((1,), (1,)), ((), ())), preferred_element_type=F32)


def _dot_tn(a, b):
    return lax.dot_general(a, b, (((0,), (0,)), ((), ())), preferred_element_type=F32)


def _split_bf16(x):
    hi = x.astype(BF16)
    lo = (x - hi.astype(F32)).astype(BF16)
    return hi, lo


def _silu(x):
    return x * jax.nn.sigmoid(x)


def _log_sigmoid(x):
    return jnp.minimum(x, 0.0) - jnp.log(1.0 + jnp.exp(-jnp.abs(x)))


def _rms(x):
    return x * lax.rsqrt(jnp.mean(x * x, axis=-1, keepdims=True) + EPS)


def _modulate(x, norm_w, shift, scale):
    return (_rms(x) * norm_w) * (1.0 + scale) + shift


def _to_token_major(dst_ref, x, row0=0):
    n = x.shape[0]
    for s in range(TOKEN_ROWS):
        dst_ref[pl.ds(row0 + s, n, stride=TOKEN_ROWS), :] = x[:, s * LANES:(s + 1) * LANES]


def _from_token_major(src_ref, n, row0=0):
    return jnp.concatenate([src_ref[pl.ds(row0 + s, n, stride=TOKEN_ROWS), :]
                            for s in range(TOKEN_ROWS)], axis=1)


def _ada_kernel(c_ref, w_ref, b_ref, o_ref):
    s = _silu(c_ref[...])
    o_ref[0] = _dot(s.astype(BF16), w_ref[0].astype(BF16)) + b_ref[0]


def _adaln(cond8, w_ada, b_ada):
    depth, d, n = w_ada.shape
    tn = ADA_TILE
    return pl.pallas_call(
        _ada_kernel,
        grid=(depth, n // tn),
        in_specs=[pl.BlockSpec((8, d), lambda l, j: (0, 0)),
                  pl.BlockSpec((1, d, tn), lambda l, j: (l, 0, j)),
                  pl.BlockSpec((1, 1, tn), lambda l, j: (l, 0, j))],
        out_specs=pl.BlockSpec((1, 8, tn), lambda l, j: (l, 0, j)),
        out_shape=jax.ShapeDtypeStruct((depth, 8, n), F32),
        compiler_params=_params(2),
        name="adaln",
    )(cond8, w_ada, b_ada.reshape(depth, 1, n))


def _mod_row(i, layer, n_prompt_tiles, tiles_per_sample):
    r = jnp.where(i < n_prompt_tiles, 0, 1 + (i - n_prompt_tiles) // tiles_per_sample)
    return layer * 8 + r


def _inproj0_kernel(xp_ref, xs_ref, mod_ref, nw_ref, w_ref, o_ref, *, n_prompt_tiles):
    i = pl.program_id(0)
    x = jnp.where(i < n_prompt_tiles, xp_ref[...], xs_ref[...])
    h = _modulate(x, nw_ref[...], mod_ref[0, 0:1, :], mod_ref[0, 1:2, :])
    o_ref[...] = _dot(h.astype(BF16), w_ref[...])


def _inproj0(xp, xs, mods, norm_w, w_bf16, sample_len):
    tp, d = xp.shape
    ts = xs.shape[0]
    n = w_bf16.shape[1]
    tile = INPROJ0_TILE
    npt, nst = tp // tile, ts // tile
    mod_map = lambda i: (_mod_row(i, 0, npt, sample_len // tile), 0, 0)
    return pl.pallas_call(
        functools.partial(_inproj0_kernel, n_prompt_tiles=npt),
        grid=(npt + nst,),
        in_specs=[pl.BlockSpec((tile, d), lambda i: (jnp.minimum(i, npt - 1), 0)),
                  pl.BlockSpec((tile, d), lambda i: (jnp.maximum(i - npt, 0), 0)),
                  pl.BlockSpec((1, 6, d), mod_map),
                  pl.BlockSpec((1, d), lambda i: (0, 0)),
                  pl.BlockSpec((d, n), lambda i: (0, 0))],
        out_specs=pl.BlockSpec((tile, n), lambda i: (i, 0)),
        out_shape=jax.ShapeDtypeStruct((tp + ts, n), F32),
        compiler_params=_params(1),
        name="inproj0",
    )(xp, xs, mods, norm_w, w_bf16)


def _scan_kernel(*refs, seq_len, has_state):
    if has_state:
        (p_ref, a2_ref, ab_ref, lb_ref, ong_ref, onh_ref, s0_ref, mixed_ref, *scratch) = refs
        sfin_ref = None
    else:
        (p_ref, a2_ref, ab_ref, lb_ref, ong_ref, onh_ref, mixed_ref, sg_ref, sh_ref,
         *scratch) = refs
        s0_ref = None
        sfin_ref = (sg_ref, sh_ref)
    (qi_f, ki_f, qo_f, ko_f, qi_b, ki_b, qo_b, ko_b,
     vv, dec_f, dec_b, o_f, o_b, st_f, st_b) = scratch
    C = SCAN_CHUNK
    n_chunks = seq_len // C
    gqk = GLA_HEADS * HEAD_DK

    row = lax.broadcasted_iota(I32, (C, C), 0)
    col = lax.broadcasted_iota(I32, (C, C), 1)
    lower = col <= row
    upper = col >= row
    tri_lo = jnp.where(lower, 1.0, 0.0).astype(BF16)
    tri_up = jnp.where(upper, 1.0, 0.0).astype(BF16)

    lbp = lb_ref[...]
    lb_max = jnp.maximum(lbp[0], lbp[1])
    lb_e0 = jnp.exp(lbp[0] - lb_max)
    lb_e1 = jnp.exp(lbp[1] - lb_max)
    lb = lb_e0 / (lb_e0 + lb_e1)

    def cumsum_chunk(tri, la):
        hi, lo = _split_bf16(la)
        return _dot(tri, hi) + _dot(tri, lo)

    def prep(n, carry):
        r0 = pl.multiple_of(n * C, C)
        rows = pl.ds(r0, C)
        gq = p_ref[rows, _C_GQ:_C_GQ + gqk] * (HEAD_DK ** -0.5)
        gk = p_ref[rows, _C_GK:_C_GK + gqk]
        hq = _silu(p_ref[rows, _C_HQ:_C_HQ + gqk]) * (HEAD_DK ** -0.5)
        for d_i, (qi_s, ki_s, qo_s, ko_s, dec_s, tri, last, mid) in enumerate(
                ((qi_f, ki_f, qo_f, ko_f, dec_f, tri_lo, C - 1, C // 2 - 1),
                 (qi_b, ki_b, qo_b, ko_b, dec_b, tri_up, 0, C // 2))):
            c_ga = _C_GAF if d_i == 0 else _C_GAB
            c_hf = _C_HFF if d_i == 0 else _C_HFB
            ga = p_ref[rows, c_ga:c_ga + GATE_RANK]
            xg = _dot(ga.astype(BF16), a2_ref[d_i].astype(BF16)) + ab_ref[d_i]
            la_g = _log_sigmoid(xg) / GLA_GATE_NORM
            f = lb[d_i:d_i + 1, :] + (1.0 - lb[d_i:d_i + 1, :]) * jax.nn.sigmoid(
                p_ref[rows, c_hf:c_hf + gqk])
            la_h = jnp.log(f)
            for q, k, la, c0 in ((gq, gk, la_g, 0), (hq, 1.0 - f, la_h, gqk)):
                b = cumsum_chunk(tri, la)
                b_mid, b_end = b[mid:mid + 1, :], b[last:last + 1, :]
                cs = slice(c0, c0 + gqk)
                qi_s[rows, cs] = (q * jnp.exp(b - b_mid)).astype(BF16)
                ki_s[rows, cs] = (k * jnp.exp(b_mid - b)).astype(BF16)
                qo_s[rows, cs] = (q * jnp.exp(b)).astype(BF16)
                ko_s[rows, cs] = (k * jnp.exp(b_end - b)).astype(BF16)
                dec_s[n, :, cs] = jnp.exp(b_end)
        gv_cols = GLA_HEADS * HEAD_DV
        vv[rows, 0:gv_cols] = p_ref[rows, _C_GV:_C_GV + gv_cols].astype(BF16)
        vv[rows, gv_cols:] = p_ref[rows, _C_HI:_C_HI + HGRN_HEADS * HEAD_DV].astype(BF16)
        return carry

    lax.fori_loop(0, n_chunks, prep, 0, unroll=8)

    for p in range(SCAN_PAIRS):
        if has_state:
            st_f[p] = s0_ref[0, 0, p]
            st_b[p] = s0_ref[0, 1, p]
        else:
            st_f[p] = jnp.zeros((2 * HEAD_DV, 2 * HEAD_DK), F32)
            st_b[p] = jnp.zeros((2 * HEAD_DV, 2 * HEAD_DK), F32)

    first_head = lax.broadcasted_iota(I32, (C, 2 * HEAD_DK), 1) < HEAD_DK
    row2 = lax.broadcasted_iota(I32, (2 * C, C), 0) % C
    col2 = lax.broadcasted_iota(I32, (2 * C, C), 1)
    lower2 = col2 <= row2
    upper2 = col2 >= row2

    def per_head_rows(x):
        z = jnp.zeros_like(x)
        return jnp.concatenate([jnp.where(first_head, x, z), jnp.where(first_head, z, x)], axis=0)

    def put_out(o_ref, rows, p, res):
        c0 = p * 2 * HEAD_DV
        o_ref[rows, c0:c0 + HEAD_DV] = res[0:C, 0:HEAD_DV]
        o_ref[rows, c0 + HEAD_DV:c0 + 2 * HEAD_DV] = res[C:2 * C, HEAD_DV:2 * HEAD_DV]

    def sweep(n, carry):
        m = n_chunks - 1 - n
        rows = pl.ds(pl.multiple_of(n * C, C), C)
        rows_m = pl.ds(pl.multiple_of(m * C, C), C)
        decay_f, decay_b = dec_f[n], dec_b[m]
        for p in range(SCAN_PAIRS):
            ks = slice(p * 2 * HEAD_DK, (p + 1) * 2 * HEAD_DK)
            vs = slice(p * 2 * HEAD_DV, (p + 1) * 2 * HEAD_DV)
            vh = vv[rows, vs]
            s_f = st_f[p]
            sc = (jnp.where(lower2, _dot_nt(per_head_rows(qi_f[rows, ks]), ki_f[rows, ks]), 0.0)
                  + jnp.where(upper2, _dot_nt(per_head_rows(qi_b[rows, ks]), ki_b[rows, ks]), 0.0))
            put_out(o_f, rows, p, _dot_nt(per_head_rows(qo_f[rows, ks]), s_f.astype(BF16))
                    + _dot(sc.astype(BF16), vh))
            st_f[p] = decay_f[:, ks] * s_f + _dot_tn(vh, ko_f[rows, ks])
            s_b = st_b[p]
            put_out(o_b, rows_m, p, _dot_nt(per_head_rows(qo_b[rows_m, ks]), s_b.astype(BF16)))
            st_b[p] = decay_b[:, ks] * s_b + _dot_tn(vv[rows_m, vs], ko_b[rows_m, ks])
        return carry

    lax.fori_loop(0, n_chunks, sweep, 0, unroll=8)

    def finish(n, carry):
        rows = pl.ds(pl.multiple_of(n * C, C), C)
        for h in range(SCAN_HEADS):
            vs = slice(h * HEAD_DV, (h + 1) * HEAD_DV)
            if h < GLA_HEADS:
                gate = p_ref[rows, _C_GG + h * HEAD_DV:_C_GG + (h + 1) * HEAD_DV]
                onw = ong_ref[...]
            else:
                hh = h - GLA_HEADS
                gate = p_ref[rows, _C_HG + hh * HEAD_DV:_C_HG + (hh + 1) * HEAD_DV]
                onw = onh_ref[...]
            o = o_f[rows, vs] + o_b[rows, vs]
            mixed_ref[rows, vs] = ((_rms(o) * onw) * _silu(gate)).astype(BF16)
        return carry

    lax.fori_loop(0, n_chunks, finish, 0, unroll=4)

    if sfin_ref is not None:
        for d_i, st in enumerate((st_f, st_b)):
            for p in range(SCAN_PAIRS):
                s_pair = st[p].T
                out_ref = sfin_ref[(2 * p) // GLA_HEADS]
                h0 = (2 * p) % GLA_HEADS
                out_ref[0, 0, d_i, h0] = s_pair[0:HEAD_DK, 0:HEAD_DV]
                out_ref[0, 0, d_i, h0 + 1] = s_pair[HEAD_DK:2 * HEAD_DK, HEAD_DV:2 * HEAD_DV]


def _scan(p, row0, batch, seq_len, a2, a_bias, lb, onorm_g, onorm_h, s0=None):
    n = p.shape[1]
    assert row0 % seq_len == 0
    blk0 = row0 // seq_len
    has_state = s0 is not None
    n_chunks = seq_len // SCAN_CHUNK
    assert GLA_HEADS == HGRN_HEADS and GLA_HEADS % 2 == 0
    st_shape = (1, 1, 2, GLA_HEADS, HEAD_DK, HEAD_DV)
    pair_shape = (SCAN_PAIRS, 2 * HEAD_DV, 2 * HEAD_DK)
    p_mode = dict(pipeline_mode=pl.Buffered(1)) if seq_len * n * 4 > SCAN_INPUT_DOUBLE_BUFFER_BYTES else {}
    in_specs = [pl.BlockSpec((seq_len, n), lambda b: (blk0 + b, 0), **p_mode),
                pl.BlockSpec(a2.shape, lambda b: (0, 0, 0)),
                pl.BlockSpec(a_bias.shape, lambda b: (0, 0, 0)),
                pl.BlockSpec(lb.shape, lambda b: (0, 0, 0)),
                pl.BlockSpec((1, HEAD_DV), lambda b: (0, 0)),
                pl.BlockSpec((1, HEAD_DV), lambda b: (0, 0))]
    args = [p, a2, a_bias, lb, onorm_g, onorm_h]
    mixed_shape = jax.ShapeDtypeStruct((batch * seq_len, D_MODEL), BF16)
    mixed_spec = pl.BlockSpec((seq_len, D_MODEL), lambda b: (b, 0))
    if has_state:
        in_specs.append(pl.BlockSpec((1, 2) + pair_shape, lambda b: (b, 0, 0, 0, 0)))
        args.append(s0)
        out_shape, out_specs = mixed_shape, mixed_spec
    else:
        st_struct = jax.ShapeDtypeStruct((batch,) + st_shape[1:], F32)
        st_spec = pl.BlockSpec(st_shape, lambda b: (b, 0, 0, 0, 0, 0))
        out_shape = (mixed_shape, st_struct, st_struct)
        out_specs = (mixed_spec, st_spec, st_spec)
    qk_cols = SCAN_HEADS * HEAD_DK
    scratch = [pltpu.VMEM((seq_len, qk_cols), BF16) for _ in range(8)]
    scratch += [pltpu.VMEM((seq_len, D_MODEL), BF16),
                pltpu.VMEM((n_chunks, 1, qk_cols), F32), pltpu.VMEM((n_chunks, 1, qk_cols), F32),
                pltpu.VMEM((seq_len, D_MODEL), F32), pltpu.VMEM((seq_len, D_MODEL), F32),
                pltpu.VMEM(pair_shape, F32), pltpu.VMEM(pair_shape, F32)]
    return pl.pallas_call(
        functools.partial(_scan_kernel, seq_len=seq_len, has_state=has_state),
        grid=(batch,),
        in_specs=in_specs, out_specs=out_specs, out_shape=out_shape,
        scratch_shapes=scratch,
        compiler_params=_params(1),
        name="scan_state" if has_state else "scan_fresh",
    )(*args)


def _post_kernel(*refs, split_x, n_prompt_tiles):
    if split_x:
        xp_ref, xs_ref = refs[0], refs[1]
        refs = refs[2:]
    else:
        x_ref = refs[0]
        refs = refs[1:]
    (mp_ref, ms_ref, mod_ref, nw_ref, wo_ref, wr_ref,
     x1_ref, h2_ref, slot_ref, wgt_ref, tab_ref, carry, earlier) = refs
    i = pl.program_id(0)
    is_prompt = i < n_prompt_tiles
    if split_x:
        x = jnp.where(is_prompt, xp_ref[...], xs_ref[...])
    else:
        x = x_ref[...]
    mixed = jnp.where(is_prompt, mp_ref[...], ms_ref[...])
    x1 = x + mod_ref[0, 2:3, :] * _dot(mixed, wo_ref[...])
    x1_ref[...] = x1
    h2 = _modulate(x1, nw_ref[...], mod_ref[0, 3:4, :], mod_ref[0, 4:5, :])
    _to_token_major(h2_ref, h2)

    logits = _dot(h2.astype(BF16), wr_ref[...])
    tm = logits.shape[0]
    lt = logits.T[0:ROUTER_ROWS]
    ridx = lax.broadcasted_iota(I32, (ROUTER_ROWS, tm), 0).astype(F32)

    def first_max(v):
        mx = jnp.max(v, axis=0, keepdims=True)
        idx = jnp.min(jnp.where(v == mx, ridx, float(ROUTER_ROWS)), axis=0, keepdims=True)
        return mx, idx

    gl = jnp.where(ridx < N_GROUPS, lt, NEG_BIG)
    gmax, gidx = first_max(gl)
    g_val = 1.0 / jnp.sum(jnp.exp(gl - gmax), axis=0, keepdims=True)
    lo = N_GROUPS + EXPERTS_PER_GROUP * gidx
    el = jnp.where((ridx >= lo) & (ridx < lo + EXPERTS_PER_GROUP), lt, NEG_BIG)
    emax, l1 = first_max(el)
    esum = jnp.sum(jnp.exp(el - emax), axis=0, keepdims=True)
    e2max, l2 = first_max(jnp.where(ridx == l1, NEG_BIG, el))
    p1 = 1.0 / esum
    p2 = jnp.exp(e2max - emax) / esum
    w1 = g_val * (p1 / (p1 + p2))
    w2 = g_val * (p2 / (p1 + p2))

    @pl.when(i == 0)
    def _():
        carry[...] = jnp.zeros_like(carry)
        t_row = lax.broadcasted_iota(I32, earlier.shape, 0)
        t_col = lax.broadcasted_iota(I32, earlier.shape, 1)
        earlier[...] = jnp.where(t_row < t_col, 1.0, 0.0).astype(BF16)

    sel1 = ridx == l1
    sel2 = ridx == l2
    onehot = jnp.where(sel1 | sel2, 1.0, 0.0)
    before = _dot(onehot.astype(BF16), earlier[...])
    count = jnp.sum(onehot, axis=1, keepdims=True)
    blocks = jnp.floor((count + (MOVE_BLOCK - 1.0)) * (1.0 / MOVE_BLOCK)) * MOVE_BLOCK
    r_row = lax.broadcasted_iota(I32, (ROUTER_ROWS, ROUTER_ROWS), 0)
    r_col = lax.broadcasted_iota(I32, (ROUTER_ROWS, ROUTER_ROWS), 1)
    lower_rows = jnp.where(r_col < r_row, 1.0, 0.0).astype(BF16)
    run_start = _dot(lower_rows,
                     jnp.broadcast_to(blocks, (ROUTER_ROWS, LANES)).astype(BF16))[:, 0:1]
    slot = before + run_start
    q1 = jnp.sum(jnp.where(sel1, slot, 0.0), axis=0, keepdims=True)
    q2 = jnp.sum(jnp.where(sel2, slot, 0.0), axis=0, keepdims=True)
    tab_lane = lax.broadcasted_iota(I32, (ROUTER_ROWS, LANES), 1)
    tab_ref[0] = jnp.where(tab_lane == 0, count,
                           jnp.where(tab_lane == 1, carry[...],
                                     jnp.where(tab_lane == 2, run_start, 0.0)))
    carry[...] = carry[...] + count

    slot_ref[0, 0:1, :] = (q1 * TOKEN_ROWS).astype(I32)
    slot_ref[0, 1:2, :] = (q2 * TOKEN_ROWS).astype(I32)
    wgt_ref[0, 0:1, :] = w1
    wgt_ref[0, 1:2, :] = w2


def _post(x_args, mixed_p, mixed_s, mods, layer, norm_w, w_out_bf16, w_router,
          tiles_per_sample):
    split_x = len(x_args) == 2
    tp, ts = mixed_p.shape[0], mixed_s.shape[0]
    t, d = tp + ts, D_MODEL
    npt, nst = tp // ROW_TILE, ts // ROW_TILE
    tile = lambda i: (i, 0)
    if split_x:
        x_specs = [pl.BlockSpec((ROW_TILE, d), lambda i: (jnp.minimum(i, npt - 1), 0)),
                   pl.BlockSpec((ROW_TILE, d), lambda i: (jnp.maximum(i - npt, 0), 0))]
    else:
        x_specs = [pl.BlockSpec((ROW_TILE, d), tile)]
    in_specs = x_specs + [
        pl.BlockSpec((ROW_TILE, d), lambda i: (jnp.minimum(i, npt - 1), 0)),
        pl.BlockSpec((ROW_TILE, d), lambda i: (jnp.maximum(i - npt, 0), 0)),
        pl.BlockSpec((1, 6, d), lambda i: (_mod_row(i, layer, npt, tiles_per_sample), 0, 0)),
        pl.BlockSpec((1, d), lambda i: (0, 0)),
        pl.BlockSpec((d, d), lambda i: (0, 0)),
        pl.BlockSpec((d, LANES), lambda i: (0, 0))]
    return pl.pallas_call(
        functools.partial(_post_kernel, split_x=split_x, n_prompt_tiles=npt),
        grid=(npt + nst,),
        in_specs=in_specs,
        out_specs=(pl.BlockSpec((ROW_TILE, d), tile),
                   pl.BlockSpec((ROW_TILE * TOKEN_ROWS, LANES), tile),
                   pl.BlockSpec((1, 2, ROW_TILE), lambda i: (i, 0, 0)),
                   pl.BlockSpec((1, 2, ROW_TILE), lambda i: (i, 0, 0)),
                   pl.BlockSpec((1, ROUTER_ROWS, LANES), lambda i: (i, 0, 0))),
        out_shape=(jax.ShapeDtypeStruct((t, d), F32),
                   jax.ShapeDtypeStruct((t * TOKEN_ROWS, LANES), F32),
                   jax.ShapeDtypeStruct((npt + nst, 2, ROW_TILE), I32),
                   jax.ShapeDtypeStruct((npt + nst, 2, ROW_TILE), F32),
                   jax.ShapeDtypeStruct((npt + nst, ROUTER_ROWS, LANES), F32)),
        scratch_shapes=[pltpu.VMEM((ROUTER_ROWS, LANES), F32),
                        pltpu.VMEM((ROW_TILE, ROW_TILE), BF16)],
        compiler_params=_params(1),
        name=f"post{layer}",
    )(*x_args, mixed_p, mixed_s, mods, norm_w, w_out_bf16, w_router)


def _for_blocks(tab_ref, fn):
    block_rows = MOVE_BLOCK * TOKEN_ROWS
    count = tab_ref[0, 0, MAX_BLOCKS]

    def call(k, parity):
        fn(pl.multiple_of(k * block_rows, block_rows),
           pl.multiple_of(tab_ref[0, 0, k], TOKEN_ROWS), parity)

    def body(k2, c):
        call(2 * k2, 0)

        @pl.when(2 * k2 + 1 < count)
        def _():
            call(2 * k2 + 1, 1)
        return c

    lax.fori_loop(0, _cdiv(count, 2), body, 0)


def _wait_blocks(tab_ref, copy):
    def body(k, c):
        copy.wait()
        return c

    lax.fori_loop(0, tab_ref[0, 0, MAX_BLOCKS], body, 0)


def _dispatch_kernel(zero_ref, tab_ref, prev_tab_ref, q_ref, h2_ref, hs_ref, zero_buf, stage, sem):
    j = pl.program_id(0)
    slot = j % 2
    block_rows = MOVE_BLOCK * TOKEN_ROWS

    @pl.when(j == 0)
    def _():
        zero_buf[...] = jnp.zeros_like(zero_buf)

        def zero_copy(k):
            start = pl.multiple_of(zero_ref[k], EXPERT_TILE * TOKEN_ROWS)
            return pltpu.make_async_copy(
                zero_buf, hs_ref.at[pl.ds(start, EXPERT_TILE * TOKEN_ROWS)], sem.at[0])

        def start_zero(k2, c):
            for parity in range(2):
                @pl.when(zero_ref[2 * k2 + parity] >= 0)
                def _():
                    zero_copy(2 * k2 + parity).start(priority=parity)
            return c

        def wait_zero(k, c):
            @pl.when(zero_ref[k] >= 0)
            def _():
                zero_copy(k).wait()
            return c

        lax.fori_loop(0, zero_ref.shape[0] // 2, start_zero, 0)
        stage[...] = jnp.zeros_like(stage)
        lax.fori_loop(0, zero_ref.shape[0], wait_zero, 0)

    def place(r, c):
        tok = h2_ref[pl.ds(pl.multiple_of(r * TOKEN_ROWS, TOKEN_ROWS), TOKEN_ROWS), :]
        for s in range(2):
            row = pl.multiple_of(q_ref[0, s, r], TOKEN_ROWS)
            stage[slot, pl.ds(row, TOKEN_ROWS), :] = tok
        return c

    lax.fori_loop(0, ROW_TILE, place, 0, unroll=8)

    def block_copy(buf, stage_row, sorted_row):
        return pltpu.make_async_copy(stage.at[buf, pl.ds(stage_row, block_rows)],
                                     hs_ref.at[pl.ds(sorted_row, block_rows)], sem.at[buf])

    @pl.when(j > 0)
    def _():
        _wait_blocks(prev_tab_ref, block_copy(1 - slot, 0, 0))

    _for_blocks(tab_ref, lambda a, b, parity: block_copy(slot, a, b).start(priority=parity))

    @pl.when(j == pl.num_programs(0) - 1)
    def _():
        _wait_blocks(tab_ref, block_copy(slot, 0, 0))


def _dispatch(zero_tiles, block_tab, slots, h2, n_rows):
    t = h2.shape[0] // TOKEN_ROWS
    nt = t // ROW_TILE
    assert zero_tiles.shape[0] % 2 == 0
    smem_tile = lambda shape: pl.BlockSpec((1,) + shape, lambda j, *_: (j, 0, 0),
                                           memory_space=pltpu.SMEM)
    grid_spec = pltpu.PrefetchScalarGridSpec(
        num_scalar_prefetch=1,
        grid=(nt,),
        in_specs=[smem_tile((1, LANES)),
                  pl.BlockSpec((1, 1, LANES), lambda j, *_: (jnp.maximum(j - 1, 0), 0, 0),
                               memory_space=pltpu.SMEM),
                  smem_tile((2, ROW_TILE)),
                  pl.BlockSpec((ROW_TILE * TOKEN_ROWS, LANES), lambda j, *_: (j, 0))],
        out_specs=pl.BlockSpec(memory_space=pl.ANY),
        scratch_shapes=[pltpu.VMEM((EXPERT_TILE * TOKEN_ROWS, LANES), F32),
                        pltpu.VMEM((2, STAGE_TOKENS * TOKEN_ROWS, LANES), F32),
                        pltpu.SemaphoreType.DMA((2,))])
    return pl.pallas_call(
        _dispatch_kernel,
        grid_spec=grid_spec,
        out_shape=jax.ShapeDtypeStruct((n_rows * TOKEN_ROWS, LANES), F32),
        compiler_params=_params(1),
        name="dispatch",
    )(zero_tiles, block_tab, block_tab, slots, h2)


def _expert_kernel(te_ref, src_ref, nv_ref, run_ref, nxt_ref, hs_ref, w1_hbm, w3_hbm, w2_hbm,
                   ys_ref, w1f, w3f, w2f, w1b, w3b, w2b, sem, *, layer):
    i = pl.program_id(0)

    def weight_copies(e, buf):
        return [pltpu.make_async_copy(src.at[layer, e], dst.at[buf], sem.at[buf])
                for src, dst in ((w1_hbm, w1f), (w3_hbm, w3f), (w2_hbm, w2f))]

    @pl.when(i == 0)
    def _():
        for c in weight_copies(te_ref[0], 0):
            c.start()

    first = (i == 0) | (run_ref[i] != run_ref[jnp.maximum(i - 1, 0)])

    @pl.when(first)
    def _():
        buf = run_ref[i] % 2
        for c in weight_copies(te_ref[i], buf):
            c.wait()

        @pl.when(nxt_ref[i] >= 0)
        def _():
            for c in weight_copies(nxt_ref[i], 1 - buf):
                c.start()

        w1b[...] = w1f[buf].astype(BF16)
        w3b[...] = w3f[buf].astype(BF16)
        w2b[...] = w2f[buf].astype(BF16)

    @pl.when(nv_ref[i] > 0)
    def _():
        h = _from_token_major(hs_ref, EXPERT_TILE).astype(BF16)
        g = _silu(_dot(h, w1b[...])) * _dot(h, w3b[...])
        _to_token_major(ys_ref, _dot(g.astype(BF16), w2b[...]))

    @pl.when(nv_ref[i] == 0)
    def _():
        ys_ref[...] = jnp.zeros_like(ys_ref)


def _experts(tile_expert, tile_src, tile_rows, hs, w1, w3, w2, layer):
    n_rows, d = hs.shape[0] // TOKEN_ROWS, D_MODEL
    nt = n_rows // EXPERT_TILE
    hid = w1.shape[-1]
    tok_tile = (EXPERT_TILE * TOKEN_ROWS, LANES)
    changed = jnp.concatenate([jnp.zeros((1,), I32),
                               (tile_expert[1:] != tile_expert[:-1]).astype(I32)])
    run = jnp.cumsum(changed).astype(I32)
    later = jnp.where(run[None, :] > run[:, None], tile_expert[None, :], N_EXPERTS)
    next_expert = jnp.min(later, axis=1)
    next_expert = jnp.where(next_expert < N_EXPERTS, next_expert, -1).astype(I32)
    grid_spec = pltpu.PrefetchScalarGridSpec(
        num_scalar_prefetch=5,
        grid=(nt,),
        in_specs=[pl.BlockSpec(tok_tile, lambda i, te, src, nv, run, nxt: (src[i], 0)),
                  pl.BlockSpec(memory_space=pl.ANY), pl.BlockSpec(memory_space=pl.ANY),
                  pl.BlockSpec(memory_space=pl.ANY)],
        out_specs=pl.BlockSpec(tok_tile, lambda i, te, src, nv, run, nxt: (i, 0)),
        scratch_shapes=[pltpu.VMEM((2, d, hid), F32), pltpu.VMEM((2, d, hid), F32),
                        pltpu.VMEM((2, hid, d), F32),
                        pltpu.VMEM((d, hid), BF16), pltpu.VMEM((d, hid), BF16),
                        pltpu.VMEM((hid, d), BF16), pltpu.SemaphoreType.DMA((2,))])
    return pl.pallas_call(
        functools.partial(_expert_kernel, layer=layer),
        grid_spec=grid_spec,
        out_shape=jax.ShapeDtypeStruct(hs.shape, F32),
        compiler_params=_params(1),
        name=f"experts{layer}",
    )(tile_expert, tile_src, tile_rows, run, next_expert, hs, w1, w3, w2)


def _combine_kernel(tab_ref, next_tab_ref, q_ref, w_ref, x1_ref, mod_ref, fw_ref, ys_ref, out_ref,
                    stage, y_tok, sem, *, final_norm):
    i = pl.program_id(0)
    slot = i % 2
    block_rows = MOVE_BLOCK * TOKEN_ROWS

    def block_copy(buf, stage_row, sorted_row):
        return pltpu.make_async_copy(ys_ref.at[pl.ds(sorted_row, block_rows)],
                                     stage.at[buf, pl.ds(stage_row, block_rows)], sem.at[buf])

    def fetch(tab, buf):
        _for_blocks(tab, lambda a, b, parity: block_copy(buf, a, b).start(priority=parity))

    @pl.when(i == 0)
    def _():
        fetch(tab_ref, slot)

    @pl.when(i + 1 < pl.num_programs(0))
    def _():
        fetch(next_tab_ref, 1 - slot)

    _wait_blocks(tab_ref, block_copy(slot, 0, 0))

    def pick(r, c):
        rows = [stage[slot, pl.ds(pl.multiple_of(q_ref[0, s, r], TOKEN_ROWS), TOKEN_ROWS), :]
                for s in range(2)]
        y_tok[pl.ds(pl.multiple_of(r * TOKEN_ROWS, TOKEN_ROWS), TOKEN_ROWS), :] = (
            w_ref[0, 0, r] * rows[0] + w_ref[0, 1, r] * rows[1])
        return c

    lax.fori_loop(0, ROW_TILE, pick, 0, unroll=8)
    x2 = x1_ref[...] + mod_ref[0, 5:6, :] * _from_token_major(y_tok, ROW_TILE)
    if final_norm:
        x2 = _rms(x2) * fw_ref[...]
    out_ref[...] = x2


def _combine(block_tab, slots, weights, x1, mods, layer, final_w, ys, tile0, n_tiles,
             n_prompt_tiles, tiles_per_sample, final_norm):
    d = D_MODEL
    tile = lambda i: (tile0 + i, 0)
    mod_map = lambda i: (_mod_row(tile0 + i, layer, n_prompt_tiles, tiles_per_sample), 0, 0)
    smem_tile = lambda shape: pl.BlockSpec((1,) + shape, lambda i: (tile0 + i, 0, 0),
                                           memory_space=pltpu.SMEM)
    return pl.pallas_call(
        functools.partial(_combine_kernel, final_norm=final_norm),
        grid=(n_tiles,),
        in_specs=[smem_tile((1, LANES)),
                  pl.BlockSpec((1, 1, LANES),
                               lambda i: (tile0 + jnp.minimum(i + 1, n_tiles - 1), 0, 0),
                               memory_space=pltpu.SMEM),
                  smem_tile((2, ROW_TILE)), smem_tile((2, ROW_TILE)),
                  pl.BlockSpec((ROW_TILE, d), tile),
                  pl.BlockSpec((1, 6, d), mod_map),
                  pl.BlockSpec((1, d), lambda i: (0, 0)),
                  pl.BlockSpec(memory_space=pl.ANY)],
        out_specs=pl.BlockSpec((ROW_TILE, d), lambda i: (i, 0)),
        out_shape=jax.ShapeDtypeStruct((n_tiles * ROW_TILE, d), F32),
        scratch_shapes=[pltpu.VMEM((2, STAGE_TOKENS * TOKEN_ROWS, LANES), F32),
                        pltpu.VMEM((ROW_TILE * TOKEN_ROWS, LANES), F32),
                        pltpu.SemaphoreType.DMA((2,))],
        compiler_params=_params(1),
        name=f"combine{layer}_{tile0}",
    )(block_tab, block_tab, slots, weights, x1, mods, final_w, ys)


def _moe(h2, slots, weights, tile_tab, w1, w3, w2, layer):
    t = h2.shape[0] // TOKEN_ROWS
    n_tiles = t // ROW_TILE
    extra_tiles = N_EXPERTS + _cdiv(N_EXPERTS * MOVE_BLOCK, EXPERT_TILE)
    n_rows = 2 * t + extra_tiles * EXPERT_TILE
    nt = n_rows // EXPERT_TILE
    tab = tile_tab[:, N_GROUPS:N_GROUPS + N_EXPERTS, 0:3].transpose(0, 2, 1).astype(I32)
    cnt = tab[-1, 0] + tab[-1, 1]
    tight = _cdiv(cnt, EXPERT_TILE) * EXPERT_TILE
    padded = jnp.where(cnt > 0, _cdiv(cnt + MOVE_BLOCK - 1, EXPERT_TILE) * EXPERT_TILE, 0)
    ends = jnp.cumsum(padded)
    offsets = ends - padded
    tails = jnp.where(cnt > 0, ends - EXPERT_TILE, -1)
    tails2 = jnp.where(padded > tight, ends - 2 * EXPERT_TILE, -1)
    used = ends[-1] // EXPERT_TILE
    tile_start = jnp.arange(nt, dtype=I32) * EXPERT_TILE
    unused = (used + jnp.arange(extra_tiles, dtype=I32)) * EXPERT_TILE
    zero_tiles = jnp.concatenate([tails, tails2, jnp.where(unused < n_rows, unused, -1)])
    zero_tiles = jnp.where(zero_tiles >= 0, zero_tiles * TOKEN_ROWS, -1).astype(I32)
    tile_src = jnp.minimum(jnp.arange(nt, dtype=I32), used - 1)
    tile_expert = jnp.sum((tile_src * EXPERT_TILE)[:, None] >= ends[None, :], axis=1).astype(I32)
    tile_rows = jnp.where(tile_start < ends[-1],
                          jnp.clip(cnt[tile_expert] - (tile_start - offsets[tile_expert]),
                                   0, EXPERT_TILE), 0).astype(I32)
    n_blocks = _cdiv(tab[:, 0], MOVE_BLOCK)
    blocks_through = jnp.cumsum(n_blocks, axis=1)
    k = jnp.arange(MAX_BLOCKS, dtype=I32)
    owner = jnp.sum(blocks_through[:, None, :] <= k[None, :, None], axis=2)
    is_owner = owner[:, :, None] == jnp.arange(N_EXPERTS, dtype=I32)[None, None, :]
    pick = lambda v: jnp.sum(jnp.where(is_owner, v[:, None, :], 0), axis=2)
    run_first = pick(offsets[None, :] + tab[:, 1])
    block_in_run = k[None, :] - pick(blocks_through - n_blocks)
    sorted_row = (run_first + block_in_run * MOVE_BLOCK) * TOKEN_ROWS
    block_tab = jnp.concatenate(
        [sorted_row, blocks_through[:, -1:],
         jnp.zeros((n_tiles, LANES - MAX_BLOCKS - 1), I32)], axis=1).astype(I32)[:, None, :]
    hs = _dispatch(zero_tiles, block_tab, slots, h2, n_rows)
    ys = _experts(tile_expert, tile_src, tile_rows, hs, w1, w3, w2, layer)
    return ys, (block_tab, slots, weights)


def _rope(x, cos, sin_signed):
    lane = lax.broadcasted_iota(I32, (x.shape[0], LANES), 1)
    low = (lane % 32) < 16
    outs = []
    for j in range(x.shape[1] // LANES):
        xb = x[:, j * LANES:(j + 1) * LANES]
        partner = jnp.where(low, pltpu.roll(xb, LANES - 16, 1), pltpu.roll(xb, 16, 1))
        outs.append(xb * cos + partner * sin_signed)
    return jnp.concatenate(outs, axis=1)


def _inproj1_prompt_kernel(x_ref, mod_ref, nw_ref, w_ref, q_ref, k_ref, v_ref, kc_ref, vc_ref):
    d = D_MODEL
    h = _modulate(x_ref[...], nw_ref[...], mod_ref[0, 0:1, :], mod_ref[0, 1:2, :]).astype(BF16)
    q_ref[...] = (_dot(h, w_ref[:, 0:d]) * (DIFF_HD ** -0.5)).astype(BF16)
    k = _dot(h, w_ref[:, d:2 * d])
    v = _dot(h, w_ref[:, 2 * d:3 * d])
    k_ref[...] = k.astype(BF16)
    v_ref[...] = v.astype(BF16)
    for b in range(k.shape[0] // ATTN_TILE):
        kc_ref[b * d:(b + 1) * d, :] = k[b * ATTN_TILE:(b + 1) * ATTN_TILE].T
    _to_token_major(vc_ref, v)


def _inproj1_sample_kernel(x_ref, mod_ref, nw_ref, w_ref, cos_ref, sin_ref, q_ref, k_ref, v_ref):
    d = D_MODEL
    h = _modulate(x_ref[...], nw_ref[...], mod_ref[0, 0:1, :], mod_ref[0, 1:2, :]).astype(BF16)
    cos, sin = cos_ref[...], sin_ref[...]
    q_ref[...] = (_rope(_dot(h, w_ref[:, 0:d]), cos, sin) * (DIFF_HD ** -0.5)).astype(BF16)
    k_ref[...] = _rope(_dot(h, w_ref[:, d:2 * d]), cos, sin).astype(BF16)
    v_ref[...] = _dot(h, w_ref[:, 2 * d:3 * d]).astype(BF16)


def _inproj1(x, mods, norm_w, w_bf16, n_prompt_tiles, n_sample_tiles, tiles_per_sample,
             cos_t, sin_t):
    d = D_MODEL
    rows = INPROJ1_TILE
    npt, nst = n_prompt_tiles, n_sample_tiles
    common = [pl.BlockSpec((1, d), lambda i: (0, 0)), pl.BlockSpec((d, 3 * d), lambda i: (0, 0))]
    tile = lambda i: (i, 0)
    out_specs = tuple(pl.BlockSpec((rows, d), tile) for _ in range(3))
    qp, kp, vp, k_cache, v_cache = pl.pallas_call(
        _inproj1_prompt_kernel,
        grid=(npt,),
        in_specs=[pl.BlockSpec((rows, d), tile),
                  pl.BlockSpec((1, 6, d), lambda i: (8, 0, 0))] + common,
        out_specs=out_specs + (pl.BlockSpec((rows // ATTN_TILE * d, ATTN_TILE), tile),
                               pl.BlockSpec((rows * TOKEN_ROWS, LANES), tile)),
        out_shape=tuple(jax.ShapeDtypeStruct((npt * rows, d), BF16) for _ in range(3))
        + (jax.ShapeDtypeStruct((npt * rows // ATTN_TILE * d, ATTN_TILE), F32),
           jax.ShapeDtypeStruct((npt * rows * TOKEN_ROWS, LANES), F32)),
        compiler_params=_params(1),
        name="inproj1_prompt",
    )(x, mods, norm_w, w_bf16)
    rope_tile = lambda i: (i % tiles_per_sample, 0)
    qs, ks, vs = pl.pallas_call(
        _inproj1_sample_kernel,
        grid=(nst,),
        in_specs=[pl.BlockSpec((rows, d), lambda i: (npt + i, 0)),
                  pl.BlockSpec((1, 6, d), lambda i: (8 + 1 + i // tiles_per_sample, 0, 0))]
        + common + [pl.BlockSpec((rows, LANES), rope_tile),
                    pl.BlockSpec((rows, LANES), rope_tile)],
        out_specs=out_specs,
        out_shape=tuple(jax.ShapeDtypeStruct((nst * rows, d), BF16) for _ in range(3)),
        compiler_params=_params(1),
        name="inproj1_sample",
    )(x, mods, norm_w, w_bf16, cos_t, sin_t)
    return (qp, kp, vp), (qs, ks, vs), (k_cache, v_cache)


def _rope_tables(n_tok):
    half = DIFF_HD // 4
    pos = np.arange(n_tok)
    lane = np.arange(LANES)
    sub = lane % DIFF_HD
    p = np.where(sub[None, :] < DIFF_HD // 2, (pos // GRID_W)[:, None], (pos % GRID_W)[:, None])
    inv = jnp.asarray(ROPE_THETA, F32) ** (-jnp.asarray(sub % half, F32) / half)
    ang = jnp.asarray(p, F32) * inv[None, :]
    sign = np.where((lane % (2 * half)) < half, -1.0, 1.0).astype(np.float32)
    return jnp.cos(ang), jnp.sin(ang) * sign[None, :]


def _diffattn_kernel(*refs, has_cache, lam_init, seqs):
    if has_cache:
        q_ref, k_ref, v_ref, ck_ref, cv_ref, lam_ref, sw_ref, o_ref = refs
    else:
        q_ref, k_ref, v_ref, lam_ref, sw_ref, o_ref = refs
    hd2 = 2 * DIFF_HD
    lv = lam_ref[...]
    lam = (jnp.exp(jnp.sum(lv[0:1] * lv[1:2], axis=1, keepdims=True))
           - jnp.exp(jnp.sum(lv[2:3] * lv[3:4], axis=1, keepdims=True)) + lam_init)
    nq, nk = q_ref.shape[0] // seqs, k_ref.shape[0] // seqs
    lane = lax.broadcasted_iota(I32, (nq, hd2), 1)
    for sq, h in ((sq, h) for sq in range(seqs) for h in range(DIFF_HEADS)):
        cols = slice(h * hd2, (h + 1) * hd2)
        q_rows, k_rows = slice(sq * nq, (sq + 1) * nq), slice(sq * nk, (sq + 1) * nk)
        q = q_ref[q_rows, cols]
        zero = jnp.zeros_like(q)
        k_new = k_ref[k_rows, cols].astype(BF16)
        values = [v_ref[k_rows, cols].astype(BF16)]
        if has_cache:
            past = ck_ref.shape[1]
            k_past_t = ck_ref[cols, :].astype(BF16)
            values.append(cv_ref[pl.ds(h, past, stride=DIFF_HEADS), :].astype(BF16))
        o = None
        for c in range(2):
            qc = jnp.where((lane < DIFF_HD) == (c == 0), q, zero)
            s = [_dot_nt(qc, k_new)]
            if has_cache:
                s.append(_dot(qc, k_past_t))
            mx = functools.reduce(jnp.maximum, [jnp.max(si, axis=1, keepdims=True) for si in s])
            e = [jnp.exp(si - mx) for si in s]
            z = functools.reduce(jnp.add, [jnp.sum(ei, axis=1, keepdims=True) for ei in e])
            pv = functools.reduce(jnp.add, [_dot(ei.astype(BF16), v) for ei, v in zip(e, values)])
            pv = pv * (1.0 / z)
            o = pv if c == 0 else o - lam * pv
        o_ref[q_rows, cols] = ((_rms(o) * sw_ref[...]) * (1.0 - lam_init)).astype(BF16)


def _diffattn(q, k, v, lam_vecs, subln_w, batch, seq_len, q_block, lam_init, cache=None, seqs=1):
    d = D_MODEL
    nq = seq_len // q_block
    has_cache = cache is not None
    assert seqs == 1 or (nq == 1 and not has_cache and batch % seqs == 0)
    kv_spec = pl.BlockSpec((seqs * seq_len, d), lambda b, qi: (b, 0))
    in_specs = [pl.BlockSpec((seqs * q_block, d), lambda b, qi: (b * nq + qi, 0)), kv_spec, kv_spec]
    args = [q, k, v]
    if has_cache:
        past = cache[0].shape[1]
        in_specs += [pl.BlockSpec((d, past), lambda b, qi: (b, 0)),
                     pl.BlockSpec((past * DIFF_HEADS, 2 * DIFF_HD), lambda b, qi: (b, 0))]
        args += list(cache)
    in_specs += [pl.BlockSpec((4, DIFF_HD), lambda b, qi: (0, 0)),
                 pl.BlockSpec((1, 2 * DIFF_HD), lambda b, qi: (0, 0))]
    args += [lam_vecs, subln_w]
    return pl.pallas_call(
        functools.partial(_diffattn_kernel, has_cache=has_cache, lam_init=lam_init, seqs=seqs),
        grid=(batch // seqs, nq),
        in_specs=in_specs,
        out_specs=pl.BlockSpec((seqs * q_block, d), lambda b, qi: (b * nq + qi, 0)),
        out_shape=jax.ShapeDtypeStruct((batch * seq_len, d), BF16),
        compiler_params=_params(2),
        name="diffattn_cache" if has_cache else "diffattn",
    )(*args)


def _router_weights(router_group, router_expert):
    w = jnp.concatenate([router_group, router_expert], axis=1)
    return jnp.pad(w, ((0, 0), (0, LANES - w.shape[1]))).astype(BF16)


def _inproj0_weights(w_in):
    gq, gk, gv, gg, gaf, gab, hq, hff, hfb, hi, hg = jnp.split(
        w_in.astype(BF16), [256, 512, 1024, 1536, 1552, 1568, 1824, 2080, 2336, 2848], axis=1)
    w = jnp.concatenate([gq, gk, gv, gg, hq, hff, hfb, hi, hg, gaf, gab], axis=1)
    return jnp.pad(w, ((0, 0), (0, AB_COLS - w.shape[1])))


def kernel(x_prompt, x_sample, state_gla, state_hgrn, cache_diff_k, cache_diff_v, c, c_ctx,
           w_ada, b_ada, norm1_w, norm2_w, w_in_ab, gla_a2, gla_a_bias, hgrn_lb, gla_onorm_w,
           hgrn_onorm_w, w_out_ab, w_in_c, lam_q1, lam_k1, lam_q2, lam_k2, diff_subln_w, w_out_c,
           router_group, router_expert, moe_w1, moe_w3, moe_w2, final_norm_w):
    bp, lp, d = x_prompt.shape
    bs, ls, _ = x_sample.shape
    depth = w_ada.shape[0]
    assert depth == 2 and d == D_MODEL and bs <= 7
    tp, ts = bp * lp, bs * ls
    npt, nst = tp // ROW_TILE, ts // ROW_TILE
    tps = ls // ROW_TILE
    xp = x_prompt.reshape(tp, d)
    xs = x_sample.reshape(ts, d)

    cond8 = jnp.concatenate([c_ctx[None, :], c, jnp.zeros((7 - bs, d), F32)], axis=0)
    mods = _adaln(cond8, w_ada, b_ada).reshape(depth * 8, 6, d)

    proj = _inproj0(xp, xs, mods, norm1_w[0:1], _inproj0_weights(w_in_ab[0]), ls)
    a_bias = gla_a_bias[0][:, None, :]
    scan_args = (gla_a2[0], a_bias, hgrn_lb, gla_onorm_w[0:1], hgrn_onorm_w[0:1])
    mixed_p, new_state_gla, new_state_hgrn = _scan(proj, 0, bp, lp, *scan_args)
    s0 = jnp.concatenate([state_gla[:, 0], state_hgrn[:, 0]], axis=2).swapaxes(-1, -2)
    s0 = s0.reshape(bs, 2, SCAN_PAIRS, 2, HEAD_DV, HEAD_DK)
    zero = jnp.zeros_like(s0[:, :, :, 0])
    s0 = jnp.concatenate([jnp.concatenate([s0[:, :, :, 0], zero], axis=-1),
                          jnp.concatenate([zero, s0[:, :, :, 1]], axis=-1)], axis=-2)
    mixed_s = _scan(proj, tp, bs, ls, *scan_args, s0=s0)

    wr = _router_weights(router_group[0], router_expert[0])
    x1, *routed = _post((xp, xs), mixed_p, mixed_s, mods, 0, norm2_w[0:1],
                        w_out_ab[0].astype(BF16), wr, tps)
    ys, tables = _moe(*routed, moe_w1, moe_w3, moe_w2, 0)
    x2 = _combine(*tables, x1, mods, 0, final_norm_w[None, :], ys, 0, npt + nst, npt, tps, False)

    lam_init = 0.8 - 0.6 * math.exp(-0.3 * 1)
    cos_t, sin_t = _rope_tables(ls)
    (qp, kp, vp), (qs, ks, vs), (k_cache, v_cache) = _inproj1(
        x2, mods, norm1_w[1:2], w_in_c[0].astype(BF16), tp // INPROJ1_TILE, ts // INPROJ1_TILE,
        ls // INPROJ1_TILE, cos_t, sin_t)
    lam_vecs = jnp.stack([lam_q1[0], lam_k1[0], lam_q2[0], lam_k2[0]])
    att_p = _diffattn(qp, kp, vp, lam_vecs, diff_subln_w[0:1], bp, lp, lp, lam_init,
                      seqs=PROMPT_SEQS_PER_STEP)
    past = cache_diff_k.shape[2]
    assert lp == ATTN_TILE and DIFF_HEADS == TOKEN_ROWS
    cache = (cache_diff_k[:, 0].transpose(0, 2, 3, 4, 1).reshape(bs * d, past),
             cache_diff_v[:, 0].reshape(bs * past * DIFF_HEADS, 2 * DIFF_HD))
    att_s = _diffattn(qs, ks, vs, lam_vecs, diff_subln_w[0:1], bs, ls, SAMPLE_Q_BLOCK, lam_init,
                      cache)

    wr = _router_weights(router_group[1], router_expert[1])
    x3, *routed = _post((x2,), att_p, att_s, mods, 1, norm2_w[1:2],
                        w_out_c[0].astype(BF16), wr, tps)
    ys, tables = _moe(*routed, moe_w1, moe_w3, moe_w2, 1)
    fw = final_norm_w[None, :]
    y_p = _combine(*tables, x3, mods, 1, fw, ys, 0, npt, npt, tps, True)
    y_s = _combine(*tables, x3, mods, 1, fw, ys, npt, nst, npt, tps, True)

    return (y_p.reshape(bp, lp, d), y_s.reshape(bs, ls, d), new_state_gla, new_state_hgrn,
            k_cache.reshape(bp, 1, DIFF_HEADS, 2, DIFF_HD, lp).transpose(0, 1, 5, 2, 3, 4),
            v_cache.reshape(bp, 1, lp, DIFF_HEADS, 2 * DIFF_HD))
```

```python
import functools
import math

import jax
import jax.numpy as jnp
import numpy as np
from jax import lax
from jax.experimental import pallas as pl
from jax.experimental.pallas import tpu as pltpu

F32 = jnp.float32
BF16 = jnp.bfloat16
I32 = jnp.int32

D_MODEL = 1024
GLA_HEADS = 4
HGRN_HEADS = 4
SCAN_HEADS = GLA_HEADS + HGRN_HEADS
SCAN_PAIRS = SCAN_HEADS // 2
HEAD_DK = 64
HEAD_DV = 128
GATE_RANK = 16
GLA_GATE_NORM = 16.0
DIFF_HEADS = 8
DIFF_HD = 64
GRID_W = 64
ROPE_THETA = 10000.0
N_GROUPS = 4
EXPERTS_PER_GROUP = 8
N_EXPERTS = N_GROUPS * EXPERTS_PER_GROUP
MOE_HIDDEN = 512
EPS = 1e-6
LANES = 128
TOKEN_ROWS = D_MODEL // LANES
NEG_BIG = -1e30
ROUTER_ROWS = 48

ROW_TILE = 512
ATTN_TILE = 256
SAMPLE_Q_BLOCK = 512
PROMPT_SEQS_PER_STEP = 2
ADA_TILE = 1536
INPROJ0_TILE = 512
INPROJ1_TILE = 512
SCAN_CHUNK = 64
EXPERT_TILE = 512
MOVE_BLOCK = 16
STAGE_TOKENS = 2 * ROW_TILE + N_EXPERTS * MOVE_BLOCK
MAX_BLOCKS = STAGE_TOKENS // MOVE_BLOCK
VMEM_LIMIT = 56 * 1024 * 1024
SCAN_INPUT_DOUBLE_BUFFER_BYTES = 16 * 1024 * 1024

_C_GQ, _C_GK, _C_GV, _C_GG = 0, 256, 512, 1024
_C_HQ, _C_HFF, _C_HFB, _C_HI, _C_HG = 1536, 1792, 2048, 2304, 2816
_C_GAF, _C_GAB = 3328, 3344
AB_COLS = 3456


def _params(n_axes, vmem=VMEM_LIMIT):
    return pltpu.CompilerParams(dimension_semantics=("arbitrary",) * n_axes,
                                vmem_limit_bytes=vmem)


def _cdiv(a, b):
    return (a + b - 1) // b


def _dot(a, b):
    return jnp.dot(a, b, preferred_element_type=F32)


def _dot_nt(a, b):
    return lax.dot_general(a, b, (((1,), (1,)), ((), ())), preferred_element_type=F32)


def _dot_tn(a, b):
    return lax.dot_general(a, b, (((0,), (0,)), ((), ())), preferred_element_type=F32)


def _split_bf16(x):
    hi = x.astype(BF16)
    lo = (x - hi.astype(F32)).astype(BF16)
    return hi, lo


def _silu(x):
    return x * jax.nn.sigmoid(x)


def _log_sigmoid(x):
    return jnp.minimum(x, 0.0) - jnp.log(1.0 + jnp.exp(-jnp.abs(x)))


def _rms(x):
    return x * lax.rsqrt(jnp.mean(x * x, axis=-1, keepdims=True) + EPS)


def _modulate(x, norm_w, shift, scale):
    return (_rms(x) * norm_w) * (1.0 + scale) + shift


def _to_token_major(dst_ref, x, row0=0):
    n = x.shape[0]
    for s in range(TOKEN_ROWS):
        dst_ref[pl.ds(row0 + s, n, stride=TOKEN_ROWS), :] = x[:, s * LANES:(s + 1) * LANES]


def _from_token_major(src_ref, n, row0=0):
    return jnp.concatenate([src_ref[pl.ds(row0 + s, n, stride=TOKEN_ROWS), :]
                            for s in range(TOKEN_ROWS)], axis=1)


def _ada_kernel(c_ref, w_ref, b_ref, o_ref):
    s = _silu(c_ref[...])
    o_ref[0] = _dot(s.astype(BF16), w_ref[0].astype(BF16)) + b_ref[0]


def _adaln(cond8, w_ada, b_ada):
    depth, d, n = w_ada.shape
    tn = ADA_TILE
    return pl.pallas_call(
        _ada_kernel,
        grid=(depth, n // tn),
        in_specs=[pl.BlockSpec((8, d), lambda l, j: (0, 0)),
                  pl.BlockSpec((1, d, tn), lambda l, j: (l, 0, j)),
                  pl.BlockSpec((1, 1, tn), lambda l, j: (l, 0, j))],
        out_specs=pl.BlockSpec((1, 8, tn), lambda l, j: (l, 0, j)),
        out_shape=jax.ShapeDtypeStruct((depth, 8, n), F32),
        compiler_params=_params(2),
        name="adaln",
    )(cond8, w_ada, b_ada.reshape(depth, 1, n))


def _mod_row(i, layer, n_prompt_tiles, tiles_per_sample):
    r = jnp.where(i < n_prompt_tiles, 0, 1 + (i - n_prompt_tiles) // tiles_per_sample)
    return layer * 8 + r


def _inproj0_kernel(xp_ref, xs_ref, mod_ref, nw_ref, w_ref, o_ref, *, n_prompt_tiles):
    i = pl.program_id(0)
    x = jnp.where(i < n_prompt_tiles, xp_ref[...], xs_ref[...])
    h = _modulate(x, nw_ref[...], mod_ref[0, 0:1, :], mod_ref[0, 1:2, :])
    o_ref[...] = _dot(h.astype(BF16), w_ref[...])


def _inproj0(xp, xs, mods, norm_w, w_bf16, sample_len):
    tp, d = xp.shape
    ts = xs.shape[0]
    n = w_bf16.shape[1]
    tile = INPROJ0_TILE
    npt, nst = tp // tile, ts // tile
    mod_map = lambda i: (_mod_row(i, 0, npt, sample_len // tile), 0, 0)
    return pl.pallas_call(
        functools.partial(_inproj0_kernel, n_prompt_tiles=npt),
        grid=(npt + nst,),
        in_specs=[pl.BlockSpec((tile, d), lambda i: (jnp.minimum(i, npt - 1), 0)),
                  pl.BlockSpec((tile, d), lambda i: (jnp.maximum(i - npt, 0), 0)),
                  pl.BlockSpec((1, 6, d), mod_map),
                  pl.BlockSpec((1, d), lambda i: (0, 0)),
                  pl.BlockSpec((d, n), lambda i: (0, 0))],
        out_specs=pl.BlockSpec((tile, n), lambda i: (i, 0)),
        out_shape=jax.ShapeDtypeStruct((tp + ts, n), F32),
        compiler_params=_params(1),
        name="inproj0",
    )(xp, xs, mods, norm_w, w_bf16)


def _scan_kernel(*refs, seq_len, has_state):
    if has_state:
        (p_ref, a2_ref, ab_ref, lb_ref, ong_ref, onh_ref, s0_ref, mixed_ref, *scratch) = refs
        sfin_ref = None
    else:
        (p_ref, a2_ref, ab_ref, lb_ref, ong_ref, onh_ref, mixed_ref, sg_ref, sh_ref,
         *scratch) = refs
        s0_ref = None
        sfin_ref = (sg_ref, sh_ref)
    (qi_f, ki_f, qo_f, ko_f, qi_b, ki_b, qo_b, ko_b,
     vv, dec_f, dec_b, o_f, o_b, st_f, st_b) = scratch
    C = SCAN_CHUNK
    n_chunks = seq_len // C
    gqk = GLA_HEADS * HEAD_DK

    row = lax.broadcasted_iota(I32, (C, C), 0)
    col = lax.broadcasted_iota(I32, (C, C), 1)
    lower = col <= row
    upper = col >= row
    tri_lo = jnp.where(lower, 1.0, 0.0).astype(BF16)
    tri_up = jnp.where(upper, 1.0, 0.0).astype(BF16)

    lbp = lb_ref[...]
    lb_max = jnp.maximum(lbp[0], lbp[1])
    lb_e0 = jnp.exp(lbp[0] - lb_max)
    lb_e1 = jnp.exp(lbp[1] - lb_max)
    lb = lb_e0 / (lb_e0 + lb_e1)

    def cumsum_chunk(tri, la):
        hi, lo = _split_bf16(la)
        return _dot(tri, hi) + _dot(tri, lo)

    def prep(n, carry):
        r0 = pl.multiple_of(n * C, C)
        rows = pl.ds(r0, C)
        gq = p_ref[rows, _C_GQ:_C_GQ + gqk] * (HEAD_DK ** -0.5)
        gk = p_ref[rows, _C_GK:_C_GK + gqk]
        hq = _silu(p_ref[rows, _C_HQ:_C_HQ + gqk]) * (HEAD_DK ** -0.5)
        for d_i, (qi_s, ki_s, qo_s, ko_s, dec_s, tri, last, mid) in enumerate(
                ((qi_f, ki_f, qo_f, ko_f, dec_f, tri_lo, C - 1, C // 2 - 1),
                 (qi_b, ki_b, qo_b, ko_b, dec_b, tri_up, 0, C // 2))):
            c_ga = _C_GAF if d_i == 0 else _C_GAB
            c_hf = _C_HFF if d_i == 0 else _C_HFB
            ga = p_ref[rows, c_ga:c_ga + GATE_RANK]
            xg = _dot(ga.astype(BF16), a2_ref[d_i].astype(BF16)) + ab_ref[d_i]
            la_g = _log_sigmoid(xg) / GLA_GATE_NORM
            f = lb[d_i:d_i + 1, :] + (1.0 - lb[d_i:d_i + 1, :]) * jax.nn.sigmoid(
                p_ref[rows, c_hf:c_hf + gqk])
            la_h = jnp.log(f)
            for q, k, la, c0 in ((gq, gk, la_g, 0), (hq, 1.0 - f, la_h, gqk)):
                b = cumsum_chunk(tri, la)
                b_mid, b_end = b[mid:mid + 1, :], b[last:last + 1, :]
                cs = slice(c0, c0 + gqk)
                qi_s[rows, cs] = (q * jnp.exp(b - b_mid)).astype(BF16)
                ki_s[rows, cs] = (k * jnp.exp(b_mid - b)).astype(BF16)
                qo_s[rows, cs] = (q * jnp.exp(b)).astype(BF16)
                ko_s[rows, cs] = (k * jnp.exp(b_end - b)).astype(BF16)
                dec_s[n, :, cs] = jnp.exp(b_end)
        gv_cols = GLA_HEADS * HEAD_DV
        vv[rows, 0:gv_cols] = p_ref[rows, _C_GV:_C_GV + gv_cols].astype(BF16)
        vv[rows, gv_cols:] = p_ref[rows, _C_HI:_C_HI + HGRN_HEADS * HEAD_DV].astype(BF16)
        return carry

    lax.fori_loop(0, n_chunks, prep, 0, unroll=8)

    for p in range(SCAN_PAIRS):
        if has_state:
            st_f[p] = s0_ref[0, 0, p]
            st_b[p] = s0_ref[0, 1, p]
        else:
            st_f[p] = jnp.zeros((2 * HEAD_DV, 2 * HEAD_DK), F32)
            st_b[p] = jnp.zeros((2 * HEAD_DV, 2 * HEAD_DK), F32)

    first_head = lax.broadcasted_iota(I32, (C, 2 * HEAD_DK), 1) < HEAD_DK
    row2 = lax.broadcasted_iota(I32, (2 * C, C), 0) % C
    col2 = lax.broadcasted_iota(I32, (2 * C, C), 1)
    lower2 = col2 <= row2
    upper2 = col2 >= row2

    def per_head_rows(x):
        z = jnp.zeros_like(x)
        return jnp.concatenate([jnp.where(first_head, x, z), jnp.where(first_head, z, x)], axis=0)

    def put_out(o_ref, rows, p, res):
        c0 = p * 2 * HEAD_DV
        o_ref[rows, c0:c0 + HEAD_DV] = res[0:C, 0:HEAD_DV]
        o_ref[rows, c0 + HEAD_DV:c0 + 2 * HEAD_DV] = res[C:2 * C, HEAD_DV:2 * HEAD_DV]

    def sweep(n, carry):
        m = n_chunks - 1 - n
        rows = pl.ds(pl.multiple_of(n * C, C), C)
        rows_m = pl.ds(pl.multiple_of(m * C, C), C)
        decay_f, decay_b = dec_f[n], dec_b[m]
        for p in range(SCAN_PAIRS):
            ks = slice(p * 2 * HEAD_DK, (p + 1) * 2 * HEAD_DK)
            vs = slice(p * 2 * HEAD_DV, (p + 1) * 2 * HEAD_DV)
            vh = vv[rows, vs]
            s_f = st_f[p]
            sc = (jnp.where(lower2, _dot_nt(per_head_rows(qi_f[rows, ks]), ki_f[rows, ks]), 0.0)
                  + jnp.where(upper2, _dot_nt(per_head_rows(qi_b[rows, ks]), ki_b[rows, ks]), 0.0))
            put_out(o_f, rows, p, _dot_nt(per_head_rows(qo_f[rows, ks]), s_f.astype(BF16))
                    + _dot(sc.astype(BF16), vh))
            st_f[p] = decay_f[:, ks] * s_f + _dot_tn(vh, ko_f[rows, ks])
            s_b = st_b[p]
            put_out(o_b, rows_m, p, _dot_nt(per_head_rows(qo_b[rows_m, ks]), s_b.astype(BF16)))
            st_b[p] = decay_b[:, ks] * s_b + _dot_tn(vv[rows_m, vs], ko_b[rows_m, ks])
        return carry

    lax.fori_loop(0, n_chunks, sweep, 0, unroll=8)

    def finish(n, carry):
        rows = pl.ds(pl.multiple_of(n * C, C), C)
        for h in range(SCAN_HEADS):
            vs = slice(h * HEAD_DV, (h + 1) * HEAD_DV)
            if h < GLA_HEADS:
                gate = p_ref[rows, _C_GG + h * HEAD_DV:_C_GG + (h + 1) * HEAD_DV]
                onw = ong_ref[...]
            else:
                hh = h - GLA_HEADS
                gate = p_ref[rows, _C_HG + hh * HEAD_DV:_C_HG + (hh + 1) * HEAD_DV]
                onw = onh_ref[...]
            o = o_f[rows, vs] + o_b[rows, vs]
            mixed_ref[rows, vs] = ((_rms(o) * onw) * _silu(gate)).astype(BF16)
        return carry

    lax.fori_loop(0, n_chunks, finish, 0, unroll=4)

    if sfin_ref is not None:
        for d_i, st in enumerate((st_f, st_b)):
            for p in range(SCAN_PAIRS):
                s_pair = st[p].T
                out_ref = sfin_ref[(2 * p) // GLA_HEADS]
                h0 = (2 * p) % GLA_HEADS
                out_ref[0, 0, d_i, h0] = s_pair[0:HEAD_DK, 0:HEAD_DV]
                out_ref[0, 0, d_i, h0 + 1] = s_pair[HEAD_DK:2 * HEAD_DK, HEAD_DV:2 * HEAD_DV]


def _scan(p, row0, batch, seq_len, a2, a_bias, lb, onorm_g, onorm_h, s0=None):
    n = p.shape[1]
    assert row0 % seq_len == 0
    blk0 = row0 // seq_len
    has_state = s0 is not None
    n_chunks = seq_len // SCAN_CHUNK
    assert GLA_HEADS == HGRN_HEADS and GLA_HEADS % 2 == 0
    st_shape = (1, 1, 2, GLA_HEADS, HEAD_DK, HEAD_DV)
    pair_shape = (SCAN_PAIRS, 2 * HEAD_DV, 2 * HEAD_DK)
    p_mode = dict(pipeline_mode=pl.Buffered(1)) if seq_len * n * 4 > SCAN_INPUT_DOUBLE_BUFFER_BYTES else {}
    in_specs = [pl.BlockSpec((seq_len, n), lambda b: (blk0 + b, 0), **p_mode),
                pl.BlockSpec(a2.shape, lambda b: (0, 0, 0)),
                pl.BlockSpec(a_bias.shape, lambda b: (0, 0, 0)),
                pl.BlockSpec(lb.shape, lambda b: (0, 0, 0)),
                pl.BlockSpec((1, HEAD_DV), lambda b: (0, 0)),
                pl.BlockSpec((1, HEAD_DV), lambda b: (0, 0))]
    args = [p, a2, a_bias, lb, onorm_g, onorm_h]
    mixed_shape = jax.ShapeDtypeStruct((batch * seq_len, D_MODEL), BF16)
    mixed_spec = pl.BlockSpec((seq_len, D_MODEL), lambda b: (b, 0))
    if has_state:
        in_specs.append(pl.BlockSpec((1, 2) + pair_shape, lambda b: (b, 0, 0, 0, 0)))
        args.append(s0)
        out_shape, out_specs = mixed_shape, mixed_spec
    else:
        st_struct = jax.ShapeDtypeStruct((batch,) + st_shape[1:], F32)
        st_spec = pl.BlockSpec(st_shape, lambda b: (b, 0, 0, 0, 0, 0))
        out_shape = (mixed_shape, st_struct, st_struct)
        out_specs = (mixed_spec, st_spec, st_spec)
    qk_cols = SCAN_HEADS * HEAD_DK
    scratch = [pltpu.VMEM((seq_len, qk_cols), BF16) for _ in range(8)]
    scratch += [pltpu.VMEM((seq_len, D_MODEL), BF16),
                pltpu.VMEM((n_chunks, 1, qk_cols), F32), pltpu.VMEM((n_chunks, 1, qk_cols), F32),
                pltpu.VMEM((seq_len, D_MODEL), F32), pltpu.VMEM((seq_len, D_MODEL), F32),
                pltpu.VMEM(pair_shape, F32), pltpu.VMEM(pair_shape, F32)]
    return pl.pallas_call(
        functools.partial(_scan_kernel, seq_len=seq_len, has_state=has_state),
        grid=(batch,),
        in_specs=in_specs, out_specs=out_specs, out_shape=out_shape,
        scratch_shapes=scratch,
        compiler_params=_params(1),
        name="scan_state" if has_state else "scan_fresh",
    )(*args)


def _post_kernel(*refs, split_x, n_prompt_tiles):
    if split_x:
        xp_ref, xs_ref = refs[0], refs[1]
        refs = refs[2:]
    else:
        x_ref = refs[0]
        refs = refs[1:]
    (mp_ref, ms_ref, mod_ref, nw_ref, wo_ref, wr_ref,
     x1_ref, h2_ref, slot_ref, wgt_ref, tab_ref, carry, earlier) = refs
    i = pl.program_id(0)
    is_prompt = i < n_prompt_tiles
    if split_x:
        x = jnp.where(is_prompt, xp_ref[...], xs_ref[...])
    else:
        x = x_ref[...]
    mixed = jnp.where(is_prompt, mp_ref[...], ms_ref[...])
    x1 = x + mod_ref[0, 2:3, :] * _dot(mixed, wo_ref[...])
    x1_ref[...] = x1
    h2 = _modulate(x1, nw_ref[...], mod_ref[0, 3:4, :], mod_ref[0, 4:5, :])
    _to_token_major(h2_ref, h2)

    logits = _dot(h2.astype(BF16), wr_ref[...])
    tm = logits.shape[0]
    lt = logits.T[0:ROUTER_ROWS]
    ridx = lax.broadcasted_iota(I32, (ROUTER_ROWS, tm), 0).astype(F32)

    def first_max(v):
        mx = jnp.max(v, axis=0, keepdims=True)
        idx = jnp.min(jnp.where(v == mx, ridx, float(ROUTER_ROWS)), axis=0, keepdims=True)
        return mx, idx

    gl = jnp.where(ridx < N_GROUPS, lt, NEG_BIG)
    gmax, gidx = first_max(gl)
    g_val = 1.0 / jnp.sum(jnp.exp(gl - gmax), axis=0, keepdims=True)
    lo = N_GROUPS + EXPERTS_PER_GROUP * gidx
    el = jnp.where((ridx >= lo) & (ridx < lo + EXPERTS_PER_GROUP), lt, NEG_BIG)
    emax, l1 = first_max(el)
    esum = jnp.sum(jnp.exp(el - emax), axis=0, keepdims=True)
    e2max, l2 = first_max(jnp.where(ridx == l1, NEG_BIG, el))
    p1 = 1.0 / esum
    p2 = jnp.exp(e2max - emax) / esum
    w1 = g_val * (p1 / (p1 + p2))
    w2 = g_val * (p2 / (p1 + p2))

    @pl.when(i == 0)
    def _():
        carry[...] = jnp.zeros_like(carry)
        t_row = lax.broadcasted_iota(I32, earlier.shape, 0)
        t_col = lax.broadcasted_iota(I32, earlier.shape, 1)
        earlier[...] = jnp.where(t_row < t_col, 1.0, 0.0).astype(BF16)

    sel1 = ridx == l1
    sel2 = ridx == l2
    onehot = jnp.where(sel1 | sel2, 1.0, 0.0)
    before = _dot(onehot.astype(BF16), earlier[...])
    count = jnp.sum(onehot, axis=1, keepdims=True)
    blocks = jnp.floor((count + (MOVE_BLOCK - 1.0)) * (1.0 / MOVE_BLOCK)) * MOVE_BLOCK
    r_row = lax.broadcasted_iota(I32, (ROUTER_ROWS, ROUTER_ROWS), 0)
    r_col = lax.broadcasted_iota(I32, (ROUTER_ROWS, ROUTER_ROWS), 1)
    lower_rows = jnp.where(r_col < r_row, 1.0, 0.0).astype(BF16)
    run_start = _dot(lower_rows,
                     jnp.broadcast_to(blocks, (ROUTER_ROWS, LANES)).astype(BF16))[:, 0:1]
    slot = before + run_start
    q1 = jnp.sum(jnp.where(sel1, slot, 0.0), axis=0, keepdims=True)
    q2 = jnp.sum(jnp.where(sel2, slot, 0.0), axis=0, keepdims=True)
    tab_lane = lax.broadcasted_iota(I32, (ROUTER_ROWS, LANES), 1)
    tab_ref[0] = jnp.where(tab_lane == 0, count,
                           jnp.where(tab_lane == 1, carry[...],
                                     jnp.where(tab_lane == 2, run_start, 0.0)))
    carry[...] = carry[...] + count

    slot_ref[0, 0:1, :] = (q1 * TOKEN_ROWS).astype(I32)
    slot_ref[0, 1:2, :] = (q2 * TOKEN_ROWS).astype(I32)
    wgt_ref[0, 0:1, :] = w1
    wgt_ref[0, 1:2, :] = w2


def _post(x_args, mixed_p, mixed_s, mods, layer, norm_w, w_out_bf16, w_router,
          tiles_per_sample):
    split_x = len(x_args) == 2
    tp, ts = mixed_p.shape[0], mixed_s.shape[0]
    t, d = tp + ts, D_MODEL
    npt, nst = tp // ROW_TILE, ts // ROW_TILE
    tile = lambda i: (i, 0)
    if split_x:
        x_specs = [pl.BlockSpec((ROW_TILE, d), lambda i: (jnp.minimum(i, npt - 1), 0)),
                   pl.BlockSpec((ROW_TILE, d), lambda i: (jnp.maximum(i - npt, 0), 0))]
    else:
        x_specs = [pl.BlockSpec((ROW_TILE, d), tile)]
    in_specs = x_specs + [
        pl.BlockSpec((ROW_TILE, d), lambda i: (jnp.minimum(i, npt - 1), 0)),
        pl.BlockSpec((ROW_TILE, d), lambda i: (jnp.maximum(i - npt, 0), 0)),
        pl.BlockSpec((1, 6, d), lambda i: (_mod_row(i, layer, npt, tiles_per_sample), 0, 0)),
        pl.BlockSpec((1, d), lambda i: (0, 0)),
        pl.BlockSpec((d, d), lambda i: (0, 0)),
        pl.BlockSpec((d, LANES), lambda i: (0, 0))]
    return pl.pallas_call(
        functools.partial(_post_kernel, split_x=split_x, n_prompt_tiles=npt),
        grid=(npt + nst,),
        in_specs=in_specs,
        out_specs=(pl.BlockSpec((ROW_TILE, d), tile),
                   pl.BlockSpec((ROW_TILE * TOKEN_ROWS, LANES), tile),
                   pl.BlockSpec((1, 2, ROW_TILE), lambda i: (i, 0, 0)),
                   pl.BlockSpec((1, 2, ROW_TILE), lambda i: (i, 0, 0)),
                   pl.BlockSpec((1, ROUTER_ROWS, LANES), lambda i: (i, 0, 0))),
        out_shape=(jax.ShapeDtypeStruct((t, d), F32),
                   jax.ShapeDtypeStruct((t * TOKEN_ROWS, LANES), F32),
                   jax.ShapeDtypeStruct((npt + nst, 2, ROW_TILE), I32),
                   jax.ShapeDtypeStruct((npt + nst, 2, ROW_TILE), F32),
                   jax.ShapeDtypeStruct((npt + nst, ROUTER_ROWS, LANES), F32)),
        scratch_shapes=[pltpu.VMEM((ROUTER_ROWS, LANES), F32),
                        pltpu.VMEM((ROW_TILE, ROW_TILE), BF16)],
        compiler_params=_params(1),
        name=f"post{layer}",
    )(*x_args, mixed_p, mixed_s, mods, norm_w, w_out_bf16, w_router)


def _for_blocks(tab_ref, fn):
    block_rows = MOVE_BLOCK * TOKEN_ROWS
    count = tab_ref[0, 0, MAX_BLOCKS]

    def call(k, parity):
        fn(pl.multiple_of(k * block_rows, block_rows),
           pl.multiple_of(tab_ref[0, 0, k], TOKEN_ROWS), parity)

    def body(k2, c):
        call(2 * k2, 0)

        @pl.when(2 * k2 + 1 < count)
        def _():
            call(2 * k2 + 1, 1)
        return c

    lax.fori_loop(0, _cdiv(count, 2), body, 0)


def _wait_blocks(tab_ref, copy):
    def body(k, c):
        copy.wait()
        return c

    lax.fori_loop(0, tab_ref[0, 0, MAX_BLOCKS], body, 0)


def _dispatch_kernel(zero_ref, tab_ref, prev_tab_ref, q_ref, h2_ref, hs_ref, zero_buf, stage, sem):
    j = pl.program_id(0)
    slot = j % 2
    block_rows = MOVE_BLOCK * TOKEN_ROWS

    @pl.when(j == 0)
    def _():
        zero_buf[...] = jnp.zeros_like(zero_buf)

        def zero_copy(k):
            start = pl.multiple_of(zero_ref[k], EXPERT_TILE * TOKEN_ROWS)
            return pltpu.make_async_copy(
                zero_buf, hs_ref.at[pl.ds(start, EXPERT_TILE * TOKEN_ROWS)], sem.at[0])

        def start_zero(k2, c):
            for parity in range(2):
                @pl.when(zero_ref[2 * k2 + parity] >= 0)
                def _():
                    zero_copy(2 * k2 + parity).start(priority=parity)
            return c

        def wait_zero(k, c):
            @pl.when(zero_ref[k] >= 0)
            def _():
                zero_copy(k).wait()
            return c

        lax.fori_loop(0, zero_ref.shape[0] // 2, start_zero, 0)
        stage[...] = jnp.zeros_like(stage)
        lax.fori_loop(0, zero_ref.shape[0], wait_zero, 0)

    def place(r, c):
        tok = h2_ref[pl.ds(pl.multiple_of(r * TOKEN_ROWS, TOKEN_ROWS), TOKEN_ROWS), :]
        for s in range(2):
            row = pl.multiple_of(q_ref[0, s, r], TOKEN_ROWS)
            stage[slot, pl.ds(row, TOKEN_ROWS), :] = tok
        return c

    lax.fori_loop(0, ROW_TILE, place, 0, unroll=8)

    def block_copy(buf, stage_row, sorted_row):
        return pltpu.make_async_copy(stage.at[buf, pl.ds(stage_row, block_rows)],
                                     hs_ref.at[pl.ds(sorted_row, block_rows)], sem.at[buf])

    @pl.when(j > 0)
    def _():
        _wait_blocks(prev_tab_ref, block_copy(1 - slot, 0, 0))

    _for_blocks(tab_ref, lambda a, b, parity: block_copy(slot, a, b).start(priority=parity))

    @pl.when(j == pl.num_programs(0) - 1)
    def _():
        _wait_blocks(tab_ref, block_copy(slot, 0, 0))


def _dispatch(zero_tiles, block_tab, slots, h2, n_rows):
    t = h2.shape[0] // TOKEN_ROWS
    nt = t // ROW_TILE
    assert zero_tiles.shape[0] % 2 == 0
    smem_tile = lambda shape: pl.BlockSpec((1,) + shape, lambda j, *_: (j, 0, 0),
                                           memory_space=pltpu.SMEM)
    grid_spec = pltpu.PrefetchScalarGridSpec(
        num_scalar_prefetch=1,
        grid=(nt,),
        in_specs=[smem_tile((1, LANES)),
                  pl.BlockSpec((1, 1, LANES), lambda j, *_: (jnp.maximum(j - 1, 0), 0, 0),
                               memory_space=pltpu.SMEM),
                  smem_tile((2, ROW_TILE)),
                  pl.BlockSpec((ROW_TILE * TOKEN_ROWS, LANES), lambda j, *_: (j, 0))],
        out_specs=pl.BlockSpec(memory_space=pl.ANY),
        scratch_shapes=[pltpu.VMEM((EXPERT_TILE * TOKEN_ROWS, LANES), F32),
                        pltpu.VMEM((2, STAGE_TOKENS * TOKEN_ROWS, LANES), F32),
                        pltpu.SemaphoreType.DMA((2,))])
    return pl.pallas_call(
        _dispatch_kernel,
        grid_spec=grid_spec,
        out_shape=jax.ShapeDtypeStruct((n_rows * TOKEN_ROWS, LANES), F32),
        compiler_params=_params(1),
        name="dispatch",
    )(zero_tiles, block_tab, block_tab, slots, h2)


def _expert_kernel(te_ref, src_ref, nv_ref, run_ref, nxt_ref, hs_ref, w1_hbm, w3_hbm, w2_hbm,
                   ys_ref, w1f, w3f, w2f, w1b, w3b, w2b, sem, *, layer):
    i = pl.program_id(0)

    def weight_copies(e, buf):
        return [pltpu.make_async_copy(src.at[layer, e], dst.at[buf], sem.at[buf])
                for src, dst in ((w1_hbm, w1f), (w3_hbm, w3f), (w2_hbm, w2f))]

    @pl.when(i == 0)
    def _():
        for c in weight_copies(te_ref[0], 0):
            c.start()

    first = (i == 0) | (run_ref[i] != run_ref[jnp.maximum(i - 1, 0)])

    @pl.when(first)
    def _():
        buf = run_ref[i] % 2
        for c in weight_copies(te_ref[i], buf):
            c.wait()

        @pl.when(nxt_ref[i] >= 0)
        def _():
            for c in weight_copies(nxt_ref[i], 1 - buf):
                c.start()

        w1b[...] = w1f[buf].astype(BF16)
        w3b[...] = w3f[buf].astype(BF16)
        w2b[...] = w2f[buf].astype(BF16)

    @pl.when(nv_ref[i] > 0)
    def _():
        h = _from_token_major(hs_ref, EXPERT_TILE).astype(BF16)
        g = _silu(_dot(h, w1b[...])) * _dot(h, w3b[...])
        _to_token_major(ys_ref, _dot(g.astype(BF16), w2b[...]))

    @pl.when(nv_ref[i] == 0)
    def _():
        ys_ref[...] = jnp.zeros_like(ys_ref)


def _experts(tile_expert, tile_src, tile_rows, hs, w1, w3, w2, layer):
    n_rows, d = hs.shape[0] // TOKEN_ROWS, D_MODEL
    nt = n_rows // EXPERT_TILE
    hid = w1.shape[-1]
    tok_tile = (EXPERT_TILE * TOKEN_ROWS, LANES)
    changed = jnp.concatenate([jnp.zeros((1,), I32),
                               (tile_expert[1:] != tile_expert[:-1]).astype(I32)])
    run = jnp.cumsum(changed).astype(I32)
    later = jnp.where(run[None, :] > run[:, None], tile_expert[None, :], N_EXPERTS)
    next_expert = jnp.min(later, axis=1)
    next_expert = jnp.where(next_expert < N_EXPERTS, next_expert, -1).astype(I32)
    grid_spec = pltpu.PrefetchScalarGridSpec(
        num_scalar_prefetch=5,
        grid=(nt,),
        in_specs=[pl.BlockSpec(tok_tile, lambda i, te, src, nv, run, nxt: (src[i], 0)),
                  pl.BlockSpec(memory_space=pl.ANY), pl.BlockSpec(memory_space=pl.ANY),
                  pl.BlockSpec(memory_space=pl.ANY)],
        out_specs=pl.BlockSpec(tok_tile, lambda i, te, src, nv, run, nxt: (i, 0)),
        scratch_shapes=[pltpu.VMEM((2, d, hid), F32), pltpu.VMEM((2, d, hid), F32),
                        pltpu.VMEM((2, hid, d), F32),
                        pltpu.VMEM((d, hid), BF16), pltpu.VMEM((d, hid), BF16),
                        pltpu.VMEM((hid, d), BF16), pltpu.SemaphoreType.DMA((2,))])
    return pl.pallas_call(
        functools.partial(_expert_kernel, layer=layer),
        grid_spec=grid_spec,
        out_shape=jax.ShapeDtypeStruct(hs.shape, F32),
        compiler_params=_params(1),
        name=f"experts{layer}",
    )(tile_expert, tile_src, tile_rows, run, next_expert, hs, w1, w3, w2)


def _combine_kernel(tab_ref, next_tab_ref, q_ref, w_ref, x1_ref, mod_ref, fw_ref, ys_ref, out_ref,
                    stage, y_tok, sem, *, final_norm):
    i = pl.program_id(0)
    slot = i % 2
    block_rows = MOVE_BLOCK * TOKEN_ROWS

    def block_copy(buf, stage_row, sorted_row):
        return pltpu.make_async_copy(ys_ref.at[pl.ds(sorted_row, block_rows)],
                                     stage.at[buf, pl.ds(stage_row, block_rows)], sem.at[buf])

    def fetch(tab, buf):
        _for_blocks(tab, lambda a, b, parity: block_copy(buf, a, b).start(priority=parity))

    @pl.when(i == 0)
    def _():
        fetch(tab_ref, slot)

    @pl.when(i + 1 < pl.num_programs(0))
    def _():
        fetch(next_tab_ref, 1 - slot)

    _wait_blocks(tab_ref, block_copy(slot, 0, 0))

    def pick(r, c):
        rows = [stage[slot, pl.ds(pl.multiple_of(q_ref[0, s, r], TOKEN_ROWS), TOKEN_ROWS), :]
                for s in range(2)]
        y_tok[pl.ds(pl.multiple_of(r * TOKEN_ROWS, TOKEN_ROWS), TOKEN_ROWS), :] = (
            w_ref[0, 0, r] * rows[0] + w_ref[0, 1, r] * rows[1])
        return c

    lax.fori_loop(0, ROW_TILE, pick, 0, unroll=8)
    x2 = x1_ref[...] + mod_ref[0, 5:6, :] * _from_token_major(y_tok, ROW_TILE)
    if final_norm:
        x2 = _rms(x2) * fw_ref[...]
    out_ref[...] = x2


def _combine(block_tab, slots, weights, x1, mods, layer, final_w, ys, tile0, n_tiles,
             n_prompt_tiles, tiles_per_sample, final_norm):
    d = D_MODEL
    tile = lambda i: (tile0 + i, 0)
    mod_map = lambda i: (_mod_row(tile0 + i, layer, n_prompt_tiles, tiles_per_sample), 0, 0)
    smem_tile = lambda shape: pl.BlockSpec((1,) + shape, lambda i: (tile0 + i, 0, 0),
                                           memory_space=pltpu.SMEM)
    return pl.pallas_call(
        functools.partial(_combine_kernel, final_norm=final_norm),
        grid=(n_tiles,),
        in_specs=[smem_tile((1, LANES)),
                  pl.BlockSpec((1, 1, LANES),
                               lambda i: (tile0 + jnp.minimum(i + 1, n_tiles - 1), 0, 0),
                               memory_space=pltpu.SMEM),
                  smem_tile((2, ROW_TILE)), smem_tile((2, ROW_TILE)),
                  pl.BlockSpec((ROW_TILE, d), tile),
                  pl.BlockSpec((1, 6, d), mod_map),
                  pl.BlockSpec((1, d), lambda i: (0, 0)),
                  pl.BlockSpec(memory_space=pl.ANY)],
        out_specs=pl.BlockSpec((ROW_TILE, d), lambda i: (i, 0)),
        out_shape=jax.ShapeDtypeStruct((n_tiles * ROW_TILE, d), F32),
        scratch_shapes=[pltpu.VMEM((2, STAGE_TOKENS * TOKEN_ROWS, LANES), F32),
                        pltpu.VMEM((ROW_TILE * TOKEN_ROWS, LANES), F32),
                        pltpu.SemaphoreType.DMA((2,))],
        compiler_params=_params(1),
        name=f"combine{layer}_{tile0}",
    )(block_tab, block_tab, slots, weights, x1, mods, final_w, ys)


def _moe(h2, slots, weights, tile_tab, w1, w3, w2, layer):
    t = h2.shape[0] // TOKEN_ROWS
    n_tiles = t // ROW_TILE
    extra_tiles = N_EXPERTS + _cdiv(N_EXPERTS * MOVE_BLOCK, EXPERT_TILE)
    n_rows = 2 * t + extra_tiles * EXPERT_TILE
    nt = n_rows // EXPERT_TILE
    tab = tile_tab[:, N_GROUPS:N_GROUPS + N_EXPERTS, 0:3].transpose(0, 2, 1).astype(I32)
    cnt = tab[-1, 0] + tab[-1, 1]
    tight = _cdiv(cnt, EXPERT_TILE) * EXPERT_TILE
    padded = jnp.where(cnt > 0, _cdiv(cnt + MOVE_BLOCK - 1, EXPERT_TILE) * EXPERT_TILE, 0)
    ends = jnp.cumsum(padded)
    offsets = ends - padded
    tails = jnp.where(cnt > 0, ends - EXPERT_TILE, -1)
    tails2 = jnp.where(padded > tight, ends - 2 * EXPERT_TILE, -1)
    used = ends[-1] // EXPERT_TILE
    tile_start = jnp.arange(nt, dtype=I32) * EXPERT_TILE
    unused = (used + jnp.arange(extra_tiles, dtype=I32)) * EXPERT_TILE
    zero_tiles = jnp.concatenate([tails, tails2, jnp.where(unused < n_rows, unused, -1),
                                  jnp.full((extra_tiles % 2,), -1, I32)])
    zero_tiles = jnp.where(zero_tiles >= 0, zero_tiles * TOKEN_ROWS, -1).astype(I32)
    tile_src = jnp.minimum(jnp.arange(nt, dtype=I32), used - 1)
    tile_expert = jnp.sum((tile_src * EXPERT_TILE)[:, None] >= ends[None, :], axis=1).astype(I32)
    tile_rows = jnp.where(tile_start < ends[-1],
                          jnp.clip(cnt[tile_expert] - (tile_start - offsets[tile_expert]),
                                   0, EXPERT_TILE), 0).astype(I32)
    n_blocks = _cdiv(tab[:, 0], MOVE_BLOCK)
    blocks_through = jnp.cumsum(n_blocks, axis=1)
    k = jnp.arange(MAX_BLOCKS, dtype=I32)
    owner = jnp.sum(blocks_through[:, None, :] <= k[None, :, None], axis=2)
    is_owner = owner[:, :, None] == jnp.arange(N_EXPERTS, dtype=I32)[None, None, :]
    pick = lambda v: jnp.sum(jnp.where(is_owner, v[:, None, :], 0), axis=2)
    run_first = pick(offsets[None, :] + tab[:, 1])
    block_in_run = k[None, :] - pick(blocks_through - n_blocks)
    sorted_row = (run_first + block_in_run * MOVE_BLOCK) * TOKEN_ROWS
    block_tab = jnp.concatenate(
        [sorted_row, blocks_through[:, -1:],
         jnp.zeros((n_tiles, LANES - MAX_BLOCKS - 1), I32)], axis=1).astype(I32)[:, None, :]
    hs = _dispatch(zero_tiles, block_tab, slots, h2, n_rows)
    ys = _experts(tile_expert, tile_src, tile_rows, hs, w1, w3, w2, layer)
    return ys, (block_tab, slots, weights)


def _rope(x, cos, sin_signed):
    lane = lax.broadcasted_iota(I32, (x.shape[0], LANES), 1)
    low = (lane % 32) < 16
    outs = []
    for j in range(x.shape[1] // LANES):
        xb = x[:, j * LANES:(j + 1) * LANES]
        partner = jnp.where(low, pltpu.roll(xb, LANES - 16, 1), pltpu.roll(xb, 16, 1))
        outs.append(xb * cos + partner * sin_signed)
    return jnp.concatenate(outs, axis=1)


def _inproj1_prompt_kernel(x_ref, mod_ref, nw_ref, w_ref, q_ref, k_ref, v_ref, kc_ref, vc_ref):
    d = D_MODEL
    h = _modulate(x_ref[...], nw_ref[...], mod_ref[0, 0:1, :], mod_ref[0, 1:2, :]).astype(BF16)
    q_ref[...] = (_dot(h, w_ref[:, 0:d]) * (DIFF_HD ** -0.5)).astype(BF16)
    k = _dot(h, w_ref[:, d:2 * d])
    v = _dot(h, w_ref[:, 2 * d:3 * d])
    k_ref[...] = k.astype(BF16)
    v_ref[...] = v.astype(BF16)
    for b in range(k.shape[0] // ATTN_TILE):
        kc_ref[b * d:(b + 1) * d, :] = k[b * ATTN_TILE:(b + 1) * ATTN_TILE].T
    _to_token_major(vc_ref, v)


def _inproj1_sample_kernel(x_ref, mod_ref, nw_ref, w_ref, cos_ref, sin_ref, q_ref, k_ref, v_ref):
    d = D_MODEL
    h = _modulate(x_ref[...], nw_ref[...], mod_ref[0, 0:1, :], mod_ref[0, 1:2, :]).astype(BF16)
    cos, sin = cos_ref[...], sin_ref[...]
    q_ref[...] = (_rope(_dot(h, w_ref[:, 0:d]), cos, sin) * (DIFF_HD ** -0.5)).astype(BF16)
    k_ref[...] = _rope(_dot(h, w_ref[:, d:2 * d]), cos, sin).astype(BF16)
    v_ref[...] = _dot(h, w_ref[:, 2 * d:3 * d]).astype(BF16)


def _inproj1(x, mods, norm_w, w_bf16, n_prompt_tiles, n_sample_tiles, tiles_per_sample,
             cos_t, sin_t):
    d = D_MODEL
    rows = INPROJ1_TILE
    npt, nst = n_prompt_tiles, n_sample_tiles
    common = [pl.BlockSpec((1, d), lambda i: (0, 0)), pl.BlockSpec((d, 3 * d), lambda i: (0, 0))]
    tile = lambda i: (i, 0)
    out_specs = tuple(pl.BlockSpec((rows, d), tile) for _ in range(3))
    qp, kp, vp, k_cache, v_cache = pl.pallas_call(
        _inproj1_prompt_kernel,
        grid=(npt,),
        in_specs=[pl.BlockSpec((rows, d), tile),
                  pl.BlockSpec((1, 6, d), lambda i: (8, 0, 0))] + common,
        out_specs=out_specs + (pl.BlockSpec((rows // ATTN_TILE * d, ATTN_TILE), tile),
                               pl.BlockSpec((rows * TOKEN_ROWS, LANES), tile)),
        out_shape=tuple(jax.ShapeDtypeStruct((npt * rows, d), BF16) for _ in range(3))
        + (jax.ShapeDtypeStruct((npt * rows // ATTN_TILE * d, ATTN_TILE), F32),
           jax.ShapeDtypeStruct((npt * rows * TOKEN_ROWS, LANES), F32)),
        compiler_params=_params(1),
        name="inproj1_prompt",
    )(x, mods, norm_w, w_bf16)
    rope_tile = lambda i: (i % tiles_per_sample, 0)
    qs, ks, vs = pl.pallas_call(
        _inproj1_sample_kernel,
        grid=(nst,),
        in_specs=[pl.BlockSpec((rows, d), lambda i: (npt + i, 0)),
                  pl.BlockSpec((1, 6, d), lambda i: (8 + 1 + i // tiles_per_sample, 0, 0))]
        + common + [pl.BlockSpec((rows, LANES), rope_tile),
                    pl.BlockSpec((rows, LANES), rope_tile)],
        out_specs=out_specs,
        out_shape=tuple(jax.ShapeDtypeStruct((nst * rows, d), BF16) for _ in range(3)),
        compiler_params=_params(1),
        name="inproj1_sample",
    )(x, mods, norm_w, w_bf16, cos_t, sin_t)
    return (qp, kp, vp), (qs, ks, vs), (k_cache, v_cache)


def _rope_tables(n_tok):
    half = DIFF_HD // 4
    pos = np.arange(n_tok)
    lane = np.arange(LANES)
    sub = lane % DIFF_HD
    p = np.where(sub[None, :] < DIFF_HD // 2, (pos // GRID_W)[:, None], (pos % GRID_W)[:, None])
    inv = jnp.asarray(ROPE_THETA, F32) ** (-jnp.asarray(sub % half, F32) / half)
    ang = jnp.asarray(p, F32) * inv[None, :]
    sign = np.where((lane % (2 * half)) < half, -1.0, 1.0).astype(np.float32)
    return jnp.cos(ang), jnp.sin(ang) * sign[None, :]


def _diffattn_kernel(*refs, has_cache, lam_init, seqs):
    if has_cache:
        q_ref, k_ref, v_ref, ck_ref, cv_ref, lam_ref, sw_ref, o_ref = refs
    else:
        q_ref, k_ref, v_ref, lam_ref, sw_ref, o_ref = refs
    hd2 = 2 * DIFF_HD
    lv = lam_ref[...]
    lam = (jnp.exp(jnp.sum(lv[0:1] * lv[1:2], axis=1, keepdims=True))
           - jnp.exp(jnp.sum(lv[2:3] * lv[3:4], axis=1, keepdims=True)) + lam_init)
    nq, nk = q_ref.shape[0] // seqs, k_ref.shape[0] // seqs
    lane = lax.broadcasted_iota(I32, (nq, hd2), 1)
    for sq, h in ((sq, h) for sq in range(seqs) for h in range(DIFF_HEADS)):
        cols = slice(h * hd2, (h + 1) * hd2)
        q_rows, k_rows = slice(sq * nq, (sq + 1) * nq), slice(sq * nk, (sq + 1) * nk)
        q = q_ref[q_rows, cols]
        zero = jnp.zeros_like(q)
        k_new = k_ref[k_rows, cols].astype(BF16)
        values = [v_ref[k_rows, cols].astype(BF16)]
        if has_cache:
            past = ck_ref.shape[1]
            k_past_t = ck_ref[cols, :].astype(BF16)
            values.append(cv_ref[pl.ds(h, past, stride=DIFF_HEADS), :].astype(BF16))
        o = None
        for c in range(2):
            qc = jnp.where((lane < DIFF_HD) == (c == 0), q, zero)
            s = [_dot_nt(qc, k_new)]
            if has_cache:
                s.append(_dot(qc, k_past_t))
            mx = functools.reduce(jnp.maximum, [jnp.max(si, axis=1, keepdims=True) for si in s])
            e = [jnp.exp(si - mx) for si in s]
            z = functools.reduce(jnp.add, [jnp.sum(ei, axis=1, keepdims=True) for ei in e])
            pv = functools.reduce(jnp.add, [_dot(ei.astype(BF16), v) for ei, v in zip(e, values)])
            pv = pv * (1.0 / z)
            o = pv if c == 0 else o - lam * pv
        o_ref[q_rows, cols] = ((_rms(o) * sw_ref[...]) * (1.0 - lam_init)).astype(BF16)


def _diffattn(q, k, v, lam_vecs, subln_w, batch, seq_len, q_block, lam_init, cache=None, seqs=1):
    d = D_MODEL
    nq = seq_len // q_block
    has_cache = cache is not None
    assert seqs == 1 or (nq == 1 and not has_cache and batch % seqs == 0)
    kv_spec = pl.BlockSpec((seqs * seq_len, d), lambda b, qi: (b, 0))
    in_specs = [pl.BlockSpec((seqs * q_block, d), lambda b, qi: (b * nq + qi, 0)), kv_spec, kv_spec]
    args = [q, k, v]
    if has_cache:
        past = cache[0].shape[1]
        in_specs += [pl.BlockSpec((d, past), lambda b, qi: (b, 0)),
                     pl.BlockSpec((past * DIFF_HEADS, 2 * DIFF_HD), lambda b, qi: (b, 0))]
        args += list(cache)
    in_specs += [pl.BlockSpec((4, DIFF_HD), lambda b, qi: (0, 0)),
                 pl.BlockSpec((1, 2 * DIFF_HD), lambda b, qi: (0, 0))]
    args += [lam_vecs, subln_w]
    return pl.pallas_call(
        functools.partial(_diffattn_kernel, has_cache=has_cache, lam_init=lam_init, seqs=seqs),
        grid=(batch // seqs, nq),
        in_specs=in_specs,
        out_specs=pl.BlockSpec((seqs * q_block, d), lambda b, qi: (b * nq + qi, 0)),
        out_shape=jax.ShapeDtypeStruct((batch * seq_len, d), BF16),
        compiler_params=_params(2),
        name="diffattn_cache" if has_cache else "diffattn",
    )(*args)


def _router_weights(router_group, router_expert):
    w = jnp.concatenate([router_group, router_expert], axis=1)
    return jnp.pad(w, ((0, 0), (0, LANES - w.shape[1]))).astype(BF16)


def _inproj0_weights(w_in):
    gq, gk, gv, gg, gaf, gab, hq, hff, hfb, hi, hg = jnp.split(
        w_in, [256, 512, 1024, 1536, 1552, 1568, 1824, 2080, 2336, 2848], axis=1)
    w = jnp.concatenate([gq, gk, gv, gg, hq, hff, hfb, hi, hg, gaf, gab], axis=1)
    return jnp.pad(w, ((0, 0), (0, AB_COLS - w.shape[1]))).astype(BF16)


def kernel(x_prompt, x_sample, state_gla, state_hgrn, cache_diff_k, cache_diff_v, c, c_ctx,
           w_ada, b_ada, norm1_w, norm2_w, w_in_ab, gla_a2, gla_a_bias, hgrn_lb, gla_onorm_w,
           hgrn_onorm_w, w_out_ab, w_in_c, lam_q1, lam_k1, lam_q2, lam_k2, diff_subln_w, w_out_c,
           router_group, router_expert, moe_w1, moe_w3, moe_w2, final_norm_w):
    bp, lp, d = x_prompt.shape
    bs, ls, _ = x_sample.shape
    depth = w_ada.shape[0]
    assert depth == 2 and d == D_MODEL and bs <= 7
    tp, ts = bp * lp, bs * ls
    npt, nst = tp // ROW_TILE, ts // ROW_TILE
    tps = ls // ROW_TILE
    xp = x_prompt.reshape(tp, d)
    xs = x_sample.reshape(ts, d)

    cond8 = jnp.concatenate([c_ctx[None, :], c, jnp.zeros((7 - bs, d), F32)], axis=0)
    mods = _adaln(cond8, w_ada, b_ada).reshape(depth * 8, 6, d)

    proj = _inproj0(xp, xs, mods, norm1_w[0:1], _inproj0_weights(w_in_ab[0]), ls)
    a_bias = gla_a_bias[0][:, None, :]
    scan_args = (gla_a2[0], a_bias, hgrn_lb, gla_onorm_w[0:1], hgrn_onorm_w[0:1])
    mixed_p, new_state_gla, new_state_hgrn = _scan(proj, 0, bp, lp, *scan_args)
    s0 = jnp.concatenate([state_gla[:, 0], state_hgrn[:, 0]], axis=2).swapaxes(-1, -2)
    s0 = s0.reshape(bs, 2, SCAN_PAIRS, 2, HEAD_DV, HEAD_DK)
    zero = jnp.zeros_like(s0[:, :, :, 0])
    s0 = jnp.concatenate([jnp.concatenate([s0[:, :, :, 0], zero], axis=-1),
                          jnp.concatenate([zero, s0[:, :, :, 1]], axis=-1)], axis=-2)
    mixed_s = _scan(proj, tp, bs, ls, *scan_args, s0=s0)

    wr = _router_weights(router_group[0], router_expert[0])
    x1, *routed = _post((xp, xs), mixed_p, mixed_s, mods, 0, norm2_w[0:1],
                        w_out_ab[0].astype(BF16), wr, tps)
    ys, tables = _moe(*routed, moe_w1, moe_w3, moe_w2, 0)
    x2 = _combine(*tables, x1, mods, 0, final_norm_w[None, :], ys, 0, npt + nst, npt, tps, False)

    lam_init = 0.8 - 0.6 * math.exp(-0.3 * 1)
    cos_t, sin_t = _rope_tables(ls)
    (qp, kp, vp), (qs, ks, vs), (k_cache, v_cache) = _inproj1(
        x2, mods, norm1_w[1:2], w_in_c[0].astype(BF16), tp // INPROJ1_TILE, ts // INPROJ1_TILE,
        ls // INPROJ1_TILE, cos_t, sin_t)
    lam_vecs = jnp.stack([lam_q1[0], lam_k1[0], lam_q2[0], lam_k2[0]])
    att_p = _diffattn(qp, kp, vp, lam_vecs, diff_subln_w[0:1], bp, lp, lp, lam_init,
                      seqs=PROMPT_SEQS_PER_STEP)
    past = cache_diff_k.shape[2]
    assert lp == ATTN_TILE and DIFF_HEADS == TOKEN_ROWS
    cache = (cache_diff_k[:, 0].transpose(0, 2, 3, 4, 1).reshape(bs * d, past),
             cache_diff_v[:, 0].reshape(bs * past * DIFF_HEADS, 2 * DIFF_HD))
    att_s = _diffattn(qs, ks, vs, lam_vecs, diff_subln_w[0:1], bs, ls, SAMPLE_Q_BLOCK, lam_init,
                      cache)

    wr = _router_weights(router_group[1], router_expert[1])
    x3, *routed = _post((x2,), att_p, att_s, mods, 1, norm2_w[1:2],
                        w_out_c[0].astype(BF16), wr, tps)
    ys, tables = _moe(*routed, moe_w1, moe_w3, moe_w2, 1)
    fw = final_norm_w[None, :]
    y_p = _combine(*tables, x3, mods, 1, fw, ys, 0, npt, npt, tps, True)
    y_s = _combine(*tables, x3, mods, 1, fw, ys, npt, nst, npt, tps, True)

    return (y_p.reshape(bp, lp, d), y_s.reshape(bs, ls, d), new_state_gla, new_state_hgrn,
            k_cache.reshape(bp, 1, DIFF_HEADS, 2, DIFF_HD, lp).transpose(0, 1, 5, 2, 3, 4),
            v_cache.reshape(bp, 1, lp, DIFF_HEADS, 2 * DIFF_HD))
```
